```python
import jax, jax.numpy as jnp
from jax import lax
import numpy as np

D_MODEL = 1024
BATCH = 8
SEQ = 4096
DEPTH = 1

N_META = 16
D_CONV = D_MODEL
CONV_WIDTH = 31
HG_HEADS = 8
HG_DK = 128
HG_DV = D_MODEL // HG_HEADS
D_HK = HG_HEADS * HG_DK
D_HV = HG_HEADS * HG_DV
CHUNK = 64
EPS = 1e-6
SPLIT_SIZES = (D_CONV, D_CONV, D_CONV, D_HK, D_HK, D_HV, D_HV, D_MODEL, D_MODEL)
D_IN = D_CONV * 3 + D_HK * 2 + D_HV * 2 + D_MODEL * 2

kernel_name = "hybrid_conformer_hgrn2_gated_block"


def rmsnorm(x, g):
    xf = x.astype(jnp.float32)
    y = xf * lax.rsqrt(jnp.mean(xf * xf, axis=-1, keepdims=True) + EPS)
    return (y * g.astype(jnp.float32)).astype(x.dtype)


def layernorm(x, g, b):
    xf = x.astype(jnp.float32)
    mu = jnp.mean(xf, axis=-1, keepdims=True)
    var = jnp.mean(jnp.square(xf - mu), axis=-1, keepdims=True)
    y = (xf - mu) * lax.rsqrt(var + EPS)
    return (y * g.astype(jnp.float32) + b.astype(jnp.float32)).astype(x.dtype)


def conformer_branch(u_a, u_b, z, conv_w, conv_b, ln_g, ln_b, w_out):
    a = u_a * jax.nn.sigmoid(u_b)
    c = lax.conv_general_dilated(
        a, conv_w[:, None, :].astype(a.dtype), window_strides=(1,),
        padding=[(CONV_WIDTH - 1, 0)],
        dimension_numbers=('NWC', 'WIO', 'NWC'),
        feature_group_count=D_CONV) + conv_b
    c = jax.nn.silu(layernorm(c, ln_g, ln_b))
    return (c * jax.nn.silu(z)) @ w_out


def hgrn2_branch(q_raw, f_raw, i_raw, g, lb, gnorm_g, w_out):
    bsz, seqlen, _ = q_raw.shape
    out_dtype = i_raw.dtype
    q = jax.nn.silu(q_raw.astype(jnp.float32))
    f = lb + (1.0 - lb) * jax.nn.sigmoid(f_raw.astype(jnp.float32))
    log_f = jnp.log(f)
    k = 1.0 - f
    v = i_raw.astype(jnp.float32)
    pad = (-seqlen) % CHUNK
    n_chunks = (seqlen + pad) // CHUNK

    def to_chunks(t, d):
        t = jnp.pad(t, ((0, 0), (pad, 0), (0, 0)))
        t = t.reshape(bsz, n_chunks, CHUNK, HG_HEADS, d)
        return jnp.transpose(t, (1, 0, 3, 2, 4))

    qc = to_chunks(q, HG_DK)
    kc = to_chunks(k, HG_DK)
    vc = to_chunks(v, HG_DV)
    bc = jnp.cumsum(to_chunks(log_f, HG_DK), axis=3)
    causal = jnp.tril(jnp.ones((CHUNK, CHUNK), dtype=bool))[None, None, :, :, None]

    def step(S, inp):
        qi, ki, vi, bi = inp
        o_inter = jnp.einsum('bhtk,bhkv->bhtv', qi * jnp.exp(bi), S)
        diff = bi[:, :, :, None, :] - bi[:, :, None, :, :]
        decay = jnp.exp(jnp.where(causal, diff, -jnp.inf))
        attn = jnp.einsum('bhtk,bhsk,bhtsk->bhts', qi, ki, decay)
        o_intra = jnp.einsum('bhts,bhsv->bhtv', attn, vi)
        b_last = bi[:, :, -1:, :]
        S_new = jnp.exp(b_last[:, :, 0, :])[..., None] * S + jnp.einsum(
            'bhsk,bhsv->bhkv', ki * jnp.exp(b_last - bi), vi)
        return S_new, o_inter + o_intra

    S0 = jnp.zeros((bsz, HG_HEADS, HG_DK, HG_DV), jnp.float32)
    _, o = lax.scan(step, S0, (qc, kc, vc, bc))
    o = jnp.transpose(o, (1, 0, 3, 2, 4)).reshape(bsz, n_chunks * CHUNK, HG_HEADS, HG_DV)
    o = o[:, pad:]
    o = o * lax.rsqrt(jnp.mean(o * o, axis=-1, keepdims=True) + EPS)
    o = o * gnorm_g.astype(jnp.float32).reshape(HG_HEADS, HG_DV)
    o = o.reshape(bsz, seqlen, D_HV).astype(out_dtype)
    return (o * jax.nn.silu(g)) @ w_out


def _fwd_setup_inputs(seed: int = 0) -> dict:
    key = jax.random.key(seed)
    ks = jax.random.split(key, 16)
    f32 = jnp.float32
    nrm = lambda k, shape, s: jax.random.normal(k, shape, f32) * s
    return {
        "x": nrm(ks[0], (BATCH, SEQ, D_MODEL), 1.0),
        "meta_tokens": nrm(ks[1], (N_META, D_MODEL), 1.0),
        "norm_g": 1.0 + nrm(ks[2], (DEPTH, D_MODEL), 0.02),
        "w_in": nrm(ks[3], (DEPTH, D_MODEL, D_IN), D_MODEL ** -0.5),
        "conv_w": nrm(ks[4], (DEPTH, CONV_WIDTH, D_CONV), CONV_WIDTH ** -0.5),
        "conv_b": nrm(ks[5], (DEPTH, D_CONV), 0.02),
        "ln_g": 1.0 + nrm(ks[6], (DEPTH, D_CONV), 0.02),
        "ln_b": nrm(ks[7], (DEPTH, D_CONV), 0.02),
        "w_conv_out": nrm(ks[8], (DEPTH, D_CONV, D_MODEL), D_CONV ** -0.5),
        "lb_logits": nrm(ks[9], (DEPTH + 1, D_HK), 0.5),
        "gnorm_g": 1.0 + nrm(ks[10], (DEPTH, D_HV), 0.02),
        "w_rec_out": nrm(ks[11], (DEPTH, D_HV, D_MODEL), D_HV ** -0.5),
        "w_out": nrm(ks[12], (DEPTH, D_MODEL, D_MODEL), D_MODEL ** -0.5),
        "final_g": 1.0 + nrm(ks[13], (D_MODEL,), 0.02),
    }


def _fwd_reference(x, meta_tokens, norm_g, w_in, conv_w, conv_b, ln_g, ln_b, w_conv_out,
              lb_logits, gnorm_g, w_rec_out, w_out, final_g):
    bsz = x.shape[0]
    meta = jnp.broadcast_to(meta_tokens.astype(x.dtype)[None], (bsz, N_META, D_MODEL))
    h_res = jnp.concatenate([meta, x], axis=1)
    lb_all = jnp.cumsum(jax.nn.softmax(lb_logits.astype(jnp.float32), axis=0), axis=0)
    split_idx = [int(v) for v in np.cumsum(SPLIT_SIZES)[:-1]]
    for l in range(DEPTH):
        h = rmsnorm(h_res, norm_g[l])
        proj = h @ w_in[l]
        glu_a, glu_b, z_conv, q, f, i, g_rec, m_conv, m_rec = jnp.split(proj, split_idx, axis=-1)
        y_conv = conformer_branch(glu_a, glu_b, z_conv, conv_w[l], conv_b[l],
                                  ln_g[l], ln_b[l], w_conv_out[l])
        y_rec = hgrn2_branch(q, f, i, g_rec, lb_all[l], gnorm_g[l], w_rec_out[l])
        merged = jax.nn.sigmoid(m_conv) * y_conv + jax.nn.sigmoid(m_rec) * y_rec
        h_res = h_res + merged @ w_out[l]
    return rmsnorm(h_res[:, N_META:], final_g)


import jax as _jax
import jax.numpy as _jnp

TWIN_FORMAT = 'train_step'
FWD_PARAMS = ['x', 'meta_tokens', 'norm_g', 'w_in', 'conv_w', 'conv_b', 'ln_g', 'ln_b', 'w_conv_out', 'lb_logits', 'gnorm_g', 'w_rec_out', 'w_out', 'final_g']
TWIN_WEIGHTS = ['meta_tokens', 'norm_g', 'w_in', 'conv_w', 'conv_b', 'ln_g', 'ln_b', 'w_conv_out', 'lb_logits', 'gnorm_g', 'w_rec_out', 'w_out', 'final_g']
TWIN_DIFF_INPUT = 'x'
TWIN_INPUTS = ['x', 'meta_tokens', 'norm_g', 'w_in', 'conv_w', 'conv_b', 'ln_g', 'ln_b', 'w_conv_out', 'lb_logits', 'gnorm_g', 'w_rec_out', 'w_out', 'final_g', 'loss_target', 'm_meta_tokens', 'm_norm_g', 'm_w_in', 'm_conv_w', 'm_conv_b', 'm_ln_g', 'm_ln_b', 'm_w_conv_out', 'm_lb_logits', 'm_gnorm_g', 'm_w_rec_out', 'm_w_out', 'm_final_g', 'v_meta_tokens', 'v_norm_g', 'v_w_in', 'v_conv_w', 'v_conv_b', 'v_ln_g', 'v_ln_b', 'v_w_conv_out', 'v_lb_logits', 'v_gnorm_g', 'v_w_rec_out', 'v_w_out', 'v_final_g']
TWIN_OUTPUTS = ['loss', 'grad_x', 'grad_meta_tokens', 'grad_norm_g', 'grad_w_in', 'grad_conv_w', 'grad_conv_b', 'grad_ln_g', 'grad_ln_b', 'grad_w_conv_out', 'grad_lb_logits', 'grad_gnorm_g', 'grad_w_rec_out', 'grad_w_out', 'grad_final_g', 'delta_meta_tokens', 'delta_norm_g', 'delta_w_in', 'delta_conv_w', 'delta_conv_b', 'delta_ln_g', 'delta_ln_b', 'delta_w_conv_out', 'delta_lb_logits', 'delta_gnorm_g', 'delta_w_rec_out', 'delta_w_out', 'delta_final_g', 'new_m_meta_tokens', 'new_m_norm_g', 'new_m_w_in', 'new_m_conv_w', 'new_m_conv_b', 'new_m_ln_g', 'new_m_ln_b', 'new_m_w_conv_out', 'new_m_lb_logits', 'new_m_gnorm_g', 'new_m_w_rec_out', 'new_m_w_out', 'new_m_final_g', 'new_v_meta_tokens', 'new_v_norm_g', 'new_v_w_in', 'new_v_conv_w', 'new_v_conv_b', 'new_v_ln_g', 'new_v_ln_b', 'new_v_w_conv_out', 'new_v_lb_logits', 'new_v_gnorm_g', 'new_v_w_rec_out', 'new_v_w_out', 'new_v_final_g']
TWIN_LEAF_KINDS = {'loss': 'loss', 'grad_x': 'grad_x', 'grad_meta_tokens': 'grad_w', 'grad_norm_g': 'grad_w', 'grad_w_in': 'grad_w', 'grad_conv_w': 'grad_w', 'grad_conv_b': 'grad_w', 'grad_ln_g': 'grad_w', 'grad_ln_b': 'grad_w', 'grad_w_conv_out': 'grad_w', 'grad_lb_logits': 'grad_w', 'grad_gnorm_g': 'grad_w', 'grad_w_rec_out': 'grad_w', 'grad_w_out': 'grad_w', 'grad_final_g': 'grad_w', 'delta_meta_tokens': 'delta_w', 'delta_norm_g': 'delta_w', 'delta_w_in': 'delta_w', 'delta_conv_w': 'delta_w', 'delta_conv_b': 'delta_w', 'delta_ln_g': 'delta_w', 'delta_ln_b': 'delta_w', 'delta_w_conv_out': 'delta_w', 'delta_lb_logits': 'delta_w', 'delta_gnorm_g': 'delta_w', 'delta_w_rec_out': 'delta_w', 'delta_w_out': 'delta_w', 'delta_final_g': 'delta_w', 'new_m_meta_tokens': 'new_m', 'new_m_norm_g': 'new_m', 'new_m_w_in': 'new_m', 'new_m_conv_w': 'new_m', 'new_m_conv_b': 'new_m', 'new_m_ln_g': 'new_m', 'new_m_ln_b': 'new_m', 'new_m_w_conv_out': 'new_m', 'new_m_lb_logits': 'new_m', 'new_m_gnorm_g': 'new_m', 'new_m_w_rec_out': 'new_m', 'new_m_w_out': 'new_m', 'new_m_final_g': 'new_m', 'new_v_meta_tokens': 'new_v', 'new_v_norm_g': 'new_v', 'new_v_w_in': 'new_v', 'new_v_conv_w': 'new_v', 'new_v_conv_b': 'new_v', 'new_v_ln_g': 'new_v', 'new_v_ln_b': 'new_v', 'new_v_w_conv_out': 'new_v', 'new_v_lb_logits': 'new_v', 'new_v_gnorm_g': 'new_v', 'new_v_w_rec_out': 'new_v', 'new_v_w_out': 'new_v', 'new_v_final_g': 'new_v'}


def _forward(args):
    return _fwd_reference(*[args[k] for k in FWD_PARAMS])


def _output_shape():
    out = _jax.eval_shape(lambda: _forward(_fwd_setup_inputs(0)))
    return out.shape, out.dtype

N_MICROBATCH = 1
ADAM_LR = 0.001
ADAM_B1 = 0.9
ADAM_B2 = 0.999
ADAM_EPS = 1e-08
ADAM_WD = 0.01
ADAM_STEP = 10
PER_EXAMPLE_BATCH_AXIS = {'x': 0, 'loss_target': 0}
SHARED_INPUTS = []
_WEIGHT_DTYPES = {'meta_tokens': _jnp.float32, 'norm_g': _jnp.float32, 'w_in': _jnp.float32, 'conv_w': _jnp.float32, 'conv_b': _jnp.float32, 'ln_g': _jnp.float32, 'ln_b': _jnp.float32, 'w_conv_out': _jnp.float32, 'lb_logits': _jnp.float32, 'gnorm_g': _jnp.float32, 'w_rec_out': _jnp.float32, 'w_out': _jnp.float32, 'final_g': _jnp.float32}
MOMENT_SCALE = {'meta_tokens': 2.314417e-03, 'norm_g': 9.452196e-02, 'w_in': 3.163819e-02, 'conv_w': 3.304344e-02, 'conv_b': 6.178043e-02, 'ln_g': 3.920176e-02, 'ln_b': 3.337497e-02, 'w_conv_out': 3.217200e-02, 'lb_logits': 4.815361e-03, 'gnorm_g': 5.744713e-02, 'w_rec_out': 5.371012e-02, 'w_out': 6.300245e-02, 'final_g': 3.198830e+01}


def _to_microbatches(a, axis):
    t = _jnp.moveaxis(a, axis, 0)
    t = t.reshape((N_MICROBATCH, t.shape[0] // N_MICROBATCH) + t.shape[1:])
    return _jnp.moveaxis(t, 1, axis + 1)


def setup_inputs(seed: int = 0) -> dict:
    inp = _fwd_setup_inputs(seed)
    key = _jax.random.fold_in(_jax.random.key(seed), 7919)
    shape, _ = _output_shape()
    out = dict(inp)
    out["loss_target"] = _jax.random.normal(_jax.random.fold_in(key, 0), shape, _jnp.float32)
    for i, name in enumerate(TWIN_WEIGHTS):
        w = inp[name].astype(_jnp.float32)
        if MOMENT_SCALE is None:
            s = _jnp.sqrt(_jnp.mean(_jnp.square(w)) + 1e-30)
        else:
            s = MOMENT_SCALE[name]
        km, kv = _jax.random.split(_jax.random.fold_in(key, i + 1))
        out[name] = w
        out["m_" + name] = s * _jax.random.normal(km, w.shape, _jnp.float32)
        out["v_" + name] = (s * s) * _jax.random.uniform(kv, w.shape, _jnp.float32, 0.5, 1.5)
    if N_MICROBATCH > 1:
        for name, axis in PER_EXAMPLE_BATCH_AXIS.items():
            out[name] = _to_microbatches(out[name], axis)
    return {'x': out['x'], 'meta_tokens': out['meta_tokens'], 'norm_g': out['norm_g'], 'w_in': out['w_in'], 'conv_w': out['conv_w'], 'conv_b': out['conv_b'], 'ln_g': out['ln_g'], 'ln_b': out['ln_b'], 'w_conv_out': out['w_conv_out'], 'lb_logits': out['lb_logits'], 'gnorm_g': out['gnorm_g'], 'w_rec_out': out['w_rec_out'], 'w_out': out['w_out'], 'final_g': out['final_g'], 'loss_target': out['loss_target'], 'm_meta_tokens': out['m_meta_tokens'], 'm_norm_g': out['m_norm_g'], 'm_w_in': out['m_w_in'], 'm_conv_w': out['m_conv_w'], 'm_conv_b': out['m_conv_b'], 'm_ln_g': out['m_ln_g'], 'm_ln_b': out['m_ln_b'], 'm_w_conv_out': out['m_w_conv_out'], 'm_lb_logits': out['m_lb_logits'], 'm_gnorm_g': out['m_gnorm_g'], 'm_w_rec_out': out['m_w_rec_out'], 'm_w_out': out['m_w_out'], 'm_final_g': out['m_final_g'], 'v_meta_tokens': out['v_meta_tokens'], 'v_norm_g': out['v_norm_g'], 'v_w_in': out['v_w_in'], 'v_conv_w': out['v_conv_w'], 'v_conv_b': out['v_conv_b'], 'v_ln_g': out['v_ln_g'], 'v_ln_b': out['v_ln_b'], 'v_w_conv_out': out['v_w_conv_out'], 'v_lb_logits': out['v_lb_logits'], 'v_gnorm_g': out['v_gnorm_g'], 'v_w_rec_out': out['v_w_rec_out'], 'v_w_out': out['v_w_out'], 'v_final_g': out['v_final_g']}


def _loss(weights, diff, rest, loss_target):
    with _jax.named_scope("forward"):
        args = {**rest, TWIN_DIFF_INPUT: diff, **{k: w.astype(_WEIGHT_DTYPES[k]) for k, w in weights.items()}}
        y = _forward(args)
    with _jax.named_scope("loss_head"):
        err = _jnp.square(y.astype(_jnp.float32) - loss_target)
        return 0.5 * _jnp.sum(_jnp.mean(err, axis=-1)) if err.ndim else 0.5 * err


def _adamw(w, g, m, v):
    m = ADAM_B1 * m + (1.0 - ADAM_B1) * g
    v = ADAM_B2 * v + (1.0 - ADAM_B2) * _jnp.square(g)
    m_hat = m / (1.0 - ADAM_B1 ** ADAM_STEP)
    v_hat = v / (1.0 - ADAM_B2 ** ADAM_STEP)
    delta = -ADAM_LR * (m_hat / (_jnp.sqrt(v_hat) + ADAM_EPS) + ADAM_WD * w)
    return delta, m, v


def reference(x, meta_tokens, norm_g, w_in, conv_w, conv_b, ln_g, ln_b, w_conv_out, lb_logits, gnorm_g, w_rec_out, w_out, final_g, loss_target, m_meta_tokens, m_norm_g, m_w_in, m_conv_w, m_conv_b, m_ln_g, m_ln_b, m_w_conv_out, m_lb_logits, m_gnorm_g, m_w_rec_out, m_w_out, m_final_g, v_meta_tokens, v_norm_g, v_w_in, v_conv_w, v_conv_b, v_ln_g, v_ln_b, v_w_conv_out, v_lb_logits, v_gnorm_g, v_w_rec_out, v_w_out, v_final_g):
    given = dict(x=x, meta_tokens=meta_tokens, norm_g=norm_g, w_in=w_in, conv_w=conv_w, conv_b=conv_b, ln_g=ln_g, ln_b=ln_b, w_conv_out=w_conv_out, lb_logits=lb_logits, gnorm_g=gnorm_g, w_rec_out=w_rec_out, w_out=w_out, final_g=final_g, loss_target=loss_target, m_meta_tokens=m_meta_tokens, m_norm_g=m_norm_g, m_w_in=m_w_in, m_conv_w=m_conv_w, m_conv_b=m_conv_b, m_ln_g=m_ln_g, m_ln_b=m_ln_b, m_w_conv_out=m_w_conv_out, m_lb_logits=m_lb_logits, m_gnorm_g=m_gnorm_g, m_w_rec_out=m_w_rec_out, m_w_out=m_w_out, m_final_g=m_final_g, v_meta_tokens=v_meta_tokens, v_norm_g=v_norm_g, v_w_in=v_w_in, v_conv_w=v_conv_w, v_conv_b=v_conv_b, v_ln_g=v_ln_g, v_ln_b=v_ln_b, v_w_conv_out=v_w_conv_out, v_lb_logits=v_lb_logits, v_gnorm_g=v_gnorm_g, v_w_rec_out=v_w_rec_out, v_w_out=v_w_out, v_final_g=v_final_g)
    weights = {n: given[n] for n in TWIN_WEIGHTS}
    shared = {n: given[n] for n in SHARED_INPUTS}
    per_example = {n: given[n] for n in ['x']}
    grad_fn = _jax.value_and_grad(_loss, argnums=(0, 1))

    def one_microbatch(ex, loss_target):
        ex = dict(ex)
        diff = ex.pop(TWIN_DIFF_INPUT)
        return grad_fn(weights, diff, {**shared, **ex}, loss_target)

    if N_MICROBATCH == 1:
        loss, (grad_w, grad_x) = one_microbatch(per_example, given["loss_target"])
    else:
        def body(carry, xs):
            loss_sum, grad_sum = carry
            l_k, (gw_k, gx_k) = one_microbatch(xs[0], xs[1])
            with _jax.named_scope("update"):
                return (loss_sum + l_k, _jax.tree.map(_jnp.add, grad_sum, gw_k)), gx_k

        init = (_jnp.zeros((), _jnp.float32), _jax.tree.map(_jnp.zeros_like, weights))
        (loss, grad_w), grad_x = _jax.lax.scan(body, init, (per_example, given["loss_target"]))
    with _jax.named_scope("update"):
        delta_w, new_m, new_v = {}, {}, {}
        for n in TWIN_WEIGHTS:
            delta_w[n], new_m[n], new_v[n] = _adamw(weights[n], grad_w[n], given["m_" + n], given["v_" + n])
    return (loss, grad_x, *[grad_w[n] for n in TWIN_WEIGHTS], *[delta_w[n] for n in TWIN_WEIGHTS],
            *[new_m[n] for n in TWIN_WEIGHTS], *[new_v[n] for n in TWIN_WEIGHTS])
```

```python
import numpy as np

import jax
import jax.numpy as jnp
from jax import lax
from jax.experimental import pallas as pl
from jax.experimental.pallas import tpu as pltpu

F32 = jnp.float32
BF16 = jnp.bfloat16

EPS = 1e-6
CHUNK = 64
N_LEVELS = 6
CONV_WIDTH = 31
HALO = 32
CONV_ROWS = 32
HEAD = 128
N_CHIPS = 4
VMEM_LIMIT_BYTES = 56 * 1024 * 1024

ADAM_LR = 0.001
ADAM_B1 = 0.9
ADAM_B2 = 0.999
ADAM_EPS = 1e-08
ADAM_WD = 0.01
ADAM_STEP = 10

MESH = pl.DeviceIdType.MESH
ANY = pl.BlockSpec(memory_space=pl.ANY)

NT = (((1,), (1,)), ((), ()))
TN = (((0,), (0,)), ((), ()))


def _params(**kw):
    return pltpu.CompilerParams(vmem_limit_bytes=VMEM_LIMIT_BYTES, **kw)


def _sigmoid(x):
    return jax.nn.sigmoid(x)


def _dsilu(x, s):
    return s * (1.0 + x * (1.0 - s))


def _row_tile(lp):
    for t in (320, 256, 192, 128, 64):
        if lp % t == 0:
            return t
    raise ValueError(f"unsupported padded length {lp}")


def _mm_row_tile(lp):
    for t in (832, 640, 320, 256, 192, 128, 64):
        if lp % t == 0:
            return t
    raise ValueError(f"unsupported padded length {lp}")


def _dot3(m_bf16, x):
    hi = x.astype(BF16)
    r1 = x - hi.astype(F32)
    mid = r1.astype(BF16)
    lo = (r1 - mid.astype(F32)).astype(BF16)
    return (jnp.dot(m_bf16, hi, preferred_element_type=F32)
            + jnp.dot(m_bf16, mid, preferred_element_type=F32)
            + jnp.dot(m_bf16, lo, preferred_element_type=F32))


def _col_to_row(col):
    return jnp.broadcast_to(col, (HEAD, 8)).T[0:1, :]


def _row_to_col(row):
    return jnp.broadcast_to(row, (8, HEAD)).T[:, 0:1]


def _hgrn_tables():
    t = np.arange(CHUNK)
    ltri = (t[None, :] <= t[:, None]).astype(np.float32)
    mats = [ltri]
    for lvl in range(1, N_LEVELS + 1):
        blk = CHUNK >> (lvl - 1)
        mid = (t // blk) * blk + blk // 2
        mats.append(ltri[mid - 1])
    after = (t[None, :] >= t[:, None]).astype(np.float32)
    before = (t[None, :] < t[:, None]).astype(np.float32)
    return jnp.asarray(np.concatenate(mats, 0), BF16), jnp.asarray(np.concatenate([after, before], 1), BF16)


def _rmsnorm_fwd(hres, g):
    lp, d = hres.shape
    tm = _row_tile(lp)

    def body(x_ref, g_ref, h_ref):
        x = x_ref[...]
        r = lax.rsqrt(jnp.mean(x * x, axis=-1, keepdims=True) + EPS)
        h_ref[...] = (x * r * g_ref[...]).astype(BF16)

    return pl.pallas_call(
        body, grid=(lp // tm,),
        in_specs=[pl.BlockSpec((tm, d), lambda i: (i, 0)), pl.BlockSpec((1, d), lambda i: (0, 0))],
        out_specs=pl.BlockSpec((tm, d), lambda i: (i, 0)),
        out_shape=jax.ShapeDtypeStruct((lp, d), BF16),
        name="rmsnorm_fwd", compiler_params=_params())(hres, g)


def _in_proj(h, wg):
    lp, d = h.shape
    _, _, ncol = wg.shape
    tm = _mm_row_tile(lp)
    nt = 3
    tn = ncol // nt

    def body(h_ref, w_ref, o_ref):
        o_ref[...] = jnp.dot(h_ref[...], w_ref[0], preferred_element_type=F32)

    return pl.pallas_call(
        body, grid=(N_CHIPS, nt, lp // tm),
        in_specs=[pl.BlockSpec((tm, d), lambda j, n, i: (i, 0)),
                  pl.BlockSpec((1, d, tn), lambda j, n, i: (j, 0, n))],
        out_specs=pl.BlockSpec((tm, tn), lambda j, n, i: (i, j * nt + n)),
        out_shape=jax.ShapeDtypeStruct((lp, N_CHIPS * ncol), F32),
        name="in_proj", compiler_params=_params())(h, wg)


def _conv_fwd(proj, conv_w, conv_b, ln_g, ln_b, w_conv):
    lp = proj.shape[0]
    d = conv_b.shape[1]
    tm = _row_tile(lp)
    hb = tm // HALO

    def body(ua_ref, ub_ref, z_ref, uap_ref, ubp_ref, cw_ref, cb_ref, lg_ref, lb_ref, w_ref,
             c_ref, ycin_ref, yconv_ref, aext_ref):
        i = pl.program_id(0)
        a_prev = uap_ref[...] * _sigmoid(ubp_ref[...])
        aext_ref[0:HALO, :] = jnp.where(i > 0, a_prev, 0.0)
        aext_ref[HALO:HALO + tm, :] = ua_ref[...] * _sigmoid(ub_ref[...])

        def row_block(r, carry):
            r0 = pl.multiple_of(r * CONV_ROWS, CONV_ROWS)
            blk = aext_ref[pl.ds(r0, CONV_ROWS + HALO), :]
            acc = jnp.zeros((CONV_ROWS, d), F32) + cb_ref[...]
            for j in range(CONV_WIDTH):
                acc = acc + cw_ref[j:j + 1, :] * blk[j + 2:j + 2 + CONV_ROWS, :]
            c_ref[pl.ds(r0, CONV_ROWS), :] = acc
            return carry

        lax.fori_loop(0, tm // CONV_ROWS, row_block, 0)

        c = c_ref[...]
        mu = jnp.mean(c, axis=-1, keepdims=True)
        xc = c - mu
        rstd = lax.rsqrt(jnp.mean(xc * xc, axis=-1, keepdims=True) + EPS)
        ln = xc * rstd * lg_ref[...] + lb_ref[...]
        s = ln * _sigmoid(ln)
        z = z_ref[...]
        ycin = (s * (z * _sigmoid(z))).astype(BF16)
        ycin_ref[...] = ycin
        yconv_ref[...] = jnp.dot(ycin, w_ref[...], preferred_element_type=F32)

    row = lambda p: pl.BlockSpec((tm, d), lambda i, p=p: (i, p))
    halo = lambda p: pl.BlockSpec((HALO, d), lambda i, p=p: (jnp.maximum(i * hb - 1, 0), p))
    vec = pl.BlockSpec((1, d), lambda i: (0, 0))
    return pl.pallas_call(
        body, grid=(lp // tm,),
        in_specs=[row(0), row(1), row(2), halo(0), halo(1),
                  pl.BlockSpec((HALO, d), lambda i: (0, 0)), vec, vec, vec,
                  pl.BlockSpec((d, d), lambda i: (0, 0))],
        out_specs=[pl.BlockSpec((tm, d), lambda i: (i, 0))] * 3,
        out_shape=[jax.ShapeDtypeStruct((lp, d), F32), jax.ShapeDtypeStruct((lp, d), BF16),
                   jax.ShapeDtypeStruct((lp, d), F32)],
        scratch_shapes=[pltpu.VMEM((HALO + tm, d), F32)],
        name="conv_fwd", compiler_params=_params())(
            proj, proj, proj, proj, proj, conv_w, conv_b, ln_g, ln_b, w_conv)


def _lower_bound(lbl_ref):
    l0 = lbl_ref[0:1, :]
    l1 = lbl_ref[1:2, :]
    m = jnp.maximum(l0, l1)
    e0 = jnp.exp(l0 - m)
    e1 = jnp.exp(l1 - m)
    p0 = e0 / (e0 + e1)
    return p0, p0 * (e1 / (e0 + e1))


def _level_masks():
    rid = lax.broadcasted_iota(jnp.int32, (CHUNK, 1), 0)
    r2 = lax.broadcasted_iota(jnp.int32, (CHUNK, CHUNK), 0)
    c2 = lax.broadcasted_iota(jnp.int32, (CHUNK, CHUNK), 1)
    out = []
    for lvl in range(1, N_LEVELS + 1):
        blk = CHUNK >> (lvl - 1)
        sh = blk.bit_length() - 1
        upper = (rid & (blk - 1)) >= (blk // 2)
        same = (r2 >> sh) == (c2 >> sh)
        out.append((upper, same))
    return out


def _gates(qr, fr, lb, valid):
    sq = _sigmoid(qr)
    q = qr * sq
    sf = _sigmoid(fr)
    f = lb + (1.0 - lb) * sf
    g = jnp.where(valid, jnp.log(f), 0.0)
    k = jnp.where(valid, 1.0 - f, 0.0)
    return q, sq, f, sf, g, k


def _level_factors(b, r, upper):
    eq = jnp.where(upper, jnp.exp(jnp.minimum(b - r, 0.0)), 0.0)
    ek = jnp.where(upper, 0.0, jnp.exp(jnp.minimum(r - b, 0.0)))
    return eq, ek


def _hgrn_fwd(proj, lb_logits, n_pad):
    lp = proj.shape[0]
    d = lb_logits.shape[1]
    n_heads = d // HEAD
    nc = lp // CHUNK
    tab, _ = _hgrn_tables()
    n_tab = tab.shape[0]

    def body(qr_ref, fr_ref, ir_ref, lbl_ref, tab_ref, o_ref, sall_ref, s_ref, t_ref):
        n = pl.program_id(0)

        @pl.when(n == 0)
        def _():
            s_ref[...] = jnp.zeros_like(s_ref)

        sall_ref[0] = s_ref[...]
        lb_all, _ = _lower_bound(lbl_ref)
        rid = lax.broadcasted_iota(jnp.int32, (CHUNK, 1), 0)
        valid = jnp.logical_or(n > 0, rid >= n_pad)
        f_all = lb_all + (1.0 - lb_all) * _sigmoid(fr_ref[...])
        t_ref[...] = _dot3(tab_ref[...], jnp.where(valid, jnp.log(f_all), 0.0))
        masks = _level_masks()

        def head(h, carry):
            off = pl.multiple_of(h * HEAD, HEAD)
            hs = pl.ds(off, HEAD)
            lb = _lower_bound_slice(lbl_ref, hs)
            q, _, _, _, _, k = _gates(qr_ref[:, hs], fr_ref[:, hs], lb, valid)
            v = ir_ref[:, hs]
            b = t_ref[0:CHUNK, hs]
            s0 = s_ref[hs, :]
            o = jnp.dot((q * jnp.exp(b)).astype(BF16), s0.astype(BF16), preferred_element_type=F32)
            o = o + jnp.sum(q * k, axis=-1, keepdims=True) * v
            a = jnp.zeros((CHUNK, CHUNK), F32)
            for lvl in range(1, N_LEVELS + 1):
                upper, same = masks[lvl - 1]
                eq, ek = _level_factors(b, t_ref[CHUNK * lvl:CHUNK * (lvl + 1), hs], upper)
                p = lax.dot_general((q * eq).astype(BF16), (k * ek).astype(BF16), NT, preferred_element_type=F32)
                a = a + jnp.where(same, p, 0.0)
            vb = v.astype(BF16)
            o_ref[:, hs] = o + jnp.dot(a.astype(BF16), vb, preferred_element_type=F32)
            b_last = t_ref[CHUNK - 1:CHUNK, hs]
            khat = (k * jnp.exp(b_last - b)).astype(BF16)
            s_ref[hs, :] = _row_to_col(jnp.exp(b_last)) * s0 + lax.dot_general(khat, vb, TN, preferred_element_type=F32)
            return carry

        lax.fori_loop(0, n_heads, head, 0)

    piece = lambda p: pl.BlockSpec((CHUNK, d), lambda n, p=p: (n, p))
    return pl.pallas_call(
        body, grid=(nc,),
        in_specs=[piece(3), piece(4), piece(5), pl.BlockSpec((2, d), lambda n: (0, 0)),
                  pl.BlockSpec((n_tab, CHUNK), lambda n: (0, 0))],
        out_specs=[pl.BlockSpec((CHUNK, d), lambda n: (n, 0)), pl.BlockSpec((1, d, HEAD), lambda n: (n, 0, 0))],
        out_shape=[jax.ShapeDtypeStruct((lp, d), F32), jax.ShapeDtypeStruct((nc, d, HEAD), F32)],
        scratch_shapes=[pltpu.VMEM((d, HEAD), F32), pltpu.VMEM((n_tab, d), F32)],
        name="hgrn_fwd", compiler_params=_params())(proj, proj, proj, lb_logits, tab)


def _lower_bound_slice(lbl_ref, hs):
    l0 = lbl_ref[0:1, hs]
    l1 = lbl_ref[1:2, hs]
    m = jnp.maximum(l0, l1)
    e0 = jnp.exp(l0 - m)
    e1 = jnp.exp(l1 - m)
    return e0 / (e0 + e1)


def _tail_fwd(o, proj, y_conv, hres, target, gnorm_g, final_g, w_rec, w_out):
    lp, d = o.shape
    n_heads = d // HEAD
    tm = _row_tile(lp)

    def body(o_ref, gr_ref, mc_ref, mr_ref, yc_ref, x_ref, t_ref, gn_ref, fg_ref, wr_ref, wo_ref,
             yrin_ref, mg_ref, yrec_ref, dout_ref, loss_ref, dfg_ref):
        i = pl.program_id(0)

        @pl.when(i == 0)
        def _():
            loss_ref[...] = jnp.zeros_like(loss_ref)
            dfg_ref[...] = jnp.zeros_like(dfg_ref)

        for h in range(n_heads):
            hs = slice(h * HEAD, (h + 1) * HEAD)
            oh = o_ref[:, hs]
            on = oh * lax.rsqrt(jnp.mean(oh * oh, axis=-1, keepdims=True) + EPS) * gn_ref[:, hs]
            gr = gr_ref[:, hs]
            yrin_ref[:, hs] = (on * (gr * _sigmoid(gr))).astype(BF16)
        yrec = jnp.dot(yrin_ref[...], wr_ref[...], preferred_element_type=F32)
        yrec_ref[...] = yrec
        merged = (_sigmoid(mc_ref[...]) * yc_ref[...] + _sigmoid(mr_ref[...]) * yrec).astype(BF16)
        mg_ref[...] = merged
        out = x_ref[...] + jnp.dot(merged, wo_ref[...], preferred_element_type=F32)
        r = lax.rsqrt(jnp.mean(out * out, axis=-1, keepdims=True) + EPS)
        yhat = out * r
        fg = fg_ref[...]
        rid = lax.broadcasted_iota(jnp.int32, (tm, 1), 0) + i * tm
        err = jnp.where(rid >= CHUNK, yhat * fg - t_ref[...], 0.0)
        loss_ref[...] += 0.5 * jnp.sum(err * err) / d
        dy = err / d
        dfg_ref[...] += jnp.sum(dy * yhat, axis=0, keepdims=True)
        dyh = dy * fg
        dout_ref[...] = r * (dyh - yhat * jnp.mean(dyh * yhat, axis=-1, keepdims=True))

    row = lambda p: pl.BlockSpec((tm, d), lambda i, p=p: (i, p))
    vec = pl.BlockSpec((1, d), lambda i: (0, 0))
    mat = pl.BlockSpec((d, d), lambda i: (0, 0))
    return pl.pallas_call(
        body, grid=(lp // tm,),
        in_specs=[row(0), row(6), row(7), row(8), row(0), row(0), row(0), vec, vec, mat, mat],
        out_specs=[row(0), row(0), row(0), row(0), pl.BlockSpec((8, 128), lambda i: (0, 0)), vec],
        out_shape=[jax.ShapeDtypeStruct((lp, d), BF16), jax.ShapeDtypeStruct((lp, d), BF16),
                   jax.ShapeDtypeStruct((lp, d), F32), jax.ShapeDtypeStruct((lp, d), F32),
                   jax.ShapeDtypeStruct((8, 128), F32), jax.ShapeDtypeStruct((1, d), F32)],
        name="tail_fwd", compiler_params=_params())(
            o, proj, proj, proj, y_conv, hres, target, gnorm_g, final_g, w_rec, w_out)


def _tail_bwd(dout, proj, y_conv, y_rec, o, c, w_out, w_rec, w_conv, ln_g, ln_b, gnorm_g):
    lp, d = dout.shape
    n_heads = d // HEAD
    tm = _row_tile(lp)

    def body(dout_ref, mc_ref, mr_ref, z_ref, gr_ref, yc_ref, yrec_ref, o_ref, c_ref,
             wo_ref, wr_ref, wc_ref, lg_ref, lb_ref, gn_ref,
             dyc_ref, dyr_ref, dz_ref, dgr_ref, dmc_ref, dmr_ref, do_ref, dc_ref,
             dgn_ref, dlg_ref, dlb_ref, dyrin_ref):
        i = pl.program_id(0)

        @pl.when(i == 0)
        def _():
            dgn_ref[...] = jnp.zeros_like(dgn_ref)
            dlg_ref[...] = jnp.zeros_like(dlg_ref)
            dlb_ref[...] = jnp.zeros_like(dlb_ref)

        dmerged = lax.dot_general(dout_ref[...].astype(BF16), wo_ref[...], NT, preferred_element_type=F32)
        smc = _sigmoid(mc_ref[...])
        smr = _sigmoid(mr_ref[...])
        dyc = (dmerged * smc).astype(BF16)
        dyr = (dmerged * smr).astype(BF16)
        dyc_ref[...] = dyc
        dyr_ref[...] = dyr
        dmc_ref[...] = (dmerged * yc_ref[...] * smc * (1.0 - smc)).astype(BF16)
        dmr_ref[...] = (dmerged * yrec_ref[...] * smr * (1.0 - smr)).astype(BF16)

        dyrin_ref[...] = lax.dot_general(dyr, wr_ref[...], NT, preferred_element_type=F32)
        for h in range(n_heads):
            hs = slice(h * HEAD, (h + 1) * HEAD)
            oh = o_ref[:, hs]
            rstd = lax.rsqrt(jnp.mean(oh * oh, axis=-1, keepdims=True) + EPS)
            ohat = oh * rstd
            gn = gn_ref[:, hs]
            gr = gr_ref[:, hs]
            sg = _sigmoid(gr)
            dyrin = dyrin_ref[:, hs]
            don = dyrin * (gr * sg)
            dgr_ref[:, hs] = (dyrin * (ohat * gn) * _dsilu(gr, sg)).astype(BF16)
            dgn_ref[:, hs] += jnp.sum(don * ohat, axis=0, keepdims=True)
            doh = don * gn
            do_ref[:, hs] = rstd * (doh - ohat * jnp.mean(doh * ohat, axis=-1, keepdims=True))

        dycin = lax.dot_general(dyc, wc_ref[...], NT, preferred_element_type=F32)
        c = c_ref[...]
        mu = jnp.mean(c, axis=-1, keepdims=True)
        xc = c - mu
        rstd = lax.rsqrt(jnp.mean(xc * xc, axis=-1, keepdims=True) + EPS)
        nrm = xc * rstd
        lg = lg_ref[...]
        ln = nrm * lg + lb_ref[...]
        sl = _sigmoid(ln)
        z = z_ref[...]
        sz = _sigmoid(z)
        dz_ref[...] = (dycin * (ln * sl) * _dsilu(z, sz)).astype(BF16)
        dln = dycin * (z * sz) * _dsilu(ln, sl)
        dlg_ref[...] += jnp.sum(dln * nrm, axis=0, keepdims=True)
        dlb_ref[...] += jnp.sum(dln, axis=0, keepdims=True)
        dn = dln * lg
        dc_ref[...] = rstd * (dn - jnp.mean(dn, axis=-1, keepdims=True)
                              - nrm * jnp.mean(dn * nrm, axis=-1, keepdims=True))

    row = lambda p: pl.BlockSpec((tm, d), lambda i, p=p: (i, p))
    vec = pl.BlockSpec((1, d), lambda i: (0, 0))
    mat = pl.BlockSpec((d, d), lambda i: (0, 0))
    act_bf = jax.ShapeDtypeStruct((lp, d), BF16)
    act_f32 = jax.ShapeDtypeStruct((lp, d), F32)
    vec_f32 = jax.ShapeDtypeStruct((1, d), F32)
    return pl.pallas_call(
        body, grid=(lp // tm,),
        in_specs=[row(0), row(7), row(8), row(2), row(6), row(0), row(0), row(0), row(0),
                  mat, mat, mat, vec, vec, vec],
        out_specs=[row(0)] * 8 + [vec] * 3,
        out_shape=[act_bf] * 6 + [act_f32] * 2 + [vec_f32] * 3,
        scratch_shapes=[pltpu.VMEM((tm, d), F32)],
        name="tail_bwd", compiler_params=_params())(
            dout, proj, proj, proj, proj, y_conv, y_rec, o, c, w_out, w_rec, w_conv, ln_g, ln_b, gnorm_g)


def _hgrn_bwd(proj, do, s_all, lb_logits, n_pad):
    lp, d = do.shape
    n_heads = d // HEAD
    nc = lp // CHUNK
    tab, utri = _hgrn_tables()
    n_tab = tab.shape[0]

    def body(qr_ref, fr_ref, ir_ref, do_ref, s0_ref, lbl_ref, tab_ref, ut_ref,
             dq_ref, df_ref, di_ref, dlbl_ref, ds_ref, t_ref, dlb_ref):
        n = pl.program_id(0)
        chunk = nc - 1 - n

        @pl.when(n == 0)
        def _():
            ds_ref[...] = jnp.zeros_like(ds_ref)
            dlb_ref[...] = jnp.zeros_like(dlb_ref)

        lb_all, pp = _lower_bound(lbl_ref)
        rid = lax.broadcasted_iota(jnp.int32, (CHUNK, 1), 0)
        valid = jnp.logical_or(chunk > 0, rid >= n_pad)
        f_all = lb_all + (1.0 - lb_all) * _sigmoid(fr_ref[...])
        t_ref[...] = _dot3(tab_ref[...], jnp.where(valid, jnp.log(f_all), 0.0))
        masks = _level_masks()
        ut = ut_ref[...]

        def head(h, carry):
            off = pl.multiple_of(h * HEAD, HEAD)
            hs = pl.ds(off, HEAD)
            lb = _lower_bound_slice(lbl_ref, hs)
            qr = qr_ref[:, hs]
            q, sq, f, sf, _, k = _gates(qr, fr_ref[:, hs], lb, valid)
            v = ir_ref[:, hs]
            do_h = do_ref[:, hs]
            b = t_ref[0:CHUNK, hs]
            b_last = t_ref[CHUNK - 1:CHUNK, hs]
            s0 = s0_ref[0, hs, :]
            ds1 = ds_ref[hs, :]
            eb = jnp.exp(b)
            ekl = jnp.exp(b_last - b)
            do_bf = do_h.astype(BF16)
            v_bf = v.astype(BF16)
            ds1_bf = ds1.astype(BF16)

            da = lax.dot_general(do_bf, v_bf, NT, preferred_element_type=F32)
            da_diag = jnp.sum(do_h * v, axis=-1, keepdims=True)
            a = jnp.zeros((CHUNK, CHUNK), F32)
            dq_x = eb * lax.dot_general(do_bf, s0.astype(BF16), NT, preferred_element_type=F32)
            dk_x = ekl * lax.dot_general(v_bf, ds1_bf, NT, preferred_element_type=F32)
            x_after = q * dq_x
            x_before = k * dk_x
            for lvl in range(1, N_LEVELS + 1):
                upper, same = masks[lvl - 1]
                eq, ek = _level_factors(b, t_ref[CHUNK * lvl:CHUNK * (lvl + 1), hs], upper)
                qt = (q * eq).astype(BF16)
                kt = (k * ek).astype(BF16)
                p = lax.dot_general(qt, kt, NT, preferred_element_type=F32)
                a = a + jnp.where(same, p, 0.0)
                dam = jnp.where(same, da, 0.0).astype(BF16)
                dqt = jnp.dot(dam, kt, preferred_element_type=F32)
                dkt = lax.dot_general(dam, qt, TN, preferred_element_type=F32)
                dq_x = dq_x + eq * dqt
                dk_x = dk_x + ek * dkt
                x_after = x_after + (qt.astype(F32) * dqt - kt.astype(F32) * dkt)

            dv = (lax.dot_general(a.astype(BF16), do_bf, TN, preferred_element_type=F32)
                  + jnp.sum(q * k, axis=-1, keepdims=True) * do_h
                  + jnp.dot((k * ekl).astype(BF16), ds1_bf, preferred_element_type=F32))
            di_ref[:, hs] = dv.astype(BF16)

            carried = jnp.exp(b_last) * _col_to_row(jnp.sum(s0 * ds1, axis=-1, keepdims=True))
            dg = _dot3(ut, jnp.concatenate([x_after, x_before], axis=0)) + carried
            dq = dq_x + da_diag * k
            dk = dk_x + da_diag * q
            dq_ref[:, hs] = (dq * _dsilu(qr, sq)).astype(BF16)
            df = jnp.where(valid, dg / f - dk, 0.0)
            df_ref[:, hs] = (df * (1.0 - lb) * sf * (1.0 - sf)).astype(BF16)
            dlb_ref[:, hs] += jnp.sum(df * (1.0 - sf), axis=0, keepdims=True)

            ds_ref[hs, :] = (_row_to_col(jnp.exp(b_last)) * ds1
                             + lax.dot_general((q * eb).astype(BF16), do_bf, TN, preferred_element_type=F32))
            return carry

        lax.fori_loop(0, n_heads, head, 0)

        @pl.when(n == nc - 1)
        def _():
            dl0 = dlb_ref[...] * pp
            dlbl_ref[0:1, :] = dl0
            dlbl_ref[1:2, :] = -dl0

    piece = lambda p: pl.BlockSpec((CHUNK, d), lambda n, p=p: (nc - 1 - n, p))
    return pl.pallas_call(
        body, grid=(nc,),
        in_specs=[piece(3), piece(4), piece(5), piece(0),
                  pl.BlockSpec((1, d, HEAD), lambda n: (nc - 1 - n, 0, 0)),
                  pl.BlockSpec((2, d), lambda n: (0, 0)),
                  pl.BlockSpec((n_tab, CHUNK), lambda n: (0, 0)),
                  pl.BlockSpec((CHUNK, 2 * CHUNK), lambda n: (0, 0))],
        out_specs=[piece(0), piece(0), piece(0), pl.BlockSpec((2, d), lambda n: (0, 0))],
        out_shape=[jax.ShapeDtypeStruct((lp, d), BF16)] * 3 + [jax.ShapeDtypeStruct((2, d), F32)],
        scratch_shapes=[pltpu.VMEM((d, HEAD), F32), pltpu.VMEM((n_tab, d), F32), pltpu.VMEM((1, d), F32)],
        name="hgrn_bwd", compiler_params=_params())(proj, proj, proj, do, s_all, lb_logits, tab, utri)


def _conv_bwd(dc, proj, conv_w):
    lp, d = dc.shape
    tm = _row_tile(lp)
    hb = tm // HALO
    n_tiles = lp // tm
    last_halo = lp // HALO - 1

    def body(dc_ref, dcn_ref, ua_ref, ub_ref, uap_ref, ubp_ref, cw_ref,
             dua_ref, dub_ref, dcw_ref, dcb_ref, aext_ref, dcext_ref, da_ref):
        i = pl.program_id(0)

        @pl.when(i == 0)
        def _():
            dcw_ref[...] = jnp.zeros_like(dcw_ref)
            dcb_ref[...] = jnp.zeros_like(dcb_ref)

        ua = ua_ref[...]
        sb = _sigmoid(ub_ref[...])
        a_prev = uap_ref[...] * _sigmoid(ubp_ref[...])
        aext_ref[0:HALO, :] = jnp.where(i > 0, a_prev, 0.0)
        aext_ref[HALO:HALO + tm, :] = ua * sb
        dcext_ref[0:tm, :] = dc_ref[...]
        dcext_ref[tm:tm + HALO, :] = jnp.where(i < n_tiles - 1, dcn_ref[...], 0.0)
        dcb_ref[...] += jnp.sum(dc_ref[...], axis=0, keepdims=True)

        def row_block(r, carry):
            r0 = pl.multiple_of(r * CONV_ROWS, CONV_ROWS)
            dblk = dcext_ref[pl.ds(r0, CONV_ROWS + HALO), :]
            ablk = aext_ref[pl.ds(r0, CONV_ROWS + HALO), :]
            dcur = dblk[0:CONV_ROWS, :]
            acc = jnp.zeros((CONV_ROWS, d), F32)
            for j in range(CONV_WIDTH):
                acc = acc + cw_ref[j:j + 1, :] * dblk[30 - j:30 - j + CONV_ROWS, :]
                dcw_ref[j:j + 1, :] += jnp.sum(dcur * ablk[j + 2:j + 2 + CONV_ROWS, :], axis=0, keepdims=True)
            da_ref[pl.ds(r0, CONV_ROWS), :] = acc
            return carry

        lax.fori_loop(0, tm // CONV_ROWS, row_block, 0)

        da = da_ref[...]
        dua_ref[...] = (da * sb).astype(BF16)
        dub_ref[...] = (da * ua * sb * (1.0 - sb)).astype(BF16)

    row = lambda p: pl.BlockSpec((tm, d), lambda i, p=p: (i, p))
    prev = lambda p: pl.BlockSpec((HALO, d), lambda i, p=p: (jnp.maximum(i * hb - 1, 0), p))
    nxt = pl.BlockSpec((HALO, d), lambda i: (jnp.minimum((i + 1) * hb, last_halo), 0))
    return pl.pallas_call(
        body, grid=(n_tiles,),
        in_specs=[row(0), nxt, row(0), row(1), prev(0), prev(1), pl.BlockSpec((HALO, d), lambda i: (0, 0))],
        out_specs=[row(0), row(0), pl.BlockSpec((HALO, d), lambda i: (0, 0)), pl.BlockSpec((1, d), lambda i: (0, 0))],
        out_shape=[jax.ShapeDtypeStruct((lp, d), BF16), jax.ShapeDtypeStruct((lp, d), BF16),
                   jax.ShapeDtypeStruct((HALO, d), F32), jax.ShapeDtypeStruct((1, d), F32)],
        scratch_shapes=[pltpu.VMEM((HALO + tm, d), F32), pltpu.VMEM((tm + HALO, d), F32), pltpu.VMEM((tm, d), F32)],
        name="conv_bwd", compiler_params=_params())(dc, dc, proj, proj, proj, proj, conv_w)


def _weight_grad(xs, dy, name, blocked):
    lp, dx = xs.shape
    n = dy.shape[1]
    tk = _mm_row_tile(lp)
    if blocked:
        ncol = n // N_CHIPS
        nt = 3
        tn = ncol // nt
        grid = (N_CHIPS * nt, lp // tk)
        out_spec = pl.BlockSpec((1, dx, tn), lambda c, k: (c // nt, 0, c % nt))
        out_shape = jax.ShapeDtypeStruct((N_CHIPS, dx, ncol), F32)
    else:
        tn = n // 2
        grid = (2, lp // tk)
        out_spec = pl.BlockSpec((dx, tn), lambda c, k: (0, c))
        out_shape = jax.ShapeDtypeStruct((dx, n), F32)

    def body(xs_ref, dy_ref, o_ref):
        @pl.when(pl.program_id(1) == 0)
        def _():
            o_ref[...] = jnp.zeros_like(o_ref)

        p = lax.dot_general(xs_ref[...], dy_ref[...], TN, preferred_element_type=F32)
        if blocked:
            o_ref[0] += p
        else:
            o_ref[...] += p

    return pl.pallas_call(
        body, grid=grid,
        in_specs=[pl.BlockSpec((tk, dx), lambda c, k: (k, 0)), pl.BlockSpec((tk, tn), lambda c, k: (k, c))],
        out_specs=out_spec, out_shape=out_shape,
        name=name, compiler_params=_params())(xs, dy)


def _in_proj_bwd(dproj, wg, hres, norm_g, dout):
    lp, d = hres.shape
    _, _, ncol = wg.shape
    tm = _mm_row_tile(lp)
    nt = 3
    tn = ncol // nt
    nk = N_CHIPS * nt

    def body(dp_ref, w_ref, x_ref, g_ref, dout_ref, dx_ref, dg_ref, acc_ref):
        i = pl.program_id(0)
        kk = pl.program_id(1)

        @pl.when(jnp.logical_and(i == 0, kk == 0))
        def _():
            dg_ref[...] = jnp.zeros_like(dg_ref)

        @pl.when(kk == 0)
        def _():
            acc_ref[...] = jnp.zeros_like(acc_ref)

        acc_ref[...] += lax.dot_general(dp_ref[...], w_ref[0], NT, preferred_element_type=F32)

        @pl.when(kk == nk - 1)
        def _():
            x = x_ref[...]
            r = lax.rsqrt(jnp.mean(x * x, axis=-1, keepdims=True) + EPS)
            xhat = x * r
            dh = acc_ref[...]
            dg_ref[...] += jnp.sum(dh * xhat, axis=0, keepdims=True)
            dxh = dh * g_ref[...]
            dx_ref[...] = dout_ref[...] + r * (dxh - xhat * jnp.mean(dxh * xhat, axis=-1, keepdims=True))

    return pl.pallas_call(
        body, grid=(lp // tm, nk),
        in_specs=[pl.BlockSpec((tm, tn), lambda i, k: (i, k)),
                  pl.BlockSpec((1, d, tn), lambda i, k: (k // nt, 0, k % nt)),
                  pl.BlockSpec((tm, d), lambda i, k: (i, 0)),
                  pl.BlockSpec((1, d), lambda i, k: (0, 0)),
                  pl.BlockSpec((tm, d), lambda i, k: (i, 0))],
        out_specs=[pl.BlockSpec((tm, d), lambda i, k: (i, 0)), pl.BlockSpec((1, d), lambda i, k: (0, 0))],
        out_shape=[jax.ShapeDtypeStruct((lp, d), F32), jax.ShapeDtypeStruct((1, d), F32)],
        scratch_shapes=[pltpu.VMEM((tm, d), F32)],
        name="in_proj_bwd", compiler_params=_params())(dproj, wg, hres, norm_g, dout)


def _adamw_math(w, g, m, v):
    m = ADAM_B1 * m + (1.0 - ADAM_B1) * g
    v = ADAM_B2 * v + (1.0 - ADAM_B2) * (g * g)
    m_hat = m / (1.0 - ADAM_B1 ** ADAM_STEP)
    v_hat = v / (1.0 - ADAM_B2 ** ADAM_STEP)
    delta = -ADAM_LR * (m_hat / (jnp.sqrt(v_hat) + ADAM_EPS) + ADAM_WD * w)
    return delta, m, v


def _elementwise_rows(shape):
    r, c = shape
    for t in (256, 128, 64, 32, 16, 8):
        if r % t == 0 and r > t and t * c * 4 <= 1024 * 1024:
            return t
    return r


def _adamw(name, w, m, v, *g_parts):
    shape = w.shape
    tr = _elementwise_rows(shape)
    n_g = len(g_parts)

    def body(*refs):
        w_ref, m_ref, v_ref = refs[:3]
        g_refs = refs[3:3 + n_g]
        g_out, d_out, m_out, v_out = refs[3 + n_g:]
        g = g_refs[0][...]
        for gr in g_refs[1:]:
            g = g + gr[...]
        delta, m_new, v_new = _adamw_math(w_ref[...], g, m_ref[...], v_ref[...])
        g_out[...] = g
        d_out[...] = delta
        m_out[...] = m_new
        v_out[...] = v_new

    spec = pl.BlockSpec((tr, shape[1]), lambda i: (i, 0))
    return pl.pallas_call(
        body, grid=(shape[0] // tr,),
        in_specs=[spec] * (3 + n_g), out_specs=[spec] * 4,
        out_shape=[jax.ShapeDtypeStruct(shape, F32)] * 4,
        name=name, compiler_params=_params())(w, m, v, *g_parts)


def _sum_parts(name, own, recv):
    shape = own.shape
    k = recv.shape[0]
    tr = _elementwise_rows(shape)

    def body(own_ref, recv_ref, o_ref):
        s = own_ref[...]
        for j in range(k):
            s = s + recv_ref[j]
        o_ref[...] = s

    spec = pl.BlockSpec((tr, shape[1]), lambda i: (i, 0))
    return pl.pallas_call(
        body, grid=(shape[0] // tr,),
        in_specs=[spec, pl.BlockSpec((k, tr, shape[1]), lambda i: (0, i, 0))],
        out_specs=spec, out_shape=jax.ShapeDtypeStruct(shape, F32),
        name=name, compiler_params=_params())(own, recv)


def _sum_slots(name, slots):
    k, r, c = slots.shape

    def body(s_ref, o_ref):
        s = s_ref[0]
        for j in range(1, k):
            s = s + s_ref[j]
        o_ref[...] = s

    return pl.pallas_call(body, out_shape=jax.ShapeDtypeStruct((r, c), F32), name=name,
                          compiler_params=_params())(slots)


def _mesh_pos():
    return lax.axis_index("x"), lax.axis_index("y"), lax.axis_index("c")


def _other_chips(x, y):
    return [(1 - x, y), (x, 1 - y), (1 - x, 1 - y)]


def _gather_weights(shards):
    n = len(shards)

    def body(*refs):
        srcs = refs[:n]
        dsts = refs[n:2 * n]
        send_sems, recv_sems, local_sems = refs[2 * n:]
        x, y, c = _mesh_pos()
        me = 2 * x + y
        chips = _other_chips(x, y)
        local = [pltpu.make_async_copy(srcs[a], dsts[a].at[me], local_sems.at[a]) for a in range(n)]
        for cp in local:
            cp.start()
        for a in range(n):
            for k, (px, py) in enumerate(chips):
                pltpu.make_async_remote_copy(
                    src_ref=srcs[a], dst_ref=dsts[a].at[me], send_sem=send_sems.at[a, k],
                    recv_sem=recv_sems.at[a, k], device_id=(px, py, c), device_id_type=MESH).start()
        for a in range(n):
            for k, (px, py) in enumerate(chips):
                pltpu.make_async_remote_copy(
                    src_ref=srcs[a], dst_ref=dsts[a].at[2 * px + py], send_sem=send_sems.at[a, k],
                    recv_sem=recv_sems.at[a, k], device_id=(px, py, c), device_id_type=MESH).wait()
        for cp in local:
            cp.wait()

    return pl.pallas_call(
        body, in_specs=[ANY] * n, out_specs=[ANY] * n,
        out_shape=[jax.ShapeDtypeStruct((N_CHIPS,) + s.shape, s.dtype) for s in shards],
        scratch_shapes=[pltpu.SemaphoreType.DMA((n, 3)), pltpu.SemaphoreType.DMA((n, 3)),
                        pltpu.SemaphoreType.DMA((n,))],
        name="gather_weights")(*shards)


def _exchange_grads(blocked, small):
    n = len(blocked)

    def body(*refs):
        srcs = refs[:n]
        small_src = refs[n]
        dsts = refs[n + 1:2 * n + 1]
        small_dst = refs[2 * n + 1]
        send_sems, recv_sems, ssend_sems, srecv_sems, local_sem = refs[2 * n + 2:]
        x, y, c = _mesh_pos()
        chips = _other_chips(x, y)
        my_idx = 4 * x + 2 * y + c
        local = pltpu.make_async_copy(small_src, small_dst.at[my_idx], local_sem)
        local.start()
        others = []
        for r in range(1, 8):
            px = 1 - x if r & 4 else x
            py = 1 - y if r & 2 else y
            pc = 1 - c if r & 1 else c
            others.append((px, py, pc))
        for r, peer in enumerate(others):
            pltpu.make_async_remote_copy(
                src_ref=small_src, dst_ref=small_dst.at[my_idx], send_sem=ssend_sems.at[r],
                recv_sem=srecv_sems.at[r], device_id=peer, device_id_type=MESH).start()
        for a in range(n):
            for k, (px, py) in enumerate(chips):
                pltpu.make_async_remote_copy(
                    src_ref=srcs[a].at[2 * px + py], dst_ref=dsts[a].at[k], send_sem=send_sems.at[a, k],
                    recv_sem=recv_sems.at[a, k], device_id=(px, py, c), device_id_type=MESH).start()
        for r, (px, py, pc) in enumerate(others):
            pltpu.make_async_remote_copy(
                src_ref=small_src, dst_ref=small_dst.at[4 * px + 2 * py + pc], send_sem=ssend_sems.at[r],
                recv_sem=srecv_sems.at[r], device_id=(px, py, pc), device_id_type=MESH).wait()
        for a in range(n):
            for k, (px, py) in enumerate(chips):
                pltpu.make_async_remote_copy(
                    src_ref=srcs[a].at[2 * px + py], dst_ref=dsts[a].at[k], send_sem=send_sems.at[a, k],
                    recv_sem=recv_sems.at[a, k], device_id=(px, py, c), device_id_type=MESH).wait()
        local.wait()

    out_shape = [jax.ShapeDtypeStruct((3,) + b.shape[1:], F32) for b in blocked]
    out_shape.append(jax.ShapeDtypeStruct((8,) + small.shape, F32))
    return pl.pallas_call(
        body, in_specs=[ANY] * (n + 1), out_specs=[ANY] * (n + 1), out_shape=out_shape,
        scratch_shapes=[pltpu.SemaphoreType.DMA((n, 3)), pltpu.SemaphoreType.DMA((n, 3)),
                        pltpu.SemaphoreType.DMA((7,)), pltpu.SemaphoreType.DMA((7,)),
                        pltpu.SemaphoreType.DMA],
        name="exchange_grads")(*blocked, small)


def _swap_with_sibling(parts):
    n = len(parts)

    def body(*refs):
        srcs = refs[:n]
        dsts = refs[n:2 * n]
        send_sems, recv_sems = refs[2 * n:]
        x, y, c = _mesh_pos()
        copies = [pltpu.make_async_remote_copy(
            src_ref=srcs[a], dst_ref=dsts[a], send_sem=send_sems.at[a], recv_sem=recv_sems.at[a],
            device_id=(x, y, 1 - c), device_id_type=MESH) for a in range(n)]
        for cp in copies:
            cp.start()
        for cp in copies:
            cp.wait()

    return pl.pallas_call(
        body, in_specs=[ANY] * n, out_specs=[ANY] * n,
        out_shape=[jax.ShapeDtypeStruct(p.shape, p.dtype) for p in parts],
        scratch_shapes=[pltpu.SemaphoreType.DMA((n,)), pltpu.SemaphoreType.DMA((n,))],
        name="swap_with_sibling")(*parts)


def kernel(x, meta_tokens, norm_g, w_in, conv_w, conv_b, ln_g, ln_b, w_conv_out, lb_logits, gnorm_g, w_rec_out, w_out, final_g, loss_target, m_meta_tokens, m_norm_g, m_w_in, m_conv_w, m_conv_b, m_ln_g, m_ln_b, m_w_conv_out, m_lb_logits, m_gnorm_g, m_w_rec_out, m_w_out, m_final_g, v_meta_tokens, v_norm_g, v_w_in, v_conv_w, v_conv_b, v_ln_g, v_ln_b, v_w_conv_out, v_lb_logits, v_gnorm_g, v_w_rec_out, v_w_out, v_final_g):
    seq, d = x.shape[1], x.shape[2]
    n_meta = meta_tokens.shape[0]
    n_pad = CHUNK - n_meta
    ds = d // N_CHIPS
    chip = 2 * lax.axis_index("x") + lax.axis_index("y")

    conv_w_pad = jnp.pad(conv_w[0], ((0, HALO - CONV_WIDTH), (0, 0)))
    win_g, wc_g, wr_g, wo_g, cw_g, meta_g = _gather_weights([
        w_in[0].astype(BF16), w_conv_out[0].astype(BF16), w_rec_out[0].astype(BF16), w_out[0].astype(BF16),
        conv_w_pad, meta_tokens])
    wc_full = wc_g.reshape(d, d)
    wr_full = wr_g.reshape(d, d)
    wo_full = wo_g.reshape(d, d)
    cw_full = jnp.transpose(cw_g, (1, 0, 2)).reshape(HALO, d)
    meta_full = jnp.transpose(meta_g, (1, 0, 2)).reshape(n_meta, d)

    hres = jnp.concatenate([jnp.zeros((n_pad, d), F32), meta_full, x[0]], axis=0)
    target = jnp.pad(loss_target[0], ((CHUNK, 0), (0, 0)))
    final_g2 = final_g.reshape(1, d)
    h = _rmsnorm_fwd(hres, norm_g)
    proj = _in_proj(h, win_g)
    c, yc_in, y_conv = _conv_fwd(proj, cw_full, conv_b, ln_g, ln_b, wc_full)
    o, s_all = _hgrn_fwd(proj, lb_logits, n_pad)
    yr_in, merged, y_rec, dout, loss_acc, dfinal_g = _tail_fwd(
        o, proj, y_conv, hres, target, gnorm_g, final_g2, wr_full, wo_full)

    (dyc, dyr, dz, dgr, dmc, dmr, do, dc, dgnorm_g, dln_g, dln_b) = _tail_bwd(
        dout, proj, y_conv, y_rec, o, c, wo_full, wr_full, wc_full, ln_g, ln_b, gnorm_g)
    dq, df, di, dlb_logits = _hgrn_bwd(proj, do, s_all, lb_logits, n_pad)
    dua, dub, dconv_w, dconv_b = _conv_bwd(dc, proj, cw_full)
    dproj = jnp.concatenate([dua, dub, dz, dq, df, di, dgr, dmc, dmr], axis=1)
    g_win = _weight_grad(h, dproj, "grad_w_in", True)
    g_wc = _weight_grad(yc_in, dyc, "grad_w_conv_out", False)
    g_wr = _weight_grad(yr_in, dyr, "grad_w_rec_out", False)
    g_wo = _weight_grad(merged, dout.astype(BF16), "grad_w_out", False)
    dhres, dnorm_g = _in_proj_bwd(dproj, win_g, hres, norm_g, dout)
    grad_x = dhres[CHUNK:][None]

    small = jnp.concatenate([dnorm_g, dconv_b, dln_g, dln_b, dlb_logits, dgnorm_g, dfinal_g,
                             dhres[n_pad:CHUNK], dconv_w[:CONV_WIDTH],
                             jnp.zeros((1, d), F32)], axis=0)
    blocked = [g_win, g_wc.reshape(N_CHIPS, ds, d), g_wr.reshape(N_CHIPS, ds, d), g_wo.reshape(N_CHIPS, ds, d)]
    recv_win, recv_wc, recv_wr, recv_wo, small_slots = _exchange_grads(blocked, small)
    own = [lax.dynamic_index_in_dim(b, chip, 0, keepdims=False) for b in blocked]
    part = [_sum_parts("sum_" + nm, o_, r_) for nm, o_, r_ in zip(
        ("w_in", "w_conv_out", "w_rec_out", "w_out"), own, (recv_win, recv_wc, recv_wr, recv_wo))]
    sib = _swap_with_sibling(part)
    small_sum = _sum_slots("sum_small", small_slots)

    res = {}
    res["w_in"] = _adamw("adamw_w_in", w_in[0], m_w_in[0], v_w_in[0], part[0], sib[0])
    res["w_conv_out"] = _adamw("adamw_w_conv_out", w_conv_out[0], m_w_conv_out[0], v_w_conv_out[0], part[1], sib[1])
    res["w_rec_out"] = _adamw("adamw_w_rec_out", w_rec_out[0], m_w_rec_out[0], v_w_rec_out[0], part[2], sib[2])
    res["w_out"] = _adamw("adamw_w_out", w_out[0], m_w_out[0], v_w_out[0], part[3], sib[3])
    big = {k: tuple(a[None] for a in v) for k, v in res.items()}

    rep_names = ("norm_g", "conv_b", "ln_g", "ln_b", "lb_logits", "gnorm_g", "final_g")
    rep_w = (norm_g, conv_b, ln_g, ln_b, lb_logits, gnorm_g, final_g2)
    rep_m = (m_norm_g, m_conv_b, m_ln_g, m_ln_b, m_lb_logits, m_gnorm_g, m_final_g.reshape(1, d))
    rep_v = (v_norm_g, v_conv_b, v_ln_g, v_ln_b, v_lb_logits, v_gnorm_g, v_final_g.reshape(1, d))
    rep = _adamw("adamw_replicated", jnp.concatenate(rep_w, 0), jnp.concatenate(rep_m, 0),
                 jnp.concatenate(rep_v, 0), small_sum[0:8])
    rep_rows = {"norm_g": (0, 1), "conv_b": (1, 2), "ln_g": (2, 3), "ln_b": (3, 4), "lb_logits": (4, 6),
                "gnorm_g": (6, 7), "final_g": (7, 8)}
    small_out = {}
    for nm in rep_names:
        lo, hi = rep_rows[nm]
        vals = tuple(a[lo:hi] for a in rep)
        if nm == "final_g":
            vals = tuple(a.reshape(d) for a in vals)
        small_out[nm] = vals
    cw_row = 8 + n_meta
    g_meta = lax.dynamic_slice_in_dim(small_sum[8:cw_row], chip * ds, ds, axis=1)
    small_out["meta_tokens"] = _adamw("adamw_meta", meta_tokens, m_meta_tokens, v_meta_tokens, g_meta)
    g_cw = lax.dynamic_slice_in_dim(small_sum[cw_row:cw_row + HALO], chip * ds, ds, axis=1)
    pad_rows = ((0, HALO - CONV_WIDTH), (0, 0))
    cw_res = _adamw("adamw_conv_w", conv_w_pad, jnp.pad(m_conv_w[0], pad_rows),
                    jnp.pad(v_conv_w[0], pad_rows, constant_values=1.0), g_cw)
    small_out["conv_w"] = tuple(a[:CONV_WIDTH][None] for a in cw_res)

    loss = lax.psum(loss_acc[0, 0], ("x", "y", "c"))

    order = ("meta_tokens", "norm_g", "w_in", "conv_w", "conv_b", "ln_g", "ln_b", "w_conv_out", "lb_logits",
             "gnorm_g", "w_rec_out", "w_out", "final_g")
    allres = {**big, **small_out}
    outs = [loss, grad_x]
    for field in range(4):
        outs.extend(allres[nm][field] for nm in order)
    return tuple(outs)
```

```python
import numpy as np

import jax
import jax.numpy as jnp
from jax import lax
from jax.experimental import pallas as pl
from jax.experimental.pallas import tpu as pltpu

F32 = jnp.float32
BF16 = jnp.bfloat16

EPS = 1e-6
CHUNK = 64
N_LEVELS = 6
CONV_WIDTH = 31
HALO = 32
CONV_ROWS = 32
HEAD = 128
N_CHIPS = 4
VMEM_LIMIT_BYTES = 56 * 1024 * 1024

ADAM_LR = 0.001
ADAM_B1 = 0.9
ADAM_B2 = 0.999
ADAM_EPS = 1e-08
ADAM_WD = 0.01
ADAM_STEP = 10

MESH = pl.DeviceIdType.MESH
ANY = pl.BlockSpec(memory_space=pl.ANY)

NT = (((1,), (1,)), ((), ()))
TN = (((0,), (0,)), ((), ()))


def _params(**kw):
    return pltpu.CompilerParams(vmem_limit_bytes=VMEM_LIMIT_BYTES, **kw)


def _sigmoid(x):
    return jax.nn.sigmoid(x)


def _dsilu(x, s):
    return s * (1.0 + x * (1.0 - s))


def _row_tile(lp):
    for t in (320, 256, 192, 128, 64):
        if lp % t == 0:
            return t
    raise ValueError(f"unsupported padded length {lp}")


def _mm_row_tile(lp):
    for t in (832, 640, 320, 256, 192, 128, 64):
        if lp % t == 0:
            return t
    raise ValueError(f"unsupported padded length {lp}")


def _dot3(m_bf16, x):
    hi = x.astype(BF16)
    r1 = x - hi.astype(F32)
    mid = r1.astype(BF16)
    lo = (r1 - mid.astype(F32)).astype(BF16)
    return (jnp.dot(m_bf16, hi, preferred_element_type=F32)
            + jnp.dot(m_bf16, mid, preferred_element_type=F32)
            + jnp.dot(m_bf16, lo, preferred_element_type=F32))


def _col_to_row(col):
    return jnp.broadcast_to(col, (HEAD, 8)).T[0:1, :]


def _row_to_col(row):
    return jnp.broadcast_to(row, (8, HEAD)).T[:, 0:1]


def _hgrn_tables():
    t = np.arange(CHUNK)
    ltri = (t[None, :] <= t[:, None]).astype(np.float32)
    mats = [ltri]
    for lvl in range(1, N_LEVELS + 1):
        blk = CHUNK >> (lvl - 1)
        mid = (t // blk) * blk + blk // 2
        mats.append(ltri[mid - 1])
    after = (t[None, :] >= t[:, None]).astype(np.float32)
    before = (t[None, :] < t[:, None]).astype(np.float32)
    return jnp.asarray(np.concatenate(mats, 0), BF16), jnp.asarray(np.concatenate([after, before], 1), BF16)


def _rmsnorm_fwd(hres, g):
    lp, d = hres.shape
    tm = _row_tile(lp)

    def body(x_ref, g_ref, h_ref):
        x = x_ref[...]
        r = lax.rsqrt(jnp.mean(x * x, axis=-1, keepdims=True) + EPS)
        h_ref[...] = (x * r * g_ref[...]).astype(BF16)

    return pl.pallas_call(
        body, grid=(lp // tm,),
        in_specs=[pl.BlockSpec((tm, d), lambda i: (i, 0)), pl.BlockSpec((1, d), lambda i: (0, 0))],
        out_specs=pl.BlockSpec((tm, d), lambda i: (i, 0)),
        out_shape=jax.ShapeDtypeStruct((lp, d), BF16),
        name="rmsnorm_fwd", compiler_params=_params())(hres, g)


def _in_proj(h, wg):
    lp, d = h.shape
    _, _, ncol = wg.shape
    tm = _mm_row_tile(lp)
    nt = 3
    tn = ncol // nt

    def body(h_ref, w_ref, o_ref):
        o_ref[...] = jnp.dot(h_ref[...], w_ref[0], preferred_element_type=F32)

    return pl.pallas_call(
        body, grid=(N_CHIPS, nt, lp // tm),
        in_specs=[pl.BlockSpec((tm, d), lambda j, n, i: (i, 0)),
                  pl.BlockSpec((1, d, tn), lambda j, n, i: (j, 0, n))],
        out_specs=pl.BlockSpec((tm, tn), lambda j, n, i: (i, j * nt + n)),
        out_shape=jax.ShapeDtypeStruct((lp, N_CHIPS * ncol), F32),
        name="in_proj", compiler_params=_params())(h, wg)


def _conv_fwd(proj, conv_w, conv_b, ln_g, ln_b, w_conv):
    lp = proj.shape[0]
    d = conv_b.shape[1]
    tm = _row_tile(lp)
    hb = tm // HALO

    def body(ua_ref, ub_ref, z_ref, uap_ref, ubp_ref, cw_ref, cb_ref, lg_ref, lb_ref, w_ref,
             c_ref, ycin_ref, yconv_ref, aext_ref):
        i = pl.program_id(0)
        a_prev = uap_ref[...] * _sigmoid(ubp_ref[...])
        aext_ref[0:HALO, :] = jnp.where(i > 0, a_prev, 0.0)
        aext_ref[HALO:HALO + tm, :] = ua_ref[...] * _sigmoid(ub_ref[...])

        def row_block(r, carry):
            r0 = pl.multiple_of(r * CONV_ROWS, CONV_ROWS)
            blk = aext_ref[pl.ds(r0, CONV_ROWS + HALO), :]
            acc = jnp.zeros((CONV_ROWS, d), F32) + cb_ref[...]
            for j in range(CONV_WIDTH):
                acc = acc + cw_ref[j:j + 1, :] * blk[j + 2:j + 2 + CONV_ROWS, :]
            c_ref[pl.ds(r0, CONV_ROWS), :] = acc
            return carry

        lax.fori_loop(0, tm // CONV_ROWS, row_block, 0)

        c = c_ref[...]
        mu = jnp.mean(c, axis=-1, keepdims=True)
        xc = c - mu
        rstd = lax.rsqrt(jnp.mean(xc * xc, axis=-1, keepdims=True) + EPS)
        ln = xc * rstd * lg_ref[...] + lb_ref[...]
        s = ln * _sigmoid(ln)
        z = z_ref[...]
        ycin = (s * (z * _sigmoid(z))).astype(BF16)
        ycin_ref[...] = ycin
        yconv_ref[...] = jnp.dot(ycin, w_ref[...], preferred_element_type=F32)

    row = lambda p: pl.BlockSpec((tm, d), lambda i, p=p: (i, p))
    halo = lambda p: pl.BlockSpec((HALO, d), lambda i, p=p: (jnp.maximum(i * hb - 1, 0), p))
    vec = pl.BlockSpec((1, d), lambda i: (0, 0))
    return pl.pallas_call(
        body, grid=(lp // tm,),
        in_specs=[row(0), row(1), row(2), halo(0), halo(1),
                  pl.BlockSpec((HALO, d), lambda i: (0, 0)), vec, vec, vec,
                  pl.BlockSpec((d, d), lambda i: (0, 0))],
        out_specs=[pl.BlockSpec((tm, d), lambda i: (i, 0))] * 3,
        out_shape=[jax.ShapeDtypeStruct((lp, d), F32), jax.ShapeDtypeStruct((lp, d), BF16),
                   jax.ShapeDtypeStruct((lp, d), F32)],
        scratch_shapes=[pltpu.VMEM((HALO + tm, d), F32)],
        name="conv_fwd", compiler_params=_params())(
            proj, proj, proj, proj, proj, conv_w, conv_b, ln_g, ln_b, w_conv)


def _lower_bound(lbl_ref):
    l0 = lbl_ref[0:1, :]
    l1 = lbl_ref[1:2, :]
    m = jnp.maximum(l0, l1)
    e0 = jnp.exp(l0 - m)
    e1 = jnp.exp(l1 - m)
    p0 = e0 / (e0 + e1)
    return p0, p0 * (e1 / (e0 + e1))


def _level_masks():
    rid = lax.broadcasted_iota(jnp.int32, (CHUNK, 1), 0)
    r2 = lax.broadcasted_iota(jnp.int32, (CHUNK, CHUNK), 0)
    c2 = lax.broadcasted_iota(jnp.int32, (CHUNK, CHUNK), 1)
    out = []
    for lvl in range(1, N_LEVELS + 1):
        blk = CHUNK >> (lvl - 1)
        sh = blk.bit_length() - 1
        upper = (rid & (blk - 1)) >= (blk // 2)
        same = (r2 >> sh) == (c2 >> sh)
        out.append((upper, same))
    return out


def _gates(qr, fr, lb, valid):
    sq = _sigmoid(qr)
    q = qr * sq
    sf = _sigmoid(fr)
    f = lb + (1.0 - lb) * sf
    g = jnp.where(valid, jnp.log(f), 0.0)
    k = jnp.where(valid, 1.0 - f, 0.0)
    return q, sq, f, sf, g, k


def _level_factors(b, r, upper):
    eq = jnp.where(upper, jnp.exp(jnp.minimum(b - r, 0.0)), 0.0)
    ek = jnp.where(upper, 0.0, jnp.exp(jnp.minimum(r - b, 0.0)))
    return eq, ek


def _hgrn_fwd(proj, lb_logits, n_pad):
    lp = proj.shape[0]
    d = lb_logits.shape[1]
    n_heads = d // HEAD
    nc = lp // CHUNK
    tab, _ = _hgrn_tables()
    n_tab = tab.shape[0]

    def body(qr_ref, fr_ref, ir_ref, lbl_ref, tab_ref, o_ref, sall_ref, s_ref, t_ref):
        n = pl.program_id(0)

        @pl.when(n == 0)
        def _():
            s_ref[...] = jnp.zeros_like(s_ref)

        sall_ref[0] = s_ref[...]
        lb_all, _ = _lower_bound(lbl_ref)
        rid = lax.broadcasted_iota(jnp.int32, (CHUNK, 1), 0)
        valid = jnp.logical_or(n > 0, rid >= n_pad)
        f_all = lb_all + (1.0 - lb_all) * _sigmoid(fr_ref[...])
        t_ref[...] = _dot3(tab_ref[...], jnp.where(valid, jnp.log(f_all), 0.0))
        masks = _level_masks()

        def head(h, carry):
            off = pl.multiple_of(h * HEAD, HEAD)
            hs = pl.ds(off, HEAD)
            lb = _lower_bound_slice(lbl_ref, hs)
            q, _, _, _, _, k = _gates(qr_ref[:, hs], fr_ref[:, hs], lb, valid)
            v = ir_ref[:, hs]
            b = t_ref[0:CHUNK, hs]
            s0 = s_ref[hs, :]
            o = jnp.dot((q * jnp.exp(b)).astype(BF16), s0.astype(BF16), preferred_element_type=F32)
            o = o + jnp.sum(q * k, axis=-1, keepdims=True) * v
            a = jnp.zeros((CHUNK, CHUNK), F32)
            for lvl in range(1, N_LEVELS + 1):
                upper, same = masks[lvl - 1]
                eq, ek = _level_factors(b, t_ref[CHUNK * lvl:CHUNK * (lvl + 1), hs], upper)
                p = lax.dot_general((q * eq).astype(BF16), (k * ek).astype(BF16), NT, preferred_element_type=F32)
                a = a + jnp.where(same, p, 0.0)
            vb = v.astype(BF16)
            o_ref[:, hs] = o + jnp.dot(a.astype(BF16), vb, preferred_element_type=F32)
            b_last = t_ref[CHUNK - 1:CHUNK, hs]
            khat = (k * jnp.exp(b_last - b)).astype(BF16)
            s_ref[hs, :] = _row_to_col(jnp.exp(b_last)) * s0 + lax.dot_general(khat, vb, TN, preferred_element_type=F32)
            return carry

        lax.fori_loop(0, n_heads, head, 0)

    piece = lambda p: pl.BlockSpec((CHUNK, d), lambda n, p=p: (n, p))
    return pl.pallas_call(
        body, grid=(nc,),
        in_specs=[piece(3), piece(4), piece(5), pl.BlockSpec((2, d), lambda n: (0, 0)),
                  pl.BlockSpec((n_tab, CHUNK), lambda n: (0, 0))],
        out_specs=[pl.BlockSpec((CHUNK, d), lambda n: (n, 0)), pl.BlockSpec((1, d, HEAD), lambda n: (n, 0, 0))],
        out_shape=[jax.ShapeDtypeStruct((lp, d), F32), jax.ShapeDtypeStruct((nc, d, HEAD), F32)],
        scratch_shapes=[pltpu.VMEM((d, HEAD), F32), pltpu.VMEM((n_tab, d), F32)],
        name="hgrn_fwd", compiler_params=_params())(proj, proj, proj, lb_logits, tab)


def _lower_bound_slice(lbl_ref, hs):
    l0 = lbl_ref[0:1, hs]
    l1 = lbl_ref[1:2, hs]
    m = jnp.maximum(l0, l1)
    e0 = jnp.exp(l0 - m)
    e1 = jnp.exp(l1 - m)
    return e0 / (e0 + e1)


def _tail_fwd(o, proj, y_conv, hres, target, gnorm_g, final_g, w_rec, w_out):
    lp, d = o.shape
    n_heads = d // HEAD
    tm = _row_tile(lp)

    def body(o_ref, gr_ref, mc_ref, mr_ref, yc_ref, x_ref, t_ref, gn_ref, fg_ref, wr_ref, wo_ref,
             yrin_ref, mg_ref, yrec_ref, dout_ref, loss_ref, dfg_ref):
        i = pl.program_id(0)

        @pl.when(i == 0)
        def _():
            loss_ref[...] = jnp.zeros_like(loss_ref)
            dfg_ref[...] = jnp.zeros_like(dfg_ref)

        for h in range(n_heads):
            hs = slice(h * HEAD, (h + 1) * HEAD)
            oh = o_ref[:, hs]
            on = oh * lax.rsqrt(jnp.mean(oh * oh, axis=-1, keepdims=True) + EPS) * gn_ref[:, hs]
            gr = gr_ref[:, hs]
            yrin_ref[:, hs] = (on * (gr * _sigmoid(gr))).astype(BF16)
        yrec = jnp.dot(yrin_ref[...], wr_ref[...], preferred_element_type=F32)
        yrec_ref[...] = yrec
        merged = (_sigmoid(mc_ref[...]) * yc_ref[...] + _sigmoid(mr_ref[...]) * yrec).astype(BF16)
        mg_ref[...] = merged
        out = x_ref[...] + jnp.dot(merged, wo_ref[...], preferred_element_type=F32)
        r = lax.rsqrt(jnp.mean(out * out, axis=-1, keepdims=True) + EPS)
        yhat = out * r
        fg = fg_ref[...]
        rid = lax.broadcasted_iota(jnp.int32, (tm, 1), 0) + i * tm
        err = jnp.where(rid >= CHUNK, yhat * fg - t_ref[...], 0.0)
        loss_ref[...] += 0.5 * jnp.sum(err * err) / d
        dy = err / d
        dfg_ref[...] += jnp.sum(dy * yhat, axis=0, keepdims=True)
        dyh = dy * fg
        dout_ref[...] = r * (dyh - yhat * jnp.mean(dyh * yhat, axis=-1, keepdims=True))

    row = lambda p: pl.BlockSpec((tm, d), lambda i, p=p: (i, p))
    vec = pl.BlockSpec((1, d), lambda i: (0, 0))
    mat = pl.BlockSpec((d, d), lambda i: (0, 0))
    return pl.pallas_call(
        body, grid=(lp // tm,),
        in_specs=[row(0), row(6), row(7), row(8), row(0), row(0), row(0), vec, vec, mat, mat],
        out_specs=[row(0), row(0), row(0), row(0), pl.BlockSpec((8, 128), lambda i: (0, 0)), vec],
        out_shape=[jax.ShapeDtypeStruct((lp, d), BF16), jax.ShapeDtypeStruct((lp, d), BF16),
                   jax.ShapeDtypeStruct((lp, d), F32), jax.ShapeDtypeStruct((lp, d), F32),
                   jax.ShapeDtypeStruct((8, 128), F32), jax.ShapeDtypeStruct((1, d), F32)],
        name="tail_fwd", compiler_params=_params())(
            o, proj, proj, proj, y_conv, hres, target, gnorm_g, final_g, w_rec, w_out)


def _tail_bwd(dout, proj, y_conv, y_rec, o, c, w_out, w_rec, w_conv, ln_g, ln_b, gnorm_g):
    lp, d = dout.shape
    n_heads = d // HEAD
    tm = _row_tile(lp)

    def body(dout_ref, mc_ref, mr_ref, z_ref, gr_ref, yc_ref, yrec_ref, o_ref, c_ref,
             wo_ref, wr_ref, wc_ref, lg_ref, lb_ref, gn_ref,
             dyc_ref, dyr_ref, dz_ref, dgr_ref, dmc_ref, dmr_ref, do_ref, dc_ref,
             dgn_ref, dlg_ref, dlb_ref, dyrin_ref):
        i = pl.program_id(0)

        @pl.when(i == 0)
        def _():
            dgn_ref[...] = jnp.zeros_like(dgn_ref)
            dlg_ref[...] = jnp.zeros_like(dlg_ref)
            dlb_ref[...] = jnp.zeros_like(dlb_ref)

        dmerged = lax.dot_general(dout_ref[...].astype(BF16), wo_ref[...], NT, preferred_element_type=F32)
        smc = _sigmoid(mc_ref[...])
        smr = _sigmoid(mr_ref[...])
        dyc = (dmerged * smc).astype(BF16)
        dyr = (dmerged * smr).astype(BF16)
        dyc_ref[...] = dyc
        dyr_ref[...] = dyr
        dmc_ref[...] = (dmerged * yc_ref[...] * smc * (1.0 - smc)).astype(BF16)
        dmr_ref[...] = (dmerged * yrec_ref[...] * smr * (1.0 - smr)).astype(BF16)

        dyrin_ref[...] = lax.dot_general(dyr, wr_ref[...], NT, preferred_element_type=F32)
        for h in range(n_heads):
            hs = slice(h * HEAD, (h + 1) * HEAD)
            oh = o_ref[:, hs]
            rstd = lax.rsqrt(jnp.mean(oh * oh, axis=-1, keepdims=True) + EPS)
            ohat = oh * rstd
            gn = gn_ref[:, hs]
            gr = gr_ref[:, hs]
            sg = _sigmoid(gr)
            dyrin = dyrin_ref[:, hs]
            don = dyrin * (gr * sg)
            dgr_ref[:, hs] = (dyrin * (ohat * gn) * _dsilu(gr, sg)).astype(BF16)
            dgn_ref[:, hs] += jnp.sum(don * ohat, axis=0, keepdims=True)
            doh = don * gn
            do_ref[:, hs] = rstd * (doh - ohat * jnp.mean(doh * ohat, axis=-1, keepdims=True))

        dycin = lax.dot_general(dyc, wc_ref[...], NT, preferred_element_type=F32)
        c = c_ref[...]
        mu = jnp.mean(c, axis=-1, keepdims=True)
        xc = c - mu
        rstd = lax.rsqrt(jnp.mean(xc * xc, axis=-1, keepdims=True) + EPS)
        nrm = xc * rstd
        lg = lg_ref[...]
        ln = nrm * lg + lb_ref[...]
        sl = _sigmoid(ln)
        z = z_ref[...]
        sz = _sigmoid(z)
        dz_ref[...] = (dycin * (ln * sl) * _dsilu(z, sz)).astype(BF16)
        dln = dycin * (z * sz) * _dsilu(ln, sl)
        dlg_ref[...] += jnp.sum(dln * nrm, axis=0, keepdims=True)
        dlb_ref[...] += jnp.sum(dln, axis=0, keepdims=True)
        dn = dln * lg
        dc_ref[...] = rstd * (dn - jnp.mean(dn, axis=-1, keepdims=True)
                              - nrm * jnp.mean(dn * nrm, axis=-1, keepdims=True))

    row = lambda p: pl.BlockSpec((tm, d), lambda i, p=p: (i, p))
    vec = pl.BlockSpec((1, d), lambda i: (0, 0))
    mat = pl.BlockSpec((d, d), lambda i: (0, 0))
    act_bf = jax.ShapeDtypeStruct((lp, d), BF16)
    act_f32 = jax.ShapeDtypeStruct((lp, d), F32)
    vec_f32 = jax.ShapeDtypeStruct((1, d), F32)
    return pl.pallas_call(
        body, grid=(lp // tm,),
        in_specs=[row(0), row(7), row(8), row(2), row(6), row(0), row(0), row(0), row(0),
                  mat, mat, mat, vec, vec, vec],
        out_specs=[row(0)] * 8 + [vec] * 3,
        out_shape=[act_bf] * 6 + [act_f32] * 2 + [vec_f32] * 3,
        scratch_shapes=[pltpu.VMEM((tm, d), F32)],
        name="tail_bwd", compiler_params=_params())(
            dout, proj, proj, proj, proj, y_conv, y_rec, o, c, w_out, w_rec, w_conv, ln_g, ln_b, gnorm_g)


def _hgrn_bwd(proj, do, s_all, lb_logits, n_pad):
    lp, d = do.shape
    n_heads = d // HEAD
    nc = lp // CHUNK
    tab, utri = _hgrn_tables()
    n_tab = tab.shape[0]

    def body(qr_ref, fr_ref, ir_ref, do_ref, s0_ref, lbl_ref, tab_ref, ut_ref,
             dq_ref, df_ref, di_ref, dlbl_ref, ds_ref, t_ref, dlb_ref):
        n = pl.program_id(0)
        chunk = nc - 1 - n

        @pl.when(n == 0)
        def _():
            ds_ref[...] = jnp.zeros_like(ds_ref)
            dlb_ref[...] = jnp.zeros_like(dlb_ref)

        lb_all, pp = _lower_bound(lbl_ref)
        rid = lax.broadcasted_iota(jnp.int32, (CHUNK, 1), 0)
        valid = jnp.logical_or(chunk > 0, rid >= n_pad)
        f_all = lb_all + (1.0 - lb_all) * _sigmoid(fr_ref[...])
        t_ref[...] = _dot3(tab_ref[...], jnp.where(valid, jnp.log(f_all), 0.0))
        masks = _level_masks()
        ut = ut_ref[...]

        def head(h, carry):
            off = pl.multiple_of(h * HEAD, HEAD)
            hs = pl.ds(off, HEAD)
            lb = _lower_bound_slice(lbl_ref, hs)
            qr = qr_ref[:, hs]
            q, sq, f, sf, _, k = _gates(qr, fr_ref[:, hs], lb, valid)
            v = ir_ref[:, hs]
            do_h = do_ref[:, hs]
            b = t_ref[0:CHUNK, hs]
            b_last = t_ref[CHUNK - 1:CHUNK, hs]
            s0 = s0_ref[0, hs, :]
            ds1 = ds_ref[hs, :]
            eb = jnp.exp(b)
            ekl = jnp.exp(b_last - b)
            do_bf = do_h.astype(BF16)
            v_bf = v.astype(BF16)
            ds1_bf = ds1.astype(BF16)

            da = lax.dot_general(do_bf, v_bf, NT, preferred_element_type=F32)
            da_diag = jnp.sum(do_h * v, axis=-1, keepdims=True)
            a = jnp.zeros((CHUNK, CHUNK), F32)
            dq_x = eb * lax.dot_general(do_bf, s0.astype(BF16), NT, preferred_element_type=F32)
            dk_x = ekl * lax.dot_general(v_bf, ds1_bf, NT, preferred_element_type=F32)
            x_after = q * dq_x
            x_before = k * dk_x
            for lvl in range(1, N_LEVELS + 1):
                upper, same = masks[lvl - 1]
                eq, ek = _level_factors(b, t_ref[CHUNK * lvl:CHUNK * (lvl + 1), hs], upper)
                qt = (q * eq).astype(BF16)
                kt = (k * ek).astype(BF16)
                p = lax.dot_general(qt, kt, NT, preferred_element_type=F32)
                a = a + jnp.where(same, p, 0.0)
                dam = jnp.where(same, da, 0.0).astype(BF16)
                dqt = jnp.dot(dam, kt, preferred_element_type=F32)
                dkt = lax.dot_general(dam, qt, TN, preferred_element_type=F32)
                dq_x = dq_x + eq * dqt
                dk_x = dk_x + ek * dkt
                x_after = x_after + (qt.astype(F32) * dqt - kt.astype(F32) * dkt)

            dv = (lax.dot_general(a.astype(BF16), do_bf, TN, preferred_element_type=F32)
                  + jnp.sum(q * k, axis=-1, keepdims=True) * do_h
                  + jnp.dot((k * ekl).astype(BF16), ds1_bf, preferred_element_type=F32))
            di_ref[:, hs] = dv.astype(BF16)

            carried = jnp.exp(b_last) * _col_to_row(jnp.sum(s0 * ds1, axis=-1, keepdims=True))
            dg = _dot3(ut, jnp.concatenate([x_after, x_before], axis=0)) + carried
            dq = dq_x + da_diag * k
            dk = dk_x + da_diag * q
            dq_ref[:, hs] = (dq * _dsilu(qr, sq)).astype(BF16)
            df = jnp.where(valid, dg / f - dk, 0.0)
            df_ref[:, hs] = (df * (1.0 - lb) * sf * (1.0 - sf)).astype(BF16)
            dlb_ref[:, hs] += jnp.sum(df * (1.0 - sf), axis=0, keepdims=True)

            ds_ref[hs, :] = (_row_to_col(jnp.exp(b_last)) * ds1
                             + lax.dot_general((q * eb).astype(BF16), do_bf, TN, preferred_element_type=F32))
            return carry

        lax.fori_loop(0, n_heads, head, 0)

        @pl.when(n == nc - 1)
        def _():
            dl0 = dlb_ref[...] * pp
            dlbl_ref[0:1, :] = dl0
            dlbl_ref[1:2, :] = -dl0

    piece = lambda p: pl.BlockSpec((CHUNK, d), lambda n, p=p: (nc - 1 - n, p))
    return pl.pallas_call(
        body, grid=(nc,),
        in_specs=[piece(3), piece(4), piece(5), piece(0),
                  pl.BlockSpec((1, d, HEAD), lambda n: (nc - 1 - n, 0, 0)),
                  pl.BlockSpec((2, d), lambda n: (0, 0)),
                  pl.BlockSpec((n_tab, CHUNK), lambda n: (0, 0)),
                  pl.BlockSpec((CHUNK, 2 * CHUNK), lambda n: (0, 0))],
        out_specs=[piece(0), piece(0), piece(0), pl.BlockSpec((2, d), lambda n: (0, 0))],
        out_shape=[jax.ShapeDtypeStruct((lp, d), BF16)] * 3 + [jax.ShapeDtypeStruct((2, d), F32)],
        scratch_shapes=[pltpu.VMEM((d, HEAD), F32), pltpu.VMEM((n_tab, d), F32), pltpu.VMEM((1, d), F32)],
        name="hgrn_bwd", compiler_params=_params())(proj, proj, proj, do, s_all, lb_logits, tab, utri)


def _conv_bwd(dc, proj, conv_w):
    lp, d = dc.shape
    tm = _row_tile(lp)
    hb = tm // HALO
    n_tiles = lp // tm
    last_halo = lp // HALO - 1

    def body(dc_ref, dcn_ref, ua_ref, ub_ref, uap_ref, ubp_ref, cw_ref,
             dua_ref, dub_ref, dcw_ref, dcb_ref, aext_ref, dcext_ref, da_ref):
        i = pl.program_id(0)

        @pl.when(i == 0)
        def _():
            dcw_ref[...] = jnp.zeros_like(dcw_ref)
            dcb_ref[...] = jnp.zeros_like(dcb_ref)

        ua = ua_ref[...]
        sb = _sigmoid(ub_ref[...])
        a_prev = uap_ref[...] * _sigmoid(ubp_ref[...])
        aext_ref[0:HALO, :] = jnp.where(i > 0, a_prev, 0.0)
        aext_ref[HALO:HALO + tm, :] = ua * sb
        dcext_ref[0:tm, :] = dc_ref[...]
        dcext_ref[tm:tm + HALO, :] = jnp.where(i < n_tiles - 1, dcn_ref[...], 0.0)
        dcb_ref[...] += jnp.sum(dc_ref[...], axis=0, keepdims=True)

        def row_block(r, carry):
            r0 = pl.multiple_of(r * CONV_ROWS, CONV_ROWS)
            dblk = dcext_ref[pl.ds(r0, CONV_ROWS + HALO), :]
            ablk = aext_ref[pl.ds(r0, CONV_ROWS + HALO), :]
            dcur = dblk[0:CONV_ROWS, :]
            acc = jnp.zeros((CONV_ROWS, d), F32)
            for j in range(CONV_WIDTH):
                acc = acc + cw_ref[j:j + 1, :] * dblk[30 - j:30 - j + CONV_ROWS, :]
                dcw_ref[j:j + 1, :] += jnp.sum(dcur * ablk[j + 2:j + 2 + CONV_ROWS, :], axis=0, keepdims=True)
            da_ref[pl.ds(r0, CONV_ROWS), :] = acc
            return carry

        lax.fori_loop(0, tm // CONV_ROWS, row_block, 0)

        da = da_ref[...]
        dua_ref[...] = (da * sb).astype(BF16)
        dub_ref[...] = (da * ua * sb * (1.0 - sb)).astype(BF16)

    row = lambda p: pl.BlockSpec((tm, d), lambda i, p=p: (i, p))
    prev = lambda p: pl.BlockSpec((HALO, d), lambda i, p=p: (jnp.maximum(i * hb - 1, 0), p))
    nxt = pl.BlockSpec((HALO, d), lambda i: (jnp.minimum((i + 1) * hb, last_halo), 0))
    return pl.pallas_call(
        body, grid=(n_tiles,),
        in_specs=[row(0), nxt, row(0), row(1), prev(0), prev(1), pl.BlockSpec((HALO, d), lambda i: (0, 0))],
        out_specs=[row(0), row(0), pl.BlockSpec((HALO, d), lambda i: (0, 0)), pl.BlockSpec((1, d), lambda i: (0, 0))],
        out_shape=[jax.ShapeDtypeStruct((lp, d), BF16), jax.ShapeDtypeStruct((lp, d), BF16),
                   jax.ShapeDtypeStruct((HALO, d), F32), jax.ShapeDtypeStruct((1, d), F32)],
        scratch_shapes=[pltpu.VMEM((HALO + tm, d), F32), pltpu.VMEM((tm + HALO, d), F32), pltpu.VMEM((tm, d), F32)],
        name="conv_bwd", compiler_params=_params())(dc, dc, proj, proj, proj, proj, conv_w)


def _weight_grad(xs, dy, name, blocked):
    lp, dx = xs.shape
    n = dy.shape[1]
    tk = _mm_row_tile(lp)
    if blocked:
        ncol = n // N_CHIPS
        nt = 3
        tn = ncol // nt
        grid = (N_CHIPS * nt, lp // tk)
        out_spec = pl.BlockSpec((1, dx, tn), lambda c, k: (c // nt, 0, c % nt))
        out_shape = jax.ShapeDtypeStruct((N_CHIPS, dx, ncol), F32)
    else:
        tn = n // 2
        grid = (2, lp // tk)
        out_spec = pl.BlockSpec((dx, tn), lambda c, k: (0, c))
        out_shape = jax.ShapeDtypeStruct((dx, n), F32)

    def body(xs_ref, dy_ref, o_ref):
        @pl.when(pl.program_id(1) == 0)
        def _():
            o_ref[...] = jnp.zeros_like(o_ref)

        p = lax.dot_general(xs_ref[...], dy_ref[...], TN, preferred_element_type=F32)
        if blocked:
            o_ref[0] += p
        else:
            o_ref[...] += p

    return pl.pallas_call(
        body, grid=grid,
        in_specs=[pl.BlockSpec((tk, dx), lambda c, k: (k, 0)), pl.BlockSpec((tk, tn), lambda c, k: (k, c))],
        out_specs=out_spec, out_shape=out_shape,
        name=name, compiler_params=_params())(xs, dy)


def _in_proj_bwd(dproj, wg, hres, norm_g, dout):
    lp, d = hres.shape
    _, _, ncol = wg.shape
    tm = _mm_row_tile(lp)
    nt = 3
    tn = ncol // nt
    nk = N_CHIPS * nt

    def body(dp_ref, w_ref, x_ref, g_ref, dout_ref, dx_ref, dg_ref, acc_ref):
        i = pl.program_id(0)
        kk = pl.program_id(1)

        @pl.when(jnp.logical_and(i == 0, kk == 0))
        def _():
            dg_ref[...] = jnp.zeros_like(dg_ref)

        @pl.when(kk == 0)
        def _():
            acc_ref[...] = jnp.zeros_like(acc_ref)

        acc_ref[...] += lax.dot_general(dp_ref[...], w_ref[0], NT, preferred_element_type=F32)

        @pl.when(kk == nk - 1)
        def _():
            x = x_ref[...]
            r = lax.rsqrt(jnp.mean(x * x, axis=-1, keepdims=True) + EPS)
            xhat = x * r
            dh = acc_ref[...]
            dg_ref[...] += jnp.sum(dh * xhat, axis=0, keepdims=True)
            dxh = dh * g_ref[...]
            dx_ref[...] = dout_ref[...] + r * (dxh - xhat * jnp.mean(dxh * xhat, axis=-1, keepdims=True))

    return pl.pallas_call(
        body, grid=(lp // tm, nk),
        in_specs=[pl.BlockSpec((tm, tn), lambda i, k: (i, k)),
                  pl.BlockSpec((1, d, tn), lambda i, k: (k // nt, 0, k % nt)),
                  pl.BlockSpec((tm, d), lambda i, k: (i, 0)),
                  pl.BlockSpec((1, d), lambda i, k: (0, 0)),
                  pl.BlockSpec((tm, d), lambda i, k: (i, 0))],
        out_specs=[pl.BlockSpec((tm, d), lambda i, k: (i, 0)), pl.BlockSpec((1, d), lambda i, k: (0, 0))],
        out_shape=[jax.ShapeDtypeStruct((lp, d), F32), jax.ShapeDtypeStruct((1, d), F32)],
        scratch_shapes=[pltpu.VMEM((tm, d), F32)],
        name="in_proj_bwd", compiler_params=_params())(dproj, wg, hres, norm_g, dout)


def _adamw_math(w, g, m, v):
    m = ADAM_B1 * m + (1.0 - ADAM_B1) * g
    v = ADAM_B2 * v + (1.0 - ADAM_B2) * (g * g)
    m_hat = m / (1.0 - ADAM_B1 ** ADAM_STEP)
    v_hat = v / (1.0 - ADAM_B2 ** ADAM_STEP)
    delta = -ADAM_LR * (m_hat / (jnp.sqrt(v_hat) + ADAM_EPS) + ADAM_WD * w)
    return delta, m, v


def _elementwise_rows(shape):
    r, c = shape
    for t in (256, 128, 64, 32, 16, 8):
        if r % t == 0 and r > t and t * c * 4 <= 1024 * 1024:
            return t
    return r


def _adamw(name, w, m, v, *g_parts):
    shape = w.shape
    tr = _elementwise_rows(shape)
    n_g = len(g_parts)

    def body(*refs):
        w_ref, m_ref, v_ref = refs[:3]
        g_refs = refs[3:3 + n_g]
        g_out, d_out, m_out, v_out = refs[3 + n_g:]
        g = g_refs[0][...]
        for gr in g_refs[1:]:
            g = g + gr[...]
        delta, m_new, v_new = _adamw_math(w_ref[...], g, m_ref[...], v_ref[...])
        g_out[...] = g
        d_out[...] = delta
        m_out[...] = m_new
        v_out[...] = v_new

    spec = pl.BlockSpec((tr, shape[1]), lambda i: (i, 0))
    return pl.pallas_call(
        body, grid=(shape[0] // tr,),
        in_specs=[spec] * (3 + n_g), out_specs=[spec] * 4,
        out_shape=[jax.ShapeDtypeStruct(shape, F32)] * 4,
        name=name, compiler_params=_params())(w, m, v, *g_parts)


def _chip_half_sum(name, g, recv, core):
    _, _, hr, cols = g.shape
    tr = _elementwise_rows((hr, cols))

    def body(core_ref, g_ref, r_ref, o_ref, ob_ref):
        s = g_ref[0, 0] + r_ref[0]
        o_ref[0] = s
        ob_ref[0] = s.astype(BF16)

    blk = pl.BlockSpec((1, tr, cols), lambda j, i, core_ref: (j, i, 0))
    grid_spec = pltpu.PrefetchScalarGridSpec(
        num_scalar_prefetch=1, grid=(N_CHIPS, hr // tr),
        in_specs=[pl.BlockSpec((1, 1, tr, cols), lambda j, i, core_ref: (j, core_ref[0], i, 0)), blk],
        out_specs=[blk, blk])
    return pl.pallas_call(
        body, grid_spec=grid_spec,
        out_shape=[jax.ShapeDtypeStruct((N_CHIPS, hr, cols), F32), jax.ShapeDtypeStruct((N_CHIPS, hr, cols), BF16)],
        name=name, compiler_params=_params())(core, g, recv)


def _block_half_total(name, chip_sums, recv, chip):
    _, hr, cols = chip_sums.shape
    tr = _elementwise_rows((hr, cols))

    def body(chip_ref, p_ref, r_ref, o_ref):
        s = p_ref[0]
        for k in range(3):
            s = s + r_ref[k].astype(F32)
        o_ref[...] = s

    grid_spec = pltpu.PrefetchScalarGridSpec(
        num_scalar_prefetch=1, grid=(hr // tr,),
        in_specs=[pl.BlockSpec((1, tr, cols), lambda i, chip_ref: (chip_ref[0], i, 0)),
                  pl.BlockSpec((3, tr, cols), lambda i, chip_ref: (0, i, 0))],
        out_specs=pl.BlockSpec((tr, cols), lambda i, chip_ref: (i, 0)))
    return pl.pallas_call(
        body, grid_spec=grid_spec, out_shape=jax.ShapeDtypeStruct((hr, cols), F32),
        name=name, compiler_params=_params())(chip, chip_sums, recv)


def _sum_slots(name, slots):
    k, r, c = slots.shape

    def body(s_ref, o_ref):
        s = s_ref[0]
        for j in range(1, k):
            s = s + s_ref[j]
        o_ref[...] = s

    return pl.pallas_call(body, out_shape=jax.ShapeDtypeStruct((r, c), F32), name=name,
                          compiler_params=_params())(slots)


def _mesh_pos():
    return lax.axis_index("x"), lax.axis_index("y"), lax.axis_index("c")


def _other_chips(x, y):
    return [(1 - x, y), (x, 1 - y), (1 - x, 1 - y)]


def _gather_weights(shards):
    n = len(shards)
    half = [s.shape[0] // 2 for s in shards]

    def body(*refs):
        srcs = refs[:n]
        dsts = refs[n:2 * n]
        ici_send, ici_recv, d2d_send, d2d_recv, local_sems = refs[2 * n:]
        x, y, c = _mesh_pos()
        me = 2 * x + y
        chips = _other_chips(x, y)

        def rows(a, core):
            return pl.ds(core * half[a], half[a])

        def over_ici(a, k, block):
            px, py = chips[k]
            return pltpu.make_async_remote_copy(
                src_ref=srcs[a].at[rows(a, c)], dst_ref=dsts[a].at[block, rows(a, c)],
                send_sem=ici_send.at[a, k], recv_sem=ici_recv.at[a, k],
                device_id=(px, py, c), device_id_type=MESH)

        def over_d2d(a, k, core):
            px, py = chips[k]
            part = dsts[a].at[2 * px + py, rows(a, core)]
            return pltpu.make_async_remote_copy(
                src_ref=part, dst_ref=part, send_sem=d2d_send.at[a, k], recv_sem=d2d_recv.at[a, k],
                device_id=(x, y, 1 - c), device_id_type=MESH)

        local = [pltpu.make_async_copy(srcs[a], dsts[a].at[me], local_sems.at[a]) for a in range(n)]
        for cp in local:
            cp.start()
        for a in range(n):
            for k in range(3):
                over_ici(a, k, me).start()
        for a in range(n):
            for k, (px, py) in enumerate(chips):
                over_ici(a, k, 2 * px + py).wait_recv()
                over_d2d(a, k, c).start()
        for a in range(n):
            for k in range(3):
                over_d2d(a, k, 1 - c).wait_recv()
        for a in range(n):
            for k in range(3):
                over_ici(a, k, me).wait_send()
                over_d2d(a, k, c).wait_send()
        for cp in local:
            cp.wait()

    return pl.pallas_call(
        body, in_specs=[ANY] * n, out_specs=[ANY] * n,
        out_shape=[jax.ShapeDtypeStruct((N_CHIPS,) + s.shape, s.dtype) for s in shards],
        scratch_shapes=[pltpu.SemaphoreType.DMA((n, 3))] * 4 + [pltpu.SemaphoreType.DMA((n,))],
        name="gather_weights")(*shards)


def _send_other_halves(grads):
    n = len(grads)

    def body(*refs):
        srcs = refs[:n]
        dsts = refs[n:2 * n]
        send_sems, recv_sems = refs[2 * n:]
        x, y, c = _mesh_pos()
        copies = [pltpu.make_async_remote_copy(
            src_ref=srcs[a].at[j, 1 - c], dst_ref=dsts[a].at[j], send_sem=send_sems.at[a, j],
            recv_sem=recv_sems.at[a, j], device_id=(x, y, 1 - c), device_id_type=MESH)
            for a in range(n) for j in range(N_CHIPS)]
        for cp in copies:
            cp.start()
        for cp in copies:
            cp.wait()

    return pl.pallas_call(
        body, in_specs=[ANY] * n, out_specs=[ANY] * n,
        out_shape=[jax.ShapeDtypeStruct((N_CHIPS,) + g.shape[2:], F32) for g in grads],
        scratch_shapes=[pltpu.SemaphoreType.DMA((n, N_CHIPS))] * 2,
        name="send_other_halves")(*grads)


def _exchange_grads(blocked, small):
    n = len(blocked)

    def body(*refs):
        srcs = refs[:n]
        small_src = refs[n]
        dsts = refs[n + 1:2 * n + 1]
        small_dst = refs[2 * n + 1]
        send_sems, recv_sems, ssend_sems, srecv_sems, local_sem = refs[2 * n + 2:]
        x, y, c = _mesh_pos()
        chips = _other_chips(x, y)
        my_idx = 4 * x + 2 * y + c
        local = pltpu.make_async_copy(small_src, small_dst.at[my_idx], local_sem)
        local.start()
        others = []
        for r in range(1, 8):
            px = 1 - x if r & 4 else x
            py = 1 - y if r & 2 else y
            pc = 1 - c if r & 1 else c
            others.append((px, py, pc))
        for r, peer in enumerate(others):
            pltpu.make_async_remote_copy(
                src_ref=small_src, dst_ref=small_dst.at[my_idx], send_sem=ssend_sems.at[r],
                recv_sem=srecv_sems.at[r], device_id=peer, device_id_type=MESH).start()
        for a in range(n):
            for k, (px, py) in enumerate(chips):
                pltpu.make_async_remote_copy(
                    src_ref=srcs[a].at[2 * px + py], dst_ref=dsts[a].at[k], send_sem=send_sems.at[a, k],
                    recv_sem=recv_sems.at[a, k], device_id=(px, py, c), device_id_type=MESH).start()
        for r, (px, py, pc) in enumerate(others):
            pltpu.make_async_remote_copy(
                src_ref=small_src, dst_ref=small_dst.at[4 * px + 2 * py + pc], send_sem=ssend_sems.at[r],
                recv_sem=srecv_sems.at[r], device_id=(px, py, pc), device_id_type=MESH).wait()
        for a in range(n):
            for k, (px, py) in enumerate(chips):
                pltpu.make_async_remote_copy(
                    src_ref=srcs[a].at[2 * px + py], dst_ref=dsts[a].at[k], send_sem=send_sems.at[a, k],
                    recv_sem=recv_sems.at[a, k], device_id=(px, py, c), device_id_type=MESH).wait()
        local.wait()

    out_shape = [jax.ShapeDtypeStruct((3,) + b.shape[1:], b.dtype) for b in blocked]
    out_shape.append(jax.ShapeDtypeStruct((8,) + small.shape, F32))
    return pl.pallas_call(
        body, in_specs=[ANY] * (n + 1), out_specs=[ANY] * (n + 1), out_shape=out_shape,
        scratch_shapes=[pltpu.SemaphoreType.DMA((n, 3)), pltpu.SemaphoreType.DMA((n, 3)),
                        pltpu.SemaphoreType.DMA((7,)), pltpu.SemaphoreType.DMA((7,)),
                        pltpu.SemaphoreType.DMA],
        name="exchange_grads")(*blocked, small)


def _join_halves(halves):
    n = len(halves)

    def body(*refs):
        srcs = refs[:n]
        dsts = refs[n:2 * n]
        send_sems, recv_sems, local_sems = refs[2 * n:]
        x, y, c = _mesh_pos()
        local = [pltpu.make_async_copy(srcs[a], dsts[a].at[c], local_sems.at[a]) for a in range(n)]
        for cp in local:
            cp.start()
        for a in range(n):
            pltpu.make_async_remote_copy(
                src_ref=srcs[a], dst_ref=dsts[a].at[c], send_sem=send_sems.at[a], recv_sem=recv_sems.at[a],
                device_id=(x, y, 1 - c), device_id_type=MESH).start()
        for a in range(n):
            pltpu.make_async_remote_copy(
                src_ref=srcs[a], dst_ref=dsts[a].at[1 - c], send_sem=send_sems.at[a], recv_sem=recv_sems.at[a],
                device_id=(x, y, 1 - c), device_id_type=MESH).wait()
        for cp in local:
            cp.wait()

    return pl.pallas_call(
        body, in_specs=[ANY] * n, out_specs=[ANY] * n,
        out_shape=[jax.ShapeDtypeStruct((2,) + h.shape, h.dtype) for h in halves],
        scratch_shapes=[pltpu.SemaphoreType.DMA((n,))] * 3,
        name="join_halves")(*halves)


def kernel(x, meta_tokens, norm_g, w_in, conv_w, conv_b, ln_g, ln_b, w_conv_out, lb_logits, gnorm_g, w_rec_out, w_out, final_g, loss_target, m_meta_tokens, m_norm_g, m_w_in, m_conv_w, m_conv_b, m_ln_g, m_ln_b, m_w_conv_out, m_lb_logits, m_gnorm_g, m_w_rec_out, m_w_out, m_final_g, v_meta_tokens, v_norm_g, v_w_in, v_conv_w, v_conv_b, v_ln_g, v_ln_b, v_w_conv_out, v_lb_logits, v_gnorm_g, v_w_rec_out, v_w_out, v_final_g):
    seq, d = x.shape[1], x.shape[2]
    n_meta = meta_tokens.shape[0]
    n_pad = CHUNK - n_meta
    ds = d // N_CHIPS
    chip = 2 * lax.axis_index("x") + lax.axis_index("y")

    conv_w_pad = jnp.pad(conv_w[0], ((0, HALO - CONV_WIDTH), (0, 0)))
    win_g, sq_g, small_g = _gather_weights([
        w_in[0].astype(BF16),
        jnp.concatenate([w_conv_out[0], w_rec_out[0], w_out[0]], axis=0).astype(BF16),
        jnp.concatenate([conv_w_pad, meta_tokens], axis=0)])
    wc_full = sq_g[:, 0:ds].reshape(d, d)
    wr_full = sq_g[:, ds:2 * ds].reshape(d, d)
    wo_full = sq_g[:, 2 * ds:3 * ds].reshape(d, d)
    cw_full = jnp.transpose(small_g[:, 0:HALO], (1, 0, 2)).reshape(HALO, d)
    meta_full = jnp.transpose(small_g[:, HALO:HALO + n_meta], (1, 0, 2)).reshape(n_meta, d)

    hres = jnp.concatenate([jnp.zeros((n_pad, d), F32), meta_full, x[0]], axis=0)
    target = jnp.pad(loss_target[0], ((CHUNK, 0), (0, 0)))
    final_g2 = final_g.reshape(1, d)
    h = _rmsnorm_fwd(hres, norm_g)
    proj = _in_proj(h, win_g)
    c, yc_in, y_conv = _conv_fwd(proj, cw_full, conv_b, ln_g, ln_b, wc_full)
    o, s_all = _hgrn_fwd(proj, lb_logits, n_pad)
    yr_in, merged, y_rec, dout, loss_acc, dfinal_g = _tail_fwd(
        o, proj, y_conv, hres, target, gnorm_g, final_g2, wr_full, wo_full)

    (dyc, dyr, dz, dgr, dmc, dmr, do, dc, dgnorm_g, dln_g, dln_b) = _tail_bwd(
        dout, proj, y_conv, y_rec, o, c, wo_full, wr_full, wc_full, ln_g, ln_b, gnorm_g)
    dq, df, di, dlb_logits = _hgrn_bwd(proj, do, s_all, lb_logits, n_pad)
    dua, dub, dconv_w, dconv_b = _conv_bwd(dc, proj, cw_full)
    dproj = jnp.concatenate([dua, dub, dz, dq, df, di, dgr, dmc, dmr], axis=1)
    g_win = _weight_grad(h, dproj, "grad_w_in", True)
    g_wc = _weight_grad(yc_in, dyc, "grad_w_conv_out", False)
    g_wr = _weight_grad(yr_in, dyr, "grad_w_rec_out", False)
    g_wo = _weight_grad(merged, dout.astype(BF16), "grad_w_out", False)
    dhres, dnorm_g = _in_proj_bwd(dproj, win_g, hres, norm_g, dout)
    grad_x = dhres[CHUNK:][None]

    small = jnp.concatenate([dnorm_g, dconv_b, dln_g, dln_b, dlb_logits, dgnorm_g, dfinal_g,
                             dhres[n_pad:CHUNK], dconv_w[:CONV_WIDTH],
                             jnp.zeros((1, d), F32)], axis=0)
    g_sq = jnp.concatenate([g.reshape(N_CHIPS, ds, d) for g in (g_wc, g_wr, g_wo)], axis=1)
    grads = [g.reshape(N_CHIPS, 2, g.shape[1] // 2, g.shape[2]) for g in (g_win, g_sq)]
    core = lax.axis_index("c").astype(jnp.int32).reshape(1)
    from_sibling = _send_other_halves(grads)
    chip_sums = [_chip_half_sum("chip_half_sum_" + nm, g, r, core)
                 for nm, g, r in zip(("w_in", "square"), grads, from_sibling)]
    recv_win, recv_sq, small_slots = _exchange_grads([s[1] for s in chip_sums], small)
    chip_idx = chip.astype(jnp.int32).reshape(1)
    totals = [_block_half_total("block_half_total_" + nm, s[0], r, chip_idx)
              for nm, s, r in zip(("w_in", "square"), chip_sums, (recv_win, recv_sq))]
    gt_win, gt_sq = [t.reshape(2 * t.shape[1], t.shape[2]) for t in _join_halves(totals)]
    small_sum = _sum_slots("sum_small", small_slots)

    res = {}
    res["w_in"] = _adamw("adamw_w_in", w_in[0], m_w_in[0], v_w_in[0], gt_win)
    res["w_conv_out"] = _adamw("adamw_w_conv_out", w_conv_out[0], m_w_conv_out[0], v_w_conv_out[0], gt_sq[0:ds])
    res["w_rec_out"] = _adamw("adamw_w_rec_out", w_rec_out[0], m_w_rec_out[0], v_w_rec_out[0], gt_sq[ds:2 * ds])
    res["w_out"] = _adamw("adamw_w_out", w_out[0], m_w_out[0], v_w_out[0], gt_sq[2 * ds:3 * ds])
    big = {k: tuple(a[None] for a in v) for k, v in res.items()}

    rep_names = ("norm_g", "conv_b", "ln_g", "ln_b", "lb_logits", "gnorm_g", "final_g")
    rep_w = (norm_g, conv_b, ln_g, ln_b, lb_logits, gnorm_g, final_g2)
    rep_m = (m_norm_g, m_conv_b, m_ln_g, m_ln_b, m_lb_logits, m_gnorm_g, m_final_g.reshape(1, d))
    rep_v = (v_norm_g, v_conv_b, v_ln_g, v_ln_b, v_lb_logits, v_gnorm_g, v_final_g.reshape(1, d))
    rep = _adamw("adamw_replicated", jnp.concatenate(rep_w, 0), jnp.concatenate(rep_m, 0),
                 jnp.concatenate(rep_v, 0), small_sum[0:8])
    rep_rows = {"norm_g": (0, 1), "conv_b": (1, 2), "ln_g": (2, 3), "ln_b": (3, 4), "lb_logits": (4, 6),
                "gnorm_g": (6, 7), "final_g": (7, 8)}
    small_out = {}
    for nm in rep_names:
        lo, hi = rep_rows[nm]
        vals = tuple(a[lo:hi] for a in rep)
        if nm == "final_g":
            vals = tuple(a.reshape(d) for a in vals)
        small_out[nm] = vals
    cw_row = 8 + n_meta
    g_meta = lax.dynamic_slice_in_dim(small_sum[8:cw_row], chip * ds, ds, axis=1)
    small_out["meta_tokens"] = _adamw("adamw_meta", meta_tokens, m_meta_tokens, v_meta_tokens, g_meta)
    g_cw = lax.dynamic_slice_in_dim(small_sum[cw_row:cw_row + HALO], chip * ds, ds, axis=1)
    pad_rows = ((0, HALO - CONV_WIDTH), (0, 0))
    cw_res = _adamw("adamw_conv_w", conv_w_pad, jnp.pad(m_conv_w[0], pad_rows),
                    jnp.pad(v_conv_w[0], pad_rows, constant_values=1.0), g_cw)
    small_out["conv_w"] = tuple(a[:CONV_WIDTH][None] for a in cw_res)

    loss = lax.psum(loss_acc[0, 0], ("x", "y", "c"))

    order = ("meta_tokens", "norm_g", "w_in", "conv_w", "conv_b", "ln_g", "ln_b", "w_conv_out", "lb_logits",
             "gnorm_g", "w_rec_out", "w_out", "final_g")
    allres = {**big, **small_out}
    outs = [loss, grad_x]
    for field in range(4):
        outs.extend(allres[nm][field] for nm in order)
    return tuple(outs)
```

```python
import numpy as np

import jax
import jax.numpy as jnp
from jax import lax
from jax.experimental import pallas as pl
from jax.experimental.pallas import tpu as pltpu

F32 = jnp.float32
BF16 = jnp.bfloat16

EPS = 1e-6
CHUNK = 64
N_LEVELS = 6
CONV_WIDTH = 31
HALO = 32
CONV_ROWS = 32
HEAD = 128
N_CHIPS = 4
VMEM_LIMIT_BYTES = 56 * 1024 * 1024

ADAM_LR = 0.001
ADAM_B1 = 0.9
ADAM_B2 = 0.999
ADAM_EPS = 1e-08
ADAM_WD = 0.01
ADAM_STEP = 10

MESH = pl.DeviceIdType.MESH
ANY = pl.BlockSpec(memory_space=pl.ANY)

NT = (((1,), (1,)), ((), ()))
TN = (((0,), (0,)), ((), ()))


def _params(**kw):
    return pltpu.CompilerParams(vmem_limit_bytes=VMEM_LIMIT_BYTES, **kw)


def _sigmoid(x):
    return jax.nn.sigmoid(x)


def _dsilu(x, s):
    return s * (1.0 + x * (1.0 - s))


def _row_tile(lp):
    for t in (320, 256, 192, 128, 64):
        if lp % t == 0:
            return t
    raise ValueError(f"unsupported padded length {lp}")


def _mm_row_tile(lp):
    for t in (832, 640, 320, 256, 192, 128, 64):
        if lp % t == 0:
            return t
    raise ValueError(f"unsupported padded length {lp}")


def _dot3(m_bf16, x):
    hi = x.astype(BF16)
    r1 = x - hi.astype(F32)
    mid = r1.astype(BF16)
    lo = (r1 - mid.astype(F32)).astype(BF16)
    return (jnp.dot(m_bf16, hi, preferred_element_type=F32)
            + jnp.dot(m_bf16, mid, preferred_element_type=F32)
            + jnp.dot(m_bf16, lo, preferred_element_type=F32))


def _col_to_row(col):
    return jnp.broadcast_to(col, (HEAD, 8)).T[0:1, :]


def _row_to_col(row):
    return jnp.broadcast_to(row, (8, HEAD)).T[:, 0:1]


def _hgrn_tables():
    t = np.arange(CHUNK)
    ltri = (t[None, :] <= t[:, None]).astype(np.float32)
    mats = [ltri]
    for lvl in range(1, N_LEVELS + 1):
        blk = CHUNK >> (lvl - 1)
        mid = (t // blk) * blk + blk // 2
        mats.append(ltri[mid - 1])
    after = (t[None, :] >= t[:, None]).astype(np.float32)
    before = (t[None, :] < t[:, None]).astype(np.float32)
    return jnp.asarray(np.concatenate(mats, 0), BF16), jnp.asarray(np.concatenate([after, before], 1), BF16)


def _rmsnorm_fwd(hres, g):
    lp, d = hres.shape
    tm = _row_tile(lp)

    def body(x_ref, g_ref, h_ref):
        x = x_ref[...]
        r = lax.rsqrt(jnp.mean(x * x, axis=-1, keepdims=True) + EPS)
        h_ref[...] = (x * r * g_ref[...]).astype(BF16)

    return pl.pallas_call(
        body, grid=(lp // tm,),
        in_specs=[pl.BlockSpec((tm, d), lambda i: (i, 0)), pl.BlockSpec((1, d), lambda i: (0, 0))],
        out_specs=pl.BlockSpec((tm, d), lambda i: (i, 0)),
        out_shape=jax.ShapeDtypeStruct((lp, d), BF16),
        name="rmsnorm_fwd", compiler_params=_params())(hres, g)


def _in_proj(h, wg):
    lp, d = h.shape
    _, _, ncol = wg.shape
    tm = _mm_row_tile(lp)
    nt = 3
    tn = ncol // nt

    def body(h_ref, w_ref, o_ref):
        o_ref[...] = jnp.dot(h_ref[...], w_ref[0], preferred_element_type=F32)

    return pl.pallas_call(
        body, grid=(N_CHIPS, nt, lp // tm),
        in_specs=[pl.BlockSpec((tm, d), lambda j, n, i: (i, 0)),
                  pl.BlockSpec((1, d, tn), lambda j, n, i: (j, 0, n))],
        out_specs=pl.BlockSpec((tm, tn), lambda j, n, i: (i, j * nt + n)),
        out_shape=jax.ShapeDtypeStruct((lp, N_CHIPS * ncol), F32),
        name="in_proj", compiler_params=_params())(h, wg)


def _conv_fwd(proj, conv_w, conv_b, ln_g, ln_b, w_conv):
    lp = proj.shape[0]
    d = conv_b.shape[1]
    tm = _row_tile(lp)
    hb = tm // HALO

    def body(ua_ref, ub_ref, z_ref, uap_ref, ubp_ref, cw_ref, cb_ref, lg_ref, lb_ref, w_ref,
             c_ref, ycin_ref, yconv_ref, aext_ref):
        i = pl.program_id(0)
        a_prev = uap_ref[...] * _sigmoid(ubp_ref[...])
        aext_ref[0:HALO, :] = jnp.where(i > 0, a_prev, 0.0)
        aext_ref[HALO:HALO + tm, :] = ua_ref[...] * _sigmoid(ub_ref[...])

        def row_block(r, carry):
            r0 = pl.multiple_of(r * CONV_ROWS, CONV_ROWS)
            blk = aext_ref[pl.ds(r0, CONV_ROWS + HALO), :]
            acc = jnp.zeros((CONV_ROWS, d), F32) + cb_ref[...]
            for j in range(CONV_WIDTH):
                acc = acc + cw_ref[j:j + 1, :] * blk[j + 2:j + 2 + CONV_ROWS, :]
            c_ref[pl.ds(r0, CONV_ROWS), :] = acc
            return carry

        lax.fori_loop(0, tm // CONV_ROWS, row_block, 0)

        c = c_ref[...]
        mu = jnp.mean(c, axis=-1, keepdims=True)
        xc = c - mu
        rstd = lax.rsqrt(jnp.mean(xc * xc, axis=-1, keepdims=True) + EPS)
        ln = xc * rstd * lg_ref[...] + lb_ref[...]
        s = ln * _sigmoid(ln)
        z = z_ref[...]
        ycin = (s * (z * _sigmoid(z))).astype(BF16)
        ycin_ref[...] = ycin
        yconv_ref[...] = jnp.dot(ycin, w_ref[...], preferred_element_type=F32)

    row = lambda p: pl.BlockSpec((tm, d), lambda i, p=p: (i, p))
    halo = lambda p: pl.BlockSpec((HALO, d), lambda i, p=p: (jnp.maximum(i * hb - 1, 0), p))
    vec = pl.BlockSpec((1, d), lambda i: (0, 0))
    return pl.pallas_call(
        body, grid=(lp // tm,),
        in_specs=[row(0), row(1), row(2), halo(0), halo(1),
                  pl.BlockSpec((HALO, d), lambda i: (0, 0)), vec, vec, vec,
                  pl.BlockSpec((d, d), lambda i: (0, 0))],
        out_specs=[pl.BlockSpec((tm, d), lambda i: (i, 0))] * 3,
        out_shape=[jax.ShapeDtypeStruct((lp, d), F32), jax.ShapeDtypeStruct((lp, d), BF16),
                   jax.ShapeDtypeStruct((lp, d), F32)],
        scratch_shapes=[pltpu.VMEM((HALO + tm, d), F32)],
        name="conv_fwd", compiler_params=_params())(
            proj, proj, proj, proj, proj, conv_w, conv_b, ln_g, ln_b, w_conv)


def _lower_bound(lbl_ref):
    l0 = lbl_ref[0:1, :]
    l1 = lbl_ref[1:2, :]
    m = jnp.maximum(l0, l1)
    e0 = jnp.exp(l0 - m)
    e1 = jnp.exp(l1 - m)
    p0 = e0 / (e0 + e1)
    return p0, p0 * (e1 / (e0 + e1))


def _level_masks():
    rid = lax.broadcasted_iota(jnp.int32, (CHUNK, 1), 0)
    r2 = lax.broadcasted_iota(jnp.int32, (CHUNK, CHUNK), 0)
    c2 = lax.broadcasted_iota(jnp.int32, (CHUNK, CHUNK), 1)
    out = []
    for lvl in range(1, N_LEVELS + 1):
        blk = CHUNK >> (lvl - 1)
        sh = blk.bit_length() - 1
        upper = (rid & (blk - 1)) >= (blk // 2)
        same = (r2 >> sh) == (c2 >> sh)
        out.append((upper, same))
    return out


def _gates(qr, fr, lb, valid):
    sq = _sigmoid(qr)
    q = qr * sq
    sf = _sigmoid(fr)
    f = lb + (1.0 - lb) * sf
    g = jnp.where(valid, jnp.log(f), 0.0)
    k = jnp.where(valid, 1.0 - f, 0.0)
    return q, sq, f, sf, g, k


def _level_factors(b, r, upper):
    eq = jnp.where(upper, jnp.exp(jnp.minimum(b - r, 0.0)), 0.0)
    ek = jnp.where(upper, 0.0, jnp.exp(jnp.minimum(r - b, 0.0)))
    return eq, ek


def _hgrn_fwd(proj, lb_logits, n_pad):
    lp = proj.shape[0]
    d = lb_logits.shape[1]
    n_heads = d // HEAD
    nc = lp // CHUNK
    tab, _ = _hgrn_tables()
    n_tab = tab.shape[0]

    def body(qr_ref, fr_ref, ir_ref, lbl_ref, tab_ref, o_ref, sall_ref, s_ref, t_ref):
        n = pl.program_id(0)

        @pl.when(n == 0)
        def _():
            s_ref[...] = jnp.zeros_like(s_ref)

        sall_ref[0] = s_ref[...]
        lb_all, _ = _lower_bound(lbl_ref)
        rid = lax.broadcasted_iota(jnp.int32, (CHUNK, 1), 0)
        valid = jnp.logical_or(n > 0, rid >= n_pad)
        f_all = lb_all + (1.0 - lb_all) * _sigmoid(fr_ref[...])
        t_ref[...] = _dot3(tab_ref[...], jnp.where(valid, jnp.log(f_all), 0.0))
        masks = _level_masks()

        def head(h, carry):
            off = pl.multiple_of(h * HEAD, HEAD)
            hs = pl.ds(off, HEAD)
            lb = _lower_bound_slice(lbl_ref, hs)
            q, _, _, _, _, k = _gates(qr_ref[:, hs], fr_ref[:, hs], lb, valid)
            v = ir_ref[:, hs]
            b = t_ref[0:CHUNK, hs]
            s0 = s_ref[hs, :]
            o = jnp.dot((q * jnp.exp(b)).astype(BF16), s0.astype(BF16), preferred_element_type=F32)
            o = o + jnp.sum(q * k, axis=-1, keepdims=True) * v
            a = jnp.zeros((CHUNK, CHUNK), F32)
            for lvl in range(1, N_LEVELS + 1):
                upper, same = masks[lvl - 1]
                eq, ek = _level_factors(b, t_ref[CHUNK * lvl:CHUNK * (lvl + 1), hs], upper)
                p = lax.dot_general((q * eq).astype(BF16), (k * ek).astype(BF16), NT, preferred_element_type=F32)
                a = a + jnp.where(same, p, 0.0)
            vb = v.astype(BF16)
            o_ref[:, hs] = o + jnp.dot(a.astype(BF16), vb, preferred_element_type=F32)
            b_last = t_ref[CHUNK - 1:CHUNK, hs]
            khat = (k * jnp.exp(b_last - b)).astype(BF16)
            s_ref[hs, :] = _row_to_col(jnp.exp(b_last)) * s0 + lax.dot_general(khat, vb, TN, preferred_element_type=F32)
            return carry

        lax.fori_loop(0, n_heads, head, 0)

    piece = lambda p: pl.BlockSpec((CHUNK, d), lambda n, p=p: (n, p))
    return pl.pallas_call(
        body, grid=(nc,),
        in_specs=[piece(3), piece(4), piece(5), pl.BlockSpec((2, d), lambda n: (0, 0)),
                  pl.BlockSpec((n_tab, CHUNK), lambda n: (0, 0))],
        out_specs=[pl.BlockSpec((CHUNK, d), lambda n: (n, 0)), pl.BlockSpec((1, d, HEAD), lambda n: (n, 0, 0))],
        out_shape=[jax.ShapeDtypeStruct((lp, d), F32), jax.ShapeDtypeStruct((nc, d, HEAD), F32)],
        scratch_shapes=[pltpu.VMEM((d, HEAD), F32), pltpu.VMEM((n_tab, d), F32)],
        name="hgrn_fwd", compiler_params=_params())(proj, proj, proj, lb_logits, tab)


def _lower_bound_slice(lbl_ref, hs):
    l0 = lbl_ref[0:1, hs]
    l1 = lbl_ref[1:2, hs]
    m = jnp.maximum(l0, l1)
    e0 = jnp.exp(l0 - m)
    e1 = jnp.exp(l1 - m)
    return e0 / (e0 + e1)


def _tail_fwd(o, proj, y_conv, hres, target, gnorm_g, final_g, w_rec, w_out):
    lp, d = o.shape
    n_heads = d // HEAD
    tm = _row_tile(lp)

    def body(o_ref, gr_ref, mc_ref, mr_ref, yc_ref, x_ref, t_ref, gn_ref, fg_ref, wr_ref, wo_ref,
             yrin_ref, mg_ref, yrec_ref, dout_ref, loss_ref, dfg_ref):
        i = pl.program_id(0)

        @pl.when(i == 0)
        def _():
            loss_ref[...] = jnp.zeros_like(loss_ref)
            dfg_ref[...] = jnp.zeros_like(dfg_ref)

        for h in range(n_heads):
            hs = slice(h * HEAD, (h + 1) * HEAD)
            oh = o_ref[:, hs]
            on = oh * lax.rsqrt(jnp.mean(oh * oh, axis=-1, keepdims=True) + EPS) * gn_ref[:, hs]
            gr = gr_ref[:, hs]
            yrin_ref[:, hs] = (on * (gr * _sigmoid(gr))).astype(BF16)
        yrec = jnp.dot(yrin_ref[...], wr_ref[...], preferred_element_type=F32)
        yrec_ref[...] = yrec
        merged = (_sigmoid(mc_ref[...]) * yc_ref[...] + _sigmoid(mr_ref[...]) * yrec).astype(BF16)
        mg_ref[...] = merged
        out = x_ref[...] + jnp.dot(merged, wo_ref[...], preferred_element_type=F32)
        r = lax.rsqrt(jnp.mean(out * out, axis=-1, keepdims=True) + EPS)
        yhat = out * r
        fg = fg_ref[...]
        rid = lax.broadcasted_iota(jnp.int32, (tm, 1), 0) + i * tm
        err = jnp.where(rid >= CHUNK, yhat * fg - t_ref[...], 0.0)
        loss_ref[...] += 0.5 * jnp.sum(err * err) / d
        dy = err / d
        dfg_ref[...] += jnp.sum(dy * yhat, axis=0, keepdims=True)
        dyh = dy * fg
        dout_ref[...] = r * (dyh - yhat * jnp.mean(dyh * yhat, axis=-1, keepdims=True))

    row = lambda p: pl.BlockSpec((tm, d), lambda i, p=p: (i, p))
    vec = pl.BlockSpec((1, d), lambda i: (0, 0))
    mat = pl.BlockSpec((d, d), lambda i: (0, 0))
    return pl.pallas_call(
        body, grid=(lp // tm,),
        in_specs=[row(0), row(6), row(7), row(8), row(0), row(0), row(0), vec, vec, mat, mat],
        out_specs=[row(0), row(0), row(0), row(0), pl.BlockSpec((8, 128), lambda i: (0, 0)), vec],
        out_shape=[jax.ShapeDtypeStruct((lp, d), BF16), jax.ShapeDtypeStruct((lp, d), BF16),
                   jax.ShapeDtypeStruct((lp, d), F32), jax.ShapeDtypeStruct((lp, d), F32),
                   jax.ShapeDtypeStruct((8, 128), F32), jax.ShapeDtypeStruct((1, d), F32)],
        name="tail_fwd", compiler_params=_params())(
            o, proj, proj, proj, y_conv, hres, target, gnorm_g, final_g, w_rec, w_out)


def _tail_bwd(dout, proj, y_conv, y_rec, o, c, w_out, w_rec, w_conv, ln_g, ln_b, gnorm_g):
    lp, d = dout.shape
    n_heads = d // HEAD
    tm = _row_tile(lp)

    def body(dout_ref, mc_ref, mr_ref, z_ref, gr_ref, yc_ref, yrec_ref, o_ref, c_ref,
             wo_ref, wr_ref, wc_ref, lg_ref, lb_ref, gn_ref,
             dyc_ref, dyr_ref, doutb_ref, dz_ref, dp_ref, do_ref, dc_ref,
             dgn_ref, dlg_ref, dlb_ref, dyrin_ref):
        i = pl.program_id(0)

        @pl.when(i == 0)
        def _():
            dgn_ref[...] = jnp.zeros_like(dgn_ref)
            dlg_ref[...] = jnp.zeros_like(dlg_ref)
            dlb_ref[...] = jnp.zeros_like(dlb_ref)

        doutb = dout_ref[...].astype(BF16)
        doutb_ref[...] = doutb
        dmerged = lax.dot_general(doutb, wo_ref[...], NT, preferred_element_type=F32)
        smc = _sigmoid(mc_ref[...])
        smr = _sigmoid(mr_ref[...])
        dyc = (dmerged * smc).astype(BF16)
        dyr = (dmerged * smr).astype(BF16)
        dyc_ref[...] = dyc
        dyr_ref[...] = dyr
        dp_ref[:, d:2 * d] = (dmerged * yc_ref[...] * smc * (1.0 - smc)).astype(BF16)
        dp_ref[:, 2 * d:3 * d] = (dmerged * yrec_ref[...] * smr * (1.0 - smr)).astype(BF16)

        dyrin_ref[...] = lax.dot_general(dyr, wr_ref[...], NT, preferred_element_type=F32)
        for h in range(n_heads):
            hs = slice(h * HEAD, (h + 1) * HEAD)
            oh = o_ref[:, hs]
            rstd = lax.rsqrt(jnp.mean(oh * oh, axis=-1, keepdims=True) + EPS)
            ohat = oh * rstd
            gn = gn_ref[:, hs]
            gr = gr_ref[:, hs]
            sg = _sigmoid(gr)
            dyrin = dyrin_ref[:, hs]
            don = dyrin * (gr * sg)
            dp_ref[:, hs] = (dyrin * (ohat * gn) * _dsilu(gr, sg)).astype(BF16)
            dgn_ref[:, hs] += jnp.sum(don * ohat, axis=0, keepdims=True)
            doh = don * gn
            do_ref[:, hs] = rstd * (doh - ohat * jnp.mean(doh * ohat, axis=-1, keepdims=True))

        dycin = lax.dot_general(dyc, wc_ref[...], NT, preferred_element_type=F32)
        c = c_ref[...]
        mu = jnp.mean(c, axis=-1, keepdims=True)
        xc = c - mu
        rstd = lax.rsqrt(jnp.mean(xc * xc, axis=-1, keepdims=True) + EPS)
        nrm = xc * rstd
        lg = lg_ref[...]
        ln = nrm * lg + lb_ref[...]
        sl = _sigmoid(ln)
        z = z_ref[...]
        sz = _sigmoid(z)
        dz_ref[...] = (dycin * (ln * sl) * _dsilu(z, sz)).astype(BF16)
        dln = dycin * (z * sz) * _dsilu(ln, sl)
        dlg_ref[...] += jnp.sum(dln * nrm, axis=0, keepdims=True)
        dlb_ref[...] += jnp.sum(dln, axis=0, keepdims=True)
        dn = dln * lg
        dc_ref[...] = rstd * (dn - jnp.mean(dn, axis=-1, keepdims=True)
                              - nrm * jnp.mean(dn * nrm, axis=-1, keepdims=True))

    row = lambda p: pl.BlockSpec((tm, d), lambda i, p=p: (i, p))
    vec = pl.BlockSpec((1, d), lambda i: (0, 0))
    mat = pl.BlockSpec((d, d), lambda i: (0, 0))
    act_bf = jax.ShapeDtypeStruct((lp, d), BF16)
    act_f32 = jax.ShapeDtypeStruct((lp, d), F32)
    vec_f32 = jax.ShapeDtypeStruct((1, d), F32)
    return pl.pallas_call(
        body, grid=(lp // tm,),
        in_specs=[row(0), row(7), row(8), row(2), row(6), row(0), row(0), row(0), row(0),
                  mat, mat, mat, vec, vec, vec],
        out_specs=[row(0)] * 4 + [pl.BlockSpec((tm, 3 * d), lambda i: (i, 2))] + [row(0)] * 2 + [vec] * 3,
        out_shape=[act_bf] * 4 + [jax.ShapeDtypeStruct((lp, 9 * d), BF16)] + [act_f32] * 2 + [vec_f32] * 3,
        scratch_shapes=[pltpu.VMEM((tm, d), F32)],
        name="tail_bwd", compiler_params=_params())(
            dout, proj, proj, proj, proj, y_conv, y_rec, o, c, w_out, w_rec, w_conv, ln_g, ln_b, gnorm_g)


def _hgrn_bwd(proj, do, s_all, lb_logits, n_pad, dproj):
    lp, d = do.shape
    n_heads = d // HEAD
    nc = lp // CHUNK
    tab, utri = _hgrn_tables()
    n_tab = tab.shape[0]

    def body(qr_ref, fr_ref, ir_ref, do_ref, s0_ref, lbl_ref, tab_ref, ut_ref, _,
             dp_ref, dlbl_ref, ds_ref, t_ref, dlb_ref):
        n = pl.program_id(0)
        chunk = nc - 1 - n

        @pl.when(n == 0)
        def _():
            ds_ref[...] = jnp.zeros_like(ds_ref)
            dlb_ref[...] = jnp.zeros_like(dlb_ref)

        lb_all, pp = _lower_bound(lbl_ref)
        rid = lax.broadcasted_iota(jnp.int32, (CHUNK, 1), 0)
        valid = jnp.logical_or(chunk > 0, rid >= n_pad)
        f_all = lb_all + (1.0 - lb_all) * _sigmoid(fr_ref[...])
        t_ref[...] = _dot3(tab_ref[...], jnp.where(valid, jnp.log(f_all), 0.0))
        masks = _level_masks()
        ut = ut_ref[...]

        def head(h, carry):
            off = pl.multiple_of(h * HEAD, HEAD)
            hs = pl.ds(off, HEAD)
            lb = _lower_bound_slice(lbl_ref, hs)
            qr = qr_ref[:, hs]
            q, sq, f, sf, _, k = _gates(qr, fr_ref[:, hs], lb, valid)
            v = ir_ref[:, hs]
            do_h = do_ref[:, hs]
            b = t_ref[0:CHUNK, hs]
            b_last = t_ref[CHUNK - 1:CHUNK, hs]
            s0 = s0_ref[0, hs, :]
            ds1 = ds_ref[hs, :]
            eb = jnp.exp(b)
            ekl = jnp.exp(b_last - b)
            do_bf = do_h.astype(BF16)
            v_bf = v.astype(BF16)
            ds1_bf = ds1.astype(BF16)

            da = lax.dot_general(do_bf, v_bf, NT, preferred_element_type=F32)
            da_diag = jnp.sum(do_h * v, axis=-1, keepdims=True)
            a = jnp.zeros((CHUNK, CHUNK), F32)
            dq_x = eb * lax.dot_general(do_bf, s0.astype(BF16), NT, preferred_element_type=F32)
            dk_x = ekl * lax.dot_general(v_bf, ds1_bf, NT, preferred_element_type=F32)
            x_after = q * dq_x
            x_before = k * dk_x
            for lvl in range(1, N_LEVELS + 1):
                upper, same = masks[lvl - 1]
                eq, ek = _level_factors(b, t_ref[CHUNK * lvl:CHUNK * (lvl + 1), hs], upper)
                qt = (q * eq).astype(BF16)
                kt = (k * ek).astype(BF16)
                p = lax.dot_general(qt, kt, NT, preferred_element_type=F32)
                a = a + jnp.where(same, p, 0.0)
                dam = jnp.where(same, da, 0.0).astype(BF16)
                dqt = jnp.dot(dam, kt, preferred_element_type=F32)
                dkt = lax.dot_general(dam, qt, TN, preferred_element_type=F32)
                dq_x = dq_x + eq * dqt
                dk_x = dk_x + ek * dkt
                x_after = x_after + (qt.astype(F32) * dqt - kt.astype(F32) * dkt)

            dv = (lax.dot_general(a.astype(BF16), do_bf, TN, preferred_element_type=F32)
                  + jnp.sum(q * k, axis=-1, keepdims=True) * do_h
                  + jnp.dot((k * ekl).astype(BF16), ds1_bf, preferred_element_type=F32))
            dp_ref[:, pl.ds(pl.multiple_of(2 * d + off, HEAD), HEAD)] = dv.astype(BF16)

            carried = jnp.exp(b_last) * _col_to_row(jnp.sum(s0 * ds1, axis=-1, keepdims=True))
            dg = _dot3(ut, jnp.concatenate([x_after, x_before], axis=0)) + carried
            dq = dq_x + da_diag * k
            dk = dk_x + da_diag * q
            dp_ref[:, hs] = (dq * _dsilu(qr, sq)).astype(BF16)
            df = jnp.where(valid, dg / f - dk, 0.0)
            dp_ref[:, pl.ds(pl.multiple_of(d + off, HEAD), HEAD)] = (df * (1.0 - lb) * sf * (1.0 - sf)).astype(BF16)
            dlb_ref[:, hs] += jnp.sum(df * (1.0 - sf), axis=0, keepdims=True)

            ds_ref[hs, :] = (_row_to_col(jnp.exp(b_last)) * ds1
                             + lax.dot_general((q * eb).astype(BF16), do_bf, TN, preferred_element_type=F32))
            return carry

        lax.fori_loop(0, n_heads, head, 0)

        @pl.when(n == nc - 1)
        def _():
            dl0 = dlb_ref[...] * pp
            dlbl_ref[0:1, :] = dl0
            dlbl_ref[1:2, :] = -dl0

    piece = lambda p: pl.BlockSpec((CHUNK, d), lambda n, p=p: (nc - 1 - n, p))
    return pl.pallas_call(
        body, grid=(nc,),
        in_specs=[piece(3), piece(4), piece(5), piece(0),
                  pl.BlockSpec((1, d, HEAD), lambda n: (nc - 1 - n, 0, 0)),
                  pl.BlockSpec((2, d), lambda n: (0, 0)),
                  pl.BlockSpec((n_tab, CHUNK), lambda n: (0, 0)),
                  pl.BlockSpec((CHUNK, 2 * CHUNK), lambda n: (0, 0)), ANY],
        out_specs=[pl.BlockSpec((CHUNK, 3 * d), lambda n: (nc - 1 - n, 1)), pl.BlockSpec((2, d), lambda n: (0, 0))],
        out_shape=[jax.ShapeDtypeStruct(dproj.shape, BF16), jax.ShapeDtypeStruct((2, d), F32)],
        input_output_aliases={8: 0},
        scratch_shapes=[pltpu.VMEM((d, HEAD), F32), pltpu.VMEM((n_tab, d), F32), pltpu.VMEM((1, d), F32)],
        name="hgrn_bwd", compiler_params=_params())(proj, proj, proj, do, s_all, lb_logits, tab, utri, dproj)


def _conv_bwd(dc, proj, conv_w, dz, dproj):
    lp, d = dc.shape
    tm = _row_tile(lp)
    hb = tm // HALO
    n_tiles = lp // tm
    last_halo = lp // HALO - 1

    def body(dc_ref, dcn_ref, ua_ref, ub_ref, uap_ref, ubp_ref, cw_ref, dz_ref, _,
             dp_ref, dcw_ref, dcb_ref, aext_ref, dcext_ref, da_ref):
        i = pl.program_id(0)

        @pl.when(i == 0)
        def _():
            dcw_ref[...] = jnp.zeros_like(dcw_ref)
            dcb_ref[...] = jnp.zeros_like(dcb_ref)

        ua = ua_ref[...]
        sb = _sigmoid(ub_ref[...])
        a_prev = uap_ref[...] * _sigmoid(ubp_ref[...])
        aext_ref[0:HALO, :] = jnp.where(i > 0, a_prev, 0.0)
        aext_ref[HALO:HALO + tm, :] = ua * sb
        dcext_ref[0:tm, :] = dc_ref[...]
        dcext_ref[tm:tm + HALO, :] = jnp.where(i < n_tiles - 1, dcn_ref[...], 0.0)
        dcb_ref[...] += jnp.sum(dc_ref[...], axis=0, keepdims=True)

        def row_block(r, carry):
            r0 = pl.multiple_of(r * CONV_ROWS, CONV_ROWS)
            dblk = dcext_ref[pl.ds(r0, CONV_ROWS + HALO), :]
            ablk = aext_ref[pl.ds(r0, CONV_ROWS + HALO), :]
            dcur = dblk[0:CONV_ROWS, :]
            acc = jnp.zeros((CONV_ROWS, d), F32)
            for j in range(CONV_WIDTH):
                acc = acc + cw_ref[j:j + 1, :] * dblk[30 - j:30 - j + CONV_ROWS, :]
                dcw_ref[j:j + 1, :] += jnp.sum(dcur * ablk[j + 2:j + 2 + CONV_ROWS, :], axis=0, keepdims=True)
            da_ref[pl.ds(r0, CONV_ROWS), :] = acc
            return carry

        lax.fori_loop(0, tm // CONV_ROWS, row_block, 0)

        da = da_ref[...]
        dp_ref[:, 0:d] = (da * sb).astype(BF16)
        dp_ref[:, d:2 * d] = (da * ua * sb * (1.0 - sb)).astype(BF16)
        dp_ref[:, 2 * d:3 * d] = dz_ref[...]

    row = lambda p: pl.BlockSpec((tm, d), lambda i, p=p: (i, p))
    prev = lambda p: pl.BlockSpec((HALO, d), lambda i, p=p: (jnp.maximum(i * hb - 1, 0), p))
    nxt = pl.BlockSpec((HALO, d), lambda i: (jnp.minimum((i + 1) * hb, last_halo), 0))
    return pl.pallas_call(
        body, grid=(n_tiles,),
        in_specs=[row(0), nxt, row(0), row(1), prev(0), prev(1), pl.BlockSpec((HALO, d), lambda i: (0, 0)),
                  row(0), ANY],
        out_specs=[pl.BlockSpec((tm, 3 * d), lambda i: (i, 0)), pl.BlockSpec((HALO, d), lambda i: (0, 0)),
                   pl.BlockSpec((1, d), lambda i: (0, 0))],
        out_shape=[jax.ShapeDtypeStruct(dproj.shape, BF16),
                   jax.ShapeDtypeStruct((HALO, d), F32), jax.ShapeDtypeStruct((1, d), F32)],
        input_output_aliases={8: 0},
        scratch_shapes=[pltpu.VMEM((HALO + tm, d), F32), pltpu.VMEM((tm + HALO, d), F32), pltpu.VMEM((tm, d), F32)],
        name="conv_bwd", compiler_params=_params())(dc, dc, proj, proj, proj, proj, conv_w, dz, dproj)


def _weight_grad(xs, dy, name, blocked):
    lp, dx = xs.shape
    n = dy.shape[1]
    tk = _mm_row_tile(lp)
    if blocked:
        ncol = n // N_CHIPS
        nt = 3
        tn = ncol // nt
        grid = (N_CHIPS * nt, lp // tk)
        out_spec = pl.BlockSpec((1, dx, tn), lambda c, k: (c // nt, 0, c % nt))
        out_shape = jax.ShapeDtypeStruct((N_CHIPS, dx, ncol), F32)
    else:
        tn = n // 2
        grid = (2, lp // tk)
        out_spec = pl.BlockSpec((dx, tn), lambda c, k: (0, c))
        out_shape = jax.ShapeDtypeStruct((dx, n), F32)

    def body(xs_ref, dy_ref, o_ref):
        @pl.when(pl.program_id(1) == 0)
        def _():
            o_ref[...] = jnp.zeros_like(o_ref)

        p = lax.dot_general(xs_ref[...], dy_ref[...], TN, preferred_element_type=F32)
        if blocked:
            o_ref[0] += p
        else:
            o_ref[...] += p

    return pl.pallas_call(
        body, grid=grid,
        in_specs=[pl.BlockSpec((tk, dx), lambda c, k: (k, 0)), pl.BlockSpec((tk, tn), lambda c, k: (k, c))],
        out_specs=out_spec, out_shape=out_shape,
        name=name, compiler_params=_params())(xs, dy)


def _in_proj_bwd(dproj, wg, hres, norm_g, dout):
    lp, d = hres.shape
    _, _, ncol = wg.shape
    tm = _mm_row_tile(lp)
    nt = 3
    tn = ncol // nt
    nk = N_CHIPS * nt

    def body(dp_ref, w_ref, x_ref, g_ref, dout_ref, dx_ref, dg_ref, acc_ref):
        i = pl.program_id(0)
        kk = pl.program_id(1)

        @pl.when(jnp.logical_and(i == 0, kk == 0))
        def _():
            dg_ref[...] = jnp.zeros_like(dg_ref)

        @pl.when(kk == 0)
        def _():
            acc_ref[...] = jnp.zeros_like(acc_ref)

        acc_ref[...] += lax.dot_general(dp_ref[...], w_ref[0], NT, preferred_element_type=F32)

        @pl.when(kk == nk - 1)
        def _():
            x = x_ref[...]
            r = lax.rsqrt(jnp.mean(x * x, axis=-1, keepdims=True) + EPS)
            xhat = x * r
            dh = acc_ref[...]
            dg_ref[...] += jnp.sum(dh * xhat, axis=0, keepdims=True)
            dxh = dh * g_ref[...]
            dx_ref[...] = dout_ref[...] + r * (dxh - xhat * jnp.mean(dxh * xhat, axis=-1, keepdims=True))

    return pl.pallas_call(
        body, grid=(lp // tm, nk),
        in_specs=[pl.BlockSpec((tm, tn), lambda i, k: (i, k)),
                  pl.BlockSpec((1, d, tn), lambda i, k: (k // nt, 0, k % nt)),
                  pl.BlockSpec((tm, d), lambda i, k: (i, 0)),
                  pl.BlockSpec((1, d), lambda i, k: (0, 0)),
                  pl.BlockSpec((tm, d), lambda i, k: (i, 0))],
        out_specs=[pl.BlockSpec((tm, d), lambda i, k: (i, 0)), pl.BlockSpec((1, d), lambda i, k: (0, 0))],
        out_shape=[jax.ShapeDtypeStruct((lp, d), F32), jax.ShapeDtypeStruct((1, d), F32)],
        scratch_shapes=[pltpu.VMEM((tm, d), F32)],
        name="in_proj_bwd", compiler_params=_params())(dproj, wg, hres, norm_g, dout)


def _adamw_math(w, g, m, v):
    m = ADAM_B1 * m + (1.0 - ADAM_B1) * g
    v = ADAM_B2 * v + (1.0 - ADAM_B2) * (g * g)
    m_hat = m / (1.0 - ADAM_B1 ** ADAM_STEP)
    v_hat = v / (1.0 - ADAM_B2 ** ADAM_STEP)
    delta = -ADAM_LR * (m_hat / (jnp.sqrt(v_hat) + ADAM_EPS) + ADAM_WD * w)
    return delta, m, v


def _elementwise_rows(shape):
    r, c = shape
    for t in (256, 128, 64, 32, 16, 8):
        if r % t == 0 and r > t and t * c * 4 <= 1024 * 1024:
            return t
    return r


def _adamw(name, w, m, v, *g_parts):
    shape = w.shape
    tr = _elementwise_rows(shape)
    n_g = len(g_parts)

    def body(*refs):
        w_ref, m_ref, v_ref = refs[:3]
        g_refs = refs[3:3 + n_g]
        g_out, d_out, m_out, v_out = refs[3 + n_g:]
        g = g_refs[0][...]
        for gr in g_refs[1:]:
            g = g + gr[...]
        delta, m_new, v_new = _adamw_math(w_ref[...], g, m_ref[...], v_ref[...])
        g_out[...] = g
        d_out[...] = delta
        m_out[...] = m_new
        v_out[...] = v_new

    spec = pl.BlockSpec((tr, shape[1]), lambda i: (i, 0))
    return pl.pallas_call(
        body, grid=(shape[0] // tr,),
        in_specs=[spec] * (3 + n_g), out_specs=[spec] * 4,
        out_shape=[jax.ShapeDtypeStruct(shape, F32)] * 4,
        name=name, compiler_params=_params())(w, m, v, *g_parts)


def _chip_half_sum(name, g, recv, core):
    _, _, hr, cols = g.shape
    tr = _elementwise_rows((hr, cols))

    def body(core_ref, g_ref, r_ref, o_ref, ob_ref):
        s = g_ref[0, 0] + r_ref[0]
        o_ref[0] = s
        ob_ref[0] = s.astype(BF16)

    blk = pl.BlockSpec((1, tr, cols), lambda j, i, core_ref: (j, i, 0))
    grid_spec = pltpu.PrefetchScalarGridSpec(
        num_scalar_prefetch=1, grid=(N_CHIPS, hr // tr),
        in_specs=[pl.BlockSpec((1, 1, tr, cols), lambda j, i, core_ref: (j, core_ref[0], i, 0)), blk],
        out_specs=[blk, blk])
    return pl.pallas_call(
        body, grid_spec=grid_spec,
        out_shape=[jax.ShapeDtypeStruct((N_CHIPS, hr, cols), F32), jax.ShapeDtypeStruct((N_CHIPS, hr, cols), BF16)],
        name=name, compiler_params=_params())(core, g, recv)


def _block_half_total(name, chip_sums, recv, chip_core):
    _, hr, cols = chip_sums.shape
    tr = _elementwise_rows((hr, cols))

    def body(cc_ref, p_ref, r_ref, o_ref):
        s = p_ref[0]
        for k in range(3):
            s = s + r_ref[k].astype(F32)
        o_ref[0] = s

    grid_spec = pltpu.PrefetchScalarGridSpec(
        num_scalar_prefetch=1, grid=(hr // tr,),
        in_specs=[pl.BlockSpec((1, tr, cols), lambda i, cc_ref: (cc_ref[0], i, 0)),
                  pl.BlockSpec((3, tr, cols), lambda i, cc_ref: (0, i, 0))],
        out_specs=pl.BlockSpec((1, tr, cols), lambda i, cc_ref: (cc_ref[1], i, 0)))
    return pl.pallas_call(
        body, grid_spec=grid_spec, out_shape=jax.ShapeDtypeStruct((2, hr, cols), F32),
        name=name, compiler_params=_params())(chip_core, chip_sums, recv)


def _place_shard(name, w, chip, dtype):
    r, c = w.shape
    tr = _elementwise_rows((r, c))

    def body(chip_ref, w_ref, o_ref):
        o_ref[0] = w_ref[...].astype(dtype)

    grid_spec = pltpu.PrefetchScalarGridSpec(
        num_scalar_prefetch=1, grid=(r // tr,),
        in_specs=[pl.BlockSpec((tr, c), lambda i, chip_ref: (i, 0))],
        out_specs=pl.BlockSpec((1, tr, c), lambda i, chip_ref: (chip_ref[0], i, 0)))
    return pl.pallas_call(
        body, grid_spec=grid_spec, out_shape=jax.ShapeDtypeStruct((N_CHIPS, r, c), dtype),
        name=name, compiler_params=_params())(chip, w)


def _sum_slots(name, slots):
    k, r, c = slots.shape

    def body(s_ref, o_ref):
        s = s_ref[0]
        for j in range(1, k):
            s = s + s_ref[j]
        o_ref[...] = s

    return pl.pallas_call(body, out_shape=jax.ShapeDtypeStruct((r, c), F32), name=name,
                          compiler_params=_params())(slots)


def _mesh_pos():
    return lax.axis_index("x"), lax.axis_index("y"), lax.axis_index("c")


def _other_chips(x, y):
    return [(1 - x, y), (x, 1 - y), (1 - x, 1 - y)]


def _gather_weights(bufs):
    n = len(bufs)
    half = [b.shape[1] // 2 for b in bufs]

    def body(*refs):
        gathered = refs[n:2 * n]
        ici_send, ici_recv, d2d_send, d2d_recv = refs[2 * n:]
        x, y, c = _mesh_pos()
        me = 2 * x + y
        chips = _other_chips(x, y)

        def part(a, block, core):
            return gathered[a].at[block, pl.ds(core * half[a], half[a])]

        def over_ici(a, k, block):
            px, py = chips[k]
            return pltpu.make_async_remote_copy(
                src_ref=part(a, block, c), dst_ref=part(a, block, c),
                send_sem=ici_send.at[a, k], recv_sem=ici_recv.at[a, k],
                device_id=(px, py, c), device_id_type=MESH)

        def over_d2d(a, k, core):
            px, py = chips[k]
            return pltpu.make_async_remote_copy(
                src_ref=part(a, 2 * px + py, core), dst_ref=part(a, 2 * px + py, core),
                send_sem=d2d_send.at[a, k], recv_sem=d2d_recv.at[a, k],
                device_id=(x, y, 1 - c), device_id_type=MESH)

        for a in range(n):
            for k in range(3):
                over_ici(a, k, me).start()
        for a in range(n):
            for k, (px, py) in enumerate(chips):
                over_ici(a, k, 2 * px + py).wait_recv()
                over_d2d(a, k, c).start()
        for a in range(n):
            for k in range(3):
                over_d2d(a, k, 1 - c).wait_recv()
        for a in range(n):
            for k in range(3):
                over_ici(a, k, me).wait_send()
                over_d2d(a, k, c).wait_send()

    return pl.pallas_call(
        body, in_specs=[ANY] * n, out_specs=[ANY] * n,
        out_shape=[jax.ShapeDtypeStruct(b.shape, b.dtype) for b in bufs],
        input_output_aliases={a: a for a in range(n)},
        scratch_shapes=[pltpu.SemaphoreType.DMA((n, 3))] * 4,
        name="gather_weights")(*bufs)


def _send_other_halves(grads):
    n = len(grads)

    def body(*refs):
        srcs = refs[:n]
        dsts = refs[n:2 * n]
        send_sems, recv_sems = refs[2 * n:]
        x, y, c = _mesh_pos()
        copies = [pltpu.make_async_remote_copy(
            src_ref=srcs[a].at[j, 1 - c], dst_ref=dsts[a].at[j], send_sem=send_sems.at[a, j],
            recv_sem=recv_sems.at[a, j], device_id=(x, y, 1 - c), device_id_type=MESH)
            for a in range(n) for j in range(N_CHIPS)]
        for cp in copies:
            cp.start()
        for cp in copies:
            cp.wait()

    return pl.pallas_call(
        body, in_specs=[ANY] * n, out_specs=[ANY] * n,
        out_shape=[jax.ShapeDtypeStruct((N_CHIPS,) + g.shape[2:], F32) for g in grads],
        scratch_shapes=[pltpu.SemaphoreType.DMA((n, N_CHIPS))] * 2,
        name="send_other_halves")(*grads)


def _exchange_grads(blocked, small):
    n = len(blocked)

    def body(*refs):
        srcs = refs[:n]
        small_src = refs[n]
        dsts = refs[n + 1:2 * n + 1]
        small_dst = refs[2 * n + 1]
        send_sems, recv_sems, ssend_sems, srecv_sems, local_sem = refs[2 * n + 2:]
        x, y, c = _mesh_pos()
        chips = _other_chips(x, y)
        my_idx = 4 * x + 2 * y + c
        local = pltpu.make_async_copy(small_src, small_dst.at[my_idx], local_sem)
        local.start()
        others = []
        for r in range(1, 8):
            px = 1 - x if r & 4 else x
            py = 1 - y if r & 2 else y
            pc = 1 - c if r & 1 else c
            others.append((px, py, pc))
        for r, peer in enumerate(others):
            pltpu.make_async_remote_copy(
                src_ref=small_src, dst_ref=small_dst.at[my_idx], send_sem=ssend_sems.at[r],
                recv_sem=srecv_sems.at[r], device_id=peer, device_id_type=MESH).start()
        for a in range(n):
            for k, (px, py) in enumerate(chips):
                pltpu.make_async_remote_copy(
                    src_ref=srcs[a].at[2 * px + py], dst_ref=dsts[a].at[k], send_sem=send_sems.at[a, k],
                    recv_sem=recv_sems.at[a, k], device_id=(px, py, c), device_id_type=MESH).start()
        for r, (px, py, pc) in enumerate(others):
            pltpu.make_async_remote_copy(
                src_ref=small_src, dst_ref=small_dst.at[4 * px + 2 * py + pc], send_sem=ssend_sems.at[r],
                recv_sem=srecv_sems.at[r], device_id=(px, py, pc), device_id_type=MESH).wait()
        for a in range(n):
            for k, (px, py) in enumerate(chips):
                pltpu.make_async_remote_copy(
                    src_ref=srcs[a].at[2 * px + py], dst_ref=dsts[a].at[k], send_sem=send_sems.at[a, k],
                    recv_sem=recv_sems.at[a, k], device_id=(px, py, c), device_id_type=MESH).wait()
        local.wait()

    out_shape = [jax.ShapeDtypeStruct((3,) + b.shape[1:], b.dtype) for b in blocked]
    out_shape.append(jax.ShapeDtypeStruct((8,) + small.shape, F32))
    return pl.pallas_call(
        body, in_specs=[ANY] * (n + 1), out_specs=[ANY] * (n + 1), out_shape=out_shape,
        scratch_shapes=[pltpu.SemaphoreType.DMA((n, 3)), pltpu.SemaphoreType.DMA((n, 3)),
                        pltpu.SemaphoreType.DMA((7,)), pltpu.SemaphoreType.DMA((7,)),
                        pltpu.SemaphoreType.DMA],
        name="exchange_grads")(*blocked, small)


def _join_halves(bufs):
    n = len(bufs)

    def body(*refs):
        joined = refs[n:2 * n]
        send_sems, recv_sems = refs[2 * n:]
        x, y, c = _mesh_pos()
        for a in range(n):
            pltpu.make_async_remote_copy(
                src_ref=joined[a].at[c], dst_ref=joined[a].at[c], send_sem=send_sems.at[a],
                recv_sem=recv_sems.at[a], device_id=(x, y, 1 - c), device_id_type=MESH).start()
        for a in range(n):
            pltpu.make_async_remote_copy(
                src_ref=joined[a].at[c], dst_ref=joined[a].at[1 - c], send_sem=send_sems.at[a],
                recv_sem=recv_sems.at[a], device_id=(x, y, 1 - c), device_id_type=MESH).wait()

    return pl.pallas_call(
        body, in_specs=[ANY] * n, out_specs=[ANY] * n,
        out_shape=[jax.ShapeDtypeStruct(b.shape, b.dtype) for b in bufs],
        input_output_aliases={a: a for a in range(n)},
        scratch_shapes=[pltpu.SemaphoreType.DMA((n,))] * 2,
        name="join_halves")(*bufs)


def kernel(x, meta_tokens, norm_g, w_in, conv_w, conv_b, ln_g, ln_b, w_conv_out, lb_logits, gnorm_g, w_rec_out, w_out, final_g, loss_target, m_meta_tokens, m_norm_g, m_w_in, m_conv_w, m_conv_b, m_ln_g, m_ln_b, m_w_conv_out, m_lb_logits, m_gnorm_g, m_w_rec_out, m_w_out, m_final_g, v_meta_tokens, v_norm_g, v_w_in, v_conv_w, v_conv_b, v_ln_g, v_ln_b, v_w_conv_out, v_lb_logits, v_gnorm_g, v_w_rec_out, v_w_out, v_final_g):
    seq, d = x.shape[1], x.shape[2]
    n_meta = meta_tokens.shape[0]
    n_pad = CHUNK - n_meta
    ds = d // N_CHIPS
    chip = 2 * lax.axis_index("x") + lax.axis_index("y")

    conv_w_pad = jnp.pad(conv_w[0], ((0, HALO - CONV_WIDTH), (0, 0)))
    chip_idx = chip.astype(jnp.int32).reshape(1)
    win_g, sq_g, small_g = _gather_weights([
        _place_shard("place_w_in", w_in[0], chip_idx, BF16),
        _place_shard("place_square", jnp.concatenate([w_conv_out[0], w_rec_out[0], w_out[0]], axis=0),
                     chip_idx, BF16),
        _place_shard("place_small", jnp.concatenate([conv_w_pad, meta_tokens], axis=0), chip_idx, F32)])
    wc_full = sq_g[:, 0:ds].reshape(d, d)
    wr_full = sq_g[:, ds:2 * ds].reshape(d, d)
    wo_full = sq_g[:, 2 * ds:3 * ds].reshape(d, d)
    cw_full = jnp.transpose(small_g[:, 0:HALO], (1, 0, 2)).reshape(HALO, d)
    meta_full = jnp.transpose(small_g[:, HALO:HALO + n_meta], (1, 0, 2)).reshape(n_meta, d)

    hres = jnp.concatenate([jnp.zeros((n_pad, d), F32), meta_full, x[0]], axis=0)
    target = jnp.pad(loss_target[0], ((CHUNK, 0), (0, 0)))
    final_g2 = final_g.reshape(1, d)
    h = _rmsnorm_fwd(hres, norm_g)
    proj = _in_proj(h, win_g)
    c, yc_in, y_conv = _conv_fwd(proj, cw_full, conv_b, ln_g, ln_b, wc_full)
    o, s_all = _hgrn_fwd(proj, lb_logits, n_pad)
    yr_in, merged, y_rec, dout, loss_acc, dfinal_g = _tail_fwd(
        o, proj, y_conv, hres, target, gnorm_g, final_g2, wr_full, wo_full)

    (dyc, dyr, dout_bf, dz, dproj, do, dc, dgnorm_g, dln_g, dln_b) = _tail_bwd(
        dout, proj, y_conv, y_rec, o, c, wo_full, wr_full, wc_full, ln_g, ln_b, gnorm_g)
    dproj, dlb_logits = _hgrn_bwd(proj, do, s_all, lb_logits, n_pad, dproj)
    dproj, dconv_w, dconv_b = _conv_bwd(dc, proj, cw_full, dz, dproj)
    g_win = _weight_grad(h, dproj, "grad_w_in", True)
    g_wc = _weight_grad(yc_in, dyc, "grad_w_conv_out", False)
    g_wr = _weight_grad(yr_in, dyr, "grad_w_rec_out", False)
    g_wo = _weight_grad(merged, dout_bf, "grad_w_out", False)
    dhres, dnorm_g = _in_proj_bwd(dproj, win_g, hres, norm_g, dout)
    grad_x = dhres[CHUNK:][None]

    small = jnp.concatenate([dnorm_g, dconv_b, dln_g, dln_b, dlb_logits, dgnorm_g, dfinal_g,
                             dhres[n_pad:CHUNK], dconv_w[:CONV_WIDTH],
                             jnp.zeros((1, d), F32)], axis=0)
    g_sq = jnp.concatenate([g.reshape(N_CHIPS, ds, d) for g in (g_wc, g_wr, g_wo)], axis=1)
    grads = [g.reshape(N_CHIPS, 2, g.shape[1] // 2, g.shape[2]) for g in (g_win, g_sq)]
    core = lax.axis_index("c").astype(jnp.int32).reshape(1)
    from_sibling = _send_other_halves(grads)
    chip_sums = [_chip_half_sum("chip_half_sum_" + nm, g, r, core)
                 for nm, g, r in zip(("w_in", "square"), grads, from_sibling)]
    recv_win, recv_sq, small_slots = _exchange_grads([s[1] for s in chip_sums], small)
    chip_core = jnp.concatenate([chip_idx, core])
    totals = [_block_half_total("block_half_total_" + nm, s[0], r, chip_core)
              for nm, s, r in zip(("w_in", "square"), chip_sums, (recv_win, recv_sq))]
    gt_win, gt_sq = [t.reshape(2 * t.shape[1], t.shape[2]) for t in _join_halves(totals)]
    small_sum = _sum_slots("sum_small", small_slots)

    res = {}
    res["w_in"] = _adamw("adamw_w_in", w_in[0], m_w_in[0], v_w_in[0], gt_win)
    res["w_conv_out"] = _adamw("adamw_w_conv_out", w_conv_out[0], m_w_conv_out[0], v_w_conv_out[0], gt_sq[0:ds])
    res["w_rec_out"] = _adamw("adamw_w_rec_out", w_rec_out[0], m_w_rec_out[0], v_w_rec_out[0], gt_sq[ds:2 * ds])
    res["w_out"] = _adamw("adamw_w_out", w_out[0], m_w_out[0], v_w_out[0], gt_sq[2 * ds:3 * ds])
    big = {k: tuple(a[None] for a in v) for k, v in res.items()}

    rep_names = ("norm_g", "conv_b", "ln_g", "ln_b", "lb_logits", "gnorm_g", "final_g")
    rep_w = (norm_g, conv_b, ln_g, ln_b, lb_logits, gnorm_g, final_g2)
    rep_m = (m_norm_g, m_conv_b, m_ln_g, m_ln_b, m_lb_logits, m_gnorm_g, m_final_g.reshape(1, d))
    rep_v = (v_norm_g, v_conv_b, v_ln_g, v_ln_b, v_lb_logits, v_gnorm_g, v_final_g.reshape(1, d))
    rep = _adamw("adamw_replicated", jnp.concatenate(rep_w, 0), jnp.concatenate(rep_m, 0),
                 jnp.concatenate(rep_v, 0), small_sum[0:8])
    rep_rows = {"norm_g": (0, 1), "conv_b": (1, 2), "ln_g": (2, 3), "ln_b": (3, 4), "lb_logits": (4, 6),
                "gnorm_g": (6, 7), "final_g": (7, 8)}
    small_out = {}
    for nm in rep_names:
        lo, hi = rep_rows[nm]
        vals = tuple(a[lo:hi] for a in rep)
        if nm == "final_g":
            vals = tuple(a.reshape(d) for a in vals)
        small_out[nm] = vals
    cw_row = 8 + n_meta
    g_meta = lax.dynamic_slice_in_dim(small_sum[8:cw_row], chip * ds, ds, axis=1)
    small_out["meta_tokens"] = _adamw("adamw_meta", meta_tokens, m_meta_tokens, v_meta_tokens, g_meta)
    g_cw = lax.dynamic_slice_in_dim(small_sum[cw_row:cw_row + HALO], chip * ds, ds, axis=1)
    pad_rows = ((0, HALO - CONV_WIDTH), (0, 0))
    cw_res = _adamw("adamw_conv_w", conv_w_pad, jnp.pad(m_conv_w[0], pad_rows),
                    jnp.pad(v_conv_w[0], pad_rows, constant_values=1.0), g_cw)
    small_out["conv_w"] = tuple(a[:CONV_WIDTH][None] for a in cw_res)

    loss = lax.psum(loss_acc[0, 0], ("x", "y", "c"))

    order = ("meta_tokens", "norm_g", "w_in", "conv_w", "conv_b", "ln_g", "ln_b", "w_conv_out", "lb_logits",
             "gnorm_g", "w_rec_out", "w_out", "final_g")
    allres = {**big, **small_out}
    outs = [loss, grad_x]
    for field in range(4):
        outs.extend(allres[nm][field] for nm in order)
    return tuple(outs)
```

```python
import numpy as np

import jax
import jax.numpy as jnp
from jax import lax
from jax.experimental import pallas as pl
from jax.experimental.pallas import tpu as pltpu

F32 = jnp.float32
BF16 = jnp.bfloat16

EPS = 1e-6
CHUNK = 64
N_LEVELS = 6
CONV_WIDTH = 31
HALO = 32
CONV_ROWS = 32
CONV_LANES = 256
HEAD = 128
N_CHIPS = 4
VMEM_LIMIT_BYTES = 56 * 1024 * 1024

ADAM_LR = 0.001
ADAM_B1 = 0.9
ADAM_B2 = 0.999
ADAM_EPS = 1e-08
ADAM_WD = 0.01
ADAM_STEP = 10

MESH = pl.DeviceIdType.MESH
ANY = pl.BlockSpec(memory_space=pl.ANY)

NT = (((1,), (1,)), ((), ()))
TN = (((0,), (0,)), ((), ()))


def _params(**kw):
    return pltpu.CompilerParams(vmem_limit_bytes=VMEM_LIMIT_BYTES, **kw)


def _sigmoid(x):
    return jax.nn.sigmoid(x)


def _dsilu(x, s):
    return s * (1.0 + x * (1.0 - s))


def _row_tile(lp):
    for t in (320, 256, 192, 128, 64):
        if lp % t == 0:
            return t
    raise ValueError(f"unsupported padded length {lp}")


def _mm_row_tile(lp):
    for t in (832, 640, 320, 256, 192, 128, 64):
        if lp % t == 0:
            return t
    raise ValueError(f"unsupported padded length {lp}")


def _dot3(m_bf16, x):
    hi = x.astype(BF16)
    r1 = x - hi.astype(F32)
    mid = r1.astype(BF16)
    lo = (r1 - mid.astype(F32)).astype(BF16)
    return (jnp.dot(m_bf16, hi, preferred_element_type=F32)
            + jnp.dot(m_bf16, mid, preferred_element_type=F32)
            + jnp.dot(m_bf16, lo, preferred_element_type=F32))


def _dot2(m_bf16, x):
    hi = x.astype(BF16)
    lo = (x - hi.astype(F32)).astype(BF16)
    return (jnp.dot(m_bf16, hi, preferred_element_type=F32)
            + jnp.dot(m_bf16, lo, preferred_element_type=F32))


def _col_to_row(col):
    return jnp.broadcast_to(col, (HEAD, 8)).T[0:1, :]


def _row_to_col(row):
    return jnp.broadcast_to(row, (8, HEAD)).T[:, 0:1]


def _hgrn_tables():
    t = np.arange(CHUNK)
    ltri = (t[None, :] <= t[:, None]).astype(np.float32)
    mats = [ltri]
    for lvl in range(1, N_LEVELS + 1):
        blk = CHUNK >> (lvl - 1)
        mid = (t // blk) * blk + blk // 2
        mats.append(ltri[mid - 1])
    after = (t[None, :] >= t[:, None]).astype(np.float32)
    before = (t[None, :] < t[:, None]).astype(np.float32)
    return jnp.asarray(np.concatenate(mats, 0), BF16), jnp.asarray(np.concatenate([after, before], 1), BF16)


def _rmsnorm_fwd(hres, g):
    lp, d = hres.shape
    tm = _row_tile(lp)

    def body(x_ref, g_ref, h_ref):
        x = x_ref[...]
        r = lax.rsqrt(jnp.mean(x * x, axis=-1, keepdims=True) + EPS)
        h_ref[...] = (x * r * g_ref[...]).astype(BF16)

    return pl.pallas_call(
        body, grid=(lp // tm,),
        in_specs=[pl.BlockSpec((tm, d), lambda i: (i, 0)), pl.BlockSpec((1, d), lambda i: (0, 0))],
        out_specs=pl.BlockSpec((tm, d), lambda i: (i, 0)),
        out_shape=jax.ShapeDtypeStruct((lp, d), BF16),
        name="rmsnorm_fwd", compiler_params=_params())(hres, g)


def _in_proj(h, wg):
    lp, d = h.shape
    _, _, ncol = wg.shape
    tm = _mm_row_tile(lp)
    nt = 3
    tn = ncol // nt

    def body(h_ref, w_ref, o_ref):
        o_ref[...] = jnp.dot(h_ref[...], w_ref[0], preferred_element_type=F32)

    return pl.pallas_call(
        body, grid=(N_CHIPS, nt, lp // tm),
        in_specs=[pl.BlockSpec((tm, d), lambda j, n, i: (i, 0)),
                  pl.BlockSpec((1, d, tn), lambda j, n, i: (j, 0, n))],
        out_specs=pl.BlockSpec((tm, tn), lambda j, n, i: (i, j * nt + n)),
        out_shape=jax.ShapeDtypeStruct((lp, N_CHIPS * ncol), F32),
        name="in_proj", compiler_params=_params())(h, wg)


def _conv_fwd(proj, conv_w, conv_b, ln_g, ln_b, w_conv):
    lp = proj.shape[0]
    d = conv_b.shape[1]
    tm = _row_tile(lp)
    hb = tm // HALO

    def body(ua_ref, ub_ref, z_ref, uap_ref, ubp_ref, cw_ref, cb_ref, lg_ref, lb_ref, w_ref,
             c_ref, ycin_ref, yconv_ref, aext_ref):
        i = pl.program_id(0)
        a_prev = uap_ref[...] * _sigmoid(ubp_ref[...])
        aext_ref[0:HALO, :] = jnp.where(i > 0, a_prev, 0.0)
        aext_ref[HALO:HALO + tm, :] = ua_ref[...] * _sigmoid(ub_ref[...])

        def row_block(r, carry):
            r0 = pl.multiple_of(r * CONV_ROWS, CONV_ROWS)
            for cs in range(d // CONV_LANES):
                cl = slice(cs * CONV_LANES, (cs + 1) * CONV_LANES)
                blk = aext_ref[pl.ds(r0, CONV_ROWS + HALO), cl]
                acc = jnp.zeros((CONV_ROWS, CONV_LANES), F32) + cb_ref[:, cl]
                for b in range(8):
                    sh = blk if b == 0 else pltpu.roll(blk, CONV_ROWS + HALO - b, axis=0)
                    for a in range(5):
                        j = 8 * a + b - 2
                        if 0 <= j < CONV_WIDTH:
                            acc = acc + cw_ref[j:j + 1, cl] * sh[8 * a:8 * a + CONV_ROWS, :]
                c_ref[pl.ds(r0, CONV_ROWS), cl] = acc
            return carry

        lax.fori_loop(0, tm // CONV_ROWS, row_block, 0)

        c = c_ref[...]
        mu = jnp.mean(c, axis=-1, keepdims=True)
        xc = c - mu
        rstd = lax.rsqrt(jnp.mean(xc * xc, axis=-1, keepdims=True) + EPS)
        ln = xc * rstd * lg_ref[...] + lb_ref[...]
        s = ln * _sigmoid(ln)
        z = z_ref[...]
        ycin = (s * (z * _sigmoid(z))).astype(BF16)
        ycin_ref[...] = ycin
        yconv_ref[...] = jnp.dot(ycin, w_ref[...], preferred_element_type=F32)

    row = lambda p: pl.BlockSpec((tm, d), lambda i, p=p: (i, p))
    halo = lambda p: pl.BlockSpec((HALO, d), lambda i, p=p: (jnp.maximum(i * hb - 1, 0), p))
    vec = pl.BlockSpec((1, d), lambda i: (0, 0))
    return pl.pallas_call(
        body, grid=(lp // tm,),
        in_specs=[row(0), row(1), row(2), halo(0), halo(1),
                  pl.BlockSpec((HALO, d), lambda i: (0, 0)), vec, vec, vec,
                  pl.BlockSpec((d, d), lambda i: (0, 0))],
        out_specs=[pl.BlockSpec((tm, d), lambda i: (i, 0))] * 3,
        out_shape=[jax.ShapeDtypeStruct((lp, d), F32), jax.ShapeDtypeStruct((lp, d), BF16),
                   jax.ShapeDtypeStruct((lp, d), F32)],
        scratch_shapes=[pltpu.VMEM((HALO + tm, d), F32)],
        name="conv_fwd", compiler_params=_params())(
            proj, proj, proj, proj, proj, conv_w, conv_b, ln_g, ln_b, w_conv)


def _lower_bound(lbl_ref):
    l0 = lbl_ref[0:1, :]
    l1 = lbl_ref[1:2, :]
    m = jnp.maximum(l0, l1)
    e0 = jnp.exp(l0 - m)
    e1 = jnp.exp(l1 - m)
    p0 = e0 / (e0 + e1)
    return p0, p0 * (e1 / (e0 + e1))


def _level_masks():
    rid = lax.broadcasted_iota(jnp.int32, (CHUNK, 1), 0)
    r2 = lax.broadcasted_iota(jnp.int32, (CHUNK, CHUNK), 0)
    c2 = lax.broadcasted_iota(jnp.int32, (CHUNK, CHUNK), 1)
    out = []
    for lvl in range(1, N_LEVELS + 1):
        blk = CHUNK >> (lvl - 1)
        sh = blk.bit_length() - 1
        upper = (rid & (blk - 1)) >= (blk // 2)
        same = (r2 >> sh) == (c2 >> sh)
        out.append((upper, same))
    return out


def _gates(qr, fr, lb, valid):
    sq = _sigmoid(qr)
    q = qr * sq
    sf = _sigmoid(fr)
    f = lb + (1.0 - lb) * sf
    g = jnp.where(valid, jnp.log(f), 0.0)
    k = jnp.where(valid, 1.0 - f, 0.0)
    return q, sq, f, sf, g, k


def _level_factors(b, r, upper):
    e = jnp.exp(jnp.where(upper, b - r, r - b))
    return jnp.where(upper, e, 0.0), jnp.where(upper, 0.0, e)


def _hgrn_fwd(proj, lb_logits, n_pad):
    lp = proj.shape[0]
    d = lb_logits.shape[1]
    n_heads = d // HEAD
    nc = lp // CHUNK
    tab, _ = _hgrn_tables()
    n_tab = tab.shape[0]

    def body(qr_ref, fr_ref, ir_ref, lbl_ref, tab_ref, o_ref, sall_ref, s_ref, t_ref):
        n = pl.program_id(0)

        @pl.when(n == 0)
        def _():
            s_ref[...] = jnp.zeros_like(s_ref)

        sall_ref[0] = s_ref[...]
        lb_all, _ = _lower_bound(lbl_ref)
        rid = lax.broadcasted_iota(jnp.int32, (CHUNK, 1), 0)
        valid = jnp.logical_or(n > 0, rid >= n_pad)
        f_all = lb_all + (1.0 - lb_all) * _sigmoid(fr_ref[...])
        t_ref[...] = _dot2(tab_ref[...], jnp.where(valid, jnp.log(f_all), 0.0))
        masks = _level_masks()

        def head(h):
            off = h * HEAD
            hs = pl.ds(off, HEAD)
            lb = _lower_bound_slice(lbl_ref, hs)
            q, _, _, _, _, k = _gates(qr_ref[:, hs], fr_ref[:, hs], lb, valid)
            v = ir_ref[:, hs]
            b = t_ref[0:CHUNK, hs]
            s0 = s_ref[hs, :]
            o = jnp.dot((q * jnp.exp(b)).astype(BF16), s0.astype(BF16), preferred_element_type=F32)
            o = o + jnp.sum(q * k, axis=-1, keepdims=True) * v
            a = jnp.zeros((CHUNK, CHUNK), F32)
            for lvl in range(1, N_LEVELS + 1):
                upper, same = masks[lvl - 1]
                eq, ek = _level_factors(b, t_ref[CHUNK * lvl:CHUNK * (lvl + 1), hs], upper)
                p = lax.dot_general((q * eq).astype(BF16), (k * ek).astype(BF16), NT, preferred_element_type=F32)
                a = a + jnp.where(same, p, 0.0)
            vb = v.astype(BF16)
            o_ref[:, hs] = o + jnp.dot(a.astype(BF16), vb, preferred_element_type=F32)
            b_last = t_ref[CHUNK - 1:CHUNK, hs]
            khat = (k * jnp.exp(b_last - b)).astype(BF16)
            s_ref[hs, :] = _row_to_col(jnp.exp(b_last)) * s0 + lax.dot_general(khat, vb, TN, preferred_element_type=F32)
        for h in range(n_heads):
            head(h)

    piece = lambda p: pl.BlockSpec((CHUNK, d), lambda n, p=p: (n, p))
    return pl.pallas_call(
        body, grid=(nc,),
        in_specs=[piece(3), piece(4), piece(5), pl.BlockSpec((2, d), lambda n: (0, 0)),
                  pl.BlockSpec((n_tab, CHUNK), lambda n: (0, 0))],
        out_specs=[pl.BlockSpec((CHUNK, d), lambda n: (n, 0)), pl.BlockSpec((1, d, HEAD), lambda n: (n, 0, 0))],
        out_shape=[jax.ShapeDtypeStruct((lp, d), F32), jax.ShapeDtypeStruct((nc, d, HEAD), F32)],
        scratch_shapes=[pltpu.VMEM((d, HEAD), F32), pltpu.VMEM((n_tab, d), F32)],
        name="hgrn_fwd", compiler_params=_params())(proj, proj, proj, lb_logits, tab)


def _lower_bound_slice(lbl_ref, hs):
    l0 = lbl_ref[0:1, hs]
    l1 = lbl_ref[1:2, hs]
    m = jnp.maximum(l0, l1)
    e0 = jnp.exp(l0 - m)
    e1 = jnp.exp(l1 - m)
    return e0 / (e0 + e1)


def _tail_fwd(o, proj, y_conv, hres, target, gnorm_g, final_g, w_rec, w_out):
    lp, d = o.shape
    n_heads = d // HEAD
    tm = _row_tile(lp)

    def body(o_ref, gr_ref, mc_ref, mr_ref, yc_ref, x_ref, t_ref, gn_ref, fg_ref, wr_ref, wo_ref,
             yrin_ref, mg_ref, yrec_ref, dout_ref, loss_ref, dfg_ref):
        i = pl.program_id(0)

        @pl.when(i == 0)
        def _():
            loss_ref[...] = jnp.zeros_like(loss_ref)
            dfg_ref[...] = jnp.zeros_like(dfg_ref)

        for h in range(n_heads):
            hs = slice(h * HEAD, (h + 1) * HEAD)
            oh = o_ref[:, hs]
            on = oh * lax.rsqrt(jnp.mean(oh * oh, axis=-1, keepdims=True) + EPS) * gn_ref[:, hs]
            gr = gr_ref[:, hs]
            yrin_ref[:, hs] = (on * (gr * _sigmoid(gr))).astype(BF16)
        yrec = jnp.dot(yrin_ref[...], wr_ref[...], preferred_element_type=F32)
        yrec_ref[...] = yrec
        merged = (_sigmoid(mc_ref[...]) * yc_ref[...] + _sigmoid(mr_ref[...]) * yrec).astype(BF16)
        mg_ref[...] = merged
        out = x_ref[...] + jnp.dot(merged, wo_ref[...], preferred_element_type=F32)
        r = lax.rsqrt(jnp.mean(out * out, axis=-1, keepdims=True) + EPS)
        yhat = out * r
        fg = fg_ref[...]
        rid = lax.broadcasted_iota(jnp.int32, (tm, 1), 0) + i * tm
        err = jnp.where(rid >= CHUNK, yhat * fg - t_ref[...], 0.0)
        loss_ref[...] += 0.5 * jnp.sum(err * err) / d
        dy = err / d
        dfg_ref[...] += jnp.sum(dy * yhat, axis=0, keepdims=True)
        dyh = dy * fg
        dout_ref[...] = r * (dyh - yhat * jnp.mean(dyh * yhat, axis=-1, keepdims=True))

    row = lambda p: pl.BlockSpec((tm, d), lambda i, p=p: (i, p))
    vec = pl.BlockSpec((1, d), lambda i: (0, 0))
    mat = pl.BlockSpec((d, d), lambda i: (0, 0))
    return pl.pallas_call(
        body, grid=(lp // tm,),
        in_specs=[row(0), row(6), row(7), row(8), row(0), row(0), row(0), vec, vec, mat, mat],
        out_specs=[row(0), row(0), row(0), row(0), pl.BlockSpec((8, 128), lambda i: (0, 0)), vec],
        out_shape=[jax.ShapeDtypeStruct((lp, d), BF16), jax.ShapeDtypeStruct((lp, d), BF16),
                   jax.ShapeDtypeStruct((lp, d), F32), jax.ShapeDtypeStruct((lp, d), F32),
                   jax.ShapeDtypeStruct((8, 128), F32), jax.ShapeDtypeStruct((1, d), F32)],
        name="tail_fwd", compiler_params=_params())(
            o, proj, proj, proj, y_conv, hres, target, gnorm_g, final_g, w_rec, w_out)


def _tail_bwd(dout, proj, y_conv, y_rec, o, c, w_out, w_rec, w_conv, ln_g, ln_b, gnorm_g):
    lp, d = dout.shape
    n_heads = d // HEAD
    tm = _row_tile(lp)

    def body(dout_ref, mc_ref, mr_ref, z_ref, gr_ref, yc_ref, yrec_ref, o_ref, c_ref,
             wo_ref, wr_ref, wc_ref, lg_ref, lb_ref, gn_ref,
             dyc_ref, dyr_ref, doutb_ref, dz_ref, dp_ref, do_ref, dc_ref,
             dgn_ref, dlg_ref, dlb_ref, dyrin_ref):
        i = pl.program_id(0)

        @pl.when(i == 0)
        def _():
            dgn_ref[...] = jnp.zeros_like(dgn_ref)
            dlg_ref[...] = jnp.zeros_like(dlg_ref)
            dlb_ref[...] = jnp.zeros_like(dlb_ref)

        doutb = dout_ref[...].astype(BF16)
        doutb_ref[...] = doutb
        dmerged = lax.dot_general(doutb, wo_ref[...], NT, preferred_element_type=F32)
        smc = _sigmoid(mc_ref[...])
        smr = _sigmoid(mr_ref[...])
        dyc = (dmerged * smc).astype(BF16)
        dyr = (dmerged * smr).astype(BF16)
        dyc_ref[...] = dyc
        dyr_ref[...] = dyr
        dp_ref[:, d:2 * d] = (dmerged * yc_ref[...] * smc * (1.0 - smc)).astype(BF16)
        dp_ref[:, 2 * d:3 * d] = (dmerged * yrec_ref[...] * smr * (1.0 - smr)).astype(BF16)

        dyrin_ref[...] = lax.dot_general(dyr, wr_ref[...], NT, preferred_element_type=F32)
        for h in range(n_heads):
            hs = slice(h * HEAD, (h + 1) * HEAD)
            oh = o_ref[:, hs]
            rstd = lax.rsqrt(jnp.mean(oh * oh, axis=-1, keepdims=True) + EPS)
            ohat = oh * rstd
            gn = gn_ref[:, hs]
            gr = gr_ref[:, hs]
            sg = _sigmoid(gr)
            dyrin = dyrin_ref[:, hs]
            don = dyrin * (gr * sg)
            dp_ref[:, hs] = (dyrin * (ohat * gn) * _dsilu(gr, sg)).astype(BF16)
            dgn_ref[:, hs] += jnp.sum(don * ohat, axis=0, keepdims=True)
            doh = don * gn
            do_ref[:, hs] = rstd * (doh - ohat * jnp.mean(doh * ohat, axis=-1, keepdims=True))

        dycin = lax.dot_general(dyc, wc_ref[...], NT, preferred_element_type=F32)
        c = c_ref[...]
        mu = jnp.mean(c, axis=-1, keepdims=True)
        xc = c - mu
        rstd = lax.rsqrt(jnp.mean(xc * xc, axis=-1, keepdims=True) + EPS)
        nrm = xc * rstd
        lg = lg_ref[...]
        ln = nrm * lg + lb_ref[...]
        sl = _sigmoid(ln)
        z = z_ref[...]
        sz = _sigmoid(z)
        dz_ref[...] = (dycin * (ln * sl) * _dsilu(z, sz)).astype(BF16)
        dln = dycin * (z * sz) * _dsilu(ln, sl)
        dlg_ref[...] += jnp.sum(dln * nrm, axis=0, keepdims=True)
        dlb_ref[...] += jnp.sum(dln, axis=0, keepdims=True)
        dn = dln * lg
        dc_ref[...] = rstd * (dn - jnp.mean(dn, axis=-1, keepdims=True)
                              - nrm * jnp.mean(dn * nrm, axis=-1, keepdims=True))

    row = lambda p: pl.BlockSpec((tm, d), lambda i, p=p: (i, p))
    vec = pl.BlockSpec((1, d), lambda i: (0, 0))
    mat = pl.BlockSpec((d, d), lambda i: (0, 0))
    act_bf = jax.ShapeDtypeStruct((lp, d), BF16)
    act_f32 = jax.ShapeDtypeStruct((lp, d), F32)
    vec_f32 = jax.ShapeDtypeStruct((1, d), F32)
    return pl.pallas_call(
        body, grid=(lp // tm,),
        in_specs=[row(0), row(7), row(8), row(2), row(6), row(0), row(0), row(0), row(0),
                  mat, mat, mat, vec, vec, vec],
        out_specs=[row(0)] * 4 + [pl.BlockSpec((tm, 3 * d), lambda i: (i, 2))] + [row(0)] * 2 + [vec] * 3,
        out_shape=[act_bf] * 4 + [jax.ShapeDtypeStruct((lp, 9 * d), BF16)] + [act_f32] * 2 + [vec_f32] * 3,
        scratch_shapes=[pltpu.VMEM((tm, d), F32)],
        name="tail_bwd", compiler_params=_params())(
            dout, proj, proj, proj, proj, y_conv, y_rec, o, c, w_out, w_rec, w_conv, ln_g, ln_b, gnorm_g)


def _hgrn_bwd(proj, do, s_all, lb_logits, n_pad, dproj):
    lp, d = do.shape
    n_heads = d // HEAD
    nc = lp // CHUNK
    tab, utri = _hgrn_tables()
    n_tab = tab.shape[0]

    def body(qr_ref, fr_ref, ir_ref, do_ref, s0_ref, lbl_ref, tab_ref, ut_ref, _,
             dp_ref, dlbl_ref, ds_ref, t_ref, dlb_ref):
        n = pl.program_id(0)
        chunk = nc - 1 - n

        @pl.when(n == 0)
        def _():
            ds_ref[...] = jnp.zeros_like(ds_ref)
            dlb_ref[...] = jnp.zeros_like(dlb_ref)

        lb_all, pp = _lower_bound(lbl_ref)
        rid = lax.broadcasted_iota(jnp.int32, (CHUNK, 1), 0)
        valid = jnp.logical_or(chunk > 0, rid >= n_pad)
        f_all = lb_all + (1.0 - lb_all) * _sigmoid(fr_ref[...])
        t_ref[...] = _dot2(tab_ref[...], jnp.where(valid, jnp.log(f_all), 0.0))
        masks = _level_masks()
        ut = ut_ref[...]

        def head(h):
            off = h * HEAD
            hs = pl.ds(off, HEAD)
            lb = _lower_bound_slice(lbl_ref, hs)
            qr = qr_ref[:, hs]
            q, sq, f, sf, _, k = _gates(qr, fr_ref[:, hs], lb, valid)
            v = ir_ref[:, hs]
            do_h = do_ref[:, hs]
            b = t_ref[0:CHUNK, hs]
            b_last = t_ref[CHUNK - 1:CHUNK, hs]
            s0 = s0_ref[0, hs, :]
            ds1 = ds_ref[hs, :]
            eb = jnp.exp(b)
            ekl = jnp.exp(b_last - b)
            do_bf = do_h.astype(BF16)
            v_bf = v.astype(BF16)
            ds1_bf = ds1.astype(BF16)

            da = lax.dot_general(do_bf, v_bf, NT, preferred_element_type=F32)
            da_diag = jnp.sum(do_h * v, axis=-1, keepdims=True)
            a = jnp.zeros((CHUNK, CHUNK), F32)
            dq_x = eb * lax.dot_general(do_bf, s0.astype(BF16), NT, preferred_element_type=F32)
            dk_x = ekl * lax.dot_general(v_bf, ds1_bf, NT, preferred_element_type=F32)
            x_after = q * dq_x
            x_before = k * dk_x
            for lvl in range(1, N_LEVELS + 1):
                upper, same = masks[lvl - 1]
                eq, ek = _level_factors(b, t_ref[CHUNK * lvl:CHUNK * (lvl + 1), hs], upper)
                qt = (q * eq).astype(BF16)
                kt = (k * ek).astype(BF16)
                p = lax.dot_general(qt, kt, NT, preferred_element_type=F32)
                a = a + jnp.where(same, p, 0.0)
                dam = jnp.where(same, da, 0.0).astype(BF16)
                dqt = jnp.dot(dam, kt, preferred_element_type=F32)
                dkt = lax.dot_general(dam, qt, TN, preferred_element_type=F32)
                dq_x = dq_x + eq * dqt
                dk_x = dk_x + ek * dkt
                x_after = x_after + (qt.astype(F32) * dqt - kt.astype(F32) * dkt)

            dv = (lax.dot_general(a.astype(BF16), do_bf, TN, preferred_element_type=F32)
                  + jnp.sum(q * k, axis=-1, keepdims=True) * do_h
                  + jnp.dot((k * ekl).astype(BF16), ds1_bf, preferred_element_type=F32))
            dp_ref[:, pl.ds(2 * d + off, HEAD)] = dv.astype(BF16)

            carried = jnp.exp(b_last) * _col_to_row(jnp.sum(s0 * ds1, axis=-1, keepdims=True))
            dg = _dot3(ut, jnp.concatenate([x_after, x_before], axis=0)) + carried
            dq = dq_x + da_diag * k
            dk = dk_x + da_diag * q
            dp_ref[:, hs] = (dq * _dsilu(qr, sq)).astype(BF16)
            df = jnp.where(valid, dg / f - dk, 0.0)
            dp_ref[:, pl.ds(d + off, HEAD)] = (df * (1.0 - lb) * sf * (1.0 - sf)).astype(BF16)
            dlb_ref[:, hs] += jnp.sum(df * (1.0 - sf), axis=0, keepdims=True)

            ds_ref[hs, :] = (_row_to_col(jnp.exp(b_last)) * ds1
                             + lax.dot_general((q * eb).astype(BF16), do_bf, TN, preferred_element_type=F32))
        for h in range(n_heads):
            head(h)

        @pl.when(n == nc - 1)
        def _():
            dl0 = dlb_ref[...] * pp
            dlbl_ref[0:1, :] = dl0
            dlbl_ref[1:2, :] = -dl0

    piece = lambda p: pl.BlockSpec((CHUNK, d), lambda n, p=p: (nc - 1 - n, p))
    return pl.pallas_call(
        body, grid=(nc,),
        in_specs=[piece(3), piece(4), piece(5), piece(0),
                  pl.BlockSpec((1, d, HEAD), lambda n: (nc - 1 - n, 0, 0)),
                  pl.BlockSpec((2, d), lambda n: (0, 0)),
                  pl.BlockSpec((n_tab, CHUNK), lambda n: (0, 0)),
                  pl.BlockSpec((CHUNK, 2 * CHUNK), lambda n: (0, 0)), ANY],
        out_specs=[pl.BlockSpec((CHUNK, 3 * d), lambda n: (nc - 1 - n, 1)), pl.BlockSpec((2, d), lambda n: (0, 0))],
        out_shape=[jax.ShapeDtypeStruct(dproj.shape, BF16), jax.ShapeDtypeStruct((2, d), F32)],
        input_output_aliases={8: 0},
        scratch_shapes=[pltpu.VMEM((d, HEAD), F32), pltpu.VMEM((n_tab, d), F32), pltpu.VMEM((1, d), F32)],
        name="hgrn_bwd", compiler_params=_params())(proj, proj, proj, do, s_all, lb_logits, tab, utri, dproj)


def _conv_bwd(dc, proj, conv_w, dz, dproj):
    lp, d = dc.shape
    tm = _row_tile(lp)
    hb = tm // HALO
    n_tiles = lp // tm
    last_halo = lp // HALO - 1

    def body(dc_ref, dcn_ref, ua_ref, ub_ref, uap_ref, ubp_ref, cw_ref, dz_ref, _,
             dp_ref, dcw_ref, dcb_ref, aext_ref, dcext_ref, da_ref):
        i = pl.program_id(0)

        @pl.when(i == 0)
        def _():
            dcw_ref[...] = jnp.zeros_like(dcw_ref)
            dcb_ref[...] = jnp.zeros_like(dcb_ref)

        ua = ua_ref[...]
        sb = _sigmoid(ub_ref[...])
        a_prev = uap_ref[...] * _sigmoid(ubp_ref[...])
        aext_ref[0:HALO, :] = jnp.where(i > 0, a_prev, 0.0)
        aext_ref[HALO:HALO + tm, :] = ua * sb
        dcext_ref[0:tm, :] = dc_ref[...]
        dcext_ref[tm:tm + HALO, :] = jnp.where(i < n_tiles - 1, dcn_ref[...], 0.0)
        dcb_ref[...] += jnp.sum(dc_ref[...], axis=0, keepdims=True)

        def row_block(r, carry):
            r0 = pl.multiple_of(r * CONV_ROWS, CONV_ROWS)
            n_rows = CONV_ROWS + HALO
            for cs in range(d // CONV_LANES):
                cl = slice(cs * CONV_LANES, (cs + 1) * CONV_LANES)
                dblk = dcext_ref[pl.ds(r0, n_rows), cl]
                ablk = aext_ref[pl.ds(r0, n_rows), cl]
                dcur = dblk[0:CONV_ROWS, :]
                acc = jnp.zeros((CONV_ROWS, CONV_LANES), F32)
                for b in range(8):
                    dsh = dblk if b == 0 else pltpu.roll(dblk, n_rows - b, axis=0)
                    ash = ablk if b == 0 else pltpu.roll(ablk, n_rows - b, axis=0)
                    for a in range(5):
                        j_da = CONV_WIDTH - 1 - (8 * a + b)
                        if 0 <= j_da < CONV_WIDTH:
                            acc = acc + cw_ref[j_da:j_da + 1, cl] * dsh[8 * a:8 * a + CONV_ROWS, :]
                        j_w = 8 * a + b - 2
                        if 0 <= j_w < CONV_WIDTH:
                            dcw_ref[j_w:j_w + 1, cl] += jnp.sum(
                                dcur * ash[8 * a:8 * a + CONV_ROWS, :], axis=0, keepdims=True)
                da_ref[pl.ds(r0, CONV_ROWS), cl] = acc
            return carry

        lax.fori_loop(0, tm // CONV_ROWS, row_block, 0)

        da = da_ref[...]
        dp_ref[:, 0:d] = (da * sb).astype(BF16)
        dp_ref[:, d:2 * d] = (da * ua * sb * (1.0 - sb)).astype(BF16)
        dp_ref[:, 2 * d:3 * d] = dz_ref[...]

    row = lambda p: pl.BlockSpec((tm, d), lambda i, p=p: (i, p))
    prev = lambda p: pl.BlockSpec((HALO, d), lambda i, p=p: (jnp.maximum(i * hb - 1, 0), p))
    nxt = pl.BlockSpec((HALO, d), lambda i: (jnp.minimum((i + 1) * hb, last_halo), 0))
    return pl.pallas_call(
        body, grid=(n_tiles,),
        in_specs=[row(0), nxt, row(0), row(1), prev(0), prev(1), pl.BlockSpec((HALO, d), lambda i: (0, 0)),
                  row(0), ANY],
        out_specs=[pl.BlockSpec((tm, 3 * d), lambda i: (i, 0)), pl.BlockSpec((HALO, d), lambda i: (0, 0)),
                   pl.BlockSpec((1, d), lambda i: (0, 0))],
        out_shape=[jax.ShapeDtypeStruct(dproj.shape, BF16),
                   jax.ShapeDtypeStruct((HALO, d), F32), jax.ShapeDtypeStruct((1, d), F32)],
        input_output_aliases={8: 0},
        scratch_shapes=[pltpu.VMEM((HALO + tm, d), F32), pltpu.VMEM((tm + HALO, d), F32), pltpu.VMEM((tm, d), F32)],
        name="conv_bwd", compiler_params=_params())(dc, dc, proj, proj, proj, proj, conv_w, dz, dproj)


def _weight_grad(xs, dy, name, blocked):
    lp, dx = xs.shape
    n = dy.shape[1]
    tk = _mm_row_tile(lp)
    if blocked:
        ncol = n // N_CHIPS
        nt = 3
        tn = ncol // nt
        grid = (N_CHIPS * nt, lp // tk)
        out_spec = pl.BlockSpec((1, dx, tn), lambda c, k: (c // nt, 0, c % nt))
        out_shape = jax.ShapeDtypeStruct((N_CHIPS, dx, ncol), F32)
    else:
        tn = n // 2
        grid = (2, lp // tk)
        out_spec = pl.BlockSpec((dx, tn), lambda c, k: (0, c))
        out_shape = jax.ShapeDtypeStruct((dx, n), F32)

    def body(xs_ref, dy_ref, o_ref):
        @pl.when(pl.program_id(1) == 0)
        def _():
            o_ref[...] = jnp.zeros_like(o_ref)

        p = lax.dot_general(xs_ref[...], dy_ref[...], TN, preferred_element_type=F32)
        if blocked:
            o_ref[0] += p
        else:
            o_ref[...] += p

    return pl.pallas_call(
        body, grid=grid,
        in_specs=[pl.BlockSpec((tk, dx), lambda c, k: (k, 0)), pl.BlockSpec((tk, tn), lambda c, k: (k, c))],
        out_specs=out_spec, out_shape=out_shape,
        name=name, compiler_params=_params())(xs, dy)


def _in_proj_bwd(dproj, wg, hres, norm_g, dout):
    lp, d = hres.shape
    _, _, ncol = wg.shape
    tm = _mm_row_tile(lp)
    nt = 3
    tn = ncol // nt
    nk = N_CHIPS * nt

    def body(dp_ref, w_ref, x_ref, g_ref, dout_ref, dx_ref, dg_ref, acc_ref):
        i = pl.program_id(0)
        kk = pl.program_id(1)

        @pl.when(jnp.logical_and(i == 0, kk == 0))
        def _():
            dg_ref[...] = jnp.zeros_like(dg_ref)

        @pl.when(kk == 0)
        def _():
            acc_ref[...] = jnp.zeros_like(acc_ref)

        acc_ref[...] += lax.dot_general(dp_ref[...], w_ref[0], NT, preferred_element_type=F32)

        @pl.when(kk == nk - 1)
        def _():
            x = x_ref[...]
            r = lax.rsqrt(jnp.mean(x * x, axis=-1, keepdims=True) + EPS)
            xhat = x * r
            dh = acc_ref[...]
            dg_ref[...] += jnp.sum(dh * xhat, axis=0, keepdims=True)
            dxh = dh * g_ref[...]
            dx_ref[...] = dout_ref[...] + r * (dxh - xhat * jnp.mean(dxh * xhat, axis=-1, keepdims=True))

    return pl.pallas_call(
        body, grid=(lp // tm, nk),
        in_specs=[pl.BlockSpec((tm, tn), lambda i, k: (i, k)),
                  pl.BlockSpec((1, d, tn), lambda i, k: (k // nt, 0, k % nt)),
                  pl.BlockSpec((tm, d), lambda i, k: (i, 0)),
                  pl.BlockSpec((1, d), lambda i, k: (0, 0)),
                  pl.BlockSpec((tm, d), lambda i, k: (i, 0))],
        out_specs=[pl.BlockSpec((tm, d), lambda i, k: (i, 0)), pl.BlockSpec((1, d), lambda i, k: (0, 0))],
        out_shape=[jax.ShapeDtypeStruct((lp, d), F32), jax.ShapeDtypeStruct((1, d), F32)],
        scratch_shapes=[pltpu.VMEM((tm, d), F32)],
        name="in_proj_bwd", compiler_params=_params())(dproj, wg, hres, norm_g, dout)


def _adamw_math(w, g, m, v):
    m = ADAM_B1 * m + (1.0 - ADAM_B1) * g
    v = ADAM_B2 * v + (1.0 - ADAM_B2) * (g * g)
    m_hat = m / (1.0 - ADAM_B1 ** ADAM_STEP)
    v_hat = v / (1.0 - ADAM_B2 ** ADAM_STEP)
    delta = -ADAM_LR * (m_hat / (jnp.sqrt(v_hat) + ADAM_EPS) + ADAM_WD * w)
    return delta, m, v


def _elementwise_rows(shape):
    r, c = shape
    for t in (256, 128, 64, 32, 16, 8):
        if r % t == 0 and r > t and t * c * 4 <= 1024 * 1024:
            return t
    return r


def _adamw(name, w, m, v, *g_parts):
    shape = w.shape
    tr = _elementwise_rows(shape)
    n_g = len(g_parts)

    def body(*refs):
        w_ref, m_ref, v_ref = refs[:3]
        g_refs = refs[3:3 + n_g]
        g_out, d_out, m_out, v_out = refs[3 + n_g:]
        g = g_refs[0][...]
        for gr in g_refs[1:]:
            g = g + gr[...]
        delta, m_new, v_new = _adamw_math(w_ref[...], g, m_ref[...], v_ref[...])
        g_out[...] = g
        d_out[...] = delta
        m_out[...] = m_new
        v_out[...] = v_new

    spec = pl.BlockSpec((tr, shape[1]), lambda i: (i, 0))
    return pl.pallas_call(
        body, grid=(shape[0] // tr,),
        in_specs=[spec] * (3 + n_g), out_specs=[spec] * 4,
        out_shape=[jax.ShapeDtypeStruct(shape, F32)] * 4,
        name=name, compiler_params=_params())(w, m, v, *g_parts)


def _chip_half_sum(name, g, recv, core):
    _, _, hr, cols = g.shape
    tr = _elementwise_rows((hr, cols))

    def body(core_ref, g_ref, r_ref, o_ref, ob_ref):
        s = g_ref[0, 0] + r_ref[0]
        o_ref[0] = s
        ob_ref[0] = s.astype(BF16)

    blk = pl.BlockSpec((1, tr, cols), lambda j, i, core_ref: (j, i, 0))
    grid_spec = pltpu.PrefetchScalarGridSpec(
        num_scalar_prefetch=1, grid=(N_CHIPS, hr // tr),
        in_specs=[pl.BlockSpec((1, 1, tr, cols), lambda j, i, core_ref: (j, core_ref[0], i, 0)), blk],
        out_specs=[blk, blk])
    return pl.pallas_call(
        body, grid_spec=grid_spec,
        out_shape=[jax.ShapeDtypeStruct((N_CHIPS, hr, cols), F32), jax.ShapeDtypeStruct((N_CHIPS, hr, cols), BF16)],
        name=name, compiler_params=_params())(core, g, recv)


def _block_half_total(name, chip_sums, recv, chip_core):
    _, hr, cols = chip_sums.shape
    tr = _elementwise_rows((hr, cols))

    def body(cc_ref, p_ref, r_ref, o_ref):
        s = p_ref[0]
        for k in range(3):
            s = s + r_ref[k].astype(F32)
        o_ref[0] = s

    grid_spec = pltpu.PrefetchScalarGridSpec(
        num_scalar_prefetch=1, grid=(hr // tr,),
        in_specs=[pl.BlockSpec((1, tr, cols), lambda i, cc_ref: (cc_ref[0], i, 0)),
                  pl.BlockSpec((3, tr, cols), lambda i, cc_ref: (0, i, 0))],
        out_specs=pl.BlockSpec((1, tr, cols), lambda i, cc_ref: (cc_ref[1], i, 0)))
    return pl.pallas_call(
        body, grid_spec=grid_spec, out_shape=jax.ShapeDtypeStruct((2, hr, cols), F32),
        name=name, compiler_params=_params())(chip_core, chip_sums, recv)


def _place_shard(name, w, chip, dtype):
    r, c = w.shape
    tr = _elementwise_rows((r, c))

    def body(chip_ref, w_ref, o_ref):
        o_ref[0] = w_ref[...].astype(dtype)

    grid_spec = pltpu.PrefetchScalarGridSpec(
        num_scalar_prefetch=1, grid=(r // tr,),
        in_specs=[pl.BlockSpec((tr, c), lambda i, chip_ref: (i, 0))],
        out_specs=pl.BlockSpec((1, tr, c), lambda i, chip_ref: (chip_ref[0], i, 0)))
    return pl.pallas_call(
        body, grid_spec=grid_spec, out_shape=jax.ShapeDtypeStruct((N_CHIPS, r, c), dtype),
        name=name, compiler_params=_params())(chip, w)


def _sum_slots(name, slots):
    k, r, c = slots.shape

    def body(s_ref, o_ref):
        s = s_ref[0]
        for j in range(1, k):
            s = s + s_ref[j]
        o_ref[...] = s

    return pl.pallas_call(body, out_shape=jax.ShapeDtypeStruct((r, c), F32), name=name,
                          compiler_params=_params())(slots)


def _mesh_pos():
    return lax.axis_index("x"), lax.axis_index("y"), lax.axis_index("c")


def _other_chips(x, y):
    return [(1 - x, y), (x, 1 - y), (1 - x, 1 - y)]


def _gather_weights(bufs):
    n = len(bufs)
    half = [b.shape[1] // 2 for b in bufs]

    def body(*refs):
        gathered = refs[n:2 * n]
        ici_send, ici_recv, d2d_send, d2d_recv = refs[2 * n:]
        x, y, c = _mesh_pos()
        me = 2 * x + y
        chips = _other_chips(x, y)

        def part(a, block, core):
            return gathered[a].at[block, pl.ds(core * half[a], half[a])]

        def over_ici(a, k, block):
            px, py = chips[k]
            return pltpu.make_async_remote_copy(
                src_ref=part(a, block, c), dst_ref=part(a, block, c),
                send_sem=ici_send.at[a, k], recv_sem=ici_recv.at[a, k],
                device_id=(px, py, c), device_id_type=MESH)

        def over_d2d(a, k, core):
            px, py = chips[k]
            return pltpu.make_async_remote_copy(
                src_ref=part(a, 2 * px + py, core), dst_ref=part(a, 2 * px + py, core),
                send_sem=d2d_send.at[a, k], recv_sem=d2d_recv.at[a, k],
                device_id=(x, y, 1 - c), device_id_type=MESH)

        for a in range(n):
            for k in range(3):
                over_ici(a, k, me).start()
        for a in range(n):
            for k, (px, py) in enumerate(chips):
                over_ici(a, k, 2 * px + py).wait_recv()
                over_d2d(a, k, c).start()
        for a in range(n):
            for k in range(3):
                over_d2d(a, k, 1 - c).wait_recv()
        for a in range(n):
            for k in range(3):
                over_ici(a, k, me).wait_send()
                over_d2d(a, k, c).wait_send()

    return pl.pallas_call(
        body, in_specs=[ANY] * n, out_specs=[ANY] * n,
        out_shape=[jax.ShapeDtypeStruct(b.shape, b.dtype) for b in bufs],
        input_output_aliases={a: a for a in range(n)},
        scratch_shapes=[pltpu.SemaphoreType.DMA((n, 3))] * 4,
        name="gather_weights")(*bufs)


def _send_other_halves(grads):
    n = len(grads)

    def body(*refs):
        srcs = refs[:n]
        dsts = refs[n:2 * n]
        send_sems, recv_sems = refs[2 * n:]
        x, y, c = _mesh_pos()
        copies = [pltpu.make_async_remote_copy(
            src_ref=srcs[a].at[j, 1 - c], dst_ref=dsts[a].at[j], send_sem=send_sems.at[a, j],
            recv_sem=recv_sems.at[a, j], device_id=(x, y, 1 - c), device_id_type=MESH)
            for a in range(n) for j in range(N_CHIPS)]
        for cp in copies:
            cp.start()
        for cp in copies:
            cp.wait()

    return pl.pallas_call(
        body, in_specs=[ANY] * n, out_specs=[ANY] * n,
        out_shape=[jax.ShapeDtypeStruct((N_CHIPS,) + g.shape[2:], F32) for g in grads],
        scratch_shapes=[pltpu.SemaphoreType.DMA((n, N_CHIPS))] * 2,
        name="send_other_halves")(*grads)


def _exchange_grads(blocked, small):
    n = len(blocked)

    def body(*refs):
        srcs = refs[:n]
        small_src = refs[n]
        dsts = refs[n + 1:2 * n + 1]
        small_dst = refs[2 * n + 1]
        send_sems, recv_sems, ssend_sems, srecv_sems, local_sem = refs[2 * n + 2:]
        x, y, c = _mesh_pos()
        chips = _other_chips(x, y)
        my_idx = 4 * x + 2 * y + c
        local = pltpu.make_async_copy(small_src, small_dst.at[my_idx], local_sem)
        local.start()
        others = []
        for r in range(1, 8):
            px = 1 - x if r & 4 else x
            py = 1 - y if r & 2 else y
            pc = 1 - c if r & 1 else c
            others.append((px, py, pc))
        for r, peer in enumerate(others):
            pltpu.make_async_remote_copy(
                src_ref=small_src, dst_ref=small_dst.at[my_idx], send_sem=ssend_sems.at[r],
                recv_sem=srecv_sems.at[r], device_id=peer, device_id_type=MESH).start()
        for a in range(n):
            for k, (px, py) in enumerate(chips):
                pltpu.make_async_remote_copy(
                    src_ref=srcs[a].at[2 * px + py], dst_ref=dsts[a].at[k], send_sem=send_sems.at[a, k],
                    recv_sem=recv_sems.at[a, k], device_id=(px, py, c), device_id_type=MESH).start()
        for r, (px, py, pc) in enumerate(others):
            pltpu.make_async_remote_copy(
                src_ref=small_src, dst_ref=small_dst.at[4 * px + 2 * py + pc], send_sem=ssend_sems.at[r],
                recv_sem=srecv_sems.at[r], device_id=(px, py, pc), device_id_type=MESH).wait()
        for a in range(n):
            for k, (px, py) in enumerate(chips):
                pltpu.make_async_remote_copy(
                    src_ref=srcs[a].at[2 * px + py], dst_ref=dsts[a].at[k], send_sem=send_sems.at[a, k],
                    recv_sem=recv_sems.at[a, k], device_id=(px, py, c), device_id_type=MESH).wait()
        local.wait()

    out_shape = [jax.ShapeDtypeStruct((3,) + b.shape[1:], b.dtype) for b in blocked]
    out_shape.append(jax.ShapeDtypeStruct((8,) + small.shape, F32))
    return pl.pallas_call(
        body, in_specs=[ANY] * (n + 1), out_specs=[ANY] * (n + 1), out_shape=out_shape,
        scratch_shapes=[pltpu.SemaphoreType.DMA((n, 3)), pltpu.SemaphoreType.DMA((n, 3)),
                        pltpu.SemaphoreType.DMA((7,)), pltpu.SemaphoreType.DMA((7,)),
                        pltpu.SemaphoreType.DMA],
        name="exchange_grads")(*blocked, small)


def _join_halves(bufs):
    n = len(bufs)

    def body(*refs):
        joined = refs[n:2 * n]
        send_sems, recv_sems = refs[2 * n:]
        x, y, c = _mesh_pos()
        for a in range(n):
            pltpu.make_async_remote_copy(
                src_ref=joined[a].at[c], dst_ref=joined[a].at[c], send_sem=send_sems.at[a],
                recv_sem=recv_sems.at[a], device_id=(x, y, 1 - c), device_id_type=MESH).start()
        for a in range(n):
            pltpu.make_async_remote_copy(
                src_ref=joined[a].at[c], dst_ref=joined[a].at[1 - c], send_sem=send_sems.at[a],
                recv_sem=recv_sems.at[a], device_id=(x, y, 1 - c), device_id_type=MESH).wait()

    return pl.pallas_call(
        body, in_specs=[ANY] * n, out_specs=[ANY] * n,
        out_shape=[jax.ShapeDtypeStruct(b.shape, b.dtype) for b in bufs],
        input_output_aliases={a: a for a in range(n)},
        scratch_shapes=[pltpu.SemaphoreType.DMA((n,))] * 2,
        name="join_halves")(*bufs)


def kernel(x, meta_tokens, norm_g, w_in, conv_w, conv_b, ln_g, ln_b, w_conv_out, lb_logits, gnorm_g, w_rec_out, w_out, final_g, loss_target, m_meta_tokens, m_norm_g, m_w_in, m_conv_w, m_conv_b, m_ln_g, m_ln_b, m_w_conv_out, m_lb_logits, m_gnorm_g, m_w_rec_out, m_w_out, m_final_g, v_meta_tokens, v_norm_g, v_w_in, v_conv_w, v_conv_b, v_ln_g, v_ln_b, v_w_conv_out, v_lb_logits, v_gnorm_g, v_w_rec_out, v_w_out, v_final_g):
    seq, d = x.shape[1], x.shape[2]
    n_meta = meta_tokens.shape[0]
    n_pad = CHUNK - n_meta
    ds = d // N_CHIPS
    chip = 2 * lax.axis_index("x") + lax.axis_index("y")

    conv_w_pad = jnp.pad(conv_w[0], ((0, HALO - CONV_WIDTH), (0, 0)))
    chip_idx = chip.astype(jnp.int32).reshape(1)
    win_g, sq_g, small_g = _gather_weights([
        _place_shard("place_w_in", w_in[0], chip_idx, BF16),
        _place_shard("place_square", jnp.concatenate([w_conv_out[0], w_rec_out[0], w_out[0]], axis=0),
                     chip_idx, BF16),
        _place_shard("place_small", jnp.concatenate([conv_w_pad, meta_tokens], axis=0), chip_idx, F32)])
    wc_full = sq_g[:, 0:ds].reshape(d, d)
    wr_full = sq_g[:, ds:2 * ds].reshape(d, d)
    wo_full = sq_g[:, 2 * ds:3 * ds].reshape(d, d)
    cw_full = jnp.transpose(small_g[:, 0:HALO], (1, 0, 2)).reshape(HALO, d)
    meta_full = jnp.transpose(small_g[:, HALO:HALO + n_meta], (1, 0, 2)).reshape(n_meta, d)

    hres = jnp.concatenate([jnp.zeros((n_pad, d), F32), meta_full, x[0]], axis=0)
    target = jnp.pad(loss_target[0], ((CHUNK, 0), (0, 0)))
    final_g2 = final_g.reshape(1, d)
    h = _rmsnorm_fwd(hres, norm_g)
    proj = _in_proj(h, win_g)
    c, yc_in, y_conv = _conv_fwd(proj, cw_full, conv_b, ln_g, ln_b, wc_full)
    o, s_all = _hgrn_fwd(proj, lb_logits, n_pad)
    yr_in, merged, y_rec, dout, loss_acc, dfinal_g = _tail_fwd(
        o, proj, y_conv, hres, target, gnorm_g, final_g2, wr_full, wo_full)

    (dyc, dyr, dout_bf, dz, dproj, do, dc, dgnorm_g, dln_g, dln_b) = _tail_bwd(
        dout, proj, y_conv, y_rec, o, c, wo_full, wr_full, wc_full, ln_g, ln_b, gnorm_g)
    dproj, dlb_logits = _hgrn_bwd(proj, do, s_all, lb_logits, n_pad, dproj)
    dproj, dconv_w, dconv_b = _conv_bwd(dc, proj, cw_full, dz, dproj)
    g_win = _weight_grad(h, dproj, "grad_w_in", True)
    g_wc = _weight_grad(yc_in, dyc, "grad_w_conv_out", False)
    g_wr = _weight_grad(yr_in, dyr, "grad_w_rec_out", False)
    g_wo = _weight_grad(merged, dout_bf, "grad_w_out", False)
    dhres, dnorm_g = _in_proj_bwd(dproj, win_g, hres, norm_g, dout)
    grad_x = dhres[CHUNK:][None]

    small = jnp.concatenate([dnorm_g, dconv_b, dln_g, dln_b, dlb_logits, dgnorm_g, dfinal_g,
                             dhres[n_pad:CHUNK], dconv_w[:CONV_WIDTH],
                             jnp.zeros((1, d), F32)], axis=0)
    g_sq = jnp.concatenate([g.reshape(N_CHIPS, ds, d) for g in (g_wc, g_wr, g_wo)], axis=1)
    grads = [g.reshape(N_CHIPS, 2, g.shape[1] // 2, g.shape[2]) for g in (g_win, g_sq)]
    core = lax.axis_index("c").astype(jnp.int32).reshape(1)
    from_sibling = _send_other_halves(grads)
    chip_sums = [_chip_half_sum("chip_half_sum_" + nm, g, r, core)
                 for nm, g, r in zip(("w_in", "square"), grads, from_sibling)]
    recv_win, recv_sq, small_slots = _exchange_grads([s[1] for s in chip_sums], small)
    chip_core = jnp.concatenate([chip_idx, core])
    totals = [_block_half_total("block_half_total_" + nm, s[0], r, chip_core)
              for nm, s, r in zip(("w_in", "square"), chip_sums, (recv_win, recv_sq))]
    gt_win, gt_sq = [t.reshape(2 * t.shape[1], t.shape[2]) for t in _join_halves(totals)]
    small_sum = _sum_slots("sum_small", small_slots)

    res = {}
    res["w_in"] = _adamw("adamw_w_in", w_in[0], m_w_in[0], v_w_in[0], gt_win)
    res["w_conv_out"] = _adamw("adamw_w_conv_out", w_conv_out[0], m_w_conv_out[0], v_w_conv_out[0], gt_sq[0:ds])
    res["w_rec_out"] = _adamw("adamw_w_rec_out", w_rec_out[0], m_w_rec_out[0], v_w_rec_out[0], gt_sq[ds:2 * ds])
    res["w_out"] = _adamw("adamw_w_out", w_out[0], m_w_out[0], v_w_out[0], gt_sq[2 * ds:3 * ds])
    big = {k: tuple(a[None] for a in v) for k, v in res.items()}

    rep_names = ("norm_g", "conv_b", "ln_g", "ln_b", "lb_logits", "gnorm_g", "final_g")
    rep_w = (norm_g, conv_b, ln_g, ln_b, lb_logits, gnorm_g, final_g2)
    rep_m = (m_norm_g, m_conv_b, m_ln_g, m_ln_b, m_lb_logits, m_gnorm_g, m_final_g.reshape(1, d))
    rep_v = (v_norm_g, v_conv_b, v_ln_g, v_ln_b, v_lb_logits, v_gnorm_g, v_final_g.reshape(1, d))
    rep = _adamw("adamw_replicated", jnp.concatenate(rep_w, 0), jnp.concatenate(rep_m, 0),
                 jnp.concatenate(rep_v, 0), small_sum[0:8])
    rep_rows = {"norm_g": (0, 1), "conv_b": (1, 2), "ln_g": (2, 3), "ln_b": (3, 4), "lb_logits": (4, 6),
                "gnorm_g": (6, 7), "final_g": (7, 8)}
    small_out = {}
    for nm in rep_names:
        lo, hi = rep_rows[nm]
        vals = tuple(a[lo:hi] for a in rep)
        if nm == "final_g":
            vals = tuple(a.reshape(d) for a in vals)
        small_out[nm] = vals
    cw_row = 8 + n_meta
    g_meta = lax.dynamic_slice_in_dim(small_sum[8:cw_row], chip * ds, ds, axis=1)
    small_out["meta_tokens"] = _adamw("adamw_meta", meta_tokens, m_meta_tokens, v_meta_tokens, g_meta)
    g_cw = lax.dynamic_slice_in_dim(small_sum[cw_row:cw_row + HALO], chip * ds, ds, axis=1)
    pad_rows = ((0, HALO - CONV_WIDTH), (0, 0))
    cw_res = _adamw("adamw_conv_w", conv_w_pad, jnp.pad(m_conv_w[0], pad_rows),
                    jnp.pad(v_conv_w[0], pad_rows, constant_values=1.0), g_cw)
    small_out["conv_w"] = tuple(a[:CONV_WIDTH][None] for a in cw_res)

    loss = lax.psum(loss_acc[0, 0], ("x", "y", "c"))

    order = ("meta_tokens", "norm_g", "w_in", "conv_w", "conv_b", "ln_g", "ln_b", "w_conv_out", "lb_logits",
             "gnorm_g", "w_rec_out", "w_out", "final_g")
    allres = {**big, **small_out}
    outs = [loss, grad_x]
    for field in range(4):
        outs.extend(allres[nm][field] for nm in order)
    return tuple(outs)
```

```python
import numpy as np

import jax
import jax.numpy as jnp
from jax import lax
from jax.experimental import pallas as pl
from jax.experimental.pallas import tpu as pltpu

F32 = jnp.float32
BF16 = jnp.bfloat16

EPS = 1e-6
CHUNK = 64
N_LEVELS = 6
CONV_WIDTH = 31
HALO = 32
CONV_ROWS = 32
CONV_LANES = 256
HEAD = 128
N_CHIPS = 4
VMEM_LIMIT_BYTES = 56 * 1024 * 1024

ADAM_LR = 0.001
ADAM_B1 = 0.9
ADAM_B2 = 0.999
ADAM_EPS = 1e-08
ADAM_WD = 0.01
ADAM_STEP = 10

MESH = pl.DeviceIdType.MESH
ANY = pl.BlockSpec(memory_space=pl.ANY)

NT = (((1,), (1,)), ((), ()))
TN = (((0,), (0,)), ((), ()))


def _params(**kw):
    return pltpu.CompilerParams(vmem_limit_bytes=VMEM_LIMIT_BYTES, **kw)


def _sigmoid(x):
    return jax.nn.sigmoid(x)


def _dsilu(x, s):
    return s * (1.0 + x * (1.0 - s))


def _row_tile(lp):
    for t in (320, 256, 192, 128, 64):
        if lp % t == 0:
            return t
    raise ValueError(f"unsupported padded length {lp}")


def _mm_row_tile(lp):
    for t in (832, 640, 320, 256, 192, 128, 64):
        if lp % t == 0:
            return t
    raise ValueError(f"unsupported padded length {lp}")


def _dot3(m_bf16, x):
    hi = x.astype(BF16)
    r1 = x - hi.astype(F32)
    mid = r1.astype(BF16)
    lo = (r1 - mid.astype(F32)).astype(BF16)
    return (jnp.dot(m_bf16, hi, preferred_element_type=F32)
            + jnp.dot(m_bf16, mid, preferred_element_type=F32)
            + jnp.dot(m_bf16, lo, preferred_element_type=F32))


def _dot2(m_bf16, x):
    hi = x.astype(BF16)
    lo = (x - hi.astype(F32)).astype(BF16)
    return (jnp.dot(m_bf16, hi, preferred_element_type=F32)
            + jnp.dot(m_bf16, lo, preferred_element_type=F32))


def _col_to_row(col):
    return jnp.broadcast_to(col, (HEAD, 8)).T[0:1, :]


def _row_to_col(row):
    return jnp.broadcast_to(row, (8, HEAD)).T[:, 0:1]


def _hgrn_tables():
    t = np.arange(CHUNK)
    ltri = (t[None, :] <= t[:, None]).astype(np.float32)
    mats = [ltri]
    for lvl in range(1, N_LEVELS + 1):
        blk = CHUNK >> (lvl - 1)
        mid = (t // blk) * blk + blk // 2
        mats.append(ltri[mid - 1])
    after = (t[None, :] >= t[:, None]).astype(np.float32)
    before = (t[None, :] < t[:, None]).astype(np.float32)
    return jnp.asarray(np.concatenate(mats, 0), BF16), jnp.asarray(np.concatenate([after, before], 1), BF16)


def _rmsnorm_fwd(hres, g):
    lp, d = hres.shape
    tm = _row_tile(lp)

    def body(x_ref, g_ref, h_ref):
        x = x_ref[...]
        r = lax.rsqrt(jnp.mean(x * x, axis=-1, keepdims=True) + EPS)
        h_ref[...] = (x * r * g_ref[...]).astype(BF16)

    return pl.pallas_call(
        body, grid=(lp // tm,),
        in_specs=[pl.BlockSpec((tm, d), lambda i: (i, 0)), pl.BlockSpec((1, d), lambda i: (0, 0))],
        out_specs=pl.BlockSpec((tm, d), lambda i: (i, 0)),
        out_shape=jax.ShapeDtypeStruct((lp, d), BF16),
        name="rmsnorm_fwd", compiler_params=_params())(hres, g)


def _in_proj(h, wg):
    lp, d = h.shape
    _, _, ncol = wg.shape
    tm = _mm_row_tile(lp)
    nt = 3
    tn = ncol // nt

    def body(h_ref, w_ref, o_ref):
        o_ref[...] = jnp.dot(h_ref[...], w_ref[0], preferred_element_type=F32)

    return pl.pallas_call(
        body, grid=(N_CHIPS, nt, lp // tm),
        in_specs=[pl.BlockSpec((tm, d), lambda j, n, i: (i, 0)),
                  pl.BlockSpec((1, d, tn), lambda j, n, i: (j, 0, n))],
        out_specs=pl.BlockSpec((tm, tn), lambda j, n, i: (i, j * nt + n)),
        out_shape=jax.ShapeDtypeStruct((lp, N_CHIPS * ncol), F32),
        name="in_proj", compiler_params=_params())(h, wg)


def _conv_fwd(proj, conv_w, conv_b, ln_g, ln_b, w_conv):
    lp = proj.shape[0]
    d = conv_b.shape[1]
    tm = _row_tile(lp)
    hb = tm // HALO

    def body(ua_ref, ub_ref, z_ref, uap_ref, ubp_ref, cw_ref, cb_ref, lg_ref, lb_ref, w_ref,
             c_ref, ycin_ref, yconv_ref, aext_ref):
        i = pl.program_id(0)
        a_prev = uap_ref[...] * _sigmoid(ubp_ref[...])
        aext_ref[0:HALO, :] = jnp.where(i > 0, a_prev, 0.0)
        aext_ref[HALO:HALO + tm, :] = ua_ref[...] * _sigmoid(ub_ref[...])

        def row_block(r, carry):
            r0 = pl.multiple_of(r * CONV_ROWS, CONV_ROWS)
            for cs in range(d // CONV_LANES):
                cl = slice(cs * CONV_LANES, (cs + 1) * CONV_LANES)
                blk = aext_ref[pl.ds(r0, CONV_ROWS + HALO), cl]
                acc = jnp.zeros((CONV_ROWS, CONV_LANES), F32) + cb_ref[:, cl]
                for b in range(8):
                    sh = blk if b == 0 else pltpu.roll(blk, CONV_ROWS + HALO - b, axis=0)
                    for a in range(5):
                        j = 8 * a + b - 2
                        if 0 <= j < CONV_WIDTH:
                            acc = acc + cw_ref[j:j + 1, cl] * sh[8 * a:8 * a + CONV_ROWS, :]
                c_ref[pl.ds(r0, CONV_ROWS), cl] = acc
            return carry

        lax.fori_loop(0, tm // CONV_ROWS, row_block, 0)

        c = c_ref[...]
        mu = jnp.mean(c, axis=-1, keepdims=True)
        xc = c - mu
        rstd = lax.rsqrt(jnp.mean(xc * xc, axis=-1, keepdims=True) + EPS)
        ln = xc * rstd * lg_ref[...] + lb_ref[...]
        s = ln * _sigmoid(ln)
        z = z_ref[...]
        ycin = (s * (z * _sigmoid(z))).astype(BF16)
        ycin_ref[...] = ycin
        yconv_ref[...] = jnp.dot(ycin, w_ref[...], preferred_element_type=F32)

    row = lambda p: pl.BlockSpec((tm, d), lambda i, p=p: (i, p))
    halo = lambda p: pl.BlockSpec((HALO, d), lambda i, p=p: (jnp.maximum(i * hb - 1, 0), p))
    vec = pl.BlockSpec((1, d), lambda i: (0, 0))
    return pl.pallas_call(
        body, grid=(lp // tm,),
        in_specs=[row(0), row(1), row(2), halo(0), halo(1),
                  pl.BlockSpec((HALO, d), lambda i: (0, 0)), vec, vec, vec,
                  pl.BlockSpec((d, d), lambda i: (0, 0))],
        out_specs=[pl.BlockSpec((tm, d), lambda i: (i, 0))] * 3,
        out_shape=[jax.ShapeDtypeStruct((lp, d), F32), jax.ShapeDtypeStruct((lp, d), BF16),
                   jax.ShapeDtypeStruct((lp, d), F32)],
        scratch_shapes=[pltpu.VMEM((HALO + tm, d), F32)],
        name="conv_fwd", compiler_params=_params())(
            proj, proj, proj, proj, proj, conv_w, conv_b, ln_g, ln_b, w_conv)


def _lower_bound(lbl_ref):
    l0 = lbl_ref[0:1, :]
    l1 = lbl_ref[1:2, :]
    m = jnp.maximum(l0, l1)
    e0 = jnp.exp(l0 - m)
    e1 = jnp.exp(l1 - m)
    p0 = e0 / (e0 + e1)
    return p0, p0 * (e1 / (e0 + e1))


def _level_masks():
    rid = lax.broadcasted_iota(jnp.int32, (CHUNK, 1), 0)
    r2 = lax.broadcasted_iota(jnp.int32, (CHUNK, CHUNK), 0)
    c2 = lax.broadcasted_iota(jnp.int32, (CHUNK, CHUNK), 1)
    out = []
    for lvl in range(1, N_LEVELS + 1):
        blk = CHUNK >> (lvl - 1)
        sh = blk.bit_length() - 1
        upper = (rid & (blk - 1)) >= (blk // 2)
        same = (r2 >> sh) == (c2 >> sh)
        out.append((upper, same))
    return out


def _gates(qr, fr, lb, valid):
    sq = _sigmoid(qr)
    q = qr * sq
    sf = _sigmoid(fr)
    f = lb + (1.0 - lb) * sf
    g = jnp.where(valid, jnp.log(f), 0.0)
    k = jnp.where(valid, 1.0 - f, 0.0)
    return q, sq, f, sf, g, k


def _level_factors(b, r, upper):
    e = jnp.exp(jnp.where(upper, b - r, r - b))
    return jnp.where(upper, e, 0.0), jnp.where(upper, 0.0, e)


def _hgrn_fwd(proj, lb_logits, n_pad):
    lp = proj.shape[0]
    d = lb_logits.shape[1]
    n_heads = d // HEAD
    nc = lp // CHUNK
    tab, _ = _hgrn_tables()
    n_tab = tab.shape[0]

    def body(qr_ref, fr_ref, ir_ref, lbl_ref, tab_ref, o_ref, sall_ref, s_ref, t_ref):
        n = pl.program_id(0)

        @pl.when(n == 0)
        def _():
            s_ref[...] = jnp.zeros_like(s_ref)

        sall_ref[0] = s_ref[...]
        lb_all, _ = _lower_bound(lbl_ref)
        rid = lax.broadcasted_iota(jnp.int32, (CHUNK, 1), 0)
        valid = jnp.logical_or(n > 0, rid >= n_pad)
        f_all = lb_all + (1.0 - lb_all) * _sigmoid(fr_ref[...])
        t_ref[...] = _dot2(tab_ref[...], jnp.where(valid, jnp.log(f_all), 0.0))
        masks = _level_masks()

        def head(h):
            off = h * HEAD
            hs = pl.ds(off, HEAD)
            lb = _lower_bound_slice(lbl_ref, hs)
            q, _, _, _, _, k = _gates(qr_ref[:, hs], fr_ref[:, hs], lb, valid)
            v = ir_ref[:, hs]
            b = t_ref[0:CHUNK, hs]
            s0 = s_ref[hs, :]
            o = jnp.dot((q * jnp.exp(b)).astype(BF16), s0.astype(BF16), preferred_element_type=F32)
            o = o + jnp.sum(q * k, axis=-1, keepdims=True) * v
            a = jnp.zeros((CHUNK, CHUNK), F32)
            for lvl in range(1, N_LEVELS + 1):
                upper, same = masks[lvl - 1]
                eq, ek = _level_factors(b, t_ref[CHUNK * lvl:CHUNK * (lvl + 1), hs], upper)
                p = lax.dot_general((q * eq).astype(BF16), (k * ek).astype(BF16), NT, preferred_element_type=F32)
                a = a + jnp.where(same, p, 0.0)
            vb = v.astype(BF16)
            o_ref[:, hs] = o + jnp.dot(a.astype(BF16), vb, preferred_element_type=F32)
            b_last = t_ref[CHUNK - 1:CHUNK, hs]
            khat = (k * jnp.exp(b_last - b)).astype(BF16)
            s_ref[hs, :] = _row_to_col(jnp.exp(b_last)) * s0 + lax.dot_general(khat, vb, TN, preferred_element_type=F32)
        for h in range(n_heads):
            head(h)

    piece = lambda p: pl.BlockSpec((CHUNK, d), lambda n, p=p: (n, p))
    return pl.pallas_call(
        body, grid=(nc,),
        in_specs=[piece(3), piece(4), piece(5), pl.BlockSpec((2, d), lambda n: (0, 0)),
                  pl.BlockSpec((n_tab, CHUNK), lambda n: (0, 0))],
        out_specs=[pl.BlockSpec((CHUNK, d), lambda n: (n, 0)), pl.BlockSpec((1, d, HEAD), lambda n: (n, 0, 0))],
        out_shape=[jax.ShapeDtypeStruct((lp, d), F32), jax.ShapeDtypeStruct((nc, d, HEAD), F32)],
        scratch_shapes=[pltpu.VMEM((d, HEAD), F32), pltpu.VMEM((n_tab, d), F32)],
        name="hgrn_fwd", compiler_params=_params())(proj, proj, proj, lb_logits, tab)


def _lower_bound_slice(lbl_ref, hs):
    l0 = lbl_ref[0:1, hs]
    l1 = lbl_ref[1:2, hs]
    m = jnp.maximum(l0, l1)
    e0 = jnp.exp(l0 - m)
    e1 = jnp.exp(l1 - m)
    return e0 / (e0 + e1)


def _tail_fwd(o, proj, y_conv, hres, target, gnorm_g, final_g, w_rec, w_out):
    lp, d = o.shape
    n_heads = d // HEAD
    tm = _row_tile(lp)

    def body(o_ref, gr_ref, mc_ref, mr_ref, yc_ref, x_ref, t_ref, gn_ref, fg_ref, wr_ref, wo_ref,
             yrin_ref, mg_ref, yrec_ref, dout_ref, loss_ref, dfg_ref):
        i = pl.program_id(0)

        @pl.when(i == 0)
        def _():
            loss_ref[...] = jnp.zeros_like(loss_ref)
            dfg_ref[...] = jnp.zeros_like(dfg_ref)

        for h in range(n_heads):
            hs = slice(h * HEAD, (h + 1) * HEAD)
            oh = o_ref[:, hs]
            on = oh * lax.rsqrt(jnp.mean(oh * oh, axis=-1, keepdims=True) + EPS) * gn_ref[:, hs]
            gr = gr_ref[:, hs]
            yrin_ref[:, hs] = (on * (gr * _sigmoid(gr))).astype(BF16)
        yrec = jnp.dot(yrin_ref[...], wr_ref[...], preferred_element_type=F32)
        yrec_ref[...] = yrec
        merged = (_sigmoid(mc_ref[...]) * yc_ref[...] + _sigmoid(mr_ref[...]) * yrec).astype(BF16)
        mg_ref[...] = merged
        out = x_ref[...] + jnp.dot(merged, wo_ref[...], preferred_element_type=F32)
        r = lax.rsqrt(jnp.mean(out * out, axis=-1, keepdims=True) + EPS)
        yhat = out * r
        fg = fg_ref[...]
        rid = lax.broadcasted_iota(jnp.int32, (tm, 1), 0) + i * tm
        err = jnp.where(rid >= CHUNK, yhat * fg - t_ref[...], 0.0)
        loss_ref[...] += 0.5 * jnp.sum(err * err) / d
        dy = err / d
        dfg_ref[...] += jnp.sum(dy * yhat, axis=0, keepdims=True)
        dyh = dy * fg
        dout_ref[...] = r * (dyh - yhat * jnp.mean(dyh * yhat, axis=-1, keepdims=True))

    row = lambda p: pl.BlockSpec((tm, d), lambda i, p=p: (i, p))
    vec = pl.BlockSpec((1, d), lambda i: (0, 0))
    mat = pl.BlockSpec((d, d), lambda i: (0, 0))
    return pl.pallas_call(
        body, grid=(lp // tm,),
        in_specs=[row(0), row(6), row(7), row(8), row(0), row(0), row(0), vec, vec, mat, mat],
        out_specs=[row(0), row(0), row(0), row(0), pl.BlockSpec((8, 128), lambda i: (0, 0)), vec],
        out_shape=[jax.ShapeDtypeStruct((lp, d), BF16), jax.ShapeDtypeStruct((lp, d), BF16),
                   jax.ShapeDtypeStruct((lp, d), F32), jax.ShapeDtypeStruct((lp, d), F32),
                   jax.ShapeDtypeStruct((8, 128), F32), jax.ShapeDtypeStruct((1, d), F32)],
        name="tail_fwd", compiler_params=_params())(
            o, proj, proj, proj, y_conv, hres, target, gnorm_g, final_g, w_rec, w_out)


def _tail_bwd(dout, proj, y_conv, y_rec, o, c, w_out, w_rec, w_conv, ln_g, ln_b, gnorm_g):
    lp, d = dout.shape
    n_heads = d // HEAD
    tm = _row_tile(lp)

    def body(dout_ref, mc_ref, mr_ref, z_ref, gr_ref, yc_ref, yrec_ref, o_ref, c_ref,
             wo_ref, wr_ref, wc_ref, lg_ref, lb_ref, gn_ref,
             dyc_ref, dyr_ref, doutb_ref, dz_ref, dp_ref, do_ref, dc_ref,
             dgn_ref, dlg_ref, dlb_ref, dyrin_ref):
        i = pl.program_id(0)

        @pl.when(i == 0)
        def _():
            dgn_ref[...] = jnp.zeros_like(dgn_ref)
            dlg_ref[...] = jnp.zeros_like(dlg_ref)
            dlb_ref[...] = jnp.zeros_like(dlb_ref)

        doutb = dout_ref[...].astype(BF16)
        doutb_ref[...] = doutb
        dmerged = lax.dot_general(doutb, wo_ref[...], NT, preferred_element_type=F32)
        smc = _sigmoid(mc_ref[...])
        smr = _sigmoid(mr_ref[...])
        dyc = (dmerged * smc).astype(BF16)
        dyr = (dmerged * smr).astype(BF16)
        dyc_ref[...] = dyc
        dyr_ref[...] = dyr
        dp_ref[:, d:2 * d] = (dmerged * yc_ref[...] * smc * (1.0 - smc)).astype(BF16)
        dp_ref[:, 2 * d:3 * d] = (dmerged * yrec_ref[...] * smr * (1.0 - smr)).astype(BF16)

        dyrin_ref[...] = lax.dot_general(dyr, wr_ref[...], NT, preferred_element_type=F32)
        for h in range(n_heads):
            hs = slice(h * HEAD, (h + 1) * HEAD)
            oh = o_ref[:, hs]
            rstd = lax.rsqrt(jnp.mean(oh * oh, axis=-1, keepdims=True) + EPS)
            ohat = oh * rstd
            gn = gn_ref[:, hs]
            gr = gr_ref[:, hs]
            sg = _sigmoid(gr)
            dyrin = dyrin_ref[:, hs]
            don = dyrin * (gr * sg)
            dp_ref[:, hs] = (dyrin * (ohat * gn) * _dsilu(gr, sg)).astype(BF16)
            dgn_ref[:, hs] += jnp.sum(don * ohat, axis=0, keepdims=True)
            doh = don * gn
            do_ref[:, hs] = rstd * (doh - ohat * jnp.mean(doh * ohat, axis=-1, keepdims=True))

        dycin = lax.dot_general(dyc, wc_ref[...], NT, preferred_element_type=F32)
        c = c_ref[...]
        mu = jnp.mean(c, axis=-1, keepdims=True)
        xc = c - mu
        rstd = lax.rsqrt(jnp.mean(xc * xc, axis=-1, keepdims=True) + EPS)
        nrm = xc * rstd
        lg = lg_ref[...]
        ln = nrm * lg + lb_ref[...]
        sl = _sigmoid(ln)
        z = z_ref[...]
        sz = _sigmoid(z)
        dz_ref[...] = (dycin * (ln * sl) * _dsilu(z, sz)).astype(BF16)
        dln = dycin * (z * sz) * _dsilu(ln, sl)
        dlg_ref[...] += jnp.sum(dln * nrm, axis=0, keepdims=True)
        dlb_ref[...] += jnp.sum(dln, axis=0, keepdims=True)
        dn = dln * lg
        dc_ref[...] = rstd * (dn - jnp.mean(dn, axis=-1, keepdims=True)
                              - nrm * jnp.mean(dn * nrm, axis=-1, keepdims=True))

    row = lambda p: pl.BlockSpec((tm, d), lambda i, p=p: (i, p))
    vec = pl.BlockSpec((1, d), lambda i: (0, 0))
    mat = pl.BlockSpec((d, d), lambda i: (0, 0))
    act_bf = jax.ShapeDtypeStruct((lp, d), BF16)
    act_f32 = jax.ShapeDtypeStruct((lp, d), F32)
    vec_f32 = jax.ShapeDtypeStruct((1, d), F32)
    return pl.pallas_call(
        body, grid=(lp // tm,),
        in_specs=[row(0), row(7), row(8), row(2), row(6), row(0), row(0), row(0), row(0),
                  mat, mat, mat, vec, vec, vec],
        out_specs=[row(0)] * 4 + [pl.BlockSpec((tm, 3 * d), lambda i: (i, 2))] + [row(0)] * 2 + [vec] * 3,
        out_shape=[act_bf] * 4 + [jax.ShapeDtypeStruct((lp, 9 * d), BF16)] + [act_f32] * 2 + [vec_f32] * 3,
        scratch_shapes=[pltpu.VMEM((tm, d), F32)],
        name="tail_bwd", compiler_params=_params())(
            dout, proj, proj, proj, proj, y_conv, y_rec, o, c, w_out, w_rec, w_conv, ln_g, ln_b, gnorm_g)


def _hgrn_bwd(proj, do, s_all, lb_logits, n_pad, dproj):
    lp, d = do.shape
    n_heads = d // HEAD
    nc = lp // CHUNK
    tab, utri = _hgrn_tables()
    n_tab = tab.shape[0]

    def body(qr_ref, fr_ref, ir_ref, do_ref, s0_ref, lbl_ref, tab_ref, ut_ref, _,
             dp_ref, dlbl_ref, ds_ref, t_ref, dlb_ref):
        n = pl.program_id(0)
        chunk = nc - 1 - n

        @pl.when(n == 0)
        def _():
            ds_ref[...] = jnp.zeros_like(ds_ref)
            dlb_ref[...] = jnp.zeros_like(dlb_ref)

        lb_all, pp = _lower_bound(lbl_ref)
        rid = lax.broadcasted_iota(jnp.int32, (CHUNK, 1), 0)
        valid = jnp.logical_or(chunk > 0, rid >= n_pad)
        f_all = lb_all + (1.0 - lb_all) * _sigmoid(fr_ref[...])
        t_ref[...] = _dot2(tab_ref[...], jnp.where(valid, jnp.log(f_all), 0.0))
        masks = _level_masks()
        ut = ut_ref[...]

        def head(h):
            off = h * HEAD
            hs = pl.ds(off, HEAD)
            lb = _lower_bound_slice(lbl_ref, hs)
            qr = qr_ref[:, hs]
            q, sq, f, sf, _, k = _gates(qr, fr_ref[:, hs], lb, valid)
            v = ir_ref[:, hs]
            do_h = do_ref[:, hs]
            b = t_ref[0:CHUNK, hs]
            b_last = t_ref[CHUNK - 1:CHUNK, hs]
            s0 = s0_ref[0, hs, :]
            ds1 = ds_ref[hs, :]
            eb = jnp.exp(b)
            ekl = jnp.exp(b_last - b)
            do_bf = do_h.astype(BF16)
            v_bf = v.astype(BF16)
            ds1_bf = ds1.astype(BF16)

            da = lax.dot_general(do_bf, v_bf, NT, preferred_element_type=F32)
            da_diag = jnp.sum(do_h * v, axis=-1, keepdims=True)
            a = jnp.zeros((CHUNK, CHUNK), F32)
            dq_x = eb * lax.dot_general(do_bf, s0.astype(BF16), NT, preferred_element_type=F32)
            dk_x = ekl * lax.dot_general(v_bf, ds1_bf, NT, preferred_element_type=F32)
            x_after = q * dq_x
            x_before = k * dk_x
            for lvl in range(1, N_LEVELS + 1):
                upper, same = masks[lvl - 1]
                eq, ek = _level_factors(b, t_ref[CHUNK * lvl:CHUNK * (lvl + 1), hs], upper)
                qt = (q * eq).astype(BF16)
                kt = (k * ek).astype(BF16)
                p = lax.dot_general(qt, kt, NT, preferred_element_type=F32)
                a = a + jnp.where(same, p, 0.0)
                dam = jnp.where(same, da, 0.0).astype(BF16)
                dqt = jnp.dot(dam, kt, preferred_element_type=F32)
                dkt = lax.dot_general(dam, qt, TN, preferred_element_type=F32)
                dq_x = dq_x + eq * dqt
                dk_x = dk_x + ek * dkt
                x_after = x_after + (qt.astype(F32) * dqt - kt.astype(F32) * dkt)

            dv = (lax.dot_general(a.astype(BF16), do_bf, TN, preferred_element_type=F32)
                  + jnp.sum(q * k, axis=-1, keepdims=True) * do_h
                  + jnp.dot((k * ekl).astype(BF16), ds1_bf, preferred_element_type=F32))
            dp_ref[:, pl.ds(2 * d + off, HEAD)] = dv.astype(BF16)

            carried = jnp.exp(b_last) * _col_to_row(jnp.sum(s0 * ds1, axis=-1, keepdims=True))
            dg = _dot3(ut, jnp.concatenate([x_after, x_before], axis=0)) + carried
            dq = dq_x + da_diag * k
            dk = dk_x + da_diag * q
            dp_ref[:, hs] = (dq * _dsilu(qr, sq)).astype(BF16)
            df = jnp.where(valid, dg / f - dk, 0.0)
            dp_ref[:, pl.ds(d + off, HEAD)] = (df * (1.0 - lb) * sf * (1.0 - sf)).astype(BF16)
            dlb_ref[:, hs] += jnp.sum(df * (1.0 - sf), axis=0, keepdims=True)

            ds_ref[hs, :] = (_row_to_col(jnp.exp(b_last)) * ds1
                             + lax.dot_general((q * eb).astype(BF16), do_bf, TN, preferred_element_type=F32))
        for h in range(n_heads):
            head(h)

        @pl.when(n == nc - 1)
        def _():
            dl0 = dlb_ref[...] * pp
            dlbl_ref[0:1, :] = dl0
            dlbl_ref[1:2, :] = -dl0

    piece = lambda p: pl.BlockSpec((CHUNK, d), lambda n, p=p: (nc - 1 - n, p))
    return pl.pallas_call(
        body, grid=(nc,),
        in_specs=[piece(3), piece(4), piece(5), piece(0),
                  pl.BlockSpec((1, d, HEAD), lambda n: (nc - 1 - n, 0, 0)),
                  pl.BlockSpec((2, d), lambda n: (0, 0)),
                  pl.BlockSpec((n_tab, CHUNK), lambda n: (0, 0)),
                  pl.BlockSpec((CHUNK, 2 * CHUNK), lambda n: (0, 0)), ANY],
        out_specs=[pl.BlockSpec((CHUNK, 3 * d), lambda n: (nc - 1 - n, 1)), pl.BlockSpec((2, d), lambda n: (0, 0))],
        out_shape=[jax.ShapeDtypeStruct(dproj.shape, BF16), jax.ShapeDtypeStruct((2, d), F32)],
        input_output_aliases={8: 0},
        scratch_shapes=[pltpu.VMEM((d, HEAD), F32), pltpu.VMEM((n_tab, d), F32), pltpu.VMEM((1, d), F32)],
        name="hgrn_bwd", compiler_params=_params())(proj, proj, proj, do, s_all, lb_logits, tab, utri, dproj)


def _conv_bwd(dc, proj, conv_w, dz, dproj):
    lp, d = dc.shape
    tm = _row_tile(lp)
    hb = tm // HALO
    n_tiles = lp // tm
    last_halo = lp // HALO - 1

    def body(dc_ref, dcn_ref, ua_ref, ub_ref, uap_ref, ubp_ref, cw_ref, dz_ref, _,
             dp_ref, dcw_ref, dcb_ref, aext_ref, dcext_ref, da_ref):
        i = pl.program_id(0)

        @pl.when(i == 0)
        def _():
            dcw_ref[...] = jnp.zeros_like(dcw_ref)
            dcb_ref[...] = jnp.zeros_like(dcb_ref)

        ua = ua_ref[...]
        sb = _sigmoid(ub_ref[...])
        a_prev = uap_ref[...] * _sigmoid(ubp_ref[...])
        aext_ref[0:HALO, :] = jnp.where(i > 0, a_prev, 0.0)
        aext_ref[HALO:HALO + tm, :] = ua * sb
        dcext_ref[0:tm, :] = dc_ref[...]
        dcext_ref[tm:tm + HALO, :] = jnp.where(i < n_tiles - 1, dcn_ref[...], 0.0)
        dcb_ref[...] += jnp.sum(dc_ref[...], axis=0, keepdims=True)

        def row_block(r, carry):
            r0 = pl.multiple_of(r * CONV_ROWS, CONV_ROWS)
            n_rows = CONV_ROWS + HALO
            for cs in range(d // CONV_LANES):
                cl = slice(cs * CONV_LANES, (cs + 1) * CONV_LANES)
                dblk = dcext_ref[pl.ds(r0, n_rows), cl]
                ablk = aext_ref[pl.ds(r0, n_rows), cl]
                dcur = dblk[0:CONV_ROWS, :]
                acc = jnp.zeros((CONV_ROWS, CONV_LANES), F32)
                for b in range(8):
                    dsh = dblk if b == 0 else pltpu.roll(dblk, n_rows - b, axis=0)
                    ash = ablk if b == 0 else pltpu.roll(ablk, n_rows - b, axis=0)
                    for a in range(5):
                        j_da = CONV_WIDTH - 1 - (8 * a + b)
                        if 0 <= j_da < CONV_WIDTH:
                            acc = acc + cw_ref[j_da:j_da + 1, cl] * dsh[8 * a:8 * a + CONV_ROWS, :]
                        j_w = 8 * a + b - 2
                        if 0 <= j_w < CONV_WIDTH:
                            dcw_ref[j_w:j_w + 1, cl] += jnp.sum(
                                dcur * ash[8 * a:8 * a + CONV_ROWS, :], axis=0, keepdims=True)
                da_ref[pl.ds(r0, CONV_ROWS), cl] = acc
            return carry

        lax.fori_loop(0, tm // CONV_ROWS, row_block, 0)

        da = da_ref[...]
        dp_ref[:, 0:d] = (da * sb).astype(BF16)
        dp_ref[:, d:2 * d] = (da * ua * sb * (1.0 - sb)).astype(BF16)
        dp_ref[:, 2 * d:3 * d] = dz_ref[...]

    row = lambda p: pl.BlockSpec((tm, d), lambda i, p=p: (i, p))
    prev = lambda p: pl.BlockSpec((HALO, d), lambda i, p=p: (jnp.maximum(i * hb - 1, 0), p))
    nxt = pl.BlockSpec((HALO, d), lambda i: (jnp.minimum((i + 1) * hb, last_halo), 0))
    return pl.pallas_call(
        body, grid=(n_tiles,),
        in_specs=[row(0), nxt, row(0), row(1), prev(0), prev(1), pl.BlockSpec((HALO, d), lambda i: (0, 0)),
                  row(0), ANY],
        out_specs=[pl.BlockSpec((tm, 3 * d), lambda i: (i, 0)), pl.BlockSpec((HALO, d), lambda i: (0, 0)),
                   pl.BlockSpec((1, d), lambda i: (0, 0))],
        out_shape=[jax.ShapeDtypeStruct(dproj.shape, BF16),
                   jax.ShapeDtypeStruct((HALO, d), F32), jax.ShapeDtypeStruct((1, d), F32)],
        input_output_aliases={8: 0},
        scratch_shapes=[pltpu.VMEM((HALO + tm, d), F32), pltpu.VMEM((tm + HALO, d), F32), pltpu.VMEM((tm, d), F32)],
        name="conv_bwd", compiler_params=_params())(dc, dc, proj, proj, proj, proj, conv_w, dz, dproj)


def _weight_grad(xs, dy, name, blocked):
    lp, dx = xs.shape
    n = dy.shape[1]
    tk = _mm_row_tile(lp)
    if blocked:
        ncol = n // N_CHIPS
        nt = 3
        tn = ncol // nt
        grid = (N_CHIPS * nt, lp // tk)
        out_spec = pl.BlockSpec((1, dx, tn), lambda c, k: (c // nt, 0, c % nt))
        out_shape = jax.ShapeDtypeStruct((N_CHIPS, dx, ncol), F32)
    else:
        tn = n // 2
        grid = (2, lp // tk)
        out_spec = pl.BlockSpec((dx, tn), lambda c, k: (0, c))
        out_shape = jax.ShapeDtypeStruct((dx, n), F32)

    def body(xs_ref, dy_ref, o_ref):
        @pl.when(pl.program_id(1) == 0)
        def _():
            o_ref[...] = jnp.zeros_like(o_ref)

        p = lax.dot_general(xs_ref[...], dy_ref[...], TN, preferred_element_type=F32)
        if blocked:
            o_ref[0] += p
        else:
            o_ref[...] += p

    return pl.pallas_call(
        body, grid=grid,
        in_specs=[pl.BlockSpec((tk, dx), lambda c, k: (k, 0)), pl.BlockSpec((tk, tn), lambda c, k: (k, c))],
        out_specs=out_spec, out_shape=out_shape,
        name=name, compiler_params=_params())(xs, dy)


def _in_proj_bwd(dproj, wg, hres, norm_g, dout):
    lp, d = hres.shape
    _, _, ncol = wg.shape
    tm = _mm_row_tile(lp)
    nt = 3
    tn = ncol // nt
    nk = N_CHIPS * nt

    def body(dp_ref, w_ref, x_ref, g_ref, dout_ref, dx_ref, dg_ref, acc_ref):
        i = pl.program_id(0)
        kk = pl.program_id(1)

        @pl.when(jnp.logical_and(i == 0, kk == 0))
        def _():
            dg_ref[...] = jnp.zeros_like(dg_ref)

        @pl.when(kk == 0)
        def _():
            acc_ref[...] = jnp.zeros_like(acc_ref)

        acc_ref[...] += lax.dot_general(dp_ref[...], w_ref[0], NT, preferred_element_type=F32)

        @pl.when(kk == nk - 1)
        def _():
            x = x_ref[...]
            r = lax.rsqrt(jnp.mean(x * x, axis=-1, keepdims=True) + EPS)
            xhat = x * r
            dh = acc_ref[...]
            dg_ref[...] += jnp.sum(dh * xhat, axis=0, keepdims=True)
            dxh = dh * g_ref[...]
            dx_ref[...] = dout_ref[...] + r * (dxh - xhat * jnp.mean(dxh * xhat, axis=-1, keepdims=True))

    return pl.pallas_call(
        body, grid=(lp // tm, nk),
        in_specs=[pl.BlockSpec((tm, tn), lambda i, k: (i, k)),
                  pl.BlockSpec((1, d, tn), lambda i, k: (k // nt, 0, k % nt)),
                  pl.BlockSpec((tm, d), lambda i, k: (i, 0)),
                  pl.BlockSpec((1, d), lambda i, k: (0, 0)),
                  pl.BlockSpec((tm, d), lambda i, k: (i, 0))],
        out_specs=[pl.BlockSpec((tm, d), lambda i, k: (i, 0)), pl.BlockSpec((1, d), lambda i, k: (0, 0))],
        out_shape=[jax.ShapeDtypeStruct((lp, d), F32), jax.ShapeDtypeStruct((1, d), F32)],
        scratch_shapes=[pltpu.VMEM((tm, d), F32)],
        name="in_proj_bwd", compiler_params=_params())(dproj, wg, hres, norm_g, dout)


def _adamw_math(w, g, m, v):
    m = ADAM_B1 * m + (1.0 - ADAM_B1) * g
    v = ADAM_B2 * v + (1.0 - ADAM_B2) * (g * g)
    m_hat = m / (1.0 - ADAM_B1 ** ADAM_STEP)
    v_hat = v / (1.0 - ADAM_B2 ** ADAM_STEP)
    delta = -ADAM_LR * (m_hat / (jnp.sqrt(v_hat) + ADAM_EPS) + ADAM_WD * w)
    return delta, m, v


def _elementwise_rows(shape):
    r, c = shape
    for t in (256, 128, 64, 32, 16, 8):
        if r % t == 0 and r > t and t * c * 4 <= 1024 * 1024:
            return t
    return r


def _adamw(name, w, m, v, *g_parts):
    shape = w.shape
    tr = _elementwise_rows(shape)
    n_g = len(g_parts)

    def body(*refs):
        w_ref, m_ref, v_ref = refs[:3]
        g_refs = refs[3:3 + n_g]
        g_out, d_out, m_out, v_out = refs[3 + n_g:]
        g = g_refs[0][...]
        for gr in g_refs[1:]:
            g = g + gr[...]
        delta, m_new, v_new = _adamw_math(w_ref[...], g, m_ref[...], v_ref[...])
        g_out[...] = g
        d_out[...] = delta
        m_out[...] = m_new
        v_out[...] = v_new

    spec = pl.BlockSpec((tr, shape[1]), lambda i: (i, 0))
    return pl.pallas_call(
        body, grid=(shape[0] // tr,),
        in_specs=[spec] * (3 + n_g), out_specs=[spec] * 4,
        out_shape=[jax.ShapeDtypeStruct(shape, F32)] * 4,
        name=name, compiler_params=_params())(w, m, v, *g_parts)


def _chip_half_sum(name, g, recv, core):
    _, _, hr, cols = g.shape
    tr = _elementwise_rows((hr, cols))

    def body(core_ref, g_ref, r_ref, o_ref, ob_ref):
        s = g_ref[0, 0] + r_ref[0]
        o_ref[0] = s
        ob_ref[0] = s.astype(BF16)

    blk = pl.BlockSpec((1, tr, cols), lambda j, i, core_ref: (j, i, 0))
    grid_spec = pltpu.PrefetchScalarGridSpec(
        num_scalar_prefetch=1, grid=(N_CHIPS, hr // tr),
        in_specs=[pl.BlockSpec((1, 1, tr, cols), lambda j, i, core_ref: (j, core_ref[0], i, 0)), blk],
        out_specs=[blk, blk])
    return pl.pallas_call(
        body, grid_spec=grid_spec,
        out_shape=[jax.ShapeDtypeStruct((N_CHIPS, hr, cols), F32), jax.ShapeDtypeStruct((N_CHIPS, hr, cols), BF16)],
        name=name, compiler_params=_params())(core, g, recv)


def _block_half_total(name, chip_sums, recv, chip_core):
    _, hr, cols = chip_sums.shape
    tr = _elementwise_rows((hr, cols))

    def body(cc_ref, p_ref, r_ref, o_ref):
        s = p_ref[0]
        for k in range(3):
            s = s + r_ref[k].astype(F32)
        o_ref[0] = s

    grid_spec = pltpu.PrefetchScalarGridSpec(
        num_scalar_prefetch=1, grid=(hr // tr,),
        in_specs=[pl.BlockSpec((1, tr, cols), lambda i, cc_ref: (cc_ref[0], i, 0)),
                  pl.BlockSpec((3, tr, cols), lambda i, cc_ref: (0, i, 0))],
        out_specs=pl.BlockSpec((1, tr, cols), lambda i, cc_ref: (cc_ref[1], i, 0)))
    return pl.pallas_call(
        body, grid_spec=grid_spec, out_shape=jax.ShapeDtypeStruct((2, hr, cols), F32),
        name=name, compiler_params=_params())(chip_core, chip_sums, recv)


def _place_shard(name, w, chip, dtype):
    r, c = w.shape
    tr = _elementwise_rows((r, c))

    def body(chip_ref, w_ref, o_ref):
        o_ref[0] = w_ref[...].astype(dtype)

    grid_spec = pltpu.PrefetchScalarGridSpec(
        num_scalar_prefetch=1, grid=(r // tr,),
        in_specs=[pl.BlockSpec((tr, c), lambda i, chip_ref: (i, 0))],
        out_specs=pl.BlockSpec((1, tr, c), lambda i, chip_ref: (chip_ref[0], i, 0)))
    return pl.pallas_call(
        body, grid_spec=grid_spec, out_shape=jax.ShapeDtypeStruct((N_CHIPS, r, c), dtype),
        name=name, compiler_params=_params())(chip, w)


def _sum_slots(name, slots):
    k, r, c = slots.shape

    def body(s_ref, o_ref):
        s = s_ref[0]
        for j in range(1, k):
            s = s + s_ref[j]
        o_ref[...] = s

    return pl.pallas_call(body, out_shape=jax.ShapeDtypeStruct((r, c), F32), name=name,
                          compiler_params=_params())(slots)


def _mesh_pos():
    return lax.axis_index("x"), lax.axis_index("y"), lax.axis_index("c")


def _other_chips(x, y):
    return [(1 - x, y), (x, 1 - y), (1 - x, 1 - y)]


def _gather_weights(bufs):
    n = len(bufs)
    half = [b.shape[1] // 2 for b in bufs]

    def body(*refs):
        gathered = refs[n:2 * n]
        ici_send, ici_recv, d2d_send, d2d_recv = refs[2 * n:]
        x, y, c = _mesh_pos()
        me = 2 * x + y
        chips = _other_chips(x, y)

        def part(a, block, core):
            return gathered[a].at[block, pl.ds(core * half[a], half[a])]

        def over_ici(a, k, block):
            px, py = chips[k]
            return pltpu.make_async_remote_copy(
                src_ref=part(a, block, c), dst_ref=part(a, block, c),
                send_sem=ici_send.at[a, k], recv_sem=ici_recv.at[a, k],
                device_id=(px, py, c), device_id_type=MESH)

        def over_d2d(a, k, core):
            px, py = chips[k]
            return pltpu.make_async_remote_copy(
                src_ref=part(a, 2 * px + py, core), dst_ref=part(a, 2 * px + py, core),
                send_sem=d2d_send.at[a, k], recv_sem=d2d_recv.at[a, k],
                device_id=(x, y, 1 - c), device_id_type=MESH)

        for a in range(n):
            for k in range(3):
                over_ici(a, k, me).start()
        for a in range(n):
            for k, (px, py) in enumerate(chips):
                over_ici(a, k, 2 * px + py).wait_recv()
                over_d2d(a, k, c).start()
        for a in range(n):
            for k in range(3):
                over_d2d(a, k, 1 - c).wait_recv()
        for a in range(n):
            for k in range(3):
                over_ici(a, k, me).wait_send()
                over_d2d(a, k, c).wait_send()

    return pl.pallas_call(
        body, in_specs=[ANY] * n, out_specs=[ANY] * n,
        out_shape=[jax.ShapeDtypeStruct(b.shape, b.dtype) for b in bufs],
        input_output_aliases={a: a for a in range(n)},
        scratch_shapes=[pltpu.SemaphoreType.DMA((n, 3))] * 4,
        name="gather_weights")(*bufs)


def _send_other_halves(grads):
    n = len(grads)

    def body(*refs):
        srcs = refs[:n]
        dsts = refs[n:2 * n]
        send_sems, recv_sems = refs[2 * n:]
        x, y, c = _mesh_pos()
        copies = [pltpu.make_async_remote_copy(
            src_ref=srcs[a].at[j, 1 - c], dst_ref=dsts[a].at[j], send_sem=send_sems.at[a, j],
            recv_sem=recv_sems.at[a, j], device_id=(x, y, 1 - c), device_id_type=MESH)
            for a in range(n) for j in range(N_CHIPS)]
        for cp in copies:
            cp.start()
        for cp in copies:
            cp.wait()

    return pl.pallas_call(
        body, in_specs=[ANY] * n, out_specs=[ANY] * n,
        out_shape=[jax.ShapeDtypeStruct((N_CHIPS,) + g.shape[2:], F32) for g in grads],
        scratch_shapes=[pltpu.SemaphoreType.DMA((n, N_CHIPS))] * 2,
        name="send_other_halves")(*grads)


HBM = pl.BlockSpec(memory_space=pltpu.HBM)
SEM = pl.BlockSpec(memory_space=pltpu.SEMAPHORE)


def _block_copies(n, srcs, dsts, send_sems, recv_sems):
    x, y, c = _mesh_pos()
    return [pltpu.make_async_remote_copy(
        src_ref=srcs[a].at[2 * px + py], dst_ref=dsts[a].at[k], send_sem=send_sems.at[3 * a + k],
        recv_sem=recv_sems.at[3 * a + k], device_id=(px, py, c), device_id_type=MESH)
        for a in range(n) for k, (px, py) in enumerate(_other_chips(x, y))]


def _exchange_start(blocked):
    n = len(blocked)
    lands = [lax.empty((3,) + b.shape[1:], b.dtype) for b in blocked]
    bufs = [pltpu.with_memory_space_constraint(b, pltpu.HBM) for b in list(blocked) + lands]
    nb = 2 * n

    def body(*refs):
        for cp in _block_copies(n, refs[:n], refs[n:nb], refs[nb], refs[nb + 1]):
            cp.start()
        refs[-1][...] = jnp.zeros_like(refs[-1])

    out = pl.pallas_call(
        body, name="exchange_start",
        in_specs=[HBM] * nb,
        out_shape=[pltpu.SemaphoreType.DMA((3 * n,)), pltpu.SemaphoreType.DMA((3 * n,))]
        + [pltpu.HBM(b.shape, b.dtype) for b in bufs] + [jax.ShapeDtypeStruct((8, 128), F32)],
        out_specs=[SEM] * 2 + [HBM] * nb + [pl.BlockSpec(memory_space=pltpu.VMEM)],
        input_output_aliases={i: 2 + i for i in range(nb)},
        compiler_params=pltpu.CompilerParams(has_side_effects=pltpu.SideEffectType.DATAFLOW_SIDE_EFFECTING),
    )(*bufs)
    return (out[:2], out[2:2 + nb]), out[-1]


def _exchange_wait(state, after):
    sems, bufs = state
    nb = len(bufs)
    n = nb // 2

    def body(*refs):
        for cp in _block_copies(n, refs[:n], refs[n:nb], refs[nb], refs[nb + 1]):
            cp.wait_send()
            cp.wait_recv()

    out = pl.pallas_call(
        body, name="exchange_wait",
        in_specs=[HBM] * nb + [SEM] * 2 + [ANY],
        out_shape=[pltpu.HBM(b.shape, b.dtype) for b in bufs],
        out_specs=[HBM] * nb,
        input_output_aliases={i: i for i in range(nb)},
        compiler_params=pltpu.CompilerParams(has_side_effects=pltpu.SideEffectType.DATAFLOW_SIDE_EFFECTING),
    )(*bufs, *sems, after)
    return out[n:nb]


def _exchange_small(small):
    def body(small_src, small_dst, ssend_sems, srecv_sems, local_sem):
        x, y, c = _mesh_pos()
        my_idx = 4 * x + 2 * y + c
        local = pltpu.make_async_copy(small_src, small_dst.at[my_idx], local_sem)
        local.start()
        others = []
        for r in range(1, 8):
            px = 1 - x if r & 4 else x
            py = 1 - y if r & 2 else y
            pc = 1 - c if r & 1 else c
            others.append((px, py, pc))
        for r, peer in enumerate(others):
            pltpu.make_async_remote_copy(
                src_ref=small_src, dst_ref=small_dst.at[my_idx], send_sem=ssend_sems.at[r],
                recv_sem=srecv_sems.at[r], device_id=peer, device_id_type=MESH).start()
        for r, (px, py, pc) in enumerate(others):
            pltpu.make_async_remote_copy(
                src_ref=small_src, dst_ref=small_dst.at[4 * px + 2 * py + pc], send_sem=ssend_sems.at[r],
                recv_sem=srecv_sems.at[r], device_id=(px, py, pc), device_id_type=MESH).wait()
        local.wait()

    return pl.pallas_call(
        body, in_specs=[ANY], out_specs=ANY, out_shape=jax.ShapeDtypeStruct((8,) + small.shape, F32),
        scratch_shapes=[pltpu.SemaphoreType.DMA((7,)), pltpu.SemaphoreType.DMA((7,)), pltpu.SemaphoreType.DMA],
        name="exchange_small")(small)


def _join_halves(bufs):
    n = len(bufs)

    def body(*refs):
        joined = refs[n:2 * n]
        send_sems, recv_sems = refs[2 * n:]
        x, y, c = _mesh_pos()
        for a in range(n):
            pltpu.make_async_remote_copy(
                src_ref=joined[a].at[c], dst_ref=joined[a].at[c], send_sem=send_sems.at[a],
                recv_sem=recv_sems.at[a], device_id=(x, y, 1 - c), device_id_type=MESH).start()
        for a in range(n):
            pltpu.make_async_remote_copy(
                src_ref=joined[a].at[c], dst_ref=joined[a].at[1 - c], send_sem=send_sems.at[a],
                recv_sem=recv_sems.at[a], device_id=(x, y, 1 - c), device_id_type=MESH).wait()

    return pl.pallas_call(
        body, in_specs=[ANY] * n, out_specs=[ANY] * n,
        out_shape=[jax.ShapeDtypeStruct(b.shape, b.dtype) for b in bufs],
        input_output_aliases={a: a for a in range(n)},
        scratch_shapes=[pltpu.SemaphoreType.DMA((n,))] * 2,
        name="join_halves")(*bufs)


def kernel(x, meta_tokens, norm_g, w_in, conv_w, conv_b, ln_g, ln_b, w_conv_out, lb_logits, gnorm_g, w_rec_out, w_out, final_g, loss_target, m_meta_tokens, m_norm_g, m_w_in, m_conv_w, m_conv_b, m_ln_g, m_ln_b, m_w_conv_out, m_lb_logits, m_gnorm_g, m_w_rec_out, m_w_out, m_final_g, v_meta_tokens, v_norm_g, v_w_in, v_conv_w, v_conv_b, v_ln_g, v_ln_b, v_w_conv_out, v_lb_logits, v_gnorm_g, v_w_rec_out, v_w_out, v_final_g):
    seq, d = x.shape[1], x.shape[2]
    n_meta = meta_tokens.shape[0]
    n_pad = CHUNK - n_meta
    ds = d // N_CHIPS
    chip = 2 * lax.axis_index("x") + lax.axis_index("y")

    conv_w_pad = jnp.pad(conv_w[0], ((0, HALO - CONV_WIDTH), (0, 0)))
    chip_idx = chip.astype(jnp.int32).reshape(1)
    win_g, sq_g, small_g = _gather_weights([
        _place_shard("place_w_in", w_in[0], chip_idx, BF16),
        _place_shard("place_square", jnp.concatenate([w_conv_out[0], w_rec_out[0], w_out[0]], axis=0),
                     chip_idx, BF16),
        _place_shard("place_small", jnp.concatenate([conv_w_pad, meta_tokens], axis=0), chip_idx, F32)])
    wc_full = sq_g[:, 0:ds].reshape(d, d)
    wr_full = sq_g[:, ds:2 * ds].reshape(d, d)
    wo_full = sq_g[:, 2 * ds:3 * ds].reshape(d, d)
    cw_full = jnp.transpose(small_g[:, 0:HALO], (1, 0, 2)).reshape(HALO, d)
    meta_full = jnp.transpose(small_g[:, HALO:HALO + n_meta], (1, 0, 2)).reshape(n_meta, d)

    hres = jnp.concatenate([jnp.zeros((n_pad, d), F32), meta_full, x[0]], axis=0)
    target = jnp.pad(loss_target[0], ((CHUNK, 0), (0, 0)))
    final_g2 = final_g.reshape(1, d)
    h = _rmsnorm_fwd(hres, norm_g)
    proj = _in_proj(h, win_g)
    c, yc_in, y_conv = _conv_fwd(proj, cw_full, conv_b, ln_g, ln_b, wc_full)
    o, s_all = _hgrn_fwd(proj, lb_logits, n_pad)
    yr_in, merged, y_rec, dout, loss_acc, dfinal_g = _tail_fwd(
        o, proj, y_conv, hres, target, gnorm_g, final_g2, wr_full, wo_full)

    (dyc, dyr, dout_bf, dz, dproj, do, dc, dgnorm_g, dln_g, dln_b) = _tail_bwd(
        dout, proj, y_conv, y_rec, o, c, wo_full, wr_full, wc_full, ln_g, ln_b, gnorm_g)
    dproj, dlb_logits = _hgrn_bwd(proj, do, s_all, lb_logits, n_pad, dproj)
    dproj, dconv_w, dconv_b = _conv_bwd(dc, proj, cw_full, dz, dproj)
    g_win = _weight_grad(h, dproj, "grad_w_in", True)
    g_wc = _weight_grad(yc_in, dyc, "grad_w_conv_out", False)
    g_wr = _weight_grad(yr_in, dyr, "grad_w_rec_out", False)
    g_wo = _weight_grad(merged, dout_bf, "grad_w_out", False)

    g_sq = jnp.concatenate([g.reshape(N_CHIPS, ds, d) for g in (g_wc, g_wr, g_wo)], axis=1)
    grads = [g.reshape(N_CHIPS, 2, g.shape[1] // 2, g.shape[2]) for g in (g_win, g_sq)]
    core = lax.axis_index("c").astype(jnp.int32).reshape(1)
    from_sibling = _send_other_halves(grads)
    chip_sums = [_chip_half_sum("chip_half_sum_" + nm, g, r, core)
                 for nm, g, r in zip(("w_in", "square"), grads, from_sibling)]
    in_flight, token = _exchange_start([s[1] for s in chip_sums])
    dhres, dnorm_g = _in_proj_bwd(dproj, win_g, hres, norm_g + token[0:1, 0:1], dout)
    grad_x = dhres[CHUNK:][None]
    recv_win, recv_sq = _exchange_wait(in_flight, dnorm_g)
    small = jnp.concatenate([dnorm_g, dconv_b, dln_g, dln_b, dlb_logits, dgnorm_g, dfinal_g,
                             dhres[n_pad:CHUNK], dconv_w[:CONV_WIDTH],
                             jnp.zeros((1, d), F32)], axis=0)
    small_slots = _exchange_small(small)
    chip_core = jnp.concatenate([chip_idx, core])
    totals = [_block_half_total("block_half_total_" + nm, s[0], r, chip_core)
              for nm, s, r in zip(("w_in", "square"), chip_sums, (recv_win, recv_sq))]
    gt_win, gt_sq = [t.reshape(2 * t.shape[1], t.shape[2]) for t in _join_halves(totals)]
    small_sum = _sum_slots("sum_small", small_slots)

    res = {}
    res["w_in"] = _adamw("adamw_w_in", w_in[0], m_w_in[0], v_w_in[0], gt_win)
    res["w_conv_out"] = _adamw("adamw_w_conv_out", w_conv_out[0], m_w_conv_out[0], v_w_conv_out[0], gt_sq[0:ds])
    res["w_rec_out"] = _adamw("adamw_w_rec_out", w_rec_out[0], m_w_rec_out[0], v_w_rec_out[0], gt_sq[ds:2 * ds])
    res["w_out"] = _adamw("adamw_w_out", w_out[0], m_w_out[0], v_w_out[0], gt_sq[2 * ds:3 * ds])
    big = {k: tuple(a[None] for a in v) for k, v in res.items()}

    rep_names = ("norm_g", "conv_b", "ln_g", "ln_b", "lb_logits", "gnorm_g", "final_g")
    rep_w = (norm_g, conv_b, ln_g, ln_b, lb_logits, gnorm_g, final_g2)
    rep_m = (m_norm_g, m_conv_b, m_ln_g, m_ln_b, m_lb_logits, m_gnorm_g, m_final_g.reshape(1, d))
    rep_v = (v_norm_g, v_conv_b, v_ln_g, v_ln_b, v_lb_logits, v_gnorm_g, v_final_g.reshape(1, d))
    rep = _adamw("adamw_replicated", jnp.concatenate(rep_w, 0), jnp.concatenate(rep_m, 0),
                 jnp.concatenate(rep_v, 0), small_sum[0:8])
    rep_rows = {"norm_g": (0, 1), "conv_b": (1, 2), "ln_g": (2, 3), "ln_b": (3, 4), "lb_logits": (4, 6),
                "gnorm_g": (6, 7), "final_g": (7, 8)}
    small_out = {}
    for nm in rep_names:
        lo, hi = rep_rows[nm]
        vals = tuple(a[lo:hi] for a in rep)
        if nm == "final_g":
            vals = tuple(a.reshape(d) for a in vals)
        small_out[nm] = vals
    cw_row = 8 + n_meta
    g_meta = lax.dynamic_slice_in_dim(small_sum[8:cw_row], chip * ds, ds, axis=1)
    small_out["meta_tokens"] = _adamw("adamw_meta", meta_tokens, m_meta_tokens, v_meta_tokens, g_meta)
    g_cw = lax.dynamic_slice_in_dim(small_sum[cw_row:cw_row + HALO], chip * ds, ds, axis=1)
    pad_rows = ((0, HALO - CONV_WIDTH), (0, 0))
    cw_res = _adamw("adamw_conv_w", conv_w_pad, jnp.pad(m_conv_w[0], pad_rows),
                    jnp.pad(v_conv_w[0], pad_rows, constant_values=1.0), g_cw)
    small_out["conv_w"] = tuple(a[:CONV_WIDTH][None] for a in cw_res)

    loss = lax.psum(loss_acc[0, 0], ("x", "y", "c"))

    order = ("meta_tokens", "norm_g", "w_in", "conv_w", "conv_b", "ln_g", "ln_b", "w_conv_out", "lb_logits",
             "gnorm_g", "w_rec_out", "w_out", "final_g")
    allres = {**big, **small_out}
    outs = [loss, grad_x]
    for field in range(4):
        outs.extend(allres[nm][field] for nm in order)
    return tuple(outs)
```

```python
import numpy as np

import jax
import jax.numpy as jnp
from jax import lax
from jax.experimental import pallas as pl
from jax.experimental.pallas import tpu as pltpu

F32 = jnp.float32
BF16 = jnp.bfloat16

EPS = 1e-6
CHUNK = 64
N_LEVELS = 6
CONV_WIDTH = 31
HALO = 32
CONV_ROWS = 32
CONV_LANES = 256
HEAD = 128
W_IN_COL_TILES = 1
HEADS_PER_TRIP = 8
N_CHIPS = 4
VMEM_LIMIT_BYTES = 56 * 1024 * 1024

ADAM_LR = 0.001
ADAM_B1 = 0.9
ADAM_B2 = 0.999
ADAM_EPS = 1e-08
ADAM_WD = 0.01
ADAM_STEP = 10

MESH = pl.DeviceIdType.MESH
ANY = pl.BlockSpec(memory_space=pl.ANY)

NT = (((1,), (1,)), ((), ()))
TN = (((0,), (0,)), ((), ()))


def _params(**kw):
    return pltpu.CompilerParams(vmem_limit_bytes=VMEM_LIMIT_BYTES, **kw)


def _sigmoid(x):
    return jax.nn.sigmoid(x)


def _dsilu(x, s):
    return s * (1.0 + x * (1.0 - s))


def _row_tile(lp):
    for t in (320, 256, 192, 128, 64):
        if lp % t == 0:
            return t
    raise ValueError(f"unsupported padded length {lp}")


def _mm_row_tile(lp):
    for t in (832, 640, 320, 256, 192, 128, 64):
        if lp % t == 0:
            return t
    raise ValueError(f"unsupported padded length {lp}")


def _dot3(m_bf16, x):
    hi = x.astype(BF16)
    r1 = x - hi.astype(F32)
    mid = r1.astype(BF16)
    lo = (r1 - mid.astype(F32)).astype(BF16)
    return (jnp.dot(m_bf16, hi, preferred_element_type=F32)
            + jnp.dot(m_bf16, mid, preferred_element_type=F32)
            + jnp.dot(m_bf16, lo, preferred_element_type=F32))


def _dot2(m_bf16, x):
    hi = x.astype(BF16)
    lo = (x - hi.astype(F32)).astype(BF16)
    return (jnp.dot(m_bf16, hi, preferred_element_type=F32)
            + jnp.dot(m_bf16, lo, preferred_element_type=F32))


def _col_to_row(col):
    return jnp.broadcast_to(col, (HEAD, 8)).T[0:1, :]


def _row_to_col(row):
    return jnp.broadcast_to(row, (8, HEAD)).T[:, 0:1]


def _hgrn_tables():
    t = np.arange(CHUNK)
    ltri = (t[None, :] <= t[:, None]).astype(np.float32)
    mats = [ltri]
    for lvl in range(1, N_LEVELS + 1):
        blk = CHUNK >> (lvl - 1)
        mid = (t // blk) * blk + blk // 2
        mats.append(ltri[mid - 1])
    after = (t[None, :] >= t[:, None]).astype(np.float32)
    before = (t[None, :] < t[:, None]).astype(np.float32)
    return jnp.asarray(np.concatenate(mats, 0), BF16), jnp.asarray(np.concatenate([after, before], 1), BF16)


def _rmsnorm_fwd(hres, g):
    lp, d = hres.shape
    tm = _row_tile(lp)

    def body(x_ref, g_ref, h_ref):
        x = x_ref[...]
        r = lax.rsqrt(jnp.mean(x * x, axis=-1, keepdims=True) + EPS)
        h_ref[...] = (x * r * g_ref[...]).astype(BF16)

    return pl.pallas_call(
        body, grid=(lp // tm,),
        in_specs=[pl.BlockSpec((tm, d), lambda i: (i, 0)), pl.BlockSpec((1, d), lambda i: (0, 0))],
        out_specs=pl.BlockSpec((tm, d), lambda i: (i, 0)),
        out_shape=jax.ShapeDtypeStruct((lp, d), BF16),
        name="rmsnorm_fwd", compiler_params=_params())(hres, g)


def _in_proj(h, wg):
    lp, d = h.shape
    _, _, ncol = wg.shape
    tm = _mm_row_tile(lp)
    nt = W_IN_COL_TILES
    tn = ncol // nt

    def body(h_ref, w_ref, o_ref):
        o_ref[...] = jnp.dot(h_ref[...], w_ref[0], preferred_element_type=F32)

    return pl.pallas_call(
        body, grid=(N_CHIPS, nt, lp // tm),
        in_specs=[pl.BlockSpec((tm, d), lambda j, n, i: (i, 0)),
                  pl.BlockSpec((1, d, tn), lambda j, n, i: (j, 0, n))],
        out_specs=pl.BlockSpec((tm, tn), lambda j, n, i: (i, j * nt + n)),
        out_shape=jax.ShapeDtypeStruct((lp, N_CHIPS * ncol), F32),
        name="in_proj", compiler_params=_params())(h, wg)


def _conv_fwd(proj, conv_w, conv_b, ln_g, ln_b, w_conv):
    lp = proj.shape[0]
    d = conv_b.shape[1]
    tm = _row_tile(lp)
    hb = tm // HALO

    def body(ua_ref, ub_ref, z_ref, uap_ref, ubp_ref, cw_ref, cb_ref, lg_ref, lb_ref, w_ref,
             c_ref, ycin_ref, yconv_ref, aext_ref):
        i = pl.program_id(0)
        a_prev = uap_ref[...] * _sigmoid(ubp_ref[...])
        aext_ref[0:HALO, :] = jnp.where(i > 0, a_prev, 0.0)
        aext_ref[HALO:HALO + tm, :] = ua_ref[...] * _sigmoid(ub_ref[...])

        def row_block(r, carry):
            r0 = pl.multiple_of(r * CONV_ROWS, CONV_ROWS)
            for cs in range(d // CONV_LANES):
                cl = slice(cs * CONV_LANES, (cs + 1) * CONV_LANES)
                blk = aext_ref[pl.ds(r0, CONV_ROWS + HALO), cl]
                acc = jnp.zeros((CONV_ROWS, CONV_LANES), F32) + cb_ref[:, cl]
                for b in range(8):
                    sh = blk if b == 0 else pltpu.roll(blk, CONV_ROWS + HALO - b, axis=0)
                    for a in range(5):
                        j = 8 * a + b - 2
                        if 0 <= j < CONV_WIDTH:
                            acc = acc + cw_ref[j:j + 1, cl] * sh[8 * a:8 * a + CONV_ROWS, :]
                c_ref[pl.ds(r0, CONV_ROWS), cl] = acc
            return carry

        lax.fori_loop(0, tm // CONV_ROWS, row_block, 0)

        c = c_ref[...]
        mu = jnp.mean(c, axis=-1, keepdims=True)
        xc = c - mu
        rstd = lax.rsqrt(jnp.mean(xc * xc, axis=-1, keepdims=True) + EPS)
        ln = xc * rstd * lg_ref[...] + lb_ref[...]
        s = ln * _sigmoid(ln)
        z = z_ref[...]
        ycin = (s * (z * _sigmoid(z))).astype(BF16)
        ycin_ref[...] = ycin
        yconv_ref[...] = jnp.dot(ycin, w_ref[...], preferred_element_type=F32)

    row = lambda p: pl.BlockSpec((tm, d), lambda i, p=p: (i, p))
    halo = lambda p: pl.BlockSpec((HALO, d), lambda i, p=p: (jnp.maximum(i * hb - 1, 0), p))
    vec = pl.BlockSpec((1, d), lambda i: (0, 0))
    return pl.pallas_call(
        body, grid=(lp // tm,),
        in_specs=[row(0), row(1), row(2), halo(0), halo(1),
                  pl.BlockSpec((HALO, d), lambda i: (0, 0)), vec, vec, vec,
                  pl.BlockSpec((d, d), lambda i: (0, 0))],
        out_specs=[pl.BlockSpec((tm, d), lambda i: (i, 0))] * 3,
        out_shape=[jax.ShapeDtypeStruct((lp, d), F32), jax.ShapeDtypeStruct((lp, d), BF16),
                   jax.ShapeDtypeStruct((lp, d), F32)],
        scratch_shapes=[pltpu.VMEM((HALO + tm, d), F32)],
        name="conv_fwd", compiler_params=_params())(
            proj, proj, proj, proj, proj, conv_w, conv_b, ln_g, ln_b, w_conv)


def _lower_bound(lbl_ref):
    l0 = lbl_ref[0:1, :]
    l1 = lbl_ref[1:2, :]
    m = jnp.maximum(l0, l1)
    e0 = jnp.exp(l0 - m)
    e1 = jnp.exp(l1 - m)
    p0 = e0 / (e0 + e1)
    return p0, p0 * (e1 / (e0 + e1))


def _level_masks():
    rid = lax.broadcasted_iota(jnp.int32, (CHUNK, 1), 0)
    r2 = lax.broadcasted_iota(jnp.int32, (CHUNK, CHUNK), 0)
    c2 = lax.broadcasted_iota(jnp.int32, (CHUNK, CHUNK), 1)
    out = []
    for lvl in range(1, N_LEVELS + 1):
        blk = CHUNK >> (lvl - 1)
        sh = blk.bit_length() - 1
        upper = (rid & (blk - 1)) >= (blk // 2)
        same = (r2 >> sh) == (c2 >> sh)
        out.append((upper, same))
    return out


def _gates(qr, fr, lb, valid):
    sq = _sigmoid(qr)
    q = qr * sq
    sf = _sigmoid(fr)
    f = lb + (1.0 - lb) * sf
    g = jnp.where(valid, jnp.log(f), 0.0)
    k = jnp.where(valid, 1.0 - f, 0.0)
    return q, sq, f, sf, g, k


def _level_factors(b, r, upper):
    e = jnp.exp(jnp.where(upper, b - r, r - b))
    return jnp.where(upper, e, 0.0), jnp.where(upper, 0.0, e)


def _hgrn_fwd(proj, lb_logits, n_pad):
    lp = proj.shape[0]
    d = lb_logits.shape[1]
    n_heads = d // HEAD
    nc = lp // CHUNK
    tab, _ = _hgrn_tables()
    n_tab = tab.shape[0]

    def body(qr_ref, fr_ref, ir_ref, lbl_ref, tab_ref, o_ref, sall_ref, s_ref, t_ref):
        n = pl.program_id(0)

        @pl.when(n == 0)
        def _():
            s_ref[...] = jnp.zeros_like(s_ref)

        sall_ref[0] = s_ref[...]
        lb_all, _ = _lower_bound(lbl_ref)
        rid = lax.broadcasted_iota(jnp.int32, (CHUNK, 1), 0)
        valid = jnp.logical_or(n > 0, rid >= n_pad)
        f_all = lb_all + (1.0 - lb_all) * _sigmoid(fr_ref[...])
        t_ref[...] = _dot2(tab_ref[...], jnp.where(valid, jnp.log(f_all), 0.0))
        masks = _level_masks()

        def head(h):
            off = h * HEAD if isinstance(h, int) else pl.multiple_of(h * HEAD, HEAD)
            hs = pl.ds(off, HEAD)
            lb = _lower_bound_slice(lbl_ref, hs)
            q, _, _, _, _, k = _gates(qr_ref[:, hs], fr_ref[:, hs], lb, valid)
            v = ir_ref[:, hs]
            b = t_ref[0:CHUNK, hs]
            s0 = s_ref[hs, :]
            o = jnp.dot((q * jnp.exp(b)).astype(BF16), s0.astype(BF16), preferred_element_type=F32)
            o = o + jnp.sum(q * k, axis=-1, keepdims=True) * v
            a = jnp.zeros((CHUNK, CHUNK), F32)
            for lvl in range(1, N_LEVELS + 1):
                upper, same = masks[lvl - 1]
                eq, ek = _level_factors(b, t_ref[CHUNK * lvl:CHUNK * (lvl + 1), hs], upper)
                p = lax.dot_general((q * eq).astype(BF16), (k * ek).astype(BF16), NT, preferred_element_type=F32)
                a = a + jnp.where(same, p, 0.0)
            vb = v.astype(BF16)
            o_ref[:, hs] = o + jnp.dot(a.astype(BF16), vb, preferred_element_type=F32)
            b_last = t_ref[CHUNK - 1:CHUNK, hs]
            khat = (k * jnp.exp(b_last - b)).astype(BF16)
            s_ref[hs, :] = _row_to_col(jnp.exp(b_last)) * s0 + lax.dot_general(khat, vb, TN, preferred_element_type=F32)
        per_trip = min(HEADS_PER_TRIP, n_heads)

        def head_group(p, carry):
            for u in range(per_trip):
                head(p * per_trip + u)
            return carry

        if n_heads == per_trip:
            head_group(0, 0)
        else:
            lax.fori_loop(0, n_heads // per_trip, head_group, 0)

    piece = lambda p: pl.BlockSpec((CHUNK, d), lambda n, p=p: (n, p))
    return pl.pallas_call(
        body, grid=(nc,),
        in_specs=[piece(3), piece(4), piece(5), pl.BlockSpec((2, d), lambda n: (0, 0)),
                  pl.BlockSpec((n_tab, CHUNK), lambda n: (0, 0))],
        out_specs=[pl.BlockSpec((CHUNK, d), lambda n: (n, 0)), pl.BlockSpec((1, d, HEAD), lambda n: (n, 0, 0))],
        out_shape=[jax.ShapeDtypeStruct((lp, d), F32), jax.ShapeDtypeStruct((nc, d, HEAD), F32)],
        scratch_shapes=[pltpu.VMEM((d, HEAD), F32), pltpu.VMEM((n_tab, d), F32)],
        name="hgrn_fwd", compiler_params=_params())(proj, proj, proj, lb_logits, tab)


def _lower_bound_slice(lbl_ref, hs):
    l0 = lbl_ref[0:1, hs]
    l1 = lbl_ref[1:2, hs]
    m = jnp.maximum(l0, l1)
    e0 = jnp.exp(l0 - m)
    e1 = jnp.exp(l1 - m)
    return e0 / (e0 + e1)


def _tail_fwd(o, proj, y_conv, hres, target, gnorm_g, final_g, w_rec, w_out):
    lp, d = o.shape
    n_heads = d // HEAD
    tm = _row_tile(lp)

    def body(o_ref, gr_ref, mc_ref, mr_ref, yc_ref, x_ref, t_ref, gn_ref, fg_ref, wr_ref, wo_ref,
             yrin_ref, mg_ref, yrec_ref, dout_ref, loss_ref, dfg_ref):
        i = pl.program_id(0)

        @pl.when(i == 0)
        def _():
            loss_ref[...] = jnp.zeros_like(loss_ref)
            dfg_ref[...] = jnp.zeros_like(dfg_ref)

        for h in range(n_heads):
            hs = slice(h * HEAD, (h + 1) * HEAD)
            oh = o_ref[:, hs]
            on = oh * lax.rsqrt(jnp.mean(oh * oh, axis=-1, keepdims=True) + EPS) * gn_ref[:, hs]
            gr = gr_ref[:, hs]
            yrin_ref[:, hs] = (on * (gr * _sigmoid(gr))).astype(BF16)
        yrec = jnp.dot(yrin_ref[...], wr_ref[...], preferred_element_type=F32)
        yrec_ref[...] = yrec
        merged = (_sigmoid(mc_ref[...]) * yc_ref[...] + _sigmoid(mr_ref[...]) * yrec).astype(BF16)
        mg_ref[...] = merged
        out = x_ref[...] + jnp.dot(merged, wo_ref[...], preferred_element_type=F32)
        r = lax.rsqrt(jnp.mean(out * out, axis=-1, keepdims=True) + EPS)
        yhat = out * r
        fg = fg_ref[...]
        rid = lax.broadcasted_iota(jnp.int32, (tm, 1), 0) + i * tm
        err = jnp.where(rid >= CHUNK, yhat * fg - t_ref[...], 0.0)
        loss_ref[...] += 0.5 * jnp.sum(err * err) / d
        dy = err / d
        dfg_ref[...] += jnp.sum(dy * yhat, axis=0, keepdims=True)
        dyh = dy * fg
        dout_ref[...] = r * (dyh - yhat * jnp.mean(dyh * yhat, axis=-1, keepdims=True))

    row = lambda p: pl.BlockSpec((tm, d), lambda i, p=p: (i, p))
    vec = pl.BlockSpec((1, d), lambda i: (0, 0))
    mat = pl.BlockSpec((d, d), lambda i: (0, 0))
    return pl.pallas_call(
        body, grid=(lp // tm,),
        in_specs=[row(0), row(6), row(7), row(8), row(0), row(0), row(0), vec, vec, mat, mat],
        out_specs=[row(0), row(0), row(0), row(0), pl.BlockSpec((8, 128), lambda i: (0, 0)), vec],
        out_shape=[jax.ShapeDtypeStruct((lp, d), BF16), jax.ShapeDtypeStruct((lp, d), BF16),
                   jax.ShapeDtypeStruct((lp, d), F32), jax.ShapeDtypeStruct((lp, d), F32),
                   jax.ShapeDtypeStruct((8, 128), F32), jax.ShapeDtypeStruct((1, d), F32)],
        name="tail_fwd", compiler_params=_params())(
            o, proj, proj, proj, y_conv, hres, target, gnorm_g, final_g, w_rec, w_out)


def _tail_bwd(dout, proj, y_conv, y_rec, o, c, w_out, w_rec, w_conv, ln_g, ln_b, gnorm_g):
    lp, d = dout.shape
    n_heads = d // HEAD
    tm = _row_tile(lp)

    def body(dout_ref, mc_ref, mr_ref, z_ref, gr_ref, yc_ref, yrec_ref, o_ref, c_ref,
             wo_ref, wr_ref, wc_ref, lg_ref, lb_ref, gn_ref,
             dyc_ref, dyr_ref, doutb_ref, dz_ref, dp_ref, do_ref, dc_ref,
             dgn_ref, dlg_ref, dlb_ref, dyrin_ref):
        i = pl.program_id(0)

        @pl.when(i == 0)
        def _():
            dgn_ref[...] = jnp.zeros_like(dgn_ref)
            dlg_ref[...] = jnp.zeros_like(dlg_ref)
            dlb_ref[...] = jnp.zeros_like(dlb_ref)

        doutb = dout_ref[...].astype(BF16)
        doutb_ref[...] = doutb
        dmerged = lax.dot_general(doutb, wo_ref[...], NT, preferred_element_type=F32)
        smc = _sigmoid(mc_ref[...])
        smr = _sigmoid(mr_ref[...])
        dyc = (dmerged * smc).astype(BF16)
        dyr = (dmerged * smr).astype(BF16)
        dyc_ref[...] = dyc
        dyr_ref[...] = dyr
        dp_ref[:, d:2 * d] = (dmerged * yc_ref[...] * smc * (1.0 - smc)).astype(BF16)
        dp_ref[:, 2 * d:3 * d] = (dmerged * yrec_ref[...] * smr * (1.0 - smr)).astype(BF16)

        dyrin_ref[...] = lax.dot_general(dyr, wr_ref[...], NT, preferred_element_type=F32)
        for h in range(n_heads):
            hs = slice(h * HEAD, (h + 1) * HEAD)
            oh = o_ref[:, hs]
            rstd = lax.rsqrt(jnp.mean(oh * oh, axis=-1, keepdims=True) + EPS)
            ohat = oh * rstd
            gn = gn_ref[:, hs]
            gr = gr_ref[:, hs]
            sg = _sigmoid(gr)
            dyrin = dyrin_ref[:, hs]
            don = dyrin * (gr * sg)
            dp_ref[:, hs] = (dyrin * (ohat * gn) * _dsilu(gr, sg)).astype(BF16)
            dgn_ref[:, hs] += jnp.sum(don * ohat, axis=0, keepdims=True)
            doh = don * gn
            do_ref[:, hs] = rstd * (doh - ohat * jnp.mean(doh * ohat, axis=-1, keepdims=True))

        dycin = lax.dot_general(dyc, wc_ref[...], NT, preferred_element_type=F32)
        c = c_ref[...]
        mu = jnp.mean(c, axis=-1, keepdims=True)
        xc = c - mu
        rstd = lax.rsqrt(jnp.mean(xc * xc, axis=-1, keepdims=True) + EPS)
        nrm = xc * rstd
        lg = lg_ref[...]
        ln = nrm * lg + lb_ref[...]
        sl = _sigmoid(ln)
        z = z_ref[...]
        sz = _sigmoid(z)
        dz_ref[...] = (dycin * (ln * sl) * _dsilu(z, sz)).astype(BF16)
        dln = dycin * (z * sz) * _dsilu(ln, sl)
        dlg_ref[...] += jnp.sum(dln * nrm, axis=0, keepdims=True)
        dlb_ref[...] += jnp.sum(dln, axis=0, keepdims=True)
        dn = dln * lg
        dc_ref[...] = rstd * (dn - jnp.mean(dn, axis=-1, keepdims=True)
                              - nrm * jnp.mean(dn * nrm, axis=-1, keepdims=True))

    row = lambda p: pl.BlockSpec((tm, d), lambda i, p=p: (i, p))
    vec = pl.BlockSpec((1, d), lambda i: (0, 0))
    mat = pl.BlockSpec((d, d), lambda i: (0, 0))
    act_bf = jax.ShapeDtypeStruct((lp, d), BF16)
    act_f32 = jax.ShapeDtypeStruct((lp, d), F32)
    vec_f32 = jax.ShapeDtypeStruct((1, d), F32)
    return pl.pallas_call(
        body, grid=(lp // tm,),
        in_specs=[row(0), row(7), row(8), row(2), row(6), row(0), row(0), row(0), row(0),
                  mat, mat, mat, vec, vec, vec],
        out_specs=[row(0)] * 4 + [pl.BlockSpec((tm, 3 * d), lambda i: (i, 2))] + [row(0)] * 2 + [vec] * 3,
        out_shape=[act_bf] * 4 + [jax.ShapeDtypeStruct((lp, 9 * d), BF16)] + [act_f32] * 2 + [vec_f32] * 3,
        scratch_shapes=[pltpu.VMEM((tm, d), F32)],
        name="tail_bwd", compiler_params=_params())(
            dout, proj, proj, proj, proj, y_conv, y_rec, o, c, w_out, w_rec, w_conv, ln_g, ln_b, gnorm_g)


def _hgrn_bwd(proj, do, s_all, lb_logits, n_pad, dproj):
    lp, d = do.shape
    n_heads = d // HEAD
    nc = lp // CHUNK
    tab, utri = _hgrn_tables()
    n_tab = tab.shape[0]

    def body(qr_ref, fr_ref, ir_ref, do_ref, s0_ref, lbl_ref, tab_ref, ut_ref, _,
             dp_ref, dlbl_ref, ds_ref, t_ref, dlb_ref):
        n = pl.program_id(0)
        chunk = nc - 1 - n

        @pl.when(n == 0)
        def _():
            ds_ref[...] = jnp.zeros_like(ds_ref)
            dlb_ref[...] = jnp.zeros_like(dlb_ref)

        lb_all, pp = _lower_bound(lbl_ref)
        rid = lax.broadcasted_iota(jnp.int32, (CHUNK, 1), 0)
        valid = jnp.logical_or(chunk > 0, rid >= n_pad)
        f_all = lb_all + (1.0 - lb_all) * _sigmoid(fr_ref[...])
        t_ref[...] = _dot2(tab_ref[...], jnp.where(valid, jnp.log(f_all), 0.0))
        masks = _level_masks()
        ut = ut_ref[...]

        def head(h):
            off = h * HEAD if isinstance(h, int) else pl.multiple_of(h * HEAD, HEAD)
            hs = pl.ds(off, HEAD)
            lb = _lower_bound_slice(lbl_ref, hs)
            qr = qr_ref[:, hs]
            q, sq, f, sf, _, k = _gates(qr, fr_ref[:, hs], lb, valid)
            v = ir_ref[:, hs]
            do_h = do_ref[:, hs]
            b = t_ref[0:CHUNK, hs]
            b_last = t_ref[CHUNK - 1:CHUNK, hs]
            s0 = s0_ref[0, hs, :]
            ds1 = ds_ref[hs, :]
            eb = jnp.exp(b)
            ekl = jnp.exp(b_last - b)
            do_bf = do_h.astype(BF16)
            v_bf = v.astype(BF16)
            ds1_bf = ds1.astype(BF16)

            da = lax.dot_general(do_bf, v_bf, NT, preferred_element_type=F32)
            da_diag = jnp.sum(do_h * v, axis=-1, keepdims=True)
            a = jnp.zeros((CHUNK, CHUNK), F32)
            dq_x = eb * lax.dot_general(do_bf, s0.astype(BF16), NT, preferred_element_type=F32)
            dk_x = ekl * lax.dot_general(v_bf, ds1_bf, NT, preferred_element_type=F32)
            x_after = q * dq_x
            x_before = k * dk_x
            for lvl in range(1, N_LEVELS + 1):
                upper, same = masks[lvl - 1]
                eq, ek = _level_factors(b, t_ref[CHUNK * lvl:CHUNK * (lvl + 1), hs], upper)
                qt = (q * eq).astype(BF16)
                kt = (k * ek).astype(BF16)
                p = lax.dot_general(qt, kt, NT, preferred_element_type=F32)
                a = a + jnp.where(same, p, 0.0)
                dam = jnp.where(same, da, 0.0).astype(BF16)
                dqt = jnp.dot(dam, kt, preferred_element_type=F32)
                dkt = lax.dot_general(dam, qt, TN, preferred_element_type=F32)
                dq_x = dq_x + eq * dqt
                dk_x = dk_x + ek * dkt
                x_after = x_after + (qt.astype(F32) * dqt - kt.astype(F32) * dkt)

            dv = (lax.dot_general(a.astype(BF16), do_bf, TN, preferred_element_type=F32)
                  + jnp.sum(q * k, axis=-1, keepdims=True) * do_h
                  + jnp.dot((k * ekl).astype(BF16), ds1_bf, preferred_element_type=F32))
            dp_ref[:, pl.ds(2 * d + off, HEAD)] = dv.astype(BF16)

            carried = jnp.exp(b_last) * _col_to_row(jnp.sum(s0 * ds1, axis=-1, keepdims=True))
            dg = _dot3(ut, jnp.concatenate([x_after, x_before], axis=0)) + carried
            dq = dq_x + da_diag * k
            dk = dk_x + da_diag * q
            dp_ref[:, hs] = (dq * _dsilu(qr, sq)).astype(BF16)
            df = jnp.where(valid, dg / f - dk, 0.0)
            dp_ref[:, pl.ds(d + off, HEAD)] = (df * (1.0 - lb) * sf * (1.0 - sf)).astype(BF16)
            dlb_ref[:, hs] += jnp.sum(df * (1.0 - sf), axis=0, keepdims=True)

            ds_ref[hs, :] = (_row_to_col(jnp.exp(b_last)) * ds1
                             + lax.dot_general((q * eb).astype(BF16), do_bf, TN, preferred_element_type=F32))
        per_trip = min(HEADS_PER_TRIP, n_heads)

        def head_group(p, carry):
            for u in range(per_trip):
                head(p * per_trip + u)
            return carry

        if n_heads == per_trip:
            head_group(0, 0)
        else:
            lax.fori_loop(0, n_heads // per_trip, head_group, 0)

        @pl.when(n == nc - 1)
        def _():
            dl0 = dlb_ref[...] * pp
            dlbl_ref[0:1, :] = dl0
            dlbl_ref[1:2, :] = -dl0

    piece = lambda p: pl.BlockSpec((CHUNK, d), lambda n, p=p: (nc - 1 - n, p))
    return pl.pallas_call(
        body, grid=(nc,),
        in_specs=[piece(3), piece(4), piece(5), piece(0),
                  pl.BlockSpec((1, d, HEAD), lambda n: (nc - 1 - n, 0, 0)),
                  pl.BlockSpec((2, d), lambda n: (0, 0)),
                  pl.BlockSpec((n_tab, CHUNK), lambda n: (0, 0)),
                  pl.BlockSpec((CHUNK, 2 * CHUNK), lambda n: (0, 0)), ANY],
        out_specs=[pl.BlockSpec((CHUNK, 3 * d), lambda n: (nc - 1 - n, 1)), pl.BlockSpec((2, d), lambda n: (0, 0))],
        out_shape=[jax.ShapeDtypeStruct(dproj.shape, BF16), jax.ShapeDtypeStruct((2, d), F32)],
        input_output_aliases={8: 0},
        scratch_shapes=[pltpu.VMEM((d, HEAD), F32), pltpu.VMEM((n_tab, d), F32), pltpu.VMEM((1, d), F32)],
        name="hgrn_bwd", compiler_params=_params())(proj, proj, proj, do, s_all, lb_logits, tab, utri, dproj)


def _conv_bwd(dc, proj, conv_w, dz, dproj):
    lp, d = dc.shape
    tm = _row_tile(lp)
    hb = tm // HALO
    n_tiles = lp // tm
    last_halo = lp // HALO - 1

    def body(dc_ref, dcn_ref, ua_ref, ub_ref, uap_ref, ubp_ref, cw_ref, dz_ref, _,
             dp_ref, dcw_ref, dcb_ref, aext_ref, dcext_ref, da_ref):
        i = pl.program_id(0)

        @pl.when(i == 0)
        def _():
            dcw_ref[...] = jnp.zeros_like(dcw_ref)
            dcb_ref[...] = jnp.zeros_like(dcb_ref)

        ua = ua_ref[...]
        sb = _sigmoid(ub_ref[...])
        a_prev = uap_ref[...] * _sigmoid(ubp_ref[...])
        aext_ref[0:HALO, :] = jnp.where(i > 0, a_prev, 0.0)
        aext_ref[HALO:HALO + tm, :] = ua * sb
        dcext_ref[0:tm, :] = dc_ref[...]
        dcext_ref[tm:tm + HALO, :] = jnp.where(i < n_tiles - 1, dcn_ref[...], 0.0)
        dcb_ref[...] += jnp.sum(dc_ref[...], axis=0, keepdims=True)

        def row_block(r, carry):
            r0 = pl.multiple_of(r * CONV_ROWS, CONV_ROWS)
            n_rows = CONV_ROWS + HALO
            for cs in range(d // CONV_LANES):
                cl = slice(cs * CONV_LANES, (cs + 1) * CONV_LANES)
                dblk = dcext_ref[pl.ds(r0, n_rows), cl]
                ablk = aext_ref[pl.ds(r0, n_rows), cl]
                dcur = dblk[0:CONV_ROWS, :]
                acc = jnp.zeros((CONV_ROWS, CONV_LANES), F32)
                for b in range(8):
                    dsh = dblk if b == 0 else pltpu.roll(dblk, n_rows - b, axis=0)
                    ash = ablk if b == 0 else pltpu.roll(ablk, n_rows - b, axis=0)
                    for a in range(5):
                        j_da = CONV_WIDTH - 1 - (8 * a + b)
                        if 0 <= j_da < CONV_WIDTH:
                            acc = acc + cw_ref[j_da:j_da + 1, cl] * dsh[8 * a:8 * a + CONV_ROWS, :]
                        j_w = 8 * a + b - 2
                        if 0 <= j_w < CONV_WIDTH:
                            dcw_ref[j_w:j_w + 1, cl] += jnp.sum(
                                dcur * ash[8 * a:8 * a + CONV_ROWS, :], axis=0, keepdims=True)
                da_ref[pl.ds(r0, CONV_ROWS), cl] = acc
            return carry

        lax.fori_loop(0, tm // CONV_ROWS, row_block, 0)

        da = da_ref[...]
        dp_ref[:, 0:d] = (da * sb).astype(BF16)
        dp_ref[:, d:2 * d] = (da * ua * sb * (1.0 - sb)).astype(BF16)
        dp_ref[:, 2 * d:3 * d] = dz_ref[...]

    row = lambda p: pl.BlockSpec((tm, d), lambda i, p=p: (i, p))
    prev = lambda p: pl.BlockSpec((HALO, d), lambda i, p=p: (jnp.maximum(i * hb - 1, 0), p))
    nxt = pl.BlockSpec((HALO, d), lambda i: (jnp.minimum((i + 1) * hb, last_halo), 0))
    return pl.pallas_call(
        body, grid=(n_tiles,),
        in_specs=[row(0), nxt, row(0), row(1), prev(0), prev(1), pl.BlockSpec((HALO, d), lambda i: (0, 0)),
                  row(0), ANY],
        out_specs=[pl.BlockSpec((tm, 3 * d), lambda i: (i, 0)), pl.BlockSpec((HALO, d), lambda i: (0, 0)),
                   pl.BlockSpec((1, d), lambda i: (0, 0))],
        out_shape=[jax.ShapeDtypeStruct(dproj.shape, BF16),
                   jax.ShapeDtypeStruct((HALO, d), F32), jax.ShapeDtypeStruct((1, d), F32)],
        input_output_aliases={8: 0},
        scratch_shapes=[pltpu.VMEM((HALO + tm, d), F32), pltpu.VMEM((tm + HALO, d), F32), pltpu.VMEM((tm, d), F32)],
        name="conv_bwd", compiler_params=_params())(dc, dc, proj, proj, proj, proj, conv_w, dz, dproj)


def _weight_grad(xs, dy, name, blocked):
    lp, dx = xs.shape
    n = dy.shape[1]
    tk = _mm_row_tile(lp)
    if blocked:
        ncol = n // N_CHIPS
        nt = W_IN_COL_TILES
        tn = ncol // nt
        grid = (N_CHIPS * nt, lp // tk)
        out_spec = pl.BlockSpec((1, dx, tn), lambda c, k: (c // nt, 0, c % nt))
        out_shape = jax.ShapeDtypeStruct((N_CHIPS, dx, ncol), F32)
    else:
        tn = n // 2
        grid = (2, lp // tk)
        out_spec = pl.BlockSpec((dx, tn), lambda c, k: (0, c))
        out_shape = jax.ShapeDtypeStruct((dx, n), F32)

    def body(xs_ref, dy_ref, o_ref):
        @pl.when(pl.program_id(1) == 0)
        def _():
            o_ref[...] = jnp.zeros_like(o_ref)

        p = lax.dot_general(xs_ref[...], dy_ref[...], TN, preferred_element_type=F32)
        if blocked:
            o_ref[0] += p
        else:
            o_ref[...] += p

    return pl.pallas_call(
        body, grid=grid,
        in_specs=[pl.BlockSpec((tk, dx), lambda c, k: (k, 0)), pl.BlockSpec((tk, tn), lambda c, k: (k, c))],
        out_specs=out_spec, out_shape=out_shape,
        name=name, compiler_params=_params())(xs, dy)


def _in_proj_bwd(dproj, wtg, hres, norm_g, dout):
    lp, d = hres.shape
    _, ncol, _ = wtg.shape
    tm = _mm_row_tile(lp)
    nt = W_IN_COL_TILES
    tn = ncol // nt
    nk = N_CHIPS * nt

    def body(dp_ref, w_ref, x_ref, g_ref, dout_ref, dx_ref, dg_ref, acc_ref):
        i = pl.program_id(0)
        kk = pl.program_id(1)

        @pl.when(jnp.logical_and(i == 0, kk == 0))
        def _():
            dg_ref[...] = jnp.zeros_like(dg_ref)

        @pl.when(kk == 0)
        def _():
            acc_ref[...] = jnp.zeros_like(acc_ref)

        acc_ref[...] += jnp.dot(dp_ref[...], w_ref[0], preferred_element_type=F32)

        @pl.when(kk == nk - 1)
        def _():
            x = x_ref[...]
            r = lax.rsqrt(jnp.mean(x * x, axis=-1, keepdims=True) + EPS)
            xhat = x * r
            dh = acc_ref[...]
            dg_ref[...] += jnp.sum(dh * xhat, axis=0, keepdims=True)
            dxh = dh * g_ref[...]
            dx_ref[...] = dout_ref[...] + r * (dxh - xhat * jnp.mean(dxh * xhat, axis=-1, keepdims=True))

    return pl.pallas_call(
        body, grid=(lp // tm, nk),
        in_specs=[pl.BlockSpec((tm, tn), lambda i, k: (i, k)),
                  pl.BlockSpec((1, tn, d), lambda i, k: (k // nt, k % nt, 0)),
                  pl.BlockSpec((tm, d), lambda i, k: (i, 0)),
                  pl.BlockSpec((1, d), lambda i, k: (0, 0)),
                  pl.BlockSpec((tm, d), lambda i, k: (i, 0))],
        out_specs=[pl.BlockSpec((tm, d), lambda i, k: (i, 0)), pl.BlockSpec((1, d), lambda i, k: (0, 0))],
        out_shape=[jax.ShapeDtypeStruct((lp, d), F32), jax.ShapeDtypeStruct((1, d), F32)],
        scratch_shapes=[pltpu.VMEM((tm, d), F32)],
        name="in_proj_bwd", compiler_params=_params())(dproj, wtg, hres, norm_g, dout)


def _adamw_math(w, g, m, v):
    m = ADAM_B1 * m + (1.0 - ADAM_B1) * g
    v = ADAM_B2 * v + (1.0 - ADAM_B2) * (g * g)
    m_hat = m / (1.0 - ADAM_B1 ** ADAM_STEP)
    v_hat = v / (1.0 - ADAM_B2 ** ADAM_STEP)
    delta = -ADAM_LR * (m_hat / (jnp.sqrt(v_hat) + ADAM_EPS) + ADAM_WD * w)
    return delta, m, v


def _elementwise_rows(shape):
    r, c = shape
    for t in (256, 128, 64, 32, 16, 8):
        if r % t == 0 and r > t and t * c * 4 <= 1024 * 1024:
            return t
    return r


def _adamw(name, w, m, v, *g_parts):
    shape = w.shape
    tr = _elementwise_rows(shape)
    n_g = len(g_parts)

    def body(*refs):
        w_ref, m_ref, v_ref = refs[:3]
        g_refs = refs[3:3 + n_g]
        g_out, d_out, m_out, v_out = refs[3 + n_g:]
        g = g_refs[0][...]
        for gr in g_refs[1:]:
            g = g + gr[...]
        delta, m_new, v_new = _adamw_math(w_ref[...], g, m_ref[...], v_ref[...])
        g_out[...] = g
        d_out[...] = delta
        m_out[...] = m_new
        v_out[...] = v_new

    spec = pl.BlockSpec((tr, shape[1]), lambda i: (i, 0))
    return pl.pallas_call(
        body, grid=(shape[0] // tr,),
        in_specs=[spec] * (3 + n_g), out_specs=[spec] * 4,
        out_shape=[jax.ShapeDtypeStruct(shape, F32)] * 4,
        name=name, compiler_params=_params())(w, m, v, *g_parts)


def _chip_half_sum(name, g, recv, core):
    _, _, hr, cols = g.shape
    tr = _elementwise_rows((hr, cols))

    def body(core_ref, g_ref, r_ref, o_ref, ob_ref):
        s = g_ref[0, 0] + r_ref[0]
        o_ref[0] = s
        ob_ref[0] = s.astype(BF16)

    blk = pl.BlockSpec((1, tr, cols), lambda j, i, core_ref: (j, i, 0))
    grid_spec = pltpu.PrefetchScalarGridSpec(
        num_scalar_prefetch=1, grid=(N_CHIPS, hr // tr),
        in_specs=[pl.BlockSpec((1, 1, tr, cols), lambda j, i, core_ref: (j, core_ref[0], i, 0)), blk],
        out_specs=[blk, blk])
    return pl.pallas_call(
        body, grid_spec=grid_spec,
        out_shape=[jax.ShapeDtypeStruct((N_CHIPS, hr, cols), F32), jax.ShapeDtypeStruct((N_CHIPS, hr, cols), BF16)],
        name=name, compiler_params=_params())(core, g, recv)


def _block_half_total(name, chip_sums, recv, chip_core):
    _, hr, cols = chip_sums.shape
    tr = _elementwise_rows((hr, cols))

    def body(cc_ref, p_ref, r_ref, o_ref):
        s = p_ref[0]
        for k in range(3):
            s = s + r_ref[k].astype(F32)
        o_ref[0] = s

    grid_spec = pltpu.PrefetchScalarGridSpec(
        num_scalar_prefetch=1, grid=(hr // tr,),
        in_specs=[pl.BlockSpec((1, tr, cols), lambda i, cc_ref: (cc_ref[0], i, 0)),
                  pl.BlockSpec((3, tr, cols), lambda i, cc_ref: (0, i, 0))],
        out_specs=pl.BlockSpec((1, tr, cols), lambda i, cc_ref: (cc_ref[1], i, 0)))
    return pl.pallas_call(
        body, grid_spec=grid_spec, out_shape=jax.ShapeDtypeStruct((2, hr, cols), F32),
        name=name, compiler_params=_params())(chip_core, chip_sums, recv)


def _place_shard(name, w, chip, dtype):
    r, c = w.shape
    tr = _elementwise_rows((r, c))

    def body(chip_ref, w_ref, o_ref):
        o_ref[0] = w_ref[...].astype(dtype)

    grid_spec = pltpu.PrefetchScalarGridSpec(
        num_scalar_prefetch=1, grid=(r // tr,),
        in_specs=[pl.BlockSpec((tr, c), lambda i, chip_ref: (i, 0))],
        out_specs=pl.BlockSpec((1, tr, c), lambda i, chip_ref: (chip_ref[0], i, 0)))
    return pl.pallas_call(
        body, grid_spec=grid_spec, out_shape=jax.ShapeDtypeStruct((N_CHIPS, r, c), dtype),
        name=name, compiler_params=_params())(chip, w)


def _sum_slots(name, slots):
    k, r, c = slots.shape

    def body(s_ref, o_ref):
        s = s_ref[0]
        for j in range(1, k):
            s = s + s_ref[j]
        o_ref[...] = s

    return pl.pallas_call(body, out_shape=jax.ShapeDtypeStruct((r, c), F32), name=name,
                          compiler_params=_params())(slots)


def _mesh_pos():
    return lax.axis_index("x"), lax.axis_index("y"), lax.axis_index("c")


def _other_chips(x, y):
    return [(1 - x, y), (x, 1 - y), (1 - x, 1 - y)]


def _gather_weights(bufs):
    n = len(bufs)
    half = [b.shape[1] // 2 for b in bufs]

    def body(*refs):
        gathered = refs[n:2 * n]
        ici_send, ici_recv, d2d_send, d2d_recv = refs[2 * n:]
        x, y, c = _mesh_pos()
        me = 2 * x + y
        chips = _other_chips(x, y)

        def part(a, block, core):
            return gathered[a].at[block, pl.ds(core * half[a], half[a])]

        def over_ici(a, k, block):
            px, py = chips[k]
            return pltpu.make_async_remote_copy(
                src_ref=part(a, block, c), dst_ref=part(a, block, c),
                send_sem=ici_send.at[a, k], recv_sem=ici_recv.at[a, k],
                device_id=(px, py, c), device_id_type=MESH)

        def over_d2d(a, k, core):
            px, py = chips[k]
            return pltpu.make_async_remote_copy(
                src_ref=part(a, 2 * px + py, core), dst_ref=part(a, 2 * px + py, core),
                send_sem=d2d_send.at[a, k], recv_sem=d2d_recv.at[a, k],
                device_id=(x, y, 1 - c), device_id_type=MESH)

        for a in range(n):
            for k in range(3):
                over_ici(a, k, me).start()
        for a in range(n):
            for k, (px, py) in enumerate(chips):
                over_ici(a, k, 2 * px + py).wait_recv()
                over_d2d(a, k, c).start()
        for a in range(n):
            for k in range(3):
                over_d2d(a, k, 1 - c).wait_recv()
        for a in range(n):
            for k in range(3):
                over_ici(a, k, me).wait_send()
                over_d2d(a, k, c).wait_send()

    return pl.pallas_call(
        body, in_specs=[ANY] * n, out_specs=[ANY] * n,
        out_shape=[jax.ShapeDtypeStruct(b.shape, b.dtype) for b in bufs],
        input_output_aliases={a: a for a in range(n)},
        scratch_shapes=[pltpu.SemaphoreType.DMA((n, 3))] * 4,
        name="gather_weights")(*bufs)


def _send_other_halves(grads, tag):
    n = len(grads)

    def body(*refs):
        srcs = refs[:n]
        dsts = refs[n:2 * n]
        send_sems, recv_sems = refs[2 * n:]
        x, y, c = _mesh_pos()
        copies = [pltpu.make_async_remote_copy(
            src_ref=srcs[a].at[j, 1 - c], dst_ref=dsts[a].at[j], send_sem=send_sems.at[a, j],
            recv_sem=recv_sems.at[a, j], device_id=(x, y, 1 - c), device_id_type=MESH)
            for a in range(n) for j in range(N_CHIPS)]
        for cp in copies:
            cp.start()
        for cp in copies:
            cp.wait()

    return pl.pallas_call(
        body, in_specs=[ANY] * n, out_specs=[ANY] * n,
        out_shape=[jax.ShapeDtypeStruct((N_CHIPS,) + g.shape[2:], F32) for g in grads],
        scratch_shapes=[pltpu.SemaphoreType.DMA((n, N_CHIPS))] * 2,
        name="send_other_halves_" + tag)(*grads)


HBM = pl.BlockSpec(memory_space=pltpu.HBM)
SEM = pl.BlockSpec(memory_space=pltpu.SEMAPHORE)


def _block_copies(n, srcs, dsts, send_sems, recv_sems):
    x, y, c = _mesh_pos()
    return [pltpu.make_async_remote_copy(
        src_ref=srcs[a].at[2 * px + py], dst_ref=dsts[a].at[k], send_sem=send_sems.at[3 * a + k],
        recv_sem=recv_sems.at[3 * a + k], device_id=(px, py, c), device_id_type=MESH)
        for a in range(n) for k, (px, py) in enumerate(_other_chips(x, y))]


def _exchange_start(blocked, tag):
    n = len(blocked)
    lands = [lax.empty((3,) + b.shape[1:], b.dtype) for b in blocked]
    bufs = [pltpu.with_memory_space_constraint(b, pltpu.HBM) for b in list(blocked) + lands]
    nb = 2 * n

    def body(*refs):
        for cp in _block_copies(n, refs[:n], refs[n:nb], refs[nb], refs[nb + 1]):
            cp.start()
        refs[-1][...] = jnp.zeros_like(refs[-1])

    out = pl.pallas_call(
        body, name="exchange_start_" + tag,
        in_specs=[HBM] * nb,
        out_shape=[pltpu.SemaphoreType.DMA((3 * n,)), pltpu.SemaphoreType.DMA((3 * n,))]
        + [pltpu.HBM(b.shape, b.dtype) for b in bufs] + [jax.ShapeDtypeStruct((8, 128), F32)],
        out_specs=[SEM] * 2 + [HBM] * nb + [pl.BlockSpec(memory_space=pltpu.VMEM)],
        input_output_aliases={i: 2 + i for i in range(nb)},
        compiler_params=pltpu.CompilerParams(has_side_effects=pltpu.SideEffectType.DATAFLOW_SIDE_EFFECTING),
    )(*bufs)
    return (out[:2], out[2:2 + nb]), out[-1]


def _exchange_wait(state, after, tag):
    sems, bufs = state
    nb = len(bufs)
    n = nb // 2

    def body(*refs):
        for cp in _block_copies(n, refs[:n], refs[n:nb], refs[nb], refs[nb + 1]):
            cp.wait_send()
            cp.wait_recv()

    out = pl.pallas_call(
        body, name="exchange_wait_" + tag,
        in_specs=[HBM] * nb + [SEM] * 2 + [ANY],
        out_shape=[pltpu.HBM(b.shape, b.dtype) for b in bufs],
        out_specs=[HBM] * nb,
        input_output_aliases={i: i for i in range(nb)},
        compiler_params=pltpu.CompilerParams(has_side_effects=pltpu.SideEffectType.DATAFLOW_SIDE_EFFECTING),
    )(*bufs, *sems, after)
    return out[n:nb]


def _exchange_small(small):
    def body(small_src, small_dst, ssend_sems, srecv_sems, local_sem):
        x, y, c = _mesh_pos()
        my_idx = 4 * x + 2 * y + c
        local = pltpu.make_async_copy(small_src, small_dst.at[my_idx], local_sem)
        local.start()
        others = []
        for r in range(1, 8):
            px = 1 - x if r & 4 else x
            py = 1 - y if r & 2 else y
            pc = 1 - c if r & 1 else c
            others.append((px, py, pc))
        for r, peer in enumerate(others):
            pltpu.make_async_remote_copy(
                src_ref=small_src, dst_ref=small_dst.at[my_idx], send_sem=ssend_sems.at[r],
                recv_sem=srecv_sems.at[r], device_id=peer, device_id_type=MESH).start()
        for r, (px, py, pc) in enumerate(others):
            pltpu.make_async_remote_copy(
                src_ref=small_src, dst_ref=small_dst.at[4 * px + 2 * py + pc], send_sem=ssend_sems.at[r],
                recv_sem=srecv_sems.at[r], device_id=(px, py, pc), device_id_type=MESH).wait()
        local.wait()

    return pl.pallas_call(
        body, in_specs=[ANY], out_specs=ANY, out_shape=jax.ShapeDtypeStruct((8,) + small.shape, F32),
        scratch_shapes=[pltpu.SemaphoreType.DMA((7,)), pltpu.SemaphoreType.DMA((7,)), pltpu.SemaphoreType.DMA],
        name="exchange_small")(small)


def _join_halves(bufs):
    n = len(bufs)

    def body(*refs):
        joined = refs[n:2 * n]
        send_sems, recv_sems = refs[2 * n:]
        x, y, c = _mesh_pos()
        for a in range(n):
            pltpu.make_async_remote_copy(
                src_ref=joined[a].at[c], dst_ref=joined[a].at[c], send_sem=send_sems.at[a],
                recv_sem=recv_sems.at[a], device_id=(x, y, 1 - c), device_id_type=MESH).start()
        for a in range(n):
            pltpu.make_async_remote_copy(
                src_ref=joined[a].at[c], dst_ref=joined[a].at[1 - c], send_sem=send_sems.at[a],
                recv_sem=recv_sems.at[a], device_id=(x, y, 1 - c), device_id_type=MESH).wait()

    return pl.pallas_call(
        body, in_specs=[ANY] * n, out_specs=[ANY] * n,
        out_shape=[jax.ShapeDtypeStruct(b.shape, b.dtype) for b in bufs],
        input_output_aliases={a: a for a in range(n)},
        scratch_shapes=[pltpu.SemaphoreType.DMA((n,))] * 2,
        name="join_halves")(*bufs)


def kernel(x, meta_tokens, norm_g, w_in, conv_w, conv_b, ln_g, ln_b, w_conv_out, lb_logits, gnorm_g, w_rec_out, w_out, final_g, loss_target, m_meta_tokens, m_norm_g, m_w_in, m_conv_w, m_conv_b, m_ln_g, m_ln_b, m_w_conv_out, m_lb_logits, m_gnorm_g, m_w_rec_out, m_w_out, m_final_g, v_meta_tokens, v_norm_g, v_w_in, v_conv_w, v_conv_b, v_ln_g, v_ln_b, v_w_conv_out, v_lb_logits, v_gnorm_g, v_w_rec_out, v_w_out, v_final_g):
    seq, d = x.shape[1], x.shape[2]
    n_meta = meta_tokens.shape[0]
    n_pad = CHUNK - n_meta
    ds = d // N_CHIPS
    chip = 2 * lax.axis_index("x") + lax.axis_index("y")

    conv_w_pad = jnp.pad(conv_w[0], ((0, HALO - CONV_WIDTH), (0, 0)))
    chip_idx = chip.astype(jnp.int32).reshape(1)
    win_g, sq_g, small_g = _gather_weights([
        _place_shard("place_w_in", w_in[0], chip_idx, BF16),
        _place_shard("place_square", jnp.concatenate([w_conv_out[0], w_rec_out[0], w_out[0]], axis=0),
                     chip_idx, BF16),
        _place_shard("place_small", jnp.concatenate([conv_w_pad, meta_tokens], axis=0), chip_idx, F32)])
    wc_full = sq_g[:, 0:ds].reshape(d, d)
    wr_full = sq_g[:, ds:2 * ds].reshape(d, d)
    wo_full = sq_g[:, 2 * ds:3 * ds].reshape(d, d)
    cw_full = jnp.transpose(small_g[:, 0:HALO], (1, 0, 2)).reshape(HALO, d)
    meta_full = jnp.transpose(small_g[:, HALO:HALO + n_meta], (1, 0, 2)).reshape(n_meta, d)

    hres = jnp.concatenate([jnp.zeros((n_pad, d), F32), meta_full, x[0]], axis=0)
    target = jnp.pad(loss_target[0], ((CHUNK, 0), (0, 0)))
    final_g2 = final_g.reshape(1, d)
    h = _rmsnorm_fwd(hres, norm_g)
    proj = _in_proj(h, win_g)
    c, yc_in, y_conv = _conv_fwd(proj, cw_full, conv_b, ln_g, ln_b, wc_full)
    o, s_all = _hgrn_fwd(proj, lb_logits, n_pad)
    yr_in, merged, y_rec, dout, loss_acc, dfinal_g = _tail_fwd(
        o, proj, y_conv, hres, target, gnorm_g, final_g2, wr_full, wo_full)

    (dyc, dyr, dout_bf, dz, dproj, do, dc, dgnorm_g, dln_g, dln_b) = _tail_bwd(
        dout, proj, y_conv, y_rec, o, c, wo_full, wr_full, wc_full, ln_g, ln_b, gnorm_g)
    g_wc = _weight_grad(yc_in, dyc, "grad_w_conv_out", False)
    g_wr = _weight_grad(yr_in, dyr, "grad_w_rec_out", False)
    g_wo = _weight_grad(merged, dout_bf, "grad_w_out", False)

    core = lax.axis_index("c").astype(jnp.int32).reshape(1)

    def chip_sum_and_start(g, tag):
        g = g.reshape(N_CHIPS, 2, g.shape[1] // 2, g.shape[2])
        (from_sibling,) = _send_other_halves([g], tag)
        sums = _chip_half_sum("chip_half_sum_" + tag, g, from_sibling, core)
        in_flight, token = _exchange_start([sums[1]], tag)
        return sums[0], in_flight, token[0:1, 0:1]

    g_sq = jnp.concatenate([g.reshape(N_CHIPS, ds, d) for g in (g_wc, g_wr, g_wo)], axis=1)
    sum_sq, flight_sq, token_sq = chip_sum_and_start(g_sq, "square")
    dproj, dlb_logits = _hgrn_bwd(proj, do, s_all, lb_logits + token_sq, n_pad, dproj)
    dproj, dconv_w, dconv_b = _conv_bwd(dc, proj, cw_full, dz, dproj)
    (recv_sq,) = _exchange_wait(flight_sq, dconv_b, "square")
    g_win = _weight_grad(h, dproj, "grad_w_in", True)
    sum_win, flight_win, token_win = chip_sum_and_start(g_win, "w_in")
    dhres, dnorm_g = _in_proj_bwd(dproj, jnp.transpose(win_g, (0, 2, 1)), hres, norm_g + token_win, dout)
    grad_x = dhres[CHUNK:][None]
    (recv_win,) = _exchange_wait(flight_win, dnorm_g, "w_in")
    small = jnp.concatenate([dnorm_g, dconv_b, dln_g, dln_b, dlb_logits, dgnorm_g, dfinal_g,
                             dhres[n_pad:CHUNK], dconv_w[:CONV_WIDTH],
                             jnp.zeros((1, d), F32)], axis=0)
    small_slots = _exchange_small(small)
    chip_core = jnp.concatenate([chip_idx, core])
    totals = [_block_half_total("block_half_total_" + nm, s, r, chip_core)
              for nm, s, r in zip(("w_in", "square"), (sum_win, sum_sq), (recv_win, recv_sq))]
    gt_win, gt_sq = [t.reshape(2 * t.shape[1], t.shape[2]) for t in _join_halves(totals)]
    small_sum = _sum_slots("sum_small", small_slots)

    res = {}
    res["w_in"] = _adamw("adamw_w_in", w_in[0], m_w_in[0], v_w_in[0], gt_win)
    res["w_conv_out"] = _adamw("adamw_w_conv_out", w_conv_out[0], m_w_conv_out[0], v_w_conv_out[0], gt_sq[0:ds])
    res["w_rec_out"] = _adamw("adamw_w_rec_out", w_rec_out[0], m_w_rec_out[0], v_w_rec_out[0], gt_sq[ds:2 * ds])
    res["w_out"] = _adamw("adamw_w_out", w_out[0], m_w_out[0], v_w_out[0], gt_sq[2 * ds:3 * ds])
    big = {k: tuple(a[None] for a in v) for k, v in res.items()}

    rep_names = ("norm_g", "conv_b", "ln_g", "ln_b", "lb_logits", "gnorm_g", "final_g")
    rep_w = (norm_g, conv_b, ln_g, ln_b, lb_logits, gnorm_g, final_g2)
    rep_m = (m_norm_g, m_conv_b, m_ln_g, m_ln_b, m_lb_logits, m_gnorm_g, m_final_g.reshape(1, d))
    rep_v = (v_norm_g, v_conv_b, v_ln_g, v_ln_b, v_lb_logits, v_gnorm_g, v_final_g.reshape(1, d))
    rep = _adamw("adamw_replicated", jnp.concatenate(rep_w, 0), jnp.concatenate(rep_m, 0),
                 jnp.concatenate(rep_v, 0), small_sum[0:8])
    rep_rows = {"norm_g": (0, 1), "conv_b": (1, 2), "ln_g": (2, 3), "ln_b": (3, 4), "lb_logits": (4, 6),
                "gnorm_g": (6, 7), "final_g": (7, 8)}
    small_out = {}
    for nm in rep_names:
        lo, hi = rep_rows[nm]
        vals = tuple(a[lo:hi] for a in rep)
        if nm == "final_g":
            vals = tuple(a.reshape(d) for a in vals)
        small_out[nm] = vals
    cw_row = 8 + n_meta
    g_meta = lax.dynamic_slice_in_dim(small_sum[8:cw_row], chip * ds, ds, axis=1)
    small_out["meta_tokens"] = _adamw("adamw_meta", meta_tokens, m_meta_tokens, v_meta_tokens, g_meta)
    g_cw = lax.dynamic_slice_in_dim(small_sum[cw_row:cw_row + HALO], chip * ds, ds, axis=1)
    pad_rows = ((0, HALO - CONV_WIDTH), (0, 0))
    cw_res = _adamw("adamw_conv_w", conv_w_pad, jnp.pad(m_conv_w[0], pad_rows),
                    jnp.pad(v_conv_w[0], pad_rows, constant_values=1.0), g_cw)
    small_out["conv_w"] = tuple(a[:CONV_WIDTH][None] for a in cw_res)

    loss = lax.psum(loss_acc[0, 0], ("x", "y", "c"))

    order = ("meta_tokens", "norm_g", "w_in", "conv_w", "conv_b", "ln_g", "ln_b", "w_conv_out", "lb_logits",
             "gnorm_g", "w_rec_out", "w_out", "final_g")
    allres = {**big, **small_out}
    outs = [loss, grad_x]
    for field in range(4):
        outs.extend(allres[nm][field] for nm in order)
    return tuple(outs)
```

```python
import numpy as np

import jax
import jax.numpy as jnp
from jax import lax
from jax.experimental import pallas as pl
from jax.experimental.pallas import tpu as pltpu

F32 = jnp.float32
BF16 = jnp.bfloat16

EPS = 1e-6
CHUNK = 64
N_LEVELS = 6
CONV_WIDTH = 31
HALO = 32
CONV_ROWS = 32
CONV_LANES = 256
HEAD = 128
W_IN_COL_TILES = 1
HEADS_PER_TRIP = 8
N_CHIPS = 4
VMEM_LIMIT_BYTES = 56 * 1024 * 1024

ADAM_LR = 0.001
ADAM_B1 = 0.9
ADAM_B2 = 0.999
ADAM_EPS = 1e-08
ADAM_WD = 0.01
ADAM_STEP = 10

MESH = pl.DeviceIdType.MESH
ANY = pl.BlockSpec(memory_space=pl.ANY)

NT = (((1,), (1,)), ((), ()))
TN = (((0,), (0,)), ((), ()))


def _params(**kw):
    return pltpu.CompilerParams(vmem_limit_bytes=VMEM_LIMIT_BYTES, **kw)


def _sigmoid(x):
    return jax.nn.sigmoid(x)


def _dsilu(x, s):
    return s * (1.0 + x * (1.0 - s))


def _row_tile(lp):
    for t in (320, 256, 192, 128, 64):
        if lp % t == 0:
            return t
    raise ValueError(f"unsupported padded length {lp}")


def _mm_row_tile(lp):
    for t in (832, 640, 320, 256, 192, 128, 64):
        if lp % t == 0:
            return t
    raise ValueError(f"unsupported padded length {lp}")


def _dot3(m_bf16, x):
    hi = x.astype(BF16)
    r1 = x - hi.astype(F32)
    mid = r1.astype(BF16)
    lo = (r1 - mid.astype(F32)).astype(BF16)
    return (jnp.dot(m_bf16, hi, preferred_element_type=F32)
            + jnp.dot(m_bf16, mid, preferred_element_type=F32)
            + jnp.dot(m_bf16, lo, preferred_element_type=F32))


def _dot2(m_bf16, x):
    hi = x.astype(BF16)
    lo = (x - hi.astype(F32)).astype(BF16)
    return (jnp.dot(m_bf16, hi, preferred_element_type=F32)
            + jnp.dot(m_bf16, lo, preferred_element_type=F32))


def _col_to_row(col):
    return jnp.broadcast_to(col, (HEAD, 8)).T[0:1, :]


def _row_to_col(row):
    return jnp.broadcast_to(row, (8, HEAD)).T[:, 0:1]


def _hgrn_tables():
    t = np.arange(CHUNK)
    ltri = (t[None, :] <= t[:, None]).astype(np.float32)
    mats = [ltri]
    for lvl in range(1, N_LEVELS + 1):
        blk = CHUNK >> (lvl - 1)
        mid = (t // blk) * blk + blk // 2
        mats.append(ltri[mid - 1])
    after = (t[None, :] >= t[:, None]).astype(np.float32)
    before = (t[None, :] < t[:, None]).astype(np.float32)
    return jnp.asarray(np.concatenate(mats, 0), BF16), jnp.asarray(np.concatenate([after, before], 1), BF16)


def _rmsnorm_fwd(hres, g):
    lp, d = hres.shape
    tm = _row_tile(lp)

    def body(x_ref, g_ref, h_ref):
        x = x_ref[...]
        r = lax.rsqrt(jnp.mean(x * x, axis=-1, keepdims=True) + EPS)
        h_ref[...] = (x * r * g_ref[...]).astype(BF16)

    return pl.pallas_call(
        body, grid=(lp // tm,),
        in_specs=[pl.BlockSpec((tm, d), lambda i: (i, 0)), pl.BlockSpec((1, d), lambda i: (0, 0))],
        out_specs=pl.BlockSpec((tm, d), lambda i: (i, 0)),
        out_shape=jax.ShapeDtypeStruct((lp, d), BF16),
        name="rmsnorm_fwd", compiler_params=_params())(hres, g)


def _in_proj(h, wg):
    lp, d = h.shape
    _, _, ncol = wg.shape
    tm = _mm_row_tile(lp)
    nt = W_IN_COL_TILES
    tn = ncol // nt

    def body(h_ref, w_ref, o_ref):
        o_ref[...] = jnp.dot(h_ref[...], w_ref[0], preferred_element_type=F32)

    return pl.pallas_call(
        body, grid=(N_CHIPS, nt, lp // tm),
        in_specs=[pl.BlockSpec((tm, d), lambda j, n, i: (i, 0)),
                  pl.BlockSpec((1, d, tn), lambda j, n, i: (j, 0, n))],
        out_specs=pl.BlockSpec((tm, tn), lambda j, n, i: (i, j * nt + n)),
        out_shape=jax.ShapeDtypeStruct((lp, N_CHIPS * ncol), F32),
        name="in_proj", compiler_params=_params())(h, wg)


def _conv_fwd(proj, conv_w, conv_b, ln_g, ln_b, w_conv):
    lp = proj.shape[0]
    d = conv_b.shape[1]
    tm = _row_tile(lp)
    hb = tm // HALO

    def body(ua_ref, ub_ref, z_ref, uap_ref, ubp_ref, cw_ref, cb_ref, lg_ref, lb_ref, w_ref,
             c_ref, ycin_ref, yconv_ref, aext_ref):
        i = pl.program_id(0)
        a_prev = uap_ref[...] * _sigmoid(ubp_ref[...])
        aext_ref[0:HALO, :] = jnp.where(i > 0, a_prev, 0.0)
        aext_ref[HALO:HALO + tm, :] = ua_ref[...] * _sigmoid(ub_ref[...])

        def row_block(r, carry):
            r0 = pl.multiple_of(r * CONV_ROWS, CONV_ROWS)
            for cs in range(d // CONV_LANES):
                cl = slice(cs * CONV_LANES, (cs + 1) * CONV_LANES)
                blk = aext_ref[pl.ds(r0, CONV_ROWS + HALO), cl]
                acc = jnp.zeros((CONV_ROWS, CONV_LANES), F32) + cb_ref[:, cl]
                for b in range(8):
                    sh = blk if b == 0 else pltpu.roll(blk, CONV_ROWS + HALO - b, axis=0)
                    for a in range(5):
                        j = 8 * a + b - 2
                        if 0 <= j < CONV_WIDTH:
                            acc = acc + cw_ref[j:j + 1, cl] * sh[8 * a:8 * a + CONV_ROWS, :]
                c_ref[pl.ds(r0, CONV_ROWS), cl] = acc
            return carry

        lax.fori_loop(0, tm // CONV_ROWS, row_block, 0)

        c = c_ref[...]
        mu = jnp.mean(c, axis=-1, keepdims=True)
        xc = c - mu
        rstd = lax.rsqrt(jnp.mean(xc * xc, axis=-1, keepdims=True) + EPS)
        ln = xc * rstd * lg_ref[...] + lb_ref[...]
        s = ln * _sigmoid(ln)
        z = z_ref[...]
        ycin = (s * (z * _sigmoid(z))).astype(BF16)
        ycin_ref[...] = ycin
        yconv_ref[...] = jnp.dot(ycin, w_ref[...], preferred_element_type=F32)

    row = lambda p: pl.BlockSpec((tm, d), lambda i, p=p: (i, p))
    halo = lambda p: pl.BlockSpec((HALO, d), lambda i, p=p: (jnp.maximum(i * hb - 1, 0), p))
    vec = pl.BlockSpec((1, d), lambda i: (0, 0))
    return pl.pallas_call(
        body, grid=(lp // tm,),
        in_specs=[row(0), row(1), row(2), halo(0), halo(1),
                  pl.BlockSpec((HALO, d), lambda i: (0, 0)), vec, vec, vec,
                  pl.BlockSpec((d, d), lambda i: (0, 0))],
        out_specs=[pl.BlockSpec((tm, d), lambda i: (i, 0))] * 3,
        out_shape=[jax.ShapeDtypeStruct((lp, d), F32), jax.ShapeDtypeStruct((lp, d), BF16),
                   jax.ShapeDtypeStruct((lp, d), F32)],
        scratch_shapes=[pltpu.VMEM((HALO + tm, d), F32)],
        name="conv_fwd", compiler_params=_params())(
            proj, proj, proj, proj, proj, conv_w, conv_b, ln_g, ln_b, w_conv)


def _lower_bound(lbl_ref):
    l0 = lbl_ref[0:1, :]
    l1 = lbl_ref[1:2, :]
    m = jnp.maximum(l0, l1)
    e0 = jnp.exp(l0 - m)
    e1 = jnp.exp(l1 - m)
    p0 = e0 / (e0 + e1)
    return p0, p0 * (e1 / (e0 + e1))


def _level_masks():
    rid = lax.broadcasted_iota(jnp.int32, (CHUNK, 1), 0)
    r2 = lax.broadcasted_iota(jnp.int32, (CHUNK, CHUNK), 0)
    c2 = lax.broadcasted_iota(jnp.int32, (CHUNK, CHUNK), 1)
    out = []
    for lvl in range(1, N_LEVELS + 1):
        blk = CHUNK >> (lvl - 1)
        sh = blk.bit_length() - 1
        upper = (rid & (blk - 1)) >= (blk // 2)
        same = (r2 >> sh) == (c2 >> sh)
        out.append((upper, same))
    return out


def _gates(qr, fr, lb, valid):
    sq = _sigmoid(qr)
    q = qr * sq
    sf = _sigmoid(fr)
    f = lb + (1.0 - lb) * sf
    g = jnp.where(valid, jnp.log(f), 0.0)
    k = jnp.where(valid, 1.0 - f, 0.0)
    return q, sq, f, sf, g, k


def _level_factors(b, r, upper):
    e = jnp.exp(jnp.where(upper, b - r, r - b))
    return jnp.where(upper, e, 0.0), jnp.where(upper, 0.0, e)


def _hgrn_fwd(proj, lb_logits, n_pad):
    lp = proj.shape[0]
    d = lb_logits.shape[1]
    n_heads = d // HEAD
    nc = lp // CHUNK
    tab, _ = _hgrn_tables()
    n_tab = tab.shape[0]

    def body(qr_ref, fr_ref, ir_ref, lbl_ref, tab_ref, o_ref, sall_ref, s_ref, t_ref):
        n = pl.program_id(0)

        @pl.when(n == 0)
        def _():
            s_ref[...] = jnp.zeros_like(s_ref)

        sall_ref[0] = s_ref[...]
        lb_all, _ = _lower_bound(lbl_ref)
        rid = lax.broadcasted_iota(jnp.int32, (CHUNK, 1), 0)
        valid = jnp.logical_or(n > 0, rid >= n_pad)
        f_all = lb_all + (1.0 - lb_all) * _sigmoid(fr_ref[...])
        t_ref[...] = _dot2(tab_ref[...], jnp.where(valid, jnp.log(f_all), 0.0))
        masks = _level_masks()

        def head(h):
            off = h * HEAD if isinstance(h, int) else pl.multiple_of(h * HEAD, HEAD)
            hs = pl.ds(off, HEAD)
            lb = _lower_bound_slice(lbl_ref, hs)
            q, _, _, _, _, k = _gates(qr_ref[:, hs], fr_ref[:, hs], lb, valid)
            v = ir_ref[:, hs]
            b = t_ref[0:CHUNK, hs]
            s0 = s_ref[hs, :]
            o = jnp.dot((q * jnp.exp(b)).astype(BF16), s0.astype(BF16), preferred_element_type=F32)
            o = o + jnp.sum(q * k, axis=-1, keepdims=True) * v
            a = jnp.zeros((CHUNK, CHUNK), F32)
            for lvl in range(1, N_LEVELS + 1):
                upper, same = masks[lvl - 1]
                eq, ek = _level_factors(b, t_ref[CHUNK * lvl:CHUNK * (lvl + 1), hs], upper)
                p = lax.dot_general((q * eq).astype(BF16), (k * ek).astype(BF16), NT, preferred_element_type=F32)
                a = a + jnp.where(same, p, 0.0)
            vb = v.astype(BF16)
            o_ref[:, hs] = o + jnp.dot(a.astype(BF16), vb, preferred_element_type=F32)
            b_last = t_ref[CHUNK - 1:CHUNK, hs]
            khat = (k * jnp.exp(b_last - b)).astype(BF16)
            s_ref[hs, :] = _row_to_col(jnp.exp(b_last)) * s0 + lax.dot_general(khat, vb, TN, preferred_element_type=F32)
        per_trip = min(HEADS_PER_TRIP, n_heads)

        def head_group(p, carry):
            for u in range(per_trip):
                head(p * per_trip + u)
            return carry

        if n_heads == per_trip:
            head_group(0, 0)
        else:
            lax.fori_loop(0, n_heads // per_trip, head_group, 0)

    piece = lambda p: pl.BlockSpec((CHUNK, d), lambda n, p=p: (n, p))
    return pl.pallas_call(
        body, grid=(nc,),
        in_specs=[piece(3), piece(4), piece(5), pl.BlockSpec((2, d), lambda n: (0, 0)),
                  pl.BlockSpec((n_tab, CHUNK), lambda n: (0, 0))],
        out_specs=[pl.BlockSpec((CHUNK, d), lambda n: (n, 0)), pl.BlockSpec((1, d, HEAD), lambda n: (n, 0, 0))],
        out_shape=[jax.ShapeDtypeStruct((lp, d), F32), jax.ShapeDtypeStruct((nc, d, HEAD), F32)],
        scratch_shapes=[pltpu.VMEM((d, HEAD), F32), pltpu.VMEM((n_tab, d), F32)],
        name="hgrn_fwd", compiler_params=_params())(proj, proj, proj, lb_logits, tab)


def _lower_bound_slice(lbl_ref, hs):
    l0 = lbl_ref[0:1, hs]
    l1 = lbl_ref[1:2, hs]
    m = jnp.maximum(l0, l1)
    e0 = jnp.exp(l0 - m)
    e1 = jnp.exp(l1 - m)
    return e0 / (e0 + e1)


def _tail_fwd(o, proj, y_conv, hres, target, gnorm_g, final_g, w_rec, w_out):
    lp, d = o.shape
    n_heads = d // HEAD
    tm = _row_tile(lp)

    def body(o_ref, gr_ref, mc_ref, mr_ref, yc_ref, x_ref, t_ref, gn_ref, fg_ref, wr_ref, wo_ref,
             yrin_ref, mg_ref, yrec_ref, dout_ref, loss_ref, dfg_ref):
        i = pl.program_id(0)

        @pl.when(i == 0)
        def _():
            loss_ref[...] = jnp.zeros_like(loss_ref)
            dfg_ref[...] = jnp.zeros_like(dfg_ref)

        for h in range(n_heads):
            hs = slice(h * HEAD, (h + 1) * HEAD)
            oh = o_ref[:, hs]
            on = oh * lax.rsqrt(jnp.mean(oh * oh, axis=-1, keepdims=True) + EPS) * gn_ref[:, hs]
            gr = gr_ref[:, hs]
            yrin_ref[:, hs] = (on * (gr * _sigmoid(gr))).astype(BF16)
        yrec = jnp.dot(yrin_ref[...], wr_ref[...], preferred_element_type=F32)
        yrec_ref[...] = yrec
        merged = (_sigmoid(mc_ref[...]) * yc_ref[...] + _sigmoid(mr_ref[...]) * yrec).astype(BF16)
        mg_ref[...] = merged
        out = x_ref[...] + jnp.dot(merged, wo_ref[...], preferred_element_type=F32)
        r = lax.rsqrt(jnp.mean(out * out, axis=-1, keepdims=True) + EPS)
        yhat = out * r
        fg = fg_ref[...]
        rid = lax.broadcasted_iota(jnp.int32, (tm, 1), 0) + i * tm
        err = jnp.where(rid >= CHUNK, yhat * fg - t_ref[...], 0.0)
        loss_ref[...] += 0.5 * jnp.sum(err * err) / d
        dy = err / d
        dfg_ref[...] += jnp.sum(dy * yhat, axis=0, keepdims=True)
        dyh = dy * fg
        dout_ref[...] = r * (dyh - yhat * jnp.mean(dyh * yhat, axis=-1, keepdims=True))

    row = lambda p: pl.BlockSpec((tm, d), lambda i, p=p: (i, p))
    vec = pl.BlockSpec((1, d), lambda i: (0, 0))
    mat = pl.BlockSpec((d, d), lambda i: (0, 0))
    return pl.pallas_call(
        body, grid=(lp // tm,),
        in_specs=[row(0), row(6), row(7), row(8), row(0), row(0), row(0), vec, vec, mat, mat],
        out_specs=[row(0), row(0), row(0), row(0), pl.BlockSpec((8, 128), lambda i: (0, 0)), vec],
        out_shape=[jax.ShapeDtypeStruct((lp, d), BF16), jax.ShapeDtypeStruct((lp, d), BF16),
                   jax.ShapeDtypeStruct((lp, d), F32), jax.ShapeDtypeStruct((lp, d), F32),
                   jax.ShapeDtypeStruct((8, 128), F32), jax.ShapeDtypeStruct((1, d), F32)],
        name="tail_fwd", compiler_params=_params())(
            o, proj, proj, proj, y_conv, hres, target, gnorm_g, final_g, w_rec, w_out)


def _tail_bwd(dout, proj, y_conv, y_rec, o, c, w_out, w_rec, w_conv, ln_g, ln_b, gnorm_g):
    lp, d = dout.shape
    n_heads = d // HEAD
    tm = _row_tile(lp)

    def body(dout_ref, mc_ref, mr_ref, z_ref, gr_ref, yc_ref, yrec_ref, o_ref, c_ref,
             wo_ref, wr_ref, wc_ref, lg_ref, lb_ref, gn_ref,
             dyc_ref, dyr_ref, doutb_ref, dz_ref, dp_ref, do_ref, dc_ref,
             dgn_ref, dlg_ref, dlb_ref, dyrin_ref):
        i = pl.program_id(0)

        @pl.when(i == 0)
        def _():
            dgn_ref[...] = jnp.zeros_like(dgn_ref)
            dlg_ref[...] = jnp.zeros_like(dlg_ref)
            dlb_ref[...] = jnp.zeros_like(dlb_ref)

        doutb = dout_ref[...].astype(BF16)
        doutb_ref[...] = doutb
        dmerged = lax.dot_general(doutb, wo_ref[...], NT, preferred_element_type=F32)
        smc = _sigmoid(mc_ref[...])
        smr = _sigmoid(mr_ref[...])
        dyc = (dmerged * smc).astype(BF16)
        dyr = (dmerged * smr).astype(BF16)
        dyc_ref[...] = dyc
        dyr_ref[...] = dyr
        dp_ref[:, d:2 * d] = (dmerged * yc_ref[...] * smc * (1.0 - smc)).astype(BF16)
        dp_ref[:, 2 * d:3 * d] = (dmerged * yrec_ref[...] * smr * (1.0 - smr)).astype(BF16)

        dyrin_ref[...] = lax.dot_general(dyr, wr_ref[...], NT, preferred_element_type=F32)
        for h in range(n_heads):
            hs = slice(h * HEAD, (h + 1) * HEAD)
            oh = o_ref[:, hs]
            rstd = lax.rsqrt(jnp.mean(oh * oh, axis=-1, keepdims=True) + EPS)
            ohat = oh * rstd
            gn = gn_ref[:, hs]
            gr = gr_ref[:, hs]
            sg = _sigmoid(gr)
            dyrin = dyrin_ref[:, hs]
            don = dyrin * (gr * sg)
            dp_ref[:, hs] = (dyrin * (ohat * gn) * _dsilu(gr, sg)).astype(BF16)
            dgn_ref[:, hs] += jnp.sum(don * ohat, axis=0, keepdims=True)
            doh = don * gn
            do_ref[:, hs] = rstd * (doh - ohat * jnp.mean(doh * ohat, axis=-1, keepdims=True))

        dycin = lax.dot_general(dyc, wc_ref[...], NT, preferred_element_type=F32)
        c = c_ref[...]
        mu = jnp.mean(c, axis=-1, keepdims=True)
        xc = c - mu
        rstd = lax.rsqrt(jnp.mean(xc * xc, axis=-1, keepdims=True) + EPS)
        nrm = xc * rstd
        lg = lg_ref[...]
        ln = nrm * lg + lb_ref[...]
        sl = _sigmoid(ln)
        z = z_ref[...]
        sz = _sigmoid(z)
        dz_ref[...] = (dycin * (ln * sl) * _dsilu(z, sz)).astype(BF16)
        dln = dycin * (z * sz) * _dsilu(ln, sl)
        dlg_ref[...] += jnp.sum(dln * nrm, axis=0, keepdims=True)
        dlb_ref[...] += jnp.sum(dln, axis=0, keepdims=True)
        dn = dln * lg
        dc_ref[...] = rstd * (dn - jnp.mean(dn, axis=-1, keepdims=True)
                              - nrm * jnp.mean(dn * nrm, axis=-1, keepdims=True))

    row = lambda p: pl.BlockSpec((tm, d), lambda i, p=p: (i, p))
    vec = pl.BlockSpec((1, d), lambda i: (0, 0))
    mat = pl.BlockSpec((d, d), lambda i: (0, 0))
    act_bf = jax.ShapeDtypeStruct((lp, d), BF16)
    act_f32 = jax.ShapeDtypeStruct((lp, d), F32)
    vec_f32 = jax.ShapeDtypeStruct((1, d), F32)
    return pl.pallas_call(
        body, grid=(lp // tm,),
        in_specs=[row(0), row(7), row(8), row(2), row(6), row(0), row(0), row(0), row(0),
                  mat, mat, mat, vec, vec, vec],
        out_specs=[row(0)] * 4 + [pl.BlockSpec((tm, 3 * d), lambda i: (i, 2))] + [row(0)] * 2 + [vec] * 3,
        out_shape=[act_bf] * 4 + [jax.ShapeDtypeStruct((lp, 9 * d), BF16)] + [act_f32] * 2 + [vec_f32] * 3,
        scratch_shapes=[pltpu.VMEM((tm, d), F32)],
        name="tail_bwd", compiler_params=_params())(
            dout, proj, proj, proj, proj, y_conv, y_rec, o, c, w_out, w_rec, w_conv, ln_g, ln_b, gnorm_g)


def _hgrn_bwd(proj, do, s_all, lb_logits, n_pad, dproj):
    lp, d = do.shape
    n_heads = d // HEAD
    nc = lp // CHUNK
    tab, utri = _hgrn_tables()
    n_tab = tab.shape[0]

    def body(qr_ref, fr_ref, ir_ref, do_ref, s0_ref, lbl_ref, tab_ref, ut_ref, _,
             dp_ref, dlbl_ref, ds_ref, t_ref, dlb_ref):
        n = pl.program_id(0)
        chunk = nc - 1 - n

        @pl.when(n == 0)
        def _():
            ds_ref[...] = jnp.zeros_like(ds_ref)
            dlb_ref[...] = jnp.zeros_like(dlb_ref)

        lb_all, pp = _lower_bound(lbl_ref)
        rid = lax.broadcasted_iota(jnp.int32, (CHUNK, 1), 0)
        valid = jnp.logical_or(chunk > 0, rid >= n_pad)
        f_all = lb_all + (1.0 - lb_all) * _sigmoid(fr_ref[...])
        t_ref[...] = _dot2(tab_ref[...], jnp.where(valid, jnp.log(f_all), 0.0))
        masks = _level_masks()
        ut = ut_ref[...]

        def head(h):
            off = h * HEAD if isinstance(h, int) else pl.multiple_of(h * HEAD, HEAD)
            hs = pl.ds(off, HEAD)
            lb = _lower_bound_slice(lbl_ref, hs)
            qr = qr_ref[:, hs]
            q, sq, f, sf, _, k = _gates(qr, fr_ref[:, hs], lb, valid)
            v = ir_ref[:, hs]
            do_h = do_ref[:, hs]
            b = t_ref[0:CHUNK, hs]
            b_last = t_ref[CHUNK - 1:CHUNK, hs]
            s0 = s0_ref[0, hs, :]
            ds1 = ds_ref[hs, :]
            eb = jnp.exp(b)
            ekl = jnp.exp(b_last - b)
            do_bf = do_h.astype(BF16)
            v_bf = v.astype(BF16)
            ds1_bf = ds1.astype(BF16)

            da = lax.dot_general(do_bf, v_bf, NT, preferred_element_type=F32)
            da_diag = jnp.sum(do_h * v, axis=-1, keepdims=True)
            a = jnp.zeros((CHUNK, CHUNK), F32)
            dq_x = eb * lax.dot_general(do_bf, s0.astype(BF16), NT, preferred_element_type=F32)
            dk_x = ekl * lax.dot_general(v_bf, ds1_bf, NT, preferred_element_type=F32)
            x_after = q * dq_x
            x_before = k * dk_x
            for lvl in range(1, N_LEVELS + 1):
                upper, same = masks[lvl - 1]
                eq, ek = _level_factors(b, t_ref[CHUNK * lvl:CHUNK * (lvl + 1), hs], upper)
                qt = (q * eq).astype(BF16)
                kt = (k * ek).astype(BF16)
                p = lax.dot_general(qt, kt, NT, preferred_element_type=F32)
                a = a + jnp.where(same, p, 0.0)
                dam = jnp.where(same, da, 0.0).astype(BF16)
                dqt = jnp.dot(dam, kt, preferred_element_type=F32)
                dkt = lax.dot_general(dam, qt, TN, preferred_element_type=F32)
                dq_x = dq_x + eq * dqt
                dk_x = dk_x + ek * dkt
                x_after = x_after + (qt.astype(F32) * dqt - kt.astype(F32) * dkt)

            dv = (lax.dot_general(a.astype(BF16), do_bf, TN, preferred_element_type=F32)
                  + jnp.sum(q * k, axis=-1, keepdims=True) * do_h
                  + jnp.dot((k * ekl).astype(BF16), ds1_bf, preferred_element_type=F32))
            dp_ref[:, pl.ds(2 * d + off, HEAD)] = dv.astype(BF16)

            carried = jnp.exp(b_last) * _col_to_row(jnp.sum(s0 * ds1, axis=-1, keepdims=True))
            dg = _dot3(ut, jnp.concatenate([x_after, x_before], axis=0)) + carried
            dq = dq_x + da_diag * k
            dk = dk_x + da_diag * q
            dp_ref[:, hs] = (dq * _dsilu(qr, sq)).astype(BF16)
            df = jnp.where(valid, dg / f - dk, 0.0)
            dp_ref[:, pl.ds(d + off, HEAD)] = (df * (1.0 - lb) * sf * (1.0 - sf)).astype(BF16)
            dlb_ref[:, hs] += jnp.sum(df * (1.0 - sf), axis=0, keepdims=True)

            ds_ref[hs, :] = (_row_to_col(jnp.exp(b_last)) * ds1
                             + lax.dot_general((q * eb).astype(BF16), do_bf, TN, preferred_element_type=F32))
        per_trip = min(HEADS_PER_TRIP, n_heads)

        def head_group(p, carry):
            for u in range(per_trip):
                head(p * per_trip + u)
            return carry

        if n_heads == per_trip:
            head_group(0, 0)
        else:
            lax.fori_loop(0, n_heads // per_trip, head_group, 0)

        @pl.when(n == nc - 1)
        def _():
            dl0 = dlb_ref[...] * pp
            dlbl_ref[0:1, :] = dl0
            dlbl_ref[1:2, :] = -dl0

    piece = lambda p: pl.BlockSpec((CHUNK, d), lambda n, p=p: (nc - 1 - n, p))
    return pl.pallas_call(
        body, grid=(nc,),
        in_specs=[piece(3), piece(4), piece(5), piece(0),
                  pl.BlockSpec((1, d, HEAD), lambda n: (nc - 1 - n, 0, 0)),
                  pl.BlockSpec((2, d), lambda n: (0, 0)),
                  pl.BlockSpec((n_tab, CHUNK), lambda n: (0, 0)),
                  pl.BlockSpec((CHUNK, 2 * CHUNK), lambda n: (0, 0)), ANY],
        out_specs=[pl.BlockSpec((CHUNK, 3 * d), lambda n: (nc - 1 - n, 1)), pl.BlockSpec((2, d), lambda n: (0, 0))],
        out_shape=[jax.ShapeDtypeStruct(dproj.shape, BF16), jax.ShapeDtypeStruct((2, d), F32)],
        input_output_aliases={8: 0},
        scratch_shapes=[pltpu.VMEM((d, HEAD), F32), pltpu.VMEM((n_tab, d), F32), pltpu.VMEM((1, d), F32)],
        name="hgrn_bwd", compiler_params=_params())(proj, proj, proj, do, s_all, lb_logits, tab, utri, dproj)


def _conv_bwd(dc, proj, conv_w, dz, dproj):
    lp, d = dc.shape
    tm = _row_tile(lp)
    hb = tm // HALO
    n_tiles = lp // tm
    last_halo = lp // HALO - 1

    def body(dc_ref, dcn_ref, ua_ref, ub_ref, uap_ref, ubp_ref, cw_ref, dz_ref, _,
             dp_ref, dcw_ref, dcb_ref, aext_ref, dcext_ref, da_ref):
        i = pl.program_id(0)

        @pl.when(i == 0)
        def _():
            dcw_ref[...] = jnp.zeros_like(dcw_ref)
            dcb_ref[...] = jnp.zeros_like(dcb_ref)

        ua = ua_ref[...]
        sb = _sigmoid(ub_ref[...])
        a_prev = uap_ref[...] * _sigmoid(ubp_ref[...])
        aext_ref[0:HALO, :] = jnp.where(i > 0, a_prev, 0.0)
        aext_ref[HALO:HALO + tm, :] = ua * sb
        dcext_ref[0:tm, :] = dc_ref[...]
        dcext_ref[tm:tm + HALO, :] = jnp.where(i < n_tiles - 1, dcn_ref[...], 0.0)
        dcb_ref[...] += jnp.sum(dc_ref[...], axis=0, keepdims=True)

        def row_block(r, carry):
            r0 = pl.multiple_of(r * CONV_ROWS, CONV_ROWS)
            n_rows = CONV_ROWS + HALO
            for cs in range(d // CONV_LANES):
                cl = slice(cs * CONV_LANES, (cs + 1) * CONV_LANES)
                dblk = dcext_ref[pl.ds(r0, n_rows), cl]
                ablk = aext_ref[pl.ds(r0, n_rows), cl]
                dcur = dblk[0:CONV_ROWS, :]
                acc = jnp.zeros((CONV_ROWS, CONV_LANES), F32)
                for b in range(8):
                    dsh = dblk if b == 0 else pltpu.roll(dblk, n_rows - b, axis=0)
                    ash = ablk if b == 0 else pltpu.roll(ablk, n_rows - b, axis=0)
                    for a in range(5):
                        j_da = CONV_WIDTH - 1 - (8 * a + b)
                        if 0 <= j_da < CONV_WIDTH:
                            acc = acc + cw_ref[j_da:j_da + 1, cl] * dsh[8 * a:8 * a + CONV_ROWS, :]
                        j_w = 8 * a + b - 2
                        if 0 <= j_w < CONV_WIDTH:
                            dcw_ref[j_w:j_w + 1, cl] += jnp.sum(
                                dcur * ash[8 * a:8 * a + CONV_ROWS, :], axis=0, keepdims=True)
                da_ref[pl.ds(r0, CONV_ROWS), cl] = acc
            return carry

        lax.fori_loop(0, tm // CONV_ROWS, row_block, 0)

        da = da_ref[...]
        dp_ref[:, 0:d] = (da * sb).astype(BF16)
        dp_ref[:, d:2 * d] = (da * ua * sb * (1.0 - sb)).astype(BF16)
        dp_ref[:, 2 * d:3 * d] = dz_ref[...]

    row = lambda p: pl.BlockSpec((tm, d), lambda i, p=p: (i, p))
    prev = lambda p: pl.BlockSpec((HALO, d), lambda i, p=p: (jnp.maximum(i * hb - 1, 0), p))
    nxt = pl.BlockSpec((HALO, d), lambda i: (jnp.minimum((i + 1) * hb, last_halo), 0))
    return pl.pallas_call(
        body, grid=(n_tiles,),
        in_specs=[row(0), nxt, row(0), row(1), prev(0), prev(1), pl.BlockSpec((HALO, d), lambda i: (0, 0)),
                  row(0), ANY],
        out_specs=[pl.BlockSpec((tm, 3 * d), lambda i: (i, 0)), pl.BlockSpec((HALO, d), lambda i: (0, 0)),
                   pl.BlockSpec((1, d), lambda i: (0, 0))],
        out_shape=[jax.ShapeDtypeStruct(dproj.shape, BF16),
                   jax.ShapeDtypeStruct((HALO, d), F32), jax.ShapeDtypeStruct((1, d), F32)],
        input_output_aliases={8: 0},
        scratch_shapes=[pltpu.VMEM((HALO + tm, d), F32), pltpu.VMEM((tm + HALO, d), F32), pltpu.VMEM((tm, d), F32)],
        name="conv_bwd", compiler_params=_params())(dc, dc, proj, proj, proj, proj, conv_w, dz, dproj)


def _weight_grad(xs, dy, name, blocked):
    lp, dx = xs.shape
    n = dy.shape[1]
    tk = _mm_row_tile(lp)
    if blocked:
        ncol = n // N_CHIPS
        nt = W_IN_COL_TILES
        tn = ncol // nt
        grid = (N_CHIPS * nt, lp // tk)
        out_spec = pl.BlockSpec((1, dx, tn), lambda c, k: (c // nt, 0, c % nt))
        out_shape = jax.ShapeDtypeStruct((N_CHIPS, dx, ncol), F32)
    else:
        tn = n // 2
        grid = (2, lp // tk)
        out_spec = pl.BlockSpec((dx, tn), lambda c, k: (0, c))
        out_shape = jax.ShapeDtypeStruct((dx, n), F32)

    def body(xs_ref, dy_ref, o_ref):
        @pl.when(pl.program_id(1) == 0)
        def _():
            o_ref[...] = jnp.zeros_like(o_ref)

        p = lax.dot_general(xs_ref[...], dy_ref[...], TN, preferred_element_type=F32)
        if blocked:
            o_ref[0] += p
        else:
            o_ref[...] += p

    return pl.pallas_call(
        body, grid=grid,
        in_specs=[pl.BlockSpec((tk, dx), lambda c, k: (k, 0)), pl.BlockSpec((tk, tn), lambda c, k: (k, c))],
        out_specs=out_spec, out_shape=out_shape,
        name=name, compiler_params=_params())(xs, dy)


def _in_proj_bwd(dproj, wtg, hres, norm_g, dout):
    lp, d = hres.shape
    _, ncol, _ = wtg.shape
    tm = _mm_row_tile(lp)
    nt = W_IN_COL_TILES
    tn = ncol // nt
    nk = N_CHIPS * nt

    def body(dp_ref, w_ref, x_ref, g_ref, dout_ref, dx_ref, dg_ref, acc_ref):
        i = pl.program_id(0)
        kk = pl.program_id(1)

        @pl.when(jnp.logical_and(i == 0, kk == 0))
        def _():
            dg_ref[...] = jnp.zeros_like(dg_ref)

        @pl.when(kk == 0)
        def _():
            acc_ref[...] = jnp.zeros_like(acc_ref)

        acc_ref[...] += jnp.dot(dp_ref[...], w_ref[0], preferred_element_type=F32)

        @pl.when(kk == nk - 1)
        def _():
            x = x_ref[...]
            r = lax.rsqrt(jnp.mean(x * x, axis=-1, keepdims=True) + EPS)
            xhat = x * r
            dh = acc_ref[...]
            dg_ref[...] += jnp.sum(dh * xhat, axis=0, keepdims=True)
            dxh = dh * g_ref[...]
            dx_ref[...] = dout_ref[...] + r * (dxh - xhat * jnp.mean(dxh * xhat, axis=-1, keepdims=True))

    return pl.pallas_call(
        body, grid=(lp // tm, nk),
        in_specs=[pl.BlockSpec((tm, tn), lambda i, k: (i, k)),
                  pl.BlockSpec((1, tn, d), lambda i, k: (k // nt, k % nt, 0)),
                  pl.BlockSpec((tm, d), lambda i, k: (i, 0)),
                  pl.BlockSpec((1, d), lambda i, k: (0, 0)),
                  pl.BlockSpec((tm, d), lambda i, k: (i, 0))],
        out_specs=[pl.BlockSpec((tm, d), lambda i, k: (i, 0)), pl.BlockSpec((1, d), lambda i, k: (0, 0))],
        out_shape=[jax.ShapeDtypeStruct((lp, d), F32), jax.ShapeDtypeStruct((1, d), F32)],
        scratch_shapes=[pltpu.VMEM((tm, d), F32)],
        name="in_proj_bwd", compiler_params=_params())(dproj, wtg, hres, norm_g, dout)


def _adamw_math(w, g, m, v):
    m = ADAM_B1 * m + (1.0 - ADAM_B1) * g
    v = ADAM_B2 * v + (1.0 - ADAM_B2) * (g * g)
    m_hat = m / (1.0 - ADAM_B1 ** ADAM_STEP)
    v_hat = v / (1.0 - ADAM_B2 ** ADAM_STEP)
    delta = -ADAM_LR * (m_hat / (jnp.sqrt(v_hat) + ADAM_EPS) + ADAM_WD * w)
    return delta, m, v


def _elementwise_rows(shape):
    r, c = shape
    for t in (256, 128, 64, 32, 16, 8):
        if r % t == 0 and r > t and t * c * 4 <= 1024 * 1024:
            return t
    return r


def _adamw(name, w, m, v, *g_parts):
    shape = w.shape
    tr = _elementwise_rows(shape)
    n_g = len(g_parts)

    def body(*refs):
        w_ref, m_ref, v_ref = refs[:3]
        g_refs = refs[3:3 + n_g]
        g_out, d_out, m_out, v_out = refs[3 + n_g:]
        g = g_refs[0][...]
        for gr in g_refs[1:]:
            g = g + gr[...]
        delta, m_new, v_new = _adamw_math(w_ref[...], g, m_ref[...], v_ref[...])
        g_out[...] = g
        d_out[...] = delta
        m_out[...] = m_new
        v_out[...] = v_new

    spec = pl.BlockSpec((tr, shape[1]), lambda i: (i, 0))
    return pl.pallas_call(
        body, grid=(shape[0] // tr,),
        in_specs=[spec] * (3 + n_g), out_specs=[spec] * 4,
        out_shape=[jax.ShapeDtypeStruct(shape, F32)] * 4,
        name=name, compiler_params=_params())(w, m, v, *g_parts)


def _chip_half_sum(name, g, recv, core):
    _, _, hr, cols = g.shape
    tr = _elementwise_rows((hr, cols))

    def body(core_ref, g_ref, r_ref, o_ref, ob_ref):
        s = g_ref[0, 0] + r_ref[0]
        o_ref[0] = s
        ob_ref[0] = s.astype(BF16)

    blk = pl.BlockSpec((1, tr, cols), lambda j, i, core_ref: (j, i, 0))
    grid_spec = pltpu.PrefetchScalarGridSpec(
        num_scalar_prefetch=1, grid=(N_CHIPS, hr // tr),
        in_specs=[pl.BlockSpec((1, 1, tr, cols), lambda j, i, core_ref: (j, core_ref[0], i, 0)), blk],
        out_specs=[blk, blk])
    return pl.pallas_call(
        body, grid_spec=grid_spec,
        out_shape=[jax.ShapeDtypeStruct((N_CHIPS, hr, cols), F32), jax.ShapeDtypeStruct((N_CHIPS, hr, cols), BF16)],
        name=name, compiler_params=_params())(core, g, recv)


def _block_half_total(name, chip_sums, recv, chip_core):
    _, hr, cols = chip_sums.shape
    tr = _elementwise_rows((hr, cols))

    def body(cc_ref, p_ref, r_ref, o_ref):
        s = p_ref[0]
        for k in range(3):
            s = s + r_ref[k].astype(F32)
        o_ref[0] = s

    grid_spec = pltpu.PrefetchScalarGridSpec(
        num_scalar_prefetch=1, grid=(hr // tr,),
        in_specs=[pl.BlockSpec((1, tr, cols), lambda i, cc_ref: (cc_ref[0], i, 0)),
                  pl.BlockSpec((3, tr, cols), lambda i, cc_ref: (0, i, 0))],
        out_specs=pl.BlockSpec((1, tr, cols), lambda i, cc_ref: (cc_ref[1], i, 0)))
    return pl.pallas_call(
        body, grid_spec=grid_spec, out_shape=jax.ShapeDtypeStruct((2, hr, cols), F32),
        name=name, compiler_params=_params())(chip_core, chip_sums, recv)


def _place_shard(name, w, chip, dtype):
    r, c = w.shape
    tr = _elementwise_rows((r, c))

    def body(chip_ref, w_ref, o_ref):
        o_ref[0] = w_ref[...].astype(dtype)

    grid_spec = pltpu.PrefetchScalarGridSpec(
        num_scalar_prefetch=1, grid=(r // tr,),
        in_specs=[pl.BlockSpec((tr, c), lambda i, chip_ref: (i, 0))],
        out_specs=pl.BlockSpec((1, tr, c), lambda i, chip_ref: (chip_ref[0], i, 0)))
    return pl.pallas_call(
        body, grid_spec=grid_spec, out_shape=jax.ShapeDtypeStruct((N_CHIPS, r, c), dtype),
        name=name, compiler_params=_params())(chip, w)


def _sum_slots(name, slots):
    k, r, c = slots.shape

    def body(s_ref, o_ref):
        s = s_ref[0]
        for j in range(1, k):
            s = s + s_ref[j]
        o_ref[...] = s

    return pl.pallas_call(body, out_shape=jax.ShapeDtypeStruct((r, c), F32), name=name,
                          compiler_params=_params())(slots)


def _mesh_pos():
    return lax.axis_index("x"), lax.axis_index("y"), lax.axis_index("c")


def _other_chips(x, y):
    return [(1 - x, y), (x, 1 - y), (1 - x, 1 - y)]


def _gather_weights(bufs):
    n = len(bufs)
    half = [b.shape[1] // 2 for b in bufs]

    def body(*refs):
        gathered = refs[n:2 * n]
        ici_send, ici_recv, d2d_send, d2d_recv = refs[2 * n:]
        x, y, c = _mesh_pos()
        me = 2 * x + y
        chips = _other_chips(x, y)

        def part(a, block, core):
            return gathered[a].at[block, pl.ds(core * half[a], half[a])]

        def over_ici(a, k, block):
            px, py = chips[k]
            return pltpu.make_async_remote_copy(
                src_ref=part(a, block, c), dst_ref=part(a, block, c),
                send_sem=ici_send.at[a, k], recv_sem=ici_recv.at[a, k],
                device_id=(px, py, c), device_id_type=MESH)

        def over_d2d(a, k, core):
            px, py = chips[k]
            return pltpu.make_async_remote_copy(
                src_ref=part(a, 2 * px + py, core), dst_ref=part(a, 2 * px + py, core),
                send_sem=d2d_send.at[a, k], recv_sem=d2d_recv.at[a, k],
                device_id=(x, y, 1 - c), device_id_type=MESH)

        for a in range(n):
            for k in range(3):
                over_ici(a, k, me).start()
        for a in range(n):
            for k, (px, py) in enumerate(chips):
                over_ici(a, k, 2 * px + py).wait_recv()
                over_d2d(a, k, c).start()
        for a in range(n):
            for k in range(3):
                over_d2d(a, k, 1 - c).wait_recv()
        for a in range(n):
            for k in range(3):
                over_ici(a, k, me).wait_send()
                over_d2d(a, k, c).wait_send()

    return pl.pallas_call(
        body, in_specs=[ANY] * n, out_specs=[ANY] * n,
        out_shape=[jax.ShapeDtypeStruct(b.shape, b.dtype) for b in bufs],
        input_output_aliases={a: a for a in range(n)},
        scratch_shapes=[pltpu.SemaphoreType.DMA((n, 3))] * 4,
        name="gather_weights")(*bufs)


def _gather_in_proj(h, bufs, order):
    n = len(bufs)
    half = [b.shape[1] // 2 for b in bufs]
    lp, d = h.shape
    ncol = bufs[0].shape[2]
    tm = _mm_row_tile(lp)
    n_row = lp // tm

    def body(order_ref, h_ref, *refs):
        gathered = refs[n:2 * n]
        o_ref = refs[2 * n]
        w_buf, ici_send, ici_recv, d2d_send, d2d_recv, w_sem = refs[2 * n + 1:]
        j = pl.program_id(0)
        i = pl.program_id(1)
        x, y, c = _mesh_pos()
        me = 2 * x + y
        chips = _other_chips(x, y)

        def part(a, block, core):
            return gathered[a].at[block, pl.ds(core * half[a], half[a])]

        def over_ici(a, k, block):
            px, py = chips[k]
            return pltpu.make_async_remote_copy(
                src_ref=part(a, block, c), dst_ref=part(a, block, c),
                send_sem=ici_send.at[a, k], recv_sem=ici_recv.at[a, k],
                device_id=(px, py, c), device_id_type=MESH)

        def over_d2d(a, k, core):
            px, py = chips[k]
            return pltpu.make_async_remote_copy(
                src_ref=part(a, 2 * px + py, core), dst_ref=part(a, 2 * px + py, core),
                send_sem=d2d_send.at[a, k], recv_sem=d2d_recv.at[a, k],
                device_id=(x, y, 1 - c), device_id_type=MESH)

        @pl.when(jnp.logical_and(j == 0, i == 0))
        def _():
            for a in range(n):
                for k in range(3):
                    over_ici(a, k, me).start()

        for k, (px, py) in enumerate(chips):
            @pl.when(jnp.logical_and(j == k + 1, i == 0))
            def _(k=k, px=px, py=py):
                for a in range(n):
                    over_ici(a, k, 2 * px + py).wait_recv()
                    over_d2d(a, k, c).start()
                for a in range(n):
                    over_d2d(a, k, 1 - c).wait_recv()

        @pl.when(i == 0)
        def _():
            load = pltpu.make_async_copy(gathered[0].at[order_ref[j]], w_buf, w_sem)
            load.start()
            load.wait()

        o_ref[...] = jnp.dot(h_ref[...], w_buf[...], preferred_element_type=F32)

        @pl.when(jnp.logical_and(j == N_CHIPS - 1, i == n_row - 1))
        def _():
            for a in range(n):
                for k in range(3):
                    over_ici(a, k, me).wait_send()
                    over_d2d(a, k, c).wait_send()

    grid_spec = pltpu.PrefetchScalarGridSpec(
        num_scalar_prefetch=1, grid=(N_CHIPS, n_row),
        in_specs=[pl.BlockSpec((tm, d), lambda j, i, order_ref: (i, 0))] + [ANY] * n,
        out_specs=[ANY] * n + [pl.BlockSpec((tm, ncol), lambda j, i, order_ref: (i, order_ref[j]))],
        scratch_shapes=[pltpu.VMEM((d, ncol), BF16)] + [pltpu.SemaphoreType.DMA((n, 3))] * 4
        + [pltpu.SemaphoreType.DMA])
    out = pl.pallas_call(
        body, grid_spec=grid_spec,
        out_shape=[jax.ShapeDtypeStruct(b.shape, b.dtype) for b in bufs]
        + [jax.ShapeDtypeStruct((lp, N_CHIPS * ncol), F32)],
        input_output_aliases={2 + a: a for a in range(n)},
        name="gather_in_proj", compiler_params=_params())(order, h, *bufs)
    return out[n], out[:n]


def _send_other_halves(grads, tag):
    n = len(grads)

    def body(*refs):
        srcs = refs[:n]
        dsts = refs[n:2 * n]
        send_sems, recv_sems = refs[2 * n:]
        x, y, c = _mesh_pos()
        copies = [pltpu.make_async_remote_copy(
            src_ref=srcs[a].at[j, 1 - c], dst_ref=dsts[a].at[j], send_sem=send_sems.at[a, j],
            recv_sem=recv_sems.at[a, j], device_id=(x, y, 1 - c), device_id_type=MESH)
            for a in range(n) for j in range(N_CHIPS)]
        for cp in copies:
            cp.start()
        for cp in copies:
            cp.wait()

    return pl.pallas_call(
        body, in_specs=[ANY] * n, out_specs=[ANY] * n,
        out_shape=[jax.ShapeDtypeStruct((N_CHIPS,) + g.shape[2:], F32) for g in grads],
        scratch_shapes=[pltpu.SemaphoreType.DMA((n, N_CHIPS))] * 2,
        name="send_other_halves_" + tag)(*grads)


HBM = pl.BlockSpec(memory_space=pltpu.HBM)
SEM = pl.BlockSpec(memory_space=pltpu.SEMAPHORE)


def _block_copies(n, srcs, dsts, send_sems, recv_sems):
    x, y, c = _mesh_pos()
    return [pltpu.make_async_remote_copy(
        src_ref=srcs[a].at[2 * px + py], dst_ref=dsts[a].at[k], send_sem=send_sems.at[3 * a + k],
        recv_sem=recv_sems.at[3 * a + k], device_id=(px, py, c), device_id_type=MESH)
        for a in range(n) for k, (px, py) in enumerate(_other_chips(x, y))]


def _exchange_start(blocked, tag):
    n = len(blocked)
    lands = [lax.empty((3,) + b.shape[1:], b.dtype) for b in blocked]
    bufs = [pltpu.with_memory_space_constraint(b, pltpu.HBM) for b in list(blocked) + lands]
    nb = 2 * n

    def body(*refs):
        for cp in _block_copies(n, refs[:n], refs[n:nb], refs[nb], refs[nb + 1]):
            cp.start()
        refs[-1][...] = jnp.zeros_like(refs[-1])

    out = pl.pallas_call(
        body, name="exchange_start_" + tag,
        in_specs=[HBM] * nb,
        out_shape=[pltpu.SemaphoreType.DMA((3 * n,)), pltpu.SemaphoreType.DMA((3 * n,))]
        + [pltpu.HBM(b.shape, b.dtype) for b in bufs] + [jax.ShapeDtypeStruct((8, 128), F32)],
        out_specs=[SEM] * 2 + [HBM] * nb + [pl.BlockSpec(memory_space=pltpu.VMEM)],
        input_output_aliases={i: 2 + i for i in range(nb)},
        compiler_params=pltpu.CompilerParams(has_side_effects=pltpu.SideEffectType.DATAFLOW_SIDE_EFFECTING),
    )(*bufs)
    return (out[:2], out[2:2 + nb]), out[-1]


def _exchange_wait(state, after, tag):
    sems, bufs = state
    nb = len(bufs)
    n = nb // 2

    def body(*refs):
        for cp in _block_copies(n, refs[:n], refs[n:nb], refs[nb], refs[nb + 1]):
            cp.wait_send()
            cp.wait_recv()

    out = pl.pallas_call(
        body, name="exchange_wait_" + tag,
        in_specs=[HBM] * nb + [SEM] * 2 + [ANY],
        out_shape=[pltpu.HBM(b.shape, b.dtype) for b in bufs],
        out_specs=[HBM] * nb,
        input_output_aliases={i: i for i in range(nb)},
        compiler_params=pltpu.CompilerParams(has_side_effects=pltpu.SideEffectType.DATAFLOW_SIDE_EFFECTING),
    )(*bufs, *sems, after)
    return out[n:nb]


def _exchange_small(small):
    def body(small_src, small_dst, ssend_sems, srecv_sems, local_sem):
        x, y, c = _mesh_pos()
        my_idx = 4 * x + 2 * y + c
        local = pltpu.make_async_copy(small_src, small_dst.at[my_idx], local_sem)
        local.start()
        others = []
        for r in range(1, 8):
            px = 1 - x if r & 4 else x
            py = 1 - y if r & 2 else y
            pc = 1 - c if r & 1 else c
            others.append((px, py, pc))
        for r, peer in enumerate(others):
            pltpu.make_async_remote_copy(
                src_ref=small_src, dst_ref=small_dst.at[my_idx], send_sem=ssend_sems.at[r],
                recv_sem=srecv_sems.at[r], device_id=peer, device_id_type=MESH).start()
        for r, (px, py, pc) in enumerate(others):
            pltpu.make_async_remote_copy(
                src_ref=small_src, dst_ref=small_dst.at[4 * px + 2 * py + pc], send_sem=ssend_sems.at[r],
                recv_sem=srecv_sems.at[r], device_id=(px, py, pc), device_id_type=MESH).wait()
        local.wait()

    return pl.pallas_call(
        body, in_specs=[ANY], out_specs=ANY, out_shape=jax.ShapeDtypeStruct((8,) + small.shape, F32),
        scratch_shapes=[pltpu.SemaphoreType.DMA((7,)), pltpu.SemaphoreType.DMA((7,)), pltpu.SemaphoreType.DMA],
        name="exchange_small")(small)


def _join_halves(bufs):
    n = len(bufs)

    def body(*refs):
        joined = refs[n:2 * n]
        send_sems, recv_sems = refs[2 * n:]
        x, y, c = _mesh_pos()
        for a in range(n):
            pltpu.make_async_remote_copy(
                src_ref=joined[a].at[c], dst_ref=joined[a].at[c], send_sem=send_sems.at[a],
                recv_sem=recv_sems.at[a], device_id=(x, y, 1 - c), device_id_type=MESH).start()
        for a in range(n):
            pltpu.make_async_remote_copy(
                src_ref=joined[a].at[c], dst_ref=joined[a].at[1 - c], send_sem=send_sems.at[a],
                recv_sem=recv_sems.at[a], device_id=(x, y, 1 - c), device_id_type=MESH).wait()

    return pl.pallas_call(
        body, in_specs=[ANY] * n, out_specs=[ANY] * n,
        out_shape=[jax.ShapeDtypeStruct(b.shape, b.dtype) for b in bufs],
        input_output_aliases={a: a for a in range(n)},
        scratch_shapes=[pltpu.SemaphoreType.DMA((n,))] * 2,
        name="join_halves")(*bufs)


def kernel(x, meta_tokens, norm_g, w_in, conv_w, conv_b, ln_g, ln_b, w_conv_out, lb_logits, gnorm_g, w_rec_out, w_out, final_g, loss_target, m_meta_tokens, m_norm_g, m_w_in, m_conv_w, m_conv_b, m_ln_g, m_ln_b, m_w_conv_out, m_lb_logits, m_gnorm_g, m_w_rec_out, m_w_out, m_final_g, v_meta_tokens, v_norm_g, v_w_in, v_conv_w, v_conv_b, v_ln_g, v_ln_b, v_w_conv_out, v_lb_logits, v_gnorm_g, v_w_rec_out, v_w_out, v_final_g):
    seq, d = x.shape[1], x.shape[2]
    n_meta = meta_tokens.shape[0]
    n_pad = CHUNK - n_meta
    ds = d // N_CHIPS
    chip = 2 * lax.axis_index("x") + lax.axis_index("y")

    conv_w_pad = jnp.pad(conv_w[0], ((0, HALO - CONV_WIDTH), (0, 0)))
    chip_idx = chip.astype(jnp.int32).reshape(1)
    (small_g,) = _gather_weights([
        _place_shard("place_small", jnp.concatenate([conv_w_pad, meta_tokens], axis=0), chip_idx, F32)])
    cw_full = jnp.transpose(small_g[:, 0:HALO], (1, 0, 2)).reshape(HALO, d)
    meta_full = jnp.transpose(small_g[:, HALO:HALO + n_meta], (1, 0, 2)).reshape(n_meta, d)

    hres = jnp.concatenate([jnp.zeros((n_pad, d), F32), meta_full, x[0]], axis=0)
    target = jnp.pad(loss_target[0], ((CHUNK, 0), (0, 0)))
    final_g2 = final_g.reshape(1, d)
    h = _rmsnorm_fwd(hres, norm_g)
    fx, fy = 1 - lax.axis_index("x"), 1 - lax.axis_index("y")
    order = jnp.stack([chip, 2 * fx + (1 - fy), 2 * (1 - fx) + fy, 2 * fx + fy]).astype(jnp.int32)
    proj, (win_g, sq_g) = _gather_in_proj(h, [
        _place_shard("place_w_in", w_in[0], chip_idx, BF16),
        _place_shard("place_square", jnp.concatenate([w_conv_out[0], w_rec_out[0], w_out[0]], axis=0),
                     chip_idx, BF16)], order)
    wc_full = sq_g[:, 0:ds].reshape(d, d)
    wr_full = sq_g[:, ds:2 * ds].reshape(d, d)
    wo_full = sq_g[:, 2 * ds:3 * ds].reshape(d, d)
    c, yc_in, y_conv = _conv_fwd(proj, cw_full, conv_b, ln_g, ln_b, wc_full)
    o, s_all = _hgrn_fwd(proj, lb_logits, n_pad)
    yr_in, merged, y_rec, dout, loss_acc, dfinal_g = _tail_fwd(
        o, proj, y_conv, hres, target, gnorm_g, final_g2, wr_full, wo_full)

    (dyc, dyr, dout_bf, dz, dproj, do, dc, dgnorm_g, dln_g, dln_b) = _tail_bwd(
        dout, proj, y_conv, y_rec, o, c, wo_full, wr_full, wc_full, ln_g, ln_b, gnorm_g)
    g_wc = _weight_grad(yc_in, dyc, "grad_w_conv_out", False)
    g_wr = _weight_grad(yr_in, dyr, "grad_w_rec_out", False)
    g_wo = _weight_grad(merged, dout_bf, "grad_w_out", False)

    core = lax.axis_index("c").astype(jnp.int32).reshape(1)

    def chip_sum_and_start(g, tag):
        g = g.reshape(N_CHIPS, 2, g.shape[1] // 2, g.shape[2])
        (from_sibling,) = _send_other_halves([g], tag)
        sums = _chip_half_sum("chip_half_sum_" + tag, g, from_sibling, core)
        in_flight, token = _exchange_start([sums[1]], tag)
        return sums[0], in_flight, token[0:1, 0:1]

    g_sq = jnp.concatenate([g.reshape(N_CHIPS, ds, d) for g in (g_wc, g_wr, g_wo)], axis=1)
    sum_sq, flight_sq, token_sq = chip_sum_and_start(g_sq, "square")
    dproj, dlb_logits = _hgrn_bwd(proj, do, s_all, lb_logits + token_sq, n_pad, dproj)
    dproj, dconv_w, dconv_b = _conv_bwd(dc, proj, cw_full, dz, dproj)
    (recv_sq,) = _exchange_wait(flight_sq, dconv_b, "square")
    g_win = _weight_grad(h, dproj, "grad_w_in", True)
    sum_win, flight_win, token_win = chip_sum_and_start(g_win, "w_in")
    dhres, dnorm_g = _in_proj_bwd(dproj, jnp.transpose(win_g, (0, 2, 1)), hres, norm_g + token_win, dout)
    grad_x = dhres[CHUNK:][None]
    (recv_win,) = _exchange_wait(flight_win, dnorm_g, "w_in")
    small = jnp.concatenate([dnorm_g, dconv_b, dln_g, dln_b, dlb_logits, dgnorm_g, dfinal_g,
                             dhres[n_pad:CHUNK], dconv_w[:CONV_WIDTH],
                             jnp.zeros((1, d), F32)], axis=0)
    small_slots = _exchange_small(small)
    chip_core = jnp.concatenate([chip_idx, core])
    totals = [_block_half_total("block_half_total_" + nm, s, r, chip_core)
              for nm, s, r in zip(("w_in", "square"), (sum_win, sum_sq), (recv_win, recv_sq))]
    gt_win, gt_sq = [t.reshape(2 * t.shape[1], t.shape[2]) for t in _join_halves(totals)]
    small_sum = _sum_slots("sum_small", small_slots)

    res = {}
    res["w_in"] = _adamw("adamw_w_in", w_in[0], m_w_in[0], v_w_in[0], gt_win)
    res["w_conv_out"] = _adamw("adamw_w_conv_out", w_conv_out[0], m_w_conv_out[0], v_w_conv_out[0], gt_sq[0:ds])
    res["w_rec_out"] = _adamw("adamw_w_rec_out", w_rec_out[0], m_w_rec_out[0], v_w_rec_out[0], gt_sq[ds:2 * ds])
    res["w_out"] = _adamw("adamw_w_out", w_out[0], m_w_out[0], v_w_out[0], gt_sq[2 * ds:3 * ds])
    big = {k: tuple(a[None] for a in v) for k, v in res.items()}

    rep_names = ("norm_g", "conv_b", "ln_g", "ln_b", "lb_logits", "gnorm_g", "final_g")
    rep_w = (norm_g, conv_b, ln_g, ln_b, lb_logits, gnorm_g, final_g2)
    rep_m = (m_norm_g, m_conv_b, m_ln_g, m_ln_b, m_lb_logits, m_gnorm_g, m_final_g.reshape(1, d))
    rep_v = (v_norm_g, v_conv_b, v_ln_g, v_ln_b, v_lb_logits, v_gnorm_g, v_final_g.reshape(1, d))
    rep = _adamw("adamw_replicated", jnp.concatenate(rep_w, 0), jnp.concatenate(rep_m, 0),
                 jnp.concatenate(rep_v, 0), small_sum[0:8])
    rep_rows = {"norm_g": (0, 1), "conv_b": (1, 2), "ln_g": (2, 3), "ln_b": (3, 4), "lb_logits": (4, 6),
                "gnorm_g": (6, 7), "final_g": (7, 8)}
    small_out = {}
    for nm in rep_names:
        lo, hi = rep_rows[nm]
        vals = tuple(a[lo:hi] for a in rep)
        if nm == "final_g":
            vals = tuple(a.reshape(d) for a in vals)
        small_out[nm] = vals
    cw_row = 8 + n_meta
    g_meta = lax.dynamic_slice_in_dim(small_sum[8:cw_row], chip * ds, ds, axis=1)
    small_out["meta_tokens"] = _adamw("adamw_meta", meta_tokens, m_meta_tokens, v_meta_tokens, g_meta)
    g_cw = lax.dynamic_slice_in_dim(small_sum[cw_row:cw_row + HALO], chip * ds, ds, axis=1)
    pad_rows = ((0, HALO - CONV_WIDTH), (0, 0))
    cw_res = _adamw("adamw_conv_w", conv_w_pad, jnp.pad(m_conv_w[0], pad_rows),
                    jnp.pad(v_conv_w[0], pad_rows, constant_values=1.0), g_cw)
    small_out["conv_w"] = tuple(a[:CONV_WIDTH][None] for a in cw_res)

    loss = lax.psum(loss_acc[0, 0], ("x", "y", "c"))

    order = ("meta_tokens", "norm_g", "w_in", "conv_w", "conv_b", "ln_g", "ln_b", "w_conv_out", "lb_logits",
             "gnorm_g", "w_rec_out", "w_out", "final_g")
    allres = {**big, **small_out}
    outs = [loss, grad_x]
    for field in range(4):
        outs.extend(allres[nm][field] for nm in order)
    return tuple(outs)
```

```python
import numpy as np

import jax
import jax.numpy as jnp
from jax import lax
from jax.experimental import pallas as pl
from jax.experimental.pallas import tpu as pltpu

F32 = jnp.float32
BF16 = jnp.bfloat16

EPS = 1e-6
CHUNK = 64
N_LEVELS = 6
CONV_WIDTH = 31
HALO = 32
CONV_ROWS = 32
CONV_LANES = 256
HEAD = 128
W_IN_COL_TILES = 1
HEADS_PER_TRIP = 8
N_CHIPS = 4
VMEM_LIMIT_BYTES = 56 * 1024 * 1024

ADAM_LR = 0.001
ADAM_B1 = 0.9
ADAM_B2 = 0.999
ADAM_EPS = 1e-08
ADAM_WD = 0.01
ADAM_STEP = 10

MESH = pl.DeviceIdType.MESH
ANY = pl.BlockSpec(memory_space=pl.ANY)

NT = (((1,), (1,)), ((), ()))
TN = (((0,), (0,)), ((), ()))


def _params(**kw):
    return pltpu.CompilerParams(vmem_limit_bytes=VMEM_LIMIT_BYTES, **kw)


def _sigmoid(x):
    return jax.nn.sigmoid(x)


def _dsilu(x, s):
    return s * (1.0 + x * (1.0 - s))


def _row_tile(lp):
    for t in (320, 256, 192, 128, 64):
        if lp % t == 0:
            return t
    raise ValueError(f"unsupported padded length {lp}")


def _mm_row_tile(lp):
    for t in (832, 640, 320, 256, 192, 128, 64):
        if lp % t == 0:
            return t
    raise ValueError(f"unsupported padded length {lp}")


def _dot3(m_bf16, x):
    hi = x.astype(BF16)
    r1 = x - hi.astype(F32)
    mid = r1.astype(BF16)
    lo = (r1 - mid.astype(F32)).astype(BF16)
    return (jnp.dot(m_bf16, hi, preferred_element_type=F32)
            + jnp.dot(m_bf16, mid, preferred_element_type=F32)
            + jnp.dot(m_bf16, lo, preferred_element_type=F32))


def _dot2(m_bf16, x):
    hi = x.astype(BF16)
    lo = (x - hi.astype(F32)).astype(BF16)
    return (jnp.dot(m_bf16, hi, preferred_element_type=F32)
            + jnp.dot(m_bf16, lo, preferred_element_type=F32))


def _col_to_row(col):
    return jnp.broadcast_to(col, (HEAD, 8)).T[0:1, :]


def _row_to_col(row):
    return jnp.broadcast_to(row, (8, HEAD)).T[:, 0:1]


def _hgrn_tables():
    t = np.arange(CHUNK)
    ltri = (t[None, :] <= t[:, None]).astype(np.float32)
    mats = [ltri]
    for lvl in range(1, N_LEVELS + 1):
        blk = CHUNK >> (lvl - 1)
        mid = (t // blk) * blk + blk // 2
        mats.append(ltri[mid - 1])
    after = (t[None, :] >= t[:, None]).astype(np.float32)
    before = (t[None, :] < t[:, None]).astype(np.float32)
    return jnp.asarray(np.concatenate(mats, 0), BF16), jnp.asarray(np.concatenate([after, before], 1), BF16)


def _rmsnorm_fwd(hres, g):
    lp, d = hres.shape
    tm = _row_tile(lp)

    def body(x_ref, g_ref, h_ref):
        x = x_ref[...]
        r = lax.rsqrt(jnp.mean(x * x, axis=-1, keepdims=True) + EPS)
        h_ref[...] = (x * r * g_ref[...]).astype(BF16)

    return pl.pallas_call(
        body, grid=(lp // tm,),
        in_specs=[pl.BlockSpec((tm, d), lambda i: (i, 0)), pl.BlockSpec((1, d), lambda i: (0, 0))],
        out_specs=pl.BlockSpec((tm, d), lambda i: (i, 0)),
        out_shape=jax.ShapeDtypeStruct((lp, d), BF16),
        name="rmsnorm_fwd", compiler_params=_params())(hres, g)


def _in_proj(h, wg):
    lp, d = h.shape
    _, _, ncol = wg.shape
    tm = _mm_row_tile(lp)
    nt = W_IN_COL_TILES
    tn = ncol // nt

    def body(h_ref, w_ref, o_ref):
        o_ref[...] = jnp.dot(h_ref[...], w_ref[0], preferred_element_type=F32)

    return pl.pallas_call(
        body, grid=(N_CHIPS, nt, lp // tm),
        in_specs=[pl.BlockSpec((tm, d), lambda j, n, i: (i, 0)),
                  pl.BlockSpec((1, d, tn), lambda j, n, i: (j, 0, n))],
        out_specs=pl.BlockSpec((tm, tn), lambda j, n, i: (i, j * nt + n)),
        out_shape=jax.ShapeDtypeStruct((lp, N_CHIPS * ncol), F32),
        name="in_proj", compiler_params=_params())(h, wg)


def _conv_fwd(proj, conv_w, conv_b, ln_g, ln_b, w_conv):
    lp = proj.shape[0]
    d = conv_b.shape[1]
    tm = _row_tile(lp)
    hb = tm // HALO

    def body(ua_ref, ub_ref, z_ref, uap_ref, ubp_ref, cw_ref, cb_ref, lg_ref, lb_ref, w_ref,
             c_ref, ycin_ref, yconv_ref, aext_ref):
        i = pl.program_id(0)
        a_prev = uap_ref[...] * _sigmoid(ubp_ref[...])
        aext_ref[0:HALO, :] = jnp.where(i > 0, a_prev, 0.0)
        aext_ref[HALO:HALO + tm, :] = ua_ref[...] * _sigmoid(ub_ref[...])

        def row_block(r, carry):
            r0 = pl.multiple_of(r * CONV_ROWS, CONV_ROWS)
            for cs in range(d // CONV_LANES):
                cl = slice(cs * CONV_LANES, (cs + 1) * CONV_LANES)
                blk = aext_ref[pl.ds(r0, CONV_ROWS + HALO), cl]
                acc = jnp.zeros((CONV_ROWS, CONV_LANES), F32) + cb_ref[:, cl]
                for b in range(8):
                    sh = blk if b == 0 else pltpu.roll(blk, CONV_ROWS + HALO - b, axis=0)
                    for a in range(5):
                        j = 8 * a + b - 2
                        if 0 <= j < CONV_WIDTH:
                            acc = acc + cw_ref[j:j + 1, cl] * sh[8 * a:8 * a + CONV_ROWS, :]
                c_ref[pl.ds(r0, CONV_ROWS), cl] = acc
            return carry

        lax.fori_loop(0, tm // CONV_ROWS, row_block, 0)

        c = c_ref[...]
        mu = jnp.mean(c, axis=-1, keepdims=True)
        xc = c - mu
        rstd = lax.rsqrt(jnp.mean(xc * xc, axis=-1, keepdims=True) + EPS)
        ln = xc * rstd * lg_ref[...] + lb_ref[...]
        s = ln * _sigmoid(ln)
        z = z_ref[...]
        ycin = (s * (z * _sigmoid(z))).astype(BF16)
        ycin_ref[...] = ycin
        yconv_ref[...] = jnp.dot(ycin, w_ref[...], preferred_element_type=F32)

    row = lambda p: pl.BlockSpec((tm, d), lambda i, p=p: (i, p))
    halo = lambda p: pl.BlockSpec((HALO, d), lambda i, p=p: (jnp.maximum(i * hb - 1, 0), p))
    vec = pl.BlockSpec((1, d), lambda i: (0, 0))
    return pl.pallas_call(
        body, grid=(lp // tm,),
        in_specs=[row(0), row(1), row(2), halo(0), halo(1),
                  pl.BlockSpec((HALO, d), lambda i: (0, 0)), vec, vec, vec,
                  pl.BlockSpec((d, d), lambda i: (0, 0))],
        out_specs=[pl.BlockSpec((tm, d), lambda i: (i, 0))] * 3,
        out_shape=[jax.ShapeDtypeStruct((lp, d), F32), jax.ShapeDtypeStruct((lp, d), BF16),
                   jax.ShapeDtypeStruct((lp, d), F32)],
        scratch_shapes=[pltpu.VMEM((HALO + tm, d), F32)],
        name="conv_fwd", compiler_params=_params())(
            proj, proj, proj, proj, proj, conv_w, conv_b, ln_g, ln_b, w_conv)


def _lower_bound(lbl_ref):
    l0 = lbl_ref[0:1, :]
    l1 = lbl_ref[1:2, :]
    m = jnp.maximum(l0, l1)
    e0 = jnp.exp(l0 - m)
    e1 = jnp.exp(l1 - m)
    p0 = e0 / (e0 + e1)
    return p0, p0 * (e1 / (e0 + e1))


def _level_masks():
    r2 = lax.broadcasted_iota(jnp.int32, (CHUNK, CHUNK), 0)
    c2 = lax.broadcasted_iota(jnp.int32, (CHUNK, CHUNK), 1)
    out = []
    for lvl in range(1, N_LEVELS + 1):
        blk = CHUNK >> (lvl - 1)
        sh = blk.bit_length() - 1
        same = (r2 >> sh) == (c2 >> sh)
        t_upper = (r2 & (blk - 1)) >= (blk // 2)
        s_lower = (c2 & (blk - 1)) < (blk // 2)
        out.append(jnp.logical_and(same, jnp.logical_and(t_upper, s_lower)))
    return out


def _gates(qr, fr, lb, valid):
    sq = _sigmoid(qr)
    q = qr * sq
    sf = _sigmoid(fr)
    f = lb + (1.0 - lb) * sf
    g = jnp.where(valid, jnp.log(f), 0.0)
    k = jnp.where(valid, 1.0 - f, 0.0)
    return q, sq, f, sf, g, k


def _level_factor(b, r):
    d = b - r
    return jnp.exp(jnp.minimum(d, -d))


def _hgrn_fwd(proj, lb_logits, n_pad):
    lp = proj.shape[0]
    d = lb_logits.shape[1]
    n_heads = d // HEAD
    nc = lp // CHUNK
    tab, _ = _hgrn_tables()
    n_tab = tab.shape[0]

    def body(qr_ref, fr_ref, ir_ref, lbl_ref, tab_ref, o_ref, sall_ref, s_ref, t_ref):
        n = pl.program_id(0)

        @pl.when(n == 0)
        def _():
            s_ref[...] = jnp.zeros_like(s_ref)

        sall_ref[0] = s_ref[...]
        lb_all, _ = _lower_bound(lbl_ref)
        rid = lax.broadcasted_iota(jnp.int32, (CHUNK, 1), 0)
        valid = jnp.logical_or(n > 0, rid >= n_pad)
        f_all = lb_all + (1.0 - lb_all) * _sigmoid(fr_ref[...])
        t_ref[...] = _dot2(tab_ref[...], jnp.where(valid, jnp.log(f_all), 0.0))
        masks = _level_masks()

        def head(h):
            off = h * HEAD if isinstance(h, int) else pl.multiple_of(h * HEAD, HEAD)
            hs = pl.ds(off, HEAD)
            lb = _lower_bound_slice(lbl_ref, hs)
            q, _, _, _, _, k = _gates(qr_ref[:, hs], fr_ref[:, hs], lb, valid)
            v = ir_ref[:, hs]
            b = t_ref[0:CHUNK, hs]
            s0 = s_ref[hs, :]
            o = jnp.dot((q * jnp.exp(b)).astype(BF16), s0.astype(BF16), preferred_element_type=F32)
            o = o + jnp.sum(q * k, axis=-1, keepdims=True) * v
            a = jnp.zeros((CHUNK, CHUNK), F32)
            for lvl in range(1, N_LEVELS + 1):
                e = _level_factor(b, t_ref[CHUNK * lvl:CHUNK * (lvl + 1), hs])
                p = lax.dot_general((q * e).astype(BF16), (k * e).astype(BF16), NT, preferred_element_type=F32)
                a = a + jnp.where(masks[lvl - 1], p, 0.0)
            vb = v.astype(BF16)
            o_ref[:, hs] = o + jnp.dot(a.astype(BF16), vb, preferred_element_type=F32)
            b_last = t_ref[CHUNK - 1:CHUNK, hs]
            khat = (k * jnp.exp(b_last - b)).astype(BF16)
            s_ref[hs, :] = _row_to_col(jnp.exp(b_last)) * s0 + lax.dot_general(khat, vb, TN, preferred_element_type=F32)
        per_trip = min(HEADS_PER_TRIP, n_heads)

        def head_group(p, carry):
            for u in range(per_trip):
                head(p * per_trip + u)
            return carry

        if n_heads == per_trip:
            head_group(0, 0)
        else:
            lax.fori_loop(0, n_heads // per_trip, head_group, 0)

    piece = lambda p: pl.BlockSpec((CHUNK, d), lambda n, p=p: (n, p))
    return pl.pallas_call(
        body, grid=(nc,),
        in_specs=[piece(3), piece(4), piece(5), pl.BlockSpec((2, d), lambda n: (0, 0)),
                  pl.BlockSpec((n_tab, CHUNK), lambda n: (0, 0))],
        out_specs=[pl.BlockSpec((CHUNK, d), lambda n: (n, 0)), pl.BlockSpec((1, d, HEAD), lambda n: (n, 0, 0))],
        out_shape=[jax.ShapeDtypeStruct((lp, d), F32), jax.ShapeDtypeStruct((nc, d, HEAD), F32)],
        scratch_shapes=[pltpu.VMEM((d, HEAD), F32), pltpu.VMEM((n_tab, d), F32)],
        name="hgrn_fwd", compiler_params=_params())(proj, proj, proj, lb_logits, tab)


def _lower_bound_slice(lbl_ref, hs):
    l0 = lbl_ref[0:1, hs]
    l1 = lbl_ref[1:2, hs]
    m = jnp.maximum(l0, l1)
    e0 = jnp.exp(l0 - m)
    e1 = jnp.exp(l1 - m)
    return e0 / (e0 + e1)


def _tail_fwd(o, proj, y_conv, hres, target, gnorm_g, final_g, w_rec, w_out):
    lp, d = o.shape
    n_heads = d // HEAD
    tm = _row_tile(lp)

    def body(o_ref, gr_ref, mc_ref, mr_ref, yc_ref, x_ref, t_ref, gn_ref, fg_ref, wr_ref, wo_ref,
             yrin_ref, mg_ref, yrec_ref, dout_ref, loss_ref, dfg_ref):
        i = pl.program_id(0)

        @pl.when(i == 0)
        def _():
            loss_ref[...] = jnp.zeros_like(loss_ref)
            dfg_ref[...] = jnp.zeros_like(dfg_ref)

        for h in range(n_heads):
            hs = slice(h * HEAD, (h + 1) * HEAD)
            oh = o_ref[:, hs]
            on = oh * lax.rsqrt(jnp.mean(oh * oh, axis=-1, keepdims=True) + EPS) * gn_ref[:, hs]
            gr = gr_ref[:, hs]
            yrin_ref[:, hs] = (on * (gr * _sigmoid(gr))).astype(BF16)
        yrec = jnp.dot(yrin_ref[...], wr_ref[...], preferred_element_type=F32)
        yrec_ref[...] = yrec
        merged = (_sigmoid(mc_ref[...]) * yc_ref[...] + _sigmoid(mr_ref[...]) * yrec).astype(BF16)
        mg_ref[...] = merged
        out = x_ref[...] + jnp.dot(merged, wo_ref[...], preferred_element_type=F32)
        r = lax.rsqrt(jnp.mean(out * out, axis=-1, keepdims=True) + EPS)
        yhat = out * r
        fg = fg_ref[...]
        rid = lax.broadcasted_iota(jnp.int32, (tm, 1), 0) + i * tm
        err = jnp.where(rid >= CHUNK, yhat * fg - t_ref[...], 0.0)
        loss_ref[...] += 0.5 * jnp.sum(err * err) / d
        dy = err / d
        dfg_ref[...] += jnp.sum(dy * yhat, axis=0, keepdims=True)
        dyh = dy * fg
        dout_ref[...] = r * (dyh - yhat * jnp.mean(dyh * yhat, axis=-1, keepdims=True))

    row = lambda p: pl.BlockSpec((tm, d), lambda i, p=p: (i, p))
    vec = pl.BlockSpec((1, d), lambda i: (0, 0))
    mat = pl.BlockSpec((d, d), lambda i: (0, 0))
    return pl.pallas_call(
        body, grid=(lp // tm,),
        in_specs=[row(0), row(6), row(7), row(8), row(0), row(0), row(0), vec, vec, mat, mat],
        out_specs=[row(0), row(0), row(0), row(0), pl.BlockSpec((8, 128), lambda i: (0, 0)), vec],
        out_shape=[jax.ShapeDtypeStruct((lp, d), BF16), jax.ShapeDtypeStruct((lp, d), BF16),
                   jax.ShapeDtypeStruct((lp, d), F32), jax.ShapeDtypeStruct((lp, d), F32),
                   jax.ShapeDtypeStruct((8, 128), F32), jax.ShapeDtypeStruct((1, d), F32)],
        name="tail_fwd", compiler_params=_params())(
            o, proj, proj, proj, y_conv, hres, target, gnorm_g, final_g, w_rec, w_out)


def _tail_bwd(dout, proj, y_conv, y_rec, o, c, w_out, w_rec, w_conv, ln_g, ln_b, gnorm_g):
    lp, d = dout.shape
    n_heads = d // HEAD
    tm = _row_tile(lp)

    def body(dout_ref, mc_ref, mr_ref, z_ref, gr_ref, yc_ref, yrec_ref, o_ref, c_ref,
             wo_ref, wr_ref, wc_ref, lg_ref, lb_ref, gn_ref,
             dyc_ref, dyr_ref, doutb_ref, dz_ref, dp_ref, do_ref, dc_ref,
             dgn_ref, dlg_ref, dlb_ref, dyrin_ref):
        i = pl.program_id(0)

        @pl.when(i == 0)
        def _():
            dgn_ref[...] = jnp.zeros_like(dgn_ref)
            dlg_ref[...] = jnp.zeros_like(dlg_ref)
            dlb_ref[...] = jnp.zeros_like(dlb_ref)

        doutb = dout_ref[...].astype(BF16)
        doutb_ref[...] = doutb
        dmerged = lax.dot_general(doutb, wo_ref[...], NT, preferred_element_type=F32)
        smc = _sigmoid(mc_ref[...])
        smr = _sigmoid(mr_ref[...])
        dyc = (dmerged * smc).astype(BF16)
        dyr = (dmerged * smr).astype(BF16)
        dyc_ref[...] = dyc
        dyr_ref[...] = dyr
        dp_ref[:, d:2 * d] = (dmerged * yc_ref[...] * smc * (1.0 - smc)).astype(BF16)
        dp_ref[:, 2 * d:3 * d] = (dmerged * yrec_ref[...] * smr * (1.0 - smr)).astype(BF16)

        dyrin_ref[...] = lax.dot_general(dyr, wr_ref[...], NT, preferred_element_type=F32)
        for h in range(n_heads):
            hs = slice(h * HEAD, (h + 1) * HEAD)
            oh = o_ref[:, hs]
            rstd = lax.rsqrt(jnp.mean(oh * oh, axis=-1, keepdims=True) + EPS)
            ohat = oh * rstd
            gn = gn_ref[:, hs]
            gr = gr_ref[:, hs]
            sg = _sigmoid(gr)
            dyrin = dyrin_ref[:, hs]
            don = dyrin * (gr * sg)
            dp_ref[:, hs] = (dyrin * (ohat * gn) * _dsilu(gr, sg)).astype(BF16)
            dgn_ref[:, hs] += jnp.sum(don * ohat, axis=0, keepdims=True)
            doh = don * gn
            do_ref[:, hs] = rstd * (doh - ohat * jnp.mean(doh * ohat, axis=-1, keepdims=True))

        dycin = lax.dot_general(dyc, wc_ref[...], NT, preferred_element_type=F32)
        c = c_ref[...]
        mu = jnp.mean(c, axis=-1, keepdims=True)
        xc = c - mu
        rstd = lax.rsqrt(jnp.mean(xc * xc, axis=-1, keepdims=True) + EPS)
        nrm = xc * rstd
        lg = lg_ref[...]
        ln = nrm * lg + lb_ref[...]
        sl = _sigmoid(ln)
        z = z_ref[...]
        sz = _sigmoid(z)
        dz_ref[...] = (dycin * (ln * sl) * _dsilu(z, sz)).astype(BF16)
        dln = dycin * (z * sz) * _dsilu(ln, sl)
        dlg_ref[...] += jnp.sum(dln * nrm, axis=0, keepdims=True)
        dlb_ref[...] += jnp.sum(dln, axis=0, keepdims=True)
        dn = dln * lg
        dc_ref[...] = rstd * (dn - jnp.mean(dn, axis=-1, keepdims=True)
                              - nrm * jnp.mean(dn * nrm, axis=-1, keepdims=True))

    row = lambda p: pl.BlockSpec((tm, d), lambda i, p=p: (i, p))
    vec = pl.BlockSpec((1, d), lambda i: (0, 0))
    mat = pl.BlockSpec((d, d), lambda i: (0, 0))
    act_bf = jax.ShapeDtypeStruct((lp, d), BF16)
    act_f32 = jax.ShapeDtypeStruct((lp, d), F32)
    vec_f32 = jax.ShapeDtypeStruct((1, d), F32)
    return pl.pallas_call(
        body, grid=(lp // tm,),
        in_specs=[row(0), row(7), row(8), row(2), row(6), row(0), row(0), row(0), row(0),
                  mat, mat, mat, vec, vec, vec],
        out_specs=[row(0)] * 4 + [pl.BlockSpec((tm, 3 * d), lambda i: (i, 2))] + [row(0)] * 2 + [vec] * 3,
        out_shape=[act_bf] * 4 + [jax.ShapeDtypeStruct((lp, 9 * d), BF16)] + [act_f32] * 2 + [vec_f32] * 3,
        scratch_shapes=[pltpu.VMEM((tm, d), F32)],
        name="tail_bwd", compiler_params=_params())(
            dout, proj, proj, proj, proj, y_conv, y_rec, o, c, w_out, w_rec, w_conv, ln_g, ln_b, gnorm_g)


def _hgrn_bwd(proj, do, s_all, lb_logits, n_pad, dproj):
    lp, d = do.shape
    n_heads = d // HEAD
    nc = lp // CHUNK
    tab, utri = _hgrn_tables()
    n_tab = tab.shape[0]

    def body(qr_ref, fr_ref, ir_ref, do_ref, s0_ref, lbl_ref, tab_ref, ut_ref, _,
             dp_ref, dlbl_ref, ds_ref, t_ref, dlb_ref):
        n = pl.program_id(0)
        chunk = nc - 1 - n

        @pl.when(n == 0)
        def _():
            ds_ref[...] = jnp.zeros_like(ds_ref)
            dlb_ref[...] = jnp.zeros_like(dlb_ref)

        lb_all, pp = _lower_bound(lbl_ref)
        rid = lax.broadcasted_iota(jnp.int32, (CHUNK, 1), 0)
        valid = jnp.logical_or(chunk > 0, rid >= n_pad)
        f_all = lb_all + (1.0 - lb_all) * _sigmoid(fr_ref[...])
        t_ref[...] = _dot2(tab_ref[...], jnp.where(valid, jnp.log(f_all), 0.0))
        masks = _level_masks()
        ut = ut_ref[...]

        def head(h):
            off = h * HEAD if isinstance(h, int) else pl.multiple_of(h * HEAD, HEAD)
            hs = pl.ds(off, HEAD)
            lb = _lower_bound_slice(lbl_ref, hs)
            qr = qr_ref[:, hs]
            q, sq, f, sf, _, k = _gates(qr, fr_ref[:, hs], lb, valid)
            v = ir_ref[:, hs]
            do_h = do_ref[:, hs]
            b = t_ref[0:CHUNK, hs]
            b_last = t_ref[CHUNK - 1:CHUNK, hs]
            s0 = s0_ref[0, hs, :]
            ds1 = ds_ref[hs, :]
            eb = jnp.exp(b)
            ekl = jnp.exp(b_last - b)
            do_bf = do_h.astype(BF16)
            v_bf = v.astype(BF16)
            ds1_bf = ds1.astype(BF16)

            da = lax.dot_general(do_bf, v_bf, NT, preferred_element_type=F32)
            da_diag = jnp.sum(do_h * v, axis=-1, keepdims=True)
            a = jnp.zeros((CHUNK, CHUNK), F32)
            dq_x = eb * lax.dot_general(do_bf, s0.astype(BF16), NT, preferred_element_type=F32)
            dk_x = ekl * lax.dot_general(v_bf, ds1_bf, NT, preferred_element_type=F32)
            x_after = q * dq_x
            x_before = k * dk_x
            for lvl in range(1, N_LEVELS + 1):
                e = _level_factor(b, t_ref[CHUNK * lvl:CHUNK * (lvl + 1), hs])
                qt = (q * e).astype(BF16)
                kt = (k * e).astype(BF16)
                p = lax.dot_general(qt, kt, NT, preferred_element_type=F32)
                a = a + jnp.where(masks[lvl - 1], p, 0.0)
                dam = jnp.where(masks[lvl - 1], da, 0.0).astype(BF16)
                dqt = jnp.dot(dam, kt, preferred_element_type=F32)
                dkt = lax.dot_general(dam, qt, TN, preferred_element_type=F32)
                dq_x = dq_x + e * dqt
                dk_x = dk_x + e * dkt
                x_after = x_after + (qt.astype(F32) * dqt - kt.astype(F32) * dkt)

            dv = (lax.dot_general(a.astype(BF16), do_bf, TN, preferred_element_type=F32)
                  + jnp.sum(q * k, axis=-1, keepdims=True) * do_h
                  + jnp.dot((k * ekl).astype(BF16), ds1_bf, preferred_element_type=F32))
            dp_ref[:, pl.ds(2 * d + off, HEAD)] = dv.astype(BF16)

            carried = jnp.exp(b_last) * _col_to_row(jnp.sum(s0 * ds1, axis=-1, keepdims=True))
            dg = _dot3(ut, jnp.concatenate([x_after, x_before], axis=0)) + carried
            dq = dq_x + da_diag * k
            dk = dk_x + da_diag * q
            dp_ref[:, hs] = (dq * _dsilu(qr, sq)).astype(BF16)
            df = jnp.where(valid, dg / f - dk, 0.0)
            dp_ref[:, pl.ds(d + off, HEAD)] = (df * (1.0 - lb) * sf * (1.0 - sf)).astype(BF16)
            dlb_ref[:, hs] += jnp.sum(df * (1.0 - sf), axis=0, keepdims=True)

            ds_ref[hs, :] = (_row_to_col(jnp.exp(b_last)) * ds1
                             + lax.dot_general((q * eb).astype(BF16), do_bf, TN, preferred_element_type=F32))
        per_trip = min(HEADS_PER_TRIP, n_heads)

        def head_group(p, carry):
            for u in range(per_trip):
                head(p * per_trip + u)
            return carry

        if n_heads == per_trip:
            head_group(0, 0)
        else:
            lax.fori_loop(0, n_heads // per_trip, head_group, 0)

        @pl.when(n == nc - 1)
        def _():
            dl0 = dlb_ref[...] * pp
            dlbl_ref[0:1, :] = dl0
            dlbl_ref[1:2, :] = -dl0

    piece = lambda p: pl.BlockSpec((CHUNK, d), lambda n, p=p: (nc - 1 - n, p))
    return pl.pallas_call(
        body, grid=(nc,),
        in_specs=[piece(3), piece(4), piece(5), piece(0),
                  pl.BlockSpec((1, d, HEAD), lambda n: (nc - 1 - n, 0, 0)),
                  pl.BlockSpec((2, d), lambda n: (0, 0)),
                  pl.BlockSpec((n_tab, CHUNK), lambda n: (0, 0)),
                  pl.BlockSpec((CHUNK, 2 * CHUNK), lambda n: (0, 0)), ANY],
        out_specs=[pl.BlockSpec((CHUNK, 3 * d), lambda n: (nc - 1 - n, 1)), pl.BlockSpec((2, d), lambda n: (0, 0))],
        out_shape=[jax.ShapeDtypeStruct(dproj.shape, BF16), jax.ShapeDtypeStruct((2, d), F32)],
        input_output_aliases={8: 0},
        scratch_shapes=[pltpu.VMEM((d, HEAD), F32), pltpu.VMEM((n_tab, d), F32), pltpu.VMEM((1, d), F32)],
        name="hgrn_bwd", compiler_params=_params())(proj, proj, proj, do, s_all, lb_logits, tab, utri, dproj)


def _conv_bwd(dc, proj, conv_w, dz, dproj):
    lp, d = dc.shape
    tm = _row_tile(lp)
    hb = tm // HALO
    n_tiles = lp // tm
    last_halo = lp // HALO - 1

    def body(dc_ref, dcn_ref, ua_ref, ub_ref, uap_ref, ubp_ref, cw_ref, dz_ref, _,
             dp_ref, dcw_ref, dcb_ref, aext_ref, dcext_ref, da_ref):
        i = pl.program_id(0)

        @pl.when(i == 0)
        def _():
            dcw_ref[...] = jnp.zeros_like(dcw_ref)
            dcb_ref[...] = jnp.zeros_like(dcb_ref)

        ua = ua_ref[...]
        sb = _sigmoid(ub_ref[...])
        a_prev = uap_ref[...] * _sigmoid(ubp_ref[...])
        aext_ref[0:HALO, :] = jnp.where(i > 0, a_prev, 0.0)
        aext_ref[HALO:HALO + tm, :] = ua * sb
        dcext_ref[0:tm, :] = dc_ref[...]
        dcext_ref[tm:tm + HALO, :] = jnp.where(i < n_tiles - 1, dcn_ref[...], 0.0)
        dcb_ref[...] += jnp.sum(dc_ref[...], axis=0, keepdims=True)

        def row_block(r, carry):
            r0 = pl.multiple_of(r * CONV_ROWS, CONV_ROWS)
            n_rows = CONV_ROWS + HALO
            for cs in range(d // CONV_LANES):
                cl = slice(cs * CONV_LANES, (cs + 1) * CONV_LANES)
                dblk = dcext_ref[pl.ds(r0, n_rows), cl]
                ablk = aext_ref[pl.ds(r0, n_rows), cl]
                dcur = dblk[0:CONV_ROWS, :]
                acc = jnp.zeros((CONV_ROWS, CONV_LANES), F32)
                for b in range(8):
                    dsh = dblk if b == 0 else pltpu.roll(dblk, n_rows - b, axis=0)
                    ash = ablk if b == 0 else pltpu.roll(ablk, n_rows - b, axis=0)
                    for a in range(5):
                        j_da = CONV_WIDTH - 1 - (8 * a + b)
                        if 0 <= j_da < CONV_WIDTH:
                            acc = acc + cw_ref[j_da:j_da + 1, cl] * dsh[8 * a:8 * a + CONV_ROWS, :]
                        j_w = 8 * a + b - 2
                        if 0 <= j_w < CONV_WIDTH:
                            dcw_ref[j_w:j_w + 1, cl] += jnp.sum(
                                dcur * ash[8 * a:8 * a + CONV_ROWS, :], axis=0, keepdims=True)
                da_ref[pl.ds(r0, CONV_ROWS), cl] = acc
            return carry

        lax.fori_loop(0, tm // CONV_ROWS, row_block, 0)

        da = da_ref[...]
        dp_ref[:, 0:d] = (da * sb).astype(BF16)
        dp_ref[:, d:2 * d] = (da * ua * sb * (1.0 - sb)).astype(BF16)
        dp_ref[:, 2 * d:3 * d] = dz_ref[...]

    row = lambda p: pl.BlockSpec((tm, d), lambda i, p=p: (i, p))
    prev = lambda p: pl.BlockSpec((HALO, d), lambda i, p=p: (jnp.maximum(i * hb - 1, 0), p))
    nxt = pl.BlockSpec((HALO, d), lambda i: (jnp.minimum((i + 1) * hb, last_halo), 0))
    return pl.pallas_call(
        body, grid=(n_tiles,),
        in_specs=[row(0), nxt, row(0), row(1), prev(0), prev(1), pl.BlockSpec((HALO, d), lambda i: (0, 0)),
                  row(0), ANY],
        out_specs=[pl.BlockSpec((tm, 3 * d), lambda i: (i, 0)), pl.BlockSpec((HALO, d), lambda i: (0, 0)),
                   pl.BlockSpec((1, d), lambda i: (0, 0))],
        out_shape=[jax.ShapeDtypeStruct(dproj.shape, BF16),
                   jax.ShapeDtypeStruct((HALO, d), F32), jax.ShapeDtypeStruct((1, d), F32)],
        input_output_aliases={8: 0},
        scratch_shapes=[pltpu.VMEM((HALO + tm, d), F32), pltpu.VMEM((tm + HALO, d), F32), pltpu.VMEM((tm, d), F32)],
        name="conv_bwd", compiler_params=_params())(dc, dc, proj, proj, proj, proj, conv_w, dz, dproj)


def _weight_grad(xs, dy, name, blocked):
    lp, dx = xs.shape
    n = dy.shape[1]
    tk = _mm_row_tile(lp)
    if blocked:
        ncol = n // N_CHIPS
        nt = W_IN_COL_TILES
        tn = ncol // nt
        grid = (N_CHIPS * nt, lp // tk)
        out_spec = pl.BlockSpec((1, dx, tn), lambda c, k: (c // nt, 0, c % nt))
        out_shape = jax.ShapeDtypeStruct((N_CHIPS, dx, ncol), F32)
    else:
        tn = n // 2
        grid = (2, lp // tk)
        out_spec = pl.BlockSpec((dx, tn), lambda c, k: (0, c))
        out_shape = jax.ShapeDtypeStruct((dx, n), F32)

    def body(xs_ref, dy_ref, o_ref):
        @pl.when(pl.program_id(1) == 0)
        def _():
            o_ref[...] = jnp.zeros_like(o_ref)

        p = lax.dot_general(xs_ref[...], dy_ref[...], TN, preferred_element_type=F32)
        if blocked:
            o_ref[0] += p
        else:
            o_ref[...] += p

    return pl.pallas_call(
        body, grid=grid,
        in_specs=[pl.BlockSpec((tk, dx), lambda c, k: (k, 0)), pl.BlockSpec((tk, tn), lambda c, k: (k, c))],
        out_specs=out_spec, out_shape=out_shape,
        name=name, compiler_params=_params())(xs, dy)


def _in_proj_bwd(dproj, wtg, hres, norm_g, dout):
    lp, d = hres.shape
    _, ncol, _ = wtg.shape
    tm = _mm_row_tile(lp)
    nt = W_IN_COL_TILES
    tn = ncol // nt
    nk = N_CHIPS * nt

    def body(dp_ref, w_ref, x_ref, g_ref, dout_ref, dx_ref, dg_ref, acc_ref):
        i = pl.program_id(0)
        kk = pl.program_id(1)

        @pl.when(jnp.logical_and(i == 0, kk == 0))
        def _():
            dg_ref[...] = jnp.zeros_like(dg_ref)

        @pl.when(kk == 0)
        def _():
            acc_ref[...] = jnp.zeros_like(acc_ref)

        acc_ref[...] += jnp.dot(dp_ref[...], w_ref[0], preferred_element_type=F32)

        @pl.when(kk == nk - 1)
        def _():
            x = x_ref[...]
            r = lax.rsqrt(jnp.mean(x * x, axis=-1, keepdims=True) + EPS)
            xhat = x * r
            dh = acc_ref[...]
            dg_ref[...] += jnp.sum(dh * xhat, axis=0, keepdims=True)
            dxh = dh * g_ref[...]
            dx_ref[...] = dout_ref[...] + r * (dxh - xhat * jnp.mean(dxh * xhat, axis=-1, keepdims=True))

    return pl.pallas_call(
        body, grid=(lp // tm, nk),
        in_specs=[pl.BlockSpec((tm, tn), lambda i, k: (i, k)),
                  pl.BlockSpec((1, tn, d), lambda i, k: (k // nt, k % nt, 0)),
                  pl.BlockSpec((tm, d), lambda i, k: (i, 0)),
                  pl.BlockSpec((1, d), lambda i, k: (0, 0)),
                  pl.BlockSpec((tm, d), lambda i, k: (i, 0))],
        out_specs=[pl.BlockSpec((tm, d), lambda i, k: (i, 0)), pl.BlockSpec((1, d), lambda i, k: (0, 0))],
        out_shape=[jax.ShapeDtypeStruct((lp, d), F32), jax.ShapeDtypeStruct((1, d), F32)],
        scratch_shapes=[pltpu.VMEM((tm, d), F32)],
        name="in_proj_bwd", compiler_params=_params())(dproj, wtg, hres, norm_g, dout)


def _adamw_math(w, g, m, v):
    m = ADAM_B1 * m + (1.0 - ADAM_B1) * g
    v = ADAM_B2 * v + (1.0 - ADAM_B2) * (g * g)
    m_hat = m / (1.0 - ADAM_B1 ** ADAM_STEP)
    v_hat = v / (1.0 - ADAM_B2 ** ADAM_STEP)
    delta = -ADAM_LR * (m_hat / (jnp.sqrt(v_hat) + ADAM_EPS) + ADAM_WD * w)
    return delta, m, v


def _elementwise_rows(shape):
    r, c = shape
    for t in (256, 128, 64, 32, 16, 8):
        if r % t == 0 and r > t and t * c * 4 <= 1024 * 1024:
            return t
    return r


def _adamw(name, w, m, v, *g_parts):
    shape = w.shape
    tr = _elementwise_rows(shape)
    n_g = len(g_parts)

    def body(*refs):
        w_ref, m_ref, v_ref = refs[:3]
        g_refs = refs[3:3 + n_g]
        g_out, d_out, m_out, v_out = refs[3 + n_g:]
        g = g_refs[0][...]
        for gr in g_refs[1:]:
            g = g + gr[...]
        delta, m_new, v_new = _adamw_math(w_ref[...], g, m_ref[...], v_ref[...])
        g_out[...] = g
        d_out[...] = delta
        m_out[...] = m_new
        v_out[...] = v_new

    spec = pl.BlockSpec((tr, shape[1]), lambda i: (i, 0))
    return pl.pallas_call(
        body, grid=(shape[0] // tr,),
        in_specs=[spec] * (3 + n_g), out_specs=[spec] * 4,
        out_shape=[jax.ShapeDtypeStruct(shape, F32)] * 4,
        name=name, compiler_params=_params())(w, m, v, *g_parts)


def _chip_half_sum(name, g, recv, core):
    _, _, hr, cols = g.shape
    tr = _elementwise_rows((hr, cols))

    def body(core_ref, g_ref, r_ref, o_ref, ob_ref):
        s = g_ref[0, 0] + r_ref[0]
        o_ref[0] = s
        ob_ref[0] = s.astype(BF16)

    blk = pl.BlockSpec((1, tr, cols), lambda j, i, core_ref: (j, i, 0))
    grid_spec = pltpu.PrefetchScalarGridSpec(
        num_scalar_prefetch=1, grid=(N_CHIPS, hr // tr),
        in_specs=[pl.BlockSpec((1, 1, tr, cols), lambda j, i, core_ref: (j, core_ref[0], i, 0)), blk],
        out_specs=[blk, blk])
    return pl.pallas_call(
        body, grid_spec=grid_spec,
        out_shape=[jax.ShapeDtypeStruct((N_CHIPS, hr, cols), F32), jax.ShapeDtypeStruct((N_CHIPS, hr, cols), BF16)],
        name=name, compiler_params=_params())(core, g, recv)


def _block_half_total(name, chip_sums, recv, chip_core):
    _, hr, cols = chip_sums.shape
    tr = _elementwise_rows((hr, cols))

    def body(cc_ref, p_ref, r_ref, o_ref):
        s = p_ref[0]
        for k in range(3):
            s = s + r_ref[k].astype(F32)
        o_ref[0] = s

    grid_spec = pltpu.PrefetchScalarGridSpec(
        num_scalar_prefetch=1, grid=(hr // tr,),
        in_specs=[pl.BlockSpec((1, tr, cols), lambda i, cc_ref: (cc_ref[0], i, 0)),
                  pl.BlockSpec((3, tr, cols), lambda i, cc_ref: (0, i, 0))],
        out_specs=pl.BlockSpec((1, tr, cols), lambda i, cc_ref: (cc_ref[1], i, 0)))
    return pl.pallas_call(
        body, grid_spec=grid_spec, out_shape=jax.ShapeDtypeStruct((2, hr, cols), F32),
        name=name, compiler_params=_params())(chip_core, chip_sums, recv)


def _place_shard(name, w, chip, dtype):
    r, c = w.shape
    tr = _elementwise_rows((r, c))

    def body(chip_ref, w_ref, o_ref):
        o_ref[0] = w_ref[...].astype(dtype)

    grid_spec = pltpu.PrefetchScalarGridSpec(
        num_scalar_prefetch=1, grid=(r // tr,),
        in_specs=[pl.BlockSpec((tr, c), lambda i, chip_ref: (i, 0))],
        out_specs=pl.BlockSpec((1, tr, c), lambda i, chip_ref: (chip_ref[0], i, 0)))
    return pl.pallas_call(
        body, grid_spec=grid_spec, out_shape=jax.ShapeDtypeStruct((N_CHIPS, r, c), dtype),
        name=name, compiler_params=_params())(chip, w)


def _sum_slots(name, slots):
    k, r, c = slots.shape

    def body(s_ref, o_ref):
        s = s_ref[0]
        for j in range(1, k):
            s = s + s_ref[j]
        o_ref[...] = s

    return pl.pallas_call(body, out_shape=jax.ShapeDtypeStruct((r, c), F32), name=name,
                          compiler_params=_params())(slots)


def _mesh_pos():
    return lax.axis_index("x"), lax.axis_index("y"), lax.axis_index("c")


def _other_chips(x, y):
    return [(1 - x, y), (x, 1 - y), (1 - x, 1 - y)]


def _gather_weights(bufs):
    n = len(bufs)
    half = [b.shape[1] // 2 for b in bufs]

    def body(*refs):
        gathered = refs[n:2 * n]
        ici_send, ici_recv, d2d_send, d2d_recv = refs[2 * n:]
        x, y, c = _mesh_pos()
        me = 2 * x + y
        chips = _other_chips(x, y)

        def part(a, block, core):
            return gathered[a].at[block, pl.ds(core * half[a], half[a])]

        def over_ici(a, k, block):
            px, py = chips[k]
            return pltpu.make_async_remote_copy(
                src_ref=part(a, block, c), dst_ref=part(a, block, c),
                send_sem=ici_send.at[a, k], recv_sem=ici_recv.at[a, k],
                device_id=(px, py, c), device_id_type=MESH)

        def over_d2d(a, k, core):
            px, py = chips[k]
            return pltpu.make_async_remote_copy(
                src_ref=part(a, 2 * px + py, core), dst_ref=part(a, 2 * px + py, core),
                send_sem=d2d_send.at[a, k], recv_sem=d2d_recv.at[a, k],
                device_id=(x, y, 1 - c), device_id_type=MESH)

        for a in range(n):
            for k in range(3):
                over_ici(a, k, me).start()
        for a in range(n):
            for k, (px, py) in enumerate(chips):
                over_ici(a, k, 2 * px + py).wait_recv()
                over_d2d(a, k, c).start()
        for a in range(n):
            for k in range(3):
                over_d2d(a, k, 1 - c).wait_recv()
        for a in range(n):
            for k in range(3):
                over_ici(a, k, me).wait_send()
                over_d2d(a, k, c).wait_send()

    return pl.pallas_call(
        body, in_specs=[ANY] * n, out_specs=[ANY] * n,
        out_shape=[jax.ShapeDtypeStruct(b.shape, b.dtype) for b in bufs],
        input_output_aliases={a: a for a in range(n)},
        scratch_shapes=[pltpu.SemaphoreType.DMA((n, 3))] * 4,
        name="gather_weights")(*bufs)


def _gather_in_proj(h, bufs, order):
    n = len(bufs)
    half = [b.shape[1] // 2 for b in bufs]
    lp, d = h.shape
    ncol = bufs[0].shape[2]
    tm = _mm_row_tile(lp)
    n_row = lp // tm

    def body(order_ref, h_ref, *refs):
        gathered = refs[n:2 * n]
        o_ref = refs[2 * n]
        w_buf, ici_send, ici_recv, d2d_send, d2d_recv, w_sem = refs[2 * n + 1:]
        j = pl.program_id(0)
        i = pl.program_id(1)
        x, y, c = _mesh_pos()
        me = 2 * x + y
        chips = _other_chips(x, y)

        def part(a, block, core):
            return gathered[a].at[block, pl.ds(core * half[a], half[a])]

        def over_ici(a, k, block):
            px, py = chips[k]
            return pltpu.make_async_remote_copy(
                src_ref=part(a, block, c), dst_ref=part(a, block, c),
                send_sem=ici_send.at[a, k], recv_sem=ici_recv.at[a, k],
                device_id=(px, py, c), device_id_type=MESH)

        def over_d2d(a, k, core):
            px, py = chips[k]
            return pltpu.make_async_remote_copy(
                src_ref=part(a, 2 * px + py, core), dst_ref=part(a, 2 * px + py, core),
                send_sem=d2d_send.at[a, k], recv_sem=d2d_recv.at[a, k],
                device_id=(x, y, 1 - c), device_id_type=MESH)

        @pl.when(jnp.logical_and(j == 0, i == 0))
        def _():
            for a in range(n):
                for k in range(2):
                    over_ici(a, k, me).start()

        for k, (px, py) in enumerate(chips):
            @pl.when(jnp.logical_and(j == k + 1, i == 0))
            def _(k=k, px=px, py=py):
                for a in range(n):
                    over_ici(a, k, 2 * px + py).wait_recv()
                    over_d2d(a, k, c).start()
                if k == 0:
                    for a in range(n):
                        over_ici(a, 2, me).start()
                for a in range(n):
                    over_d2d(a, k, 1 - c).wait_recv()

        @pl.when(i == 0)
        def _():
            load = pltpu.make_async_copy(gathered[0].at[order_ref[j]], w_buf, w_sem)
            load.start()
            load.wait()

        o_ref[...] = jnp.dot(h_ref[...], w_buf[...], preferred_element_type=F32)

        @pl.when(jnp.logical_and(j == N_CHIPS - 1, i == n_row - 1))
        def _():
            for a in range(n):
                for k in range(3):
                    over_ici(a, k, me).wait_send()
                    over_d2d(a, k, c).wait_send()

    grid_spec = pltpu.PrefetchScalarGridSpec(
        num_scalar_prefetch=1, grid=(N_CHIPS, n_row),
        in_specs=[pl.BlockSpec((tm, d), lambda j, i, order_ref: (i, 0))] + [ANY] * n,
        out_specs=[ANY] * n + [pl.BlockSpec((tm, ncol), lambda j, i, order_ref: (i, order_ref[j]))],
        scratch_shapes=[pltpu.VMEM((d, ncol), BF16)] + [pltpu.SemaphoreType.DMA((n, 3))] * 4
        + [pltpu.SemaphoreType.DMA])
    out = pl.pallas_call(
        body, grid_spec=grid_spec,
        out_shape=[jax.ShapeDtypeStruct(b.shape, b.dtype) for b in bufs]
        + [jax.ShapeDtypeStruct((lp, N_CHIPS * ncol), F32)],
        input_output_aliases={2 + a: a for a in range(n)},
        name="gather_in_proj", compiler_params=_params())(order, h, *bufs)
    return out[n], out[:n]


def _send_other_halves(grads, tag):
    n = len(grads)

    def body(*refs):
        srcs = refs[:n]
        dsts = refs[n:2 * n]
        send_sems, recv_sems = refs[2 * n:]
        x, y, c = _mesh_pos()
        copies = [pltpu.make_async_remote_copy(
            src_ref=srcs[a].at[j, 1 - c], dst_ref=dsts[a].at[j], send_sem=send_sems.at[a, j],
            recv_sem=recv_sems.at[a, j], device_id=(x, y, 1 - c), device_id_type=MESH)
            for a in range(n) for j in range(N_CHIPS)]
        for cp in copies:
            cp.start()
        for cp in copies:
            cp.wait()

    return pl.pallas_call(
        body, in_specs=[ANY] * n, out_specs=[ANY] * n,
        out_shape=[jax.ShapeDtypeStruct((N_CHIPS,) + g.shape[2:], F32) for g in grads],
        scratch_shapes=[pltpu.SemaphoreType.DMA((n, N_CHIPS))] * 2,
        name="send_other_halves_" + tag)(*grads)


HBM = pl.BlockSpec(memory_space=pltpu.HBM)
SEM = pl.BlockSpec(memory_space=pltpu.SEMAPHORE)


def _block_copies(n, srcs, dsts, send_sems, recv_sems):
    x, y, c = _mesh_pos()
    return [pltpu.make_async_remote_copy(
        src_ref=srcs[a].at[2 * px + py], dst_ref=dsts[a].at[k], send_sem=send_sems.at[3 * a + k],
        recv_sem=recv_sems.at[3 * a + k], device_id=(px, py, c), device_id_type=MESH)
        for a in range(n) for k, (px, py) in enumerate(_other_chips(x, y))]


def _exchange_start(blocked, tag):
    n = len(blocked)
    lands = [lax.empty((3,) + b.shape[1:], b.dtype) for b in blocked]
    bufs = [pltpu.with_memory_space_constraint(b, pltpu.HBM) for b in list(blocked) + lands]
    nb = 2 * n

    def body(*refs):
        for cp in _block_copies(n, refs[:n], refs[n:nb], refs[nb], refs[nb + 1]):
            cp.start()
        refs[-1][...] = jnp.zeros_like(refs[-1])

    out = pl.pallas_call(
        body, name="exchange_start_" + tag,
        in_specs=[HBM] * nb,
        out_shape=[pltpu.SemaphoreType.DMA((3 * n,)), pltpu.SemaphoreType.DMA((3 * n,))]
        + [pltpu.HBM(b.shape, b.dtype) for b in bufs] + [jax.ShapeDtypeStruct((8, 128), F32)],
        out_specs=[SEM] * 2 + [HBM] * nb + [pl.BlockSpec(memory_space=pltpu.VMEM)],
        input_output_aliases={i: 2 + i for i in range(nb)},
        compiler_params=pltpu.CompilerParams(has_side_effects=pltpu.SideEffectType.DATAFLOW_SIDE_EFFECTING),
    )(*bufs)
    return (out[:2], out[2:2 + nb]), out[-1]


def _exchange_wait(state, after, tag):
    sems, bufs = state
    nb = len(bufs)
    n = nb // 2

    def body(*refs):
        for cp in _block_copies(n, refs[:n], refs[n:nb], refs[nb], refs[nb + 1]):
            cp.wait_send()
            cp.wait_recv()

    out = pl.pallas_call(
        body, name="exchange_wait_" + tag,
        in_specs=[HBM] * nb + [SEM] * 2 + [ANY],
        out_shape=[pltpu.HBM(b.shape, b.dtype) for b in bufs],
        out_specs=[HBM] * nb,
        input_output_aliases={i: i for i in range(nb)},
        compiler_params=pltpu.CompilerParams(has_side_effects=pltpu.SideEffectType.DATAFLOW_SIDE_EFFECTING),
    )(*bufs, *sems, after)
    return out[n:nb]


def _exchange_small(small):
    def body(small_src, small_dst, ssend_sems, srecv_sems, local_sem):
        x, y, c = _mesh_pos()
        my_idx = 4 * x + 2 * y + c
        local = pltpu.make_async_copy(small_src, small_dst.at[my_idx], local_sem)
        local.start()
        others = []
        for r in range(1, 8):
            px = 1 - x if r & 4 else x
            py = 1 - y if r & 2 else y
            pc = 1 - c if r & 1 else c
            others.append((px, py, pc))
        for r, peer in enumerate(others):
            pltpu.make_async_remote_copy(
                src_ref=small_src, dst_ref=small_dst.at[my_idx], send_sem=ssend_sems.at[r],
                recv_sem=srecv_sems.at[r], device_id=peer, device_id_type=MESH).start()
        for r, (px, py, pc) in enumerate(others):
            pltpu.make_async_remote_copy(
                src_ref=small_src, dst_ref=small_dst.at[4 * px + 2 * py + pc], send_sem=ssend_sems.at[r],
                recv_sem=srecv_sems.at[r], device_id=(px, py, pc), device_id_type=MESH).wait()
        local.wait()

    return pl.pallas_call(
        body, in_specs=[ANY], out_specs=ANY, out_shape=jax.ShapeDtypeStruct((8,) + small.shape, F32),
        scratch_shapes=[pltpu.SemaphoreType.DMA((7,)), pltpu.SemaphoreType.DMA((7,)), pltpu.SemaphoreType.DMA],
        name="exchange_small")(small)


def _join_halves(bufs):
    n = len(bufs)

    def body(*refs):
        joined = refs[n:2 * n]
        send_sems, recv_sems = refs[2 * n:]
        x, y, c = _mesh_pos()
        for a in range(n):
            pltpu.make_async_remote_copy(
                src_ref=joined[a].at[c], dst_ref=joined[a].at[c], send_sem=send_sems.at[a],
                recv_sem=recv_sems.at[a], device_id=(x, y, 1 - c), device_id_type=MESH).start()
        for a in range(n):
            pltpu.make_async_remote_copy(
                src_ref=joined[a].at[c], dst_ref=joined[a].at[1 - c], send_sem=send_sems.at[a],
                recv_sem=recv_sems.at[a], device_id=(x, y, 1 - c), device_id_type=MESH).wait()

    return pl.pallas_call(
        body, in_specs=[ANY] * n, out_specs=[ANY] * n,
        out_shape=[jax.ShapeDtypeStruct(b.shape, b.dtype) for b in bufs],
        input_output_aliases={a: a for a in range(n)},
        scratch_shapes=[pltpu.SemaphoreType.DMA((n,))] * 2,
        name="join_halves")(*bufs)


def kernel(x, meta_tokens, norm_g, w_in, conv_w, conv_b, ln_g, ln_b, w_conv_out, lb_logits, gnorm_g, w_rec_out, w_out, final_g, loss_target, m_meta_tokens, m_norm_g, m_w_in, m_conv_w, m_conv_b, m_ln_g, m_ln_b, m_w_conv_out, m_lb_logits, m_gnorm_g, m_w_rec_out, m_w_out, m_final_g, v_meta_tokens, v_norm_g, v_w_in, v_conv_w, v_conv_b, v_ln_g, v_ln_b, v_w_conv_out, v_lb_logits, v_gnorm_g, v_w_rec_out, v_w_out, v_final_g):
    seq, d = x.shape[1], x.shape[2]
    n_meta = meta_tokens.shape[0]
    n_pad = CHUNK - n_meta
    ds = d // N_CHIPS
    chip = 2 * lax.axis_index("x") + lax.axis_index("y")

    conv_w_pad = jnp.pad(conv_w[0], ((0, HALO - CONV_WIDTH), (0, 0)))
    chip_idx = chip.astype(jnp.int32).reshape(1)
    (small_g,) = _gather_weights([
        _place_shard("place_small", jnp.concatenate([conv_w_pad, meta_tokens], axis=0), chip_idx, F32)])
    cw_full = jnp.transpose(small_g[:, 0:HALO], (1, 0, 2)).reshape(HALO, d)
    meta_full = jnp.transpose(small_g[:, HALO:HALO + n_meta], (1, 0, 2)).reshape(n_meta, d)

    hres = jnp.concatenate([jnp.zeros((n_pad, d), F32), meta_full, x[0]], axis=0)
    target = jnp.pad(loss_target[0], ((CHUNK, 0), (0, 0)))
    final_g2 = final_g.reshape(1, d)
    h = _rmsnorm_fwd(hres, norm_g)
    fx, fy = 1 - lax.axis_index("x"), 1 - lax.axis_index("y")
    order = jnp.stack([chip, 2 * fx + (1 - fy), 2 * (1 - fx) + fy, 2 * fx + fy]).astype(jnp.int32)
    proj, (win_g, sq_g) = _gather_in_proj(h, [
        _place_shard("place_w_in", w_in[0], chip_idx, BF16),
        _place_shard("place_square", jnp.concatenate([w_conv_out[0], w_rec_out[0], w_out[0]], axis=0),
                     chip_idx, BF16)], order)
    wc_full = sq_g[:, 0:ds].reshape(d, d)
    wr_full = sq_g[:, ds:2 * ds].reshape(d, d)
    wo_full = sq_g[:, 2 * ds:3 * ds].reshape(d, d)
    c, yc_in, y_conv = _conv_fwd(proj, cw_full, conv_b, ln_g, ln_b, wc_full)
    o, s_all = _hgrn_fwd(proj, lb_logits, n_pad)
    yr_in, merged, y_rec, dout, loss_acc, dfinal_g = _tail_fwd(
        o, proj, y_conv, hres, target, gnorm_g, final_g2, wr_full, wo_full)

    (dyc, dyr, dout_bf, dz, dproj, do, dc, dgnorm_g, dln_g, dln_b) = _tail_bwd(
        dout, proj, y_conv, y_rec, o, c, wo_full, wr_full, wc_full, ln_g, ln_b, gnorm_g)
    g_wc = _weight_grad(yc_in, dyc, "grad_w_conv_out", False)
    g_wr = _weight_grad(yr_in, dyr, "grad_w_rec_out", False)
    g_wo = _weight_grad(merged, dout_bf, "grad_w_out", False)

    core = lax.axis_index("c").astype(jnp.int32).reshape(1)

    def chip_sum_and_start(g, tag):
        g = g.reshape(N_CHIPS, 2, g.shape[1] // 2, g.shape[2])
        (from_sibling,) = _send_other_halves([g], tag)
        sums = _chip_half_sum("chip_half_sum_" + tag, g, from_sibling, core)
        in_flight, token = _exchange_start([sums[1]], tag)
        return sums[0], in_flight, token[0:1, 0:1]

    g_sq = jnp.concatenate([g.reshape(N_CHIPS, ds, d) for g in (g_wc, g_wr, g_wo)], axis=1)
    sum_sq, flight_sq, token_sq = chip_sum_and_start(g_sq, "square")
    dproj, dlb_logits = _hgrn_bwd(proj, do, s_all, lb_logits + token_sq, n_pad, dproj)
    dproj, dconv_w, dconv_b = _conv_bwd(dc, proj, cw_full, dz, dproj)
    (recv_sq,) = _exchange_wait(flight_sq, dconv_b, "square")
    g_win = _weight_grad(h, dproj, "grad_w_in", True)
    sum_win, flight_win, token_win = chip_sum_and_start(g_win, "w_in")
    dhres, dnorm_g = _in_proj_bwd(dproj, jnp.transpose(win_g, (0, 2, 1)), hres, norm_g + token_win, dout)
    grad_x = dhres[CHUNK:][None]
    (recv_win,) = _exchange_wait(flight_win, dnorm_g, "w_in")
    small = jnp.concatenate([dnorm_g, dconv_b, dln_g, dln_b, dlb_logits, dgnorm_g, dfinal_g,
                             dhres[n_pad:CHUNK], dconv_w[:CONV_WIDTH],
                             jnp.zeros((1, d), F32)], axis=0)
    small_slots = _exchange_small(small)
    chip_core = jnp.concatenate([chip_idx, core])
    totals = [_block_half_total("block_half_total_" + nm, s, r, chip_core)
              for nm, s, r in zip(("w_in", "square"), (sum_win, sum_sq), (recv_win, recv_sq))]
    gt_win, gt_sq = [t.reshape(2 * t.shape[1], t.shape[2]) for t in _join_halves(totals)]
    small_sum = _sum_slots("sum_small", small_slots)

    res = {}
    res["w_in"] = _adamw("adamw_w_in", w_in[0], m_w_in[0], v_w_in[0], gt_win)
    res["w_conv_out"] = _adamw("adamw_w_conv_out", w_conv_out[0], m_w_conv_out[0], v_w_conv_out[0], gt_sq[0:ds])
    res["w_rec_out"] = _adamw("adamw_w_rec_out", w_rec_out[0], m_w_rec_out[0], v_w_rec_out[0], gt_sq[ds:2 * ds])
    res["w_out"] = _adamw("adamw_w_out", w_out[0], m_w_out[0], v_w_out[0], gt_sq[2 * ds:3 * ds])
    big = {k: tuple(a[None] for a in v) for k, v in res.items()}

    rep_names = ("norm_g", "conv_b", "ln_g", "ln_b", "lb_logits", "gnorm_g", "final_g")
    rep_w = (norm_g, conv_b, ln_g, ln_b, lb_logits, gnorm_g, final_g2)
    rep_m = (m_norm_g, m_conv_b, m_ln_g, m_ln_b, m_lb_logits, m_gnorm_g, m_final_g.reshape(1, d))
    rep_v = (v_norm_g, v_conv_b, v_ln_g, v_ln_b, v_lb_logits, v_gnorm_g, v_final_g.reshape(1, d))
    rep = _adamw("adamw_replicated", jnp.concatenate(rep_w, 0), jnp.concatenate(rep_m, 0),
                 jnp.concatenate(rep_v, 0), small_sum[0:8])
    rep_rows = {"norm_g": (0, 1), "conv_b": (1, 2), "ln_g": (2, 3), "ln_b": (3, 4), "lb_logits": (4, 6),
                "gnorm_g": (6, 7), "final_g": (7, 8)}
    small_out = {}
    for nm in rep_names:
        lo, hi = rep_rows[nm]
        vals = tuple(a[lo:hi] for a in rep)
        if nm == "final_g":
            vals = tuple(a.reshape(d) for a in vals)
        small_out[nm] = vals
    cw_row = 8 + n_meta
    g_meta = lax.dynamic_slice_in_dim(small_sum[8:cw_row], chip * ds, ds, axis=1)
    small_out["meta_tokens"] = _adamw("adamw_meta", meta_tokens, m_meta_tokens, v_meta_tokens, g_meta)
    g_cw = lax.dynamic_slice_in_dim(small_sum[cw_row:cw_row + HALO], chip * ds, ds, axis=1)
    pad_rows = ((0, HALO - CONV_WIDTH), (0, 0))
    cw_res = _adamw("adamw_conv_w", conv_w_pad, jnp.pad(m_conv_w[0], pad_rows),
                    jnp.pad(v_conv_w[0], pad_rows, constant_values=1.0), g_cw)
    small_out["conv_w"] = tuple(a[:CONV_WIDTH][None] for a in cw_res)

    loss = lax.psum(loss_acc[0, 0], ("x", "y", "c"))

    order = ("meta_tokens", "norm_g", "w_in", "conv_w", "conv_b", "ln_g", "ln_b", "w_conv_out", "lb_logits",
             "gnorm_g", "w_rec_out", "w_out", "final_g")
    allres = {**big, **small_out}
    outs = [loss, grad_x]
    for field in range(4):
        outs.extend(allres[nm][field] for nm in order)
    return tuple(outs)
```

```python
import numpy as np

import jax
import jax.numpy as jnp
from jax import lax
from jax.experimental import pallas as pl
from jax.experimental.pallas import tpu as pltpu

F32 = jnp.float32
BF16 = jnp.bfloat16

EPS = 1e-6
CHUNK = 64
N_LEVELS = 6
CONV_WIDTH = 31
HALO = 32
CONV_ROWS = 32
CONV_LANES = 256
HEAD = 128
W_IN_COL_TILES = 1
HEADS_PER_TRIP = 8
N_CHIPS = 4
VMEM_LIMIT_BYTES = 56 * 1024 * 1024

ADAM_LR = 0.001
ADAM_B1 = 0.9
ADAM_B2 = 0.999
ADAM_EPS = 1e-08
ADAM_WD = 0.01
ADAM_STEP = 10

MESH = pl.DeviceIdType.MESH
ANY = pl.BlockSpec(memory_space=pl.ANY)

NT = (((1,), (1,)), ((), ()))
TN = (((0,), (0,)), ((), ()))


def _params(**kw):
    return pltpu.CompilerParams(vmem_limit_bytes=VMEM_LIMIT_BYTES, **kw)


def _sigmoid(x):
    return jax.nn.sigmoid(x)


def _dsilu(x, s):
    return s * (1.0 + x * (1.0 - s))


def _row_tile(lp):
    for t in (320, 256, 192, 128, 64):
        if lp % t == 0:
            return t
    raise ValueError(f"unsupported padded length {lp}")


def _mm_row_tile(lp):
    for t in (832, 640, 320, 256, 192, 128, 64):
        if lp % t == 0:
            return t
    raise ValueError(f"unsupported padded length {lp}")


def _dot3(m_bf16, x):
    hi = x.astype(BF16)
    r1 = x - hi.astype(F32)
    mid = r1.astype(BF16)
    lo = (r1 - mid.astype(F32)).astype(BF16)
    return (jnp.dot(m_bf16, hi, preferred_element_type=F32)
            + jnp.dot(m_bf16, mid, preferred_element_type=F32)
            + jnp.dot(m_bf16, lo, preferred_element_type=F32))


def _dot2(m_bf16, x):
    hi = x.astype(BF16)
    lo = (x - hi.astype(F32)).astype(BF16)
    return (jnp.dot(m_bf16, hi, preferred_element_type=F32)
            + jnp.dot(m_bf16, lo, preferred_element_type=F32))


def _col_to_row(col):
    return jnp.broadcast_to(col, (HEAD, 8)).T[0:1, :]


def _row_to_col(row):
    return jnp.broadcast_to(row, (8, HEAD)).T[:, 0:1]


def _hgrn_tables():
    t = np.arange(CHUNK)
    ltri = (t[None, :] <= t[:, None]).astype(np.float32)
    mats = [ltri]
    for lvl in range(1, N_LEVELS + 1):
        blk = CHUNK >> (lvl - 1)
        mid = (t // blk) * blk + blk // 2
        mats.append(ltri[mid - 1])
    after = (t[None, :] >= t[:, None]).astype(np.float32)
    before = (t[None, :] < t[:, None]).astype(np.float32)
    return jnp.asarray(np.concatenate(mats, 0), BF16), jnp.asarray(np.concatenate([after, before], 1), BF16)


def _rmsnorm_fwd(hres, g):
    lp, d = hres.shape
    tm = _row_tile(lp)

    def body(x_ref, g_ref, h_ref):
        x = x_ref[...]
        r = lax.rsqrt(jnp.mean(x * x, axis=-1, keepdims=True) + EPS)
        h_ref[...] = (x * r * g_ref[...]).astype(BF16)

    return pl.pallas_call(
        body, grid=(lp // tm,),
        in_specs=[pl.BlockSpec((tm, d), lambda i: (i, 0)), pl.BlockSpec((1, d), lambda i: (0, 0))],
        out_specs=pl.BlockSpec((tm, d), lambda i: (i, 0)),
        out_shape=jax.ShapeDtypeStruct((lp, d), BF16),
        name="rmsnorm_fwd", compiler_params=_params())(hres, g)


def _in_proj(h, wg):
    lp, d = h.shape
    _, _, ncol = wg.shape
    tm = _mm_row_tile(lp)
    nt = W_IN_COL_TILES
    tn = ncol // nt

    def body(h_ref, w_ref, o_ref):
        o_ref[...] = jnp.dot(h_ref[...], w_ref[0], preferred_element_type=F32)

    return pl.pallas_call(
        body, grid=(N_CHIPS, nt, lp // tm),
        in_specs=[pl.BlockSpec((tm, d), lambda j, n, i: (i, 0)),
                  pl.BlockSpec((1, d, tn), lambda j, n, i: (j, 0, n))],
        out_specs=pl.BlockSpec((tm, tn), lambda j, n, i: (i, j * nt + n)),
        out_shape=jax.ShapeDtypeStruct((lp, N_CHIPS * ncol), F32),
        name="in_proj", compiler_params=_params())(h, wg)


def _conv_fwd(proj, conv_w, conv_b, ln_g, ln_b, w_conv):
    lp = proj.shape[0]
    d = conv_b.shape[1]
    tm = _row_tile(lp)
    hb = tm // HALO

    def body(ua_ref, ub_ref, z_ref, uap_ref, ubp_ref, cw_ref, cb_ref, lg_ref, lb_ref, w_ref,
             c_ref, ycin_ref, yconv_ref, aext_ref):
        i = pl.program_id(0)
        a_prev = uap_ref[...] * _sigmoid(ubp_ref[...])
        aext_ref[0:HALO, :] = jnp.where(i > 0, a_prev, 0.0)
        aext_ref[HALO:HALO + tm, :] = ua_ref[...] * _sigmoid(ub_ref[...])

        def row_block(r, carry):
            r0 = pl.multiple_of(r * CONV_ROWS, CONV_ROWS)
            for cs in range(d // CONV_LANES):
                cl = slice(cs * CONV_LANES, (cs + 1) * CONV_LANES)
                blk = aext_ref[pl.ds(r0, CONV_ROWS + HALO), cl]
                acc = jnp.zeros((CONV_ROWS, CONV_LANES), F32) + cb_ref[:, cl]
                for b in range(8):
                    sh = blk if b == 0 else pltpu.roll(blk, CONV_ROWS + HALO - b, axis=0)
                    for a in range(5):
                        j = 8 * a + b - 2
                        if 0 <= j < CONV_WIDTH:
                            acc = acc + cw_ref[j:j + 1, cl] * sh[8 * a:8 * a + CONV_ROWS, :]
                c_ref[pl.ds(r0, CONV_ROWS), cl] = acc
            return carry

        lax.fori_loop(0, tm // CONV_ROWS, row_block, 0)

        c = c_ref[...]
        mu = jnp.mean(c, axis=-1, keepdims=True)
        xc = c - mu
        rstd = lax.rsqrt(jnp.mean(xc * xc, axis=-1, keepdims=True) + EPS)
        ln = xc * rstd * lg_ref[...] + lb_ref[...]
        s = ln * _sigmoid(ln)
        z = z_ref[...]
        ycin = (s * (z * _sigmoid(z))).astype(BF16)
        ycin_ref[...] = ycin
        yconv_ref[...] = jnp.dot(ycin, w_ref[...], preferred_element_type=F32)

    row = lambda p: pl.BlockSpec((tm, d), lambda i, p=p: (i, p))
    halo = lambda p: pl.BlockSpec((HALO, d), lambda i, p=p: (jnp.maximum(i * hb - 1, 0), p))
    vec = pl.BlockSpec((1, d), lambda i: (0, 0))
    return pl.pallas_call(
        body, grid=(lp // tm,),
        in_specs=[row(0), row(1), row(2), halo(0), halo(1),
                  pl.BlockSpec((HALO, d), lambda i: (0, 0)), vec, vec, vec,
                  pl.BlockSpec((d, d), lambda i: (0, 0))],
        out_specs=[pl.BlockSpec((tm, d), lambda i: (i, 0))] * 3,
        out_shape=[jax.ShapeDtypeStruct((lp, d), F32), jax.ShapeDtypeStruct((lp, d), BF16),
                   jax.ShapeDtypeStruct((lp, d), F32)],
        scratch_shapes=[pltpu.VMEM((HALO + tm, d), F32)],
        name="conv_fwd", compiler_params=_params())(
            proj, proj, proj, proj, proj, conv_w, conv_b, ln_g, ln_b, w_conv)


def _lower_bound(lbl_ref):
    l0 = lbl_ref[0:1, :]
    l1 = lbl_ref[1:2, :]
    m = jnp.maximum(l0, l1)
    e0 = jnp.exp(l0 - m)
    e1 = jnp.exp(l1 - m)
    p0 = e0 / (e0 + e1)
    return p0, p0 * (e1 / (e0 + e1))


def _level_masks():
    r2 = lax.broadcasted_iota(jnp.int32, (CHUNK, CHUNK), 0)
    c2 = lax.broadcasted_iota(jnp.int32, (CHUNK, CHUNK), 1)
    out = []
    for lvl in range(1, N_LEVELS + 1):
        blk = CHUNK >> (lvl - 1)
        sh = blk.bit_length() - 1
        same = (r2 >> sh) == (c2 >> sh)
        t_upper = (r2 & (blk - 1)) >= (blk // 2)
        s_lower = (c2 & (blk - 1)) < (blk // 2)
        out.append(jnp.logical_and(same, jnp.logical_and(t_upper, s_lower)))
    return out


def _gates(qr, fr, lb, valid):
    sq = _sigmoid(qr)
    q = qr * sq
    sf = _sigmoid(fr)
    f = lb + (1.0 - lb) * sf
    g = jnp.where(valid, jnp.log(f), 0.0)
    k = jnp.where(valid, 1.0 - f, 0.0)
    return q, sq, f, sf, g, k


def _level_factor(b, r):
    d = b - r
    return jnp.exp(jnp.minimum(d, -d))


def _hgrn_fwd(proj, lb_logits, n_pad):
    lp = proj.shape[0]
    d = lb_logits.shape[1]
    n_heads = d // HEAD
    nc = lp // CHUNK
    tab, _ = _hgrn_tables()
    n_tab = tab.shape[0]

    def body(qr_ref, fr_ref, ir_ref, lbl_ref, tab_ref, o_ref, sall_ref, s_ref, t_ref):
        n = pl.program_id(0)

        @pl.when(n == 0)
        def _():
            s_ref[...] = jnp.zeros_like(s_ref)

        sall_ref[0] = s_ref[...]
        lb_all, _ = _lower_bound(lbl_ref)
        rid = lax.broadcasted_iota(jnp.int32, (CHUNK, 1), 0)
        valid = jnp.logical_or(n > 0, rid >= n_pad)
        f_all = lb_all + (1.0 - lb_all) * _sigmoid(fr_ref[...])
        t_ref[...] = _dot2(tab_ref[...], jnp.where(valid, jnp.log(f_all), 0.0))
        masks = _level_masks()

        def head(h):
            off = h * HEAD if isinstance(h, int) else pl.multiple_of(h * HEAD, HEAD)
            hs = pl.ds(off, HEAD)
            lb = _lower_bound_slice(lbl_ref, hs)
            q, _, _, _, _, k = _gates(qr_ref[:, hs], fr_ref[:, hs], lb, valid)
            v = ir_ref[:, hs]
            b = t_ref[0:CHUNK, hs]
            s0 = s_ref[hs, :]
            o = jnp.dot((q * jnp.exp(b)).astype(BF16), s0.astype(BF16), preferred_element_type=F32)
            o = o + jnp.sum(q * k, axis=-1, keepdims=True) * v
            a = jnp.zeros((CHUNK, CHUNK), F32)
            for lvl in range(1, N_LEVELS + 1):
                e = _level_factor(b, t_ref[CHUNK * lvl:CHUNK * (lvl + 1), hs])
                p = lax.dot_general((q * e).astype(BF16), (k * e).astype(BF16), NT, preferred_element_type=F32)
                a = a + jnp.where(masks[lvl - 1], p, 0.0)
            vb = v.astype(BF16)
            o_ref[:, hs] = o + jnp.dot(a.astype(BF16), vb, preferred_element_type=F32)
            b_last = t_ref[CHUNK - 1:CHUNK, hs]
            khat = (k * jnp.exp(b_last - b)).astype(BF16)
            s_ref[hs, :] = _row_to_col(jnp.exp(b_last)) * s0 + lax.dot_general(khat, vb, TN, preferred_element_type=F32)
        per_trip = min(HEADS_PER_TRIP, n_heads)

        def head_group(p, carry):
            for u in range(per_trip):
                head(p * per_trip + u)
            return carry

        if n_heads == per_trip:
            head_group(0, 0)
        else:
            lax.fori_loop(0, n_heads // per_trip, head_group, 0)

    piece = lambda p: pl.BlockSpec((CHUNK, d), lambda n, p=p: (n, p))
    return pl.pallas_call(
        body, grid=(nc,),
        in_specs=[piece(3), piece(4), piece(5), pl.BlockSpec((2, d), lambda n: (0, 0)),
                  pl.BlockSpec((n_tab, CHUNK), lambda n: (0, 0))],
        out_specs=[pl.BlockSpec((CHUNK, d), lambda n: (n, 0)), pl.BlockSpec((1, d, HEAD), lambda n: (n, 0, 0))],
        out_shape=[jax.ShapeDtypeStruct((lp, d), F32), jax.ShapeDtypeStruct((nc, d, HEAD), F32)],
        scratch_shapes=[pltpu.VMEM((d, HEAD), F32), pltpu.VMEM((n_tab, d), F32)],
        name="hgrn_fwd", compiler_params=_params())(proj, proj, proj, lb_logits, tab)


def _lower_bound_slice(lbl_ref, hs):
    l0 = lbl_ref[0:1, hs]
    l1 = lbl_ref[1:2, hs]
    m = jnp.maximum(l0, l1)
    e0 = jnp.exp(l0 - m)
    e1 = jnp.exp(l1 - m)
    return e0 / (e0 + e1)


def _tail_fwd(o, proj, y_conv, hres, target, gnorm_g, final_g, w_rec, w_out):
    lp, d = o.shape
    n_heads = d // HEAD
    tm = _row_tile(lp)

    def body(o_ref, gr_ref, mc_ref, mr_ref, yc_ref, x_ref, t_ref, gn_ref, fg_ref, wr_ref, wo_ref,
             yrin_ref, mg_ref, yrec_ref, dout_ref, loss_ref, dfg_ref):
        i = pl.program_id(0)

        @pl.when(i == 0)
        def _():
            loss_ref[...] = jnp.zeros_like(loss_ref)
            dfg_ref[...] = jnp.zeros_like(dfg_ref)

        for h in range(n_heads):
            hs = slice(h * HEAD, (h + 1) * HEAD)
            oh = o_ref[:, hs]
            on = oh * lax.rsqrt(jnp.mean(oh * oh, axis=-1, keepdims=True) + EPS) * gn_ref[:, hs]
            gr = gr_ref[:, hs]
            yrin_ref[:, hs] = (on * (gr * _sigmoid(gr))).astype(BF16)
        yrec = jnp.dot(yrin_ref[...], wr_ref[...], preferred_element_type=F32)
        yrec_ref[...] = yrec
        merged = (_sigmoid(mc_ref[...]) * yc_ref[...] + _sigmoid(mr_ref[...]) * yrec).astype(BF16)
        mg_ref[...] = merged
        out = x_ref[...] + jnp.dot(merged, wo_ref[...], preferred_element_type=F32)
        r = lax.rsqrt(jnp.mean(out * out, axis=-1, keepdims=True) + EPS)
        yhat = out * r
        fg = fg_ref[...]
        rid = lax.broadcasted_iota(jnp.int32, (tm, 1), 0) + i * tm
        err = jnp.where(rid >= CHUNK, yhat * fg - t_ref[...], 0.0)
        loss_ref[...] += 0.5 * jnp.sum(err * err) / d
        dy = err / d
        dfg_ref[...] += jnp.sum(dy * yhat, axis=0, keepdims=True)
        dyh = dy * fg
        dout_ref[...] = r * (dyh - yhat * jnp.mean(dyh * yhat, axis=-1, keepdims=True))

    row = lambda p: pl.BlockSpec((tm, d), lambda i, p=p: (i, p))
    vec = pl.BlockSpec((1, d), lambda i: (0, 0))
    mat = pl.BlockSpec((d, d), lambda i: (0, 0))
    return pl.pallas_call(
        body, grid=(lp // tm,),
        in_specs=[row(0), row(6), row(7), row(8), row(0), row(0), row(0), vec, vec, mat, mat],
        out_specs=[row(0), row(0), row(0), row(0), pl.BlockSpec((8, 128), lambda i: (0, 0)), vec],
        out_shape=[jax.ShapeDtypeStruct((lp, d), BF16), jax.ShapeDtypeStruct((lp, d), BF16),
                   jax.ShapeDtypeStruct((lp, d), F32), jax.ShapeDtypeStruct((lp, d), F32),
                   jax.ShapeDtypeStruct((8, 128), F32), jax.ShapeDtypeStruct((1, d), F32)],
        name="tail_fwd", compiler_params=_params())(
            o, proj, proj, proj, y_conv, hres, target, gnorm_g, final_g, w_rec, w_out)


def _tail_bwd(dout, proj, y_conv, y_rec, o, c, w_out, w_rec, w_conv, ln_g, ln_b, gnorm_g):
    lp, d = dout.shape
    n_heads = d // HEAD
    tm = _row_tile(lp)

    def body(dout_ref, mc_ref, mr_ref, z_ref, gr_ref, yc_ref, yrec_ref, o_ref, c_ref,
             wo_ref, wr_ref, wc_ref, lg_ref, lb_ref, gn_ref,
             dyc_ref, dyr_ref, doutb_ref, dz_ref, dp_ref, do_ref, dc_ref,
             dgn_ref, dlg_ref, dlb_ref, dyrin_ref):
        i = pl.program_id(0)

        @pl.when(i == 0)
        def _():
            dgn_ref[...] = jnp.zeros_like(dgn_ref)
            dlg_ref[...] = jnp.zeros_like(dlg_ref)
            dlb_ref[...] = jnp.zeros_like(dlb_ref)

        doutb = dout_ref[...].astype(BF16)
        doutb_ref[...] = doutb
        dmerged = lax.dot_general(doutb, wo_ref[...], NT, preferred_element_type=F32)
        smc = _sigmoid(mc_ref[...])
        smr = _sigmoid(mr_ref[...])
        dyc = (dmerged * smc).astype(BF16)
        dyr = (dmerged * smr).astype(BF16)
        dyc_ref[...] = dyc
        dyr_ref[...] = dyr
        dp_ref[:, d:2 * d] = (dmerged * yc_ref[...] * smc * (1.0 - smc)).astype(BF16)
        dp_ref[:, 2 * d:3 * d] = (dmerged * yrec_ref[...] * smr * (1.0 - smr)).astype(BF16)

        dyrin_ref[...] = lax.dot_general(dyr, wr_ref[...], NT, preferred_element_type=F32)
        for h in range(n_heads):
            hs = slice(h * HEAD, (h + 1) * HEAD)
            oh = o_ref[:, hs]
            rstd = lax.rsqrt(jnp.mean(oh * oh, axis=-1, keepdims=True) + EPS)
            ohat = oh * rstd
            gn = gn_ref[:, hs]
            gr = gr_ref[:, hs]
            sg = _sigmoid(gr)
            dyrin = dyrin_ref[:, hs]
            don = dyrin * (gr * sg)
            dp_ref[:, hs] = (dyrin * (ohat * gn) * _dsilu(gr, sg)).astype(BF16)
            dgn_ref[:, hs] += jnp.sum(don * ohat, axis=0, keepdims=True)
            doh = don * gn
            do_ref[:, hs] = rstd * (doh - ohat * jnp.mean(doh * ohat, axis=-1, keepdims=True))

        dycin = lax.dot_general(dyc, wc_ref[...], NT, preferred_element_type=F32)
        c = c_ref[...]
        mu = jnp.mean(c, axis=-1, keepdims=True)
        xc = c - mu
        rstd = lax.rsqrt(jnp.mean(xc * xc, axis=-1, keepdims=True) + EPS)
        nrm = xc * rstd
        lg = lg_ref[...]
        ln = nrm * lg + lb_ref[...]
        sl = _sigmoid(ln)
        z = z_ref[...]
        sz = _sigmoid(z)
        dz_ref[...] = (dycin * (ln * sl) * _dsilu(z, sz)).astype(BF16)
        dln = dycin * (z * sz) * _dsilu(ln, sl)
        dlg_ref[...] += jnp.sum(dln * nrm, axis=0, keepdims=True)
        dlb_ref[...] += jnp.sum(dln, axis=0, keepdims=True)
        dn = dln * lg
        dc_ref[...] = rstd * (dn - jnp.mean(dn, axis=-1, keepdims=True)
                              - nrm * jnp.mean(dn * nrm, axis=-1, keepdims=True))

    row = lambda p: pl.BlockSpec((tm, d), lambda i, p=p: (i, p))
    vec = pl.BlockSpec((1, d), lambda i: (0, 0))
    mat = pl.BlockSpec((d, d), lambda i: (0, 0))
    act_bf = jax.ShapeDtypeStruct((lp, d), BF16)
    act_f32 = jax.ShapeDtypeStruct((lp, d), F32)
    vec_f32 = jax.ShapeDtypeStruct((1, d), F32)
    return pl.pallas_call(
        body, grid=(lp // tm,),
        in_specs=[row(0), row(7), row(8), row(2), row(6), row(0), row(0), row(0), row(0),
                  mat, mat, mat, vec, vec, vec],
        out_specs=[row(0)] * 4 + [pl.BlockSpec((tm, 3 * d), lambda i: (i, 2))] + [row(0)] * 2 + [vec] * 3,
        out_shape=[act_bf] * 4 + [jax.ShapeDtypeStruct((lp, 9 * d), BF16)] + [act_f32] * 2 + [vec_f32] * 3,
        scratch_shapes=[pltpu.VMEM((tm, d), F32)],
        name="tail_bwd", compiler_params=_params())(
            dout, proj, proj, proj, proj, y_conv, y_rec, o, c, w_out, w_rec, w_conv, ln_g, ln_b, gnorm_g)


def _hgrn_bwd(proj, do, s_all, lb_logits, n_pad, dproj):
    lp, d = do.shape
    n_heads = d // HEAD
    nc = lp // CHUNK
    tab, utri = _hgrn_tables()
    n_tab = tab.shape[0]

    def body(qr_ref, fr_ref, ir_ref, do_ref, s0_ref, lbl_ref, tab_ref, ut_ref, _,
             dp_ref, dlbl_ref, ds_ref, t_ref, dlb_ref):
        n = pl.program_id(0)
        chunk = nc - 1 - n

        @pl.when(n == 0)
        def _():
            ds_ref[...] = jnp.zeros_like(ds_ref)
            dlb_ref[...] = jnp.zeros_like(dlb_ref)

        lb_all, pp = _lower_bound(lbl_ref)
        rid = lax.broadcasted_iota(jnp.int32, (CHUNK, 1), 0)
        valid = jnp.logical_or(chunk > 0, rid >= n_pad)
        f_all = lb_all + (1.0 - lb_all) * _sigmoid(fr_ref[...])
        t_ref[...] = _dot2(tab_ref[...], jnp.where(valid, jnp.log(f_all), 0.0))
        masks = _level_masks()
        ut = ut_ref[...]

        def head(h):
            off = h * HEAD if isinstance(h, int) else pl.multiple_of(h * HEAD, HEAD)
            hs = pl.ds(off, HEAD)
            lb = _lower_bound_slice(lbl_ref, hs)
            qr = qr_ref[:, hs]
            q, sq, f, sf, _, k = _gates(qr, fr_ref[:, hs], lb, valid)
            v = ir_ref[:, hs]
            do_h = do_ref[:, hs]
            b = t_ref[0:CHUNK, hs]
            b_last = t_ref[CHUNK - 1:CHUNK, hs]
            s0 = s0_ref[0, hs, :]
            ds1 = ds_ref[hs, :]
            eb = jnp.exp(b)
            ekl = jnp.exp(b_last - b)
            do_bf = do_h.astype(BF16)
            v_bf = v.astype(BF16)
            ds1_bf = ds1.astype(BF16)

            da = lax.dot_general(do_bf, v_bf, NT, preferred_element_type=F32)
            da_diag = jnp.sum(do_h * v, axis=-1, keepdims=True)
            a = jnp.zeros((CHUNK, CHUNK), F32)
            dq_x = eb * lax.dot_general(do_bf, s0.astype(BF16), NT, preferred_element_type=F32)
            dk_x = ekl * lax.dot_general(v_bf, ds1_bf, NT, preferred_element_type=F32)
            x_after = q * dq_x
            x_before = k * dk_x
            for lvl in range(1, N_LEVELS + 1):
                e = _level_factor(b, t_ref[CHUNK * lvl:CHUNK * (lvl + 1), hs])
                qt = (q * e).astype(BF16)
                kt = (k * e).astype(BF16)
                p = lax.dot_general(qt, kt, NT, preferred_element_type=F32)
                a = a + jnp.where(masks[lvl - 1], p, 0.0)
                dam = jnp.where(masks[lvl - 1], da, 0.0).astype(BF16)
                dqt = jnp.dot(dam, kt, preferred_element_type=F32)
                dkt = lax.dot_general(dam, qt, TN, preferred_element_type=F32)
                dq_x = dq_x + e * dqt
                dk_x = dk_x + e * dkt
                x_after = x_after + (qt.astype(F32) * dqt - kt.astype(F32) * dkt)

            dv = (lax.dot_general(a.astype(BF16), do_bf, TN, preferred_element_type=F32)
                  + jnp.sum(q * k, axis=-1, keepdims=True) * do_h
                  + jnp.dot((k * ekl).astype(BF16), ds1_bf, preferred_element_type=F32))
            dp_ref[:, pl.ds(2 * d + off, HEAD)] = dv.astype(BF16)

            carried = jnp.exp(b_last) * _col_to_row(jnp.sum(s0 * ds1, axis=-1, keepdims=True))
            dg = _dot3(ut, jnp.concatenate([x_after, x_before], axis=0)) + carried
            dq = dq_x + da_diag * k
            dk = dk_x + da_diag * q
            dp_ref[:, hs] = (dq * _dsilu(qr, sq)).astype(BF16)
            df = jnp.where(valid, dg / f - dk, 0.0)
            dp_ref[:, pl.ds(d + off, HEAD)] = (df * (1.0 - lb) * sf * (1.0 - sf)).astype(BF16)
            dlb_ref[:, hs] += jnp.sum(df * (1.0 - sf), axis=0, keepdims=True)

            ds_ref[hs, :] = (_row_to_col(jnp.exp(b_last)) * ds1
                             + lax.dot_general((q * eb).astype(BF16), do_bf, TN, preferred_element_type=F32))
        per_trip = min(HEADS_PER_TRIP, n_heads)

        def head_group(p, carry):
            for u in range(per_trip):
                head(p * per_trip + u)
            return carry

        if n_heads == per_trip:
            head_group(0, 0)
        else:
            lax.fori_loop(0, n_heads // per_trip, head_group, 0)

        @pl.when(n == nc - 1)
        def _():
            dl0 = dlb_ref[...] * pp
            dlbl_ref[0:1, :] = dl0
            dlbl_ref[1:2, :] = -dl0

    piece = lambda p: pl.BlockSpec((CHUNK, d), lambda n, p=p: (nc - 1 - n, p))
    return pl.pallas_call(
        body, grid=(nc,),
        in_specs=[piece(3), piece(4), piece(5), piece(0),
                  pl.BlockSpec((1, d, HEAD), lambda n: (nc - 1 - n, 0, 0)),
                  pl.BlockSpec((2, d), lambda n: (0, 0)),
                  pl.BlockSpec((n_tab, CHUNK), lambda n: (0, 0)),
                  pl.BlockSpec((CHUNK, 2 * CHUNK), lambda n: (0, 0)), ANY],
        out_specs=[pl.BlockSpec((CHUNK, 3 * d), lambda n: (nc - 1 - n, 1)), pl.BlockSpec((2, d), lambda n: (0, 0))],
        out_shape=[jax.ShapeDtypeStruct(dproj.shape, BF16), jax.ShapeDtypeStruct((2, d), F32)],
        input_output_aliases={8: 0},
        scratch_shapes=[pltpu.VMEM((d, HEAD), F32), pltpu.VMEM((n_tab, d), F32), pltpu.VMEM((1, d), F32)],
        name="hgrn_bwd", compiler_params=_params())(proj, proj, proj, do, s_all, lb_logits, tab, utri, dproj)


def _conv_bwd(dc, proj, conv_w, dz, dproj):
    lp, d = dc.shape
    tm = _row_tile(lp)
    hb = tm // HALO
    n_tiles = lp // tm
    last_halo = lp // HALO - 1

    def body(dc_ref, dcn_ref, ua_ref, ub_ref, uap_ref, ubp_ref, cw_ref, dz_ref, _,
             dp_ref, dcw_ref, dcb_ref, aext_ref, dcext_ref, da_ref, dcw_acc):
        i = pl.program_id(0)

        @pl.when(i == 0)
        def _():
            dcw_acc[...] = jnp.zeros_like(dcw_acc)
            dcb_ref[...] = jnp.zeros_like(dcb_ref)

        ua = ua_ref[...]
        sb = _sigmoid(ub_ref[...])
        a_prev = uap_ref[...] * _sigmoid(ubp_ref[...])
        aext_ref[0:HALO, :] = jnp.where(i > 0, a_prev, 0.0)
        aext_ref[HALO:HALO + tm, :] = ua * sb
        dcext_ref[0:tm, :] = dc_ref[...]
        dcext_ref[tm:tm + HALO, :] = jnp.where(i < n_tiles - 1, dcn_ref[...], 0.0)
        dcb_ref[...] += jnp.sum(dc_ref[...], axis=0, keepdims=True)

        def row_block(r, carry):
            r0 = pl.multiple_of(r * CONV_ROWS, CONV_ROWS)
            n_rows = CONV_ROWS + HALO
            for cs in range(d // CONV_LANES):
                cl = slice(cs * CONV_LANES, (cs + 1) * CONV_LANES)
                dblk = dcext_ref[pl.ds(r0, n_rows), cl]
                ablk = aext_ref[pl.ds(r0, n_rows), cl]
                dcur = dblk[0:CONV_ROWS, :]
                acc = jnp.zeros((CONV_ROWS, CONV_LANES), F32)
                for b in range(8):
                    dsh = dblk if b == 0 else pltpu.roll(dblk, n_rows - b, axis=0)
                    ash = ablk if b == 0 else pltpu.roll(ablk, n_rows - b, axis=0)
                    for a in range(5):
                        j_da = CONV_WIDTH - 1 - (8 * a + b)
                        if 0 <= j_da < CONV_WIDTH:
                            acc = acc + cw_ref[j_da:j_da + 1, cl] * dsh[8 * a:8 * a + CONV_ROWS, :]
                        j_w = 8 * a + b - 2
                        if 0 <= j_w < CONV_WIDTH:
                            prod = dcur * ash[8 * a:8 * a + CONV_ROWS, :]
                            dcw_acc[j_w, :, cl] += prod.reshape(CONV_ROWS // 8, 8, CONV_LANES).sum(axis=0)
                da_ref[pl.ds(r0, CONV_ROWS), cl] = acc
            return carry

        lax.fori_loop(0, tm // CONV_ROWS, row_block, 0)

        da = da_ref[...]
        dp_ref[:, 0:d] = (da * sb).astype(BF16)
        dp_ref[:, d:2 * d] = (da * ua * sb * (1.0 - sb)).astype(BF16)
        dp_ref[:, 2 * d:3 * d] = dz_ref[...]

        @pl.when(i == n_tiles - 1)
        def _():
            dcw_ref[...] = jnp.sum(dcw_acc[...], axis=1)

    row = lambda p: pl.BlockSpec((tm, d), lambda i, p=p: (i, p))
    prev = lambda p: pl.BlockSpec((HALO, d), lambda i, p=p: (jnp.maximum(i * hb - 1, 0), p))
    nxt = pl.BlockSpec((HALO, d), lambda i: (jnp.minimum((i + 1) * hb, last_halo), 0))
    return pl.pallas_call(
        body, grid=(n_tiles,),
        in_specs=[row(0), nxt, row(0), row(1), prev(0), prev(1), pl.BlockSpec((HALO, d), lambda i: (0, 0)),
                  row(0), ANY],
        out_specs=[pl.BlockSpec((tm, 3 * d), lambda i: (i, 0)), pl.BlockSpec((HALO, d), lambda i: (0, 0)),
                   pl.BlockSpec((1, d), lambda i: (0, 0))],
        out_shape=[jax.ShapeDtypeStruct(dproj.shape, BF16),
                   jax.ShapeDtypeStruct((HALO, d), F32), jax.ShapeDtypeStruct((1, d), F32)],
        input_output_aliases={8: 0},
        scratch_shapes=[pltpu.VMEM((HALO + tm, d), F32), pltpu.VMEM((tm + HALO, d), F32), pltpu.VMEM((tm, d), F32),
                        pltpu.VMEM((HALO, 8, d), F32)],
        name="conv_bwd", compiler_params=_params())(dc, dc, proj, proj, proj, proj, conv_w, dz, dproj)


def _weight_grad(xs, dy, name, blocked):
    lp, dx = xs.shape
    n = dy.shape[1]
    tk = _mm_row_tile(lp)
    if blocked:
        ncol = n // N_CHIPS
        nt = W_IN_COL_TILES
        tn = ncol // nt
        grid = (N_CHIPS * nt, lp // tk)
        out_spec = pl.BlockSpec((1, dx, tn), lambda c, k: (c // nt, 0, c % nt))
        out_shape = jax.ShapeDtypeStruct((N_CHIPS, dx, ncol), F32)
    else:
        tn = n // 2
        grid = (2, lp // tk)
        out_spec = pl.BlockSpec((dx, tn), lambda c, k: (0, c))
        out_shape = jax.ShapeDtypeStruct((dx, n), F32)

    def body(xs_ref, dy_ref, o_ref):
        @pl.when(pl.program_id(1) == 0)
        def _():
            o_ref[...] = jnp.zeros_like(o_ref)

        p = lax.dot_general(xs_ref[...], dy_ref[...], TN, preferred_element_type=F32)
        if blocked:
            o_ref[0] += p
        else:
            o_ref[...] += p

    return pl.pallas_call(
        body, grid=grid,
        in_specs=[pl.BlockSpec((tk, dx), lambda c, k: (k, 0)), pl.BlockSpec((tk, tn), lambda c, k: (k, c))],
        out_specs=out_spec, out_shape=out_shape,
        name=name, compiler_params=_params())(xs, dy)


def _in_proj_bwd(dproj, wtg, hres, norm_g, dout):
    lp, d = hres.shape
    _, ncol, _ = wtg.shape
    tm = _mm_row_tile(lp)
    nt = W_IN_COL_TILES
    tn = ncol // nt
    nk = N_CHIPS * nt

    def body(dp_ref, w_ref, x_ref, g_ref, dout_ref, dx_ref, dg_ref, acc_ref):
        i = pl.program_id(0)
        kk = pl.program_id(1)

        @pl.when(jnp.logical_and(i == 0, kk == 0))
        def _():
            dg_ref[...] = jnp.zeros_like(dg_ref)

        @pl.when(kk == 0)
        def _():
            acc_ref[...] = jnp.zeros_like(acc_ref)

        acc_ref[...] += jnp.dot(dp_ref[...], w_ref[0], preferred_element_type=F32)

        @pl.when(kk == nk - 1)
        def _():
            x = x_ref[...]
            r = lax.rsqrt(jnp.mean(x * x, axis=-1, keepdims=True) + EPS)
            xhat = x * r
            dh = acc_ref[...]
            dg_ref[...] += jnp.sum(dh * xhat, axis=0, keepdims=True)
            dxh = dh * g_ref[...]
            dx_ref[...] = dout_ref[...] + r * (dxh - xhat * jnp.mean(dxh * xhat, axis=-1, keepdims=True))

    return pl.pallas_call(
        body, grid=(lp // tm, nk),
        in_specs=[pl.BlockSpec((tm, tn), lambda i, k: (i, k)),
                  pl.BlockSpec((1, tn, d), lambda i, k: (k // nt, k % nt, 0)),
                  pl.BlockSpec((tm, d), lambda i, k: (i, 0)),
                  pl.BlockSpec((1, d), lambda i, k: (0, 0)),
                  pl.BlockSpec((tm, d), lambda i, k: (i, 0))],
        out_specs=[pl.BlockSpec((tm, d), lambda i, k: (i, 0)), pl.BlockSpec((1, d), lambda i, k: (0, 0))],
        out_shape=[jax.ShapeDtypeStruct((lp, d), F32), jax.ShapeDtypeStruct((1, d), F32)],
        scratch_shapes=[pltpu.VMEM((tm, d), F32)],
        name="in_proj_bwd", compiler_params=_params())(dproj, wtg, hres, norm_g, dout)


def _adamw_math(w, g, m, v):
    m = ADAM_B1 * m + (1.0 - ADAM_B1) * g
    v = ADAM_B2 * v + (1.0 - ADAM_B2) * (g * g)
    m_hat = m / (1.0 - ADAM_B1 ** ADAM_STEP)
    v_hat = v / (1.0 - ADAM_B2 ** ADAM_STEP)
    delta = -ADAM_LR * (m_hat / (jnp.sqrt(v_hat) + ADAM_EPS) + ADAM_WD * w)
    return delta, m, v


def _elementwise_rows(shape):
    r, c = shape
    for t in (256, 128, 64, 32, 16, 8):
        if r % t == 0 and r > t and t * c * 4 <= 1024 * 1024:
            return t
    return r


def _adamw(name, w, m, v, *g_parts):
    shape = w.shape
    tr = _elementwise_rows(shape)
    n_g = len(g_parts)

    def body(*refs):
        w_ref, m_ref, v_ref = refs[:3]
        g_refs = refs[3:3 + n_g]
        g_out, d_out, m_out, v_out = refs[3 + n_g:]
        g = g_refs[0][...]
        for gr in g_refs[1:]:
            g = g + gr[...]
        delta, m_new, v_new = _adamw_math(w_ref[...], g, m_ref[...], v_ref[...])
        g_out[...] = g
        d_out[...] = delta
        m_out[...] = m_new
        v_out[...] = v_new

    spec = pl.BlockSpec((tr, shape[1]), lambda i: (i, 0))
    return pl.pallas_call(
        body, grid=(shape[0] // tr,),
        in_specs=[spec] * (3 + n_g), out_specs=[spec] * 4,
        out_shape=[jax.ShapeDtypeStruct(shape, F32)] * 4,
        name=name, compiler_params=_params())(w, m, v, *g_parts)


def _chip_half_sum(name, g, recv, core):
    _, _, hr, cols = g.shape
    tr = _elementwise_rows((hr, cols))

    def body(core_ref, g_ref, r_ref, o_ref, ob_ref):
        s = g_ref[0, 0] + r_ref[0]
        o_ref[0] = s
        ob_ref[0] = s.astype(BF16)

    blk = pl.BlockSpec((1, tr, cols), lambda j, i, core_ref: (j, i, 0))
    grid_spec = pltpu.PrefetchScalarGridSpec(
        num_scalar_prefetch=1, grid=(N_CHIPS, hr // tr),
        in_specs=[pl.BlockSpec((1, 1, tr, cols), lambda j, i, core_ref: (j, core_ref[0], i, 0)), blk],
        out_specs=[blk, blk])
    return pl.pallas_call(
        body, grid_spec=grid_spec,
        out_shape=[jax.ShapeDtypeStruct((N_CHIPS, hr, cols), F32), jax.ShapeDtypeStruct((N_CHIPS, hr, cols), BF16)],
        name=name, compiler_params=_params())(core, g, recv)


def _block_half_total(name, chip_sums, recv, chip_core):
    _, hr, cols = chip_sums.shape
    tr = _elementwise_rows((hr, cols))

    def body(cc_ref, p_ref, r_ref, o_ref):
        s = p_ref[0]
        for k in range(3):
            s = s + r_ref[k].astype(F32)
        o_ref[0] = s

    grid_spec = pltpu.PrefetchScalarGridSpec(
        num_scalar_prefetch=1, grid=(hr // tr,),
        in_specs=[pl.BlockSpec((1, tr, cols), lambda i, cc_ref: (cc_ref[0], i, 0)),
                  pl.BlockSpec((3, tr, cols), lambda i, cc_ref: (0, i, 0))],
        out_specs=pl.BlockSpec((1, tr, cols), lambda i, cc_ref: (cc_ref[1], i, 0)))
    return pl.pallas_call(
        body, grid_spec=grid_spec, out_shape=jax.ShapeDtypeStruct((2, hr, cols), F32),
        name=name, compiler_params=_params())(chip_core, chip_sums, recv)


def _place_shard(name, w, chip, dtype):
    r, c = w.shape
    tr = _elementwise_rows((r, c))

    def body(chip_ref, w_ref, o_ref):
        o_ref[0] = w_ref[...].astype(dtype)

    grid_spec = pltpu.PrefetchScalarGridSpec(
        num_scalar_prefetch=1, grid=(r // tr,),
        in_specs=[pl.BlockSpec((tr, c), lambda i, chip_ref: (i, 0))],
        out_specs=pl.BlockSpec((1, tr, c), lambda i, chip_ref: (chip_ref[0], i, 0)))
    return pl.pallas_call(
        body, grid_spec=grid_spec, out_shape=jax.ShapeDtypeStruct((N_CHIPS, r, c), dtype),
        name=name, compiler_params=_params())(chip, w)


def _sum_slots(name, slots):
    k, r, c = slots.shape

    def body(s_ref, o_ref):
        s = s_ref[0]
        for j in range(1, k):
            s = s + s_ref[j]
        o_ref[...] = s

    return pl.pallas_call(body, out_shape=jax.ShapeDtypeStruct((r, c), F32), name=name,
                          compiler_params=_params())(slots)


def _mesh_pos():
    return lax.axis_index("x"), lax.axis_index("y"), lax.axis_index("c")


def _other_chips(x, y):
    return [(1 - x, y), (x, 1 - y), (1 - x, 1 - y)]


def _gather_weights(bufs):
    n = len(bufs)
    half = [b.shape[1] // 2 for b in bufs]

    def body(*refs):
        gathered = refs[n:2 * n]
        ici_send, ici_recv, d2d_send, d2d_recv = refs[2 * n:]
        x, y, c = _mesh_pos()
        me = 2 * x + y
        chips = _other_chips(x, y)

        def part(a, block, core):
            return gathered[a].at[block, pl.ds(core * half[a], half[a])]

        def over_ici(a, k, block):
            px, py = chips[k]
            return pltpu.make_async_remote_copy(
                src_ref=part(a, block, c), dst_ref=part(a, block, c),
                send_sem=ici_send.at[a, k], recv_sem=ici_recv.at[a, k],
                device_id=(px, py, c), device_id_type=MESH)

        def over_d2d(a, k, core):
            px, py = chips[k]
            return pltpu.make_async_remote_copy(
                src_ref=part(a, 2 * px + py, core), dst_ref=part(a, 2 * px + py, core),
                send_sem=d2d_send.at[a, k], recv_sem=d2d_recv.at[a, k],
                device_id=(x, y, 1 - c), device_id_type=MESH)

        for a in range(n):
            for k in range(3):
                over_ici(a, k, me).start()
        for a in range(n):
            for k, (px, py) in enumerate(chips):
                over_ici(a, k, 2 * px + py).wait_recv()
                over_d2d(a, k, c).start()
        for a in range(n):
            for k in range(3):
                over_d2d(a, k, 1 - c).wait_recv()
        for a in range(n):
            for k in range(3):
                over_ici(a, k, me).wait_send()
                over_d2d(a, k, c).wait_send()

    return pl.pallas_call(
        body, in_specs=[ANY] * n, out_specs=[ANY] * n,
        out_shape=[jax.ShapeDtypeStruct(b.shape, b.dtype) for b in bufs],
        input_output_aliases={a: a for a in range(n)},
        scratch_shapes=[pltpu.SemaphoreType.DMA((n, 3))] * 4,
        name="gather_weights")(*bufs)


def _gather_in_proj(h, bufs, order):
    n = len(bufs)
    half = [b.shape[1] // 2 for b in bufs]
    lp, d = h.shape
    ncol = bufs[0].shape[2]
    tm = _mm_row_tile(lp)
    n_row = lp // tm

    def body(order_ref, h_ref, *refs):
        gathered = refs[n:2 * n]
        o_ref, wt_ref = refs[2 * n], refs[2 * n + 1]
        w_buf, ici_send, ici_recv, d2d_send, d2d_recv, w_sem = refs[2 * n + 2:]
        j = pl.program_id(0)
        i = pl.program_id(1)
        x, y, c = _mesh_pos()
        me = 2 * x + y
        chips = _other_chips(x, y)

        def part(a, block, core):
            return gathered[a].at[block, pl.ds(core * half[a], half[a])]

        def over_ici(a, k, block):
            px, py = chips[k]
            return pltpu.make_async_remote_copy(
                src_ref=part(a, block, c), dst_ref=part(a, block, c),
                send_sem=ici_send.at[a, k], recv_sem=ici_recv.at[a, k],
                device_id=(px, py, c), device_id_type=MESH)

        def over_d2d(a, k, core):
            px, py = chips[k]
            return pltpu.make_async_remote_copy(
                src_ref=part(a, 2 * px + py, core), dst_ref=part(a, 2 * px + py, core),
                send_sem=d2d_send.at[a, k], recv_sem=d2d_recv.at[a, k],
                device_id=(x, y, 1 - c), device_id_type=MESH)

        @pl.when(jnp.logical_and(j == 0, i == 0))
        def _():
            for a in range(n):
                for k in range(2):
                    over_ici(a, k, me).start()

        for k, (px, py) in enumerate(chips):
            @pl.when(jnp.logical_and(j == k + 1, i == 0))
            def _(k=k, px=px, py=py):
                for a in range(n):
                    over_ici(a, k, 2 * px + py).wait_recv()
                    over_d2d(a, k, c).start()
                if k == 0:
                    for a in range(n):
                        over_ici(a, 2, me).start()
                for a in range(n):
                    over_d2d(a, k, 1 - c).wait_recv()

        @pl.when(i == 0)
        def _():
            load = pltpu.make_async_copy(gathered[0].at[order_ref[j]], w_buf, w_sem)
            load.start()
            load.wait()
            wt_ref[0] = w_buf[...].T

        o_ref[...] = jnp.dot(h_ref[...], w_buf[...], preferred_element_type=F32)

        @pl.when(jnp.logical_and(j == N_CHIPS - 1, i == n_row - 1))
        def _():
            for a in range(n):
                for k in range(3):
                    over_ici(a, k, me).wait_send()
                    over_d2d(a, k, c).wait_send()

    grid_spec = pltpu.PrefetchScalarGridSpec(
        num_scalar_prefetch=1, grid=(N_CHIPS, n_row),
        in_specs=[pl.BlockSpec((tm, d), lambda j, i, order_ref: (i, 0))] + [ANY] * n,
        out_specs=[ANY] * n + [pl.BlockSpec((tm, ncol), lambda j, i, order_ref: (i, order_ref[j])),
                               pl.BlockSpec((1, ncol, d), lambda j, i, order_ref: (order_ref[j], 0, 0))],
        scratch_shapes=[pltpu.VMEM((d, ncol), BF16)] + [pltpu.SemaphoreType.DMA((n, 3))] * 4
        + [pltpu.SemaphoreType.DMA])
    out = pl.pallas_call(
        body, grid_spec=grid_spec,
        out_shape=[jax.ShapeDtypeStruct(b.shape, b.dtype) for b in bufs]
        + [jax.ShapeDtypeStruct((lp, N_CHIPS * ncol), F32), jax.ShapeDtypeStruct((N_CHIPS, ncol, d), BF16)],
        input_output_aliases={2 + a: a for a in range(n)},
        name="gather_in_proj", compiler_params=_params())(order, h, *bufs)
    return out[n], out[n + 1], out[:n]


def _send_other_halves(grads, tag):
    n = len(grads)

    def body(*refs):
        srcs = refs[:n]
        dsts = refs[n:2 * n]
        send_sems, recv_sems = refs[2 * n:]
        x, y, c = _mesh_pos()
        copies = [pltpu.make_async_remote_copy(
            src_ref=srcs[a].at[j, 1 - c], dst_ref=dsts[a].at[j], send_sem=send_sems.at[a, j],
            recv_sem=recv_sems.at[a, j], device_id=(x, y, 1 - c), device_id_type=MESH)
            for a in range(n) for j in range(N_CHIPS)]
        for cp in copies:
            cp.start()
        for cp in copies:
            cp.wait()

    return pl.pallas_call(
        body, in_specs=[ANY] * n, out_specs=[ANY] * n,
        out_shape=[jax.ShapeDtypeStruct((N_CHIPS,) + g.shape[2:], F32) for g in grads],
        scratch_shapes=[pltpu.SemaphoreType.DMA((n, N_CHIPS))] * 2,
        name="send_other_halves_" + tag)(*grads)


HBM = pl.BlockSpec(memory_space=pltpu.HBM)
SEM = pl.BlockSpec(memory_space=pltpu.SEMAPHORE)


def _block_copies(n, srcs, dsts, send_sems, recv_sems):
    x, y, c = _mesh_pos()
    return [pltpu.make_async_remote_copy(
        src_ref=srcs[a].at[2 * px + py], dst_ref=dsts[a].at[k], send_sem=send_sems.at[3 * a + k],
        recv_sem=recv_sems.at[3 * a + k], device_id=(px, py, c), device_id_type=MESH)
        for a in range(n) for k, (px, py) in enumerate(_other_chips(x, y))]


def _exchange_start(blocked, tag):
    n = len(blocked)
    lands = [lax.empty((3,) + b.shape[1:], b.dtype) for b in blocked]
    bufs = [pltpu.with_memory_space_constraint(b, pltpu.HBM) for b in list(blocked) + lands]
    nb = 2 * n

    def body(*refs):
        for cp in _block_copies(n, refs[:n], refs[n:nb], refs[nb], refs[nb + 1]):
            cp.start()
        refs[-1][...] = jnp.zeros_like(refs[-1])

    out = pl.pallas_call(
        body, name="exchange_start_" + tag,
        in_specs=[HBM] * nb,
        out_shape=[pltpu.SemaphoreType.DMA((3 * n,)), pltpu.SemaphoreType.DMA((3 * n,))]
        + [pltpu.HBM(b.shape, b.dtype) for b in bufs] + [jax.ShapeDtypeStruct((8, 128), F32)],
        out_specs=[SEM] * 2 + [HBM] * nb + [pl.BlockSpec(memory_space=pltpu.VMEM)],
        input_output_aliases={i: 2 + i for i in range(nb)},
        compiler_params=pltpu.CompilerParams(has_side_effects=pltpu.SideEffectType.DATAFLOW_SIDE_EFFECTING),
    )(*bufs)
    return (out[:2], out[2:2 + nb]), out[-1]


def _exchange_wait(state, after, tag):
    sems, bufs = state
    nb = len(bufs)
    n = nb // 2

    def body(*refs):
        for cp in _block_copies(n, refs[:n], refs[n:nb], refs[nb], refs[nb + 1]):
            cp.wait_send()
            cp.wait_recv()

    out = pl.pallas_call(
        body, name="exchange_wait_" + tag,
        in_specs=[HBM] * nb + [SEM] * 2 + [ANY],
        out_shape=[pltpu.HBM(b.shape, b.dtype) for b in bufs],
        out_specs=[HBM] * nb,
        input_output_aliases={i: i for i in range(nb)},
        compiler_params=pltpu.CompilerParams(has_side_effects=pltpu.SideEffectType.DATAFLOW_SIDE_EFFECTING),
    )(*bufs, *sems, after)
    return out[n:nb]


def _exchange_small(small):
    def body(small_src, small_dst, ssend_sems, srecv_sems, local_sem):
        x, y, c = _mesh_pos()
        my_idx = 4 * x + 2 * y + c
        local = pltpu.make_async_copy(small_src, small_dst.at[my_idx], local_sem)
        local.start()
        others = []
        for r in range(1, 8):
            px = 1 - x if r & 4 else x
            py = 1 - y if r & 2 else y
            pc = 1 - c if r & 1 else c
            others.append((px, py, pc))
        for r, peer in enumerate(others):
            pltpu.make_async_remote_copy(
                src_ref=small_src, dst_ref=small_dst.at[my_idx], send_sem=ssend_sems.at[r],
                recv_sem=srecv_sems.at[r], device_id=peer, device_id_type=MESH).start()
        for r, (px, py, pc) in enumerate(others):
            pltpu.make_async_remote_copy(
                src_ref=small_src, dst_ref=small_dst.at[4 * px + 2 * py + pc], send_sem=ssend_sems.at[r],
                recv_sem=srecv_sems.at[r], device_id=(px, py, pc), device_id_type=MESH).wait()
        local.wait()

    return pl.pallas_call(
        body, in_specs=[ANY], out_specs=ANY, out_shape=jax.ShapeDtypeStruct((8,) + small.shape, F32),
        scratch_shapes=[pltpu.SemaphoreType.DMA((7,)), pltpu.SemaphoreType.DMA((7,)), pltpu.SemaphoreType.DMA],
        name="exchange_small")(small)


def _join_halves(bufs):
    n = len(bufs)

    def body(*refs):
        joined = refs[n:2 * n]
        send_sems, recv_sems = refs[2 * n:]
        x, y, c = _mesh_pos()
        for a in range(n):
            pltpu.make_async_remote_copy(
                src_ref=joined[a].at[c], dst_ref=joined[a].at[c], send_sem=send_sems.at[a],
                recv_sem=recv_sems.at[a], device_id=(x, y, 1 - c), device_id_type=MESH).start()
        for a in range(n):
            pltpu.make_async_remote_copy(
                src_ref=joined[a].at[c], dst_ref=joined[a].at[1 - c], send_sem=send_sems.at[a],
                recv_sem=recv_sems.at[a], device_id=(x, y, 1 - c), device_id_type=MESH).wait()

    return pl.pallas_call(
        body, in_specs=[ANY] * n, out_specs=[ANY] * n,
        out_shape=[jax.ShapeDtypeStruct(b.shape, b.dtype) for b in bufs],
        input_output_aliases={a: a for a in range(n)},
        scratch_shapes=[pltpu.SemaphoreType.DMA((n,))] * 2,
        name="join_halves")(*bufs)


def kernel(x, meta_tokens, norm_g, w_in, conv_w, conv_b, ln_g, ln_b, w_conv_out, lb_logits, gnorm_g, w_rec_out, w_out, final_g, loss_target, m_meta_tokens, m_norm_g, m_w_in, m_conv_w, m_conv_b, m_ln_g, m_ln_b, m_w_conv_out, m_lb_logits, m_gnorm_g, m_w_rec_out, m_w_out, m_final_g, v_meta_tokens, v_norm_g, v_w_in, v_conv_w, v_conv_b, v_ln_g, v_ln_b, v_w_conv_out, v_lb_logits, v_gnorm_g, v_w_rec_out, v_w_out, v_final_g):
    seq, d = x.shape[1], x.shape[2]
    n_meta = meta_tokens.shape[0]
    n_pad = CHUNK - n_meta
    ds = d // N_CHIPS
    chip = 2 * lax.axis_index("x") + lax.axis_index("y")

    conv_w_pad = jnp.pad(conv_w[0], ((0, HALO - CONV_WIDTH), (0, 0)))
    chip_idx = chip.astype(jnp.int32).reshape(1)
    (small_g,) = _gather_weights([
        _place_shard("place_small", jnp.concatenate([conv_w_pad, meta_tokens], axis=0), chip_idx, F32)])
    cw_full = jnp.transpose(small_g[:, 0:HALO], (1, 0, 2)).reshape(HALO, d)
    meta_full = jnp.transpose(small_g[:, HALO:HALO + n_meta], (1, 0, 2)).reshape(n_meta, d)

    hres = jnp.concatenate([jnp.zeros((n_pad, d), F32), meta_full, x[0]], axis=0)
    target = jnp.pad(loss_target[0], ((CHUNK, 0), (0, 0)))
    final_g2 = final_g.reshape(1, d)
    h = _rmsnorm_fwd(hres, norm_g)
    fx, fy = 1 - lax.axis_index("x"), 1 - lax.axis_index("y")
    order = jnp.stack([chip, 2 * fx + (1 - fy), 2 * (1 - fx) + fy, 2 * fx + fy]).astype(jnp.int32)
    proj, win_t, (win_g, sq_g) = _gather_in_proj(h, [
        _place_shard("place_w_in", w_in[0], chip_idx, BF16),
        _place_shard("place_square", jnp.concatenate([w_conv_out[0], w_rec_out[0], w_out[0]], axis=0),
                     chip_idx, BF16)], order)
    wc_full = sq_g[:, 0:ds].reshape(d, d)
    wr_full = sq_g[:, ds:2 * ds].reshape(d, d)
    wo_full = sq_g[:, 2 * ds:3 * ds].reshape(d, d)
    c, yc_in, y_conv = _conv_fwd(proj, cw_full, conv_b, ln_g, ln_b, wc_full)
    o, s_all = _hgrn_fwd(proj, lb_logits, n_pad)
    yr_in, merged, y_rec, dout, loss_acc, dfinal_g = _tail_fwd(
        o, proj, y_conv, hres, target, gnorm_g, final_g2, wr_full, wo_full)

    (dyc, dyr, dout_bf, dz, dproj, do, dc, dgnorm_g, dln_g, dln_b) = _tail_bwd(
        dout, proj, y_conv, y_rec, o, c, wo_full, wr_full, wc_full, ln_g, ln_b, gnorm_g)
    g_wc = _weight_grad(yc_in, dyc, "grad_w_conv_out", False)
    g_wr = _weight_grad(yr_in, dyr, "grad_w_rec_out", False)
    g_wo = _weight_grad(merged, dout_bf, "grad_w_out", False)

    core = lax.axis_index("c").astype(jnp.int32).reshape(1)

    def chip_sum_and_start(g, tag):
        g = g.reshape(N_CHIPS, 2, g.shape[1] // 2, g.shape[2])
        (from_sibling,) = _send_other_halves([g], tag)
        sums = _chip_half_sum("chip_half_sum_" + tag, g, from_sibling, core)
        in_flight, token = _exchange_start([sums[1]], tag)
        return sums[0], in_flight, token[0:1, 0:1]

    g_sq = jnp.concatenate([g.reshape(N_CHIPS, ds, d) for g in (g_wc, g_wr, g_wo)], axis=1)
    sum_sq, flight_sq, token_sq = chip_sum_and_start(g_sq, "square")
    dproj, dlb_logits = _hgrn_bwd(proj, do, s_all, lb_logits + token_sq, n_pad, dproj)
    dproj, dconv_w, dconv_b = _conv_bwd(dc, proj, cw_full, dz, dproj)
    (recv_sq,) = _exchange_wait(flight_sq, dconv_b, "square")
    g_win = _weight_grad(h, dproj, "grad_w_in", True)
    sum_win, flight_win, token_win = chip_sum_and_start(g_win, "w_in")
    dhres, dnorm_g = _in_proj_bwd(dproj, win_t, hres, norm_g + token_win, dout)
    grad_x = dhres[CHUNK:][None]
    (recv_win,) = _exchange_wait(flight_win, dnorm_g, "w_in")
    small = jnp.concatenate([dnorm_g, dconv_b, dln_g, dln_b, dlb_logits, dgnorm_g, dfinal_g,
                             dhres[n_pad:CHUNK], dconv_w[:CONV_WIDTH],
                             jnp.zeros((1, d), F32)], axis=0)
    small_slots = _exchange_small(small)
    chip_core = jnp.concatenate([chip_idx, core])
    totals = [_block_half_total("block_half_total_" + nm, s, r, chip_core)
              for nm, s, r in zip(("w_in", "square"), (sum_win, sum_sq), (recv_win, recv_sq))]
    gt_win, gt_sq = [t.reshape(2 * t.shape[1], t.shape[2]) for t in _join_halves(totals)]
    small_sum = _sum_slots("sum_small", small_slots)

    res = {}
    res["w_in"] = _adamw("adamw_w_in", w_in[0], m_w_in[0], v_w_in[0], gt_win)
    res["w_conv_out"] = _adamw("adamw_w_conv_out", w_conv_out[0], m_w_conv_out[0], v_w_conv_out[0], gt_sq[0:ds])
    res["w_rec_out"] = _adamw("adamw_w_rec_out", w_rec_out[0], m_w_rec_out[0], v_w_rec_out[0], gt_sq[ds:2 * ds])
    res["w_out"] = _adamw("adamw_w_out", w_out[0], m_w_out[0], v_w_out[0], gt_sq[2 * ds:3 * ds])
    big = {k: tuple(a[None] for a in v) for k, v in res.items()}

    rep_names = ("norm_g", "conv_b", "ln_g", "ln_b", "lb_logits", "gnorm_g", "final_g")
    rep_w = (norm_g, conv_b, ln_g, ln_b, lb_logits, gnorm_g, final_g2)
    rep_m = (m_norm_g, m_conv_b, m_ln_g, m_ln_b, m_lb_logits, m_gnorm_g, m_final_g.reshape(1, d))
    rep_v = (v_norm_g, v_conv_b, v_ln_g, v_ln_b, v_lb_logits, v_gnorm_g, v_final_g.reshape(1, d))
    rep = _adamw("adamw_replicated", jnp.concatenate(rep_w, 0), jnp.concatenate(rep_m, 0),
                 jnp.concatenate(rep_v, 0), small_sum[0:8])
    rep_rows = {"norm_g": (0, 1), "conv_b": (1, 2), "ln_g": (2, 3), "ln_b": (3, 4), "lb_logits": (4, 6),
                "gnorm_g": (6, 7), "final_g": (7, 8)}
    small_out = {}
    for nm in rep_names:
        lo, hi = rep_rows[nm]
        vals = tuple(a[lo:hi] for a in rep)
        if nm == "final_g":
            vals = tuple(a.reshape(d) for a in vals)
        small_out[nm] = vals
    cw_row = 8 + n_meta
    g_meta = lax.dynamic_slice_in_dim(small_sum[8:cw_row], chip * ds, ds, axis=1)
    small_out["meta_tokens"] = _adamw("adamw_meta", meta_tokens, m_meta_tokens, v_meta_tokens, g_meta)
    g_cw = lax.dynamic_slice_in_dim(small_sum[cw_row:cw_row + HALO], chip * ds, ds, axis=1)
    pad_rows = ((0, HALO - CONV_WIDTH), (0, 0))
    cw_res = _adamw("adamw_conv_w", conv_w_pad, jnp.pad(m_conv_w[0], pad_rows),
                    jnp.pad(v_conv_w[0], pad_rows, constant_values=1.0), g_cw)
    small_out["conv_w"] = tuple(a[:CONV_WIDTH][None] for a in cw_res)

    loss = lax.psum(loss_acc[0, 0], ("x", "y", "c"))

    order = ("meta_tokens", "norm_g", "w_in", "conv_w", "conv_b", "ln_g", "ln_b", "w_conv_out", "lb_logits",
             "gnorm_g", "w_rec_out", "w_out", "final_g")
    allres = {**big, **small_out}
    outs = [loss, grad_x]
    for field in range(4):
        outs.extend(allres[nm][field] for nm in order)
    return tuple(outs)
```

```python
import numpy as np

import jax
import jax.numpy as jnp
from jax import lax
from jax.experimental import pallas as pl
from jax.experimental.pallas import tpu as pltpu

F32 = jnp.float32
BF16 = jnp.bfloat16

EPS = 1e-6
CHUNK = 64
N_LEVELS = 6
FIRST_TABLE_LEVEL = 5
CONV_WIDTH = 31
HALO = 32
CONV_ROWS = 32
CONV_LANES = 256
HEAD = 128
W_IN_COL_TILES = 1
HEADS_PER_TRIP = 8
N_CHIPS = 4
VMEM_LIMIT_BYTES = 56 * 1024 * 1024

ADAM_LR = 0.001
ADAM_B1 = 0.9
ADAM_B2 = 0.999
ADAM_EPS = 1e-08
ADAM_WD = 0.01
ADAM_STEP = 10

MESH = pl.DeviceIdType.MESH
ANY = pl.BlockSpec(memory_space=pl.ANY)

NT = (((1,), (1,)), ((), ()))
TN = (((0,), (0,)), ((), ()))


def _params(**kw):
    return pltpu.CompilerParams(vmem_limit_bytes=VMEM_LIMIT_BYTES, **kw)


def _sigmoid(x):
    return jax.nn.sigmoid(x)


def _dsilu(x, s):
    return s * (1.0 + x * (1.0 - s))


def _row_tile(lp):
    for t in (320, 256, 192, 128, 64):
        if lp % t == 0:
            return t
    raise ValueError(f"unsupported padded length {lp}")


def _mm_row_tile(lp):
    for t in (832, 640, 320, 256, 192, 128, 64):
        if lp % t == 0:
            return t
    raise ValueError(f"unsupported padded length {lp}")


def _dot3(m_bf16, x):
    hi = x.astype(BF16)
    r1 = x - hi.astype(F32)
    mid = r1.astype(BF16)
    lo = (r1 - mid.astype(F32)).astype(BF16)
    return (jnp.dot(m_bf16, hi, preferred_element_type=F32)
            + jnp.dot(m_bf16, mid, preferred_element_type=F32)
            + jnp.dot(m_bf16, lo, preferred_element_type=F32))


def _dot2(m_bf16, x):
    hi = x.astype(BF16)
    lo = (x - hi.astype(F32)).astype(BF16)
    return (jnp.dot(m_bf16, hi, preferred_element_type=F32)
            + jnp.dot(m_bf16, lo, preferred_element_type=F32))


def _col_to_row(col):
    return jnp.broadcast_to(col, (HEAD, 8)).T[0:1, :]


def _row_to_col(row):
    return jnp.broadcast_to(row, (8, HEAD)).T[:, 0:1]


def _hgrn_tables():
    t = np.arange(CHUNK)
    ltri = (t[None, :] <= t[:, None]).astype(np.float32)
    mats = [ltri]
    for lvl in range(FIRST_TABLE_LEVEL, N_LEVELS + 1):
        blk = CHUNK >> (lvl - 1)
        mid = (t // blk) * blk + blk // 2
        mats.append(ltri[mid - 1])
    after = (t[None, :] >= t[:, None]).astype(np.float32)
    before = (t[None, :] < t[:, None]).astype(np.float32)
    return jnp.asarray(np.concatenate(mats, 0), BF16), jnp.asarray(np.concatenate([after, before], 1), BF16)


def _rmsnorm_fwd(hres, g):
    lp, d = hres.shape
    tm = _row_tile(lp)

    def body(x_ref, g_ref, h_ref):
        x = x_ref[...]
        r = lax.rsqrt(jnp.mean(x * x, axis=-1, keepdims=True) + EPS)
        h_ref[...] = (x * r * g_ref[...]).astype(BF16)

    return pl.pallas_call(
        body, grid=(lp // tm,),
        in_specs=[pl.BlockSpec((tm, d), lambda i: (i, 0)), pl.BlockSpec((1, d), lambda i: (0, 0))],
        out_specs=pl.BlockSpec((tm, d), lambda i: (i, 0)),
        out_shape=jax.ShapeDtypeStruct((lp, d), BF16),
        name="rmsnorm_fwd", compiler_params=_params())(hres, g)


def _in_proj(h, wg):
    lp, d = h.shape
    _, _, ncol = wg.shape
    tm = _mm_row_tile(lp)
    nt = W_IN_COL_TILES
    tn = ncol // nt

    def body(h_ref, w_ref, o_ref):
        o_ref[...] = jnp.dot(h_ref[...], w_ref[0], preferred_element_type=F32)

    return pl.pallas_call(
        body, grid=(N_CHIPS, nt, lp // tm),
        in_specs=[pl.BlockSpec((tm, d), lambda j, n, i: (i, 0)),
                  pl.BlockSpec((1, d, tn), lambda j, n, i: (j, 0, n))],
        out_specs=pl.BlockSpec((tm, tn), lambda j, n, i: (i, j * nt + n)),
        out_shape=jax.ShapeDtypeStruct((lp, N_CHIPS * ncol), F32),
        name="in_proj", compiler_params=_params())(h, wg)


def _conv_fwd(proj, conv_w, conv_b, ln_g, ln_b, w_conv):
    lp = proj.shape[0]
    d = conv_b.shape[1]
    tm = _row_tile(lp)
    hb = tm // HALO

    def body(ua_ref, ub_ref, z_ref, uap_ref, ubp_ref, cw_ref, cb_ref, lg_ref, lb_ref, w_ref,
             c_ref, ycin_ref, yconv_ref, aext_ref):
        i = pl.program_id(0)
        a_prev = uap_ref[...] * _sigmoid(ubp_ref[...])
        aext_ref[0:HALO, :] = jnp.where(i > 0, a_prev, 0.0)
        aext_ref[HALO:HALO + tm, :] = ua_ref[...] * _sigmoid(ub_ref[...])

        def row_block(r, carry):
            r0 = pl.multiple_of(r * CONV_ROWS, CONV_ROWS)
            for cs in range(d // CONV_LANES):
                cl = slice(cs * CONV_LANES, (cs + 1) * CONV_LANES)
                blk = aext_ref[pl.ds(r0, CONV_ROWS + HALO), cl]
                acc = jnp.zeros((CONV_ROWS, CONV_LANES), F32) + cb_ref[:, cl]
                for b in range(8):
                    sh = blk if b == 0 else pltpu.roll(blk, CONV_ROWS + HALO - b, axis=0)
                    for a in range(5):
                        j = 8 * a + b - 2
                        if 0 <= j < CONV_WIDTH:
                            acc = acc + cw_ref[j:j + 1, cl] * sh[8 * a:8 * a + CONV_ROWS, :]
                c_ref[pl.ds(r0, CONV_ROWS), cl] = acc
            return carry

        lax.fori_loop(0, tm // CONV_ROWS, row_block, 0)

        c = c_ref[...]
        mu = jnp.mean(c, axis=-1, keepdims=True)
        xc = c - mu
        rstd = lax.rsqrt(jnp.mean(xc * xc, axis=-1, keepdims=True) + EPS)
        ln = xc * rstd * lg_ref[...] + lb_ref[...]
        s = ln * _sigmoid(ln)
        z = z_ref[...]
        ycin = (s * (z * _sigmoid(z))).astype(BF16)
        ycin_ref[...] = ycin
        yconv_ref[...] = jnp.dot(ycin, w_ref[...], preferred_element_type=F32)

    row = lambda p: pl.BlockSpec((tm, d), lambda i, p=p: (i, p))
    halo = lambda p: pl.BlockSpec((HALO, d), lambda i, p=p: (jnp.maximum(i * hb - 1, 0), p))
    vec = pl.BlockSpec((1, d), lambda i: (0, 0))
    return pl.pallas_call(
        body, grid=(lp // tm,),
        in_specs=[row(0), row(1), row(2), halo(0), halo(1),
                  pl.BlockSpec((HALO, d), lambda i: (0, 0)), vec, vec, vec,
                  pl.BlockSpec((d, d), lambda i: (0, 0))],
        out_specs=[pl.BlockSpec((tm, d), lambda i: (i, 0))] * 3,
        out_shape=[jax.ShapeDtypeStruct((lp, d), F32), jax.ShapeDtypeStruct((lp, d), BF16),
                   jax.ShapeDtypeStruct((lp, d), F32)],
        scratch_shapes=[pltpu.VMEM((HALO + tm, d), F32)],
        name="conv_fwd", compiler_params=_params())(
            proj, proj, proj, proj, proj, conv_w, conv_b, ln_g, ln_b, w_conv)


def _lower_bound(lbl_ref):
    l0 = lbl_ref[0:1, :]
    l1 = lbl_ref[1:2, :]
    m = jnp.maximum(l0, l1)
    e0 = jnp.exp(l0 - m)
    e1 = jnp.exp(l1 - m)
    p0 = e0 / (e0 + e1)
    return p0, p0 * (e1 / (e0 + e1))


def _level_masks():
    r2 = lax.broadcasted_iota(jnp.int32, (CHUNK, CHUNK), 0)
    c2 = lax.broadcasted_iota(jnp.int32, (CHUNK, CHUNK), 1)
    out = []
    for lvl in range(1, N_LEVELS + 1):
        blk = CHUNK >> (lvl - 1)
        sh = blk.bit_length() - 1
        same = (r2 >> sh) == (c2 >> sh)
        t_upper = (r2 & (blk - 1)) >= (blk // 2)
        s_lower = (c2 & (blk - 1)) < (blk // 2)
        out.append(jnp.logical_and(same, jnp.logical_and(t_upper, s_lower)))
    return out


def _gates(qr, fr, lb, valid):
    sq = _sigmoid(qr)
    q = qr * sq
    sf = _sigmoid(fr)
    f = lb + (1.0 - lb) * sf
    g = jnp.where(valid, jnp.log(f), 0.0)
    k = jnp.where(valid, 1.0 - f, 0.0)
    return q, sq, f, sf, g, k


def _level_reference(lvl, b, t_ref, hs):
    if lvl >= FIRST_TABLE_LEVEL:
        base = CHUNK * (lvl - FIRST_TABLE_LEVEL + 1)
        return t_ref[base:base + CHUNK, hs]
    blk = CHUNK >> (lvl - 1)
    rows = [jnp.broadcast_to(b[m + blk // 2 - 1:m + blk // 2, :], (blk, HEAD)) for m in range(0, CHUNK, blk)]
    return rows[0] if len(rows) == 1 else jnp.concatenate(rows, axis=0)


def _level_factor(b, r):
    d = b - r
    return jnp.exp(jnp.minimum(d, -d))


def _hgrn_fwd(proj, lb_logits, n_pad):
    lp = proj.shape[0]
    d = lb_logits.shape[1]
    n_heads = d // HEAD
    nc = lp // CHUNK
    tab, _ = _hgrn_tables()
    n_tab = tab.shape[0]

    def body(qr_ref, fr_ref, ir_ref, lbl_ref, tab_ref, o_ref, sall_ref, s_ref, t_ref):
        n = pl.program_id(0)

        @pl.when(n == 0)
        def _():
            s_ref[...] = jnp.zeros_like(s_ref)

        sall_ref[0] = s_ref[...]
        lb_all, _ = _lower_bound(lbl_ref)
        rid = lax.broadcasted_iota(jnp.int32, (CHUNK, 1), 0)
        valid = jnp.logical_or(n > 0, rid >= n_pad)
        f_all = lb_all + (1.0 - lb_all) * _sigmoid(fr_ref[...])
        t_ref[...] = _dot2(tab_ref[...], jnp.where(valid, jnp.log(f_all), 0.0))
        masks = _level_masks()

        def head(h):
            off = h * HEAD if isinstance(h, int) else pl.multiple_of(h * HEAD, HEAD)
            hs = pl.ds(off, HEAD)
            lb = _lower_bound_slice(lbl_ref, hs)
            q, _, _, _, _, k = _gates(qr_ref[:, hs], fr_ref[:, hs], lb, valid)
            v = ir_ref[:, hs]
            b = t_ref[0:CHUNK, hs]
            s0 = s_ref[hs, :]
            o = jnp.dot((q * jnp.exp(b)).astype(BF16), s0.astype(BF16), preferred_element_type=F32)
            o = o + jnp.sum(q * k, axis=-1, keepdims=True) * v
            a = jnp.zeros((CHUNK, CHUNK), F32)
            for lvl in range(1, N_LEVELS + 1):
                e = _level_factor(b, _level_reference(lvl, b, t_ref, hs))
                p = lax.dot_general((q * e).astype(BF16), (k * e).astype(BF16), NT, preferred_element_type=F32)
                a = a + jnp.where(masks[lvl - 1], p, 0.0)
            vb = v.astype(BF16)
            o_ref[:, hs] = o + jnp.dot(a.astype(BF16), vb, preferred_element_type=F32)
            b_last = t_ref[CHUNK - 1:CHUNK, hs]
            khat = (k * jnp.exp(b_last - b)).astype(BF16)
            s_ref[hs, :] = _row_to_col(jnp.exp(b_last)) * s0 + lax.dot_general(khat, vb, TN, preferred_element_type=F32)
        per_trip = min(HEADS_PER_TRIP, n_heads)

        def head_group(p, carry):
            for u in range(per_trip):
                head(p * per_trip + u)
            return carry

        if n_heads == per_trip:
            head_group(0, 0)
        else:
            lax.fori_loop(0, n_heads // per_trip, head_group, 0)

    piece = lambda p: pl.BlockSpec((CHUNK, d), lambda n, p=p: (n, p))
    return pl.pallas_call(
        body, grid=(nc,),
        in_specs=[piece(3), piece(4), piece(5), pl.BlockSpec((2, d), lambda n: (0, 0)),
                  pl.BlockSpec((n_tab, CHUNK), lambda n: (0, 0))],
        out_specs=[pl.BlockSpec((CHUNK, d), lambda n: (n, 0)), pl.BlockSpec((1, d, HEAD), lambda n: (n, 0, 0))],
        out_shape=[jax.ShapeDtypeStruct((lp, d), F32), jax.ShapeDtypeStruct((nc, d, HEAD), F32)],
        scratch_shapes=[pltpu.VMEM((d, HEAD), F32), pltpu.VMEM((n_tab, d), F32)],
        name="hgrn_fwd", compiler_params=_params())(proj, proj, proj, lb_logits, tab)


def _lower_bound_slice(lbl_ref, hs):
    l0 = lbl_ref[0:1, hs]
    l1 = lbl_ref[1:2, hs]
    m = jnp.maximum(l0, l1)
    e0 = jnp.exp(l0 - m)
    e1 = jnp.exp(l1 - m)
    return e0 / (e0 + e1)


def _tail_fwd(o, proj, y_conv, hres, target, gnorm_g, final_g, w_rec, w_out):
    lp, d = o.shape
    n_heads = d // HEAD
    tm = _row_tile(lp)

    n_slabs = tm // CHUNK

    def body(o_ref, gr_ref, mc_ref, mr_ref, yc_ref, x_ref, gn_ref, fg_ref, wr_ref, wo_ref, *rest):
        t_refs = rest[:n_slabs]
        yrin_ref, mg_ref, yrec_ref, dout_ref, loss_ref, dfg_ref = rest[n_slabs:]
        i = pl.program_id(0)

        @pl.when(i == 0)
        def _():
            loss_ref[...] = jnp.zeros_like(loss_ref)
            dfg_ref[...] = jnp.zeros_like(dfg_ref)

        for h in range(n_heads):
            hs = slice(h * HEAD, (h + 1) * HEAD)
            oh = o_ref[:, hs]
            on = oh * lax.rsqrt(jnp.mean(oh * oh, axis=-1, keepdims=True) + EPS) * gn_ref[:, hs]
            gr = gr_ref[:, hs]
            yrin_ref[:, hs] = (on * (gr * _sigmoid(gr))).astype(BF16)
        yrec = jnp.dot(yrin_ref[...], wr_ref[...], preferred_element_type=F32)
        yrec_ref[...] = yrec
        merged = (_sigmoid(mc_ref[...]) * yc_ref[...] + _sigmoid(mr_ref[...]) * yrec).astype(BF16)
        mg_ref[...] = merged
        out = x_ref[...] + jnp.dot(merged, wo_ref[...], preferred_element_type=F32)
        r = lax.rsqrt(jnp.mean(out * out, axis=-1, keepdims=True) + EPS)
        yhat = out * r
        fg = fg_ref[...]
        rid = lax.broadcasted_iota(jnp.int32, (tm, 1), 0) + i * tm
        tgt = jnp.concatenate([t[...] for t in t_refs], axis=0)
        err = jnp.where(rid >= CHUNK, yhat * fg - tgt, 0.0)
        loss_ref[...] += 0.5 * jnp.sum(err * err) / d
        dy = err / d
        dfg_ref[...] += jnp.sum(dy * yhat, axis=0, keepdims=True)
        dyh = dy * fg
        dout_ref[...] = r * (dyh - yhat * jnp.mean(dyh * yhat, axis=-1, keepdims=True))

    row = lambda p: pl.BlockSpec((tm, d), lambda i, p=p: (i, p))
    vec = pl.BlockSpec((1, d), lambda i: (0, 0))
    mat = pl.BlockSpec((d, d), lambda i: (0, 0))
    return pl.pallas_call(
        body, grid=(lp // tm,),
        in_specs=[row(0), row(6), row(7), row(8), row(0), row(0), vec, vec, mat, mat]
        + [pl.BlockSpec((CHUNK, d), lambda i, u=u: (jnp.maximum(i * n_slabs + u - 1, 0), 0)) for u in range(n_slabs)],
        out_specs=[row(0), row(0), row(0), row(0), pl.BlockSpec((8, 128), lambda i: (0, 0)), vec],
        out_shape=[jax.ShapeDtypeStruct((lp, d), BF16), jax.ShapeDtypeStruct((lp, d), BF16),
                   jax.ShapeDtypeStruct((lp, d), F32), jax.ShapeDtypeStruct((lp, d), F32),
                   jax.ShapeDtypeStruct((8, 128), F32), jax.ShapeDtypeStruct((1, d), F32)],
        name="tail_fwd", compiler_params=_params())(
            o, proj, proj, proj, y_conv, hres, gnorm_g, final_g, w_rec, w_out, *([target] * n_slabs))


def _tail_bwd(dout, proj, y_conv, y_rec, o, c, w_out, w_rec, w_conv, ln_g, ln_b, gnorm_g):
    lp, d = dout.shape
    n_heads = d // HEAD
    tm = _row_tile(lp)

    def body(dout_ref, mc_ref, mr_ref, z_ref, gr_ref, yc_ref, yrec_ref, o_ref, c_ref,
             wo_ref, wr_ref, wc_ref, lg_ref, lb_ref, gn_ref,
             dyc_ref, dyr_ref, doutb_ref, dz_ref, dp_ref, do_ref, dc_ref,
             dgn_ref, dlg_ref, dlb_ref, dyrin_ref):
        i = pl.program_id(0)

        @pl.when(i == 0)
        def _():
            dgn_ref[...] = jnp.zeros_like(dgn_ref)
            dlg_ref[...] = jnp.zeros_like(dlg_ref)
            dlb_ref[...] = jnp.zeros_like(dlb_ref)

        doutb = dout_ref[...].astype(BF16)
        doutb_ref[...] = doutb
        dmerged = lax.dot_general(doutb, wo_ref[...], NT, preferred_element_type=F32)
        smc = _sigmoid(mc_ref[...])
        smr = _sigmoid(mr_ref[...])
        dyc = (dmerged * smc).astype(BF16)
        dyr = (dmerged * smr).astype(BF16)
        dyc_ref[...] = dyc
        dyr_ref[...] = dyr
        dp_ref[:, d:2 * d] = (dmerged * yc_ref[...] * smc * (1.0 - smc)).astype(BF16)
        dp_ref[:, 2 * d:3 * d] = (dmerged * yrec_ref[...] * smr * (1.0 - smr)).astype(BF16)

        dyrin_ref[...] = lax.dot_general(dyr, wr_ref[...], NT, preferred_element_type=F32)
        for h in range(n_heads):
            hs = slice(h * HEAD, (h + 1) * HEAD)
            oh = o_ref[:, hs]
            rstd = lax.rsqrt(jnp.mean(oh * oh, axis=-1, keepdims=True) + EPS)
            ohat = oh * rstd
            gn = gn_ref[:, hs]
            gr = gr_ref[:, hs]
            sg = _sigmoid(gr)
            dyrin = dyrin_ref[:, hs]
            don = dyrin * (gr * sg)
            dp_ref[:, hs] = (dyrin * (ohat * gn) * _dsilu(gr, sg)).astype(BF16)
            dgn_ref[:, hs] += jnp.sum(don * ohat, axis=0, keepdims=True)
            doh = don * gn
            do_ref[:, hs] = rstd * (doh - ohat * jnp.mean(doh * ohat, axis=-1, keepdims=True))

        dycin = lax.dot_general(dyc, wc_ref[...], NT, preferred_element_type=F32)
        c = c_ref[...]
        mu = jnp.mean(c, axis=-1, keepdims=True)
        xc = c - mu
        rstd = lax.rsqrt(jnp.mean(xc * xc, axis=-1, keepdims=True) + EPS)
        nrm = xc * rstd
        lg = lg_ref[...]
        ln = nrm * lg + lb_ref[...]
        sl = _sigmoid(ln)
        z = z_ref[...]
        sz = _sigmoid(z)
        dz_ref[...] = (dycin * (ln * sl) * _dsilu(z, sz)).astype(BF16)
        dln = dycin * (z * sz) * _dsilu(ln, sl)
        dlg_ref[...] += jnp.sum(dln * nrm, axis=0, keepdims=True)
        dlb_ref[...] += jnp.sum(dln, axis=0, keepdims=True)
        dn = dln * lg
        dc_ref[...] = rstd * (dn - jnp.mean(dn, axis=-1, keepdims=True)
                              - nrm * jnp.mean(dn * nrm, axis=-1, keepdims=True))

    row = lambda p: pl.BlockSpec((tm, d), lambda i, p=p: (i, p))
    vec = pl.BlockSpec((1, d), lambda i: (0, 0))
    mat = pl.BlockSpec((d, d), lambda i: (0, 0))
    act_bf = jax.ShapeDtypeStruct((lp, d), BF16)
    act_f32 = jax.ShapeDtypeStruct((lp, d), F32)
    vec_f32 = jax.ShapeDtypeStruct((1, d), F32)
    return pl.pallas_call(
        body, grid=(lp // tm,),
        in_specs=[row(0), row(7), row(8), row(2), row(6), row(0), row(0), row(0), row(0),
                  mat, mat, mat, vec, vec, vec],
        out_specs=[row(0)] * 4 + [pl.BlockSpec((tm, 3 * d), lambda i: (i, 2))] + [row(0)] * 2 + [vec] * 3,
        out_shape=[act_bf] * 4 + [jax.ShapeDtypeStruct((lp, 9 * d), BF16)] + [act_f32] * 2 + [vec_f32] * 3,
        scratch_shapes=[pltpu.VMEM((tm, d), F32)],
        name="tail_bwd", compiler_params=_params())(
            dout, proj, proj, proj, proj, y_conv, y_rec, o, c, w_out, w_rec, w_conv, ln_g, ln_b, gnorm_g)


def _hgrn_bwd(proj, do, s_all, lb_logits, n_pad, dproj):
    lp, d = do.shape
    n_heads = d // HEAD
    nc = lp // CHUNK
    tab, utri = _hgrn_tables()
    n_tab = tab.shape[0]

    def body(qr_ref, fr_ref, ir_ref, do_ref, s0_ref, lbl_ref, tab_ref, ut_ref, _,
             dp_ref, dlbl_ref, ds_ref, t_ref, dlb_ref):
        n = pl.program_id(0)
        chunk = nc - 1 - n

        @pl.when(n == 0)
        def _():
            ds_ref[...] = jnp.zeros_like(ds_ref)
            dlb_ref[...] = jnp.zeros_like(dlb_ref)

        lb_all, pp = _lower_bound(lbl_ref)
        rid = lax.broadcasted_iota(jnp.int32, (CHUNK, 1), 0)
        valid = jnp.logical_or(chunk > 0, rid >= n_pad)
        f_all = lb_all + (1.0 - lb_all) * _sigmoid(fr_ref[...])
        t_ref[...] = _dot2(tab_ref[...], jnp.where(valid, jnp.log(f_all), 0.0))
        masks = _level_masks()
        ut = ut_ref[...]

        def head(h):
            off = h * HEAD if isinstance(h, int) else pl.multiple_of(h * HEAD, HEAD)
            hs = pl.ds(off, HEAD)
            lb = _lower_bound_slice(lbl_ref, hs)
            qr = qr_ref[:, hs]
            q, sq, f, sf, _, k = _gates(qr, fr_ref[:, hs], lb, valid)
            v = ir_ref[:, hs]
            do_h = do_ref[:, hs]
            b = t_ref[0:CHUNK, hs]
            b_last = t_ref[CHUNK - 1:CHUNK, hs]
            s0 = s0_ref[0, hs, :]
            ds1 = ds_ref[hs, :]
            eb = jnp.exp(b)
            ekl = jnp.exp(b_last - b)
            do_bf = do_h.astype(BF16)
            v_bf = v.astype(BF16)
            ds1_bf = ds1.astype(BF16)

            da = lax.dot_general(do_bf, v_bf, NT, preferred_element_type=F32)
            da_diag = jnp.sum(do_h * v, axis=-1, keepdims=True)
            a = jnp.zeros((CHUNK, CHUNK), F32)
            dq_x = eb * lax.dot_general(do_bf, s0.astype(BF16), NT, preferred_element_type=F32)
            dk_x = ekl * lax.dot_general(v_bf, ds1_bf, NT, preferred_element_type=F32)
            x_after = q * dq_x
            x_before = k * dk_x
            for lvl in range(1, N_LEVELS + 1):
                e = _level_factor(b, _level_reference(lvl, b, t_ref, hs))
                qt = (q * e).astype(BF16)
                kt = (k * e).astype(BF16)
                p = lax.dot_general(qt, kt, NT, preferred_element_type=F32)
                a = a + jnp.where(masks[lvl - 1], p, 0.0)
                dam = jnp.where(masks[lvl - 1], da, 0.0).astype(BF16)
                dqt = jnp.dot(dam, kt, preferred_element_type=F32)
                dkt = lax.dot_general(dam, qt, TN, preferred_element_type=F32)
                dq_x = dq_x + e * dqt
                dk_x = dk_x + e * dkt
                x_after = x_after + (qt.astype(F32) * dqt - kt.astype(F32) * dkt)

            dv = (lax.dot_general(a.astype(BF16), do_bf, TN, preferred_element_type=F32)
                  + jnp.sum(q * k, axis=-1, keepdims=True) * do_h
                  + jnp.dot((k * ekl).astype(BF16), ds1_bf, preferred_element_type=F32))
            dp_ref[:, pl.ds(2 * d + off, HEAD)] = dv.astype(BF16)

            carried = jnp.exp(b_last) * _col_to_row(jnp.sum(s0 * ds1, axis=-1, keepdims=True))
            dg = _dot3(ut, jnp.concatenate([x_after, x_before], axis=0)) + carried
            dq = dq_x + da_diag * k
            dk = dk_x + da_diag * q
            dp_ref[:, hs] = (dq * _dsilu(qr, sq)).astype(BF16)
            df = jnp.where(valid, dg / f - dk, 0.0)
            dp_ref[:, pl.ds(d + off, HEAD)] = (df * (1.0 - lb) * sf * (1.0 - sf)).astype(BF16)
            dlb_ref[:, hs] += jnp.sum(df * (1.0 - sf), axis=0, keepdims=True)

            ds_ref[hs, :] = (_row_to_col(jnp.exp(b_last)) * ds1
                             + lax.dot_general((q * eb).astype(BF16), do_bf, TN, preferred_element_type=F32))
        per_trip = min(HEADS_PER_TRIP, n_heads)

        def head_group(p, carry):
            for u in range(per_trip):
                head(p * per_trip + u)
            return carry

        if n_heads == per_trip:
            head_group(0, 0)
        else:
            lax.fori_loop(0, n_heads // per_trip, head_group, 0)

        @pl.when(n == nc - 1)
        def _():
            dl0 = dlb_ref[...] * pp
            dlbl_ref[0:1, :] = dl0
            dlbl_ref[1:2, :] = -dl0

    piece = lambda p: pl.BlockSpec((CHUNK, d), lambda n, p=p: (nc - 1 - n, p))
    return pl.pallas_call(
        body, grid=(nc,),
        in_specs=[piece(3), piece(4), piece(5), piece(0),
                  pl.BlockSpec((1, d, HEAD), lambda n: (nc - 1 - n, 0, 0)),
                  pl.BlockSpec((2, d), lambda n: (0, 0)),
                  pl.BlockSpec((n_tab, CHUNK), lambda n: (0, 0)),
                  pl.BlockSpec((CHUNK, 2 * CHUNK), lambda n: (0, 0)), ANY],
        out_specs=[pl.BlockSpec((CHUNK, 3 * d), lambda n: (nc - 1 - n, 1)), pl.BlockSpec((2, d), lambda n: (0, 0))],
        out_shape=[jax.ShapeDtypeStruct(dproj.shape, BF16), jax.ShapeDtypeStruct((2, d), F32)],
        input_output_aliases={8: 0},
        scratch_shapes=[pltpu.VMEM((d, HEAD), F32), pltpu.VMEM((n_tab, d), F32), pltpu.VMEM((1, d), F32)],
        name="hgrn_bwd", compiler_params=_params())(proj, proj, proj, do, s_all, lb_logits, tab, utri, dproj)


def _conv_bwd(dc, proj, conv_w, dz, dproj):
    lp, d = dc.shape
    tm = _row_tile(lp)
    hb = tm // HALO
    n_tiles = lp // tm
    last_halo = lp // HALO - 1

    def body(dc_ref, dcn_ref, ua_ref, ub_ref, uap_ref, ubp_ref, cw_ref, dz_ref, _,
             dp_ref, dcw_ref, dcb_ref, aext_ref, dcext_ref, da_ref, dcw_acc):
        i = pl.program_id(0)

        @pl.when(i == 0)
        def _():
            dcw_acc[...] = jnp.zeros_like(dcw_acc)
            dcb_ref[...] = jnp.zeros_like(dcb_ref)

        ua = ua_ref[...]
        sb = _sigmoid(ub_ref[...])
        a_prev = uap_ref[...] * _sigmoid(ubp_ref[...])
        aext_ref[0:HALO, :] = jnp.where(i > 0, a_prev, 0.0)
        aext_ref[HALO:HALO + tm, :] = ua * sb
        dcext_ref[0:tm, :] = dc_ref[...]
        dcext_ref[tm:tm + HALO, :] = jnp.where(i < n_tiles - 1, dcn_ref[...], 0.0)
        dcb_ref[...] += jnp.sum(dc_ref[...], axis=0, keepdims=True)

        def row_block(r, carry):
            r0 = pl.multiple_of(r * CONV_ROWS, CONV_ROWS)
            n_rows = CONV_ROWS + HALO
            for cs in range(d // CONV_LANES):
                cl = slice(cs * CONV_LANES, (cs + 1) * CONV_LANES)
                dblk = dcext_ref[pl.ds(r0, n_rows), cl]
                ablk = aext_ref[pl.ds(r0, n_rows), cl]
                dcur = dblk[0:CONV_ROWS, :]
                acc = jnp.zeros((CONV_ROWS, CONV_LANES), F32)
                for b in range(8):
                    dsh = dblk if b == 0 else pltpu.roll(dblk, n_rows - b, axis=0)
                    ash = ablk if b == 0 else pltpu.roll(ablk, n_rows - b, axis=0)
                    for a in range(5):
                        j_da = CONV_WIDTH - 1 - (8 * a + b)
                        if 0 <= j_da < CONV_WIDTH:
                            acc = acc + cw_ref[j_da:j_da + 1, cl] * dsh[8 * a:8 * a + CONV_ROWS, :]
                        j_w = 8 * a + b - 2
                        if 0 <= j_w < CONV_WIDTH:
                            prod = dcur * ash[8 * a:8 * a + CONV_ROWS, :]
                            dcw_acc[j_w, :, cl] += prod.reshape(CONV_ROWS // 8, 8, CONV_LANES).sum(axis=0)
                da_ref[pl.ds(r0, CONV_ROWS), cl] = acc
            return carry

        lax.fori_loop(0, tm // CONV_ROWS, row_block, 0)

        da = da_ref[...]
        dp_ref[:, 0:d] = (da * sb).astype(BF16)
        dp_ref[:, d:2 * d] = (da * ua * sb * (1.0 - sb)).astype(BF16)
        dp_ref[:, 2 * d:3 * d] = dz_ref[...]

        @pl.when(i == n_tiles - 1)
        def _():
            dcw_ref[...] = jnp.sum(dcw_acc[...], axis=1)

    row = lambda p: pl.BlockSpec((tm, d), lambda i, p=p: (i, p))
    prev = lambda p: pl.BlockSpec((HALO, d), lambda i, p=p: (jnp.maximum(i * hb - 1, 0), p))
    nxt = pl.BlockSpec((HALO, d), lambda i: (jnp.minimum((i + 1) * hb, last_halo), 0))
    return pl.pallas_call(
        body, grid=(n_tiles,),
        in_specs=[row(0), nxt, row(0), row(1), prev(0), prev(1), pl.BlockSpec((HALO, d), lambda i: (0, 0)),
                  row(0), ANY],
        out_specs=[pl.BlockSpec((tm, 3 * d), lambda i: (i, 0)), pl.BlockSpec((HALO, d), lambda i: (0, 0)),
                   pl.BlockSpec((1, d), lambda i: (0, 0))],
        out_shape=[jax.ShapeDtypeStruct(dproj.shape, BF16),
                   jax.ShapeDtypeStruct((HALO, d), F32), jax.ShapeDtypeStruct((1, d), F32)],
        input_output_aliases={8: 0},
        scratch_shapes=[pltpu.VMEM((HALO + tm, d), F32), pltpu.VMEM((tm + HALO, d), F32), pltpu.VMEM((tm, d), F32),
                        pltpu.VMEM((HALO, 8, d), F32)],
        name="conv_bwd", compiler_params=_params())(dc, dc, proj, proj, proj, proj, conv_w, dz, dproj)


def _weight_grad(xs, dy, name, blocked):
    lp, dx = xs.shape
    n = dy.shape[1]
    tk = _mm_row_tile(lp)
    if blocked:
        ncol = n // N_CHIPS
        nt = W_IN_COL_TILES
        tn = ncol // nt
        grid = (N_CHIPS * nt, lp // tk)
        out_spec = pl.BlockSpec((1, dx, tn), lambda c, k: (c // nt, 0, c % nt))
        out_shape = jax.ShapeDtypeStruct((N_CHIPS, dx, ncol), F32)
    else:
        tn = n // 2
        grid = (2, lp // tk)
        out_spec = pl.BlockSpec((dx, tn), lambda c, k: (0, c))
        out_shape = jax.ShapeDtypeStruct((dx, n), F32)

    def body(xs_ref, dy_ref, o_ref):
        @pl.when(pl.program_id(1) == 0)
        def _():
            o_ref[...] = jnp.zeros_like(o_ref)

        p = lax.dot_general(xs_ref[...], dy_ref[...], TN, preferred_element_type=F32)
        if blocked:
            o_ref[0] += p
        else:
            o_ref[...] += p

    return pl.pallas_call(
        body, grid=grid,
        in_specs=[pl.BlockSpec((tk, dx), lambda c, k: (k, 0)), pl.BlockSpec((tk, tn), lambda c, k: (k, c))],
        out_specs=out_spec, out_shape=out_shape,
        name=name, compiler_params=_params())(xs, dy)


def _in_proj_bwd(dproj, wtg, hres, norm_g, dout):
    lp, d = hres.shape
    _, ncol, _ = wtg.shape
    tm = _mm_row_tile(lp)
    nt = W_IN_COL_TILES
    tn = ncol // nt
    nk = N_CHIPS * nt

    def body(dp_ref, w_ref, x_ref, g_ref, dout_ref, dx_ref, dg_ref, acc_ref):
        i = pl.program_id(0)
        kk = pl.program_id(1)

        @pl.when(jnp.logical_and(i == 0, kk == 0))
        def _():
            dg_ref[...] = jnp.zeros_like(dg_ref)

        @pl.when(kk == 0)
        def _():
            acc_ref[...] = jnp.zeros_like(acc_ref)

        acc_ref[...] += jnp.dot(dp_ref[...], w_ref[0], preferred_element_type=F32)

        @pl.when(kk == nk - 1)
        def _():
            x = x_ref[...]
            r = lax.rsqrt(jnp.mean(x * x, axis=-1, keepdims=True) + EPS)
            xhat = x * r
            dh = acc_ref[...]
            dg_ref[...] += jnp.sum(dh * xhat, axis=0, keepdims=True)
            dxh = dh * g_ref[...]
            dx_ref[...] = dout_ref[...] + r * (dxh - xhat * jnp.mean(dxh * xhat, axis=-1, keepdims=True))

    return pl.pallas_call(
        body, grid=(lp // tm, nk),
        in_specs=[pl.BlockSpec((tm, tn), lambda i, k: (i, k)),
                  pl.BlockSpec((1, tn, d), lambda i, k: (k // nt, k % nt, 0)),
                  pl.BlockSpec((tm, d), lambda i, k: (i, 0)),
                  pl.BlockSpec((1, d), lambda i, k: (0, 0)),
                  pl.BlockSpec((tm, d), lambda i, k: (i, 0))],
        out_specs=[pl.BlockSpec((tm, d), lambda i, k: (i, 0)), pl.BlockSpec((1, d), lambda i, k: (0, 0))],
        out_shape=[jax.ShapeDtypeStruct((lp, d), F32), jax.ShapeDtypeStruct((1, d), F32)],
        scratch_shapes=[pltpu.VMEM((tm, d), F32)],
        name="in_proj_bwd", compiler_params=_params())(dproj, wtg, hres, norm_g, dout)


def _adamw_math(w, g, m, v):
    m = ADAM_B1 * m + (1.0 - ADAM_B1) * g
    v = ADAM_B2 * v + (1.0 - ADAM_B2) * (g * g)
    m_hat = m / (1.0 - ADAM_B1 ** ADAM_STEP)
    v_hat = v / (1.0 - ADAM_B2 ** ADAM_STEP)
    delta = -ADAM_LR * (m_hat / (jnp.sqrt(v_hat) + ADAM_EPS) + ADAM_WD * w)
    return delta, m, v


def _elementwise_rows(shape):
    r, c = shape
    for t in (256, 128, 64, 32, 16, 8):
        if r % t == 0 and r > t and t * c * 4 <= 1024 * 1024:
            return t
    return r


def _adamw(name, w, m, v, *g_parts):
    shape = w.shape
    tr = _elementwise_rows(shape)
    n_g = len(g_parts)

    def body(*refs):
        w_ref, m_ref, v_ref = refs[:3]
        g_refs = refs[3:3 + n_g]
        g_out, d_out, m_out, v_out = refs[3 + n_g:]
        g = g_refs[0][...]
        for gr in g_refs[1:]:
            g = g + gr[...]
        delta, m_new, v_new = _adamw_math(w_ref[...], g, m_ref[...], v_ref[...])
        g_out[...] = g
        d_out[...] = delta
        m_out[...] = m_new
        v_out[...] = v_new

    spec = pl.BlockSpec((tr, shape[1]), lambda i: (i, 0))
    return pl.pallas_call(
        body, grid=(shape[0] // tr,),
        in_specs=[spec] * (3 + n_g), out_specs=[spec] * 4,
        out_shape=[jax.ShapeDtypeStruct(shape, F32)] * 4,
        name=name, compiler_params=_params())(w, m, v, *g_parts)


def _chip_half_sum(name, g, recv, core):
    _, _, hr, cols = g.shape
    tr = _elementwise_rows((hr, cols))

    def body(core_ref, g_ref, r_ref, o_ref, ob_ref):
        s = g_ref[0, 0] + r_ref[0]
        o_ref[0] = s
        ob_ref[0] = s.astype(BF16)

    blk = pl.BlockSpec((1, tr, cols), lambda j, i, core_ref: (j, i, 0))
    grid_spec = pltpu.PrefetchScalarGridSpec(
        num_scalar_prefetch=1, grid=(N_CHIPS, hr // tr),
        in_specs=[pl.BlockSpec((1, 1, tr, cols), lambda j, i, core_ref: (j, core_ref[0], i, 0)), blk],
        out_specs=[blk, blk])
    return pl.pallas_call(
        body, grid_spec=grid_spec,
        out_shape=[jax.ShapeDtypeStruct((N_CHIPS, hr, cols), F32), jax.ShapeDtypeStruct((N_CHIPS, hr, cols), BF16)],
        name=name, compiler_params=_params())(core, g, recv)


def _block_half_total(name, chip_sums, recv, chip_core):
    _, hr, cols = chip_sums.shape
    tr = _elementwise_rows((hr, cols))

    def body(cc_ref, p_ref, r_ref, o_ref):
        s = p_ref[0]
        for k in range(3):
            s = s + r_ref[k].astype(F32)
        o_ref[0] = s

    grid_spec = pltpu.PrefetchScalarGridSpec(
        num_scalar_prefetch=1, grid=(hr // tr,),
        in_specs=[pl.BlockSpec((1, tr, cols), lambda i, cc_ref: (cc_ref[0], i, 0)),
                  pl.BlockSpec((3, tr, cols), lambda i, cc_ref: (0, i, 0))],
        out_specs=pl.BlockSpec((1, tr, cols), lambda i, cc_ref: (cc_ref[1], i, 0)))
    return pl.pallas_call(
        body, grid_spec=grid_spec, out_shape=jax.ShapeDtypeStruct((2, hr, cols), F32),
        name=name, compiler_params=_params())(chip_core, chip_sums, recv)


def _place_shard(name, w, chip, dtype):
    r, c = w.shape
    tr = _elementwise_rows((r, c))

    def body(chip_ref, w_ref, o_ref):
        o_ref[0] = w_ref[...].astype(dtype)

    grid_spec = pltpu.PrefetchScalarGridSpec(
        num_scalar_prefetch=1, grid=(r // tr,),
        in_specs=[pl.BlockSpec((tr, c), lambda i, chip_ref: (i, 0))],
        out_specs=pl.BlockSpec((1, tr, c), lambda i, chip_ref: (chip_ref[0], i, 0)))
    return pl.pallas_call(
        body, grid_spec=grid_spec, out_shape=jax.ShapeDtypeStruct((N_CHIPS, r, c), dtype),
        name=name, compiler_params=_params())(chip, w)


def _sum_slots(name, slots):
    k, r, c = slots.shape

    def body(s_ref, o_ref):
        s = s_ref[0]
        for j in range(1, k):
            s = s + s_ref[j]
        o_ref[...] = s

    return pl.pallas_call(body, out_shape=jax.ShapeDtypeStruct((r, c), F32), name=name,
                          compiler_params=_params())(slots)


def _mesh_pos():
    return lax.axis_index("x"), lax.axis_index("y"), lax.axis_index("c")


def _other_chips(x, y):
    return [(1 - x, y), (x, 1 - y), (1 - x, 1 - y)]


def _gather_weights(bufs):
    n = len(bufs)
    half = [b.shape[1] // 2 for b in bufs]

    def body(*refs):
        gathered = refs[n:2 * n]
        ici_send, ici_recv, d2d_send, d2d_recv = refs[2 * n:]
        x, y, c = _mesh_pos()
        me = 2 * x + y
        chips = _other_chips(x, y)

        def part(a, block, core):
            return gathered[a].at[block, pl.ds(core * half[a], half[a])]

        def over_ici(a, k, block):
            px, py = chips[k]
            return pltpu.make_async_remote_copy(
                src_ref=part(a, block, c), dst_ref=part(a, block, c),
                send_sem=ici_send.at[a, k], recv_sem=ici_recv.at[a, k],
                device_id=(px, py, c), device_id_type=MESH)

        def over_d2d(a, k, core):
            px, py = chips[k]
            return pltpu.make_async_remote_copy(
                src_ref=part(a, 2 * px + py, core), dst_ref=part(a, 2 * px + py, core),
                send_sem=d2d_send.at[a, k], recv_sem=d2d_recv.at[a, k],
                device_id=(x, y, 1 - c), device_id_type=MESH)

        for a in range(n):
            for k in range(3):
                over_ici(a, k, me).start()
        for a in range(n):
            for k, (px, py) in enumerate(chips):
                over_ici(a, k, 2 * px + py).wait_recv()
                over_d2d(a, k, c).start()
        for a in range(n):
            for k in range(3):
                over_d2d(a, k, 1 - c).wait_recv()
        for a in range(n):
            for k in range(3):
                over_ici(a, k, me).wait_send()
                over_d2d(a, k, c).wait_send()

    return pl.pallas_call(
        body, in_specs=[ANY] * n, out_specs=[ANY] * n,
        out_shape=[jax.ShapeDtypeStruct(b.shape, b.dtype) for b in bufs],
        input_output_aliases={a: a for a in range(n)},
        scratch_shapes=[pltpu.SemaphoreType.DMA((n, 3))] * 4,
        name="gather_weights")(*bufs)


def _gather_in_proj(h, bufs, order):
    n = len(bufs)
    half = [b.shape[1] // 2 for b in bufs]
    lp, d = h.shape
    ncol = bufs[0].shape[2]
    tm = _mm_row_tile(lp)
    n_row = lp // tm

    def body(order_ref, h_ref, *refs):
        gathered = refs[n:2 * n]
        o_ref, wt_ref = refs[2 * n], refs[2 * n + 1]
        w_buf, ici_send, ici_recv, d2d_send, d2d_recv, w_sem = refs[2 * n + 2:]
        j = pl.program_id(0)
        i = pl.program_id(1)
        x, y, c = _mesh_pos()
        me = 2 * x + y
        chips = _other_chips(x, y)

        def part(a, block, core):
            return gathered[a].at[block, pl.ds(core * half[a], half[a])]

        def over_ici(a, k, block):
            px, py = chips[k]
            return pltpu.make_async_remote_copy(
                src_ref=part(a, block, c), dst_ref=part(a, block, c),
                send_sem=ici_send.at[a, k], recv_sem=ici_recv.at[a, k],
                device_id=(px, py, c), device_id_type=MESH)

        def over_d2d(a, k, core):
            px, py = chips[k]
            return pltpu.make_async_remote_copy(
                src_ref=part(a, 2 * px + py, core), dst_ref=part(a, 2 * px + py, core),
                send_sem=d2d_send.at[a, k], recv_sem=d2d_recv.at[a, k],
                device_id=(x, y, 1 - c), device_id_type=MESH)

        @pl.when(jnp.logical_and(j == 0, i == 0))
        def _():
            for a in range(n):
                for k in range(2):
                    over_ici(a, k, me).start()

        for k, (px, py) in enumerate(chips):
            @pl.when(jnp.logical_and(j == k + 1, i == 0))
            def _(k=k, px=px, py=py):
                for a in range(n):
                    over_ici(a, k, 2 * px + py).wait_recv()
                    over_d2d(a, k, c).start()
                if k == 0:
                    for a in range(n):
                        over_ici(a, 2, me).start()
                for a in range(n):
                    over_d2d(a, k, 1 - c).wait_recv()

        @pl.when(i == 0)
        def _():
            load = pltpu.make_async_copy(gathered[0].at[order_ref[j]], w_buf, w_sem)
            load.start()
            load.wait()
            wt_ref[0] = w_buf[...].T

        o_ref[...] = jnp.dot(h_ref[...], w_buf[...], preferred_element_type=F32)

        @pl.when(jnp.logical_and(j == N_CHIPS - 1, i == n_row - 1))
        def _():
            for a in range(n):
                for k in range(3):
                    over_ici(a, k, me).wait_send()
                    over_d2d(a, k, c).wait_send()

    grid_spec = pltpu.PrefetchScalarGridSpec(
        num_scalar_prefetch=1, grid=(N_CHIPS, n_row),
        in_specs=[pl.BlockSpec((tm, d), lambda j, i, order_ref: (i, 0))] + [ANY] * n,
        out_specs=[ANY] * n + [pl.BlockSpec((tm, ncol), lambda j, i, order_ref: (i, order_ref[j])),
                               pl.BlockSpec((1, ncol, d), lambda j, i, order_ref: (order_ref[j], 0, 0))],
        scratch_shapes=[pltpu.VMEM((d, ncol), BF16)] + [pltpu.SemaphoreType.DMA((n, 3))] * 4
        + [pltpu.SemaphoreType.DMA])
    out = pl.pallas_call(
        body, grid_spec=grid_spec,
        out_shape=[jax.ShapeDtypeStruct(b.shape, b.dtype) for b in bufs]
        + [jax.ShapeDtypeStruct((lp, N_CHIPS * ncol), F32), jax.ShapeDtypeStruct((N_CHIPS, ncol, d), BF16)],
        input_output_aliases={2 + a: a for a in range(n)},
        name="gather_in_proj", compiler_params=_params())(order, h, *bufs)
    return out[n], out[n + 1], out[:n]


def _send_other_halves(grads, tag):
    n = len(grads)

    def body(*refs):
        srcs = refs[:n]
        dsts = refs[n:2 * n]
        send_sems, recv_sems = refs[2 * n:]
        x, y, c = _mesh_pos()
        copies = [pltpu.make_async_remote_copy(
            src_ref=srcs[a].at[j, 1 - c], dst_ref=dsts[a].at[j], send_sem=send_sems.at[a, j],
            recv_sem=recv_sems.at[a, j], device_id=(x, y, 1 - c), device_id_type=MESH)
            for a in range(n) for j in range(N_CHIPS)]
        for cp in copies:
            cp.start()
        for cp in copies:
            cp.wait()

    return pl.pallas_call(
        body, in_specs=[ANY] * n, out_specs=[ANY] * n,
        out_shape=[jax.ShapeDtypeStruct((N_CHIPS,) + g.shape[2:], F32) for g in grads],
        scratch_shapes=[pltpu.SemaphoreType.DMA((n, N_CHIPS))] * 2,
        name="send_other_halves_" + tag)(*grads)


HBM = pl.BlockSpec(memory_space=pltpu.HBM)
SEM = pl.BlockSpec(memory_space=pltpu.SEMAPHORE)


def _block_copies(n, srcs, dsts, send_sems, recv_sems):
    x, y, c = _mesh_pos()
    return [pltpu.make_async_remote_copy(
        src_ref=srcs[a].at[2 * px + py], dst_ref=dsts[a].at[k], send_sem=send_sems.at[3 * a + k],
        recv_sem=recv_sems.at[3 * a + k], device_id=(px, py, c), device_id_type=MESH)
        for a in range(n) for k, (px, py) in enumerate(_other_chips(x, y))]


def _exchange_start(blocked, tag):
    n = len(blocked)
    lands = [lax.empty((3,) + b.shape[1:], b.dtype) for b in blocked]
    bufs = [pltpu.with_memory_space_constraint(b, pltpu.HBM) for b in list(blocked) + lands]
    nb = 2 * n

    def body(*refs):
        for cp in _block_copies(n, refs[:n], refs[n:nb], refs[nb], refs[nb + 1]):
            cp.start()
        refs[-1][...] = jnp.zeros_like(refs[-1])

    out = pl.pallas_call(
        body, name="exchange_start_" + tag,
        in_specs=[HBM] * nb,
        out_shape=[pltpu.SemaphoreType.DMA((3 * n,)), pltpu.SemaphoreType.DMA((3 * n,))]
        + [pltpu.HBM(b.shape, b.dtype) for b in bufs] + [jax.ShapeDtypeStruct((8, 128), F32)],
        out_specs=[SEM] * 2 + [HBM] * nb + [pl.BlockSpec(memory_space=pltpu.VMEM)],
        input_output_aliases={i: 2 + i for i in range(nb)},
        compiler_params=pltpu.CompilerParams(has_side_effects=pltpu.SideEffectType.DATAFLOW_SIDE_EFFECTING),
    )(*bufs)
    return (out[:2], out[2:2 + nb]), out[-1]


def _exchange_wait(state, after, tag):
    sems, bufs = state
    nb = len(bufs)
    n = nb // 2

    def body(*refs):
        for cp in _block_copies(n, refs[:n], refs[n:nb], refs[nb], refs[nb + 1]):
            cp.wait_send()
            cp.wait_recv()

    out = pl.pallas_call(
        body, name="exchange_wait_" + tag,
        in_specs=[HBM] * nb + [SEM] * 2 + [ANY],
        out_shape=[pltpu.HBM(b.shape, b.dtype) for b in bufs],
        out_specs=[HBM] * nb,
        input_output_aliases={i: i for i in range(nb)},
        compiler_params=pltpu.CompilerParams(has_side_effects=pltpu.SideEffectType.DATAFLOW_SIDE_EFFECTING),
    )(*bufs, *sems, after)
    return out[n:nb]


def _exchange_small(small):
    def body(small_src, small_dst, ssend_sems, srecv_sems, local_sem):
        x, y, c = _mesh_pos()
        my_idx = 4 * x + 2 * y + c
        local = pltpu.make_async_copy(small_src, small_dst.at[my_idx], local_sem)
        local.start()
        others = []
        for r in range(1, 8):
            px = 1 - x if r & 4 else x
            py = 1 - y if r & 2 else y
            pc = 1 - c if r & 1 else c
            others.append((px, py, pc))
        for r, peer in enumerate(others):
            pltpu.make_async_remote_copy(
                src_ref=small_src, dst_ref=small_dst.at[my_idx], send_sem=ssend_sems.at[r],
                recv_sem=srecv_sems.at[r], device_id=peer, device_id_type=MESH).start()
        for r, (px, py, pc) in enumerate(others):
            pltpu.make_async_remote_copy(
                src_ref=small_src, dst_ref=small_dst.at[4 * px + 2 * py + pc], send_sem=ssend_sems.at[r],
                recv_sem=srecv_sems.at[r], device_id=(px, py, pc), device_id_type=MESH).wait()
        local.wait()

    return pl.pallas_call(
        body, in_specs=[ANY], out_specs=ANY, out_shape=jax.ShapeDtypeStruct((8,) + small.shape, F32),
        scratch_shapes=[pltpu.SemaphoreType.DMA((7,)), pltpu.SemaphoreType.DMA((7,)), pltpu.SemaphoreType.DMA],
        name="exchange_small")(small)


def _join_halves(bufs):
    n = len(bufs)

    def body(*refs):
        joined = refs[n:2 * n]
        send_sems, recv_sems = refs[2 * n:]
        x, y, c = _mesh_pos()
        for a in range(n):
            pltpu.make_async_remote_copy(
                src_ref=joined[a].at[c], dst_ref=joined[a].at[c], send_sem=send_sems.at[a],
                recv_sem=recv_sems.at[a], device_id=(x, y, 1 - c), device_id_type=MESH).start()
        for a in range(n):
            pltpu.make_async_remote_copy(
                src_ref=joined[a].at[c], dst_ref=joined[a].at[1 - c], send_sem=send_sems.at[a],
                recv_sem=recv_sems.at[a], device_id=(x, y, 1 - c), device_id_type=MESH).wait()

    return pl.pallas_call(
        body, in_specs=[ANY] * n, out_specs=[ANY] * n,
        out_shape=[jax.ShapeDtypeStruct(b.shape, b.dtype) for b in bufs],
        input_output_aliases={a: a for a in range(n)},
        scratch_shapes=[pltpu.SemaphoreType.DMA((n,))] * 2,
        name="join_halves")(*bufs)


def kernel(x, meta_tokens, norm_g, w_in, conv_w, conv_b, ln_g, ln_b, w_conv_out, lb_logits, gnorm_g, w_rec_out, w_out, final_g, loss_target, m_meta_tokens, m_norm_g, m_w_in, m_conv_w, m_conv_b, m_ln_g, m_ln_b, m_w_conv_out, m_lb_logits, m_gnorm_g, m_w_rec_out, m_w_out, m_final_g, v_meta_tokens, v_norm_g, v_w_in, v_conv_w, v_conv_b, v_ln_g, v_ln_b, v_w_conv_out, v_lb_logits, v_gnorm_g, v_w_rec_out, v_w_out, v_final_g):
    seq, d = x.shape[1], x.shape[2]
    n_meta = meta_tokens.shape[0]
    n_pad = CHUNK - n_meta
    ds = d // N_CHIPS
    chip = 2 * lax.axis_index("x") + lax.axis_index("y")

    conv_w_pad = jnp.pad(conv_w[0], ((0, HALO - CONV_WIDTH), (0, 0)))
    chip_idx = chip.astype(jnp.int32).reshape(1)
    (small_g,) = _gather_weights([
        _place_shard("place_small", jnp.concatenate([conv_w_pad, meta_tokens], axis=0), chip_idx, F32)])
    cw_full = jnp.transpose(small_g[:, 0:HALO], (1, 0, 2)).reshape(HALO, d)
    meta_full = jnp.transpose(small_g[:, HALO:HALO + n_meta], (1, 0, 2)).reshape(n_meta, d)

    hres = jnp.concatenate([jnp.zeros((n_pad, d), F32), meta_full, x[0]], axis=0)
    target = loss_target[0]
    final_g2 = final_g.reshape(1, d)
    h = _rmsnorm_fwd(hres, norm_g)
    fx, fy = 1 - lax.axis_index("x"), 1 - lax.axis_index("y")
    order = jnp.stack([chip, 2 * fx + (1 - fy), 2 * (1 - fx) + fy, 2 * fx + fy]).astype(jnp.int32)
    proj, win_t, (win_g, sq_g) = _gather_in_proj(h, [
        _place_shard("place_w_in", w_in[0], chip_idx, BF16),
        _place_shard("place_square", jnp.concatenate([w_conv_out[0], w_rec_out[0], w_out[0]], axis=0),
                     chip_idx, BF16)], order)
    wc_full = sq_g[:, 0:ds].reshape(d, d)
    wr_full = sq_g[:, ds:2 * ds].reshape(d, d)
    wo_full = sq_g[:, 2 * ds:3 * ds].reshape(d, d)
    c, yc_in, y_conv = _conv_fwd(proj, cw_full, conv_b, ln_g, ln_b, wc_full)
    o, s_all = _hgrn_fwd(proj, lb_logits, n_pad)
    yr_in, merged, y_rec, dout, loss_acc, dfinal_g = _tail_fwd(
        o, proj, y_conv, hres, target, gnorm_g, final_g2, wr_full, wo_full)

    (dyc, dyr, dout_bf, dz, dproj, do, dc, dgnorm_g, dln_g, dln_b) = _tail_bwd(
        dout, proj, y_conv, y_rec, o, c, wo_full, wr_full, wc_full, ln_g, ln_b, gnorm_g)
    g_wc = _weight_grad(yc_in, dyc, "grad_w_conv_out", False)
    g_wr = _weight_grad(yr_in, dyr, "grad_w_rec_out", False)
    g_wo = _weight_grad(merged, dout_bf, "grad_w_out", False)

    core = lax.axis_index("c").astype(jnp.int32).reshape(1)

    def chip_sum_and_start(g, tag):
        g = g.reshape(N_CHIPS, 2, g.shape[1] // 2, g.shape[2])
        (from_sibling,) = _send_other_halves([g], tag)
        sums = _chip_half_sum("chip_half_sum_" + tag, g, from_sibling, core)
        in_flight, token = _exchange_start([sums[1]], tag)
        return sums[0], in_flight, token[0:1, 0:1]

    g_sq = jnp.concatenate([g.reshape(N_CHIPS, ds, d) for g in (g_wc, g_wr, g_wo)], axis=1)
    sum_sq, flight_sq, token_sq = chip_sum_and_start(g_sq, "square")
    dproj, dlb_logits = _hgrn_bwd(proj, do, s_all, lb_logits + token_sq, n_pad, dproj)
    dproj, dconv_w, dconv_b = _conv_bwd(dc, proj, cw_full, dz, dproj)
    (recv_sq,) = _exchange_wait(flight_sq, dconv_b, "square")
    g_win = _weight_grad(h, dproj, "grad_w_in", True)
    sum_win, flight_win, token_win = chip_sum_and_start(g_win, "w_in")
    dhres, dnorm_g = _in_proj_bwd(dproj, win_t, hres, norm_g + token_win, dout)
    grad_x = dhres[CHUNK:][None]
    (recv_win,) = _exchange_wait(flight_win, dnorm_g, "w_in")
    small = jnp.concatenate([dnorm_g, dconv_b, dln_g, dln_b, dlb_logits, dgnorm_g, dfinal_g,
                             dhres[n_pad:CHUNK], dconv_w[:CONV_WIDTH],
                             jnp.broadcast_to(loss_acc[0:1, 0:1], (1, d))], axis=0)
    small_slots = _exchange_small(small)
    chip_core = jnp.concatenate([chip_idx, core])
    totals = [_block_half_total("block_half_total_" + nm, s, r, chip_core)
              for nm, s, r in zip(("w_in", "square"), (sum_win, sum_sq), (recv_win, recv_sq))]
    gt_win, gt_sq = [t.reshape(2 * t.shape[1], t.shape[2]) for t in _join_halves(totals)]
    small_sum = _sum_slots("sum_small", small_slots)

    res = {}
    res["w_in"] = _adamw("adamw_w_in", w_in[0], m_w_in[0], v_w_in[0], gt_win)
    res["w_conv_out"] = _adamw("adamw_w_conv_out", w_conv_out[0], m_w_conv_out[0], v_w_conv_out[0], gt_sq[0:ds])
    res["w_rec_out"] = _adamw("adamw_w_rec_out", w_rec_out[0], m_w_rec_out[0], v_w_rec_out[0], gt_sq[ds:2 * ds])
    res["w_out"] = _adamw("adamw_w_out", w_out[0], m_w_out[0], v_w_out[0], gt_sq[2 * ds:3 * ds])
    big = {k: tuple(a[None] for a in v) for k, v in res.items()}

    rep_names = ("norm_g", "conv_b", "ln_g", "ln_b", "lb_logits", "gnorm_g", "final_g")
    rep_w = (norm_g, conv_b, ln_g, ln_b, lb_logits, gnorm_g, final_g2)
    rep_m = (m_norm_g, m_conv_b, m_ln_g, m_ln_b, m_lb_logits, m_gnorm_g, m_final_g.reshape(1, d))
    rep_v = (v_norm_g, v_conv_b, v_ln_g, v_ln_b, v_lb_logits, v_gnorm_g, v_final_g.reshape(1, d))
    rep = _adamw("adamw_replicated", jnp.concatenate(rep_w, 0), jnp.concatenate(rep_m, 0),
                 jnp.concatenate(rep_v, 0), small_sum[0:8])
    rep_rows = {"norm_g": (0, 1), "conv_b": (1, 2), "ln_g": (2, 3), "ln_b": (3, 4), "lb_logits": (4, 6),
                "gnorm_g": (6, 7), "final_g": (7, 8)}
    small_out = {}
    for nm in rep_names:
        lo, hi = rep_rows[nm]
        vals = tuple(a[lo:hi] for a in rep)
        if nm == "final_g":
            vals = tuple(a.reshape(d) for a in vals)
        small_out[nm] = vals
    cw_row = 8 + n_meta
    g_meta = lax.dynamic_slice_in_dim(small_sum[8:cw_row], chip * ds, ds, axis=1)
    small_out["meta_tokens"] = _adamw("adamw_meta", meta_tokens, m_meta_tokens, v_meta_tokens, g_meta)
    g_cw = lax.dynamic_slice_in_dim(small_sum[cw_row:cw_row + HALO], chip * ds, ds, axis=1)
    pad_rows = ((0, HALO - CONV_WIDTH), (0, 0))
    cw_res = _adamw("adamw_conv_w", conv_w_pad, jnp.pad(m_conv_w[0], pad_rows),
                    jnp.pad(v_conv_w[0], pad_rows, constant_values=1.0), g_cw)
    small_out["conv_w"] = tuple(a[:CONV_WIDTH][None] for a in cw_res)

    loss = small_sum[cw_row + HALO - 1, 0]

    order = ("meta_tokens", "norm_g", "w_in", "conv_w", "conv_b", "ln_g", "ln_b", "w_conv_out", "lb_logits",
             "gnorm_g", "w_rec_out", "w_out", "final_g")
    allres = {**big, **small_out}
    outs = [loss, grad_x]
    for field in range(4):
        outs.extend(allres[nm][field] for nm in order)
    return tuple(outs)
```

```python
import numpy as np

import jax
import jax.numpy as jnp
from jax import lax
from jax.experimental import pallas as pl
from jax.experimental.pallas import tpu as pltpu

F32 = jnp.float32
BF16 = jnp.bfloat16

EPS = 1e-6
CHUNK = 64
N_LEVELS = 6
FIRST_TABLE_LEVEL = 5
CONV_WIDTH = 31
HALO = 32
CONV_ROWS = 32
CONV_LANES = 256
HEAD = 128
W_IN_COL_TILES = 1
HEADS_PER_TRIP = 8
N_CHIPS = 4
VMEM_LIMIT_BYTES = 56 * 1024 * 1024

ADAM_LR = 0.001
ADAM_B1 = 0.9
ADAM_B2 = 0.999
ADAM_EPS = 1e-08
ADAM_WD = 0.01
ADAM_STEP = 10

MESH = pl.DeviceIdType.MESH
ANY = pl.BlockSpec(memory_space=pl.ANY)

NT = (((1,), (1,)), ((), ()))
TN = (((0,), (0,)), ((), ()))


def _params(**kw):
    return pltpu.CompilerParams(vmem_limit_bytes=VMEM_LIMIT_BYTES, **kw)


def _sigmoid(x):
    return jax.nn.sigmoid(x)


def _dsilu(x, s):
    return s * (1.0 + x * (1.0 - s))


def _row_tile(lp):
    for t in (320, 256, 192, 128, 64):
        if lp % t == 0:
            return t
    raise ValueError(f"unsupported padded length {lp}")


def _mm_row_tile(lp):
    for t in (832, 640, 320, 256, 192, 128, 64):
        if lp % t == 0:
            return t
    raise ValueError(f"unsupported padded length {lp}")


def _dot3(m_bf16, x):
    hi = x.astype(BF16)
    r1 = x - hi.astype(F32)
    mid = r1.astype(BF16)
    lo = (r1 - mid.astype(F32)).astype(BF16)
    return (jnp.dot(m_bf16, hi, preferred_element_type=F32)
            + jnp.dot(m_bf16, mid, preferred_element_type=F32)
            + jnp.dot(m_bf16, lo, preferred_element_type=F32))


def _dot2(m_bf16, x):
    hi = x.astype(BF16)
    lo = (x - hi.astype(F32)).astype(BF16)
    return (jnp.dot(m_bf16, hi, preferred_element_type=F32)
            + jnp.dot(m_bf16, lo, preferred_element_type=F32))


def _col_to_row(col):
    return jnp.broadcast_to(col, (HEAD, 8)).T[0:1, :]


def _row_to_col(row):
    return jnp.broadcast_to(row, (8, HEAD)).T[:, 0:1]


def _hgrn_tables():
    t = np.arange(CHUNK)
    ltri = (t[None, :] <= t[:, None]).astype(np.float32)
    mats = [ltri]
    for lvl in range(FIRST_TABLE_LEVEL, N_LEVELS + 1):
        blk = CHUNK >> (lvl - 1)
        mid = (t // blk) * blk + blk // 2
        mats.append(ltri[mid - 1])
    after = (t[None, :] >= t[:, None]).astype(np.float32)
    before = (t[None, :] < t[:, None]).astype(np.float32)
    return jnp.asarray(np.concatenate(mats, 0), BF16), jnp.asarray(np.concatenate([after, before], 1), BF16)


def _rmsnorm_fwd(hres, g):
    lp, d = hres.shape
    tm = _row_tile(lp)

    def body(x_ref, g_ref, h_ref):
        x = x_ref[...]
        r = lax.rsqrt(jnp.mean(x * x, axis=-1, keepdims=True) + EPS)
        h_ref[...] = (x * r * g_ref[...]).astype(BF16)

    return pl.pallas_call(
        body, grid=(lp // tm,),
        in_specs=[pl.BlockSpec((tm, d), lambda i: (i, 0)), pl.BlockSpec((1, d), lambda i: (0, 0))],
        out_specs=pl.BlockSpec((tm, d), lambda i: (i, 0)),
        out_shape=jax.ShapeDtypeStruct((lp, d), BF16),
        name="rmsnorm_fwd", compiler_params=_params())(hres, g)


def _in_proj(h, wg):
    lp, d = h.shape
    _, _, ncol = wg.shape
    tm = _mm_row_tile(lp)
    nt = W_IN_COL_TILES
    tn = ncol // nt

    def body(h_ref, w_ref, o_ref):
        o_ref[...] = jnp.dot(h_ref[...], w_ref[0], preferred_element_type=F32)

    return pl.pallas_call(
        body, grid=(N_CHIPS, nt, lp // tm),
        in_specs=[pl.BlockSpec((tm, d), lambda j, n, i: (i, 0)),
                  pl.BlockSpec((1, d, tn), lambda j, n, i: (j, 0, n))],
        out_specs=pl.BlockSpec((tm, tn), lambda j, n, i: (i, j * nt + n)),
        out_shape=jax.ShapeDtypeStruct((lp, N_CHIPS * ncol), F32),
        name="in_proj", compiler_params=_params())(h, wg)


def _conv_fwd(proj, conv_w, conv_b, ln_g, ln_b, w_conv):
    lp = proj.shape[0]
    d = conv_b.shape[1]
    tm = _row_tile(lp)
    hb = tm // HALO

    def body(ua_ref, ub_ref, z_ref, uap_ref, ubp_ref, cw_ref, cb_ref, lg_ref, lb_ref, w_ref,
             c_ref, ycin_ref, yconv_ref, aext_ref):
        i = pl.program_id(0)
        a_prev = uap_ref[...] * _sigmoid(ubp_ref[...])
        aext_ref[0:HALO, :] = jnp.where(i > 0, a_prev, 0.0)
        aext_ref[HALO:HALO + tm, :] = ua_ref[...] * _sigmoid(ub_ref[...])

        def row_block(r, carry):
            r0 = pl.multiple_of(r * CONV_ROWS, CONV_ROWS)
            for cs in range(d // CONV_LANES):
                cl = slice(cs * CONV_LANES, (cs + 1) * CONV_LANES)
                blk = aext_ref[pl.ds(r0, CONV_ROWS + HALO), cl]
                acc = jnp.zeros((CONV_ROWS, CONV_LANES), F32) + cb_ref[:, cl]
                for b in range(8):
                    sh = blk if b == 0 else pltpu.roll(blk, CONV_ROWS + HALO - b, axis=0)
                    for a in range(5):
                        j = 8 * a + b - 2
                        if 0 <= j < CONV_WIDTH:
                            acc = acc + cw_ref[j:j + 1, cl] * sh[8 * a:8 * a + CONV_ROWS, :]
                c_ref[pl.ds(r0, CONV_ROWS), cl] = acc
            return carry

        lax.fori_loop(0, tm // CONV_ROWS, row_block, 0)

        c = c_ref[...]
        mu = jnp.mean(c, axis=-1, keepdims=True)
        xc = c - mu
        rstd = lax.rsqrt(jnp.mean(xc * xc, axis=-1, keepdims=True) + EPS)
        ln = xc * rstd * lg_ref[...] + lb_ref[...]
        s = ln * _sigmoid(ln)
        z = z_ref[...]
        ycin = (s * (z * _sigmoid(z))).astype(BF16)
        ycin_ref[...] = ycin
        yconv_ref[...] = jnp.dot(ycin, w_ref[...], preferred_element_type=F32)

    row = lambda p: pl.BlockSpec((tm, d), lambda i, p=p: (i, p))
    halo = lambda p: pl.BlockSpec((HALO, d), lambda i, p=p: (jnp.maximum(i * hb - 1, 0), p))
    vec = pl.BlockSpec((1, d), lambda i: (0, 0))
    return pl.pallas_call(
        body, grid=(lp // tm,),
        in_specs=[row(0), row(1), row(2), halo(0), halo(1),
                  pl.BlockSpec((HALO, d), lambda i: (0, 0)), vec, vec, vec,
                  pl.BlockSpec((d, d), lambda i: (0, 0))],
        out_specs=[pl.BlockSpec((tm, d), lambda i: (i, 0))] * 3,
        out_shape=[jax.ShapeDtypeStruct((lp, d), F32), jax.ShapeDtypeStruct((lp, d), BF16),
                   jax.ShapeDtypeStruct((lp, d), F32)],
        scratch_shapes=[pltpu.VMEM((HALO + tm, d), F32)],
        name="conv_fwd", compiler_params=_params())(
            proj, proj, proj, proj, proj, conv_w, conv_b, ln_g, ln_b, w_conv)


def _lower_bound(lbl_ref):
    l0 = lbl_ref[0:1, :]
    l1 = lbl_ref[1:2, :]
    m = jnp.maximum(l0, l1)
    e0 = jnp.exp(l0 - m)
    e1 = jnp.exp(l1 - m)
    p0 = e0 / (e0 + e1)
    return p0, p0 * (e1 / (e0 + e1))


def _level_masks():
    r2 = lax.broadcasted_iota(jnp.int32, (CHUNK, CHUNK), 0)
    c2 = lax.broadcasted_iota(jnp.int32, (CHUNK, CHUNK), 1)
    out = []
    for lvl in range(1, N_LEVELS + 1):
        blk = CHUNK >> (lvl - 1)
        sh = blk.bit_length() - 1
        same = (r2 >> sh) == (c2 >> sh)
        t_upper = (r2 & (blk - 1)) >= (blk // 2)
        s_lower = (c2 & (blk - 1)) < (blk // 2)
        out.append(jnp.logical_and(same, jnp.logical_and(t_upper, s_lower)))
    return out


def _gates(qr, fr, lb, valid):
    sq = _sigmoid(qr)
    q = qr * sq
    sf = _sigmoid(fr)
    f = lb + (1.0 - lb) * sf
    g = jnp.where(valid, jnp.log(f), 0.0)
    k = jnp.where(valid, 1.0 - f, 0.0)
    return q, sq, f, sf, g, k


def _level_reference(lvl, b, t_ref, hs):
    if lvl >= FIRST_TABLE_LEVEL:
        base = CHUNK * (lvl - FIRST_TABLE_LEVEL + 1)
        return t_ref[base:base + CHUNK, hs]
    blk = CHUNK >> (lvl - 1)
    rows = [jnp.broadcast_to(b[m + blk // 2 - 1:m + blk // 2, :], (blk, HEAD)) for m in range(0, CHUNK, blk)]
    return rows[0] if len(rows) == 1 else jnp.concatenate(rows, axis=0)


def _level_factor(b, r):
    d = b - r
    return jnp.exp(jnp.minimum(d, -d))


def _hgrn_fwd(proj, lb_logits, n_pad):
    lp = proj.shape[0]
    d = lb_logits.shape[1]
    n_heads = d // HEAD
    nc = lp // CHUNK
    tab, _ = _hgrn_tables()
    n_tab = tab.shape[0]

    def body(qr_ref, fr_ref, ir_ref, lbl_ref, tab_ref, o_ref, sall_ref, s_ref, t_ref):
        n = pl.program_id(0)

        @pl.when(n == 0)
        def _():
            s_ref[...] = jnp.zeros_like(s_ref)

        sall_ref[0] = s_ref[...]
        lb_all, _ = _lower_bound(lbl_ref)
        rid = lax.broadcasted_iota(jnp.int32, (CHUNK, 1), 0)
        valid = jnp.logical_or(n > 0, rid >= n_pad)
        f_all = lb_all + (1.0 - lb_all) * _sigmoid(fr_ref[...])
        t_ref[...] = _dot2(tab_ref[...], jnp.where(valid, jnp.log(f_all), 0.0))
        masks = _level_masks()

        def head(h):
            off = h * HEAD if isinstance(h, int) else pl.multiple_of(h * HEAD, HEAD)
            hs = pl.ds(off, HEAD)
            lb = _lower_bound_slice(lbl_ref, hs)
            q, _, _, _, _, k = _gates(qr_ref[:, hs], fr_ref[:, hs], lb, valid)
            v = ir_ref[:, hs]
            b = t_ref[0:CHUNK, hs]
            s0 = s_ref[hs, :]
            o = jnp.dot((q * jnp.exp(b)).astype(BF16), s0.astype(BF16), preferred_element_type=F32)
            o = o + jnp.sum(q * k, axis=-1, keepdims=True) * v
            a = jnp.zeros((CHUNK, CHUNK), F32)
            for lvl in range(1, N_LEVELS + 1):
                e = _level_factor(b, _level_reference(lvl, b, t_ref, hs))
                p = lax.dot_general((q * e).astype(BF16), (k * e).astype(BF16), NT, preferred_element_type=F32)
                a = a + jnp.where(masks[lvl - 1], p, 0.0)
            vb = v.astype(BF16)
            o_ref[:, hs] = o + jnp.dot(a.astype(BF16), vb, preferred_element_type=F32)
            b_last = t_ref[CHUNK - 1:CHUNK, hs]
            khat = (k * jnp.exp(b_last - b)).astype(BF16)
            s_ref[hs, :] = _row_to_col(jnp.exp(b_last)) * s0 + lax.dot_general(khat, vb, TN, preferred_element_type=F32)
        per_trip = min(HEADS_PER_TRIP, n_heads)

        def head_group(p, carry):
            for u in range(per_trip):
                head(p * per_trip + u)
            return carry

        if n_heads == per_trip:
            head_group(0, 0)
        else:
            lax.fori_loop(0, n_heads // per_trip, head_group, 0)

    piece = lambda p: pl.BlockSpec((CHUNK, d), lambda n, p=p: (n, p))
    return pl.pallas_call(
        body, grid=(nc,),
        in_specs=[piece(3), piece(4), piece(5), pl.BlockSpec((2, d), lambda n: (0, 0)),
                  pl.BlockSpec((n_tab, CHUNK), lambda n: (0, 0))],
        out_specs=[pl.BlockSpec((CHUNK, d), lambda n: (n, 0)), pl.BlockSpec((1, d, HEAD), lambda n: (n, 0, 0))],
        out_shape=[jax.ShapeDtypeStruct((lp, d), F32), jax.ShapeDtypeStruct((nc, d, HEAD), F32)],
        scratch_shapes=[pltpu.VMEM((d, HEAD), F32), pltpu.VMEM((n_tab, d), F32)],
        name="hgrn_fwd", compiler_params=_params())(proj, proj, proj, lb_logits, tab)


def _lower_bound_slice(lbl_ref, hs):
    l0 = lbl_ref[0:1, hs]
    l1 = lbl_ref[1:2, hs]
    m = jnp.maximum(l0, l1)
    e0 = jnp.exp(l0 - m)
    e1 = jnp.exp(l1 - m)
    return e0 / (e0 + e1)


def _tail_fwd(o, proj, y_conv, hres, target, gnorm_g, final_g, w_rec, w_out):
    lp, d = o.shape
    n_heads = d // HEAD
    tm = _row_tile(lp)

    n_slabs = tm // CHUNK

    def body(o_ref, gr_ref, mc_ref, mr_ref, yc_ref, x_ref, gn_ref, fg_ref, wr_ref, wo_ref, *rest):
        t_refs = rest[:n_slabs]
        yrin_ref, mg_ref, yrec_ref, dout_ref, loss_ref, dfg_ref = rest[n_slabs:]
        i = pl.program_id(0)

        @pl.when(i == 0)
        def _():
            loss_ref[...] = jnp.zeros_like(loss_ref)
            dfg_ref[...] = jnp.zeros_like(dfg_ref)

        for h in range(n_heads):
            hs = slice(h * HEAD, (h + 1) * HEAD)
            oh = o_ref[:, hs]
            on = oh * lax.rsqrt(jnp.mean(oh * oh, axis=-1, keepdims=True) + EPS) * gn_ref[:, hs]
            gr = gr_ref[:, hs]
            yrin_ref[:, hs] = (on * (gr * _sigmoid(gr))).astype(BF16)
        yrec = jnp.dot(yrin_ref[...], wr_ref[...], preferred_element_type=F32)
        yrec_ref[...] = yrec
        merged = (_sigmoid(mc_ref[...]) * yc_ref[...] + _sigmoid(mr_ref[...]) * yrec).astype(BF16)
        mg_ref[...] = merged
        out = x_ref[...] + jnp.dot(merged, wo_ref[...], preferred_element_type=F32)
        r = lax.rsqrt(jnp.mean(out * out, axis=-1, keepdims=True) + EPS)
        yhat = out * r
        fg = fg_ref[...]
        rid = lax.broadcasted_iota(jnp.int32, (tm, 1), 0) + i * tm
        tgt = jnp.concatenate([t[...] for t in t_refs], axis=0)
        err = jnp.where(rid >= CHUNK, yhat * fg - tgt, 0.0)
        loss_ref[...] += 0.5 * jnp.sum(err * err) / d
        dy = err / d
        dfg_ref[...] += jnp.sum(dy * yhat, axis=0, keepdims=True)
        dyh = dy * fg
        dout_ref[...] = r * (dyh - yhat * jnp.mean(dyh * yhat, axis=-1, keepdims=True))

    row = lambda p: pl.BlockSpec((tm, d), lambda i, p=p: (i, p))
    vec = pl.BlockSpec((1, d), lambda i: (0, 0))
    mat = pl.BlockSpec((d, d), lambda i: (0, 0))
    return pl.pallas_call(
        body, grid=(lp // tm,),
        in_specs=[row(0), row(6), row(7), row(8), row(0), row(0), vec, vec, mat, mat]
        + [pl.BlockSpec((CHUNK, d), lambda i, u=u: (jnp.maximum(i * n_slabs + u - 1, 0), 0)) for u in range(n_slabs)],
        out_specs=[row(0), row(0), row(0), row(0), pl.BlockSpec((8, 128), lambda i: (0, 0)), vec],
        out_shape=[jax.ShapeDtypeStruct((lp, d), BF16), jax.ShapeDtypeStruct((lp, d), BF16),
                   jax.ShapeDtypeStruct((lp, d), F32), jax.ShapeDtypeStruct((lp, d), F32),
                   jax.ShapeDtypeStruct((8, 128), F32), jax.ShapeDtypeStruct((1, d), F32)],
        name="tail_fwd", compiler_params=_params())(
            o, proj, proj, proj, y_conv, hres, gnorm_g, final_g, w_rec, w_out, *([target] * n_slabs))


def _tail_bwd(dout, proj, y_conv, y_rec, o, c, w_out, w_rec, w_conv, ln_g, ln_b, gnorm_g):
    lp, d = dout.shape
    n_heads = d // HEAD
    tm = _row_tile(lp)

    def body(dout_ref, mc_ref, mr_ref, z_ref, gr_ref, yc_ref, yrec_ref, o_ref, c_ref,
             wo_ref, wr_ref, wc_ref, lg_ref, lb_ref, gn_ref,
             dyc_ref, dyr_ref, doutb_ref, dz_ref, dp_ref, do_ref, dc_ref,
             dgn_ref, dlg_ref, dlb_ref, dyrin_ref):
        i = pl.program_id(0)

        @pl.when(i == 0)
        def _():
            dgn_ref[...] = jnp.zeros_like(dgn_ref)
            dlg_ref[...] = jnp.zeros_like(dlg_ref)
            dlb_ref[...] = jnp.zeros_like(dlb_ref)

        doutb = dout_ref[...].astype(BF16)
        doutb_ref[...] = doutb
        dmerged = lax.dot_general(doutb, wo_ref[...], NT, preferred_element_type=F32)
        smc = _sigmoid(mc_ref[...])
        smr = _sigmoid(mr_ref[...])
        dyc = (dmerged * smc).astype(BF16)
        dyr = (dmerged * smr).astype(BF16)
        dyc_ref[...] = dyc
        dyr_ref[...] = dyr
        dp_ref[:, d:2 * d] = (dmerged * yc_ref[...] * smc * (1.0 - smc)).astype(BF16)
        dp_ref[:, 2 * d:3 * d] = (dmerged * yrec_ref[...] * smr * (1.0 - smr)).astype(BF16)

        dyrin_ref[...] = lax.dot_general(dyr, wr_ref[...], NT, preferred_element_type=F32)
        for h in range(n_heads):
            hs = slice(h * HEAD, (h + 1) * HEAD)
            oh = o_ref[:, hs]
            rstd = lax.rsqrt(jnp.mean(oh * oh, axis=-1, keepdims=True) + EPS)
            ohat = oh * rstd
            gn = gn_ref[:, hs]
            gr = gr_ref[:, hs]
            sg = _sigmoid(gr)
            dyrin = dyrin_ref[:, hs]
            don = dyrin * (gr * sg)
            dp_ref[:, hs] = (dyrin * (ohat * gn) * _dsilu(gr, sg)).astype(BF16)
            dgn_ref[:, hs] += jnp.sum(don * ohat, axis=0, keepdims=True)
            doh = don * gn
            do_ref[:, hs] = rstd * (doh - ohat * jnp.mean(doh * ohat, axis=-1, keepdims=True))

        dycin = lax.dot_general(dyc, wc_ref[...], NT, preferred_element_type=F32)
        c = c_ref[...]
        mu = jnp.mean(c, axis=-1, keepdims=True)
        xc = c - mu
        rstd = lax.rsqrt(jnp.mean(xc * xc, axis=-1, keepdims=True) + EPS)
        nrm = xc * rstd
        lg = lg_ref[...]
        ln = nrm * lg + lb_ref[...]
        sl = _sigmoid(ln)
        z = z_ref[...]
        sz = _sigmoid(z)
        dz_ref[...] = (dycin * (ln * sl) * _dsilu(z, sz)).astype(BF16)
        dln = dycin * (z * sz) * _dsilu(ln, sl)
        dlg_ref[...] += jnp.sum(dln * nrm, axis=0, keepdims=True)
        dlb_ref[...] += jnp.sum(dln, axis=0, keepdims=True)
        dn = dln * lg
        dc_ref[...] = rstd * (dn - jnp.mean(dn, axis=-1, keepdims=True)
                              - nrm * jnp.mean(dn * nrm, axis=-1, keepdims=True))

    row = lambda p: pl.BlockSpec((tm, d), lambda i, p=p: (i, p))
    vec = pl.BlockSpec((1, d), lambda i: (0, 0))
    mat = pl.BlockSpec((d, d), lambda i: (0, 0))
    act_bf = jax.ShapeDtypeStruct((lp, d), BF16)
    act_f32 = jax.ShapeDtypeStruct((lp, d), F32)
    vec_f32 = jax.ShapeDtypeStruct((1, d), F32)
    return pl.pallas_call(
        body, grid=(lp // tm,),
        in_specs=[row(0), row(7), row(8), row(2), row(6), row(0), row(0), row(0), row(0),
                  mat, mat, mat, vec, vec, vec],
        out_specs=[row(0)] * 4 + [pl.BlockSpec((tm, 3 * d), lambda i: (i, 2))] + [row(0)] * 2 + [vec] * 3,
        out_shape=[act_bf] * 4 + [jax.ShapeDtypeStruct((lp, 9 * d), BF16)] + [act_f32] * 2 + [vec_f32] * 3,
        scratch_shapes=[pltpu.VMEM((tm, d), F32)],
        name="tail_bwd", compiler_params=_params())(
            dout, proj, proj, proj, proj, y_conv, y_rec, o, c, w_out, w_rec, w_conv, ln_g, ln_b, gnorm_g)


def _hgrn_bwd(proj, do, s_all, lb_logits, n_pad, dproj):
    lp, d = do.shape
    n_heads = d // HEAD
    nc = lp // CHUNK
    tab, utri = _hgrn_tables()
    n_tab = tab.shape[0]

    def body(qr_ref, fr_ref, ir_ref, do_ref, s0_ref, lbl_ref, tab_ref, ut_ref, _,
             dp_ref, dlbl_ref, ds_ref, t_ref, dlb_ref):
        n = pl.program_id(0)
        chunk = nc - 1 - n

        @pl.when(n == 0)
        def _():
            ds_ref[...] = jnp.zeros_like(ds_ref)
            dlb_ref[...] = jnp.zeros_like(dlb_ref)

        lb_all, pp = _lower_bound(lbl_ref)
        rid = lax.broadcasted_iota(jnp.int32, (CHUNK, 1), 0)
        valid = jnp.logical_or(chunk > 0, rid >= n_pad)
        f_all = lb_all + (1.0 - lb_all) * _sigmoid(fr_ref[...])
        t_ref[...] = _dot2(tab_ref[...], jnp.where(valid, jnp.log(f_all), 0.0))
        masks = _level_masks()
        ut = ut_ref[...]

        def head(h):
            off = h * HEAD if isinstance(h, int) else pl.multiple_of(h * HEAD, HEAD)
            hs = pl.ds(off, HEAD)
            lb = _lower_bound_slice(lbl_ref, hs)
            qr = qr_ref[:, hs]
            q, sq, f, sf, _, k = _gates(qr, fr_ref[:, hs], lb, valid)
            v = ir_ref[:, hs]
            do_h = do_ref[:, hs]
            b = t_ref[0:CHUNK, hs]
            b_last = t_ref[CHUNK - 1:CHUNK, hs]
            s0 = s0_ref[0, hs, :]
            ds1 = ds_ref[hs, :]
            eb = jnp.exp(b)
            ekl = jnp.exp(b_last - b)
            do_bf = do_h.astype(BF16)
            v_bf = v.astype(BF16)
            ds1_bf = ds1.astype(BF16)

            da = lax.dot_general(do_bf, v_bf, NT, preferred_element_type=F32)
            da_diag = jnp.sum(do_h * v, axis=-1, keepdims=True)
            a = jnp.zeros((CHUNK, CHUNK), F32)
            dq_x = eb * lax.dot_general(do_bf, s0.astype(BF16), NT, preferred_element_type=F32)
            dk_x = ekl * lax.dot_general(v_bf, ds1_bf, NT, preferred_element_type=F32)
            x_after = q * dq_x
            x_before = k * dk_x
            for lvl in range(1, N_LEVELS + 1):
                e = _level_factor(b, _level_reference(lvl, b, t_ref, hs))
                qt = (q * e).astype(BF16)
                kt = (k * e).astype(BF16)
                p = lax.dot_general(qt, kt, NT, preferred_element_type=F32)
                a = a + jnp.where(masks[lvl - 1], p, 0.0)
                dam = jnp.where(masks[lvl - 1], da, 0.0).astype(BF16)
                dqt = jnp.dot(dam, kt, preferred_element_type=F32)
                dkt = lax.dot_general(dam, qt, TN, preferred_element_type=F32)
                dq_x = dq_x + e * dqt
                dk_x = dk_x + e * dkt
                x_after = x_after + (qt.astype(F32) * dqt - kt.astype(F32) * dkt)

            dv = (lax.dot_general(a.astype(BF16), do_bf, TN, preferred_element_type=F32)
                  + jnp.sum(q * k, axis=-1, keepdims=True) * do_h
                  + jnp.dot((k * ekl).astype(BF16), ds1_bf, preferred_element_type=F32))
            dp_ref[:, pl.ds(2 * d + off, HEAD)] = dv.astype(BF16)

            carried = jnp.exp(b_last) * _col_to_row(jnp.sum(s0 * ds1, axis=-1, keepdims=True))
            dg = _dot3(ut, jnp.concatenate([x_after, x_before], axis=0)) + carried
            dq = dq_x + da_diag * k
            dk = dk_x + da_diag * q
            dp_ref[:, hs] = (dq * _dsilu(qr, sq)).astype(BF16)
            df = jnp.where(valid, dg / f - dk, 0.0)
            dp_ref[:, pl.ds(d + off, HEAD)] = (df * (1.0 - lb) * sf * (1.0 - sf)).astype(BF16)
            dlb_ref[:, hs] += jnp.sum(df * (1.0 - sf), axis=0, keepdims=True)

            ds_ref[hs, :] = (_row_to_col(jnp.exp(b_last)) * ds1
                             + lax.dot_general((q * eb).astype(BF16), do_bf, TN, preferred_element_type=F32))
        per_trip = min(HEADS_PER_TRIP, n_heads)

        def head_group(p, carry):
            for u in range(per_trip):
                head(p * per_trip + u)
            return carry

        if n_heads == per_trip:
            head_group(0, 0)
        else:
            lax.fori_loop(0, n_heads // per_trip, head_group, 0)

        @pl.when(n == nc - 1)
        def _():
            dl0 = dlb_ref[...] * pp
            dlbl_ref[0:1, :] = dl0
            dlbl_ref[1:2, :] = -dl0

    piece = lambda p: pl.BlockSpec((CHUNK, d), lambda n, p=p: (nc - 1 - n, p))
    return pl.pallas_call(
        body, grid=(nc,),
        in_specs=[piece(3), piece(4), piece(5), piece(0),
                  pl.BlockSpec((1, d, HEAD), lambda n: (nc - 1 - n, 0, 0)),
                  pl.BlockSpec((2, d), lambda n: (0, 0)),
                  pl.BlockSpec((n_tab, CHUNK), lambda n: (0, 0)),
                  pl.BlockSpec((CHUNK, 2 * CHUNK), lambda n: (0, 0)), ANY],
        out_specs=[pl.BlockSpec((CHUNK, 3 * d), lambda n: (nc - 1 - n, 1)), pl.BlockSpec((2, d), lambda n: (0, 0))],
        out_shape=[jax.ShapeDtypeStruct(dproj.shape, BF16), jax.ShapeDtypeStruct((2, d), F32)],
        input_output_aliases={8: 0},
        scratch_shapes=[pltpu.VMEM((d, HEAD), F32), pltpu.VMEM((n_tab, d), F32), pltpu.VMEM((1, d), F32)],
        name="hgrn_bwd", compiler_params=_params())(proj, proj, proj, do, s_all, lb_logits, tab, utri, dproj)


def _conv_bwd(dc, proj, conv_w, dz, dproj):
    lp, d = dc.shape
    tm = _row_tile(lp)
    hb = tm // HALO
    n_tiles = lp // tm
    last_halo = lp // HALO - 1

    def body(dc_ref, dcn_ref, ua_ref, ub_ref, uap_ref, ubp_ref, cw_ref, dz_ref, _,
             dp_ref, dcw_ref, dcb_ref, aext_ref, dcext_ref, da_ref, dcw_acc):
        i = pl.program_id(0)

        @pl.when(i == 0)
        def _():
            dcw_acc[...] = jnp.zeros_like(dcw_acc)
            dcb_ref[...] = jnp.zeros_like(dcb_ref)

        ua = ua_ref[...]
        sb = _sigmoid(ub_ref[...])
        a_prev = uap_ref[...] * _sigmoid(ubp_ref[...])
        aext_ref[0:HALO, :] = jnp.where(i > 0, a_prev, 0.0)
        aext_ref[HALO:HALO + tm, :] = ua * sb
        dcext_ref[0:tm, :] = dc_ref[...]
        dcext_ref[tm:tm + HALO, :] = jnp.where(i < n_tiles - 1, dcn_ref[...], 0.0)
        dcb_ref[...] += jnp.sum(dc_ref[...], axis=0, keepdims=True)

        def row_block(r, carry):
            r0 = pl.multiple_of(r * CONV_ROWS, CONV_ROWS)
            n_rows = CONV_ROWS + HALO
            for cs in range(d // CONV_LANES):
                cl = slice(cs * CONV_LANES, (cs + 1) * CONV_LANES)
                dblk = dcext_ref[pl.ds(r0, n_rows), cl]
                ablk = aext_ref[pl.ds(r0, n_rows), cl]
                dcur = dblk[0:CONV_ROWS, :]
                acc = jnp.zeros((CONV_ROWS, CONV_LANES), F32)
                for b in range(8):
                    dsh = dblk if b == 0 else pltpu.roll(dblk, n_rows - b, axis=0)
                    ash = ablk if b == 0 else pltpu.roll(ablk, n_rows - b, axis=0)
                    for a in range(5):
                        j_da = CONV_WIDTH - 1 - (8 * a + b)
                        if 0 <= j_da < CONV_WIDTH:
                            acc = acc + cw_ref[j_da:j_da + 1, cl] * dsh[8 * a:8 * a + CONV_ROWS, :]
                        j_w = 8 * a + b - 2
                        if 0 <= j_w < CONV_WIDTH:
                            prod = dcur * ash[8 * a:8 * a + CONV_ROWS, :]
                            dcw_acc[j_w, :, cl] += prod.reshape(CONV_ROWS // 8, 8, CONV_LANES).sum(axis=0)
                da_ref[pl.ds(r0, CONV_ROWS), cl] = acc
            return carry

        lax.fori_loop(0, tm // CONV_ROWS, row_block, 0)

        da = da_ref[...]
        dp_ref[:, 0:d] = (da * sb).astype(BF16)
        dp_ref[:, d:2 * d] = (da * ua * sb * (1.0 - sb)).astype(BF16)
        dp_ref[:, 2 * d:3 * d] = dz_ref[...]

        @pl.when(i == n_tiles - 1)
        def _():
            dcw_ref[...] = jnp.sum(dcw_acc[...], axis=1)

    row = lambda p: pl.BlockSpec((tm, d), lambda i, p=p: (i, p))
    prev = lambda p: pl.BlockSpec((HALO, d), lambda i, p=p: (jnp.maximum(i * hb - 1, 0), p))
    nxt = pl.BlockSpec((HALO, d), lambda i: (jnp.minimum((i + 1) * hb, last_halo), 0))
    return pl.pallas_call(
        body, grid=(n_tiles,),
        in_specs=[row(0), nxt, row(0), row(1), prev(0), prev(1), pl.BlockSpec((HALO, d), lambda i: (0, 0)),
                  row(0), ANY],
        out_specs=[pl.BlockSpec((tm, 3 * d), lambda i: (i, 0)), pl.BlockSpec((HALO, d), lambda i: (0, 0)),
                   pl.BlockSpec((1, d), lambda i: (0, 0))],
        out_shape=[jax.ShapeDtypeStruct(dproj.shape, BF16),
                   jax.ShapeDtypeStruct((HALO, d), F32), jax.ShapeDtypeStruct((1, d), F32)],
        input_output_aliases={8: 0},
        scratch_shapes=[pltpu.VMEM((HALO + tm, d), F32), pltpu.VMEM((tm + HALO, d), F32), pltpu.VMEM((tm, d), F32),
                        pltpu.VMEM((HALO, 8, d), F32)],
        name="conv_bwd", compiler_params=_params())(dc, dc, proj, proj, proj, proj, conv_w, dz, dproj)


def _weight_grad(xs, dy, name, blocked):
    lp, dx = xs.shape
    n = dy.shape[1]
    tk = _mm_row_tile(lp)
    if blocked:
        ncol = n // N_CHIPS
        nt = W_IN_COL_TILES
        tn = ncol // nt
        grid = (N_CHIPS * nt, lp // tk)
        out_spec = pl.BlockSpec((1, dx, tn), lambda c, k: (c // nt, 0, c % nt))
        out_shape = jax.ShapeDtypeStruct((N_CHIPS, dx, ncol), F32)
    else:
        tn = n // 2
        grid = (2, lp // tk)
        out_spec = pl.BlockSpec((dx, tn), lambda c, k: (0, c))
        out_shape = jax.ShapeDtypeStruct((dx, n), F32)

    def body(xs_ref, dy_ref, o_ref):
        @pl.when(pl.program_id(1) == 0)
        def _():
            o_ref[...] = jnp.zeros_like(o_ref)

        p = lax.dot_general(xs_ref[...], dy_ref[...], TN, preferred_element_type=F32)
        if blocked:
            o_ref[0] += p
        else:
            o_ref[...] += p

    return pl.pallas_call(
        body, grid=grid,
        in_specs=[pl.BlockSpec((tk, dx), lambda c, k: (k, 0)), pl.BlockSpec((tk, tn), lambda c, k: (k, c))],
        out_specs=out_spec, out_shape=out_shape,
        name=name, compiler_params=_params())(xs, dy)


def _in_proj_bwd(dproj, wtg, hres, norm_g, dout):
    lp, d = hres.shape
    _, ncol, _ = wtg.shape
    tm = _mm_row_tile(lp)
    nt = W_IN_COL_TILES
    tn = ncol // nt
    nk = N_CHIPS * nt

    def body(dp_ref, w_ref, x_ref, g_ref, dout_ref, dx_ref, dg_ref, acc_ref):
        i = pl.program_id(0)
        kk = pl.program_id(1)

        @pl.when(jnp.logical_and(i == 0, kk == 0))
        def _():
            dg_ref[...] = jnp.zeros_like(dg_ref)

        @pl.when(kk == 0)
        def _():
            acc_ref[...] = jnp.zeros_like(acc_ref)

        acc_ref[...] += jnp.dot(dp_ref[...], w_ref[0], preferred_element_type=F32)

        @pl.when(kk == nk - 1)
        def _():
            x = x_ref[...]
            r = lax.rsqrt(jnp.mean(x * x, axis=-1, keepdims=True) + EPS)
            xhat = x * r
            dh = acc_ref[...]
            dg_ref[...] += jnp.sum(dh * xhat, axis=0, keepdims=True)
            dxh = dh * g_ref[...]
            dx_ref[...] = dout_ref[...] + r * (dxh - xhat * jnp.mean(dxh * xhat, axis=-1, keepdims=True))

    return pl.pallas_call(
        body, grid=(lp // tm, nk),
        in_specs=[pl.BlockSpec((tm, tn), lambda i, k: (i, k)),
                  pl.BlockSpec((1, tn, d), lambda i, k: (k // nt, k % nt, 0)),
                  pl.BlockSpec((tm, d), lambda i, k: (i, 0)),
                  pl.BlockSpec((1, d), lambda i, k: (0, 0)),
                  pl.BlockSpec((tm, d), lambda i, k: (i, 0))],
        out_specs=[pl.BlockSpec((tm, d), lambda i, k: (i, 0)), pl.BlockSpec((1, d), lambda i, k: (0, 0))],
        out_shape=[jax.ShapeDtypeStruct((lp, d), F32), jax.ShapeDtypeStruct((1, d), F32)],
        scratch_shapes=[pltpu.VMEM((tm, d), F32)],
        name="in_proj_bwd", compiler_params=_params())(dproj, wtg, hres, norm_g, dout)


def _adamw_math(w, g, m, v):
    m = ADAM_B1 * m + (1.0 - ADAM_B1) * g
    v = ADAM_B2 * v + (1.0 - ADAM_B2) * (g * g)
    m_hat = m / (1.0 - ADAM_B1 ** ADAM_STEP)
    v_hat = v / (1.0 - ADAM_B2 ** ADAM_STEP)
    delta = -ADAM_LR * (m_hat / (jnp.sqrt(v_hat) + ADAM_EPS) + ADAM_WD * w)
    return delta, m, v


def _elementwise_rows(shape):
    r, c = shape
    for t in (256, 128, 64, 32, 16, 8):
        if r % t == 0 and r > t and t * c * 4 <= 1024 * 1024:
            return t
    return r


def _adamw(name, w, m, v, *g_parts):
    shape = w.shape
    tr = _elementwise_rows(shape)
    n_g = len(g_parts)

    def body(*refs):
        w_ref, m_ref, v_ref = refs[:3]
        g_refs = refs[3:3 + n_g]
        g_out, d_out, m_out, v_out = refs[3 + n_g:]
        g = g_refs[0][...]
        for gr in g_refs[1:]:
            g = g + gr[...]
        delta, m_new, v_new = _adamw_math(w_ref[...], g, m_ref[...], v_ref[...])
        g_out[...] = g
        d_out[...] = delta
        m_out[...] = m_new
        v_out[...] = v_new

    spec = pl.BlockSpec((tr, shape[1]), lambda i: (i, 0))
    return pl.pallas_call(
        body, grid=(shape[0] // tr,),
        in_specs=[spec] * (3 + n_g), out_specs=[spec] * 4,
        out_shape=[jax.ShapeDtypeStruct(shape, F32)] * 4,
        name=name, compiler_params=_params())(w, m, v, *g_parts)


def _chip_half_sum(name, g, recv, core):
    _, _, hr, cols = g.shape
    tr = _elementwise_rows((hr, cols))

    def body(core_ref, g_ref, r_ref, o_ref, ob_ref):
        s = g_ref[0, 0] + r_ref[0]
        o_ref[0] = s
        ob_ref[0] = s.astype(BF16)

    blk = pl.BlockSpec((1, tr, cols), lambda j, i, core_ref: (j, i, 0))
    grid_spec = pltpu.PrefetchScalarGridSpec(
        num_scalar_prefetch=1, grid=(N_CHIPS, hr // tr),
        in_specs=[pl.BlockSpec((1, 1, tr, cols), lambda j, i, core_ref: (j, core_ref[0], i, 0)), blk],
        out_specs=[blk, blk])
    return pl.pallas_call(
        body, grid_spec=grid_spec,
        out_shape=[jax.ShapeDtypeStruct((N_CHIPS, hr, cols), F32), jax.ShapeDtypeStruct((N_CHIPS, hr, cols), BF16)],
        name=name, compiler_params=_params())(core, g, recv)


def _block_half_total(name, chip_sums, recv, chip_core):
    _, hr, cols = chip_sums.shape
    tr = _elementwise_rows((hr, cols))

    def body(cc_ref, p_ref, r_ref, o_ref):
        s = p_ref[0]
        for k in range(3):
            s = s + r_ref[k].astype(F32)
        o_ref[0] = s

    grid_spec = pltpu.PrefetchScalarGridSpec(
        num_scalar_prefetch=1, grid=(hr // tr,),
        in_specs=[pl.BlockSpec((1, tr, cols), lambda i, cc_ref: (cc_ref[0], i, 0)),
                  pl.BlockSpec((3, tr, cols), lambda i, cc_ref: (0, i, 0))],
        out_specs=pl.BlockSpec((1, tr, cols), lambda i, cc_ref: (cc_ref[1], i, 0)))
    return pl.pallas_call(
        body, grid_spec=grid_spec, out_shape=jax.ShapeDtypeStruct((2, hr, cols), F32),
        name=name, compiler_params=_params())(chip_core, chip_sums, recv)


def _place_shard(name, w, chip, dtype):
    r, c = w.shape
    tr = _elementwise_rows((r, c))

    def body(chip_ref, w_ref, o_ref):
        o_ref[0] = w_ref[...].astype(dtype)

    grid_spec = pltpu.PrefetchScalarGridSpec(
        num_scalar_prefetch=1, grid=(r // tr,),
        in_specs=[pl.BlockSpec((tr, c), lambda i, chip_ref: (i, 0))],
        out_specs=pl.BlockSpec((1, tr, c), lambda i, chip_ref: (chip_ref[0], i, 0)))
    return pl.pallas_call(
        body, grid_spec=grid_spec, out_shape=jax.ShapeDtypeStruct((N_CHIPS, r, c), dtype),
        name=name, compiler_params=_params())(chip, w)


def _sum_slots(name, slots):
    k, r, c = slots.shape

    def body(s_ref, o_ref):
        s = s_ref[0]
        for j in range(1, k):
            s = s + s_ref[j]
        o_ref[...] = s

    return pl.pallas_call(body, out_shape=jax.ShapeDtypeStruct((r, c), F32), name=name,
                          compiler_params=_params())(slots)


def _mesh_pos():
    return lax.axis_index("x"), lax.axis_index("y"), lax.axis_index("c")


def _other_chips(x, y):
    return [(1 - x, y), (x, 1 - y), (1 - x, 1 - y)]


def _gather_weights(bufs):
    n = len(bufs)
    half = [b.shape[1] // 2 for b in bufs]

    def body(*refs):
        gathered = refs[n:2 * n]
        ici_send, ici_recv, d2d_send, d2d_recv = refs[2 * n:]
        x, y, c = _mesh_pos()
        me = 2 * x + y
        chips = _other_chips(x, y)

        def part(a, block, core):
            return gathered[a].at[block, pl.ds(core * half[a], half[a])]

        def over_ici(a, k, block):
            px, py = chips[k]
            return pltpu.make_async_remote_copy(
                src_ref=part(a, block, c), dst_ref=part(a, block, c),
                send_sem=ici_send.at[a, k], recv_sem=ici_recv.at[a, k],
                device_id=(px, py, c), device_id_type=MESH)

        def over_d2d(a, k, core):
            px, py = chips[k]
            return pltpu.make_async_remote_copy(
                src_ref=part(a, 2 * px + py, core), dst_ref=part(a, 2 * px + py, core),
                send_sem=d2d_send.at[a, k], recv_sem=d2d_recv.at[a, k],
                device_id=(x, y, 1 - c), device_id_type=MESH)

        for a in range(n):
            for k in range(3):
                over_ici(a, k, me).start()
        for a in range(n):
            for k, (px, py) in enumerate(chips):
                over_ici(a, k, 2 * px + py).wait_recv()
                over_d2d(a, k, c).start()
        for a in range(n):
            for k in range(3):
                over_d2d(a, k, 1 - c).wait_recv()
        for a in range(n):
            for k in range(3):
                over_ici(a, k, me).wait_send()
                over_d2d(a, k, c).wait_send()

    return pl.pallas_call(
        body, in_specs=[ANY] * n, out_specs=[ANY] * n,
        out_shape=[jax.ShapeDtypeStruct(b.shape, b.dtype) for b in bufs],
        input_output_aliases={a: a for a in range(n)},
        scratch_shapes=[pltpu.SemaphoreType.DMA((n, 3))] * 4,
        name="gather_weights")(*bufs)


def _gather_in_proj(h, bufs, order):
    n = len(bufs)
    half = [b.shape[1] // 2 for b in bufs]
    lp, d = h.shape
    ncol = bufs[0].shape[2]
    tm = _mm_row_tile(lp)
    n_row = lp // tm

    def body(order_ref, h_ref, *refs):
        gathered = refs[n:2 * n]
        o_ref, wt_ref = refs[2 * n], refs[2 * n + 1]
        w_buf, ici_send, ici_recv, d2d_send, d2d_recv, w_sem = refs[2 * n + 2:]
        j = pl.program_id(0)
        i = pl.program_id(1)
        x, y, c = _mesh_pos()
        me = 2 * x + y
        chips = _other_chips(x, y)

        def part(a, block, core):
            return gathered[a].at[block, pl.ds(core * half[a], half[a])]

        def over_ici(a, k, block):
            px, py = chips[k]
            return pltpu.make_async_remote_copy(
                src_ref=part(a, block, c), dst_ref=part(a, block, c),
                send_sem=ici_send.at[a, k], recv_sem=ici_recv.at[a, k],
                device_id=(px, py, c), device_id_type=MESH)

        def over_d2d(a, k, core):
            px, py = chips[k]
            return pltpu.make_async_remote_copy(
                src_ref=part(a, 2 * px + py, core), dst_ref=part(a, 2 * px + py, core),
                send_sem=d2d_send.at[a, k], recv_sem=d2d_recv.at[a, k],
                device_id=(x, y, 1 - c), device_id_type=MESH)

        @pl.when(jnp.logical_and(j == 0, i == 0))
        def _():
            for a in range(n):
                for k in range(2):
                    over_ici(a, k, me).start()

        for k, (px, py) in enumerate(chips):
            @pl.when(jnp.logical_and(j == k + 1, i == 0))
            def _(k=k, px=px, py=py):
                for a in range(n):
                    over_ici(a, k, 2 * px + py).wait_recv()
                    over_d2d(a, k, c).start()
                if k == 0:
                    for a in range(n):
                        over_ici(a, 2, me).start()
                for a in range(n):
                    over_d2d(a, k, 1 - c).wait_recv()

        @pl.when(i == 0)
        def _():
            load = pltpu.make_async_copy(gathered[0].at[order_ref[j]], w_buf, w_sem)
            load.start()
            load.wait()
            wt_ref[0] = w_buf[...].T

        o_ref[...] = jnp.dot(h_ref[...], w_buf[...], preferred_element_type=F32)

        @pl.when(jnp.logical_and(j == N_CHIPS - 1, i == n_row - 1))
        def _():
            for a in range(n):
                for k in range(3):
                    over_ici(a, k, me).wait_send()
                    over_d2d(a, k, c).wait_send()

    grid_spec = pltpu.PrefetchScalarGridSpec(
        num_scalar_prefetch=1, grid=(N_CHIPS, n_row),
        in_specs=[pl.BlockSpec((tm, d), lambda j, i, order_ref: (i, 0))] + [ANY] * n,
        out_specs=[ANY] * n + [pl.BlockSpec((tm, ncol), lambda j, i, order_ref: (i, order_ref[j])),
                               pl.BlockSpec((1, ncol, d), lambda j, i, order_ref: (order_ref[j], 0, 0))],
        scratch_shapes=[pltpu.VMEM((d, ncol), BF16)] + [pltpu.SemaphoreType.DMA((n, 3))] * 4
        + [pltpu.SemaphoreType.DMA])
    out = pl.pallas_call(
        body, grid_spec=grid_spec,
        out_shape=[jax.ShapeDtypeStruct(b.shape, b.dtype) for b in bufs]
        + [jax.ShapeDtypeStruct((lp, N_CHIPS * ncol), F32), jax.ShapeDtypeStruct((N_CHIPS, ncol, d), BF16)],
        input_output_aliases={2 + a: a for a in range(n)},
        name="gather_in_proj", compiler_params=_params())(order, h, *bufs)
    return out[n], out[n + 1], out[:n]


def _send_other_halves(grads, tag):
    n = len(grads)

    def body(*refs):
        srcs = refs[:n]
        dsts = refs[n:2 * n]
        send_sems, recv_sems = refs[2 * n:]
        x, y, c = _mesh_pos()
        copies = [pltpu.make_async_remote_copy(
            src_ref=srcs[a].at[j, 1 - c], dst_ref=dsts[a].at[j], send_sem=send_sems.at[a, j],
            recv_sem=recv_sems.at[a, j], device_id=(x, y, 1 - c), device_id_type=MESH)
            for a in range(n) for j in range(N_CHIPS)]
        for cp in copies:
            cp.start()
        for cp in copies:
            cp.wait()

    return pl.pallas_call(
        body, in_specs=[ANY] * n, out_specs=[ANY] * n,
        out_shape=[jax.ShapeDtypeStruct((N_CHIPS,) + g.shape[2:], F32) for g in grads],
        scratch_shapes=[pltpu.SemaphoreType.DMA((n, N_CHIPS))] * 2,
        name="send_other_halves_" + tag)(*grads)


HBM = pl.BlockSpec(memory_space=pltpu.HBM)
SEM = pl.BlockSpec(memory_space=pltpu.SEMAPHORE)


def _block_copies(n, srcs, dsts, send_sems, recv_sems):
    x, y, c = _mesh_pos()
    return [pltpu.make_async_remote_copy(
        src_ref=srcs[a].at[2 * px + py], dst_ref=dsts[a].at[k], send_sem=send_sems.at[3 * a + k],
        recv_sem=recv_sems.at[3 * a + k], device_id=(px, py, c), device_id_type=MESH)
        for a in range(n) for k, (px, py) in enumerate(_other_chips(x, y))]


def _exchange_start(blocked, tag):
    n = len(blocked)
    lands = [lax.empty((3,) + b.shape[1:], b.dtype) for b in blocked]
    bufs = [pltpu.with_memory_space_constraint(b, pltpu.HBM) for b in list(blocked) + lands]
    nb = 2 * n

    def body(*refs):
        for cp in _block_copies(n, refs[:n], refs[n:nb], refs[nb], refs[nb + 1]):
            cp.start()
        refs[-1][...] = jnp.zeros_like(refs[-1])

    out = pl.pallas_call(
        body, name="exchange_start_" + tag,
        in_specs=[HBM] * nb,
        out_shape=[pltpu.SemaphoreType.DMA((3 * n,)), pltpu.SemaphoreType.DMA((3 * n,))]
        + [pltpu.HBM(b.shape, b.dtype) for b in bufs] + [jax.ShapeDtypeStruct((8, 128), F32)],
        out_specs=[SEM] * 2 + [HBM] * nb + [pl.BlockSpec(memory_space=pltpu.VMEM)],
        input_output_aliases={i: 2 + i for i in range(nb)},
        compiler_params=pltpu.CompilerParams(has_side_effects=pltpu.SideEffectType.DATAFLOW_SIDE_EFFECTING),
    )(*bufs)
    return (out[:2], out[2:2 + nb]), out[-1]


def _exchange_wait(state, after, tag):
    sems, bufs = state
    nb = len(bufs)
    n = nb // 2

    def body(*refs):
        for cp in _block_copies(n, refs[:n], refs[n:nb], refs[nb], refs[nb + 1]):
            cp.wait_send()
            cp.wait_recv()

    out = pl.pallas_call(
        body, name="exchange_wait_" + tag,
        in_specs=[HBM] * nb + [SEM] * 2 + [ANY],
        out_shape=[pltpu.HBM(b.shape, b.dtype) for b in bufs],
        out_specs=[HBM] * nb,
        input_output_aliases={i: i for i in range(nb)},
        compiler_params=pltpu.CompilerParams(has_side_effects=pltpu.SideEffectType.DATAFLOW_SIDE_EFFECTING),
    )(*bufs, *sems, after)
    return out[n:nb]


def _whole_block_copies(buf, send_sems, recv_sems, incoming):
    x, y, c = _mesh_pos()
    me = 2 * x + y
    out = []
    for k, (px, py) in enumerate(_other_chips(x, y)):
        block = 2 * px + py if incoming else me
        out.append(pltpu.make_async_remote_copy(
            src_ref=buf.at[block], dst_ref=buf.at[block], send_sem=send_sems.at[k], recv_sem=recv_sems.at[k],
            device_id=(px, py, c), device_id_type=MESH))
    return out


def _gather_start(buf, after, tag):
    buf = pltpu.with_memory_space_constraint(buf, pltpu.HBM)

    def body(buf_ref, after_ref, send_sems, recv_sems, thru_ref, token):
        for cp in _whole_block_copies(buf_ref, send_sems, recv_sems, incoming=False):
            cp.start()
        token[...] = jnp.zeros_like(token)

    out = pl.pallas_call(
        body, name="gather_start_" + tag,
        in_specs=[HBM, ANY],
        out_shape=[pltpu.SemaphoreType.DMA((3,)), pltpu.SemaphoreType.DMA((3,)), pltpu.HBM(buf.shape, buf.dtype),
                   jax.ShapeDtypeStruct((8, 128), F32)],
        out_specs=[SEM, SEM, HBM, pl.BlockSpec(memory_space=pltpu.VMEM)],
        input_output_aliases={0: 2},
        compiler_params=pltpu.CompilerParams(has_side_effects=pltpu.SideEffectType.DATAFLOW_SIDE_EFFECTING),
    )(buf, after)
    return out[:3], out[3]


def _gather_wait(state, after, tag):
    send_sems, recv_sems, buf = state

    def body(buf_ref, send_ref, recv_ref, after_ref, out_ref):
        for cp in _whole_block_copies(buf_ref, send_ref, recv_ref, incoming=True):
            cp.wait_send()
            cp.wait_recv()

    return pl.pallas_call(
        body, name="gather_wait_" + tag,
        in_specs=[HBM, SEM, SEM, ANY],
        out_shape=pltpu.HBM(buf.shape, buf.dtype), out_specs=HBM,
        input_output_aliases={0: 0},
        compiler_params=pltpu.CompilerParams(has_side_effects=pltpu.SideEffectType.DATAFLOW_SIDE_EFFECTING),
    )(buf, send_sems, recv_sems, after)


def _exchange_small(small):
    def body(small_src, small_dst, ssend_sems, srecv_sems, local_sem):
        x, y, c = _mesh_pos()
        my_idx = 4 * x + 2 * y + c
        local = pltpu.make_async_copy(small_src, small_dst.at[my_idx], local_sem)
        local.start()
        others = []
        for r in range(1, 8):
            px = 1 - x if r & 4 else x
            py = 1 - y if r & 2 else y
            pc = 1 - c if r & 1 else c
            others.append((px, py, pc))
        for r, peer in enumerate(others):
            pltpu.make_async_remote_copy(
                src_ref=small_src, dst_ref=small_dst.at[my_idx], send_sem=ssend_sems.at[r],
                recv_sem=srecv_sems.at[r], device_id=peer, device_id_type=MESH).start()
        for r, (px, py, pc) in enumerate(others):
            pltpu.make_async_remote_copy(
                src_ref=small_src, dst_ref=small_dst.at[4 * px + 2 * py + pc], send_sem=ssend_sems.at[r],
                recv_sem=srecv_sems.at[r], device_id=(px, py, pc), device_id_type=MESH).wait()
        local.wait()

    return pl.pallas_call(
        body, in_specs=[ANY], out_specs=ANY, out_shape=jax.ShapeDtypeStruct((8,) + small.shape, F32),
        scratch_shapes=[pltpu.SemaphoreType.DMA((7,)), pltpu.SemaphoreType.DMA((7,)), pltpu.SemaphoreType.DMA],
        name="exchange_small")(small)


def _join_halves(bufs):
    n = len(bufs)

    def body(*refs):
        joined = refs[n:2 * n]
        send_sems, recv_sems = refs[2 * n:]
        x, y, c = _mesh_pos()
        for a in range(n):
            pltpu.make_async_remote_copy(
                src_ref=joined[a].at[c], dst_ref=joined[a].at[c], send_sem=send_sems.at[a],
                recv_sem=recv_sems.at[a], device_id=(x, y, 1 - c), device_id_type=MESH).start()
        for a in range(n):
            pltpu.make_async_remote_copy(
                src_ref=joined[a].at[c], dst_ref=joined[a].at[1 - c], send_sem=send_sems.at[a],
                recv_sem=recv_sems.at[a], device_id=(x, y, 1 - c), device_id_type=MESH).wait()

    return pl.pallas_call(
        body, in_specs=[ANY] * n, out_specs=[ANY] * n,
        out_shape=[jax.ShapeDtypeStruct(b.shape, b.dtype) for b in bufs],
        input_output_aliases={a: a for a in range(n)},
        scratch_shapes=[pltpu.SemaphoreType.DMA((n,))] * 2,
        name="join_halves")(*bufs)


def kernel(x, meta_tokens, norm_g, w_in, conv_w, conv_b, ln_g, ln_b, w_conv_out, lb_logits, gnorm_g, w_rec_out, w_out, final_g, loss_target, m_meta_tokens, m_norm_g, m_w_in, m_conv_w, m_conv_b, m_ln_g, m_ln_b, m_w_conv_out, m_lb_logits, m_gnorm_g, m_w_rec_out, m_w_out, m_final_g, v_meta_tokens, v_norm_g, v_w_in, v_conv_w, v_conv_b, v_ln_g, v_ln_b, v_w_conv_out, v_lb_logits, v_gnorm_g, v_w_rec_out, v_w_out, v_final_g):
    seq, d = x.shape[1], x.shape[2]
    n_meta = meta_tokens.shape[0]
    n_pad = CHUNK - n_meta
    ds = d // N_CHIPS
    chip = 2 * lax.axis_index("x") + lax.axis_index("y")

    conv_w_pad = jnp.pad(conv_w[0], ((0, HALO - CONV_WIDTH), (0, 0)))
    chip_idx = chip.astype(jnp.int32).reshape(1)
    (small_g,) = _gather_weights([
        _place_shard("place_small", jnp.concatenate([conv_w_pad, meta_tokens], axis=0), chip_idx, F32)])
    cw_full = jnp.transpose(small_g[:, 0:HALO], (1, 0, 2)).reshape(HALO, d)
    meta_full = jnp.transpose(small_g[:, HALO:HALO + n_meta], (1, 0, 2)).reshape(n_meta, d)

    hres = jnp.concatenate([jnp.zeros((n_pad, d), F32), meta_full, x[0]], axis=0)
    target = loss_target[0]
    final_g2 = final_g.reshape(1, d)
    h = _rmsnorm_fwd(hres, norm_g)
    fx, fy = 1 - lax.axis_index("x"), 1 - lax.axis_index("y")
    order = jnp.stack([chip, 2 * fx + (1 - fy), 2 * (1 - fx) + fy, 2 * fx + fy]).astype(jnp.int32)
    proj, win_t, _ = _gather_in_proj(h, [_place_shard("place_w_in", w_in[0], chip_idx, BF16)], order)
    sq_own = _place_shard("place_square", jnp.concatenate([w_conv_out[0], w_rec_out[0], w_out[0]], axis=0),
                          chip_idx, BF16)
    sq_flight, sq_token = _gather_start(sq_own, proj, "square")
    o, s_all = _hgrn_fwd(proj, lb_logits + sq_token[0:1, 0:1], n_pad)
    sq_g = _gather_wait(sq_flight, s_all, "square")
    wc_full = sq_g[:, 0:ds].reshape(d, d)
    wr_full = sq_g[:, ds:2 * ds].reshape(d, d)
    wo_full = sq_g[:, 2 * ds:3 * ds].reshape(d, d)
    c, yc_in, y_conv = _conv_fwd(proj, cw_full, conv_b, ln_g, ln_b, wc_full)
    yr_in, merged, y_rec, dout, loss_acc, dfinal_g = _tail_fwd(
        o, proj, y_conv, hres, target, gnorm_g, final_g2, wr_full, wo_full)

    (dyc, dyr, dout_bf, dz, dproj, do, dc, dgnorm_g, dln_g, dln_b) = _tail_bwd(
        dout, proj, y_conv, y_rec, o, c, wo_full, wr_full, wc_full, ln_g, ln_b, gnorm_g)
    g_wc = _weight_grad(yc_in, dyc, "grad_w_conv_out", False)
    g_wr = _weight_grad(yr_in, dyr, "grad_w_rec_out", False)
    g_wo = _weight_grad(merged, dout_bf, "grad_w_out", False)

    core = lax.axis_index("c").astype(jnp.int32).reshape(1)

    def chip_sum_and_start(g, tag):
        g = g.reshape(N_CHIPS, 2, g.shape[1] // 2, g.shape[2])
        (from_sibling,) = _send_other_halves([g], tag)
        sums = _chip_half_sum("chip_half_sum_" + tag, g, from_sibling, core)
        in_flight, token = _exchange_start([sums[1]], tag)
        return sums[0], in_flight, token[0:1, 0:1]

    g_sq = jnp.concatenate([g.reshape(N_CHIPS, ds, d) for g in (g_wc, g_wr, g_wo)], axis=1)
    sum_sq, flight_sq, token_sq = chip_sum_and_start(g_sq, "square")
    dproj, dlb_logits = _hgrn_bwd(proj, do, s_all, lb_logits + token_sq, n_pad, dproj)
    dproj, dconv_w, dconv_b = _conv_bwd(dc, proj, cw_full, dz, dproj)
    (recv_sq,) = _exchange_wait(flight_sq, dconv_b, "square")
    g_win = _weight_grad(h, dproj, "grad_w_in", True)
    sum_win, flight_win, token_win = chip_sum_and_start(g_win, "w_in")
    dhres, dnorm_g = _in_proj_bwd(dproj, win_t, hres, norm_g + token_win, dout)
    grad_x = dhres[CHUNK:][None]
    (recv_win,) = _exchange_wait(flight_win, dnorm_g, "w_in")
    small = jnp.concatenate([dnorm_g, dconv_b, dln_g, dln_b, dlb_logits, dgnorm_g, dfinal_g,
                             dhres[n_pad:CHUNK], dconv_w[:CONV_WIDTH],
                             jnp.broadcast_to(loss_acc[0:1, 0:1], (1, d))], axis=0)
    small_slots = _exchange_small(small)
    chip_core = jnp.concatenate([chip_idx, core])
    totals = [_block_half_total("block_half_total_" + nm, s, r, chip_core)
              for nm, s, r in zip(("w_in", "square"), (sum_win, sum_sq), (recv_win, recv_sq))]
    gt_win, gt_sq = [t.reshape(2 * t.shape[1], t.shape[2]) for t in _join_halves(totals)]
    small_sum = _sum_slots("sum_small", small_slots)

    res = {}
    res["w_in"] = _adamw("adamw_w_in", w_in[0], m_w_in[0], v_w_in[0], gt_win)
    res["w_conv_out"] = _adamw("adamw_w_conv_out", w_conv_out[0], m_w_conv_out[0], v_w_conv_out[0], gt_sq[0:ds])
    res["w_rec_out"] = _adamw("adamw_w_rec_out", w_rec_out[0], m_w_rec_out[0], v_w_rec_out[0], gt_sq[ds:2 * ds])
    res["w_out"] = _adamw("adamw_w_out", w_out[0], m_w_out[0], v_w_out[0], gt_sq[2 * ds:3 * ds])
    big = {k: tuple(a[None] for a in v) for k, v in res.items()}

    rep_names = ("norm_g", "conv_b", "ln_g", "ln_b", "lb_logits", "gnorm_g", "final_g")
    rep_w = (norm_g, conv_b, ln_g, ln_b, lb_logits, gnorm_g, final_g2)
    rep_m = (m_norm_g, m_conv_b, m_ln_g, m_ln_b, m_lb_logits, m_gnorm_g, m_final_g.reshape(1, d))
    rep_v = (v_norm_g, v_conv_b, v_ln_g, v_ln_b, v_lb_logits, v_gnorm_g, v_final_g.reshape(1, d))
    rep = _adamw("adamw_replicated", jnp.concatenate(rep_w, 0), jnp.concatenate(rep_m, 0),
                 jnp.concatenate(rep_v, 0), small_sum[0:8])
    rep_rows = {"norm_g": (0, 1), "conv_b": (1, 2), "ln_g": (2, 3), "ln_b": (3, 4), "lb_logits": (4, 6),
                "gnorm_g": (6, 7), "final_g": (7, 8)}
    small_out = {}
    for nm in rep_names:
        lo, hi = rep_rows[nm]
        vals = tuple(a[lo:hi] for a in rep)
        if nm == "final_g":
            vals = tuple(a.reshape(d) for a in vals)
        small_out[nm] = vals
    cw_row = 8 + n_meta
    g_meta = lax.dynamic_slice_in_dim(small_sum[8:cw_row], chip * ds, ds, axis=1)
    small_out["meta_tokens"] = _adamw("adamw_meta", meta_tokens, m_meta_tokens, v_meta_tokens, g_meta)
    g_cw = lax.dynamic_slice_in_dim(small_sum[cw_row:cw_row + HALO], chip * ds, ds, axis=1)
    pad_rows = ((0, HALO - CONV_WIDTH), (0, 0))
    cw_res = _adamw("adamw_conv_w", conv_w_pad, jnp.pad(m_conv_w[0], pad_rows),
                    jnp.pad(v_conv_w[0], pad_rows, constant_values=1.0), g_cw)
    small_out["conv_w"] = tuple(a[:CONV_WIDTH][None] for a in cw_res)

    loss = small_sum[cw_row + HALO - 1, 0]

    order = ("meta_tokens", "norm_g", "w_in", "conv_w", "conv_b", "ln_g", "ln_b", "w_conv_out", "lb_logits",
             "gnorm_g", "w_rec_out", "w_out", "final_g")
    allres = {**big, **small_out}
    outs = [loss, grad_x]
    for field in range(4):
        outs.extend(allres[nm][field] for nm in order)
    return tuple(outs)
```

```python
import numpy as np

import jax
import jax.numpy as jnp
from jax import lax
from jax.experimental import pallas as pl
from jax.experimental.pallas import tpu as pltpu

F32 = jnp.float32
BF16 = jnp.bfloat16

EPS = 1e-6
CHUNK = 64
N_LEVELS = 6
FIRST_TABLE_LEVEL = 5
CONV_WIDTH = 31
HALO = 32
CONV_ROWS = 32
CONV_LANES = 256
HEAD = 128
W_IN_COL_TILES = 1
HEADS_PER_TRIP = 8
N_CHIPS = 4
VMEM_LIMIT_BYTES = 56 * 1024 * 1024

ADAM_LR = 0.001
ADAM_B1 = 0.9
ADAM_B2 = 0.999
ADAM_EPS = 1e-08
ADAM_WD = 0.01
ADAM_STEP = 10

MESH = pl.DeviceIdType.MESH
ANY = pl.BlockSpec(memory_space=pl.ANY)

NT = (((1,), (1,)), ((), ()))
TN = (((0,), (0,)), ((), ()))


def _params(**kw):
    return pltpu.CompilerParams(vmem_limit_bytes=VMEM_LIMIT_BYTES, **kw)


def _sigmoid(x):
    return jax.nn.sigmoid(x)


def _dsilu(x, s):
    return s * (1.0 + x * (1.0 - s))


def _row_tile(lp):
    for t in (320, 256, 192, 128, 64):
        if lp % t == 0:
            return t
    raise ValueError(f"unsupported padded length {lp}")


def _mm_row_tile(lp):
    for t in (832, 640, 320, 256, 192, 128, 64):
        if lp % t == 0:
            return t
    raise ValueError(f"unsupported padded length {lp}")


def _dot3(m_bf16, x):
    hi = x.astype(BF16)
    r1 = x - hi.astype(F32)
    mid = r1.astype(BF16)
    lo = (r1 - mid.astype(F32)).astype(BF16)
    return (jnp.dot(m_bf16, hi, preferred_element_type=F32)
            + jnp.dot(m_bf16, mid, preferred_element_type=F32)
            + jnp.dot(m_bf16, lo, preferred_element_type=F32))


def _dot2(m_bf16, x):
    hi = x.astype(BF16)
    lo = (x - hi.astype(F32)).astype(BF16)
    return (jnp.dot(m_bf16, hi, preferred_element_type=F32)
            + jnp.dot(m_bf16, lo, preferred_element_type=F32))


def _col_to_row(col):
    return jnp.broadcast_to(col, (HEAD, 8)).T[0:1, :]


def _row_to_col(row):
    return jnp.broadcast_to(row, (8, HEAD)).T[:, 0:1]


def _hgrn_tables():
    t = np.arange(CHUNK)
    ltri = (t[None, :] <= t[:, None]).astype(np.float32)
    mats = [ltri]
    for lvl in range(FIRST_TABLE_LEVEL, N_LEVELS + 1):
        blk = CHUNK >> (lvl - 1)
        mid = (t // blk) * blk + blk // 2
        mats.append(ltri[mid - 1])
    after = (t[None, :] >= t[:, None]).astype(np.float32)
    before = (t[None, :] < t[:, None]).astype(np.float32)
    return jnp.asarray(np.concatenate(mats, 0), BF16), jnp.asarray(np.concatenate([after, before], 1), BF16)


def _rmsnorm_fwd(hres, g):
    lp, d = hres.shape
    tm = _row_tile(lp)

    def body(x_ref, g_ref, h_ref):
        x = x_ref[...]
        r = lax.rsqrt(jnp.mean(x * x, axis=-1, keepdims=True) + EPS)
        h_ref[...] = (x * r * g_ref[...]).astype(BF16)

    return pl.pallas_call(
        body, grid=(lp // tm,),
        in_specs=[pl.BlockSpec((tm, d), lambda i: (i, 0)), pl.BlockSpec((1, d), lambda i: (0, 0))],
        out_specs=pl.BlockSpec((tm, d), lambda i: (i, 0)),
        out_shape=jax.ShapeDtypeStruct((lp, d), BF16),
        name="rmsnorm_fwd", compiler_params=_params())(hres, g)


def _conv_fwd(proj, conv_w, conv_b, ln_g, ln_b, w_conv):
    lp = proj.shape[0]
    d = conv_b.shape[1]
    tm = _row_tile(lp)
    hb = tm // HALO

    def body(ua_ref, ub_ref, z_ref, uap_ref, ubp_ref, cw_ref, cb_ref, lg_ref, lb_ref, w_ref,
             c_ref, ycin_ref, yconv_ref, aext_ref):
        i = pl.program_id(0)
        a_prev = uap_ref[...] * _sigmoid(ubp_ref[...])
        aext_ref[0:HALO, :] = jnp.where(i > 0, a_prev, 0.0)
        aext_ref[HALO:HALO + tm, :] = ua_ref[...] * _sigmoid(ub_ref[...])

        def row_block(r, carry):
            r0 = pl.multiple_of(r * CONV_ROWS, CONV_ROWS)
            for cs in range(d // CONV_LANES):
                cl = slice(cs * CONV_LANES, (cs + 1) * CONV_LANES)
                blk = aext_ref[pl.ds(r0, CONV_ROWS + HALO), cl]
                acc = jnp.zeros((CONV_ROWS, CONV_LANES), F32) + cb_ref[:, cl]
                for b in range(8):
                    sh = blk if b == 0 else pltpu.roll(blk, CONV_ROWS + HALO - b, axis=0)
                    for a in range(5):
                        j = 8 * a + b - 2
                        if 0 <= j < CONV_WIDTH:
                            acc = acc + cw_ref[j:j + 1, cl] * sh[8 * a:8 * a + CONV_ROWS, :]
                c_ref[pl.ds(r0, CONV_ROWS), cl] = acc
            return carry

        lax.fori_loop(0, tm // CONV_ROWS, row_block, 0)

        c = c_ref[...]
        mu = jnp.mean(c, axis=-1, keepdims=True)
        xc = c - mu
        rstd = lax.rsqrt(jnp.mean(xc * xc, axis=-1, keepdims=True) + EPS)
        ln = xc * rstd * lg_ref[...] + lb_ref[...]
        s = ln * _sigmoid(ln)
        z = z_ref[...]
        ycin = (s * (z * _sigmoid(z))).astype(BF16)
        ycin_ref[...] = ycin
        yconv_ref[...] = jnp.dot(ycin, w_ref[...], preferred_element_type=F32)

    row = lambda p: pl.BlockSpec((tm, d), lambda i, p=p: (i, p))
    halo = lambda p: pl.BlockSpec((HALO, d), lambda i, p=p: (jnp.maximum(i * hb - 1, 0), p))
    vec = pl.BlockSpec((1, d), lambda i: (0, 0))
    return pl.pallas_call(
        body, grid=(lp // tm,),
        in_specs=[row(0), row(1), row(2), halo(0), halo(1),
                  pl.BlockSpec((HALO, d), lambda i: (0, 0)), vec, vec, vec,
                  pl.BlockSpec((d, d), lambda i: (0, 0))],
        out_specs=[pl.BlockSpec((tm, d), lambda i: (i, 0))] * 3,
        out_shape=[jax.ShapeDtypeStruct((lp, d), F32), jax.ShapeDtypeStruct((lp, d), BF16),
                   jax.ShapeDtypeStruct((lp, d), F32)],
        scratch_shapes=[pltpu.VMEM((HALO + tm, d), F32)],
        name="conv_fwd", compiler_params=_params())(
            proj, proj, proj, proj, proj, conv_w, conv_b, ln_g, ln_b, w_conv)


def _lower_bound(lbl_ref):
    l0 = lbl_ref[0:1, :]
    l1 = lbl_ref[1:2, :]
    m = jnp.maximum(l0, l1)
    e0 = jnp.exp(l0 - m)
    e1 = jnp.exp(l1 - m)
    p0 = e0 / (e0 + e1)
    return p0, p0 * (e1 / (e0 + e1))


def _level_masks():
    r2 = lax.broadcasted_iota(jnp.int32, (CHUNK, CHUNK), 0)
    c2 = lax.broadcasted_iota(jnp.int32, (CHUNK, CHUNK), 1)
    out = []
    for lvl in range(1, N_LEVELS + 1):
        blk = CHUNK >> (lvl - 1)
        sh = blk.bit_length() - 1
        same = (r2 >> sh) == (c2 >> sh)
        t_upper = (r2 & (blk - 1)) >= (blk // 2)
        s_lower = (c2 & (blk - 1)) < (blk // 2)
        out.append(jnp.logical_and(same, jnp.logical_and(t_upper, s_lower)))
    return out


def _gates(qr, fr, lb, valid):
    sq = _sigmoid(qr)
    q = qr * sq
    sf = _sigmoid(fr)
    f = lb + (1.0 - lb) * sf
    g = jnp.where(valid, jnp.log(f), 0.0)
    k = jnp.where(valid, 1.0 - f, 0.0)
    return q, sq, f, sf, g, k


def _level_reference(lvl, b, t_ref, hs):
    if lvl >= FIRST_TABLE_LEVEL:
        base = CHUNK * (lvl - FIRST_TABLE_LEVEL + 1)
        return t_ref[base:base + CHUNK, hs]
    blk = CHUNK >> (lvl - 1)
    rows = [jnp.broadcast_to(b[m + blk // 2 - 1:m + blk // 2, :], (blk, HEAD)) for m in range(0, CHUNK, blk)]
    return rows[0] if len(rows) == 1 else jnp.concatenate(rows, axis=0)


def _level_factor(b, r):
    d = b - r
    return jnp.exp(jnp.minimum(d, -d))


def _hgrn_fwd(proj, lb_logits, n_pad):
    lp = proj.shape[0]
    d = lb_logits.shape[1]
    n_heads = d // HEAD
    nc = lp // CHUNK
    tab, _ = _hgrn_tables()
    n_tab = tab.shape[0]

    def body(qr_ref, fr_ref, ir_ref, lbl_ref, tab_ref, o_ref, sall_ref, s_ref, t_ref):
        n = pl.program_id(0)

        @pl.when(n == 0)
        def _():
            s_ref[...] = jnp.zeros_like(s_ref)

        sall_ref[0] = s_ref[...]
        lb_all, _ = _lower_bound(lbl_ref)
        rid = lax.broadcasted_iota(jnp.int32, (CHUNK, 1), 0)
        valid = jnp.logical_or(n > 0, rid >= n_pad)
        f_all = lb_all + (1.0 - lb_all) * _sigmoid(fr_ref[...])
        t_ref[...] = _dot2(tab_ref[...], jnp.where(valid, jnp.log(f_all), 0.0))
        masks = _level_masks()

        def head(h):
            off = h * HEAD if isinstance(h, int) else pl.multiple_of(h * HEAD, HEAD)
            hs = pl.ds(off, HEAD)
            lb = _lower_bound_slice(lbl_ref, hs)
            q, _, _, _, _, k = _gates(qr_ref[:, hs], fr_ref[:, hs], lb, valid)
            v = ir_ref[:, hs]
            b = t_ref[0:CHUNK, hs]
            s0 = s_ref[hs, :]
            o = jnp.dot((q * jnp.exp(b)).astype(BF16), s0.astype(BF16), preferred_element_type=F32)
            o = o + jnp.sum(q * k, axis=-1, keepdims=True) * v
            a = jnp.zeros((CHUNK, CHUNK), F32)
            for lvl in range(1, N_LEVELS + 1):
                e = _level_factor(b, _level_reference(lvl, b, t_ref, hs))
                p = lax.dot_general((q * e).astype(BF16), (k * e).astype(BF16), NT, preferred_element_type=F32)
                a = a + jnp.where(masks[lvl - 1], p, 0.0)
            vb = v.astype(BF16)
            o_ref[:, hs] = o + jnp.dot(a.astype(BF16), vb, preferred_element_type=F32)
            b_last = t_ref[CHUNK - 1:CHUNK, hs]
            khat = (k * jnp.exp(b_last - b)).astype(BF16)
            s_ref[hs, :] = _row_to_col(jnp.exp(b_last)) * s0 + lax.dot_general(khat, vb, TN, preferred_element_type=F32)
        per_trip = min(HEADS_PER_TRIP, n_heads)

        def head_group(p, carry):
            for u in range(per_trip):
                head(p * per_trip + u)
            return carry

        if n_heads == per_trip:
            head_group(0, 0)
        else:
            lax.fori_loop(0, n_heads // per_trip, head_group, 0)

    piece = lambda p: pl.BlockSpec((CHUNK, d), lambda n, p=p: (n, p))
    return pl.pallas_call(
        body, grid=(nc,),
        in_specs=[piece(3), piece(4), piece(5), pl.BlockSpec((2, d), lambda n: (0, 0)),
                  pl.BlockSpec((n_tab, CHUNK), lambda n: (0, 0))],
        out_specs=[pl.BlockSpec((CHUNK, d), lambda n: (n, 0)), pl.BlockSpec((1, d, HEAD), lambda n: (n, 0, 0))],
        out_shape=[jax.ShapeDtypeStruct((lp, d), F32), jax.ShapeDtypeStruct((nc, d, HEAD), F32)],
        scratch_shapes=[pltpu.VMEM((d, HEAD), F32), pltpu.VMEM((n_tab, d), F32)],
        name="hgrn_fwd", compiler_params=_params())(proj, proj, proj, lb_logits, tab)


def _lower_bound_slice(lbl_ref, hs):
    l0 = lbl_ref[0:1, hs]
    l1 = lbl_ref[1:2, hs]
    m = jnp.maximum(l0, l1)
    e0 = jnp.exp(l0 - m)
    e1 = jnp.exp(l1 - m)
    return e0 / (e0 + e1)


def _tail_fwd(o, proj, y_conv, hres, target, gnorm_g, final_g, w_rec, w_out):
    lp, d = o.shape
    n_heads = d // HEAD
    tm = _row_tile(lp)

    n_slabs = tm // CHUNK

    def body(o_ref, gr_ref, mc_ref, mr_ref, yc_ref, x_ref, gn_ref, fg_ref, wr_ref, wo_ref, *rest):
        t_refs = rest[:n_slabs]
        yrin_ref, mg_ref, yrec_ref, dout_ref, loss_ref, dfg_ref = rest[n_slabs:]
        i = pl.program_id(0)

        @pl.when(i == 0)
        def _():
            loss_ref[...] = jnp.zeros_like(loss_ref)
            dfg_ref[...] = jnp.zeros_like(dfg_ref)

        for h in range(n_heads):
            hs = slice(h * HEAD, (h + 1) * HEAD)
            oh = o_ref[:, hs]
            on = oh * lax.rsqrt(jnp.mean(oh * oh, axis=-1, keepdims=True) + EPS) * gn_ref[:, hs]
            gr = gr_ref[:, hs]
            yrin_ref[:, hs] = (on * (gr * _sigmoid(gr))).astype(BF16)
        yrec = jnp.dot(yrin_ref[...], wr_ref[...], preferred_element_type=F32)
        yrec_ref[...] = yrec
        merged = (_sigmoid(mc_ref[...]) * yc_ref[...] + _sigmoid(mr_ref[...]) * yrec).astype(BF16)
        mg_ref[...] = merged
        out = x_ref[...] + jnp.dot(merged, wo_ref[...], preferred_element_type=F32)
        r = lax.rsqrt(jnp.mean(out * out, axis=-1, keepdims=True) + EPS)
        yhat = out * r
        fg = fg_ref[...]
        rid = lax.broadcasted_iota(jnp.int32, (tm, 1), 0) + i * tm
        tgt = jnp.concatenate([t[...] for t in t_refs], axis=0)
        err = jnp.where(rid >= CHUNK, yhat * fg - tgt, 0.0)
        loss_ref[...] += 0.5 * jnp.sum(err * err) / d
        dy = err / d
        dfg_ref[...] += jnp.sum(dy * yhat, axis=0, keepdims=True)
        dyh = dy * fg
        dout_ref[...] = r * (dyh - yhat * jnp.mean(dyh * yhat, axis=-1, keepdims=True))

    row = lambda p: pl.BlockSpec((tm, d), lambda i, p=p: (i, p))
    vec = pl.BlockSpec((1, d), lambda i: (0, 0))
    mat = pl.BlockSpec((d, d), lambda i: (0, 0))
    return pl.pallas_call(
        body, grid=(lp // tm,),
        in_specs=[row(0), row(6), row(7), row(8), row(0), row(0), vec, vec, mat, mat]
        + [pl.BlockSpec((CHUNK, d), lambda i, u=u: (jnp.maximum(i * n_slabs + u - 1, 0), 0)) for u in range(n_slabs)],
        out_specs=[row(0), row(0), row(0), row(0), pl.BlockSpec((8, 128), lambda i: (0, 0)), vec],
        out_shape=[jax.ShapeDtypeStruct((lp, d), BF16), jax.ShapeDtypeStruct((lp, d), BF16),
                   jax.ShapeDtypeStruct((lp, d), F32), jax.ShapeDtypeStruct((lp, d), F32),
                   jax.ShapeDtypeStruct((8, 128), F32), jax.ShapeDtypeStruct((1, d), F32)],
        name="tail_fwd", compiler_params=_params())(
            o, proj, proj, proj, y_conv, hres, gnorm_g, final_g, w_rec, w_out, *([target] * n_slabs))


def _tail_bwd(dout, proj, y_conv, y_rec, o, c, w_out, w_rec, w_conv, ln_g, ln_b, gnorm_g):
    lp, d = dout.shape
    n_heads = d // HEAD
    tm = _row_tile(lp)

    def body(dout_ref, mc_ref, mr_ref, z_ref, gr_ref, yc_ref, yrec_ref, o_ref, c_ref,
             wo_ref, wr_ref, wc_ref, lg_ref, lb_ref, gn_ref,
             dyc_ref, dyr_ref, doutb_ref, dz_ref, dp_ref, do_ref, dc_ref,
             dgn_ref, dlg_ref, dlb_ref, dyrin_ref):
        i = pl.program_id(0)

        @pl.when(i == 0)
        def _():
            dgn_ref[...] = jnp.zeros_like(dgn_ref)
            dlg_ref[...] = jnp.zeros_like(dlg_ref)
            dlb_ref[...] = jnp.zeros_like(dlb_ref)

        doutb = dout_ref[...].astype(BF16)
        doutb_ref[...] = doutb
        dmerged = lax.dot_general(doutb, wo_ref[...], NT, preferred_element_type=F32)
        smc = _sigmoid(mc_ref[...])
        smr = _sigmoid(mr_ref[...])
        dyc = (dmerged * smc).astype(BF16)
        dyr = (dmerged * smr).astype(BF16)
        dyc_ref[...] = dyc
        dyr_ref[...] = dyr
        dp_ref[:, d:2 * d] = (dmerged * yc_ref[...] * smc * (1.0 - smc)).astype(BF16)
        dp_ref[:, 2 * d:3 * d] = (dmerged * yrec_ref[...] * smr * (1.0 - smr)).astype(BF16)

        dyrin_ref[...] = lax.dot_general(dyr, wr_ref[...], NT, preferred_element_type=F32)
        for h in range(n_heads):
            hs = slice(h * HEAD, (h + 1) * HEAD)
            oh = o_ref[:, hs]
            rstd = lax.rsqrt(jnp.mean(oh * oh, axis=-1, keepdims=True) + EPS)
            ohat = oh * rstd
            gn = gn_ref[:, hs]
            gr = gr_ref[:, hs]
            sg = _sigmoid(gr)
            dyrin = dyrin_ref[:, hs]
            don = dyrin * (gr * sg)
            dp_ref[:, hs] = (dyrin * (ohat * gn) * _dsilu(gr, sg)).astype(BF16)
            dgn_ref[:, hs] += jnp.sum(don * ohat, axis=0, keepdims=True)
            doh = don * gn
            do_ref[:, hs] = rstd * (doh - ohat * jnp.mean(doh * ohat, axis=-1, keepdims=True))

        dycin = lax.dot_general(dyc, wc_ref[...], NT, preferred_element_type=F32)
        c = c_ref[...]
        mu = jnp.mean(c, axis=-1, keepdims=True)
        xc = c - mu
        rstd = lax.rsqrt(jnp.mean(xc * xc, axis=-1, keepdims=True) + EPS)
        nrm = xc * rstd
        lg = lg_ref[...]
        ln = nrm * lg + lb_ref[...]
        sl = _sigmoid(ln)
        z = z_ref[...]
        sz = _sigmoid(z)
        dz_ref[...] = (dycin * (ln * sl) * _dsilu(z, sz)).astype(BF16)
        dln = dycin * (z * sz) * _dsilu(ln, sl)
        dlg_ref[...] += jnp.sum(dln * nrm, axis=0, keepdims=True)
        dlb_ref[...] += jnp.sum(dln, axis=0, keepdims=True)
        dn = dln * lg
        dc_ref[...] = rstd * (dn - jnp.mean(dn, axis=-1, keepdims=True)
                              - nrm * jnp.mean(dn * nrm, axis=-1, keepdims=True))

    row = lambda p: pl.BlockSpec((tm, d), lambda i, p=p: (i, p))
    vec = pl.BlockSpec((1, d), lambda i: (0, 0))
    mat = pl.BlockSpec((d, d), lambda i: (0, 0))
    act_bf = jax.ShapeDtypeStruct((lp, d), BF16)
    act_f32 = jax.ShapeDtypeStruct((lp, d), F32)
    vec_f32 = jax.ShapeDtypeStruct((1, d), F32)
    return pl.pallas_call(
        body, grid=(lp // tm,),
        in_specs=[row(0), row(7), row(8), row(2), row(6), row(0), row(0), row(0), row(0),
                  mat, mat, mat, vec, vec, vec],
        out_specs=[row(0)] * 4 + [pl.BlockSpec((tm, 3 * d), lambda i: (i, 2))] + [row(0)] * 2 + [vec] * 3,
        out_shape=[act_bf] * 4 + [jax.ShapeDtypeStruct((lp, 9 * d), BF16)] + [act_f32] * 2 + [vec_f32] * 3,
        scratch_shapes=[pltpu.VMEM((tm, d), F32)],
        name="tail_bwd", compiler_params=_params())(
            dout, proj, proj, proj, proj, y_conv, y_rec, o, c, w_out, w_rec, w_conv, ln_g, ln_b, gnorm_g)


def _hgrn_bwd(proj, do, s_all, lb_logits, n_pad, dproj):
    lp, d = do.shape
    n_heads = d // HEAD
    nc = lp // CHUNK
    tab, utri = _hgrn_tables()
    n_tab = tab.shape[0]

    def body(qr_ref, fr_ref, ir_ref, do_ref, s0_ref, lbl_ref, tab_ref, ut_ref, _,
             dp_ref, dlbl_ref, ds_ref, t_ref, dlb_ref):
        n = pl.program_id(0)
        chunk = nc - 1 - n

        @pl.when(n == 0)
        def _():
            ds_ref[...] = jnp.zeros_like(ds_ref)
            dlb_ref[...] = jnp.zeros_like(dlb_ref)

        lb_all, pp = _lower_bound(lbl_ref)
        rid = lax.broadcasted_iota(jnp.int32, (CHUNK, 1), 0)
        valid = jnp.logical_or(chunk > 0, rid >= n_pad)
        f_all = lb_all + (1.0 - lb_all) * _sigmoid(fr_ref[...])
        t_ref[...] = _dot2(tab_ref[...], jnp.where(valid, jnp.log(f_all), 0.0))
        masks = _level_masks()
        ut = ut_ref[...]

        def head(h):
            off = h * HEAD if isinstance(h, int) else pl.multiple_of(h * HEAD, HEAD)
            hs = pl.ds(off, HEAD)
            lb = _lower_bound_slice(lbl_ref, hs)
            qr = qr_ref[:, hs]
            q, sq, f, sf, _, k = _gates(qr, fr_ref[:, hs], lb, valid)
            v = ir_ref[:, hs]
            do_h = do_ref[:, hs]
            b = t_ref[0:CHUNK, hs]
            b_last = t_ref[CHUNK - 1:CHUNK, hs]
            s0 = s0_ref[0, hs, :]
            ds1 = ds_ref[hs, :]
            eb = jnp.exp(b)
            ekl = jnp.exp(b_last - b)
            do_bf = do_h.astype(BF16)
            v_bf = v.astype(BF16)
            ds1_bf = ds1.astype(BF16)

            da = lax.dot_general(do_bf, v_bf, NT, preferred_element_type=F32)
            da_diag = jnp.sum(do_h * v, axis=-1, keepdims=True)
            a = jnp.zeros((CHUNK, CHUNK), F32)
            dq_x = eb * lax.dot_general(do_bf, s0.astype(BF16), NT, preferred_element_type=F32)
            dk_x = ekl * lax.dot_general(v_bf, ds1_bf, NT, preferred_element_type=F32)
            x_after = q * dq_x
            x_before = k * dk_x
            for lvl in range(1, N_LEVELS + 1):
                e = _level_factor(b, _level_reference(lvl, b, t_ref, hs))
                qt = (q * e).astype(BF16)
                kt = (k * e).astype(BF16)
                p = lax.dot_general(qt, kt, NT, preferred_element_type=F32)
                a = a + jnp.where(masks[lvl - 1], p, 0.0)
                dam = jnp.where(masks[lvl - 1], da, 0.0).astype(BF16)
                dqt = jnp.dot(dam, kt, preferred_element_type=F32)
                dkt = lax.dot_general(dam, qt, TN, preferred_element_type=F32)
                dq_x = dq_x + e * dqt
                dk_x = dk_x + e * dkt
                x_after = x_after + (qt.astype(F32) * dqt - kt.astype(F32) * dkt)

            dv = (lax.dot_general(a.astype(BF16), do_bf, TN, preferred_element_type=F32)
                  + jnp.sum(q * k, axis=-1, keepdims=True) * do_h
                  + jnp.dot((k * ekl).astype(BF16), ds1_bf, preferred_element_type=F32))
            dp_ref[:, pl.ds(2 * d + off, HEAD)] = dv.astype(BF16)

            carried = jnp.exp(b_last) * _col_to_row(jnp.sum(s0 * ds1, axis=-1, keepdims=True))
            dg = _dot3(ut, jnp.concatenate([x_after, x_before], axis=0)) + carried
            dq = dq_x + da_diag * k
            dk = dk_x + da_diag * q
            dp_ref[:, hs] = (dq * _dsilu(qr, sq)).astype(BF16)
            df = jnp.where(valid, dg / f - dk, 0.0)
            dp_ref[:, pl.ds(d + off, HEAD)] = (df * (1.0 - lb) * sf * (1.0 - sf)).astype(BF16)
            dlb_ref[:, hs] += jnp.sum(df * (1.0 - sf), axis=0, keepdims=True)

            ds_ref[hs, :] = (_row_to_col(jnp.exp(b_last)) * ds1
                             + lax.dot_general((q * eb).astype(BF16), do_bf, TN, preferred_element_type=F32))
        per_trip = min(HEADS_PER_TRIP, n_heads)

        def head_group(p, carry):
            for u in range(per_trip):
                head(p * per_trip + u)
            return carry

        if n_heads == per_trip:
            head_group(0, 0)
        else:
            lax.fori_loop(0, n_heads // per_trip, head_group, 0)

        @pl.when(n == nc - 1)
        def _():
            dl0 = dlb_ref[...] * pp
            dlbl_ref[0:1, :] = dl0
            dlbl_ref[1:2, :] = -dl0

    piece = lambda p: pl.BlockSpec((CHUNK, d), lambda n, p=p: (nc - 1 - n, p))
    return pl.pallas_call(
        body, grid=(nc,),
        in_specs=[piece(3), piece(4), piece(5), piece(0),
                  pl.BlockSpec((1, d, HEAD), lambda n: (nc - 1 - n, 0, 0)),
                  pl.BlockSpec((2, d), lambda n: (0, 0)),
                  pl.BlockSpec((n_tab, CHUNK), lambda n: (0, 0)),
                  pl.BlockSpec((CHUNK, 2 * CHUNK), lambda n: (0, 0)), ANY],
        out_specs=[pl.BlockSpec((CHUNK, 3 * d), lambda n: (nc - 1 - n, 1)), pl.BlockSpec((2, d), lambda n: (0, 0))],
        out_shape=[jax.ShapeDtypeStruct(dproj.shape, BF16), jax.ShapeDtypeStruct((2, d), F32)],
        input_output_aliases={8: 0},
        scratch_shapes=[pltpu.VMEM((d, HEAD), F32), pltpu.VMEM((n_tab, d), F32), pltpu.VMEM((1, d), F32)],
        name="hgrn_bwd", compiler_params=_params())(proj, proj, proj, do, s_all, lb_logits, tab, utri, dproj)


def _conv_bwd(dc, proj, conv_w, dz, dproj):
    lp, d = dc.shape
    tm = _row_tile(lp)
    hb = tm // HALO
    n_tiles = lp // tm
    last_halo = lp // HALO - 1

    def body(dc_ref, dcn_ref, ua_ref, ub_ref, uap_ref, ubp_ref, cw_ref, dz_ref, _,
             dp_ref, dcw_ref, dcb_ref, aext_ref, dcext_ref, da_ref, dcw_acc):
        i = pl.program_id(0)

        @pl.when(i == 0)
        def _():
            dcw_acc[...] = jnp.zeros_like(dcw_acc)
            dcb_ref[...] = jnp.zeros_like(dcb_ref)

        ua = ua_ref[...]
        sb = _sigmoid(ub_ref[...])
        a_prev = uap_ref[...] * _sigmoid(ubp_ref[...])
        aext_ref[0:HALO, :] = jnp.where(i > 0, a_prev, 0.0)
        aext_ref[HALO:HALO + tm, :] = ua * sb
        dcext_ref[0:tm, :] = dc_ref[...]
        dcext_ref[tm:tm + HALO, :] = jnp.where(i < n_tiles - 1, dcn_ref[...], 0.0)
        dcb_ref[...] += jnp.sum(dc_ref[...], axis=0, keepdims=True)

        def row_block(r, carry):
            r0 = pl.multiple_of(r * CONV_ROWS, CONV_ROWS)
            n_rows = CONV_ROWS + HALO
            for cs in range(d // CONV_LANES):
                cl = slice(cs * CONV_LANES, (cs + 1) * CONV_LANES)
                dblk = dcext_ref[pl.ds(r0, n_rows), cl]
                ablk = aext_ref[pl.ds(r0, n_rows), cl]
                dcur = dblk[0:CONV_ROWS, :]
                acc = jnp.zeros((CONV_ROWS, CONV_LANES), F32)
                for b in range(8):
                    dsh = dblk if b == 0 else pltpu.roll(dblk, n_rows - b, axis=0)
                    ash = ablk if b == 0 else pltpu.roll(ablk, n_rows - b, axis=0)
                    for a in range(5):
                        j_da = CONV_WIDTH - 1 - (8 * a + b)
                        if 0 <= j_da < CONV_WIDTH:
                            acc = acc + cw_ref[j_da:j_da + 1, cl] * dsh[8 * a:8 * a + CONV_ROWS, :]
                        j_w = 8 * a + b - 2
                        if 0 <= j_w < CONV_WIDTH:
                            prod = dcur * ash[8 * a:8 * a + CONV_ROWS, :]
                            dcw_acc[j_w, :, cl] += prod.reshape(CONV_ROWS // 8, 8, CONV_LANES).sum(axis=0)
                da_ref[pl.ds(r0, CONV_ROWS), cl] = acc
            return carry

        lax.fori_loop(0, tm // CONV_ROWS, row_block, 0)

        da = da_ref[...]
        dp_ref[:, 0:d] = (da * sb).astype(BF16)
        dp_ref[:, d:2 * d] = (da * ua * sb * (1.0 - sb)).astype(BF16)
        dp_ref[:, 2 * d:3 * d] = dz_ref[...]

        @pl.when(i == n_tiles - 1)
        def _():
            dcw_ref[...] = jnp.sum(dcw_acc[...], axis=1)

    row = lambda p: pl.BlockSpec((tm, d), lambda i, p=p: (i, p))
    prev = lambda p: pl.BlockSpec((HALO, d), lambda i, p=p: (jnp.maximum(i * hb - 1, 0), p))
    nxt = pl.BlockSpec((HALO, d), lambda i: (jnp.minimum((i + 1) * hb, last_halo), 0))
    return pl.pallas_call(
        body, grid=(n_tiles,),
        in_specs=[row(0), nxt, row(0), row(1), prev(0), prev(1), pl.BlockSpec((HALO, d), lambda i: (0, 0)),
                  row(0), ANY],
        out_specs=[pl.BlockSpec((tm, 3 * d), lambda i: (i, 0)), pl.BlockSpec((HALO, d), lambda i: (0, 0)),
                   pl.BlockSpec((1, d), lambda i: (0, 0))],
        out_shape=[jax.ShapeDtypeStruct(dproj.shape, BF16),
                   jax.ShapeDtypeStruct((HALO, d), F32), jax.ShapeDtypeStruct((1, d), F32)],
        input_output_aliases={8: 0},
        scratch_shapes=[pltpu.VMEM((HALO + tm, d), F32), pltpu.VMEM((tm + HALO, d), F32), pltpu.VMEM((tm, d), F32),
                        pltpu.VMEM((HALO, 8, d), F32)],
        name="conv_bwd", compiler_params=_params())(dc, dc, proj, proj, proj, proj, conv_w, dz, dproj)


def _weight_grad(xs, dy, name, blocked):
    lp, dx = xs.shape
    n = dy.shape[1]
    tk = _mm_row_tile(lp)
    if blocked:
        ncol = n // N_CHIPS
        nt = W_IN_COL_TILES
        tn = ncol // nt
        grid = (N_CHIPS * nt, lp // tk)
        out_spec = pl.BlockSpec((1, dx, tn), lambda c, k: (c // nt, 0, c % nt))
        out_shape = jax.ShapeDtypeStruct((N_CHIPS, dx, ncol), F32)
    else:
        tn = n // 2
        grid = (2, lp // tk)
        out_spec = pl.BlockSpec((dx, tn), lambda c, k: (0, c))
        out_shape = jax.ShapeDtypeStruct((dx, n), F32)

    def body(xs_ref, dy_ref, o_ref):
        @pl.when(pl.program_id(1) == 0)
        def _():
            o_ref[...] = jnp.zeros_like(o_ref)

        p = lax.dot_general(xs_ref[...], dy_ref[...], TN, preferred_element_type=F32)
        if blocked:
            o_ref[0] += p
        else:
            o_ref[...] += p

    return pl.pallas_call(
        body, grid=grid,
        in_specs=[pl.BlockSpec((tk, dx), lambda c, k: (k, 0)), pl.BlockSpec((tk, tn), lambda c, k: (k, c))],
        out_specs=out_spec, out_shape=out_shape,
        name=name, compiler_params=_params())(xs, dy)


def _in_proj_bwd(dproj, wtg, hres, norm_g, dout):
    lp, d = hres.shape
    _, ncol, _ = wtg.shape
    tm = _mm_row_tile(lp)
    nt = W_IN_COL_TILES
    tn = ncol // nt
    nk = N_CHIPS * nt

    def body(dp_ref, w_ref, x_ref, g_ref, dout_ref, dx_ref, dg_ref, acc_ref):
        i = pl.program_id(0)
        kk = pl.program_id(1)

        @pl.when(jnp.logical_and(i == 0, kk == 0))
        def _():
            dg_ref[...] = jnp.zeros_like(dg_ref)

        @pl.when(kk == 0)
        def _():
            acc_ref[...] = jnp.zeros_like(acc_ref)

        acc_ref[...] += jnp.dot(dp_ref[...], w_ref[0], preferred_element_type=F32)

        @pl.when(kk == nk - 1)
        def _():
            x = x_ref[...]
            r = lax.rsqrt(jnp.mean(x * x, axis=-1, keepdims=True) + EPS)
            xhat = x * r
            dh = acc_ref[...]
            dg_ref[...] += jnp.sum(dh * xhat, axis=0, keepdims=True)
            dxh = dh * g_ref[...]
            dx_ref[...] = dout_ref[...] + r * (dxh - xhat * jnp.mean(dxh * xhat, axis=-1, keepdims=True))

    return pl.pallas_call(
        body, grid=(lp // tm, nk),
        in_specs=[pl.BlockSpec((tm, tn), lambda i, k: (i, k)),
                  pl.BlockSpec((1, tn, d), lambda i, k: (k // nt, k % nt, 0)),
                  pl.BlockSpec((tm, d), lambda i, k: (i, 0)),
                  pl.BlockSpec((1, d), lambda i, k: (0, 0)),
                  pl.BlockSpec((tm, d), lambda i, k: (i, 0))],
        out_specs=[pl.BlockSpec((tm, d), lambda i, k: (i, 0)), pl.BlockSpec((1, d), lambda i, k: (0, 0))],
        out_shape=[jax.ShapeDtypeStruct((lp, d), F32), jax.ShapeDtypeStruct((1, d), F32)],
        scratch_shapes=[pltpu.VMEM((tm, d), F32)],
        name="in_proj_bwd", compiler_params=_params())(dproj, wtg, hres, norm_g, dout)


def _adamw_math(w, g, m, v):
    m = ADAM_B1 * m + (1.0 - ADAM_B1) * g
    v = ADAM_B2 * v + (1.0 - ADAM_B2) * (g * g)
    m_hat = m / (1.0 - ADAM_B1 ** ADAM_STEP)
    v_hat = v / (1.0 - ADAM_B2 ** ADAM_STEP)
    delta = -ADAM_LR * (m_hat / (jnp.sqrt(v_hat) + ADAM_EPS) + ADAM_WD * w)
    return delta, m, v


def _elementwise_rows(shape):
    r, c = shape
    for t in (256, 128, 64, 32, 16, 8):
        if r % t == 0 and r > t and t * c * 4 <= 1024 * 1024:
            return t
    return r


def _adamw(name, w, m, v, *g_parts):
    shape = w.shape
    tr = _elementwise_rows(shape)
    n_g = len(g_parts)

    def body(*refs):
        w_ref, m_ref, v_ref = refs[:3]
        g_refs = refs[3:3 + n_g]
        g_out, d_out, m_out, v_out = refs[3 + n_g:]
        g = g_refs[0][...]
        for gr in g_refs[1:]:
            g = g + gr[...]
        delta, m_new, v_new = _adamw_math(w_ref[...], g, m_ref[...], v_ref[...])
        g_out[...] = g
        d_out[...] = delta
        m_out[...] = m_new
        v_out[...] = v_new

    spec = pl.BlockSpec((tr, shape[1]), lambda i: (i, 0))
    return pl.pallas_call(
        body, grid=(shape[0] // tr,),
        in_specs=[spec] * (3 + n_g), out_specs=[spec] * 4,
        out_shape=[jax.ShapeDtypeStruct(shape, F32)] * 4,
        name=name, compiler_params=_params())(w, m, v, *g_parts)


def _chip_half_sum(name, g, recv, core):
    _, _, hr, cols = g.shape
    tr = _elementwise_rows((hr, cols))

    def body(core_ref, g_ref, r_ref, o_ref, ob_ref):
        s = g_ref[0, 0] + r_ref[0]
        o_ref[0] = s
        ob_ref[0] = s.astype(BF16)

    blk = pl.BlockSpec((1, tr, cols), lambda j, i, core_ref: (j, i, 0))
    grid_spec = pltpu.PrefetchScalarGridSpec(
        num_scalar_prefetch=1, grid=(N_CHIPS, hr // tr),
        in_specs=[pl.BlockSpec((1, 1, tr, cols), lambda j, i, core_ref: (j, core_ref[0], i, 0)), blk],
        out_specs=[blk, blk])
    return pl.pallas_call(
        body, grid_spec=grid_spec,
        out_shape=[jax.ShapeDtypeStruct((N_CHIPS, hr, cols), F32), jax.ShapeDtypeStruct((N_CHIPS, hr, cols), BF16)],
        name=name, compiler_params=_params())(core, g, recv)


def _block_half_total(name, chip_sums, recv, chip_core, after):
    _, hr, cols = chip_sums.shape
    tr = _elementwise_rows((hr, cols))

    def body(cc_ref, p_ref, r_ref, after_ref, o_ref):
        s = p_ref[0]
        for k in range(3):
            s = s + r_ref[k].astype(F32)
        o_ref[0] = s

    grid_spec = pltpu.PrefetchScalarGridSpec(
        num_scalar_prefetch=1, grid=(hr // tr,),
        in_specs=[pl.BlockSpec((1, tr, cols), lambda i, cc_ref: (cc_ref[0], i, 0)),
                  pl.BlockSpec((3, tr, cols), lambda i, cc_ref: (0, i, 0)), ANY],
        out_specs=pl.BlockSpec((1, tr, cols), lambda i, cc_ref: (cc_ref[1], i, 0)))
    return pl.pallas_call(
        body, grid_spec=grid_spec, out_shape=jax.ShapeDtypeStruct((2, hr, cols), F32),
        name=name, compiler_params=_params())(chip_core, chip_sums, recv, after)


def _place_shard(name, w, chip, dtype):
    r, c = w.shape
    tr = _elementwise_rows((r, c))

    def body(chip_ref, w_ref, o_ref):
        o_ref[0] = w_ref[...].astype(dtype)

    grid_spec = pltpu.PrefetchScalarGridSpec(
        num_scalar_prefetch=1, grid=(r // tr,),
        in_specs=[pl.BlockSpec((tr, c), lambda i, chip_ref: (i, 0))],
        out_specs=pl.BlockSpec((1, tr, c), lambda i, chip_ref: (chip_ref[0], i, 0)))
    return pl.pallas_call(
        body, grid_spec=grid_spec, out_shape=jax.ShapeDtypeStruct((N_CHIPS, r, c), dtype),
        name=name, compiler_params=_params())(chip, w)


def _sum_slots(name, slots, own, my_idx):
    k, r, c = slots.shape

    def body(idx_ref, s_ref, own_ref, o_ref):
        s = None
        for j in range(k):
            term = jnp.where(idx_ref[0] == j, own_ref[...], s_ref[j])
            s = term if s is None else s + term
        o_ref[...] = s

    grid_spec = pltpu.PrefetchScalarGridSpec(
        num_scalar_prefetch=1, grid=(1,),
        in_specs=[pl.BlockSpec((k, r, c), lambda i, idx_ref: (0, 0, 0)),
                  pl.BlockSpec((r, c), lambda i, idx_ref: (0, 0))],
        out_specs=pl.BlockSpec((r, c), lambda i, idx_ref: (0, 0)))
    return pl.pallas_call(body, grid_spec=grid_spec, out_shape=jax.ShapeDtypeStruct((r, c), F32), name=name,
                          compiler_params=_params())(my_idx, slots, own)


def _mesh_pos():
    return lax.axis_index("x"), lax.axis_index("y"), lax.axis_index("c")


def _other_chips(x, y):
    return [(1 - x, y), (x, 1 - y), (1 - x, 1 - y)]


def _gather_weights(bufs):
    n = len(bufs)
    half = [b.shape[1] // 2 for b in bufs]

    def body(*refs):
        gathered = refs[n:2 * n]
        ici_send, ici_recv, d2d_send, d2d_recv = refs[2 * n:]
        x, y, c = _mesh_pos()
        me = 2 * x + y
        chips = _other_chips(x, y)

        def part(a, block, core):
            return gathered[a].at[block, pl.ds(core * half[a], half[a])]

        def over_ici(a, k, block):
            px, py = chips[k]
            return pltpu.make_async_remote_copy(
                src_ref=part(a, block, c), dst_ref=part(a, block, c),
                send_sem=ici_send.at[a, k], recv_sem=ici_recv.at[a, k],
                device_id=(px, py, c), device_id_type=MESH)

        def over_d2d(a, k, core):
            px, py = chips[k]
            return pltpu.make_async_remote_copy(
                src_ref=part(a, 2 * px + py, core), dst_ref=part(a, 2 * px + py, core),
                send_sem=d2d_send.at[a, k], recv_sem=d2d_recv.at[a, k],
                device_id=(x, y, 1 - c), device_id_type=MESH)

        for a in range(n):
            for k in range(3):
                over_ici(a, k, me).start()
        for a in range(n):
            for k, (px, py) in enumerate(chips):
                over_ici(a, k, 2 * px + py).wait_recv()
                over_d2d(a, k, c).start()
        for a in range(n):
            for k in range(3):
                over_d2d(a, k, 1 - c).wait_recv()
        for a in range(n):
            for k in range(3):
                over_ici(a, k, me).wait_send()
                over_d2d(a, k, c).wait_send()

    return pl.pallas_call(
        body, in_specs=[ANY] * n, out_specs=[ANY] * n,
        out_shape=[jax.ShapeDtypeStruct(b.shape, b.dtype) for b in bufs],
        input_output_aliases={a: a for a in range(n)},
        scratch_shapes=[pltpu.SemaphoreType.DMA((n, 3))] * 4,
        name="gather_weights")(*bufs)


def _gather_in_proj(h, bufs, order):
    n = len(bufs)
    half = [b.shape[1] // 2 for b in bufs]
    lp, d = h.shape
    ncol = bufs[0].shape[2]
    tm = _mm_row_tile(lp)
    n_row = lp // tm

    def body(order_ref, h_ref, *refs):
        gathered = refs[n:2 * n]
        o_ref, wt_ref = refs[2 * n], refs[2 * n + 1]
        w_buf, ici_send, ici_recv, d2d_send, d2d_recv, w_sem = refs[2 * n + 2:]
        j = pl.program_id(0)
        i = pl.program_id(1)
        x, y, c = _mesh_pos()
        me = 2 * x + y
        chips = _other_chips(x, y)

        def part(a, block, core):
            return gathered[a].at[block, pl.ds(core * half[a], half[a])]

        def over_ici(a, k, block):
            px, py = chips[k]
            return pltpu.make_async_remote_copy(
                src_ref=part(a, block, c), dst_ref=part(a, block, c),
                send_sem=ici_send.at[a, k], recv_sem=ici_recv.at[a, k],
                device_id=(px, py, c), device_id_type=MESH)

        def over_d2d(a, k, core):
            px, py = chips[k]
            return pltpu.make_async_remote_copy(
                src_ref=part(a, 2 * px + py, core), dst_ref=part(a, 2 * px + py, core),
                send_sem=d2d_send.at[a, k], recv_sem=d2d_recv.at[a, k],
                device_id=(x, y, 1 - c), device_id_type=MESH)

        @pl.when(jnp.logical_and(j == 0, i == 0))
        def _():
            for a in range(n):
                for k in range(2):
                    over_ici(a, k, me).start()

        for k, (px, py) in enumerate(chips):
            @pl.when(jnp.logical_and(j == k + 1, i == 0))
            def _(k=k, px=px, py=py):
                for a in range(n):
                    over_ici(a, k, 2 * px + py).wait_recv()
                    over_d2d(a, k, c).start()
                if k == 0:
                    for a in range(n):
                        over_ici(a, 2, me).start()
                for a in range(n):
                    over_d2d(a, k, 1 - c).wait_recv()

        @pl.when(i == 0)
        def _():
            load = pltpu.make_async_copy(gathered[0].at[order_ref[j]], w_buf, w_sem)
            load.start()
            load.wait()
            wt_ref[0] = w_buf[...].T

        o_ref[...] = jnp.dot(h_ref[...], w_buf[...], preferred_element_type=F32)

        @pl.when(jnp.logical_and(j == N_CHIPS - 1, i == n_row - 1))
        def _():
            for a in range(n):
                for k in range(3):
                    over_ici(a, k, me).wait_send()
                    over_d2d(a, k, c).wait_send()

    grid_spec = pltpu.PrefetchScalarGridSpec(
        num_scalar_prefetch=1, grid=(N_CHIPS, n_row),
        in_specs=[pl.BlockSpec((tm, d), lambda j, i, order_ref: (i, 0))] + [ANY] * n,
        out_specs=[ANY] * n + [pl.BlockSpec((tm, ncol), lambda j, i, order_ref: (i, order_ref[j])),
                               pl.BlockSpec((1, ncol, d), lambda j, i, order_ref: (order_ref[j], 0, 0))],
        scratch_shapes=[pltpu.VMEM((d, ncol), BF16)] + [pltpu.SemaphoreType.DMA((n, 3))] * 4
        + [pltpu.SemaphoreType.DMA])
    out = pl.pallas_call(
        body, grid_spec=grid_spec,
        out_shape=[jax.ShapeDtypeStruct(b.shape, b.dtype) for b in bufs]
        + [jax.ShapeDtypeStruct((lp, N_CHIPS * ncol), F32), jax.ShapeDtypeStruct((N_CHIPS, ncol, d), BF16)],
        input_output_aliases={2 + a: a for a in range(n)},
        name="gather_in_proj", compiler_params=_params())(order, h, *bufs)
    return out[n], out[n + 1], out[:n]


def _send_other_halves(grads, tag):
    n = len(grads)

    def body(*refs):
        srcs = refs[:n]
        dsts = refs[n:2 * n]
        send_sems, recv_sems = refs[2 * n:]
        x, y, c = _mesh_pos()
        copies = [pltpu.make_async_remote_copy(
            src_ref=srcs[a].at[j, 1 - c], dst_ref=dsts[a].at[j], send_sem=send_sems.at[a, j],
            recv_sem=recv_sems.at[a, j], device_id=(x, y, 1 - c), device_id_type=MESH)
            for a in range(n) for j in range(N_CHIPS)]
        for cp in copies:
            cp.start()
        for cp in copies:
            cp.wait()

    return pl.pallas_call(
        body, in_specs=[ANY] * n, out_specs=[ANY] * n,
        out_shape=[jax.ShapeDtypeStruct((N_CHIPS,) + g.shape[2:], F32) for g in grads],
        scratch_shapes=[pltpu.SemaphoreType.DMA((n, N_CHIPS))] * 2,
        name="send_other_halves_" + tag)(*grads)


HBM = pl.BlockSpec(memory_space=pltpu.HBM)
SEM = pl.BlockSpec(memory_space=pltpu.SEMAPHORE)


def _block_copies(n, srcs, dsts, send_sems, recv_sems):
    x, y, c = _mesh_pos()
    return [pltpu.make_async_remote_copy(
        src_ref=srcs[a].at[2 * px + py], dst_ref=dsts[a].at[k], send_sem=send_sems.at[3 * a + k],
        recv_sem=recv_sems.at[3 * a + k], device_id=(px, py, c), device_id_type=MESH)
        for a in range(n) for k, (px, py) in enumerate(_other_chips(x, y))]


def _exchange_start(blocked, tag):
    n = len(blocked)
    lands = [lax.empty((3,) + b.shape[1:], b.dtype) for b in blocked]
    bufs = [pltpu.with_memory_space_constraint(b, pltpu.HBM) for b in list(blocked) + lands]
    nb = 2 * n

    def body(*refs):
        for cp in _block_copies(n, refs[:n], refs[n:nb], refs[nb], refs[nb + 1]):
            cp.start()
        refs[-1][...] = jnp.zeros_like(refs[-1])

    out = pl.pallas_call(
        body, name="exchange_start_" + tag,
        in_specs=[HBM] * nb,
        out_shape=[pltpu.SemaphoreType.DMA((3 * n,)), pltpu.SemaphoreType.DMA((3 * n,))]
        + [pltpu.HBM(b.shape, b.dtype) for b in bufs] + [jax.ShapeDtypeStruct((8, 128), F32)],
        out_specs=[SEM] * 2 + [HBM] * nb + [pl.BlockSpec(memory_space=pltpu.VMEM)],
        input_output_aliases={i: 2 + i for i in range(nb)},
        compiler_params=pltpu.CompilerParams(has_side_effects=pltpu.SideEffectType.DATAFLOW_SIDE_EFFECTING),
    )(*bufs)
    return (out[:2], out[2:2 + nb]), out[-1]


def _exchange_wait(state, after, tag):
    sems, bufs = state
    nb = len(bufs)
    n = nb // 2

    def body(*refs):
        for cp in _block_copies(n, refs[:n], refs[n:nb], refs[nb], refs[nb + 1]):
            cp.wait_send()
            cp.wait_recv()

    out = pl.pallas_call(
        body, name="exchange_wait_" + tag,
        in_specs=[HBM] * nb + [SEM] * 2 + [ANY],
        out_shape=[pltpu.HBM(b.shape, b.dtype) for b in bufs],
        out_specs=[HBM] * nb,
        input_output_aliases={i: i for i in range(nb)},
        compiler_params=pltpu.CompilerParams(has_side_effects=pltpu.SideEffectType.DATAFLOW_SIDE_EFFECTING),
    )(*bufs, *sems, after)
    return out[n:nb]


def _whole_block_copies(buf, send_sems, recv_sems, incoming):
    x, y, c = _mesh_pos()
    me = 2 * x + y
    out = []
    for k, (px, py) in enumerate(_other_chips(x, y)):
        block = 2 * px + py if incoming else me
        out.append(pltpu.make_async_remote_copy(
            src_ref=buf.at[block], dst_ref=buf.at[block], send_sem=send_sems.at[k], recv_sem=recv_sems.at[k],
            device_id=(px, py, c), device_id_type=MESH))
    return out


def _gather_start(buf, after, tag):
    buf = pltpu.with_memory_space_constraint(buf, pltpu.HBM)

    def body(buf_ref, after_ref, send_sems, recv_sems, thru_ref, token):
        for cp in _whole_block_copies(buf_ref, send_sems, recv_sems, incoming=False):
            cp.start()
        token[...] = jnp.zeros_like(token)

    out = pl.pallas_call(
        body, name="gather_start_" + tag,
        in_specs=[HBM, ANY],
        out_shape=[pltpu.SemaphoreType.DMA((3,)), pltpu.SemaphoreType.DMA((3,)), pltpu.HBM(buf.shape, buf.dtype),
                   jax.ShapeDtypeStruct((8, 128), F32)],
        out_specs=[SEM, SEM, HBM, pl.BlockSpec(memory_space=pltpu.VMEM)],
        input_output_aliases={0: 2},
        compiler_params=pltpu.CompilerParams(has_side_effects=pltpu.SideEffectType.DATAFLOW_SIDE_EFFECTING),
    )(buf, after)
    return out[:3], out[3]


def _gather_wait(state, after, tag):
    send_sems, recv_sems, buf = state

    def body(buf_ref, send_ref, recv_ref, after_ref, out_ref):
        for cp in _whole_block_copies(buf_ref, send_ref, recv_ref, incoming=True):
            cp.wait_send()
            cp.wait_recv()

    return pl.pallas_call(
        body, name="gather_wait_" + tag,
        in_specs=[HBM, SEM, SEM, ANY],
        out_shape=pltpu.HBM(buf.shape, buf.dtype), out_specs=HBM,
        input_output_aliases={0: 0},
        compiler_params=pltpu.CompilerParams(has_side_effects=pltpu.SideEffectType.DATAFLOW_SIDE_EFFECTING),
    )(buf, send_sems, recv_sems, after)


def _small_copies(small_ref, slots_ref, send_sems, recv_sems, incoming):
    x, y, c = _mesh_pos()
    out = []
    for r in range(1, 8):
        px = 1 - x if r & 4 else x
        py = 1 - y if r & 2 else y
        pc = 1 - c if r & 1 else c
        slot = 4 * px + 2 * py + pc if incoming else 4 * x + 2 * y + c
        out.append(pltpu.make_async_remote_copy(
            src_ref=small_ref, dst_ref=slots_ref.at[slot], send_sem=send_sems.at[r - 1],
            recv_sem=recv_sems.at[r - 1], device_id=(px, py, pc), device_id_type=MESH))
    return out


def _small_start(small):
    bufs = [pltpu.with_memory_space_constraint(b, pltpu.HBM)
            for b in (small, lax.empty((8,) + small.shape, small.dtype))]

    def body(small_ref, slots_ref, send_sems, recv_sems, small_thru, slots_thru, token):
        for cp in _small_copies(small_ref, slots_ref, send_sems, recv_sems, incoming=False):
            cp.start()
        token[...] = jnp.zeros_like(token)

    out = pl.pallas_call(
        body, name="small_start",
        in_specs=[HBM, HBM],
        out_shape=[pltpu.SemaphoreType.DMA((7,)), pltpu.SemaphoreType.DMA((7,))]
        + [pltpu.HBM(b.shape, b.dtype) for b in bufs] + [jax.ShapeDtypeStruct((8, 128), F32)],
        out_specs=[SEM, SEM, HBM, HBM, pl.BlockSpec(memory_space=pltpu.VMEM)],
        input_output_aliases={0: 2, 1: 3},
        compiler_params=pltpu.CompilerParams(has_side_effects=pltpu.SideEffectType.DATAFLOW_SIDE_EFFECTING),
    )(*bufs)
    return out[:4], out[4]


def _small_wait(state, after):
    send_sems, recv_sems, small, slots = state

    def body(small_ref, slots_ref, send_ref, recv_ref, after_ref, small_out, slots_out):
        for cp in _small_copies(small_ref, slots_ref, send_ref, recv_ref, incoming=True):
            cp.wait_send()
            cp.wait_recv()

    return pl.pallas_call(
        body, name="small_wait",
        in_specs=[HBM, HBM, SEM, SEM, ANY],
        out_shape=[pltpu.HBM(small.shape, small.dtype), pltpu.HBM(slots.shape, slots.dtype)],
        out_specs=[HBM, HBM], input_output_aliases={0: 0, 1: 1},
        compiler_params=pltpu.CompilerParams(has_side_effects=pltpu.SideEffectType.DATAFLOW_SIDE_EFFECTING),
    )(small, slots, send_sems, recv_sems, after)[1]


def _join_halves(bufs):
    n = len(bufs)

    def body(*refs):
        joined = refs[n:2 * n]
        send_sems, recv_sems = refs[2 * n:]
        x, y, c = _mesh_pos()
        for a in range(n):
            pltpu.make_async_remote_copy(
                src_ref=joined[a].at[c], dst_ref=joined[a].at[c], send_sem=send_sems.at[a],
                recv_sem=recv_sems.at[a], device_id=(x, y, 1 - c), device_id_type=MESH).start()
        for a in range(n):
            pltpu.make_async_remote_copy(
                src_ref=joined[a].at[c], dst_ref=joined[a].at[1 - c], send_sem=send_sems.at[a],
                recv_sem=recv_sems.at[a], device_id=(x, y, 1 - c), device_id_type=MESH).wait()

    return pl.pallas_call(
        body, in_specs=[ANY] * n, out_specs=[ANY] * n,
        out_shape=[jax.ShapeDtypeStruct(b.shape, b.dtype) for b in bufs],
        input_output_aliases={a: a for a in range(n)},
        scratch_shapes=[pltpu.SemaphoreType.DMA((n,))] * 2,
        name="join_halves")(*bufs)


def kernel(x, meta_tokens, norm_g, w_in, conv_w, conv_b, ln_g, ln_b, w_conv_out, lb_logits, gnorm_g, w_rec_out, w_out, final_g, loss_target, m_meta_tokens, m_norm_g, m_w_in, m_conv_w, m_conv_b, m_ln_g, m_ln_b, m_w_conv_out, m_lb_logits, m_gnorm_g, m_w_rec_out, m_w_out, m_final_g, v_meta_tokens, v_norm_g, v_w_in, v_conv_w, v_conv_b, v_ln_g, v_ln_b, v_w_conv_out, v_lb_logits, v_gnorm_g, v_w_rec_out, v_w_out, v_final_g):
    seq, d = x.shape[1], x.shape[2]
    n_meta = meta_tokens.shape[0]
    n_pad = CHUNK - n_meta
    ds = d // N_CHIPS
    chip = 2 * lax.axis_index("x") + lax.axis_index("y")

    conv_w_pad = jnp.pad(conv_w[0], ((0, HALO - CONV_WIDTH), (0, 0)))
    chip_idx = chip.astype(jnp.int32).reshape(1)
    (small_g,) = _gather_weights([
        _place_shard("place_small", jnp.concatenate([conv_w_pad, meta_tokens], axis=0), chip_idx, F32)])
    cw_full = jnp.transpose(small_g[:, 0:HALO], (1, 0, 2)).reshape(HALO, d)
    meta_full = jnp.transpose(small_g[:, HALO:HALO + n_meta], (1, 0, 2)).reshape(n_meta, d)

    hres = jnp.concatenate([jnp.zeros((n_pad, d), F32), meta_full, x[0]], axis=0)
    target = loss_target[0]
    final_g2 = final_g.reshape(1, d)
    h = _rmsnorm_fwd(hres, norm_g)
    fx, fy = 1 - lax.axis_index("x"), 1 - lax.axis_index("y")
    order = jnp.stack([chip, 2 * fx + (1 - fy), 2 * (1 - fx) + fy, 2 * fx + fy]).astype(jnp.int32)
    proj, win_t, _ = _gather_in_proj(h, [_place_shard("place_w_in", w_in[0], chip_idx, BF16)], order)
    sq_own = _place_shard("place_square", jnp.concatenate([w_conv_out[0], w_rec_out[0], w_out[0]], axis=0),
                          chip_idx, BF16)
    sq_flight, sq_token = _gather_start(sq_own, proj, "square")
    o, s_all = _hgrn_fwd(proj, lb_logits + sq_token[0:1, 0:1], n_pad)
    sq_g = _gather_wait(sq_flight, s_all, "square")
    wc_full = sq_g[:, 0:ds].reshape(d, d)
    wr_full = sq_g[:, ds:2 * ds].reshape(d, d)
    wo_full = sq_g[:, 2 * ds:3 * ds].reshape(d, d)
    c, yc_in, y_conv = _conv_fwd(proj, cw_full, conv_b, ln_g, ln_b, wc_full)
    yr_in, merged, y_rec, dout, loss_acc, dfinal_g = _tail_fwd(
        o, proj, y_conv, hres, target, gnorm_g, final_g2, wr_full, wo_full)

    (dyc, dyr, dout_bf, dz, dproj, do, dc, dgnorm_g, dln_g, dln_b) = _tail_bwd(
        dout, proj, y_conv, y_rec, o, c, wo_full, wr_full, wc_full, ln_g, ln_b, gnorm_g)
    g_wc = _weight_grad(yc_in, dyc, "grad_w_conv_out", False)
    g_wr = _weight_grad(yr_in, dyr, "grad_w_rec_out", False)
    g_wo = _weight_grad(merged, dout_bf, "grad_w_out", False)

    core = lax.axis_index("c").astype(jnp.int32).reshape(1)

    def chip_sum_and_start(g, tag):
        g = g.reshape(N_CHIPS, 2, g.shape[1] // 2, g.shape[2])
        (from_sibling,) = _send_other_halves([g], tag)
        sums = _chip_half_sum("chip_half_sum_" + tag, g, from_sibling, core)
        in_flight, token = _exchange_start([sums[1]], tag)
        return sums[0], in_flight, token[0:1, 0:1]

    g_sq = jnp.concatenate([g.reshape(N_CHIPS, ds, d) for g in (g_wc, g_wr, g_wo)], axis=1)
    sum_sq, flight_sq, token_sq = chip_sum_and_start(g_sq, "square")
    dproj, dlb_logits = _hgrn_bwd(proj, do, s_all, lb_logits + token_sq, n_pad, dproj)
    dproj, dconv_w, dconv_b = _conv_bwd(dc, proj, cw_full, dz, dproj)
    (recv_sq,) = _exchange_wait(flight_sq, dconv_b, "square")
    g_win = _weight_grad(h, dproj, "grad_w_in", True)
    sum_win, flight_win, token_win = chip_sum_and_start(g_win, "w_in")
    dhres, dnorm_g = _in_proj_bwd(dproj, win_t, hres, norm_g + token_win, dout)
    grad_x = dhres[CHUNK:][None]
    (recv_win,) = _exchange_wait(flight_win, dnorm_g, "w_in")
    small = jnp.concatenate([dnorm_g, dconv_b, dln_g, dln_b, dlb_logits, dgnorm_g, dfinal_g,
                             dhres[n_pad:CHUNK], dconv_w[:CONV_WIDTH],
                             jnp.broadcast_to(loss_acc[0:1, 0:1], (1, d))], axis=0)
    small_flight, small_token = _small_start(small)
    chip_core = jnp.concatenate([chip_idx, core])
    totals = [_block_half_total("block_half_total_" + nm, s, r, chip_core, small_token)
              for nm, s, r in zip(("w_in", "square"), (sum_win, sum_sq), (recv_win, recv_sq))]
    joined = _join_halves(totals)
    gt_win, gt_sq = [t.reshape(2 * t.shape[1], t.shape[2]) for t in joined]
    small_slots = _small_wait(small_flight, joined[1])
    device_idx = (2 * chip_idx + core).astype(jnp.int32)
    small_sum = _sum_slots("sum_small", small_slots, small, device_idx)

    res = {}
    res["w_in"] = _adamw("adamw_w_in", w_in[0], m_w_in[0], v_w_in[0], gt_win)
    res["w_conv_out"] = _adamw("adamw_w_conv_out", w_conv_out[0], m_w_conv_out[0], v_w_conv_out[0], gt_sq[0:ds])
    res["w_rec_out"] = _adamw("adamw_w_rec_out", w_rec_out[0], m_w_rec_out[0], v_w_rec_out[0], gt_sq[ds:2 * ds])
    res["w_out"] = _adamw("adamw_w_out", w_out[0], m_w_out[0], v_w_out[0], gt_sq[2 * ds:3 * ds])
    big = {k: tuple(a[None] for a in v) for k, v in res.items()}

    rep_names = ("norm_g", "conv_b", "ln_g", "ln_b", "lb_logits", "gnorm_g", "final_g")
    rep_w = (norm_g, conv_b, ln_g, ln_b, lb_logits, gnorm_g, final_g2)
    rep_m = (m_norm_g, m_conv_b, m_ln_g, m_ln_b, m_lb_logits, m_gnorm_g, m_final_g.reshape(1, d))
    rep_v = (v_norm_g, v_conv_b, v_ln_g, v_ln_b, v_lb_logits, v_gnorm_g, v_final_g.reshape(1, d))
    rep = _adamw("adamw_replicated", jnp.concatenate(rep_w, 0), jnp.concatenate(rep_m, 0),
                 jnp.concatenate(rep_v, 0), small_sum[0:8])
    rep_rows = {"norm_g": (0, 1), "conv_b": (1, 2), "ln_g": (2, 3), "ln_b": (3, 4), "lb_logits": (4, 6),
                "gnorm_g": (6, 7), "final_g": (7, 8)}
    small_out = {}
    for nm in rep_names:
        lo, hi = rep_rows[nm]
        vals = tuple(a[lo:hi] for a in rep)
        if nm == "final_g":
            vals = tuple(a.reshape(d) for a in vals)
        small_out[nm] = vals
    cw_row = 8 + n_meta
    g_meta = lax.dynamic_slice_in_dim(small_sum[8:cw_row], chip * ds, ds, axis=1)
    small_out["meta_tokens"] = _adamw("adamw_meta", meta_tokens, m_meta_tokens, v_meta_tokens, g_meta)
    g_cw = lax.dynamic_slice_in_dim(small_sum[cw_row:cw_row + HALO], chip * ds, ds, axis=1)
    pad_rows = ((0, HALO - CONV_WIDTH), (0, 0))
    cw_res = _adamw("adamw_conv_w", conv_w_pad, jnp.pad(m_conv_w[0], pad_rows),
                    jnp.pad(v_conv_w[0], pad_rows, constant_values=1.0), g_cw)
    small_out["conv_w"] = tuple(a[:CONV_WIDTH][None] for a in cw_res)

    loss = small_sum[cw_row + HALO - 1, 0]

    order = ("meta_tokens", "norm_g", "w_in", "conv_w", "conv_b", "ln_g", "ln_b", "w_conv_out", "lb_logits",
             "gnorm_g", "w_rec_out", "w_out", "final_g")
    allres = {**big, **small_out}
    outs = [loss, grad_x]
    for field in range(4):
        outs.extend(allres[nm][field] for nm in order)
    return tuple(outs)
```

```python
import numpy as np

import jax
import jax.numpy as jnp
from jax import lax
from jax.experimental import pallas as pl
from jax.experimental.pallas import tpu as pltpu

F32 = jnp.float32
BF16 = jnp.bfloat16

EPS = 1e-6
CHUNK = 64
N_LEVELS = 6
FIRST_TABLE_LEVEL = 5
CONV_WIDTH = 31
HALO = 32
CONV_ROWS = 32
CONV_LANES = 256
HEAD = 128
W_IN_COL_TILES = 1
HEADS_PER_TRIP = 8
N_CHIPS = 4
VMEM_LIMIT_BYTES = 56 * 1024 * 1024

ADAM_LR = 0.001
ADAM_B1 = 0.9
ADAM_B2 = 0.999
ADAM_EPS = 1e-08
ADAM_WD = 0.01
ADAM_STEP = 10

MESH = pl.DeviceIdType.MESH
ANY = pl.BlockSpec(memory_space=pl.ANY)

NT = (((1,), (1,)), ((), ()))
TN = (((0,), (0,)), ((), ()))


def _params(**kw):
    return pltpu.CompilerParams(vmem_limit_bytes=VMEM_LIMIT_BYTES, **kw)


def _sigmoid(x):
    return jax.nn.sigmoid(x)


def _dsilu(x, s):
    return s * (1.0 + x * (1.0 - s))


def _row_tile(lp):
    for t in (320, 256, 192, 128, 64):
        if lp % t == 0:
            return t
    raise ValueError(f"unsupported padded length {lp}")


def _mm_row_tile(lp):
    for t in (832, 640, 320, 256, 192, 128, 64):
        if lp % t == 0:
            return t
    raise ValueError(f"unsupported padded length {lp}")


def _dot3(m_bf16, x):
    hi = x.astype(BF16)
    r1 = x - hi.astype(F32)
    mid = r1.astype(BF16)
    lo = (r1 - mid.astype(F32)).astype(BF16)
    return (jnp.dot(m_bf16, hi, preferred_element_type=F32)
            + jnp.dot(m_bf16, mid, preferred_element_type=F32)
            + jnp.dot(m_bf16, lo, preferred_element_type=F32))


def _dot2(m_bf16, x):
    hi = x.astype(BF16)
    lo = (x - hi.astype(F32)).astype(BF16)
    return (jnp.dot(m_bf16, hi, preferred_element_type=F32)
            + jnp.dot(m_bf16, lo, preferred_element_type=F32))


def _col_to_row(col):
    return jnp.broadcast_to(col, (HEAD, 8)).T[0:1, :]


def _row_to_col(row):
    return jnp.broadcast_to(row, (8, HEAD)).T[:, 0:1]


def _hgrn_tables():
    t = np.arange(CHUNK)
    ltri = (t[None, :] <= t[:, None]).astype(np.float32)
    mats = [ltri]
    for lvl in range(FIRST_TABLE_LEVEL, N_LEVELS + 1):
        blk = CHUNK >> (lvl - 1)
        mid = (t // blk) * blk + blk // 2
        mats.append(ltri[mid - 1])
    after = (t[None, :] >= t[:, None]).astype(np.float32)
    before = (t[None, :] < t[:, None]).astype(np.float32)
    return jnp.asarray(np.concatenate(mats, 0), BF16), jnp.asarray(np.concatenate([after, before], 1), BF16)


def _rmsnorm_fwd(hres, g):
    lp, d = hres.shape
    tm = _row_tile(lp)

    def body(x_ref, g_ref, h_ref):
        x = x_ref[...]
        r = lax.rsqrt(jnp.mean(x * x, axis=-1, keepdims=True) + EPS)
        h_ref[...] = (x * r * g_ref[...]).astype(BF16)

    return pl.pallas_call(
        body, grid=(lp // tm,),
        in_specs=[pl.BlockSpec((tm, d), lambda i: (i, 0)), pl.BlockSpec((1, d), lambda i: (0, 0))],
        out_specs=pl.BlockSpec((tm, d), lambda i: (i, 0)),
        out_shape=jax.ShapeDtypeStruct((lp, d), BF16),
        name="rmsnorm_fwd", compiler_params=_params())(hres, g)


def _conv_fwd(proj, conv_w, conv_b, ln_g, ln_b, w_conv):
    lp = proj.shape[0]
    d = conv_b.shape[1]
    tm = _row_tile(lp)
    hb = tm // HALO

    def body(ua_ref, ub_ref, z_ref, uap_ref, ubp_ref, cw_ref, cb_ref, lg_ref, lb_ref, w_ref,
             c_ref, ycin_ref, yconv_ref, aext_ref):
        i = pl.program_id(0)
        a_prev = uap_ref[...] * _sigmoid(ubp_ref[...])
        aext_ref[0:HALO, :] = jnp.where(i > 0, a_prev, 0.0)
        aext_ref[HALO:HALO + tm, :] = ua_ref[...] * _sigmoid(ub_ref[...])

        def row_block(r, carry):
            r0 = pl.multiple_of(r * CONV_ROWS, CONV_ROWS)
            for cs in range(d // CONV_LANES):
                cl = slice(cs * CONV_LANES, (cs + 1) * CONV_LANES)
                blk = aext_ref[pl.ds(r0, CONV_ROWS + HALO), cl]
                acc = jnp.zeros((CONV_ROWS, CONV_LANES), F32) + cb_ref[:, cl]
                for b in range(8):
                    sh = blk if b == 0 else pltpu.roll(blk, CONV_ROWS + HALO - b, axis=0)
                    for a in range(5):
                        j = 8 * a + b - 2
                        if 0 <= j < CONV_WIDTH:
                            acc = acc + cw_ref[j:j + 1, cl] * sh[8 * a:8 * a + CONV_ROWS, :]
                c_ref[pl.ds(r0, CONV_ROWS), cl] = acc
            return carry

        lax.fori_loop(0, tm // CONV_ROWS, row_block, 0)

        c = c_ref[...]
        mu = jnp.mean(c, axis=-1, keepdims=True)
        xc = c - mu
        rstd = lax.rsqrt(jnp.mean(xc * xc, axis=-1, keepdims=True) + EPS)
        ln = xc * rstd * lg_ref[...] + lb_ref[...]
        s = ln * _sigmoid(ln)
        z = z_ref[...]
        ycin = (s * (z * _sigmoid(z))).astype(BF16)
        ycin_ref[...] = ycin
        yconv_ref[...] = jnp.dot(ycin, w_ref[...], preferred_element_type=F32)

    row = lambda p: pl.BlockSpec((tm, d), lambda i, p=p: (i, p))
    halo = lambda p: pl.BlockSpec((HALO, d), lambda i, p=p: (jnp.maximum(i * hb - 1, 0), p))
    vec = pl.BlockSpec((1, d), lambda i: (0, 0))
    return pl.pallas_call(
        body, grid=(lp // tm,),
        in_specs=[row(0), row(1), row(2), halo(0), halo(1),
                  pl.BlockSpec((HALO, d), lambda i: (0, 0)), vec, vec, vec,
                  pl.BlockSpec((d, d), lambda i: (0, 0))],
        out_specs=[pl.BlockSpec((tm, d), lambda i: (i, 0))] * 3,
        out_shape=[jax.ShapeDtypeStruct((lp, d), F32), jax.ShapeDtypeStruct((lp, d), BF16),
                   jax.ShapeDtypeStruct((lp, d), F32)],
        scratch_shapes=[pltpu.VMEM((HALO + tm, d), F32)],
        name="conv_fwd", compiler_params=_params())(
            proj, proj, proj, proj, proj, conv_w, conv_b, ln_g, ln_b, w_conv)


def _lower_bound(lbl_ref):
    l0 = lbl_ref[0:1, :]
    l1 = lbl_ref[1:2, :]
    m = jnp.maximum(l0, l1)
    e0 = jnp.exp(l0 - m)
    e1 = jnp.exp(l1 - m)
    p0 = e0 / (e0 + e1)
    return p0, p0 * (e1 / (e0 + e1))


def _level_masks():
    r2 = lax.broadcasted_iota(jnp.int32, (CHUNK, CHUNK), 0)
    c2 = lax.broadcasted_iota(jnp.int32, (CHUNK, CHUNK), 1)
    out = []
    for lvl in range(1, N_LEVELS + 1):
        blk = CHUNK >> (lvl - 1)
        sh = blk.bit_length() - 1
        same = (r2 >> sh) == (c2 >> sh)
        t_upper = (r2 & (blk - 1)) >= (blk // 2)
        s_lower = (c2 & (blk - 1)) < (blk // 2)
        out.append(jnp.logical_and(same, jnp.logical_and(t_upper, s_lower)))
    return out


def _gates(qr, fr, lb, valid):
    sq = _sigmoid(qr)
    q = qr * sq
    sf = _sigmoid(fr)
    f = lb + (1.0 - lb) * sf
    g = jnp.where(valid, jnp.log(f), 0.0)
    k = jnp.where(valid, 1.0 - f, 0.0)
    return q, sq, f, sf, g, k


def _level_reference(lvl, b, t_ref, hs):
    if lvl >= FIRST_TABLE_LEVEL:
        base = CHUNK * (lvl - FIRST_TABLE_LEVEL + 1)
        return t_ref[base:base + CHUNK, hs]
    blk = CHUNK >> (lvl - 1)
    rows = [jnp.broadcast_to(b[m + blk // 2 - 1:m + blk // 2, :], (blk, HEAD)) for m in range(0, CHUNK, blk)]
    return rows[0] if len(rows) == 1 else jnp.concatenate(rows, axis=0)


def _level_factor(b, r):
    d = b - r
    return jnp.exp(jnp.minimum(d, -d))


def _hgrn_fwd(proj, lb_logits, n_pad):
    lp = proj.shape[0]
    d = lb_logits.shape[1]
    n_heads = d // HEAD
    nc = lp // CHUNK
    tab, _ = _hgrn_tables()
    n_tab = tab.shape[0]

    def body(qr_ref, fr_ref, ir_ref, lbl_ref, tab_ref, o_ref, sall_ref, s_ref, t_ref):
        n = pl.program_id(0)

        @pl.when(n == 0)
        def _():
            s_ref[...] = jnp.zeros_like(s_ref)

        sall_ref[0] = s_ref[...]
        lb_all, _ = _lower_bound(lbl_ref)
        rid = lax.broadcasted_iota(jnp.int32, (CHUNK, 1), 0)
        valid = jnp.logical_or(n > 0, rid >= n_pad)
        f_all = lb_all + (1.0 - lb_all) * _sigmoid(fr_ref[...])
        t_ref[...] = _dot2(tab_ref[...], jnp.where(valid, jnp.log(f_all), 0.0))
        masks = _level_masks()

        def head(h):
            off = h * HEAD if isinstance(h, int) else pl.multiple_of(h * HEAD, HEAD)
            hs = pl.ds(off, HEAD)
            lb = _lower_bound_slice(lbl_ref, hs)
            q, _, _, _, _, k = _gates(qr_ref[:, hs], fr_ref[:, hs], lb, valid)
            v = ir_ref[:, hs]
            b = t_ref[0:CHUNK, hs]
            s0 = s_ref[hs, :]
            o = jnp.dot((q * jnp.exp(b)).astype(BF16), s0.astype(BF16), preferred_element_type=F32)
            o = o + jnp.sum(q * k, axis=-1, keepdims=True) * v
            a = jnp.zeros((CHUNK, CHUNK), F32)
            for lvl in range(1, N_LEVELS + 1):
                e = _level_factor(b, _level_reference(lvl, b, t_ref, hs))
                p = lax.dot_general((q * e).astype(BF16), (k * e).astype(BF16), NT, preferred_element_type=F32)
                a = a + jnp.where(masks[lvl - 1], p, 0.0)
            vb = v.astype(BF16)
            o_ref[:, hs] = o + jnp.dot(a.astype(BF16), vb, preferred_element_type=F32)
            b_last = t_ref[CHUNK - 1:CHUNK, hs]
            khat = (k * jnp.exp(b_last - b)).astype(BF16)
            s_ref[hs, :] = _row_to_col(jnp.exp(b_last)) * s0 + lax.dot_general(khat, vb, TN, preferred_element_type=F32)
        per_trip = min(HEADS_PER_TRIP, n_heads)

        def head_group(p, carry):
            for u in range(per_trip):
                head(p * per_trip + u)
            return carry

        if n_heads == per_trip:
            head_group(0, 0)
        else:
            lax.fori_loop(0, n_heads // per_trip, head_group, 0)

    piece = lambda p: pl.BlockSpec((CHUNK, d), lambda n, p=p: (n, p))
    return pl.pallas_call(
        body, grid=(nc,),
        in_specs=[piece(3), piece(4), piece(5), pl.BlockSpec((2, d), lambda n: (0, 0)),
                  pl.BlockSpec((n_tab, CHUNK), lambda n: (0, 0))],
        out_specs=[pl.BlockSpec((CHUNK, d), lambda n: (n, 0)), pl.BlockSpec((1, d, HEAD), lambda n: (n, 0, 0))],
        out_shape=[jax.ShapeDtypeStruct((lp, d), F32), jax.ShapeDtypeStruct((nc, d, HEAD), F32)],
        scratch_shapes=[pltpu.VMEM((d, HEAD), F32), pltpu.VMEM((n_tab, d), F32)],
        name="hgrn_fwd", compiler_params=_params())(proj, proj, proj, lb_logits, tab)


def _lower_bound_slice(lbl_ref, hs):
    l0 = lbl_ref[0:1, hs]
    l1 = lbl_ref[1:2, hs]
    m = jnp.maximum(l0, l1)
    e0 = jnp.exp(l0 - m)
    e1 = jnp.exp(l1 - m)
    return e0 / (e0 + e1)


def _tail_fwd(o, proj, y_conv, hres, target, gnorm_g, final_g, w_rec, w_out):
    lp, d = o.shape
    n_heads = d // HEAD
    tm = _row_tile(lp)

    n_slabs = tm // CHUNK

    def body(o_ref, gr_ref, mc_ref, mr_ref, yc_ref, x_ref, gn_ref, fg_ref, wr_ref, wo_ref, *rest):
        t_refs = rest[:n_slabs]
        yrin_ref, mg_ref, yrec_ref, dout_ref, loss_ref, dfg_ref = rest[n_slabs:]
        i = pl.program_id(0)

        @pl.when(i == 0)
        def _():
            loss_ref[...] = jnp.zeros_like(loss_ref)
            dfg_ref[...] = jnp.zeros_like(dfg_ref)

        for h in range(n_heads):
            hs = slice(h * HEAD, (h + 1) * HEAD)
            oh = o_ref[:, hs]
            on = oh * lax.rsqrt(jnp.mean(oh * oh, axis=-1, keepdims=True) + EPS) * gn_ref[:, hs]
            gr = gr_ref[:, hs]
            yrin_ref[:, hs] = (on * (gr * _sigmoid(gr))).astype(BF16)
        yrec = jnp.dot(yrin_ref[...], wr_ref[...], preferred_element_type=F32)
        yrec_ref[...] = yrec
        merged = (_sigmoid(mc_ref[...]) * yc_ref[...] + _sigmoid(mr_ref[...]) * yrec).astype(BF16)
        mg_ref[...] = merged
        out = x_ref[...] + jnp.dot(merged, wo_ref[...], preferred_element_type=F32)
        r = lax.rsqrt(jnp.mean(out * out, axis=-1, keepdims=True) + EPS)
        yhat = out * r
        fg = fg_ref[...]
        rid = lax.broadcasted_iota(jnp.int32, (tm, 1), 0) + i * tm
        tgt = jnp.concatenate([t[...] for t in t_refs], axis=0)
        err = jnp.where(rid >= CHUNK, yhat * fg - tgt, 0.0)
        loss_ref[...] += 0.5 * jnp.sum(err * err) / d
        dy = err / d
        dfg_ref[...] += jnp.sum(dy * yhat, axis=0, keepdims=True)
        dyh = dy * fg
        dout_ref[...] = r * (dyh - yhat * jnp.mean(dyh * yhat, axis=-1, keepdims=True))

    row = lambda p: pl.BlockSpec((tm, d), lambda i, p=p: (i, p))
    vec = pl.BlockSpec((1, d), lambda i: (0, 0))
    mat = pl.BlockSpec((d, d), lambda i: (0, 0))
    return pl.pallas_call(
        body, grid=(lp // tm,),
        in_specs=[row(0), row(6), row(7), row(8), row(0), row(0), vec, vec, mat, mat]
        + [pl.BlockSpec((CHUNK, d), lambda i, u=u: (jnp.maximum(i * n_slabs + u - 1, 0), 0)) for u in range(n_slabs)],
        out_specs=[row(0), row(0), row(0), row(0), pl.BlockSpec((8, 128), lambda i: (0, 0)), vec],
        out_shape=[jax.ShapeDtypeStruct((lp, d), BF16), jax.ShapeDtypeStruct((lp, d), BF16),
                   jax.ShapeDtypeStruct((lp, d), F32), jax.ShapeDtypeStruct((lp, d), F32),
                   jax.ShapeDtypeStruct((8, 128), F32), jax.ShapeDtypeStruct((1, d), F32)],
        name="tail_fwd", compiler_params=_params())(
            o, proj, proj, proj, y_conv, hres, gnorm_g, final_g, w_rec, w_out, *([target] * n_slabs))


def _tail_bwd(dout, proj, y_conv, y_rec, o, c, w_out, w_rec, w_conv, ln_g, ln_b, gnorm_g):
    lp, d = dout.shape
    n_heads = d // HEAD
    tm = _row_tile(lp)

    def body(dout_ref, mc_ref, mr_ref, z_ref, gr_ref, yc_ref, yrec_ref, o_ref, c_ref,
             wo_ref, wr_ref, wc_ref, lg_ref, lb_ref, gn_ref,
             dyc_ref, dyr_ref, doutb_ref, dz_ref, dp_ref, do_ref, dc_ref,
             dgn_ref, dlg_ref, dlb_ref, dyrin_ref):
        i = pl.program_id(0)

        @pl.when(i == 0)
        def _():
            dgn_ref[...] = jnp.zeros_like(dgn_ref)
            dlg_ref[...] = jnp.zeros_like(dlg_ref)
            dlb_ref[...] = jnp.zeros_like(dlb_ref)

        doutb = dout_ref[...].astype(BF16)
        doutb_ref[...] = doutb
        dmerged = lax.dot_general(doutb, wo_ref[...], NT, preferred_element_type=F32)
        smc = _sigmoid(mc_ref[...])
        smr = _sigmoid(mr_ref[...])
        dyc = (dmerged * smc).astype(BF16)
        dyr = (dmerged * smr).astype(BF16)
        dyc_ref[...] = dyc
        dyr_ref[...] = dyr
        dp_ref[:, d:2 * d] = (dmerged * yc_ref[...] * smc * (1.0 - smc)).astype(BF16)
        dp_ref[:, 2 * d:3 * d] = (dmerged * yrec_ref[...] * smr * (1.0 - smr)).astype(BF16)

        dyrin_ref[...] = lax.dot_general(dyr, wr_ref[...], NT, preferred_element_type=F32)
        for h in range(n_heads):
            hs = slice(h * HEAD, (h + 1) * HEAD)
            oh = o_ref[:, hs]
            rstd = lax.rsqrt(jnp.mean(oh * oh, axis=-1, keepdims=True) + EPS)
            ohat = oh * rstd
            gn = gn_ref[:, hs]
            gr = gr_ref[:, hs]
            sg = _sigmoid(gr)
            dyrin = dyrin_ref[:, hs]
            don = dyrin * (gr * sg)
            dp_ref[:, hs] = (dyrin * (ohat * gn) * _dsilu(gr, sg)).astype(BF16)
            dgn_ref[:, hs] += jnp.sum(don * ohat, axis=0, keepdims=True)
            doh = don * gn
            do_ref[:, hs] = rstd * (doh - ohat * jnp.mean(doh * ohat, axis=-1, keepdims=True))

        dycin = lax.dot_general(dyc, wc_ref[...], NT, preferred_element_type=F32)
        c = c_ref[...]
        mu = jnp.mean(c, axis=-1, keepdims=True)
        xc = c - mu
        rstd = lax.rsqrt(jnp.mean(xc * xc, axis=-1, keepdims=True) + EPS)
        nrm = xc * rstd
        lg = lg_ref[...]
        ln = nrm * lg + lb_ref[...]
        sl = _sigmoid(ln)
        z = z_ref[...]
        sz = _sigmoid(z)
        dz_ref[...] = (dycin * (ln * sl) * _dsilu(z, sz)).astype(BF16)
        dln = dycin * (z * sz) * _dsilu(ln, sl)
        dlg_ref[...] += jnp.sum(dln * nrm, axis=0, keepdims=True)
        dlb_ref[...] += jnp.sum(dln, axis=0, keepdims=True)
        dn = dln * lg
        dc_ref[...] = rstd * (dn - jnp.mean(dn, axis=-1, keepdims=True)
                              - nrm * jnp.mean(dn * nrm, axis=-1, keepdims=True))

    row = lambda p: pl.BlockSpec((tm, d), lambda i, p=p: (i, p))
    vec = pl.BlockSpec((1, d), lambda i: (0, 0))
    mat = pl.BlockSpec((d, d), lambda i: (0, 0))
    act_bf = jax.ShapeDtypeStruct((lp, d), BF16)
    act_f32 = jax.ShapeDtypeStruct((lp, d), F32)
    vec_f32 = jax.ShapeDtypeStruct((1, d), F32)
    return pl.pallas_call(
        body, grid=(lp // tm,),
        in_specs=[row(0), row(7), row(8), row(2), row(6), row(0), row(0), row(0), row(0),
                  mat, mat, mat, vec, vec, vec],
        out_specs=[row(0)] * 4 + [pl.BlockSpec((tm, 3 * d), lambda i: (i, 2))] + [row(0)] * 2 + [vec] * 3,
        out_shape=[act_bf] * 4 + [jax.ShapeDtypeStruct((lp, 9 * d), BF16)] + [act_f32] * 2 + [vec_f32] * 3,
        scratch_shapes=[pltpu.VMEM((tm, d), F32)],
        name="tail_bwd", compiler_params=_params())(
            dout, proj, proj, proj, proj, y_conv, y_rec, o, c, w_out, w_rec, w_conv, ln_g, ln_b, gnorm_g)


def _hgrn_bwd(proj, do, s_all, lb_logits, n_pad, dproj):
    lp, d = do.shape
    n_heads = d // HEAD
    nc = lp // CHUNK
    tab, utri = _hgrn_tables()
    n_tab = tab.shape[0]

    def body(qr_ref, fr_ref, ir_ref, do_ref, s0_ref, lbl_ref, tab_ref, ut_ref, _,
             dp_ref, dlbl_ref, ds_ref, t_ref, dlb_ref):
        n = pl.program_id(0)
        chunk = nc - 1 - n

        @pl.when(n == 0)
        def _():
            ds_ref[...] = jnp.zeros_like(ds_ref)
            dlb_ref[...] = jnp.zeros_like(dlb_ref)

        lb_all, pp = _lower_bound(lbl_ref)
        rid = lax.broadcasted_iota(jnp.int32, (CHUNK, 1), 0)
        valid = jnp.logical_or(chunk > 0, rid >= n_pad)
        f_all = lb_all + (1.0 - lb_all) * _sigmoid(fr_ref[...])
        t_ref[...] = _dot2(tab_ref[...], jnp.where(valid, jnp.log(f_all), 0.0))
        masks = _level_masks()
        ut = ut_ref[...]

        def head(h):
            off = h * HEAD if isinstance(h, int) else pl.multiple_of(h * HEAD, HEAD)
            hs = pl.ds(off, HEAD)
            lb = _lower_bound_slice(lbl_ref, hs)
            qr = qr_ref[:, hs]
            q, sq, f, sf, _, k = _gates(qr, fr_ref[:, hs], lb, valid)
            v = ir_ref[:, hs]
            do_h = do_ref[:, hs]
            b = t_ref[0:CHUNK, hs]
            b_last = t_ref[CHUNK - 1:CHUNK, hs]
            s0 = s0_ref[0, hs, :]
            ds1 = ds_ref[hs, :]
            eb = jnp.exp(b)
            ekl = jnp.exp(b_last - b)
            do_bf = do_h.astype(BF16)
            v_bf = v.astype(BF16)
            ds1_bf = ds1.astype(BF16)

            da = lax.dot_general(do_bf, v_bf, NT, preferred_element_type=F32)
            da_diag = jnp.sum(do_h * v, axis=-1, keepdims=True)
            a = jnp.zeros((CHUNK, CHUNK), F32)
            dq_x = eb * lax.dot_general(do_bf, s0.astype(BF16), NT, preferred_element_type=F32)
            dk_x = ekl * lax.dot_general(v_bf, ds1_bf, NT, preferred_element_type=F32)
            x_after = q * dq_x
            x_before = k * dk_x
            for lvl in range(1, N_LEVELS + 1):
                e = _level_factor(b, _level_reference(lvl, b, t_ref, hs))
                qt = (q * e).astype(BF16)
                kt = (k * e).astype(BF16)
                p = lax.dot_general(qt, kt, NT, preferred_element_type=F32)
                a = a + jnp.where(masks[lvl - 1], p, 0.0)
                dam = jnp.where(masks[lvl - 1], da, 0.0).astype(BF16)
                dqt = jnp.dot(dam, kt, preferred_element_type=F32)
                dkt = lax.dot_general(dam, qt, TN, preferred_element_type=F32)
                dq_x = dq_x + e * dqt
                dk_x = dk_x + e * dkt
                x_after = x_after + (qt.astype(F32) * dqt - kt.astype(F32) * dkt)

            dv = (lax.dot_general(a.astype(BF16), do_bf, TN, preferred_element_type=F32)
                  + jnp.sum(q * k, axis=-1, keepdims=True) * do_h
                  + jnp.dot((k * ekl).astype(BF16), ds1_bf, preferred_element_type=F32))
            dp_ref[:, pl.ds(2 * d + off, HEAD)] = dv.astype(BF16)

            carried = jnp.exp(b_last) * _col_to_row(jnp.sum(s0 * ds1, axis=-1, keepdims=True))
            dg = _dot3(ut, jnp.concatenate([x_after, x_before], axis=0)) + carried
            dq = dq_x + da_diag * k
            dk = dk_x + da_diag * q
            dp_ref[:, hs] = (dq * _dsilu(qr, sq)).astype(BF16)
            df = jnp.where(valid, dg / f - dk, 0.0)
            dp_ref[:, pl.ds(d + off, HEAD)] = (df * (1.0 - lb) * sf * (1.0 - sf)).astype(BF16)
            dlb_ref[:, hs] += jnp.sum(df * (1.0 - sf), axis=0, keepdims=True)

            ds_ref[hs, :] = (_row_to_col(jnp.exp(b_last)) * ds1
                             + lax.dot_general((q * eb).astype(BF16), do_bf, TN, preferred_element_type=F32))
        per_trip = min(HEADS_PER_TRIP, n_heads)

        def head_group(p, carry):
            for u in range(per_trip):
                head(p * per_trip + u)
            return carry

        if n_heads == per_trip:
            head_group(0, 0)
        else:
            lax.fori_loop(0, n_heads // per_trip, head_group, 0)

        @pl.when(n == nc - 1)
        def _():
            dl0 = dlb_ref[...] * pp
            dlbl_ref[0:1, :] = dl0
            dlbl_ref[1:2, :] = -dl0

    piece = lambda p: pl.BlockSpec((CHUNK, d), lambda n, p=p: (nc - 1 - n, p))
    return pl.pallas_call(
        body, grid=(nc,),
        in_specs=[piece(3), piece(4), piece(5), piece(0),
                  pl.BlockSpec((1, d, HEAD), lambda n: (nc - 1 - n, 0, 0)),
                  pl.BlockSpec((2, d), lambda n: (0, 0)),
                  pl.BlockSpec((n_tab, CHUNK), lambda n: (0, 0)),
                  pl.BlockSpec((CHUNK, 2 * CHUNK), lambda n: (0, 0)), ANY],
        out_specs=[pl.BlockSpec((CHUNK, 3 * d), lambda n: (nc - 1 - n, 1)), pl.BlockSpec((2, d), lambda n: (0, 0))],
        out_shape=[jax.ShapeDtypeStruct(dproj.shape, BF16), jax.ShapeDtypeStruct((2, d), F32)],
        input_output_aliases={8: 0},
        scratch_shapes=[pltpu.VMEM((d, HEAD), F32), pltpu.VMEM((n_tab, d), F32), pltpu.VMEM((1, d), F32)],
        name="hgrn_bwd", compiler_params=_params())(proj, proj, proj, do, s_all, lb_logits, tab, utri, dproj)


def _conv_bwd(dc, proj, conv_w, dz, dproj):
    lp, d = dc.shape
    tm = _row_tile(lp)
    hb = tm // HALO
    n_tiles = lp // tm
    last_halo = lp // HALO - 1

    def body(dc_ref, dcn_ref, ua_ref, ub_ref, uap_ref, ubp_ref, cw_ref, dz_ref, _,
             dp_ref, dcw_ref, dcb_ref, aext_ref, dcext_ref, da_ref, dcw_acc):
        i = pl.program_id(0)

        @pl.when(i == 0)
        def _():
            dcw_acc[...] = jnp.zeros_like(dcw_acc)
            dcb_ref[...] = jnp.zeros_like(dcb_ref)

        ua = ua_ref[...]
        sb = _sigmoid(ub_ref[...])
        a_prev = uap_ref[...] * _sigmoid(ubp_ref[...])
        aext_ref[0:HALO, :] = jnp.where(i > 0, a_prev, 0.0)
        aext_ref[HALO:HALO + tm, :] = ua * sb
        dcext_ref[0:tm, :] = dc_ref[...]
        dcext_ref[tm:tm + HALO, :] = jnp.where(i < n_tiles - 1, dcn_ref[...], 0.0)
        dcb_ref[...] += jnp.sum(dc_ref[...], axis=0, keepdims=True)

        def row_block(r, carry):
            r0 = pl.multiple_of(r * CONV_ROWS, CONV_ROWS)
            n_rows = CONV_ROWS + HALO
            for cs in range(d // CONV_LANES):
                cl = slice(cs * CONV_LANES, (cs + 1) * CONV_LANES)
                dblk = dcext_ref[pl.ds(r0, n_rows), cl]
                ablk = aext_ref[pl.ds(r0, n_rows), cl]
                dcur = dblk[0:CONV_ROWS, :]
                acc = jnp.zeros((CONV_ROWS, CONV_LANES), F32)
                for b in range(8):
                    dsh = dblk if b == 0 else pltpu.roll(dblk, n_rows - b, axis=0)
                    ash = ablk if b == 0 else pltpu.roll(ablk, n_rows - b, axis=0)
                    for a in range(5):
                        j_da = CONV_WIDTH - 1 - (8 * a + b)
                        if 0 <= j_da < CONV_WIDTH:
                            acc = acc + cw_ref[j_da:j_da + 1, cl] * dsh[8 * a:8 * a + CONV_ROWS, :]
                        j_w = 8 * a + b - 2
                        if 0 <= j_w < CONV_WIDTH:
                            prod = dcur * ash[8 * a:8 * a + CONV_ROWS, :]
                            dcw_acc[j_w, :, cl] += prod.reshape(CONV_ROWS // 8, 8, CONV_LANES).sum(axis=0)
                da_ref[pl.ds(r0, CONV_ROWS), cl] = acc
            return carry

        lax.fori_loop(0, tm // CONV_ROWS, row_block, 0)

        da = da_ref[...]
        dp_ref[:, 0:d] = (da * sb).astype(BF16)
        dp_ref[:, d:2 * d] = (da * ua * sb * (1.0 - sb)).astype(BF16)
        dp_ref[:, 2 * d:3 * d] = dz_ref[...]

        @pl.when(i == n_tiles - 1)
        def _():
            dcw_ref[...] = jnp.sum(dcw_acc[...], axis=1)

    row = lambda p: pl.BlockSpec((tm, d), lambda i, p=p: (i, p))
    prev = lambda p: pl.BlockSpec((HALO, d), lambda i, p=p: (jnp.maximum(i * hb - 1, 0), p))
    nxt = pl.BlockSpec((HALO, d), lambda i: (jnp.minimum((i + 1) * hb, last_halo), 0))
    return pl.pallas_call(
        body, grid=(n_tiles,),
        in_specs=[row(0), nxt, row(0), row(1), prev(0), prev(1), pl.BlockSpec((HALO, d), lambda i: (0, 0)),
                  row(0), ANY],
        out_specs=[pl.BlockSpec((tm, 3 * d), lambda i: (i, 0)), pl.BlockSpec((HALO, d), lambda i: (0, 0)),
                   pl.BlockSpec((1, d), lambda i: (0, 0))],
        out_shape=[jax.ShapeDtypeStruct(dproj.shape, BF16),
                   jax.ShapeDtypeStruct((HALO, d), F32), jax.ShapeDtypeStruct((1, d), F32)],
        input_output_aliases={8: 0},
        scratch_shapes=[pltpu.VMEM((HALO + tm, d), F32), pltpu.VMEM((tm + HALO, d), F32), pltpu.VMEM((tm, d), F32),
                        pltpu.VMEM((HALO, 8, d), F32)],
        name="conv_bwd", compiler_params=_params())(dc, dc, proj, proj, proj, proj, conv_w, dz, dproj)


def _weight_grad(xs, dy, name, blocked):
    lp, dx = xs.shape
    n = dy.shape[1]
    tk = _mm_row_tile(lp)
    if blocked:
        ncol = n // N_CHIPS
        nt = W_IN_COL_TILES
        tn = ncol // nt
        grid = (N_CHIPS * nt, lp // tk)
        out_spec = pl.BlockSpec((1, dx, tn), lambda c, k: (c // nt, 0, c % nt))
        out_shape = jax.ShapeDtypeStruct((N_CHIPS, dx, ncol), F32)
    else:
        tn = n
        grid = (1, lp // tk)
        out_spec = pl.BlockSpec((dx, tn), lambda c, k: (0, c))
        out_shape = jax.ShapeDtypeStruct((dx, n), F32)

    def body(xs_ref, dy_ref, o_ref, *copy_ref):
        @pl.when(pl.program_id(1) == 0)
        def _():
            o_ref[...] = jnp.zeros_like(o_ref)

        p = lax.dot_general(xs_ref[...], dy_ref[...], TN, preferred_element_type=F32)
        if blocked:
            o_ref[0] += p

            @pl.when(pl.program_id(1) == lp // tk - 1)
            def _():
                copy_ref[0][0] = o_ref[0].astype(BF16)
        else:
            o_ref[...] += p

    if blocked:
        out_spec = [out_spec, out_spec]
        out_shape = [out_shape, jax.ShapeDtypeStruct(out_shape.shape, BF16)]
    return pl.pallas_call(
        body, grid=grid,
        in_specs=[pl.BlockSpec((tk, dx), lambda c, k: (k, 0)), pl.BlockSpec((tk, tn), lambda c, k: (k, c))],
        out_specs=out_spec, out_shape=out_shape,
        name=name, compiler_params=_params())(xs, dy)


def _in_proj_bwd(dproj, wtg, hres, norm_g, dout):
    lp, d = hres.shape
    _, ncol, _ = wtg.shape
    tm = _mm_row_tile(lp)
    nt = W_IN_COL_TILES
    tn = ncol // nt
    nk = N_CHIPS * nt

    def body(dp_ref, w_ref, x_ref, g_ref, dout_ref, dx_ref, dg_ref, acc_ref):
        i = pl.program_id(0)
        kk = pl.program_id(1)

        @pl.when(jnp.logical_and(i == 0, kk == 0))
        def _():
            dg_ref[...] = jnp.zeros_like(dg_ref)

        @pl.when(kk == 0)
        def _():
            acc_ref[...] = jnp.zeros_like(acc_ref)

        acc_ref[...] += jnp.dot(dp_ref[...], w_ref[0], preferred_element_type=F32)

        @pl.when(kk == nk - 1)
        def _():
            x = x_ref[...]
            r = lax.rsqrt(jnp.mean(x * x, axis=-1, keepdims=True) + EPS)
            xhat = x * r
            dh = acc_ref[...]
            dg_ref[...] += jnp.sum(dh * xhat, axis=0, keepdims=True)
            dxh = dh * g_ref[...]
            dx_ref[...] = dout_ref[...] + r * (dxh - xhat * jnp.mean(dxh * xhat, axis=-1, keepdims=True))

    return pl.pallas_call(
        body, grid=(lp // tm, nk),
        in_specs=[pl.BlockSpec((tm, tn), lambda i, k: (i, k)),
                  pl.BlockSpec((1, tn, d), lambda i, k: (k // nt, k % nt, 0)),
                  pl.BlockSpec((tm, d), lambda i, k: (i, 0)),
                  pl.BlockSpec((1, d), lambda i, k: (0, 0)),
                  pl.BlockSpec((tm, d), lambda i, k: (i, 0))],
        out_specs=[pl.BlockSpec((tm, d), lambda i, k: (i, 0)), pl.BlockSpec((1, d), lambda i, k: (0, 0))],
        out_shape=[jax.ShapeDtypeStruct((lp, d), F32), jax.ShapeDtypeStruct((1, d), F32)],
        scratch_shapes=[pltpu.VMEM((tm, d), F32)],
        name="in_proj_bwd", compiler_params=_params())(dproj, wtg, hres, norm_g, dout)


def _adamw_math(w, g, m, v):
    m = ADAM_B1 * m + (1.0 - ADAM_B1) * g
    v = ADAM_B2 * v + (1.0 - ADAM_B2) * (g * g)
    m_hat = m / (1.0 - ADAM_B1 ** ADAM_STEP)
    v_hat = v / (1.0 - ADAM_B2 ** ADAM_STEP)
    delta = -ADAM_LR * (m_hat / (jnp.sqrt(v_hat) + ADAM_EPS) + ADAM_WD * w)
    return delta, m, v


def _elementwise_rows(shape):
    r, c = shape
    for t in (256, 128, 64, 32, 16, 8):
        if r % t == 0 and r > t and t * c * 4 <= 1024 * 1024:
            return t
    return r


def _adamw(name, w, m, v, *g_parts):
    shape = w.shape
    tr = _elementwise_rows(shape)
    n_g = len(g_parts)

    def body(*refs):
        w_ref, m_ref, v_ref = refs[:3]
        g_refs = refs[3:3 + n_g]
        g_out, d_out, m_out, v_out = refs[3 + n_g:]
        g = g_refs[0][...]
        for gr in g_refs[1:]:
            g = g + gr[...]
        delta, m_new, v_new = _adamw_math(w_ref[...], g, m_ref[...], v_ref[...])
        g_out[...] = g
        d_out[...] = delta
        m_out[...] = m_new
        v_out[...] = v_new

    spec = pl.BlockSpec((tr, shape[1]), lambda i: (i, 0))
    return pl.pallas_call(
        body, grid=(shape[0] // tr,),
        in_specs=[spec] * (3 + n_g), out_specs=[spec] * 4,
        out_shape=[jax.ShapeDtypeStruct(shape, F32)] * 4,
        name=name, compiler_params=_params())(w, m, v, *g_parts)


def _chip_half_sum(name, g, recv, core):
    _, _, hr, cols = g.shape
    tr = _elementwise_rows((hr, cols))

    def body(core_ref, g_ref, r_ref, o_ref, ob_ref):
        s = g_ref[0, 0] + r_ref[0].astype(F32)
        o_ref[0] = s
        ob_ref[0] = s.astype(BF16)

    blk = pl.BlockSpec((1, tr, cols), lambda j, i, core_ref: (j, i, 0))
    grid_spec = pltpu.PrefetchScalarGridSpec(
        num_scalar_prefetch=1, grid=(N_CHIPS, hr // tr),
        in_specs=[pl.BlockSpec((1, 1, tr, cols), lambda j, i, core_ref: (j, core_ref[0], i, 0)), blk],
        out_specs=[blk, blk])
    return pl.pallas_call(
        body, grid_spec=grid_spec,
        out_shape=[jax.ShapeDtypeStruct((N_CHIPS, hr, cols), F32), jax.ShapeDtypeStruct((N_CHIPS, hr, cols), BF16)],
        name=name, compiler_params=_params())(core, g, recv)


def _block_half_total(name, chip_sums, recv, chip_core, after):
    _, hr, cols = chip_sums.shape
    tr = _elementwise_rows((hr, cols))

    def body(cc_ref, p_ref, r_ref, after_ref, o_ref):
        s = p_ref[0]
        for k in range(3):
            s = s + r_ref[k].astype(F32)
        o_ref[0] = s

    grid_spec = pltpu.PrefetchScalarGridSpec(
        num_scalar_prefetch=1, grid=(hr // tr,),
        in_specs=[pl.BlockSpec((1, tr, cols), lambda i, cc_ref: (cc_ref[0], i, 0)),
                  pl.BlockSpec((3, tr, cols), lambda i, cc_ref: (0, i, 0)), ANY],
        out_specs=pl.BlockSpec((1, tr, cols), lambda i, cc_ref: (cc_ref[1], i, 0)))
    return pl.pallas_call(
        body, grid_spec=grid_spec, out_shape=jax.ShapeDtypeStruct((2, hr, cols), F32),
        name=name, compiler_params=_params())(chip_core, chip_sums, recv, after)


def _place_shard(name, w, chip, dtype):
    r, c = w.shape
    tr = _elementwise_rows((r, c))

    def body(chip_ref, w_ref, o_ref):
        o_ref[0] = w_ref[...].astype(dtype)

    grid_spec = pltpu.PrefetchScalarGridSpec(
        num_scalar_prefetch=1, grid=(r // tr,),
        in_specs=[pl.BlockSpec((tr, c), lambda i, chip_ref: (i, 0))],
        out_specs=pl.BlockSpec((1, tr, c), lambda i, chip_ref: (chip_ref[0], i, 0)))
    return pl.pallas_call(
        body, grid_spec=grid_spec, out_shape=jax.ShapeDtypeStruct((N_CHIPS, r, c), dtype),
        name=name, compiler_params=_params())(chip, w)


def _sum_slots(name, slots, own, my_idx):
    k, r, c = slots.shape

    def body(idx_ref, s_ref, own_ref, o_ref):
        s = None
        for j in range(k):
            term = jnp.where(idx_ref[0] == j, own_ref[...], s_ref[j])
            s = term if s is None else s + term
        o_ref[...] = s

    grid_spec = pltpu.PrefetchScalarGridSpec(
        num_scalar_prefetch=1, grid=(1,),
        in_specs=[pl.BlockSpec((k, r, c), lambda i, idx_ref: (0, 0, 0)),
                  pl.BlockSpec((r, c), lambda i, idx_ref: (0, 0))],
        out_specs=pl.BlockSpec((r, c), lambda i, idx_ref: (0, 0)))
    return pl.pallas_call(body, grid_spec=grid_spec, out_shape=jax.ShapeDtypeStruct((r, c), F32), name=name,
                          compiler_params=_params())(my_idx, slots, own)


def _mesh_pos():
    return lax.axis_index("x"), lax.axis_index("y"), lax.axis_index("c")


def _other_chips(x, y):
    return [(1 - x, y), (x, 1 - y), (1 - x, 1 - y)]


def _gather_weights(bufs):
    n = len(bufs)
    half = [b.shape[1] // 2 for b in bufs]

    def body(*refs):
        gathered = refs[n:2 * n]
        ici_send, ici_recv, d2d_send, d2d_recv = refs[2 * n:]
        x, y, c = _mesh_pos()
        me = 2 * x + y
        chips = _other_chips(x, y)

        def part(a, block, core):
            return gathered[a].at[block, pl.ds(core * half[a], half[a])]

        def over_ici(a, k, block):
            px, py = chips[k]
            return pltpu.make_async_remote_copy(
                src_ref=part(a, block, c), dst_ref=part(a, block, c),
                send_sem=ici_send.at[a, k], recv_sem=ici_recv.at[a, k],
                device_id=(px, py, c), device_id_type=MESH)

        def over_d2d(a, k, core):
            px, py = chips[k]
            return pltpu.make_async_remote_copy(
                src_ref=part(a, 2 * px + py, core), dst_ref=part(a, 2 * px + py, core),
                send_sem=d2d_send.at[a, k], recv_sem=d2d_recv.at[a, k],
                device_id=(x, y, 1 - c), device_id_type=MESH)

        for a in range(n):
            for k in range(3):
                over_ici(a, k, me).start()
        for a in range(n):
            for k, (px, py) in enumerate(chips):
                over_ici(a, k, 2 * px + py).wait_recv()
                over_d2d(a, k, c).start()
        for a in range(n):
            for k in range(3):
                over_d2d(a, k, 1 - c).wait_recv()
        for a in range(n):
            for k in range(3):
                over_ici(a, k, me).wait_send()
                over_d2d(a, k, c).wait_send()

    return pl.pallas_call(
        body, in_specs=[ANY] * n, out_specs=[ANY] * n,
        out_shape=[jax.ShapeDtypeStruct(b.shape, b.dtype) for b in bufs],
        input_output_aliases={a: a for a in range(n)},
        scratch_shapes=[pltpu.SemaphoreType.DMA((n, 3))] * 4,
        name="gather_weights")(*bufs)


def _gather_in_proj(h, bufs, order):
    n = len(bufs)
    half = [b.shape[1] // 2 for b in bufs]
    lp, d = h.shape
    ncol = bufs[0].shape[2]
    tm = _mm_row_tile(lp)
    n_row = lp // tm

    def body(order_ref, h_ref, *refs):
        gathered = refs[n:2 * n]
        o_ref, wt_ref = refs[2 * n], refs[2 * n + 1]
        w_buf, ici_send, ici_recv, d2d_send, d2d_recv, w_sem = refs[2 * n + 2:]
        j = pl.program_id(0)
        i = pl.program_id(1)
        x, y, c = _mesh_pos()
        me = 2 * x + y
        chips = _other_chips(x, y)

        def part(a, block, core):
            return gathered[a].at[block, pl.ds(core * half[a], half[a])]

        def over_ici(a, k, block):
            px, py = chips[k]
            return pltpu.make_async_remote_copy(
                src_ref=part(a, block, c), dst_ref=part(a, block, c),
                send_sem=ici_send.at[a, k], recv_sem=ici_recv.at[a, k],
                device_id=(px, py, c), device_id_type=MESH)

        def over_d2d(a, k, core):
            px, py = chips[k]
            return pltpu.make_async_remote_copy(
                src_ref=part(a, 2 * px + py, core), dst_ref=part(a, 2 * px + py, core),
                send_sem=d2d_send.at[a, k], recv_sem=d2d_recv.at[a, k],
                device_id=(x, y, 1 - c), device_id_type=MESH)

        @pl.when(jnp.logical_and(j == 0, i == 0))
        def _():
            for a in range(n):
                for k in range(2):
                    over_ici(a, k, me).start()

        for k, (px, py) in enumerate(chips):
            @pl.when(jnp.logical_and(j == k + 1, i == 0))
            def _(k=k, px=px, py=py):
                for a in range(n):
                    over_ici(a, k, 2 * px + py).wait_recv()
                    over_d2d(a, k, c).start()
                if k == 0:
                    for a in range(n):
                        over_ici(a, 2, me).start()
                for a in range(n):
                    over_d2d(a, k, 1 - c).wait_recv()

        @pl.when(i == 0)
        def _():
            load = pltpu.make_async_copy(gathered[0].at[order_ref[j]], w_buf, w_sem)
            load.start()
            load.wait()
            wt_ref[0] = w_buf[...].T

        o_ref[...] = jnp.dot(h_ref[...], w_buf[...], preferred_element_type=F32)

        @pl.when(jnp.logical_and(j == N_CHIPS - 1, i == n_row - 1))
        def _():
            for a in range(n):
                for k in range(3):
                    over_ici(a, k, me).wait_send()
                    over_d2d(a, k, c).wait_send()

    grid_spec = pltpu.PrefetchScalarGridSpec(
        num_scalar_prefetch=1, grid=(N_CHIPS, n_row),
        in_specs=[pl.BlockSpec((tm, d), lambda j, i, order_ref: (i, 0))] + [ANY] * n,
        out_specs=[ANY] * n + [pl.BlockSpec((tm, ncol), lambda j, i, order_ref: (i, order_ref[j])),
                               pl.BlockSpec((1, ncol, d), lambda j, i, order_ref: (order_ref[j], 0, 0))],
        scratch_shapes=[pltpu.VMEM((d, ncol), BF16)] + [pltpu.SemaphoreType.DMA((n, 3))] * 4
        + [pltpu.SemaphoreType.DMA])
    out = pl.pallas_call(
        body, grid_spec=grid_spec,
        out_shape=[jax.ShapeDtypeStruct(b.shape, b.dtype) for b in bufs]
        + [jax.ShapeDtypeStruct((lp, N_CHIPS * ncol), F32), jax.ShapeDtypeStruct((N_CHIPS, ncol, d), BF16)],
        input_output_aliases={2 + a: a for a in range(n)},
        name="gather_in_proj", compiler_params=_params())(order, h, *bufs)
    return out[n], out[n + 1], out[:n]


def _send_other_halves(grads, tag):
    n = len(grads)

    def body(*refs):
        srcs = refs[:n]
        dsts = refs[n:2 * n]
        send_sems, recv_sems = refs[2 * n:]
        x, y, c = _mesh_pos()
        copies = [pltpu.make_async_remote_copy(
            src_ref=srcs[a].at[j, 1 - c], dst_ref=dsts[a].at[j], send_sem=send_sems.at[a, j],
            recv_sem=recv_sems.at[a, j], device_id=(x, y, 1 - c), device_id_type=MESH)
            for a in range(n) for j in range(N_CHIPS)]
        for cp in copies:
            cp.start()
        for cp in copies:
            cp.wait()

    return pl.pallas_call(
        body, in_specs=[ANY] * n, out_specs=[ANY] * n,
        out_shape=[jax.ShapeDtypeStruct((N_CHIPS,) + g.shape[2:], g.dtype) for g in grads],
        scratch_shapes=[pltpu.SemaphoreType.DMA((n, N_CHIPS))] * 2,
        name="send_other_halves_" + tag)(*grads)


HBM = pl.BlockSpec(memory_space=pltpu.HBM)
SEM = pl.BlockSpec(memory_space=pltpu.SEMAPHORE)


def _block_copies(n, srcs, dsts, send_sems, recv_sems):
    x, y, c = _mesh_pos()
    return [pltpu.make_async_remote_copy(
        src_ref=srcs[a].at[2 * px + py], dst_ref=dsts[a].at[k], send_sem=send_sems.at[3 * a + k],
        recv_sem=recv_sems.at[3 * a + k], device_id=(px, py, c), device_id_type=MESH)
        for a in range(n) for k, (px, py) in enumerate(_other_chips(x, y))]


def _exchange_start(blocked, tag):
    n = len(blocked)
    lands = [lax.empty((3,) + b.shape[1:], b.dtype) for b in blocked]
    bufs = [pltpu.with_memory_space_constraint(b, pltpu.HBM) for b in list(blocked) + lands]
    nb = 2 * n

    def body(*refs):
        for cp in _block_copies(n, refs[:n], refs[n:nb], refs[nb], refs[nb + 1]):
            cp.start()
        refs[-1][...] = jnp.zeros_like(refs[-1])

    out = pl.pallas_call(
        body, name="exchange_start_" + tag,
        in_specs=[HBM] * nb,
        out_shape=[pltpu.SemaphoreType.DMA((3 * n,)), pltpu.SemaphoreType.DMA((3 * n,))]
        + [pltpu.HBM(b.shape, b.dtype) for b in bufs] + [jax.ShapeDtypeStruct((8, 128), F32)],
        out_specs=[SEM] * 2 + [HBM] * nb + [pl.BlockSpec(memory_space=pltpu.VMEM)],
        input_output_aliases={i: 2 + i for i in range(nb)},
        compiler_params=pltpu.CompilerParams(has_side_effects=pltpu.SideEffectType.DATAFLOW_SIDE_EFFECTING),
    )(*bufs)
    return (out[:2], out[2:2 + nb]), out[-1]


def _exchange_wait(state, after, tag):
    sems, bufs = state
    nb = len(bufs)
    n = nb // 2

    def body(*refs):
        for cp in _block_copies(n, refs[:n], refs[n:nb], refs[nb], refs[nb + 1]):
            cp.wait_send()
            cp.wait_recv()

    out = pl.pallas_call(
        body, name="exchange_wait_" + tag,
        in_specs=[HBM] * nb + [SEM] * 2 + [ANY],
        out_shape=[pltpu.HBM(b.shape, b.dtype) for b in bufs],
        out_specs=[HBM] * nb,
        input_output_aliases={i: i for i in range(nb)},
        compiler_params=pltpu.CompilerParams(has_side_effects=pltpu.SideEffectType.DATAFLOW_SIDE_EFFECTING),
    )(*bufs, *sems, after)
    return out[n:nb]


def _whole_block_copies(buf, send_sems, recv_sems, incoming):
    x, y, c = _mesh_pos()
    me = 2 * x + y
    out = []
    for k, (px, py) in enumerate(_other_chips(x, y)):
        block = 2 * px + py if incoming else me
        out.append(pltpu.make_async_remote_copy(
            src_ref=buf.at[block], dst_ref=buf.at[block], send_sem=send_sems.at[k], recv_sem=recv_sems.at[k],
            device_id=(px, py, c), device_id_type=MESH))
    return out


def _gather_start(buf, after, tag):
    buf = pltpu.with_memory_space_constraint(buf, pltpu.HBM)

    def body(buf_ref, after_ref, send_sems, recv_sems, thru_ref, token):
        for cp in _whole_block_copies(buf_ref, send_sems, recv_sems, incoming=False):
            cp.start()
        token[...] = jnp.zeros_like(token)

    out = pl.pallas_call(
        body, name="gather_start_" + tag,
        in_specs=[HBM, ANY],
        out_shape=[pltpu.SemaphoreType.DMA((3,)), pltpu.SemaphoreType.DMA((3,)), pltpu.HBM(buf.shape, buf.dtype),
                   jax.ShapeDtypeStruct((8, 128), F32)],
        out_specs=[SEM, SEM, HBM, pl.BlockSpec(memory_space=pltpu.VMEM)],
        input_output_aliases={0: 2},
        compiler_params=pltpu.CompilerParams(has_side_effects=pltpu.SideEffectType.DATAFLOW_SIDE_EFFECTING),
    )(buf, after)
    return out[:3], out[3]


def _gather_wait(state, after, tag):
    send_sems, recv_sems, buf = state

    def body(buf_ref, send_ref, recv_ref, after_ref, out_ref):
        for cp in _whole_block_copies(buf_ref, send_ref, recv_ref, incoming=True):
            cp.wait_send()
            cp.wait_recv()

    return pl.pallas_call(
        body, name="gather_wait_" + tag,
        in_specs=[HBM, SEM, SEM, ANY],
        out_shape=pltpu.HBM(buf.shape, buf.dtype), out_specs=HBM,
        input_output_aliases={0: 0},
        compiler_params=pltpu.CompilerParams(has_side_effects=pltpu.SideEffectType.DATAFLOW_SIDE_EFFECTING),
    )(buf, send_sems, recv_sems, after)


def _small_copies(small_ref, slots_ref, send_sems, recv_sems, incoming):
    x, y, c = _mesh_pos()
    out = []
    for r in range(1, 8):
        px = 1 - x if r & 4 else x
        py = 1 - y if r & 2 else y
        pc = 1 - c if r & 1 else c
        slot = 4 * px + 2 * py + pc if incoming else 4 * x + 2 * y + c
        out.append(pltpu.make_async_remote_copy(
            src_ref=small_ref, dst_ref=slots_ref.at[slot], send_sem=send_sems.at[r - 1],
            recv_sem=recv_sems.at[r - 1], device_id=(px, py, pc), device_id_type=MESH))
    return out


def _small_start(small):
    bufs = [pltpu.with_memory_space_constraint(b, pltpu.HBM)
            for b in (small, lax.empty((8,) + small.shape, small.dtype))]

    def body(small_ref, slots_ref, send_sems, recv_sems, small_thru, slots_thru, token):
        for cp in _small_copies(small_ref, slots_ref, send_sems, recv_sems, incoming=False):
            cp.start()
        token[...] = jnp.zeros_like(token)

    out = pl.pallas_call(
        body, name="small_start",
        in_specs=[HBM, HBM],
        out_shape=[pltpu.SemaphoreType.DMA((7,)), pltpu.SemaphoreType.DMA((7,))]
        + [pltpu.HBM(b.shape, b.dtype) for b in bufs] + [jax.ShapeDtypeStruct((8, 128), F32)],
        out_specs=[SEM, SEM, HBM, HBM, pl.BlockSpec(memory_space=pltpu.VMEM)],
        input_output_aliases={0: 2, 1: 3},
        compiler_params=pltpu.CompilerParams(has_side_effects=pltpu.SideEffectType.DATAFLOW_SIDE_EFFECTING),
    )(*bufs)
    return out[:4], out[4]


def _small_wait(state, after):
    send_sems, recv_sems, small, slots = state

    def body(small_ref, slots_ref, send_ref, recv_ref, after_ref, small_out, slots_out):
        for cp in _small_copies(small_ref, slots_ref, send_ref, recv_ref, incoming=True):
            cp.wait_send()
            cp.wait_recv()

    return pl.pallas_call(
        body, name="small_wait",
        in_specs=[HBM, HBM, SEM, SEM, ANY],
        out_shape=[pltpu.HBM(small.shape, small.dtype), pltpu.HBM(slots.shape, slots.dtype)],
        out_specs=[HBM, HBM], input_output_aliases={0: 0, 1: 1},
        compiler_params=pltpu.CompilerParams(has_side_effects=pltpu.SideEffectType.DATAFLOW_SIDE_EFFECTING),
    )(small, slots, send_sems, recv_sems, after)[1]


def _join_halves(bufs):
    n = len(bufs)

    def body(*refs):
        joined = refs[n:2 * n]
        send_sems, recv_sems = refs[2 * n:]
        x, y, c = _mesh_pos()
        for a in range(n):
            pltpu.make_async_remote_copy(
                src_ref=joined[a].at[c], dst_ref=joined[a].at[c], send_sem=send_sems.at[a],
                recv_sem=recv_sems.at[a], device_id=(x, y, 1 - c), device_id_type=MESH).start()
        for a in range(n):
            pltpu.make_async_remote_copy(
                src_ref=joined[a].at[c], dst_ref=joined[a].at[1 - c], send_sem=send_sems.at[a],
                recv_sem=recv_sems.at[a], device_id=(x, y, 1 - c), device_id_type=MESH).wait()

    return pl.pallas_call(
        body, in_specs=[ANY] * n, out_specs=[ANY] * n,
        out_shape=[jax.ShapeDtypeStruct(b.shape, b.dtype) for b in bufs],
        input_output_aliases={a: a for a in range(n)},
        scratch_shapes=[pltpu.SemaphoreType.DMA((n,))] * 2,
        name="join_halves")(*bufs)


def kernel(x, meta_tokens, norm_g, w_in, conv_w, conv_b, ln_g, ln_b, w_conv_out, lb_logits, gnorm_g, w_rec_out, w_out, final_g, loss_target, m_meta_tokens, m_norm_g, m_w_in, m_conv_w, m_conv_b, m_ln_g, m_ln_b, m_w_conv_out, m_lb_logits, m_gnorm_g, m_w_rec_out, m_w_out, m_final_g, v_meta_tokens, v_norm_g, v_w_in, v_conv_w, v_conv_b, v_ln_g, v_ln_b, v_w_conv_out, v_lb_logits, v_gnorm_g, v_w_rec_out, v_w_out, v_final_g):
    seq, d = x.shape[1], x.shape[2]
    n_meta = meta_tokens.shape[0]
    n_pad = CHUNK - n_meta
    ds = d // N_CHIPS
    chip = 2 * lax.axis_index("x") + lax.axis_index("y")

    conv_w_pad = jnp.pad(conv_w[0], ((0, HALO - CONV_WIDTH), (0, 0)))
    chip_idx = chip.astype(jnp.int32).reshape(1)
    (small_g,) = _gather_weights([
        _place_shard("place_small", jnp.concatenate([conv_w_pad, meta_tokens], axis=0), chip_idx, F32)])
    cw_full = jnp.transpose(small_g[:, 0:HALO], (1, 0, 2)).reshape(HALO, d)
    meta_full = jnp.transpose(small_g[:, HALO:HALO + n_meta], (1, 0, 2)).reshape(n_meta, d)

    hres = jnp.concatenate([jnp.zeros((n_pad, d), F32), meta_full, x[0]], axis=0)
    target = loss_target[0]
    final_g2 = final_g.reshape(1, d)
    h = _rmsnorm_fwd(hres, norm_g)
    fx, fy = 1 - lax.axis_index("x"), 1 - lax.axis_index("y")
    order = jnp.stack([chip, 2 * fx + (1 - fy), 2 * (1 - fx) + fy, 2 * fx + fy]).astype(jnp.int32)
    proj, win_t, _ = _gather_in_proj(h, [_place_shard("place_w_in", w_in[0], chip_idx, BF16)], order)
    sq_own = _place_shard("place_square", jnp.concatenate([w_conv_out[0], w_rec_out[0], w_out[0]], axis=0),
                          chip_idx, BF16)
    sq_flight, sq_token = _gather_start(sq_own, proj, "square")
    o, s_all = _hgrn_fwd(proj, lb_logits + sq_token[0:1, 0:1], n_pad)
    sq_g = _gather_wait(sq_flight, s_all, "square")
    wc_full = sq_g[:, 0:ds].reshape(d, d)
    wr_full = sq_g[:, ds:2 * ds].reshape(d, d)
    wo_full = sq_g[:, 2 * ds:3 * ds].reshape(d, d)
    c, yc_in, y_conv = _conv_fwd(proj, cw_full, conv_b, ln_g, ln_b, wc_full)
    yr_in, merged, y_rec, dout, loss_acc, dfinal_g = _tail_fwd(
        o, proj, y_conv, hres, target, gnorm_g, final_g2, wr_full, wo_full)

    (dyc, dyr, dout_bf, dz, dproj, do, dc, dgnorm_g, dln_g, dln_b) = _tail_bwd(
        dout, proj, y_conv, y_rec, o, c, wo_full, wr_full, wc_full, ln_g, ln_b, gnorm_g)
    g_wc = _weight_grad(yc_in, dyc, "grad_w_conv_out", False)
    g_wr = _weight_grad(yr_in, dyr, "grad_w_rec_out", False)
    g_wo = _weight_grad(merged, dout_bf, "grad_w_out", False)

    core = lax.axis_index("c").astype(jnp.int32).reshape(1)

    def chip_sum_and_start(g, tag, g_to_sibling=None):
        halves = lambda a: a.reshape(N_CHIPS, 2, a.shape[1] // 2, a.shape[2])
        g = halves(g)
        (from_sibling,) = _send_other_halves([g if g_to_sibling is None else halves(g_to_sibling)], tag)
        sums = _chip_half_sum("chip_half_sum_" + tag, g, from_sibling, core)
        in_flight, token = _exchange_start([sums[1]], tag)
        return sums[0], in_flight, token[0:1, 0:1]

    g_sq = jnp.concatenate([g.reshape(N_CHIPS, ds, d) for g in (g_wc, g_wr, g_wo)], axis=1)
    sum_sq, flight_sq, token_sq = chip_sum_and_start(g_sq, "square")
    dproj, dlb_logits = _hgrn_bwd(proj, do, s_all, lb_logits + token_sq, n_pad, dproj)
    dproj, dconv_w, dconv_b = _conv_bwd(dc, proj, cw_full, dz, dproj)
    (recv_sq,) = _exchange_wait(flight_sq, dconv_b, "square")
    g_win, g_win_bf = _weight_grad(h, dproj, "grad_w_in", True)
    sum_win, flight_win, token_win = chip_sum_and_start(g_win, "w_in", g_win_bf)
    dhres, dnorm_g = _in_proj_bwd(dproj, win_t, hres, norm_g + token_win, dout)
    grad_x = dhres[CHUNK:][None]
    (recv_win,) = _exchange_wait(flight_win, dnorm_g, "w_in")
    small = jnp.concatenate([dnorm_g, dconv_b, dln_g, dln_b, dlb_logits, dgnorm_g, dfinal_g,
                             dhres[n_pad:CHUNK], dconv_w[:CONV_WIDTH],
                             jnp.broadcast_to(loss_acc[0:1, 0:1], (1, d))], axis=0)
    small_flight, small_token = _small_start(small)
    chip_core = jnp.concatenate([chip_idx, core])
    totals = [_block_half_total("block_half_total_" + nm, s, r, chip_core, small_token)
              for nm, s, r in zip(("w_in", "square"), (sum_win, sum_sq), (recv_win, recv_sq))]
    joined = _join_halves(totals)
    gt_win, gt_sq = [t.reshape(2 * t.shape[1], t.shape[2]) for t in joined]
    small_slots = _small_wait(small_flight, joined[1])
    device_idx = (2 * chip_idx + core).astype(jnp.int32)
    small_sum = _sum_slots("sum_small", small_slots, small, device_idx)

    res = {}
    res["w_in"] = _adamw("adamw_w_in", w_in[0], m_w_in[0], v_w_in[0], gt_win)
    res["w_conv_out"] = _adamw("adamw_w_conv_out", w_conv_out[0], m_w_conv_out[0], v_w_conv_out[0], gt_sq[0:ds])
    res["w_rec_out"] = _adamw("adamw_w_rec_out", w_rec_out[0], m_w_rec_out[0], v_w_rec_out[0], gt_sq[ds:2 * ds])
    res["w_out"] = _adamw("adamw_w_out", w_out[0], m_w_out[0], v_w_out[0], gt_sq[2 * ds:3 * ds])
    big = {k: tuple(a[None] for a in v) for k, v in res.items()}

    rep_names = ("norm_g", "conv_b", "ln_g", "ln_b", "lb_logits", "gnorm_g", "final_g")
    rep_w = (norm_g, conv_b, ln_g, ln_b, lb_logits, gnorm_g, final_g2)
    rep_m = (m_norm_g, m_conv_b, m_ln_g, m_ln_b, m_lb_logits, m_gnorm_g, m_final_g.reshape(1, d))
    rep_v = (v_norm_g, v_conv_b, v_ln_g, v_ln_b, v_lb_logits, v_gnorm_g, v_final_g.reshape(1, d))
    rep = _adamw("adamw_replicated", jnp.concatenate(rep_w, 0), jnp.concatenate(rep_m, 0),
                 jnp.concatenate(rep_v, 0), small_sum[0:8])
    rep_rows = {"norm_g": (0, 1), "conv_b": (1, 2), "ln_g": (2, 3), "ln_b": (3, 4), "lb_logits": (4, 6),
                "gnorm_g": (6, 7), "final_g": (7, 8)}
    small_out = {}
    for nm in rep_names:
        lo, hi = rep_rows[nm]
        vals = tuple(a[lo:hi] for a in rep)
        if nm == "final_g":
            vals = tuple(a.reshape(d) for a in vals)
        small_out[nm] = vals
    cw_row = 8 + n_meta
    g_meta = lax.dynamic_slice_in_dim(small_sum[8:cw_row], chip * ds, ds, axis=1)
    small_out["meta_tokens"] = _adamw("adamw_meta", meta_tokens, m_meta_tokens, v_meta_tokens, g_meta)
    g_cw = lax.dynamic_slice_in_dim(small_sum[cw_row:cw_row + HALO], chip * ds, ds, axis=1)
    pad_rows = ((0, HALO - CONV_WIDTH), (0, 0))
    cw_res = _adamw("adamw_conv_w", conv_w_pad, jnp.pad(m_conv_w[0], pad_rows),
                    jnp.pad(v_conv_w[0], pad_rows, constant_values=1.0), g_cw)
    small_out["conv_w"] = tuple(a[:CONV_WIDTH][None] for a in cw_res)

    loss = small_sum[cw_row + HALO - 1, 0]

    order = ("meta_tokens", "norm_g", "w_in", "conv_w", "conv_b", "ln_g", "ln_b", "w_conv_out", "lb_logits",
             "gnorm_g", "w_rec_out", "w_out", "final_g")
    allres = {**big, **small_out}
    outs = [loss, grad_x]
    for field in range(4):
        outs.extend(allres[nm][field] for nm in order)
    return tuple(outs)
```

```python
import numpy as np

import jax
import jax.numpy as jnp
from jax import lax
from jax.experimental import pallas as pl
from jax.experimental.pallas import tpu as pltpu

F32 = jnp.float32
BF16 = jnp.bfloat16

EPS = 1e-6
CHUNK = 64
N_LEVELS = 6
FIRST_TABLE_LEVEL = 5
CONV_WIDTH = 31
HALO = 32
CONV_ROWS = 32
CONV_LANES = 256
SUBLANES = 8
HEAD = 128
W_IN_COL_TILES = 1
HEADS_PER_TRIP = 8
N_CHIPS = 4
VMEM_LIMIT_BYTES = 56 * 1024 * 1024
ELEMENTWISE_BLOCK_BYTES = 2 * 1024 * 1024

ADAM_LR = 0.001
ADAM_B1 = 0.9
ADAM_B2 = 0.999
ADAM_EPS = 1e-08
ADAM_WD = 0.01
ADAM_STEP = 10

MESH = pl.DeviceIdType.MESH
ANY = pl.BlockSpec(memory_space=pl.ANY)

NT = (((1,), (1,)), ((), ()))
TN = (((0,), (0,)), ((), ()))


def _params(**kw):
    return pltpu.CompilerParams(vmem_limit_bytes=VMEM_LIMIT_BYTES, **kw)


def _sigmoid(x):
    return jax.nn.sigmoid(x)


def _dsilu(x, s):
    return s * (1.0 + x * (1.0 - s))


def _row_tile(lp):
    for t in (320, 256, 192, 128, 64):
        if lp % t == 0:
            return t
    raise ValueError(f"unsupported padded length {lp}")


def _mm_row_tile(lp):
    for t in (832, 640, 320, 256, 192, 128, 64):
        if lp % t == 0:
            return t
    raise ValueError(f"unsupported padded length {lp}")


def _dot3(m_bf16, x):
    hi = x.astype(BF16)
    r1 = x - hi.astype(F32)
    mid = r1.astype(BF16)
    lo = (r1 - mid.astype(F32)).astype(BF16)
    return (jnp.dot(m_bf16, hi, preferred_element_type=F32)
            + jnp.dot(m_bf16, mid, preferred_element_type=F32)
            + jnp.dot(m_bf16, lo, preferred_element_type=F32))


def _dot2(m_bf16, x):
    hi = x.astype(BF16)
    lo = (x - hi.astype(F32)).astype(BF16)
    return (jnp.dot(m_bf16, hi, preferred_element_type=F32)
            + jnp.dot(m_bf16, lo, preferred_element_type=F32))


def _col_to_row(col):
    return jnp.broadcast_to(col, (HEAD, SUBLANES)).T[0:1, :]


def _row_to_col(row):
    return jnp.broadcast_to(row, (SUBLANES, HEAD)).T[:, 0:1]


def _hgrn_tables():
    t = np.arange(CHUNK)
    ltri = (t[None, :] <= t[:, None]).astype(np.float32)
    mats = [ltri]
    for lvl in range(FIRST_TABLE_LEVEL, N_LEVELS + 1):
        blk = CHUNK >> (lvl - 1)
        mid = (t // blk) * blk + blk // 2
        mats.append(ltri[mid - 1])
    after = (t[None, :] >= t[:, None]).astype(np.float32)
    before = (t[None, :] < t[:, None]).astype(np.float32)
    return jnp.asarray(np.concatenate(mats, 0), BF16), jnp.asarray(np.concatenate([after, before], 1), BF16)


def _rmsnorm_fwd(hres, g):
    lp, d = hres.shape
    tm = _row_tile(lp)

    def body(x_ref, g_ref, h_ref):
        x = x_ref[...]
        r = lax.rsqrt(jnp.mean(x * x, axis=-1, keepdims=True) + EPS)
        h_ref[...] = (x * r * g_ref[...]).astype(BF16)

    return pl.pallas_call(
        body, grid=(lp // tm,),
        in_specs=[pl.BlockSpec((tm, d), lambda i: (i, 0)), pl.BlockSpec((1, d), lambda i: (0, 0))],
        out_specs=pl.BlockSpec((tm, d), lambda i: (i, 0)),
        out_shape=jax.ShapeDtypeStruct((lp, d), BF16),
        name="rmsnorm_fwd", compiler_params=_params())(hres, g)


def _conv_fwd(proj, conv_w, conv_b, ln_g, ln_b, w_conv):
    lp = proj.shape[0]
    d = conv_b.shape[1]
    tm = _row_tile(lp)
    hb = tm // HALO

    def body(ua_ref, ub_ref, z_ref, uap_ref, ubp_ref, cw_ref, cb_ref, lg_ref, lb_ref, w_ref,
             c_ref, ycin_ref, yconv_ref, aext_ref):
        i = pl.program_id(0)
        a_prev = uap_ref[...] * _sigmoid(ubp_ref[...])
        aext_ref[0:HALO, :] = jnp.where(i > 0, a_prev, 0.0)
        aext_ref[HALO:HALO + tm, :] = ua_ref[...] * _sigmoid(ub_ref[...])

        def row_block(r, carry):
            r0 = pl.multiple_of(r * CONV_ROWS, CONV_ROWS)
            for cs in range(d // CONV_LANES):
                cl = slice(cs * CONV_LANES, (cs + 1) * CONV_LANES)
                blk = aext_ref[pl.ds(r0, CONV_ROWS + HALO), cl]
                acc = jnp.zeros((CONV_ROWS, CONV_LANES), F32) + cb_ref[:, cl]
                for b in range(SUBLANES):
                    sh = blk if b == 0 else pltpu.roll(blk, CONV_ROWS + HALO - b, axis=0)
                    for a in range(5):
                        j = SUBLANES * a + b - 2
                        if 0 <= j < CONV_WIDTH:
                            acc = acc + cw_ref[j:j + 1, cl] * sh[SUBLANES * a:SUBLANES * a + CONV_ROWS, :]
                c_ref[pl.ds(r0, CONV_ROWS), cl] = acc
            return carry

        lax.fori_loop(0, tm // CONV_ROWS, row_block, 0)

        c = c_ref[...]
        mu = jnp.mean(c, axis=-1, keepdims=True)
        xc = c - mu
        rstd = lax.rsqrt(jnp.mean(xc * xc, axis=-1, keepdims=True) + EPS)
        ln = xc * rstd * lg_ref[...] + lb_ref[...]
        s = ln * _sigmoid(ln)
        z = z_ref[...]
        ycin = (s * (z * _sigmoid(z))).astype(BF16)
        ycin_ref[...] = ycin
        yconv_ref[...] = jnp.dot(ycin, w_ref[...], preferred_element_type=F32)

    row = lambda p: pl.BlockSpec((tm, d), lambda i, p=p: (i, p))
    halo = lambda p: pl.BlockSpec((HALO, d), lambda i, p=p: (jnp.maximum(i * hb - 1, 0), p))
    vec = pl.BlockSpec((1, d), lambda i: (0, 0))
    return pl.pallas_call(
        body, grid=(lp // tm,),
        in_specs=[row(0), row(1), row(2), halo(0), halo(1),
                  pl.BlockSpec((HALO, d), lambda i: (0, 0)), vec, vec, vec,
                  pl.BlockSpec((d, d), lambda i: (0, 0))],
        out_specs=[pl.BlockSpec((tm, d), lambda i: (i, 0))] * 3,
        out_shape=[jax.ShapeDtypeStruct((lp, d), F32), jax.ShapeDtypeStruct((lp, d), BF16),
                   jax.ShapeDtypeStruct((lp, d), F32)],
        scratch_shapes=[pltpu.VMEM((HALO + tm, d), F32)],
        name="conv_fwd", compiler_params=_params())(
            proj, proj, proj, proj, proj, conv_w, conv_b, ln_g, ln_b, w_conv)


def _lower_bound(lbl_ref):
    l0 = lbl_ref[0:1, :]
    l1 = lbl_ref[1:2, :]
    m = jnp.maximum(l0, l1)
    e0 = jnp.exp(l0 - m)
    e1 = jnp.exp(l1 - m)
    p0 = e0 / (e0 + e1)
    return p0, p0 * (e1 / (e0 + e1))


def _level_masks():
    r2 = lax.broadcasted_iota(jnp.int32, (CHUNK, CHUNK), 0)
    c2 = lax.broadcasted_iota(jnp.int32, (CHUNK, CHUNK), 1)
    out = []
    for lvl in range(1, N_LEVELS + 1):
        blk = CHUNK >> (lvl - 1)
        sh = blk.bit_length() - 1
        same = (r2 >> sh) == (c2 >> sh)
        t_upper = (r2 & (blk - 1)) >= (blk // 2)
        s_lower = (c2 & (blk - 1)) < (blk // 2)
        out.append(jnp.logical_and(same, jnp.logical_and(t_upper, s_lower)))
    return out


def _gates(qr, fr, lb, valid):
    sq = _sigmoid(qr)
    q = qr * sq
    sf = _sigmoid(fr)
    f = lb + (1.0 - lb) * sf
    g = jnp.where(valid, jnp.log(f), 0.0)
    k = jnp.where(valid, 1.0 - f, 0.0)
    return q, sq, f, sf, g, k


def _level_reference(lvl, b, t_ref, hs):
    if lvl >= FIRST_TABLE_LEVEL:
        base = CHUNK * (lvl - FIRST_TABLE_LEVEL + 1)
        return t_ref[base:base + CHUNK, hs]
    blk = CHUNK >> (lvl - 1)
    rows = [jnp.broadcast_to(b[m + blk // 2 - 1:m + blk // 2, :], (blk, HEAD)) for m in range(0, CHUNK, blk)]
    return rows[0] if len(rows) == 1 else jnp.concatenate(rows, axis=0)


def _level_factor(b, r):
    d = b - r
    return jnp.exp(jnp.minimum(d, -d))


def _hgrn_fwd(proj, lb_logits, n_pad):
    lp = proj.shape[0]
    d = lb_logits.shape[1]
    n_heads = d // HEAD
    nc = lp // CHUNK
    tab, _ = _hgrn_tables()
    n_tab = tab.shape[0]

    def body(qr_ref, fr_ref, ir_ref, lbl_ref, tab_ref, o_ref, sall_ref, s_ref, t_ref):
        n = pl.program_id(0)

        @pl.when(n == 0)
        def _():
            s_ref[...] = jnp.zeros_like(s_ref)

        sall_ref[0] = s_ref[...]
        lb_all, _ = _lower_bound(lbl_ref)
        rid = lax.broadcasted_iota(jnp.int32, (CHUNK, 1), 0)
        valid = jnp.logical_or(n > 0, rid >= n_pad)
        f_all = lb_all + (1.0 - lb_all) * _sigmoid(fr_ref[...])
        t_ref[...] = _dot2(tab_ref[...], jnp.where(valid, jnp.log(f_all), 0.0))
        masks = _level_masks()

        def head(h):
            off = h * HEAD if isinstance(h, int) else pl.multiple_of(h * HEAD, HEAD)
            hs = pl.ds(off, HEAD)
            lb = _lower_bound_slice(lbl_ref, hs)
            q, _, _, _, _, k = _gates(qr_ref[:, hs], fr_ref[:, hs], lb, valid)
            v = ir_ref[:, hs]
            b = t_ref[0:CHUNK, hs]
            s0 = s_ref[hs, :]
            o = jnp.dot((q * jnp.exp(b)).astype(BF16), s0.astype(BF16), preferred_element_type=F32)
            o = o + jnp.sum(q * k, axis=-1, keepdims=True) * v
            a = jnp.zeros((CHUNK, CHUNK), F32)
            for lvl in range(1, N_LEVELS + 1):
                e = _level_factor(b, _level_reference(lvl, b, t_ref, hs))
                p = lax.dot_general((q * e).astype(BF16), (k * e).astype(BF16), NT, preferred_element_type=F32)
                a = a + jnp.where(masks[lvl - 1], p, 0.0)
            vb = v.astype(BF16)
            o_ref[:, hs] = o + jnp.dot(a.astype(BF16), vb, preferred_element_type=F32)
            b_last = t_ref[CHUNK - 1:CHUNK, hs]
            khat = (k * jnp.exp(b_last - b)).astype(BF16)
            s_ref[hs, :] = _row_to_col(jnp.exp(b_last)) * s0 + lax.dot_general(khat, vb, TN, preferred_element_type=F32)
        per_trip = min(HEADS_PER_TRIP, n_heads)

        def head_group(p, carry):
            for u in range(per_trip):
                head(p * per_trip + u)
            return carry

        if n_heads == per_trip:
            head_group(0, 0)
        else:
            lax.fori_loop(0, n_heads // per_trip, head_group, 0)

    piece = lambda p: pl.BlockSpec((CHUNK, d), lambda n, p=p: (n, p))
    return pl.pallas_call(
        body, grid=(nc,),
        in_specs=[piece(3), piece(4), piece(5), pl.BlockSpec((2, d), lambda n: (0, 0)),
                  pl.BlockSpec((n_tab, CHUNK), lambda n: (0, 0))],
        out_specs=[pl.BlockSpec((CHUNK, d), lambda n: (n, 0)), pl.BlockSpec((1, d, HEAD), lambda n: (n, 0, 0))],
        out_shape=[jax.ShapeDtypeStruct((lp, d), F32), jax.ShapeDtypeStruct((nc, d, HEAD), F32)],
        scratch_shapes=[pltpu.VMEM((d, HEAD), F32), pltpu.VMEM((n_tab, d), F32)],
        name="hgrn_fwd", compiler_params=_params())(proj, proj, proj, lb_logits, tab)


def _lower_bound_slice(lbl_ref, hs):
    l0 = lbl_ref[0:1, hs]
    l1 = lbl_ref[1:2, hs]
    m = jnp.maximum(l0, l1)
    e0 = jnp.exp(l0 - m)
    e1 = jnp.exp(l1 - m)
    return e0 / (e0 + e1)


def _tail_fwd(o, proj, y_conv, hres, target, gnorm_g, final_g, w_rec, w_out):
    lp, d = o.shape
    n_heads = d // HEAD
    tm = _row_tile(lp)

    n_slabs = tm // CHUNK

    def body(o_ref, gr_ref, mc_ref, mr_ref, yc_ref, x_ref, gn_ref, fg_ref, wr_ref, wo_ref, *rest):
        t_refs = rest[:n_slabs]
        yrin_ref, mg_ref, yrec_ref, dout_ref, loss_ref, dfg_ref = rest[n_slabs:]
        i = pl.program_id(0)

        @pl.when(i == 0)
        def _():
            loss_ref[...] = jnp.zeros_like(loss_ref)
            dfg_ref[...] = jnp.zeros_like(dfg_ref)

        for h in range(n_heads):
            hs = slice(h * HEAD, (h + 1) * HEAD)
            oh = o_ref[:, hs]
            on = oh * lax.rsqrt(jnp.mean(oh * oh, axis=-1, keepdims=True) + EPS) * gn_ref[:, hs]
            gr = gr_ref[:, hs]
            yrin_ref[:, hs] = (on * (gr * _sigmoid(gr))).astype(BF16)
        yrec = jnp.dot(yrin_ref[...], wr_ref[...], preferred_element_type=F32)
        yrec_ref[...] = yrec
        merged = (_sigmoid(mc_ref[...]) * yc_ref[...] + _sigmoid(mr_ref[...]) * yrec).astype(BF16)
        mg_ref[...] = merged
        out = x_ref[...] + jnp.dot(merged, wo_ref[...], preferred_element_type=F32)
        r = lax.rsqrt(jnp.mean(out * out, axis=-1, keepdims=True) + EPS)
        yhat = out * r
        fg = fg_ref[...]
        rid = lax.broadcasted_iota(jnp.int32, (tm, 1), 0) + i * tm
        tgt = jnp.concatenate([t[...] for t in t_refs], axis=0)
        err = jnp.where(rid >= CHUNK, yhat * fg - tgt, 0.0)
        loss_ref[...] += 0.5 * jnp.sum(err * err) / d
        dy = err / d
        dfg_ref[...] += jnp.sum(dy * yhat, axis=0, keepdims=True)
        dyh = dy * fg
        dout_ref[...] = r * (dyh - yhat * jnp.mean(dyh * yhat, axis=-1, keepdims=True))

    row = lambda p: pl.BlockSpec((tm, d), lambda i, p=p: (i, p))
    vec = pl.BlockSpec((1, d), lambda i: (0, 0))
    mat = pl.BlockSpec((d, d), lambda i: (0, 0))
    return pl.pallas_call(
        body, grid=(lp // tm,),
        in_specs=[row(0), row(6), row(7), row(8), row(0), row(0), vec, vec, mat, mat]
        + [pl.BlockSpec((CHUNK, d), lambda i, u=u: (jnp.maximum(i * n_slabs + u - 1, 0), 0)) for u in range(n_slabs)],
        out_specs=[row(0), row(0), row(0), row(0), pl.BlockSpec((8, 128), lambda i: (0, 0)), vec],
        out_shape=[jax.ShapeDtypeStruct((lp, d), BF16), jax.ShapeDtypeStruct((lp, d), BF16),
                   jax.ShapeDtypeStruct((lp, d), F32), jax.ShapeDtypeStruct((lp, d), F32),
                   jax.ShapeDtypeStruct((8, 128), F32), jax.ShapeDtypeStruct((1, d), F32)],
        name="tail_fwd", compiler_params=_params())(
            o, proj, proj, proj, y_conv, hres, gnorm_g, final_g, w_rec, w_out, *([target] * n_slabs))


def _tail_bwd(dout, proj, y_conv, y_rec, o, c, w_out, w_rec, w_conv, ln_g, ln_b, gnorm_g):
    lp, d = dout.shape
    n_heads = d // HEAD
    tm = _row_tile(lp)

    def body(dout_ref, mc_ref, mr_ref, z_ref, gr_ref, yc_ref, yrec_ref, o_ref, c_ref,
             wo_ref, wr_ref, wc_ref, lg_ref, lb_ref, gn_ref,
             dyc_ref, dyr_ref, doutb_ref, dz_ref, dp_ref, do_ref, dc_ref,
             dgn_ref, dlg_ref, dlb_ref, dyrin_ref):
        i = pl.program_id(0)

        @pl.when(i == 0)
        def _():
            dgn_ref[...] = jnp.zeros_like(dgn_ref)
            dlg_ref[...] = jnp.zeros_like(dlg_ref)
            dlb_ref[...] = jnp.zeros_like(dlb_ref)

        doutb = dout_ref[...].astype(BF16)
        doutb_ref[...] = doutb
        dmerged = lax.dot_general(doutb, wo_ref[...], NT, preferred_element_type=F32)
        smc = _sigmoid(mc_ref[...])
        smr = _sigmoid(mr_ref[...])
        dyc = (dmerged * smc).astype(BF16)
        dyr = (dmerged * smr).astype(BF16)
        dyc_ref[...] = dyc
        dyr_ref[...] = dyr
        dp_ref[:, d:2 * d] = (dmerged * yc_ref[...] * smc * (1.0 - smc)).astype(BF16)
        dp_ref[:, 2 * d:3 * d] = (dmerged * yrec_ref[...] * smr * (1.0 - smr)).astype(BF16)

        dyrin_ref[...] = lax.dot_general(dyr, wr_ref[...], NT, preferred_element_type=F32)
        for h in range(n_heads):
            hs = slice(h * HEAD, (h + 1) * HEAD)
            oh = o_ref[:, hs]
            rstd = lax.rsqrt(jnp.mean(oh * oh, axis=-1, keepdims=True) + EPS)
            ohat = oh * rstd
            gn = gn_ref[:, hs]
            gr = gr_ref[:, hs]
            sg = _sigmoid(gr)
            dyrin = dyrin_ref[:, hs]
            don = dyrin * (gr * sg)
            dp_ref[:, hs] = (dyrin * (ohat * gn) * _dsilu(gr, sg)).astype(BF16)
            dgn_ref[:, hs] += jnp.sum(don * ohat, axis=0, keepdims=True)
            doh = don * gn
            do_ref[:, hs] = rstd * (doh - ohat * jnp.mean(doh * ohat, axis=-1, keepdims=True))

        dycin = lax.dot_general(dyc, wc_ref[...], NT, preferred_element_type=F32)
        c = c_ref[...]
        mu = jnp.mean(c, axis=-1, keepdims=True)
        xc = c - mu
        rstd = lax.rsqrt(jnp.mean(xc * xc, axis=-1, keepdims=True) + EPS)
        nrm = xc * rstd
        lg = lg_ref[...]
        ln = nrm * lg + lb_ref[...]
        sl = _sigmoid(ln)
        z = z_ref[...]
        sz = _sigmoid(z)
        dz_ref[...] = (dycin * (ln * sl) * _dsilu(z, sz)).astype(BF16)
        dln = dycin * (z * sz) * _dsilu(ln, sl)
        dlg_ref[...] += jnp.sum(dln * nrm, axis=0, keepdims=True)
        dlb_ref[...] += jnp.sum(dln, axis=0, keepdims=True)
        dn = dln * lg
        dc_ref[...] = rstd * (dn - jnp.mean(dn, axis=-1, keepdims=True)
                              - nrm * jnp.mean(dn * nrm, axis=-1, keepdims=True))

    row = lambda p: pl.BlockSpec((tm, d), lambda i, p=p: (i, p))
    vec = pl.BlockSpec((1, d), lambda i: (0, 0))
    mat = pl.BlockSpec((d, d), lambda i: (0, 0))
    act_bf = jax.ShapeDtypeStruct((lp, d), BF16)
    act_f32 = jax.ShapeDtypeStruct((lp, d), F32)
    vec_f32 = jax.ShapeDtypeStruct((1, d), F32)
    return pl.pallas_call(
        body, grid=(lp // tm,),
        in_specs=[row(0), row(7), row(8), row(2), row(6), row(0), row(0), row(0), row(0),
                  mat, mat, mat, vec, vec, vec],
        out_specs=[row(0)] * 4 + [pl.BlockSpec((tm, 3 * d), lambda i: (i, 2))] + [row(0)] * 2 + [vec] * 3,
        out_shape=[act_bf] * 4 + [jax.ShapeDtypeStruct((lp, 9 * d), BF16)] + [act_f32] * 2 + [vec_f32] * 3,
        scratch_shapes=[pltpu.VMEM((tm, d), F32)],
        name="tail_bwd", compiler_params=_params())(
            dout, proj, proj, proj, proj, y_conv, y_rec, o, c, w_out, w_rec, w_conv, ln_g, ln_b, gnorm_g)


def _hgrn_bwd(proj, do, s_all, lb_logits, n_pad, dproj):
    lp, d = do.shape
    n_heads = d // HEAD
    nc = lp // CHUNK
    tab, utri = _hgrn_tables()
    n_tab = tab.shape[0]

    def body(qr_ref, fr_ref, ir_ref, do_ref, s0_ref, lbl_ref, tab_ref, ut_ref, _,
             dp_ref, dlbl_ref, ds_ref, t_ref, dlb_ref):
        n = pl.program_id(0)
        chunk = nc - 1 - n

        @pl.when(n == 0)
        def _():
            ds_ref[...] = jnp.zeros_like(ds_ref)
            dlb_ref[...] = jnp.zeros_like(dlb_ref)

        lb_all, pp = _lower_bound(lbl_ref)
        rid = lax.broadcasted_iota(jnp.int32, (CHUNK, 1), 0)
        valid = jnp.logical_or(chunk > 0, rid >= n_pad)
        f_all = lb_all + (1.0 - lb_all) * _sigmoid(fr_ref[...])
        t_ref[...] = _dot2(tab_ref[...], jnp.where(valid, jnp.log(f_all), 0.0))
        masks = _level_masks()
        ut = ut_ref[...]

        def head(h):
            off = h * HEAD if isinstance(h, int) else pl.multiple_of(h * HEAD, HEAD)
            hs = pl.ds(off, HEAD)
            lb = _lower_bound_slice(lbl_ref, hs)
            qr = qr_ref[:, hs]
            q, sq, f, sf, _, k = _gates(qr, fr_ref[:, hs], lb, valid)
            v = ir_ref[:, hs]
            do_h = do_ref[:, hs]
            b = t_ref[0:CHUNK, hs]
            b_last = t_ref[CHUNK - 1:CHUNK, hs]
            s0 = s0_ref[0, hs, :]
            ds1 = ds_ref[hs, :]
            eb = jnp.exp(b)
            ekl = jnp.exp(b_last - b)
            do_bf = do_h.astype(BF16)
            v_bf = v.astype(BF16)
            ds1_bf = ds1.astype(BF16)

            da = lax.dot_general(do_bf, v_bf, NT, preferred_element_type=F32)
            da_diag = jnp.sum(do_h * v, axis=-1, keepdims=True)
            a = jnp.zeros((CHUNK, CHUNK), F32)
            dq_x = eb * lax.dot_general(do_bf, s0.astype(BF16), NT, preferred_element_type=F32)
            dk_x = ekl * lax.dot_general(v_bf, ds1_bf, NT, preferred_element_type=F32)
            x_after = q * dq_x
            x_before = k * dk_x
            for lvl in range(1, N_LEVELS + 1):
                e = _level_factor(b, _level_reference(lvl, b, t_ref, hs))
                qt = (q * e).astype(BF16)
                kt = (k * e).astype(BF16)
                p = lax.dot_general(qt, kt, NT, preferred_element_type=F32)
                a = a + jnp.where(masks[lvl - 1], p, 0.0)
                dam = jnp.where(masks[lvl - 1], da, 0.0).astype(BF16)
                dqt = jnp.dot(dam, kt, preferred_element_type=F32)
                dkt = lax.dot_general(dam, qt, TN, preferred_element_type=F32)
                dq_x = dq_x + e * dqt
                dk_x = dk_x + e * dkt
                x_after = x_after + (qt.astype(F32) * dqt - kt.astype(F32) * dkt)

            dv = (lax.dot_general(a.astype(BF16), do_bf, TN, preferred_element_type=F32)
                  + jnp.sum(q * k, axis=-1, keepdims=True) * do_h
                  + jnp.dot((k * ekl).astype(BF16), ds1_bf, preferred_element_type=F32))
            dp_ref[:, pl.ds(2 * d + off, HEAD)] = dv.astype(BF16)

            carried = jnp.exp(b_last) * _col_to_row(jnp.sum(s0 * ds1, axis=-1, keepdims=True))
            dg = _dot3(ut, jnp.concatenate([x_after, x_before], axis=0)) + carried
            dq = dq_x + da_diag * k
            dk = dk_x + da_diag * q
            dp_ref[:, hs] = (dq * _dsilu(qr, sq)).astype(BF16)
            df = jnp.where(valid, dg / f - dk, 0.0)
            dp_ref[:, pl.ds(d + off, HEAD)] = (df * (1.0 - lb) * sf * (1.0 - sf)).astype(BF16)
            dlb_ref[:, hs] += jnp.sum(df * (1.0 - sf), axis=0, keepdims=True)

            ds_ref[hs, :] = (_row_to_col(jnp.exp(b_last)) * ds1
                             + lax.dot_general((q * eb).astype(BF16), do_bf, TN, preferred_element_type=F32))
        per_trip = min(HEADS_PER_TRIP, n_heads)

        def head_group(p, carry):
            for u in range(per_trip):
                head(p * per_trip + u)
            return carry

        if n_heads == per_trip:
            head_group(0, 0)
        else:
            lax.fori_loop(0, n_heads // per_trip, head_group, 0)

        @pl.when(n == nc - 1)
        def _():
            dl0 = dlb_ref[...] * pp
            dlbl_ref[0:1, :] = dl0
            dlbl_ref[1:2, :] = -dl0

    piece = lambda p: pl.BlockSpec((CHUNK, d), lambda n, p=p: (nc - 1 - n, p))
    return pl.pallas_call(
        body, grid=(nc,),
        in_specs=[piece(3), piece(4), piece(5), piece(0),
                  pl.BlockSpec((1, d, HEAD), lambda n: (nc - 1 - n, 0, 0)),
                  pl.BlockSpec((2, d), lambda n: (0, 0)),
                  pl.BlockSpec((n_tab, CHUNK), lambda n: (0, 0)),
                  pl.BlockSpec((CHUNK, 2 * CHUNK), lambda n: (0, 0)), ANY],
        out_specs=[pl.BlockSpec((CHUNK, 3 * d), lambda n: (nc - 1 - n, 1)), pl.BlockSpec((2, d), lambda n: (0, 0))],
        out_shape=[jax.ShapeDtypeStruct(dproj.shape, BF16), jax.ShapeDtypeStruct((2, d), F32)],
        input_output_aliases={8: 0},
        scratch_shapes=[pltpu.VMEM((d, HEAD), F32), pltpu.VMEM((n_tab, d), F32), pltpu.VMEM((1, d), F32)],
        name="hgrn_bwd", compiler_params=_params())(proj, proj, proj, do, s_all, lb_logits, tab, utri, dproj)


def _conv_bwd(dc, proj, conv_w, dz, dproj):
    lp, d = dc.shape
    tm = _row_tile(lp)
    hb = tm // HALO
    n_tiles = lp // tm
    last_halo = lp // HALO - 1

    def body(dc_ref, dcn_ref, ua_ref, ub_ref, uap_ref, ubp_ref, cw_ref, dz_ref, _,
             dp_ref, dcw_ref, dcb_ref, aext_ref, dcext_ref, da_ref, dcw_acc):
        i = pl.program_id(0)

        @pl.when(i == 0)
        def _():
            dcw_acc[...] = jnp.zeros_like(dcw_acc)
            dcb_ref[...] = jnp.zeros_like(dcb_ref)

        ua = ua_ref[...]
        sb = _sigmoid(ub_ref[...])
        a_prev = uap_ref[...] * _sigmoid(ubp_ref[...])
        aext_ref[0:HALO, :] = jnp.where(i > 0, a_prev, 0.0)
        aext_ref[HALO:HALO + tm, :] = ua * sb
        dcext_ref[0:tm, :] = dc_ref[...]
        dcext_ref[tm:tm + HALO, :] = jnp.where(i < n_tiles - 1, dcn_ref[...], 0.0)
        dcb_ref[...] += jnp.sum(dc_ref[...], axis=0, keepdims=True)

        def row_block(r, carry):
            r0 = pl.multiple_of(r * CONV_ROWS, CONV_ROWS)
            n_rows = CONV_ROWS + HALO
            for cs in range(d // CONV_LANES):
                cl = slice(cs * CONV_LANES, (cs + 1) * CONV_LANES)
                dblk = dcext_ref[pl.ds(r0, n_rows), cl]
                ablk = aext_ref[pl.ds(r0, n_rows), cl]
                dcur = dblk[0:CONV_ROWS, :]
                acc = jnp.zeros((CONV_ROWS, CONV_LANES), F32)
                for b in range(SUBLANES):
                    dsh = dblk if b == 0 else pltpu.roll(dblk, n_rows - b, axis=0)
                    ash = ablk if b == 0 else pltpu.roll(ablk, n_rows - b, axis=0)
                    for a in range(5):
                        j_da = CONV_WIDTH - 1 - (SUBLANES * a + b)
                        if 0 <= j_da < CONV_WIDTH:
                            acc = acc + cw_ref[j_da:j_da + 1, cl] * dsh[SUBLANES * a:SUBLANES * a + CONV_ROWS, :]
                        j_w = SUBLANES * a + b - 2
                        if 0 <= j_w < CONV_WIDTH:
                            prod = dcur * ash[SUBLANES * a:SUBLANES * a + CONV_ROWS, :]
                            dcw_acc[j_w, :, cl] += prod.reshape(CONV_ROWS // SUBLANES, SUBLANES, CONV_LANES).sum(axis=0)
                da_ref[pl.ds(r0, CONV_ROWS), cl] = acc
            return carry

        lax.fori_loop(0, tm // CONV_ROWS, row_block, 0)

        da = da_ref[...]
        dp_ref[:, 0:d] = (da * sb).astype(BF16)
        dp_ref[:, d:2 * d] = (da * ua * sb * (1.0 - sb)).astype(BF16)
        dp_ref[:, 2 * d:3 * d] = dz_ref[...]

        @pl.when(i == n_tiles - 1)
        def _():
            dcw_ref[...] = jnp.sum(dcw_acc[...], axis=1)

    row = lambda p: pl.BlockSpec((tm, d), lambda i, p=p: (i, p))
    prev = lambda p: pl.BlockSpec((HALO, d), lambda i, p=p: (jnp.maximum(i * hb - 1, 0), p))
    nxt = pl.BlockSpec((HALO, d), lambda i: (jnp.minimum((i + 1) * hb, last_halo), 0))
    return pl.pallas_call(
        body, grid=(n_tiles,),
        in_specs=[row(0), nxt, row(0), row(1), prev(0), prev(1), pl.BlockSpec((HALO, d), lambda i: (0, 0)),
                  row(0), ANY],
        out_specs=[pl.BlockSpec((tm, 3 * d), lambda i: (i, 0)), pl.BlockSpec((HALO, d), lambda i: (0, 0)),
                   pl.BlockSpec((1, d), lambda i: (0, 0))],
        out_shape=[jax.ShapeDtypeStruct(dproj.shape, BF16),
                   jax.ShapeDtypeStruct((HALO, d), F32), jax.ShapeDtypeStruct((1, d), F32)],
        input_output_aliases={8: 0},
        scratch_shapes=[pltpu.VMEM((HALO + tm, d), F32), pltpu.VMEM((tm + HALO, d), F32), pltpu.VMEM((tm, d), F32),
                        pltpu.VMEM((HALO, SUBLANES, d), F32)],
        name="conv_bwd", compiler_params=_params())(dc, dc, proj, proj, proj, proj, conv_w, dz, dproj)


def _weight_grad(xs, dy, name, blocked):
    lp, dx = xs.shape
    n = dy.shape[1]
    tk = _mm_row_tile(lp)
    if blocked:
        ncol = n // N_CHIPS
        nt = W_IN_COL_TILES
        tn = ncol // nt
        grid = (N_CHIPS * nt, lp // tk)
        out_spec = pl.BlockSpec((1, dx, tn), lambda c, k: (c // nt, 0, c % nt))
        out_shape = jax.ShapeDtypeStruct((N_CHIPS, dx, ncol), F32)
    else:
        tn = n
        grid = (1, lp // tk)
        out_spec = pl.BlockSpec((dx, tn), lambda c, k: (0, c))
        out_shape = jax.ShapeDtypeStruct((dx, n), F32)

    def body(xs_ref, dy_ref, o_ref, *copy_ref):
        @pl.when(pl.program_id(1) == 0)
        def _():
            o_ref[...] = jnp.zeros_like(o_ref)

        p = lax.dot_general(xs_ref[...], dy_ref[...], TN, preferred_element_type=F32)
        if blocked:
            o_ref[0] += p

            @pl.when(pl.program_id(1) == lp // tk - 1)
            def _():
                copy_ref[0][0] = o_ref[0].astype(BF16)
        else:
            o_ref[...] += p

    if blocked:
        out_spec = [out_spec, out_spec]
        out_shape = [out_shape, jax.ShapeDtypeStruct(out_shape.shape, BF16)]
    return pl.pallas_call(
        body, grid=grid,
        in_specs=[pl.BlockSpec((tk, dx), lambda c, k: (k, 0)), pl.BlockSpec((tk, tn), lambda c, k: (k, c))],
        out_specs=out_spec, out_shape=out_shape,
        name=name, compiler_params=_params())(xs, dy)


def _in_proj_bwd(dproj, wtg, hres, norm_g, dout):
    lp, d = hres.shape
    _, ncol, _ = wtg.shape
    tm = _mm_row_tile(lp)
    nt = W_IN_COL_TILES
    tn = ncol // nt
    nk = N_CHIPS * nt

    def body(dp_ref, w_ref, x_ref, g_ref, dout_ref, dx_ref, dg_ref, acc_ref):
        i = pl.program_id(0)
        kk = pl.program_id(1)

        @pl.when(jnp.logical_and(i == 0, kk == 0))
        def _():
            dg_ref[...] = jnp.zeros_like(dg_ref)

        @pl.when(kk == 0)
        def _():
            acc_ref[...] = jnp.zeros_like(acc_ref)

        acc_ref[...] += jnp.dot(dp_ref[...], w_ref[0], preferred_element_type=F32)

        @pl.when(kk == nk - 1)
        def _():
            x = x_ref[...]
            r = lax.rsqrt(jnp.mean(x * x, axis=-1, keepdims=True) + EPS)
            xhat = x * r
            dh = acc_ref[...]
            dg_ref[...] += jnp.sum(dh * xhat, axis=0, keepdims=True)
            dxh = dh * g_ref[...]
            dx_ref[...] = dout_ref[...] + r * (dxh - xhat * jnp.mean(dxh * xhat, axis=-1, keepdims=True))

    return pl.pallas_call(
        body, grid=(lp // tm, nk),
        in_specs=[pl.BlockSpec((tm, tn), lambda i, k: (i, k)),
                  pl.BlockSpec((1, tn, d), lambda i, k: (k // nt, k % nt, 0)),
                  pl.BlockSpec((tm, d), lambda i, k: (i, 0)),
                  pl.BlockSpec((1, d), lambda i, k: (0, 0)),
                  pl.BlockSpec((tm, d), lambda i, k: (i, 0))],
        out_specs=[pl.BlockSpec((tm, d), lambda i, k: (i, 0)), pl.BlockSpec((1, d), lambda i, k: (0, 0))],
        out_shape=[jax.ShapeDtypeStruct((lp, d), F32), jax.ShapeDtypeStruct((1, d), F32)],
        scratch_shapes=[pltpu.VMEM((tm, d), F32)],
        name="in_proj_bwd", compiler_params=_params())(dproj, wtg, hres, norm_g, dout)


def _adamw_math(w, g, m, v):
    m = ADAM_B1 * m + (1.0 - ADAM_B1) * g
    v = ADAM_B2 * v + (1.0 - ADAM_B2) * (g * g)
    m_hat = m / (1.0 - ADAM_B1 ** ADAM_STEP)
    v_hat = v / (1.0 - ADAM_B2 ** ADAM_STEP)
    delta = -ADAM_LR * (m_hat / (jnp.sqrt(v_hat) + ADAM_EPS) + ADAM_WD * w)
    return delta, m, v


def _elementwise_rows(shape):
    r, c = shape
    for t in (256, 128, 64, 32, 16, 8):
        if r % t == 0 and r > t and t * c * 4 <= ELEMENTWISE_BLOCK_BYTES:
            return t
    return r


def _adamw(name, w, m, v, *g_parts):
    shape = w.shape
    tr = _elementwise_rows(shape)
    n_g = len(g_parts)

    def body(*refs):
        w_ref, m_ref, v_ref = refs[:3]
        g_refs = refs[3:3 + n_g]
        g_out, d_out, m_out, v_out = refs[3 + n_g:]
        g = g_refs[0][...]
        for gr in g_refs[1:]:
            g = g + gr[...]
        delta, m_new, v_new = _adamw_math(w_ref[...], g, m_ref[...], v_ref[...])
        g_out[...] = g
        d_out[...] = delta
        m_out[...] = m_new
        v_out[...] = v_new

    spec = pl.BlockSpec((tr, shape[1]), lambda i: (i, 0))
    return pl.pallas_call(
        body, grid=(shape[0] // tr,),
        in_specs=[spec] * (3 + n_g), out_specs=[spec] * 4,
        out_shape=[jax.ShapeDtypeStruct(shape, F32)] * 4,
        name=name, compiler_params=_params())(w, m, v, *g_parts)


def _chip_half_sum(name, g, recv, core):
    _, _, hr, cols = g.shape
    tr = _elementwise_rows((hr, cols))

    def body(core_ref, g_ref, r_ref, o_ref, ob_ref):
        s = g_ref[0, 0] + r_ref[0].astype(F32)
        o_ref[0] = s
        ob_ref[0] = s.astype(BF16)

    blk = pl.BlockSpec((1, tr, cols), lambda j, i, core_ref: (j, i, 0))
    grid_spec = pltpu.PrefetchScalarGridSpec(
        num_scalar_prefetch=1, grid=(N_CHIPS, hr // tr),
        in_specs=[pl.BlockSpec((1, 1, tr, cols), lambda j, i, core_ref: (j, core_ref[0], i, 0)), blk],
        out_specs=[blk, blk])
    return pl.pallas_call(
        body, grid_spec=grid_spec,
        out_shape=[jax.ShapeDtypeStruct((N_CHIPS, hr, cols), F32), jax.ShapeDtypeStruct((N_CHIPS, hr, cols), BF16)],
        name=name, compiler_params=_params())(core, g, recv)


def _block_half_total(name, chip_sums, recv, chip_core, after):
    _, hr, cols = chip_sums.shape
    tr = _elementwise_rows((hr, cols))

    def body(cc_ref, p_ref, r_ref, after_ref, o_ref):
        s = p_ref[0]
        for k in range(3):
            s = s + r_ref[k].astype(F32)
        o_ref[0] = s

    grid_spec = pltpu.PrefetchScalarGridSpec(
        num_scalar_prefetch=1, grid=(hr // tr,),
        in_specs=[pl.BlockSpec((1, tr, cols), lambda i, cc_ref: (cc_ref[0], i, 0)),
                  pl.BlockSpec((3, tr, cols), lambda i, cc_ref: (0, i, 0)), ANY],
        out_specs=pl.BlockSpec((1, tr, cols), lambda i, cc_ref: (cc_ref[1], i, 0)))
    return pl.pallas_call(
        body, grid_spec=grid_spec, out_shape=jax.ShapeDtypeStruct((2, hr, cols), F32),
        name=name, compiler_params=_params())(chip_core, chip_sums, recv, after)


def _place_shard(name, w, chip, dtype):
    r, c = w.shape
    tr = _elementwise_rows((r, c))

    def body(chip_ref, w_ref, o_ref):
        o_ref[0] = w_ref[...].astype(dtype)

    grid_spec = pltpu.PrefetchScalarGridSpec(
        num_scalar_prefetch=1, grid=(r // tr,),
        in_specs=[pl.BlockSpec((tr, c), lambda i, chip_ref: (i, 0))],
        out_specs=pl.BlockSpec((1, tr, c), lambda i, chip_ref: (chip_ref[0], i, 0)))
    return pl.pallas_call(
        body, grid_spec=grid_spec, out_shape=jax.ShapeDtypeStruct((N_CHIPS, r, c), dtype),
        name=name, compiler_params=_params())(chip, w)


def _sum_slots(name, slots, own, my_idx):
    k, r, c = slots.shape

    def body(idx_ref, s_ref, own_ref, o_ref):
        s = None
        for j in range(k):
            term = jnp.where(idx_ref[0] == j, own_ref[...], s_ref[j])
            s = term if s is None else s + term
        o_ref[...] = s

    grid_spec = pltpu.PrefetchScalarGridSpec(
        num_scalar_prefetch=1, grid=(1,),
        in_specs=[pl.BlockSpec((k, r, c), lambda i, idx_ref: (0, 0, 0)),
                  pl.BlockSpec((r, c), lambda i, idx_ref: (0, 0))],
        out_specs=pl.BlockSpec((r, c), lambda i, idx_ref: (0, 0)))
    return pl.pallas_call(body, grid_spec=grid_spec, out_shape=jax.ShapeDtypeStruct((r, c), F32), name=name,
                          compiler_params=_params())(my_idx, slots, own)


def _mesh_pos():
    return lax.axis_index("x"), lax.axis_index("y"), lax.axis_index("c")


def _other_chips(x, y):
    return [(1 - x, y), (x, 1 - y), (1 - x, 1 - y)]


def _gather_weights(bufs):
    n = len(bufs)
    half = [b.shape[1] // 2 for b in bufs]

    def body(*refs):
        gathered = refs[n:2 * n]
        ici_send, ici_recv, d2d_send, d2d_recv = refs[2 * n:]
        x, y, c = _mesh_pos()
        me = 2 * x + y
        chips = _other_chips(x, y)

        def part(a, block, core):
            return gathered[a].at[block, pl.ds(core * half[a], half[a])]

        def over_ici(a, k, block):
            px, py = chips[k]
            return pltpu.make_async_remote_copy(
                src_ref=part(a, block, c), dst_ref=part(a, block, c),
                send_sem=ici_send.at[a, k], recv_sem=ici_recv.at[a, k],
                device_id=(px, py, c), device_id_type=MESH)

        def over_d2d(a, k, core):
            px, py = chips[k]
            return pltpu.make_async_remote_copy(
                src_ref=part(a, 2 * px + py, core), dst_ref=part(a, 2 * px + py, core),
                send_sem=d2d_send.at[a, k], recv_sem=d2d_recv.at[a, k],
                device_id=(x, y, 1 - c), device_id_type=MESH)

        for a in range(n):
            for k in range(3):
                over_ici(a, k, me).start()
        for a in range(n):
            for k, (px, py) in enumerate(chips):
                over_ici(a, k, 2 * px + py).wait_recv()
                over_d2d(a, k, c).start()
        for a in range(n):
            for k in range(3):
                over_d2d(a, k, 1 - c).wait_recv()
        for a in range(n):
            for k in range(3):
                over_ici(a, k, me).wait_send()
                over_d2d(a, k, c).wait_send()

    return pl.pallas_call(
        body, in_specs=[ANY] * n, out_specs=[ANY] * n,
        out_shape=[jax.ShapeDtypeStruct(b.shape, b.dtype) for b in bufs],
        input_output_aliases={a: a for a in range(n)},
        scratch_shapes=[pltpu.SemaphoreType.DMA((n, 3))] * 4,
        name="gather_weights")(*bufs)


def _gather_in_proj(h, bufs, order):
    n = len(bufs)
    half = [b.shape[1] // 2 for b in bufs]
    lp, d = h.shape
    ncol = bufs[0].shape[2]
    tm = _mm_row_tile(lp)
    n_row = lp // tm

    def body(order_ref, h_ref, *refs):
        gathered = refs[n:2 * n]
        o_ref, wt_ref = refs[2 * n], refs[2 * n + 1]
        w_buf, ici_send, ici_recv, d2d_send, d2d_recv, w_sem = refs[2 * n + 2:]
        j = pl.program_id(0)
        i = pl.program_id(1)
        x, y, c = _mesh_pos()
        me = 2 * x + y
        chips = _other_chips(x, y)

        def part(a, block, core):
            return gathered[a].at[block, pl.ds(core * half[a], half[a])]

        def over_ici(a, k, block):
            px, py = chips[k]
            return pltpu.make_async_remote_copy(
                src_ref=part(a, block, c), dst_ref=part(a, block, c),
                send_sem=ici_send.at[a, k], recv_sem=ici_recv.at[a, k],
                device_id=(px, py, c), device_id_type=MESH)

        def over_d2d(a, k, core):
            px, py = chips[k]
            return pltpu.make_async_remote_copy(
                src_ref=part(a, 2 * px + py, core), dst_ref=part(a, 2 * px + py, core),
                send_sem=d2d_send.at[a, k], recv_sem=d2d_recv.at[a, k],
                device_id=(x, y, 1 - c), device_id_type=MESH)

        @pl.when(jnp.logical_and(j == 0, i == 0))
        def _():
            for a in range(n):
                for k in range(2):
                    over_ici(a, k, me).start()

        for k, (px, py) in enumerate(chips):
            @pl.when(jnp.logical_and(j == k + 1, i == 0))
            def _(k=k, px=px, py=py):
                for a in range(n):
                    over_ici(a, k, 2 * px + py).wait_recv()
                    over_d2d(a, k, c).start()
                if k == 0:
                    for a in range(n):
                        over_ici(a, 2, me).start()
                for a in range(n):
                    over_d2d(a, k, 1 - c).wait_recv()

        @pl.when(i == 0)
        def _():
            load = pltpu.make_async_copy(gathered[0].at[order_ref[j]], w_buf, w_sem)
            load.start()
            load.wait()
            wt_ref[0] = w_buf[...].T

        o_ref[...] = jnp.dot(h_ref[...], w_buf[...], preferred_element_type=F32)

        @pl.when(jnp.logical_and(j == N_CHIPS - 1, i == n_row - 1))
        def _():
            for a in range(n):
                for k in range(3):
                    over_ici(a, k, me).wait_send()
                    over_d2d(a, k, c).wait_send()

    grid_spec = pltpu.PrefetchScalarGridSpec(
        num_scalar_prefetch=1, grid=(N_CHIPS, n_row),
        in_specs=[pl.BlockSpec((tm, d), lambda j, i, order_ref: (i, 0))] + [ANY] * n,
        out_specs=[ANY] * n + [pl.BlockSpec((tm, ncol), lambda j, i, order_ref: (i, order_ref[j])),
                               pl.BlockSpec((1, ncol, d), lambda j, i, order_ref: (order_ref[j], 0, 0))],
        scratch_shapes=[pltpu.VMEM((d, ncol), BF16)] + [pltpu.SemaphoreType.DMA((n, 3))] * 4
        + [pltpu.SemaphoreType.DMA])
    out = pl.pallas_call(
        body, grid_spec=grid_spec,
        out_shape=[jax.ShapeDtypeStruct(b.shape, b.dtype) for b in bufs]
        + [jax.ShapeDtypeStruct((lp, N_CHIPS * ncol), F32), jax.ShapeDtypeStruct((N_CHIPS, ncol, d), BF16)],
        input_output_aliases={2 + a: a for a in range(n)},
        name="gather_in_proj", compiler_params=_params())(order, h, *bufs)
    return out[n], out[n + 1], out[:n]


def _send_other_halves(grads, tag):
    n = len(grads)

    def body(*refs):
        srcs = refs[:n]
        dsts = refs[n:2 * n]
        send_sems, recv_sems = refs[2 * n:]
        x, y, c = _mesh_pos()
        copies = [pltpu.make_async_remote_copy(
            src_ref=srcs[a].at[j, 1 - c], dst_ref=dsts[a].at[j], send_sem=send_sems.at[a, j],
            recv_sem=recv_sems.at[a, j], device_id=(x, y, 1 - c), device_id_type=MESH)
            for a in range(n) for j in range(N_CHIPS)]
        for cp in copies:
            cp.start()
        for cp in copies:
            cp.wait()

    return pl.pallas_call(
        body, in_specs=[ANY] * n, out_specs=[ANY] * n,
        out_shape=[jax.ShapeDtypeStruct((N_CHIPS,) + g.shape[2:], g.dtype) for g in grads],
        scratch_shapes=[pltpu.SemaphoreType.DMA((n, N_CHIPS))] * 2,
        name="send_other_halves_" + tag)(*grads)


HBM = pl.BlockSpec(memory_space=pltpu.HBM)
SEM = pl.BlockSpec(memory_space=pltpu.SEMAPHORE)


def _block_copies(n, srcs, dsts, send_sems, recv_sems):
    x, y, c = _mesh_pos()
    return [pltpu.make_async_remote_copy(
        src_ref=srcs[a].at[2 * px + py], dst_ref=dsts[a].at[k], send_sem=send_sems.at[3 * a + k],
        recv_sem=recv_sems.at[3 * a + k], device_id=(px, py, c), device_id_type=MESH)
        for a in range(n) for k, (px, py) in enumerate(_other_chips(x, y))]


def _exchange_start(blocked, tag):
    n = len(blocked)
    lands = [lax.empty((3,) + b.shape[1:], b.dtype) for b in blocked]
    bufs = [pltpu.with_memory_space_constraint(b, pltpu.HBM) for b in list(blocked) + lands]
    nb = 2 * n

    def body(*refs):
        for cp in _block_copies(n, refs[:n], refs[n:nb], refs[nb], refs[nb + 1]):
            cp.start()
        refs[-1][...] = jnp.zeros_like(refs[-1])

    out = pl.pallas_call(
        body, name="exchange_start_" + tag,
        in_specs=[HBM] * nb,
        out_shape=[pltpu.SemaphoreType.DMA((3 * n,)), pltpu.SemaphoreType.DMA((3 * n,))]
        + [pltpu.HBM(b.shape, b.dtype) for b in bufs] + [jax.ShapeDtypeStruct((8, 128), F32)],
        out_specs=[SEM] * 2 + [HBM] * nb + [pl.BlockSpec(memory_space=pltpu.VMEM)],
        input_output_aliases={i: 2 + i for i in range(nb)},
        compiler_params=pltpu.CompilerParams(has_side_effects=pltpu.SideEffectType.DATAFLOW_SIDE_EFFECTING),
    )(*bufs)
    return (out[:2], out[2:2 + nb]), out[-1]


def _exchange_wait(state, after, tag):
    sems, bufs = state
    nb = len(bufs)
    n = nb // 2

    def body(*refs):
        for cp in _block_copies(n, refs[:n], refs[n:nb], refs[nb], refs[nb + 1]):
            cp.wait_send()
            cp.wait_recv()

    out = pl.pallas_call(
        body, name="exchange_wait_" + tag,
        in_specs=[HBM] * nb + [SEM] * 2 + [ANY],
        out_shape=[pltpu.HBM(b.shape, b.dtype) for b in bufs],
        out_specs=[HBM] * nb,
        input_output_aliases={i: i for i in range(nb)},
        compiler_params=pltpu.CompilerParams(has_side_effects=pltpu.SideEffectType.DATAFLOW_SIDE_EFFECTING),
    )(*bufs, *sems, after)
    return out[n:nb]


def _whole_block_copies(buf, send_sems, recv_sems, incoming):
    x, y, c = _mesh_pos()
    me = 2 * x + y
    out = []
    for k, (px, py) in enumerate(_other_chips(x, y)):
        block = 2 * px + py if incoming else me
        out.append(pltpu.make_async_remote_copy(
            src_ref=buf.at[block], dst_ref=buf.at[block], send_sem=send_sems.at[k], recv_sem=recv_sems.at[k],
            device_id=(px, py, c), device_id_type=MESH))
    return out


def _gather_start(buf, after, tag):
    buf = pltpu.with_memory_space_constraint(buf, pltpu.HBM)

    def body(buf_ref, after_ref, send_sems, recv_sems, thru_ref, token):
        for cp in _whole_block_copies(buf_ref, send_sems, recv_sems, incoming=False):
            cp.start()
        token[...] = jnp.zeros_like(token)

    out = pl.pallas_call(
        body, name="gather_start_" + tag,
        in_specs=[HBM, ANY],
        out_shape=[pltpu.SemaphoreType.DMA((3,)), pltpu.SemaphoreType.DMA((3,)), pltpu.HBM(buf.shape, buf.dtype),
                   jax.ShapeDtypeStruct((8, 128), F32)],
        out_specs=[SEM, SEM, HBM, pl.BlockSpec(memory_space=pltpu.VMEM)],
        input_output_aliases={0: 2},
        compiler_params=pltpu.CompilerParams(has_side_effects=pltpu.SideEffectType.DATAFLOW_SIDE_EFFECTING),
    )(buf, after)
    return out[:3], out[3]


def _gather_wait(state, after, tag):
    send_sems, recv_sems, buf = state

    def body(buf_ref, send_ref, recv_ref, after_ref, out_ref):
        for cp in _whole_block_copies(buf_ref, send_ref, recv_ref, incoming=True):
            cp.wait_send()
            cp.wait_recv()

    return pl.pallas_call(
        body, name="gather_wait_" + tag,
        in_specs=[HBM, SEM, SEM, ANY],
        out_shape=pltpu.HBM(buf.shape, buf.dtype), out_specs=HBM,
        input_output_aliases={0: 0},
        compiler_params=pltpu.CompilerParams(has_side_effects=pltpu.SideEffectType.DATAFLOW_SIDE_EFFECTING),
    )(buf, send_sems, recv_sems, after)


def _small_copies(small_ref, slots_ref, send_sems, recv_sems, incoming):
    x, y, c = _mesh_pos()
    out = []
    for r in range(1, 8):
        px = 1 - x if r & 4 else x
        py = 1 - y if r & 2 else y
        pc = 1 - c if r & 1 else c
        slot = 4 * px + 2 * py + pc if incoming else 4 * x + 2 * y + c
        out.append(pltpu.make_async_remote_copy(
            src_ref=small_ref, dst_ref=slots_ref.at[slot], send_sem=send_sems.at[r - 1],
            recv_sem=recv_sems.at[r - 1], device_id=(px, py, pc), device_id_type=MESH))
    return out


def _small_start(small):
    bufs = [pltpu.with_memory_space_constraint(b, pltpu.HBM)
            for b in (small, lax.empty((8,) + small.shape, small.dtype))]

    def body(small_ref, slots_ref, send_sems, recv_sems, small_thru, slots_thru, token):
        for cp in _small_copies(small_ref, slots_ref, send_sems, recv_sems, incoming=False):
            cp.start()
        token[...] = jnp.zeros_like(token)

    out = pl.pallas_call(
        body, name="small_start",
        in_specs=[HBM, HBM],
        out_shape=[pltpu.SemaphoreType.DMA((7,)), pltpu.SemaphoreType.DMA((7,))]
        + [pltpu.HBM(b.shape, b.dtype) for b in bufs] + [jax.ShapeDtypeStruct((8, 128), F32)],
        out_specs=[SEM, SEM, HBM, HBM, pl.BlockSpec(memory_space=pltpu.VMEM)],
        input_output_aliases={0: 2, 1: 3},
        compiler_params=pltpu.CompilerParams(has_side_effects=pltpu.SideEffectType.DATAFLOW_SIDE_EFFECTING),
    )(*bufs)
    return out[:4], out[4]


def _small_wait(state, after):
    send_sems, recv_sems, small, slots = state

    def body(small_ref, slots_ref, send_ref, recv_ref, after_ref, small_out, slots_out):
        for cp in _small_copies(small_ref, slots_ref, send_ref, recv_ref, incoming=True):
            cp.wait_send()
            cp.wait_recv()

    return pl.pallas_call(
        body, name="small_wait",
        in_specs=[HBM, HBM, SEM, SEM, ANY],
        out_shape=[pltpu.HBM(small.shape, small.dtype), pltpu.HBM(slots.shape, slots.dtype)],
        out_specs=[HBM, HBM], input_output_aliases={0: 0, 1: 1},
        compiler_params=pltpu.CompilerParams(has_side_effects=pltpu.SideEffectType.DATAFLOW_SIDE_EFFECTING),
    )(small, slots, send_sems, recv_sems, after)[1]


def _join_halves(bufs):
    n = len(bufs)

    def body(*refs):
        joined = refs[n:2 * n]
        send_sems, recv_sems = refs[2 * n:]
        x, y, c = _mesh_pos()
        for a in range(n):
            pltpu.make_async_remote_copy(
                src_ref=joined[a].at[c], dst_ref=joined[a].at[c], send_sem=send_sems.at[a],
                recv_sem=recv_sems.at[a], device_id=(x, y, 1 - c), device_id_type=MESH).start()
        for a in range(n):
            pltpu.make_async_remote_copy(
                src_ref=joined[a].at[c], dst_ref=joined[a].at[1 - c], send_sem=send_sems.at[a],
                recv_sem=recv_sems.at[a], device_id=(x, y, 1 - c), device_id_type=MESH).wait()

    return pl.pallas_call(
        body, in_specs=[ANY] * n, out_specs=[ANY] * n,
        out_shape=[jax.ShapeDtypeStruct(b.shape, b.dtype) for b in bufs],
        input_output_aliases={a: a for a in range(n)},
        scratch_shapes=[pltpu.SemaphoreType.DMA((n,))] * 2,
        name="join_halves")(*bufs)


def kernel(x, meta_tokens, norm_g, w_in, conv_w, conv_b, ln_g, ln_b, w_conv_out, lb_logits, gnorm_g, w_rec_out, w_out, final_g, loss_target, m_meta_tokens, m_norm_g, m_w_in, m_conv_w, m_conv_b, m_ln_g, m_ln_b, m_w_conv_out, m_lb_logits, m_gnorm_g, m_w_rec_out, m_w_out, m_final_g, v_meta_tokens, v_norm_g, v_w_in, v_conv_w, v_conv_b, v_ln_g, v_ln_b, v_w_conv_out, v_lb_logits, v_gnorm_g, v_w_rec_out, v_w_out, v_final_g):
    d = x.shape[2]
    n_meta = meta_tokens.shape[0]
    n_pad = CHUNK - n_meta
    ds = d // N_CHIPS
    chip = 2 * lax.axis_index("x") + lax.axis_index("y")

    conv_w_pad = jnp.pad(conv_w[0], ((0, HALO - CONV_WIDTH), (0, 0)))
    chip_idx = chip.astype(jnp.int32).reshape(1)
    (small_g,) = _gather_weights([
        _place_shard("place_small", jnp.concatenate([conv_w_pad, meta_tokens], axis=0), chip_idx, F32)])
    cw_full = jnp.transpose(small_g[:, 0:HALO], (1, 0, 2)).reshape(HALO, d)
    meta_full = jnp.transpose(small_g[:, HALO:HALO + n_meta], (1, 0, 2)).reshape(n_meta, d)

    hres = jnp.concatenate([jnp.zeros((n_pad, d), F32), meta_full, x[0]], axis=0)
    target = loss_target[0]
    final_g2 = final_g.reshape(1, d)
    h = _rmsnorm_fwd(hres, norm_g)
    fx, fy = 1 - lax.axis_index("x"), 1 - lax.axis_index("y")
    order = jnp.stack([chip, 2 * fx + (1 - fy), 2 * (1 - fx) + fy, 2 * fx + fy]).astype(jnp.int32)
    proj, win_t, _ = _gather_in_proj(h, [_place_shard("place_w_in", w_in[0], chip_idx, BF16)], order)
    sq_own = _place_shard("place_square", jnp.concatenate([w_conv_out[0], w_rec_out[0], w_out[0]], axis=0),
                          chip_idx, BF16)
    sq_flight, sq_token = _gather_start(sq_own, proj, "square")
    o, s_all = _hgrn_fwd(proj, lb_logits + sq_token[0:1, 0:1], n_pad)
    sq_g = _gather_wait(sq_flight, s_all, "square")
    wc_full = sq_g[:, 0:ds].reshape(d, d)
    wr_full = sq_g[:, ds:2 * ds].reshape(d, d)
    wo_full = sq_g[:, 2 * ds:3 * ds].reshape(d, d)
    c, yc_in, y_conv = _conv_fwd(proj, cw_full, conv_b, ln_g, ln_b, wc_full)
    yr_in, merged, y_rec, dout, loss_acc, dfinal_g = _tail_fwd(
        o, proj, y_conv, hres, target, gnorm_g, final_g2, wr_full, wo_full)

    (dyc, dyr, dout_bf, dz, dproj, do, dc, dgnorm_g, dln_g, dln_b) = _tail_bwd(
        dout, proj, y_conv, y_rec, o, c, wo_full, wr_full, wc_full, ln_g, ln_b, gnorm_g)
    g_wc = _weight_grad(yc_in, dyc, "grad_w_conv_out", False)
    g_wr = _weight_grad(yr_in, dyr, "grad_w_rec_out", False)
    g_wo = _weight_grad(merged, dout_bf, "grad_w_out", False)

    core = lax.axis_index("c").astype(jnp.int32).reshape(1)

    def chip_sum_and_start(g, tag, g_to_sibling=None):
        halves = lambda a: a.reshape(N_CHIPS, 2, a.shape[1] // 2, a.shape[2])
        g = halves(g)
        (from_sibling,) = _send_other_halves([g if g_to_sibling is None else halves(g_to_sibling)], tag)
        sums = _chip_half_sum("chip_half_sum_" + tag, g, from_sibling, core)
        in_flight, token = _exchange_start([sums[1]], tag)
        return sums[0], in_flight, token[0:1, 0:1]

    g_sq = jnp.concatenate([g.reshape(N_CHIPS, ds, d) for g in (g_wc, g_wr, g_wo)], axis=1)
    sum_sq, flight_sq, token_sq = chip_sum_and_start(g_sq, "square")
    dproj, dlb_logits = _hgrn_bwd(proj, do, s_all, lb_logits + token_sq, n_pad, dproj)
    dproj, dconv_w, dconv_b = _conv_bwd(dc, proj, cw_full, dz, dproj)
    (recv_sq,) = _exchange_wait(flight_sq, dconv_b, "square")
    g_win, g_win_bf = _weight_grad(h, dproj, "grad_w_in", True)
    sum_win, flight_win, token_win = chip_sum_and_start(g_win, "w_in", g_win_bf)
    dhres, dnorm_g = _in_proj_bwd(dproj, win_t, hres, norm_g + token_win, dout)
    grad_x = dhres[CHUNK:][None]
    (recv_win,) = _exchange_wait(flight_win, dnorm_g, "w_in")
    small = jnp.concatenate([dnorm_g, dconv_b, dln_g, dln_b, dlb_logits, dgnorm_g, dfinal_g,
                             dhres[n_pad:CHUNK], dconv_w[:CONV_WIDTH],
                             jnp.broadcast_to(loss_acc[0:1, 0:1], (1, d))], axis=0)
    small_flight, small_token = _small_start(small)
    chip_core = jnp.concatenate([chip_idx, core])
    totals = [_block_half_total("block_half_total_" + nm, s, r, chip_core, small_token)
              for nm, s, r in zip(("w_in", "square"), (sum_win, sum_sq), (recv_win, recv_sq))]
    joined = _join_halves(totals)
    gt_win, gt_sq = [t.reshape(2 * t.shape[1], t.shape[2]) for t in joined]
    small_slots = _small_wait(small_flight, joined[1])
    device_idx = (2 * chip_idx + core).astype(jnp.int32)
    small_sum = _sum_slots("sum_small", small_slots, small, device_idx)

    res = {}
    res["w_in"] = _adamw("adamw_w_in", w_in[0], m_w_in[0], v_w_in[0], gt_win)
    res["w_conv_out"] = _adamw("adamw_w_conv_out", w_conv_out[0], m_w_conv_out[0], v_w_conv_out[0], gt_sq[0:ds])
    res["w_rec_out"] = _adamw("adamw_w_rec_out", w_rec_out[0], m_w_rec_out[0], v_w_rec_out[0], gt_sq[ds:2 * ds])
    res["w_out"] = _adamw("adamw_w_out", w_out[0], m_w_out[0], v_w_out[0], gt_sq[2 * ds:3 * ds])
    big = {k: tuple(a[None] for a in v) for k, v in res.items()}

    rep_names = ("norm_g", "conv_b", "ln_g", "ln_b", "lb_logits", "gnorm_g", "final_g")
    rep_w = (norm_g, conv_b, ln_g, ln_b, lb_logits, gnorm_g, final_g2)
    rep_m = (m_norm_g, m_conv_b, m_ln_g, m_ln_b, m_lb_logits, m_gnorm_g, m_final_g.reshape(1, d))
    rep_v = (v_norm_g, v_conv_b, v_ln_g, v_ln_b, v_lb_logits, v_gnorm_g, v_final_g.reshape(1, d))
    rep = _adamw("adamw_replicated", jnp.concatenate(rep_w, 0), jnp.concatenate(rep_m, 0),
                 jnp.concatenate(rep_v, 0), small_sum[0:8])
    rep_rows = {"norm_g": (0, 1), "conv_b": (1, 2), "ln_g": (2, 3), "ln_b": (3, 4), "lb_logits": (4, 6),
                "gnorm_g": (6, 7), "final_g": (7, 8)}
    small_out = {}
    for nm in rep_names:
        lo, hi = rep_rows[nm]
        vals = tuple(a[lo:hi] for a in rep)
        if nm == "final_g":
            vals = tuple(a.reshape(d) for a in vals)
        small_out[nm] = vals
    cw_row = 8 + n_meta
    g_meta = lax.dynamic_slice_in_dim(small_sum[8:cw_row], chip * ds, ds, axis=1)
    small_out["meta_tokens"] = _adamw("adamw_meta", meta_tokens, m_meta_tokens, v_meta_tokens, g_meta)
    g_cw = lax.dynamic_slice_in_dim(small_sum[cw_row:cw_row + HALO], chip * ds, ds, axis=1)
    pad_rows = ((0, HALO - CONV_WIDTH), (0, 0))
    cw_res = _adamw("adamw_conv_w", conv_w_pad, jnp.pad(m_conv_w[0], pad_rows),
                    jnp.pad(v_conv_w[0], pad_rows, constant_values=1.0), g_cw)
    small_out["conv_w"] = tuple(a[:CONV_WIDTH][None] for a in cw_res)

    loss = small_sum[cw_row + HALO - 1, 0]

    order = ("meta_tokens", "norm_g", "w_in", "conv_w", "conv_b", "ln_g", "ln_b", "w_conv_out", "lb_logits",
             "gnorm_g", "w_rec_out", "w_out", "final_g")
    allres = {**big, **small_out}
    outs = [loss, grad_x]
    for field in range(4):
        outs.extend(allres[nm][field] for nm in order)
    return tuple(outs)
```

```python
import numpy as np

import jax
import jax.numpy as jnp
from jax import lax
from jax.experimental import pallas as pl
from jax.experimental.pallas import tpu as pltpu

F32 = jnp.float32
BF16 = jnp.bfloat16

EPS = 1e-6
CHUNK = 64
N_LEVELS = 6
FIRST_TABLE_LEVEL = 5
CONV_WIDTH = 31
HALO = 32
CONV_ROWS = 32
CONV_LANES = 256
SUBLANES = 8
HEAD = 128
W_IN_COL_TILES = 1
HEADS_PER_TRIP = 8
N_CHIPS = 4
VMEM_LIMIT_BYTES = 56 * 1024 * 1024
ELEMENTWISE_BLOCK_BYTES = 3 * 1024 * 1024

ADAM_LR = 0.001
ADAM_B1 = 0.9
ADAM_B2 = 0.999
ADAM_EPS = 1e-08
ADAM_WD = 0.01
ADAM_STEP = 10

MESH = pl.DeviceIdType.MESH
ANY = pl.BlockSpec(memory_space=pl.ANY)

NT = (((1,), (1,)), ((), ()))
TN = (((0,), (0,)), ((), ()))


def _params(**kw):
    return pltpu.CompilerParams(vmem_limit_bytes=VMEM_LIMIT_BYTES, **kw)


def _sigmoid(x):
    return jax.nn.sigmoid(x)


def _dsilu(x, s):
    return s * (1.0 + x * (1.0 - s))


def _row_tile(lp):
    for t in (320, 256, 192, 128, 64):
        if lp % t == 0:
            return t
    raise ValueError(f"unsupported padded length {lp}")


def _mm_row_tile(lp):
    for t in (832, 640, 320, 256, 192, 128, 64):
        if lp % t == 0:
            return t
    raise ValueError(f"unsupported padded length {lp}")


def _dot3(m_bf16, x):
    hi = x.astype(BF16)
    r1 = x - hi.astype(F32)
    mid = r1.astype(BF16)
    lo = (r1 - mid.astype(F32)).astype(BF16)
    return (jnp.dot(m_bf16, hi, preferred_element_type=F32)
            + jnp.dot(m_bf16, mid, preferred_element_type=F32)
            + jnp.dot(m_bf16, lo, preferred_element_type=F32))


def _dot2(m_bf16, x):
    hi = x.astype(BF16)
    lo = (x - hi.astype(F32)).astype(BF16)
    return (jnp.dot(m_bf16, hi, preferred_element_type=F32)
            + jnp.dot(m_bf16, lo, preferred_element_type=F32))


def _col_to_row(col):
    return jnp.broadcast_to(col, (HEAD, SUBLANES)).T[0:1, :]


def _row_to_col(row):
    return jnp.broadcast_to(row, (SUBLANES, HEAD)).T[:, 0:1]


def _hgrn_tables():
    t = np.arange(CHUNK)
    ltri = (t[None, :] <= t[:, None]).astype(np.float32)
    mats = [ltri]
    for lvl in range(FIRST_TABLE_LEVEL, N_LEVELS + 1):
        blk = CHUNK >> (lvl - 1)
        mid = (t // blk) * blk + blk // 2
        mats.append(ltri[mid - 1])
    after = (t[None, :] >= t[:, None]).astype(np.float32)
    before = (t[None, :] < t[:, None]).astype(np.float32)
    return jnp.asarray(np.concatenate(mats, 0), BF16), jnp.asarray(np.concatenate([after, before], 1), BF16)


def _rmsnorm_fwd(hres, g):
    lp, d = hres.shape
    tm = _mm_row_tile(lp)

    def body(x_ref, g_ref, h_ref):
        x = x_ref[...]
        r = lax.rsqrt(jnp.mean(x * x, axis=-1, keepdims=True) + EPS)
        h_ref[...] = (x * r * g_ref[...]).astype(BF16)

    return pl.pallas_call(
        body, grid=(lp // tm,),
        in_specs=[pl.BlockSpec((tm, d), lambda i: (i, 0)), pl.BlockSpec((1, d), lambda i: (0, 0))],
        out_specs=pl.BlockSpec((tm, d), lambda i: (i, 0)),
        out_shape=jax.ShapeDtypeStruct((lp, d), BF16),
        name="rmsnorm_fwd", compiler_params=_params())(hres, g)


def _conv_fwd(proj, conv_w, conv_b, ln_g, ln_b, w_conv):
    lp = proj.shape[0]
    d = conv_b.shape[1]
    tm = _row_tile(lp)
    hb = tm // HALO

    def body(ua_ref, ub_ref, z_ref, uap_ref, ubp_ref, cw_ref, cb_ref, lg_ref, lb_ref, w_ref,
             c_ref, ycin_ref, yconv_ref, aext_ref):
        i = pl.program_id(0)
        a_prev = uap_ref[...] * _sigmoid(ubp_ref[...])
        aext_ref[0:HALO, :] = jnp.where(i > 0, a_prev, 0.0)
        aext_ref[HALO:HALO + tm, :] = ua_ref[...] * _sigmoid(ub_ref[...])

        def row_block(r, carry):
            r0 = pl.multiple_of(r * CONV_ROWS, CONV_ROWS)
            for cs in range(d // CONV_LANES):
                cl = slice(cs * CONV_LANES, (cs + 1) * CONV_LANES)
                blk = aext_ref[pl.ds(r0, CONV_ROWS + HALO), cl]
                acc = jnp.zeros((CONV_ROWS, CONV_LANES), F32) + cb_ref[:, cl]
                for b in range(SUBLANES):
                    sh = blk if b == 0 else pltpu.roll(blk, CONV_ROWS + HALO - b, axis=0)
                    for a in range(5):
                        j = SUBLANES * a + b - 2
                        if 0 <= j < CONV_WIDTH:
                            acc = acc + cw_ref[j:j + 1, cl] * sh[SUBLANES * a:SUBLANES * a + CONV_ROWS, :]
                c_ref[pl.ds(r0, CONV_ROWS), cl] = acc
            return carry

        lax.fori_loop(0, tm // CONV_ROWS, row_block, 0)

        c = c_ref[...]
        mu = jnp.mean(c, axis=-1, keepdims=True)
        xc = c - mu
        rstd = lax.rsqrt(jnp.mean(xc * xc, axis=-1, keepdims=True) + EPS)
        ln = xc * rstd * lg_ref[...] + lb_ref[...]
        s = ln * _sigmoid(ln)
        z = z_ref[...]
        ycin = (s * (z * _sigmoid(z))).astype(BF16)
        ycin_ref[...] = ycin
        yconv_ref[...] = jnp.dot(ycin, w_ref[...], preferred_element_type=F32)

    row = lambda p: pl.BlockSpec((tm, d), lambda i, p=p: (i, p))
    halo = lambda p: pl.BlockSpec((HALO, d), lambda i, p=p: (jnp.maximum(i * hb - 1, 0), p))
    vec = pl.BlockSpec((1, d), lambda i: (0, 0))
    return pl.pallas_call(
        body, grid=(lp // tm,),
        in_specs=[row(0), row(1), row(2), halo(0), halo(1),
                  pl.BlockSpec((HALO, d), lambda i: (0, 0)), vec, vec, vec,
                  pl.BlockSpec((d, d), lambda i: (0, 0))],
        out_specs=[pl.BlockSpec((tm, d), lambda i: (i, 0))] * 3,
        out_shape=[jax.ShapeDtypeStruct((lp, d), F32), jax.ShapeDtypeStruct((lp, d), BF16),
                   jax.ShapeDtypeStruct((lp, d), F32)],
        scratch_shapes=[pltpu.VMEM((HALO + tm, d), F32)],
        name="conv_fwd", compiler_params=_params())(
            proj, proj, proj, proj, proj, conv_w, conv_b, ln_g, ln_b, w_conv)


def _lower_bound(lbl_ref):
    l0 = lbl_ref[0:1, :]
    l1 = lbl_ref[1:2, :]
    m = jnp.maximum(l0, l1)
    e0 = jnp.exp(l0 - m)
    e1 = jnp.exp(l1 - m)
    p0 = e0 / (e0 + e1)
    return p0, p0 * (e1 / (e0 + e1))


def _level_masks():
    r2 = lax.broadcasted_iota(jnp.int32, (CHUNK, CHUNK), 0)
    c2 = lax.broadcasted_iota(jnp.int32, (CHUNK, CHUNK), 1)
    out = []
    for lvl in range(1, N_LEVELS + 1):
        blk = CHUNK >> (lvl - 1)
        sh = blk.bit_length() - 1
        same = (r2 >> sh) == (c2 >> sh)
        t_upper = (r2 & (blk - 1)) >= (blk // 2)
        s_lower = (c2 & (blk - 1)) < (blk // 2)
        out.append(jnp.logical_and(same, jnp.logical_and(t_upper, s_lower)))
    return out


def _gates(qr, fr, lb, valid):
    sq = _sigmoid(qr)
    q = qr * sq
    sf = _sigmoid(fr)
    f = lb + (1.0 - lb) * sf
    g = jnp.where(valid, jnp.log(f), 0.0)
    k = jnp.where(valid, 1.0 - f, 0.0)
    return q, sq, f, sf, g, k


def _level_reference(lvl, b, t_ref, hs):
    if lvl >= FIRST_TABLE_LEVEL:
        base = CHUNK * (lvl - FIRST_TABLE_LEVEL + 1)
        return t_ref[base:base + CHUNK, hs]
    blk = CHUNK >> (lvl - 1)
    rows = [jnp.broadcast_to(b[m + blk // 2 - 1:m + blk // 2, :], (blk, HEAD)) for m in range(0, CHUNK, blk)]
    return rows[0] if len(rows) == 1 else jnp.concatenate(rows, axis=0)


def _level_factor(b, r):
    d = b - r
    return jnp.exp(jnp.minimum(d, -d))


def _hgrn_fwd(proj, lb_logits, n_pad):
    lp = proj.shape[0]
    d = lb_logits.shape[1]
    n_heads = d // HEAD
    nc = lp // CHUNK
    tab, _ = _hgrn_tables()
    n_tab = tab.shape[0]

    def body(qr_ref, fr_ref, ir_ref, lbl_ref, tab_ref, o_ref, sall_ref, s_ref, t_ref):
        n = pl.program_id(0)

        @pl.when(n == 0)
        def _():
            s_ref[...] = jnp.zeros_like(s_ref)

        sall_ref[0] = s_ref[...]
        lb_all, _ = _lower_bound(lbl_ref)
        rid = lax.broadcasted_iota(jnp.int32, (CHUNK, 1), 0)
        valid = jnp.logical_or(n > 0, rid >= n_pad)
        f_all = lb_all + (1.0 - lb_all) * _sigmoid(fr_ref[...])
        t_ref[...] = _dot2(tab_ref[...], jnp.where(valid, jnp.log(f_all), 0.0))
        masks = _level_masks()

        def head(h):
            off = h * HEAD if isinstance(h, int) else pl.multiple_of(h * HEAD, HEAD)
            hs = pl.ds(off, HEAD)
            lb = _lower_bound_slice(lbl_ref, hs)
            q, _, _, _, _, k = _gates(qr_ref[:, hs], fr_ref[:, hs], lb, valid)
            v = ir_ref[:, hs]
            b = t_ref[0:CHUNK, hs]
            s0 = s_ref[hs, :]
            o = jnp.dot((q * jnp.exp(b)).astype(BF16), s0.astype(BF16), preferred_element_type=F32)
            o = o + jnp.sum(q * k, axis=-1, keepdims=True) * v
            a = jnp.zeros((CHUNK, CHUNK), F32)
            for lvl in range(1, N_LEVELS + 1):
                e = _level_factor(b, _level_reference(lvl, b, t_ref, hs))
                p = lax.dot_general((q * e).astype(BF16), (k * e).astype(BF16), NT, preferred_element_type=F32)
                a = a + jnp.where(masks[lvl - 1], p, 0.0)
            vb = v.astype(BF16)
            o_ref[:, hs] = o + jnp.dot(a.astype(BF16), vb, preferred_element_type=F32)
            b_last = t_ref[CHUNK - 1:CHUNK, hs]
            khat = (k * jnp.exp(b_last - b)).astype(BF16)
            s_ref[hs, :] = _row_to_col(jnp.exp(b_last)) * s0 + lax.dot_general(khat, vb, TN, preferred_element_type=F32)
        per_trip = min(HEADS_PER_TRIP, n_heads)

        def head_group(p, carry):
            for u in range(per_trip):
                head(p * per_trip + u)
            return carry

        if n_heads == per_trip:
            head_group(0, 0)
        else:
            lax.fori_loop(0, n_heads // per_trip, head_group, 0)

    piece = lambda p: pl.BlockSpec((CHUNK, d), lambda n, p=p: (n, p))
    return pl.pallas_call(
        body, grid=(nc,),
        in_specs=[piece(3), piece(4), piece(5), pl.BlockSpec((2, d), lambda n: (0, 0)),
                  pl.BlockSpec((n_tab, CHUNK), lambda n: (0, 0))],
        out_specs=[pl.BlockSpec((CHUNK, d), lambda n: (n, 0)), pl.BlockSpec((1, d, HEAD), lambda n: (n, 0, 0))],
        out_shape=[jax.ShapeDtypeStruct((lp, d), F32), jax.ShapeDtypeStruct((nc, d, HEAD), F32)],
        scratch_shapes=[pltpu.VMEM((d, HEAD), F32), pltpu.VMEM((n_tab, d), F32)],
        name="hgrn_fwd", compiler_params=_params())(proj, proj, proj, lb_logits, tab)


def _lower_bound_slice(lbl_ref, hs):
    l0 = lbl_ref[0:1, hs]
    l1 = lbl_ref[1:2, hs]
    m = jnp.maximum(l0, l1)
    e0 = jnp.exp(l0 - m)
    e1 = jnp.exp(l1 - m)
    return e0 / (e0 + e1)


def _tail_fwd(o, proj, y_conv, hres, target, gnorm_g, final_g, w_rec, w_out):
    lp, d = o.shape
    n_heads = d // HEAD
    tm = _row_tile(lp)

    n_slabs = tm // CHUNK

    def body(o_ref, gr_ref, mc_ref, mr_ref, yc_ref, x_ref, gn_ref, fg_ref, wr_ref, wo_ref, *rest):
        t_refs = rest[:n_slabs]
        yrin_ref, mg_ref, yrec_ref, dout_ref, loss_ref, dfg_ref = rest[n_slabs:]
        i = pl.program_id(0)

        @pl.when(i == 0)
        def _():
            loss_ref[...] = jnp.zeros_like(loss_ref)
            dfg_ref[...] = jnp.zeros_like(dfg_ref)

        for h in range(n_heads):
            hs = slice(h * HEAD, (h + 1) * HEAD)
            oh = o_ref[:, hs]
            on = oh * lax.rsqrt(jnp.mean(oh * oh, axis=-1, keepdims=True) + EPS) * gn_ref[:, hs]
            gr = gr_ref[:, hs]
            yrin_ref[:, hs] = (on * (gr * _sigmoid(gr))).astype(BF16)
        yrec = jnp.dot(yrin_ref[...], wr_ref[...], preferred_element_type=F32)
        yrec_ref[...] = yrec
        merged = (_sigmoid(mc_ref[...]) * yc_ref[...] + _sigmoid(mr_ref[...]) * yrec).astype(BF16)
        mg_ref[...] = merged
        out = x_ref[...] + jnp.dot(merged, wo_ref[...], preferred_element_type=F32)
        r = lax.rsqrt(jnp.mean(out * out, axis=-1, keepdims=True) + EPS)
        yhat = out * r
        fg = fg_ref[...]
        rid = lax.broadcasted_iota(jnp.int32, (tm, 1), 0) + i * tm
        tgt = jnp.concatenate([t[...] for t in t_refs], axis=0)
        err = jnp.where(rid >= CHUNK, yhat * fg - tgt, 0.0)
        loss_ref[...] += 0.5 * jnp.sum(err * err) / d
        dy = err / d
        dfg_ref[...] += jnp.sum(dy * yhat, axis=0, keepdims=True)
        dyh = dy * fg
        dout_ref[...] = r * (dyh - yhat * jnp.mean(dyh * yhat, axis=-1, keepdims=True))

    row = lambda p: pl.BlockSpec((tm, d), lambda i, p=p: (i, p))
    vec = pl.BlockSpec((1, d), lambda i: (0, 0))
    mat = pl.BlockSpec((d, d), lambda i: (0, 0))
    return pl.pallas_call(
        body, grid=(lp // tm,),
        in_specs=[row(0), row(6), row(7), row(8), row(0), row(0), vec, vec, mat, mat]
        + [pl.BlockSpec((CHUNK, d), lambda i, u=u: (jnp.maximum(i * n_slabs + u - 1, 0), 0)) for u in range(n_slabs)],
        out_specs=[row(0), row(0), row(0), row(0), pl.BlockSpec((8, 128), lambda i: (0, 0)), vec],
        out_shape=[jax.ShapeDtypeStruct((lp, d), BF16), jax.ShapeDtypeStruct((lp, d), BF16),
                   jax.ShapeDtypeStruct((lp, d), F32), jax.ShapeDtypeStruct((lp, d), F32),
                   jax.ShapeDtypeStruct((8, 128), F32), jax.ShapeDtypeStruct((1, d), F32)],
        name="tail_fwd", compiler_params=_params())(
            o, proj, proj, proj, y_conv, hres, gnorm_g, final_g, w_rec, w_out, *([target] * n_slabs))


def _tail_bwd(dout, proj, y_conv, y_rec, o, c, w_out, w_rec, w_conv, ln_g, ln_b, gnorm_g):
    lp, d = dout.shape
    n_heads = d // HEAD
    tm = _row_tile(lp)

    def body(dout_ref, mc_ref, mr_ref, z_ref, gr_ref, yc_ref, yrec_ref, o_ref, c_ref,
             wo_ref, wr_ref, wc_ref, lg_ref, lb_ref, gn_ref,
             dyc_ref, dyr_ref, doutb_ref, dz_ref, dp_ref, do_ref, dc_ref,
             dgn_ref, dlg_ref, dlb_ref, dyrin_ref):
        i = pl.program_id(0)

        @pl.when(i == 0)
        def _():
            dgn_ref[...] = jnp.zeros_like(dgn_ref)
            dlg_ref[...] = jnp.zeros_like(dlg_ref)
            dlb_ref[...] = jnp.zeros_like(dlb_ref)

        doutb = dout_ref[...].astype(BF16)
        doutb_ref[...] = doutb
        dmerged = lax.dot_general(doutb, wo_ref[...], NT, preferred_element_type=F32)
        smc = _sigmoid(mc_ref[...])
        smr = _sigmoid(mr_ref[...])
        dyc = (dmerged * smc).astype(BF16)
        dyr = (dmerged * smr).astype(BF16)
        dyc_ref[...] = dyc
        dyr_ref[...] = dyr
        dp_ref[:, d:2 * d] = (dmerged * yc_ref[...] * smc * (1.0 - smc)).astype(BF16)
        dp_ref[:, 2 * d:3 * d] = (dmerged * yrec_ref[...] * smr * (1.0 - smr)).astype(BF16)

        dyrin_ref[...] = lax.dot_general(dyr, wr_ref[...], NT, preferred_element_type=F32)
        for h in range(n_heads):
            hs = slice(h * HEAD, (h + 1) * HEAD)
            oh = o_ref[:, hs]
            rstd = lax.rsqrt(jnp.mean(oh * oh, axis=-1, keepdims=True) + EPS)
            ohat = oh * rstd
            gn = gn_ref[:, hs]
            gr = gr_ref[:, hs]
            sg = _sigmoid(gr)
            dyrin = dyrin_ref[:, hs]
            don = dyrin * (gr * sg)
            dp_ref[:, hs] = (dyrin * (ohat * gn) * _dsilu(gr, sg)).astype(BF16)
            dgn_ref[:, hs] += jnp.sum(don * ohat, axis=0, keepdims=True)
            doh = don * gn
            do_ref[:, hs] = rstd * (doh - ohat * jnp.mean(doh * ohat, axis=-1, keepdims=True))

        dycin = lax.dot_general(dyc, wc_ref[...], NT, preferred_element_type=F32)
        c = c_ref[...]
        mu = jnp.mean(c, axis=-1, keepdims=True)
        xc = c - mu
        rstd = lax.rsqrt(jnp.mean(xc * xc, axis=-1, keepdims=True) + EPS)
        nrm = xc * rstd
        lg = lg_ref[...]
        ln = nrm * lg + lb_ref[...]
        sl = _sigmoid(ln)
        z = z_ref[...]
        sz = _sigmoid(z)
        dz_ref[...] = (dycin * (ln * sl) * _dsilu(z, sz)).astype(BF16)
        dln = dycin * (z * sz) * _dsilu(ln, sl)
        dlg_ref[...] += jnp.sum(dln * nrm, axis=0, keepdims=True)
        dlb_ref[...] += jnp.sum(dln, axis=0, keepdims=True)
        dn = dln * lg
        dc_ref[...] = rstd * (dn - jnp.mean(dn, axis=-1, keepdims=True)
                              - nrm * jnp.mean(dn * nrm, axis=-1, keepdims=True))

    row = lambda p: pl.BlockSpec((tm, d), lambda i, p=p: (i, p))
    vec = pl.BlockSpec((1, d), lambda i: (0, 0))
    mat = pl.BlockSpec((d, d), lambda i: (0, 0))
    act_bf = jax.ShapeDtypeStruct((lp, d), BF16)
    act_f32 = jax.ShapeDtypeStruct((lp, d), F32)
    vec_f32 = jax.ShapeDtypeStruct((1, d), F32)
    return pl.pallas_call(
        body, grid=(lp // tm,),
        in_specs=[row(0), row(7), row(8), row(2), row(6), row(0), row(0), row(0), row(0),
                  mat, mat, mat, vec, vec, vec],
        out_specs=[row(0)] * 4 + [pl.BlockSpec((tm, 3 * d), lambda i: (i, 2))] + [row(0)] * 2 + [vec] * 3,
        out_shape=[act_bf] * 4 + [jax.ShapeDtypeStruct((lp, 9 * d), BF16)] + [act_f32] * 2 + [vec_f32] * 3,
        scratch_shapes=[pltpu.VMEM((tm, d), F32)],
        name="tail_bwd", compiler_params=_params())(
            dout, proj, proj, proj, proj, y_conv, y_rec, o, c, w_out, w_rec, w_conv, ln_g, ln_b, gnorm_g)


def _hgrn_bwd(proj, do, s_all, lb_logits, n_pad, dproj):
    lp, d = do.shape
    n_heads = d // HEAD
    nc = lp // CHUNK
    tab, utri = _hgrn_tables()
    n_tab = tab.shape[0]

    def body(qr_ref, fr_ref, ir_ref, do_ref, s0_ref, lbl_ref, tab_ref, ut_ref, _,
             dp_ref, dlbl_ref, ds_ref, t_ref, dlb_ref):
        n = pl.program_id(0)
        chunk = nc - 1 - n

        @pl.when(n == 0)
        def _():
            ds_ref[...] = jnp.zeros_like(ds_ref)
            dlb_ref[...] = jnp.zeros_like(dlb_ref)

        lb_all, pp = _lower_bound(lbl_ref)
        rid = lax.broadcasted_iota(jnp.int32, (CHUNK, 1), 0)
        valid = jnp.logical_or(chunk > 0, rid >= n_pad)
        f_all = lb_all + (1.0 - lb_all) * _sigmoid(fr_ref[...])
        t_ref[...] = _dot2(tab_ref[...], jnp.where(valid, jnp.log(f_all), 0.0))
        masks = _level_masks()
        ut = ut_ref[...]

        def head(h):
            off = h * HEAD if isinstance(h, int) else pl.multiple_of(h * HEAD, HEAD)
            hs = pl.ds(off, HEAD)
            lb = _lower_bound_slice(lbl_ref, hs)
            qr = qr_ref[:, hs]
            q, sq, f, sf, _, k = _gates(qr, fr_ref[:, hs], lb, valid)
            v = ir_ref[:, hs]
            do_h = do_ref[:, hs]
            b = t_ref[0:CHUNK, hs]
            b_last = t_ref[CHUNK - 1:CHUNK, hs]
            s0 = s0_ref[0, hs, :]
            ds1 = ds_ref[hs, :]
            eb = jnp.exp(b)
            ekl = jnp.exp(b_last - b)
            do_bf = do_h.astype(BF16)
            v_bf = v.astype(BF16)
            ds1_bf = ds1.astype(BF16)

            da = lax.dot_general(do_bf, v_bf, NT, preferred_element_type=F32)
            da_diag = jnp.sum(do_h * v, axis=-1, keepdims=True)
            a = jnp.zeros((CHUNK, CHUNK), F32)
            dq_x = eb * lax.dot_general(do_bf, s0.astype(BF16), NT, preferred_element_type=F32)
            dk_x = ekl * lax.dot_general(v_bf, ds1_bf, NT, preferred_element_type=F32)
            x_after = q * dq_x
            x_before = k * dk_x
            for lvl in range(1, N_LEVELS + 1):
                e = _level_factor(b, _level_reference(lvl, b, t_ref, hs))
                qt = (q * e).astype(BF16)
                kt = (k * e).astype(BF16)
                p = lax.dot_general(qt, kt, NT, preferred_element_type=F32)
                a = a + jnp.where(masks[lvl - 1], p, 0.0)
                dam = jnp.where(masks[lvl - 1], da, 0.0).astype(BF16)
                dqt = jnp.dot(dam, kt, preferred_element_type=F32)
                dkt = lax.dot_general(dam, qt, TN, preferred_element_type=F32)
                dq_x = dq_x + e * dqt
                dk_x = dk_x + e * dkt
                x_after = x_after + (qt.astype(F32) * dqt - kt.astype(F32) * dkt)

            dv = (lax.dot_general(a.astype(BF16), do_bf, TN, preferred_element_type=F32)
                  + jnp.sum(q * k, axis=-1, keepdims=True) * do_h
                  + jnp.dot((k * ekl).astype(BF16), ds1_bf, preferred_element_type=F32))
            dp_ref[:, pl.ds(2 * d + off, HEAD)] = dv.astype(BF16)

            carried = jnp.exp(b_last) * _col_to_row(jnp.sum(s0 * ds1, axis=-1, keepdims=True))
            dg = _dot3(ut, jnp.concatenate([x_after, x_before], axis=0)) + carried
            dq = dq_x + da_diag * k
            dk = dk_x + da_diag * q
            dp_ref[:, hs] = (dq * _dsilu(qr, sq)).astype(BF16)
            df = jnp.where(valid, dg / f - dk, 0.0)
            dp_ref[:, pl.ds(d + off, HEAD)] = (df * (1.0 - lb) * sf * (1.0 - sf)).astype(BF16)
            dlb_ref[:, hs] += jnp.sum(df * (1.0 - sf), axis=0, keepdims=True)

            ds_ref[hs, :] = (_row_to_col(jnp.exp(b_last)) * ds1
                             + lax.dot_general((q * eb).astype(BF16), do_bf, TN, preferred_element_type=F32))
        per_trip = min(HEADS_PER_TRIP, n_heads)

        def head_group(p, carry):
            for u in range(per_trip):
                head(p * per_trip + u)
            return carry

        if n_heads == per_trip:
            head_group(0, 0)
        else:
            lax.fori_loop(0, n_heads // per_trip, head_group, 0)

        @pl.when(n == nc - 1)
        def _():
            dl0 = dlb_ref[...] * pp
            dlbl_ref[0:1, :] = dl0
            dlbl_ref[1:2, :] = -dl0

    piece = lambda p: pl.BlockSpec((CHUNK, d), lambda n, p=p: (nc - 1 - n, p))
    return pl.pallas_call(
        body, grid=(nc,),
        in_specs=[piece(3), piece(4), piece(5), piece(0),
                  pl.BlockSpec((1, d, HEAD), lambda n: (nc - 1 - n, 0, 0)),
                  pl.BlockSpec((2, d), lambda n: (0, 0)),
                  pl.BlockSpec((n_tab, CHUNK), lambda n: (0, 0)),
                  pl.BlockSpec((CHUNK, 2 * CHUNK), lambda n: (0, 0)), ANY],
        out_specs=[pl.BlockSpec((CHUNK, 3 * d), lambda n: (nc - 1 - n, 1)), pl.BlockSpec((2, d), lambda n: (0, 0))],
        out_shape=[jax.ShapeDtypeStruct(dproj.shape, BF16), jax.ShapeDtypeStruct((2, d), F32)],
        input_output_aliases={8: 0},
        scratch_shapes=[pltpu.VMEM((d, HEAD), F32), pltpu.VMEM((n_tab, d), F32), pltpu.VMEM((1, d), F32)],
        name="hgrn_bwd", compiler_params=_params())(proj, proj, proj, do, s_all, lb_logits, tab, utri, dproj)


def _conv_bwd(dc, proj, conv_w, dz, dproj):
    lp, d = dc.shape
    tm = _row_tile(lp)
    hb = tm // HALO
    n_tiles = lp // tm
    last_halo = lp // HALO - 1

    def body(dc_ref, dcn_ref, ua_ref, ub_ref, uap_ref, ubp_ref, cw_ref, dz_ref, _,
             dp_ref, dcw_ref, dcb_ref, aext_ref, dcext_ref, da_ref, dcw_acc):
        i = pl.program_id(0)

        @pl.when(i == 0)
        def _():
            dcw_acc[...] = jnp.zeros_like(dcw_acc)
            dcb_ref[...] = jnp.zeros_like(dcb_ref)

        ua = ua_ref[...]
        sb = _sigmoid(ub_ref[...])
        a_prev = uap_ref[...] * _sigmoid(ubp_ref[...])
        aext_ref[0:HALO, :] = jnp.where(i > 0, a_prev, 0.0)
        aext_ref[HALO:HALO + tm, :] = ua * sb
        dcext_ref[0:tm, :] = dc_ref[...]
        dcext_ref[tm:tm + HALO, :] = jnp.where(i < n_tiles - 1, dcn_ref[...], 0.0)
        dcb_ref[...] += jnp.sum(dc_ref[...], axis=0, keepdims=True)

        def row_block(r, carry):
            r0 = pl.multiple_of(r * CONV_ROWS, CONV_ROWS)
            n_rows = CONV_ROWS + HALO
            for cs in range(d // CONV_LANES):
                cl = slice(cs * CONV_LANES, (cs + 1) * CONV_LANES)
                dblk = dcext_ref[pl.ds(r0, n_rows), cl]
                ablk = aext_ref[pl.ds(r0, n_rows), cl]
                dcur = dblk[0:CONV_ROWS, :]
                acc = jnp.zeros((CONV_ROWS, CONV_LANES), F32)
                for b in range(SUBLANES):
                    dsh = dblk if b == 0 else pltpu.roll(dblk, n_rows - b, axis=0)
                    ash = ablk if b == 0 else pltpu.roll(ablk, n_rows - b, axis=0)
                    for a in range(5):
                        j_da = CONV_WIDTH - 1 - (SUBLANES * a + b)
                        if 0 <= j_da < CONV_WIDTH:
                            acc = acc + cw_ref[j_da:j_da + 1, cl] * dsh[SUBLANES * a:SUBLANES * a + CONV_ROWS, :]
                        j_w = SUBLANES * a + b - 2
                        if 0 <= j_w < CONV_WIDTH:
                            prod = dcur * ash[SUBLANES * a:SUBLANES * a + CONV_ROWS, :]
                            dcw_acc[j_w, :, cl] += prod.reshape(CONV_ROWS // SUBLANES, SUBLANES, CONV_LANES).sum(axis=0)
                da_ref[pl.ds(r0, CONV_ROWS), cl] = acc
            return carry

        lax.fori_loop(0, tm // CONV_ROWS, row_block, 0)

        da = da_ref[...]
        dp_ref[:, 0:d] = (da * sb).astype(BF16)
        dp_ref[:, d:2 * d] = (da * ua * sb * (1.0 - sb)).astype(BF16)
        dp_ref[:, 2 * d:3 * d] = dz_ref[...]

        @pl.when(i == n_tiles - 1)
        def _():
            dcw_ref[...] = jnp.sum(dcw_acc[...], axis=1)

    row = lambda p: pl.BlockSpec((tm, d), lambda i, p=p: (i, p))
    prev = lambda p: pl.BlockSpec((HALO, d), lambda i, p=p: (jnp.maximum(i * hb - 1, 0), p))
    nxt = pl.BlockSpec((HALO, d), lambda i: (jnp.minimum((i + 1) * hb, last_halo), 0))
    return pl.pallas_call(
        body, grid=(n_tiles,),
        in_specs=[row(0), nxt, row(0), row(1), prev(0), prev(1), pl.BlockSpec((HALO, d), lambda i: (0, 0)),
                  row(0), ANY],
        out_specs=[pl.BlockSpec((tm, 3 * d), lambda i: (i, 0)), pl.BlockSpec((HALO, d), lambda i: (0, 0)),
                   pl.BlockSpec((1, d), lambda i: (0, 0))],
        out_shape=[jax.ShapeDtypeStruct(dproj.shape, BF16),
                   jax.ShapeDtypeStruct((HALO, d), F32), jax.ShapeDtypeStruct((1, d), F32)],
        input_output_aliases={8: 0},
        scratch_shapes=[pltpu.VMEM((HALO + tm, d), F32), pltpu.VMEM((tm + HALO, d), F32), pltpu.VMEM((tm, d), F32),
                        pltpu.VMEM((HALO, SUBLANES, d), F32)],
        name="conv_bwd", compiler_params=_params())(dc, dc, proj, proj, proj, proj, conv_w, dz, dproj)


def _weight_grad(xs, dy, name, blocked):
    lp, dx = xs.shape
    n = dy.shape[1]
    tk = _mm_row_tile(lp)
    if blocked:
        ncol = n // N_CHIPS
        nt = W_IN_COL_TILES
        tn = ncol // nt
        grid = (N_CHIPS * nt, lp // tk)
        out_spec = pl.BlockSpec((1, dx, tn), lambda c, k: (c // nt, 0, c % nt))
        out_shape = jax.ShapeDtypeStruct((N_CHIPS, dx, ncol), F32)
    else:
        tn = n
        grid = (1, lp // tk)
        out_spec = pl.BlockSpec((dx, tn), lambda c, k: (0, c))
        out_shape = jax.ShapeDtypeStruct((dx, n), F32)

    def body(xs_ref, dy_ref, o_ref, *copy_ref):
        @pl.when(pl.program_id(1) == 0)
        def _():
            o_ref[...] = jnp.zeros_like(o_ref)

        p = lax.dot_general(xs_ref[...], dy_ref[...], TN, preferred_element_type=F32)
        if blocked:
            o_ref[0] += p

            @pl.when(pl.program_id(1) == lp // tk - 1)
            def _():
                copy_ref[0][0] = o_ref[0].astype(BF16)
        else:
            o_ref[...] += p

    if blocked:
        out_spec = [out_spec, out_spec]
        out_shape = [out_shape, jax.ShapeDtypeStruct(out_shape.shape, BF16)]
    return pl.pallas_call(
        body, grid=grid,
        in_specs=[pl.BlockSpec((tk, dx), lambda c, k: (k, 0)), pl.BlockSpec((tk, tn), lambda c, k: (k, c))],
        out_specs=out_spec, out_shape=out_shape,
        name=name, compiler_params=_params())(xs, dy)


def _in_proj_bwd(dproj, wtg, hres, norm_g, dout):
    lp, d = hres.shape
    _, ncol, _ = wtg.shape
    tm = _mm_row_tile(lp)
    nt = W_IN_COL_TILES
    tn = ncol // nt
    nk = N_CHIPS * nt

    def body(dp_ref, w_ref, x_ref, g_ref, dout_ref, dx_ref, dg_ref, acc_ref):
        i = pl.program_id(0)
        kk = pl.program_id(1)

        @pl.when(jnp.logical_and(i == 0, kk == 0))
        def _():
            dg_ref[...] = jnp.zeros_like(dg_ref)

        @pl.when(kk == 0)
        def _():
            acc_ref[...] = jnp.zeros_like(acc_ref)

        acc_ref[...] += jnp.dot(dp_ref[...], w_ref[0], preferred_element_type=F32)

        @pl.when(kk == nk - 1)
        def _():
            x = x_ref[...]
            r = lax.rsqrt(jnp.mean(x * x, axis=-1, keepdims=True) + EPS)
            xhat = x * r
            dh = acc_ref[...]
            dg_ref[...] += jnp.sum(dh * xhat, axis=0, keepdims=True)
            dxh = dh * g_ref[...]
            dx_ref[...] = dout_ref[...] + r * (dxh - xhat * jnp.mean(dxh * xhat, axis=-1, keepdims=True))

    return pl.pallas_call(
        body, grid=(lp // tm, nk),
        in_specs=[pl.BlockSpec((tm, tn), lambda i, k: (i, k)),
                  pl.BlockSpec((1, tn, d), lambda i, k: (k // nt, k % nt, 0)),
                  pl.BlockSpec((tm, d), lambda i, k: (i, 0)),
                  pl.BlockSpec((1, d), lambda i, k: (0, 0)),
                  pl.BlockSpec((tm, d), lambda i, k: (i, 0))],
        out_specs=[pl.BlockSpec((tm, d), lambda i, k: (i, 0)), pl.BlockSpec((1, d), lambda i, k: (0, 0))],
        out_shape=[jax.ShapeDtypeStruct((lp, d), F32), jax.ShapeDtypeStruct((1, d), F32)],
        scratch_shapes=[pltpu.VMEM((tm, d), F32)],
        name="in_proj_bwd", compiler_params=_params())(dproj, wtg, hres, norm_g, dout)


def _adamw_math(w, g, m, v):
    m = ADAM_B1 * m + (1.0 - ADAM_B1) * g
    v = ADAM_B2 * v + (1.0 - ADAM_B2) * (g * g)
    m_hat = m / (1.0 - ADAM_B1 ** ADAM_STEP)
    v_hat = v / (1.0 - ADAM_B2 ** ADAM_STEP)
    delta = -ADAM_LR * (m_hat / (jnp.sqrt(v_hat) + ADAM_EPS) + ADAM_WD * w)
    return delta, m, v


def _elementwise_rows(shape):
    r, c = shape
    for t in (256, 128, 64, 32, 16, 8):
        if r % t == 0 and r > t and t * c * 4 <= ELEMENTWISE_BLOCK_BYTES:
            return t
    return r


def _adamw(name, w, m, v, *g_parts):
    shape = w.shape
    tr = _elementwise_rows(shape)
    n_g = len(g_parts)

    def body(*refs):
        w_ref, m_ref, v_ref = refs[:3]
        g_refs = refs[3:3 + n_g]
        g_out, d_out, m_out, v_out = refs[3 + n_g:]
        g = g_refs[0][...]
        for gr in g_refs[1:]:
            g = g + gr[...]
        delta, m_new, v_new = _adamw_math(w_ref[...], g, m_ref[...], v_ref[...])
        g_out[...] = g
        d_out[...] = delta
        m_out[...] = m_new
        v_out[...] = v_new

    spec = pl.BlockSpec((tr, shape[1]), lambda i: (i, 0))
    return pl.pallas_call(
        body, grid=(shape[0] // tr,),
        in_specs=[spec] * (3 + n_g), out_specs=[spec] * 4,
        out_shape=[jax.ShapeDtypeStruct(shape, F32)] * 4,
        name=name, compiler_params=_params())(w, m, v, *g_parts)


def _chip_half_sum(name, g, recv, core):
    _, _, hr, cols = g.shape
    tr = _elementwise_rows((hr, cols))

    def body(core_ref, g_ref, r_ref, o_ref, ob_ref):
        s = g_ref[0, 0] + r_ref[0].astype(F32)
        o_ref[0] = s
        ob_ref[0] = s.astype(BF16)

    blk = pl.BlockSpec((1, tr, cols), lambda j, i, core_ref: (j, i, 0))
    grid_spec = pltpu.PrefetchScalarGridSpec(
        num_scalar_prefetch=1, grid=(N_CHIPS, hr // tr),
        in_specs=[pl.BlockSpec((1, 1, tr, cols), lambda j, i, core_ref: (j, core_ref[0], i, 0)), blk],
        out_specs=[blk, blk])
    return pl.pallas_call(
        body, grid_spec=grid_spec,
        out_shape=[jax.ShapeDtypeStruct((N_CHIPS, hr, cols), F32), jax.ShapeDtypeStruct((N_CHIPS, hr, cols), BF16)],
        name=name, compiler_params=_params())(core, g, recv)


def _block_half_total(name, chip_sums, recv, chip_core, after):
    _, hr, cols = chip_sums.shape
    tr = _elementwise_rows((hr, cols))

    def body(cc_ref, p_ref, r_ref, after_ref, o_ref):
        s = p_ref[0]
        for k in range(3):
            s = s + r_ref[k].astype(F32)
        o_ref[0] = s

    grid_spec = pltpu.PrefetchScalarGridSpec(
        num_scalar_prefetch=1, grid=(hr // tr,),
        in_specs=[pl.BlockSpec((1, tr, cols), lambda i, cc_ref: (cc_ref[0], i, 0)),
                  pl.BlockSpec((3, tr, cols), lambda i, cc_ref: (0, i, 0)), ANY],
        out_specs=pl.BlockSpec((1, tr, cols), lambda i, cc_ref: (cc_ref[1], i, 0)))
    return pl.pallas_call(
        body, grid_spec=grid_spec, out_shape=jax.ShapeDtypeStruct((2, hr, cols), F32),
        name=name, compiler_params=_params())(chip_core, chip_sums, recv, after)


def _place_shard(name, w, chip, dtype):
    r, c = w.shape
    tr = _elementwise_rows((r, c))

    def body(chip_ref, w_ref, o_ref):
        o_ref[0] = w_ref[...].astype(dtype)

    grid_spec = pltpu.PrefetchScalarGridSpec(
        num_scalar_prefetch=1, grid=(r // tr,),
        in_specs=[pl.BlockSpec((tr, c), lambda i, chip_ref: (i, 0))],
        out_specs=pl.BlockSpec((1, tr, c), lambda i, chip_ref: (chip_ref[0], i, 0)))
    return pl.pallas_call(
        body, grid_spec=grid_spec, out_shape=jax.ShapeDtypeStruct((N_CHIPS, r, c), dtype),
        name=name, compiler_params=_params())(chip, w)


def _sum_slots(name, slots, own, my_idx):
    k, r, c = slots.shape

    def body(idx_ref, s_ref, own_ref, o_ref):
        s = None
        for j in range(k):
            term = jnp.where(idx_ref[0] == j, own_ref[...], s_ref[j])
            s = term if s is None else s + term
        o_ref[...] = s

    grid_spec = pltpu.PrefetchScalarGridSpec(
        num_scalar_prefetch=1, grid=(1,),
        in_specs=[pl.BlockSpec((k, r, c), lambda i, idx_ref: (0, 0, 0)),
                  pl.BlockSpec((r, c), lambda i, idx_ref: (0, 0))],
        out_specs=pl.BlockSpec((r, c), lambda i, idx_ref: (0, 0)))
    return pl.pallas_call(body, grid_spec=grid_spec, out_shape=jax.ShapeDtypeStruct((r, c), F32), name=name,
                          compiler_params=_params())(my_idx, slots, own)


def _mesh_pos():
    return lax.axis_index("x"), lax.axis_index("y"), lax.axis_index("c")


def _other_chips(x, y):
    return [(1 - x, y), (x, 1 - y), (1 - x, 1 - y)]


def _gather_weights(bufs):
    n = len(bufs)
    half = [b.shape[1] // 2 for b in bufs]

    def body(*refs):
        gathered = refs[n:2 * n]
        ici_send, ici_recv, d2d_send, d2d_recv = refs[2 * n:]
        x, y, c = _mesh_pos()
        me = 2 * x + y
        chips = _other_chips(x, y)

        def part(a, block, core):
            return gathered[a].at[block, pl.ds(core * half[a], half[a])]

        def over_ici(a, k, block):
            px, py = chips[k]
            return pltpu.make_async_remote_copy(
                src_ref=part(a, block, c), dst_ref=part(a, block, c),
                send_sem=ici_send.at[a, k], recv_sem=ici_recv.at[a, k],
                device_id=(px, py, c), device_id_type=MESH)

        def over_d2d(a, k, core):
            px, py = chips[k]
            return pltpu.make_async_remote_copy(
                src_ref=part(a, 2 * px + py, core), dst_ref=part(a, 2 * px + py, core),
                send_sem=d2d_send.at[a, k], recv_sem=d2d_recv.at[a, k],
                device_id=(x, y, 1 - c), device_id_type=MESH)

        for a in range(n):
            for k in range(3):
                over_ici(a, k, me).start()
        for a in range(n):
            for k, (px, py) in enumerate(chips):
                over_ici(a, k, 2 * px + py).wait_recv()
                over_d2d(a, k, c).start()
        for a in range(n):
            for k in range(3):
                over_d2d(a, k, 1 - c).wait_recv()
        for a in range(n):
            for k in range(3):
                over_ici(a, k, me).wait_send()
                over_d2d(a, k, c).wait_send()

    return pl.pallas_call(
        body, in_specs=[ANY] * n, out_specs=[ANY] * n,
        out_shape=[jax.ShapeDtypeStruct(b.shape, b.dtype) for b in bufs],
        input_output_aliases={a: a for a in range(n)},
        scratch_shapes=[pltpu.SemaphoreType.DMA((n, 3))] * 4,
        name="gather_weights")(*bufs)


def _gather_in_proj(h, bufs, order):
    n = len(bufs)
    half = [b.shape[1] // 2 for b in bufs]
    lp, d = h.shape
    ncol = bufs[0].shape[2]
    tm = _mm_row_tile(lp)
    n_row = lp // tm

    def body(order_ref, h_ref, *refs):
        gathered = refs[n:2 * n]
        o_ref, wt_ref = refs[2 * n], refs[2 * n + 1]
        w_buf, ici_send, ici_recv, d2d_send, d2d_recv, w_sem = refs[2 * n + 2:]
        j = pl.program_id(0)
        i = pl.program_id(1)
        x, y, c = _mesh_pos()
        me = 2 * x + y
        chips = _other_chips(x, y)

        def part(a, block, core):
            return gathered[a].at[block, pl.ds(core * half[a], half[a])]

        def over_ici(a, k, block):
            px, py = chips[k]
            return pltpu.make_async_remote_copy(
                src_ref=part(a, block, c), dst_ref=part(a, block, c),
                send_sem=ici_send.at[a, k], recv_sem=ici_recv.at[a, k],
                device_id=(px, py, c), device_id_type=MESH)

        def over_d2d(a, k, core):
            px, py = chips[k]
            return pltpu.make_async_remote_copy(
                src_ref=part(a, 2 * px + py, core), dst_ref=part(a, 2 * px + py, core),
                send_sem=d2d_send.at[a, k], recv_sem=d2d_recv.at[a, k],
                device_id=(x, y, 1 - c), device_id_type=MESH)

        @pl.when(jnp.logical_and(j == 0, i == 0))
        def _():
            for a in range(n):
                for k in range(2):
                    over_ici(a, k, me).start()

        for k, (px, py) in enumerate(chips):
            @pl.when(jnp.logical_and(j == k + 1, i == 0))
            def _(k=k, px=px, py=py):
                for a in range(n):
                    over_ici(a, k, 2 * px + py).wait_recv()
                    over_d2d(a, k, c).start()
                if k == 0:
                    for a in range(n):
                        over_ici(a, 2, me).start()
                for a in range(n):
                    over_d2d(a, k, 1 - c).wait_recv()

        @pl.when(i == 0)
        def _():
            load = pltpu.make_async_copy(gathered[0].at[order_ref[j]], w_buf, w_sem)
            load.start()
            load.wait()
            wt_ref[0] = w_buf[...].T

        o_ref[...] = jnp.dot(h_ref[...], w_buf[...], preferred_element_type=F32)

        @pl.when(jnp.logical_and(j == N_CHIPS - 1, i == n_row - 1))
        def _():
            for a in range(n):
                for k in range(3):
                    over_ici(a, k, me).wait_send()
                    over_d2d(a, k, c).wait_send()

    grid_spec = pltpu.PrefetchScalarGridSpec(
        num_scalar_prefetch=1, grid=(N_CHIPS, n_row),
        in_specs=[pl.BlockSpec((tm, d), lambda j, i, order_ref: (i, 0))] + [ANY] * n,
        out_specs=[ANY] * n + [pl.BlockSpec((tm, ncol), lambda j, i, order_ref: (i, order_ref[j])),
                               pl.BlockSpec((1, ncol, d), lambda j, i, order_ref: (order_ref[j], 0, 0))],
        scratch_shapes=[pltpu.VMEM((d, ncol), BF16)] + [pltpu.SemaphoreType.DMA((n, 3))] * 4
        + [pltpu.SemaphoreType.DMA])
    out = pl.pallas_call(
        body, grid_spec=grid_spec,
        out_shape=[jax.ShapeDtypeStruct(b.shape, b.dtype) for b in bufs]
        + [jax.ShapeDtypeStruct((lp, N_CHIPS * ncol), F32), jax.ShapeDtypeStruct((N_CHIPS, ncol, d), BF16)],
        input_output_aliases={2 + a: a for a in range(n)},
        name="gather_in_proj", compiler_params=_params())(order, h, *bufs)
    return out[n], out[n + 1], out[:n]


def _send_other_halves(grads, tag):
    n = len(grads)

    def body(*refs):
        srcs = refs[:n]
        dsts = refs[n:2 * n]
        send_sems, recv_sems = refs[2 * n:]
        x, y, c = _mesh_pos()
        copies = [pltpu.make_async_remote_copy(
            src_ref=srcs[a].at[j, 1 - c], dst_ref=dsts[a].at[j], send_sem=send_sems.at[a, j],
            recv_sem=recv_sems.at[a, j], device_id=(x, y, 1 - c), device_id_type=MESH)
            for a in range(n) for j in range(N_CHIPS)]
        for cp in copies:
            cp.start()
        for cp in copies:
            cp.wait()

    return pl.pallas_call(
        body, in_specs=[ANY] * n, out_specs=[ANY] * n,
        out_shape=[jax.ShapeDtypeStruct((N_CHIPS,) + g.shape[2:], g.dtype) for g in grads],
        scratch_shapes=[pltpu.SemaphoreType.DMA((n, N_CHIPS))] * 2,
        name="send_other_halves_" + tag)(*grads)


HBM = pl.BlockSpec(memory_space=pltpu.HBM)
SEM = pl.BlockSpec(memory_space=pltpu.SEMAPHORE)


def _block_copies(n, srcs, dsts, send_sems, recv_sems):
    x, y, c = _mesh_pos()
    return [pltpu.make_async_remote_copy(
        src_ref=srcs[a].at[2 * px + py], dst_ref=dsts[a].at[k], send_sem=send_sems.at[3 * a + k],
        recv_sem=recv_sems.at[3 * a + k], device_id=(px, py, c), device_id_type=MESH)
        for a in range(n) for k, (px, py) in enumerate(_other_chips(x, y))]


def _exchange_start(blocked, tag):
    n = len(blocked)
    lands = [lax.empty((3,) + b.shape[1:], b.dtype) for b in blocked]
    bufs = [pltpu.with_memory_space_constraint(b, pltpu.HBM) for b in list(blocked) + lands]
    nb = 2 * n

    def body(*refs):
        for cp in _block_copies(n, refs[:n], refs[n:nb], refs[nb], refs[nb + 1]):
            cp.start()
        refs[-1][...] = jnp.zeros_like(refs[-1])

    out = pl.pallas_call(
        body, name="exchange_start_" + tag,
        in_specs=[HBM] * nb,
        out_shape=[pltpu.SemaphoreType.DMA((3 * n,)), pltpu.SemaphoreType.DMA((3 * n,))]
        + [pltpu.HBM(b.shape, b.dtype) for b in bufs] + [jax.ShapeDtypeStruct((8, 128), F32)],
        out_specs=[SEM] * 2 + [HBM] * nb + [pl.BlockSpec(memory_space=pltpu.VMEM)],
        input_output_aliases={i: 2 + i for i in range(nb)},
        compiler_params=pltpu.CompilerParams(has_side_effects=pltpu.SideEffectType.DATAFLOW_SIDE_EFFECTING),
    )(*bufs)
    return (out[:2], out[2:2 + nb]), out[-1]


def _exchange_wait(state, after, tag):
    sems, bufs = state
    nb = len(bufs)
    n = nb // 2

    def body(*refs):
        for cp in _block_copies(n, refs[:n], refs[n:nb], refs[nb], refs[nb + 1]):
            cp.wait_send()
            cp.wait_recv()

    out = pl.pallas_call(
        body, name="exchange_wait_" + tag,
        in_specs=[HBM] * nb + [SEM] * 2 + [ANY],
        out_shape=[pltpu.HBM(b.shape, b.dtype) for b in bufs],
        out_specs=[HBM] * nb,
        input_output_aliases={i: i for i in range(nb)},
        compiler_params=pltpu.CompilerParams(has_side_effects=pltpu.SideEffectType.DATAFLOW_SIDE_EFFECTING),
    )(*bufs, *sems, after)
    return out[n:nb]


def _whole_block_copies(buf, send_sems, recv_sems, incoming):
    x, y, c = _mesh_pos()
    me = 2 * x + y
    out = []
    for k, (px, py) in enumerate(_other_chips(x, y)):
        block = 2 * px + py if incoming else me
        out.append(pltpu.make_async_remote_copy(
            src_ref=buf.at[block], dst_ref=buf.at[block], send_sem=send_sems.at[k], recv_sem=recv_sems.at[k],
            device_id=(px, py, c), device_id_type=MESH))
    return out


def _gather_start(buf, after, tag):
    buf = pltpu.with_memory_space_constraint(buf, pltpu.HBM)

    def body(buf_ref, after_ref, send_sems, recv_sems, thru_ref, token):
        for cp in _whole_block_copies(buf_ref, send_sems, recv_sems, incoming=False):
            cp.start()
        token[...] = jnp.zeros_like(token)

    out = pl.pallas_call(
        body, name="gather_start_" + tag,
        in_specs=[HBM, ANY],
        out_shape=[pltpu.SemaphoreType.DMA((3,)), pltpu.SemaphoreType.DMA((3,)), pltpu.HBM(buf.shape, buf.dtype),
                   jax.ShapeDtypeStruct((8, 128), F32)],
        out_specs=[SEM, SEM, HBM, pl.BlockSpec(memory_space=pltpu.VMEM)],
        input_output_aliases={0: 2},
        compiler_params=pltpu.CompilerParams(has_side_effects=pltpu.SideEffectType.DATAFLOW_SIDE_EFFECTING),
    )(buf, after)
    return out[:3], out[3]


def _gather_wait(state, after, tag):
    send_sems, recv_sems, buf = state

    def body(buf_ref, send_ref, recv_ref, after_ref, out_ref):
        for cp in _whole_block_copies(buf_ref, send_ref, recv_ref, incoming=True):
            cp.wait_send()
            cp.wait_recv()

    return pl.pallas_call(
        body, name="gather_wait_" + tag,
        in_specs=[HBM, SEM, SEM, ANY],
        out_shape=pltpu.HBM(buf.shape, buf.dtype), out_specs=HBM,
        input_output_aliases={0: 0},
        compiler_params=pltpu.CompilerParams(has_side_effects=pltpu.SideEffectType.DATAFLOW_SIDE_EFFECTING),
    )(buf, send_sems, recv_sems, after)


def _small_copies(small_ref, slots_ref, send_sems, recv_sems, incoming):
    x, y, c = _mesh_pos()
    out = []
    for r in range(1, 8):
        px = 1 - x if r & 4 else x
        py = 1 - y if r & 2 else y
        pc = 1 - c if r & 1 else c
        slot = 4 * px + 2 * py + pc if incoming else 4 * x + 2 * y + c
        out.append(pltpu.make_async_remote_copy(
            src_ref=small_ref, dst_ref=slots_ref.at[slot], send_sem=send_sems.at[r - 1],
            recv_sem=recv_sems.at[r - 1], device_id=(px, py, pc), device_id_type=MESH))
    return out


def _small_start(small):
    bufs = [pltpu.with_memory_space_constraint(b, pltpu.HBM)
            for b in (small, lax.empty((8,) + small.shape, small.dtype))]

    def body(small_ref, slots_ref, send_sems, recv_sems, small_thru, slots_thru, token):
        for cp in _small_copies(small_ref, slots_ref, send_sems, recv_sems, incoming=False):
            cp.start()
        token[...] = jnp.zeros_like(token)

    out = pl.pallas_call(
        body, name="small_start",
        in_specs=[HBM, HBM],
        out_shape=[pltpu.SemaphoreType.DMA((7,)), pltpu.SemaphoreType.DMA((7,))]
        + [pltpu.HBM(b.shape, b.dtype) for b in bufs] + [jax.ShapeDtypeStruct((8, 128), F32)],
        out_specs=[SEM, SEM, HBM, HBM, pl.BlockSpec(memory_space=pltpu.VMEM)],
        input_output_aliases={0: 2, 1: 3},
        compiler_params=pltpu.CompilerParams(has_side_effects=pltpu.SideEffectType.DATAFLOW_SIDE_EFFECTING),
    )(*bufs)
    return out[:4], out[4]


def _small_wait(state, after):
    send_sems, recv_sems, small, slots = state

    def body(small_ref, slots_ref, send_ref, recv_ref, after_ref, small_out, slots_out):
        for cp in _small_copies(small_ref, slots_ref, send_ref, recv_ref, incoming=True):
            cp.wait_send()
            cp.wait_recv()

    return pl.pallas_call(
        body, name="small_wait",
        in_specs=[HBM, HBM, SEM, SEM, ANY],
        out_shape=[pltpu.HBM(small.shape, small.dtype), pltpu.HBM(slots.shape, slots.dtype)],
        out_specs=[HBM, HBM], input_output_aliases={0: 0, 1: 1},
        compiler_params=pltpu.CompilerParams(has_side_effects=pltpu.SideEffectType.DATAFLOW_SIDE_EFFECTING),
    )(small, slots, send_sems, recv_sems, after)[1]


def _join_halves(bufs):
    n = len(bufs)

    def body(*refs):
        joined = refs[n:2 * n]
        send_sems, recv_sems = refs[2 * n:]
        x, y, c = _mesh_pos()
        for a in range(n):
            pltpu.make_async_remote_copy(
                src_ref=joined[a].at[c], dst_ref=joined[a].at[c], send_sem=send_sems.at[a],
                recv_sem=recv_sems.at[a], device_id=(x, y, 1 - c), device_id_type=MESH).start()
        for a in range(n):
            pltpu.make_async_remote_copy(
                src_ref=joined[a].at[c], dst_ref=joined[a].at[1 - c], send_sem=send_sems.at[a],
                recv_sem=recv_sems.at[a], device_id=(x, y, 1 - c), device_id_type=MESH).wait()

    return pl.pallas_call(
        body, in_specs=[ANY] * n, out_specs=[ANY] * n,
        out_shape=[jax.ShapeDtypeStruct(b.shape, b.dtype) for b in bufs],
        input_output_aliases={a: a for a in range(n)},
        scratch_shapes=[pltpu.SemaphoreType.DMA((n,))] * 2,
        name="join_halves")(*bufs)


def kernel(x, meta_tokens, norm_g, w_in, conv_w, conv_b, ln_g, ln_b, w_conv_out, lb_logits, gnorm_g, w_rec_out, w_out, final_g, loss_target, m_meta_tokens, m_norm_g, m_w_in, m_conv_w, m_conv_b, m_ln_g, m_ln_b, m_w_conv_out, m_lb_logits, m_gnorm_g, m_w_rec_out, m_w_out, m_final_g, v_meta_tokens, v_norm_g, v_w_in, v_conv_w, v_conv_b, v_ln_g, v_ln_b, v_w_conv_out, v_lb_logits, v_gnorm_g, v_w_rec_out, v_w_out, v_final_g):
    d = x.shape[2]
    n_meta = meta_tokens.shape[0]
    n_pad = CHUNK - n_meta
    ds = d // N_CHIPS
    chip = 2 * lax.axis_index("x") + lax.axis_index("y")

    conv_w_pad = jnp.pad(conv_w[0], ((0, HALO - CONV_WIDTH), (0, 0)))
    chip_idx = chip.astype(jnp.int32).reshape(1)
    (small_g,) = _gather_weights([
        _place_shard("place_small", jnp.concatenate([conv_w_pad, meta_tokens], axis=0), chip_idx, F32)])
    cw_full = jnp.transpose(small_g[:, 0:HALO], (1, 0, 2)).reshape(HALO, d)
    meta_full = jnp.transpose(small_g[:, HALO:HALO + n_meta], (1, 0, 2)).reshape(n_meta, d)

    hres = jnp.concatenate([jnp.zeros((n_pad, d), F32), meta_full, x[0]], axis=0)
    target = loss_target[0]
    final_g2 = final_g.reshape(1, d)
    h = _rmsnorm_fwd(hres, norm_g)
    fx, fy = 1 - lax.axis_index("x"), 1 - lax.axis_index("y")
    order = jnp.stack([chip, 2 * fx + (1 - fy), 2 * (1 - fx) + fy, 2 * fx + fy]).astype(jnp.int32)
    proj, win_t, _ = _gather_in_proj(h, [_place_shard("place_w_in", w_in[0], chip_idx, BF16)], order)
    sq_own = _place_shard("place_square", jnp.concatenate([w_conv_out[0], w_rec_out[0], w_out[0]], axis=0),
                          chip_idx, BF16)
    sq_flight, sq_token = _gather_start(sq_own, proj, "square")
    o, s_all = _hgrn_fwd(proj, lb_logits + sq_token[0:1, 0:1], n_pad)
    sq_g = _gather_wait(sq_flight, s_all, "square")
    wc_full = sq_g[:, 0:ds].reshape(d, d)
    wr_full = sq_g[:, ds:2 * ds].reshape(d, d)
    wo_full = sq_g[:, 2 * ds:3 * ds].reshape(d, d)
    c, yc_in, y_conv = _conv_fwd(proj, cw_full, conv_b, ln_g, ln_b, wc_full)
    yr_in, merged, y_rec, dout, loss_acc, dfinal_g = _tail_fwd(
        o, proj, y_conv, hres, target, gnorm_g, final_g2, wr_full, wo_full)

    (dyc, dyr, dout_bf, dz, dproj, do, dc, dgnorm_g, dln_g, dln_b) = _tail_bwd(
        dout, proj, y_conv, y_rec, o, c, wo_full, wr_full, wc_full, ln_g, ln_b, gnorm_g)
    g_wc = _weight_grad(yc_in, dyc, "grad_w_conv_out", False)
    g_wr = _weight_grad(yr_in, dyr, "grad_w_rec_out", False)
    g_wo = _weight_grad(merged, dout_bf, "grad_w_out", False)

    core = lax.axis_index("c").astype(jnp.int32).reshape(1)

    def chip_sum_and_start(g, tag, g_to_sibling=None):
        halves = lambda a: a.reshape(N_CHIPS, 2, a.shape[1] // 2, a.shape[2])
        g = halves(g)
        (from_sibling,) = _send_other_halves([g if g_to_sibling is None else halves(g_to_sibling)], tag)
        sums = _chip_half_sum("chip_half_sum_" + tag, g, from_sibling, core)
        in_flight, token = _exchange_start([sums[1]], tag)
        return sums[0], in_flight, token[0:1, 0:1]

    g_sq = jnp.concatenate([g.reshape(N_CHIPS, ds, d) for g in (g_wc, g_wr, g_wo)], axis=1)
    sum_sq, flight_sq, token_sq = chip_sum_and_start(g_sq, "square")
    dproj, dlb_logits = _hgrn_bwd(proj, do, s_all, lb_logits + token_sq, n_pad, dproj)
    dproj, dconv_w, dconv_b = _conv_bwd(dc, proj, cw_full, dz, dproj)
    (recv_sq,) = _exchange_wait(flight_sq, dconv_b, "square")
    g_win, g_win_bf = _weight_grad(h, dproj, "grad_w_in", True)
    sum_win, flight_win, token_win = chip_sum_and_start(g_win, "w_in", g_win_bf)
    dhres, dnorm_g = _in_proj_bwd(dproj, win_t, hres, norm_g + token_win, dout)
    grad_x = dhres[CHUNK:][None]
    (recv_win,) = _exchange_wait(flight_win, dnorm_g, "w_in")
    small = jnp.concatenate([dnorm_g, dconv_b, dln_g, dln_b, dlb_logits, dgnorm_g, dfinal_g,
                             dhres[n_pad:CHUNK], dconv_w[:CONV_WIDTH],
                             jnp.broadcast_to(loss_acc[0:1, 0:1], (1, d))], axis=0)
    small_flight, small_token = _small_start(small)
    chip_core = jnp.concatenate([chip_idx, core])
    totals = [_block_half_total("block_half_total_" + nm, s, r, chip_core, small_token)
              for nm, s, r in zip(("w_in", "square"), (sum_win, sum_sq), (recv_win, recv_sq))]
    joined = _join_halves(totals)
    gt_win, gt_sq = [t.reshape(2 * t.shape[1], t.shape[2]) for t in joined]
    small_slots = _small_wait(small_flight, joined[1])
    device_idx = (2 * chip_idx + core).astype(jnp.int32)
    small_sum = _sum_slots("sum_small", small_slots, small, device_idx)

    res = {}
    res["w_in"] = _adamw("adamw_w_in", w_in[0], m_w_in[0], v_w_in[0], gt_win)
    res["w_conv_out"] = _adamw("adamw_w_conv_out", w_conv_out[0], m_w_conv_out[0], v_w_conv_out[0], gt_sq[0:ds])
    res["w_rec_out"] = _adamw("adamw_w_rec_out", w_rec_out[0], m_w_rec_out[0], v_w_rec_out[0], gt_sq[ds:2 * ds])
    res["w_out"] = _adamw("adamw_w_out", w_out[0], m_w_out[0], v_w_out[0], gt_sq[2 * ds:3 * ds])
    big = {k: tuple(a[None] for a in v) for k, v in res.items()}

    rep_names = ("norm_g", "conv_b", "ln_g", "ln_b", "lb_logits", "gnorm_g", "final_g")
    rep_w = (norm_g, conv_b, ln_g, ln_b, lb_logits, gnorm_g, final_g2)
    rep_m = (m_norm_g, m_conv_b, m_ln_g, m_ln_b, m_lb_logits, m_gnorm_g, m_final_g.reshape(1, d))
    rep_v = (v_norm_g, v_conv_b, v_ln_g, v_ln_b, v_lb_logits, v_gnorm_g, v_final_g.reshape(1, d))
    rep = _adamw("adamw_replicated", jnp.concatenate(rep_w, 0), jnp.concatenate(rep_m, 0),
                 jnp.concatenate(rep_v, 0), small_sum[0:8])
    rep_rows = {"norm_g": (0, 1), "conv_b": (1, 2), "ln_g": (2, 3), "ln_b": (3, 4), "lb_logits": (4, 6),
                "gnorm_g": (6, 7), "final_g": (7, 8)}
    small_out = {}
    for nm in rep_names:
        lo, hi = rep_rows[nm]
        vals = tuple(a[lo:hi] for a in rep)
        if nm == "final_g":
            vals = tuple(a.reshape(d) for a in vals)
        small_out[nm] = vals
    cw_row = 8 + n_meta
    g_meta = lax.dynamic_slice_in_dim(small_sum[8:cw_row], chip * ds, ds, axis=1)
    small_out["meta_tokens"] = _adamw("adamw_meta", meta_tokens, m_meta_tokens, v_meta_tokens, g_meta)
    g_cw = lax.dynamic_slice_in_dim(small_sum[cw_row:cw_row + HALO], chip * ds, ds, axis=1)
    pad_rows = ((0, HALO - CONV_WIDTH), (0, 0))
    cw_res = _adamw("adamw_conv_w", conv_w_pad, jnp.pad(m_conv_w[0], pad_rows),
                    jnp.pad(v_conv_w[0], pad_rows, constant_values=1.0), g_cw)
    small_out["conv_w"] = tuple(a[:CONV_WIDTH][None] for a in cw_res)

    loss = small_sum[cw_row + HALO - 1, 0]

    order = ("meta_tokens", "norm_g", "w_in", "conv_w", "conv_b", "ln_g", "ln_b", "w_conv_out", "lb_logits",
             "gnorm_g", "w_rec_out", "w_out", "final_g")
    allres = {**big, **small_out}
    outs = [loss, grad_x]
    for field in range(4):
        outs.extend(allres[nm][field] for nm in order)
    return tuple(outs)
```

```python
import numpy as np

import jax
import jax.numpy as jnp
from jax import lax
from jax.experimental import pallas as pl
from jax.experimental.pallas import tpu as pltpu

F32 = jnp.float32
BF16 = jnp.bfloat16

EPS = 1e-6
CHUNK = 64
N_LEVELS = 6
FIRST_TABLE_LEVEL = 5
CONV_WIDTH = 31
HALO = 32
CONV_ROWS = 64
CONV_LANES = 128
SUBLANES = 8
HEAD = 128
W_IN_COL_TILES = 1
HEADS_PER_TRIP = 8
N_CHIPS = 4
VMEM_LIMIT_BYTES = 56 * 1024 * 1024
ELEMENTWISE_BLOCK_BYTES = 3 * 1024 * 1024

ADAM_LR = 0.001
ADAM_B1 = 0.9
ADAM_B2 = 0.999
ADAM_EPS = 1e-08
ADAM_WD = 0.01
ADAM_STEP = 10

MESH = pl.DeviceIdType.MESH
ANY = pl.BlockSpec(memory_space=pl.ANY)

NT = (((1,), (1,)), ((), ()))
TN = (((0,), (0,)), ((), ()))


def _params(**kw):
    return pltpu.CompilerParams(vmem_limit_bytes=VMEM_LIMIT_BYTES, **kw)


def _sigmoid(x):
    return jax.nn.sigmoid(x)


def _dsilu(x, s):
    return s * (1.0 + x * (1.0 - s))


def _row_tile(lp):
    for t in (320, 256, 192, 128, 64):
        if lp % t == 0:
            return t
    raise ValueError(f"unsupported padded length {lp}")


def _mm_row_tile(lp):
    for t in (832, 640, 320, 256, 192, 128, 64):
        if lp % t == 0:
            return t
    raise ValueError(f"unsupported padded length {lp}")


def _dot3(m_bf16, x):
    hi = x.astype(BF16)
    r1 = x - hi.astype(F32)
    mid = r1.astype(BF16)
    lo = (r1 - mid.astype(F32)).astype(BF16)
    return (jnp.dot(m_bf16, hi, preferred_element_type=F32)
            + jnp.dot(m_bf16, mid, preferred_element_type=F32)
            + jnp.dot(m_bf16, lo, preferred_element_type=F32))


def _dot2(m_bf16, x):
    hi = x.astype(BF16)
    lo = (x - hi.astype(F32)).astype(BF16)
    return (jnp.dot(m_bf16, hi, preferred_element_type=F32)
            + jnp.dot(m_bf16, lo, preferred_element_type=F32))


def _col_to_row(col):
    return jnp.broadcast_to(col, (HEAD, SUBLANES)).T[0:1, :]


def _row_to_col(row):
    return jnp.broadcast_to(row, (SUBLANES, HEAD)).T[:, 0:1]


def _hgrn_tables():
    t = np.arange(CHUNK)
    ltri = (t[None, :] <= t[:, None]).astype(np.float32)
    mats = [ltri]
    for lvl in range(FIRST_TABLE_LEVEL, N_LEVELS + 1):
        blk = CHUNK >> (lvl - 1)
        mid = (t // blk) * blk + blk // 2
        mats.append(ltri[mid - 1])
    after = (t[None, :] >= t[:, None]).astype(np.float32)
    before = (t[None, :] < t[:, None]).astype(np.float32)
    return jnp.asarray(np.concatenate(mats, 0), BF16), jnp.asarray(np.concatenate([after, before], 1), BF16)


def _rmsnorm_fwd(hres, g):
    lp, d = hres.shape
    tm = _mm_row_tile(lp)

    def body(x_ref, g_ref, h_ref):
        x = x_ref[...]
        r = lax.rsqrt(jnp.mean(x * x, axis=-1, keepdims=True) + EPS)
        h_ref[...] = (x * r * g_ref[...]).astype(BF16)

    return pl.pallas_call(
        body, grid=(lp // tm,),
        in_specs=[pl.BlockSpec((tm, d), lambda i: (i, 0)), pl.BlockSpec((1, d), lambda i: (0, 0))],
        out_specs=pl.BlockSpec((tm, d), lambda i: (i, 0)),
        out_shape=jax.ShapeDtypeStruct((lp, d), BF16),
        name="rmsnorm_fwd", compiler_params=_params())(hres, g)


def _conv_fwd(proj, conv_w, conv_b, ln_g, ln_b, w_conv):
    lp = proj.shape[0]
    d = conv_b.shape[1]
    tm = _row_tile(lp)
    hb = tm // HALO

    def body(ua_ref, ub_ref, z_ref, uap_ref, ubp_ref, cw_ref, cb_ref, lg_ref, lb_ref, w_ref,
             c_ref, ycin_ref, yconv_ref, aext_ref):
        i = pl.program_id(0)
        a_prev = uap_ref[...] * _sigmoid(ubp_ref[...])
        aext_ref[0:HALO, :] = jnp.where(i > 0, a_prev, 0.0)
        aext_ref[HALO:HALO + tm, :] = ua_ref[...] * _sigmoid(ub_ref[...])

        def row_block(r, carry):
            r0 = pl.multiple_of(r * CONV_ROWS, CONV_ROWS)
            for cs in range(d // CONV_LANES):
                cl = slice(cs * CONV_LANES, (cs + 1) * CONV_LANES)
                blk = aext_ref[pl.ds(r0, CONV_ROWS + HALO), cl]
                acc = jnp.zeros((CONV_ROWS, CONV_LANES), F32) + cb_ref[:, cl]
                for b in range(SUBLANES):
                    sh = blk if b == 0 else pltpu.roll(blk, CONV_ROWS + HALO - b, axis=0)
                    for a in range(5):
                        j = SUBLANES * a + b - 2
                        if 0 <= j < CONV_WIDTH:
                            acc = acc + cw_ref[j:j + 1, cl] * sh[SUBLANES * a:SUBLANES * a + CONV_ROWS, :]
                c_ref[pl.ds(r0, CONV_ROWS), cl] = acc
            return carry

        lax.fori_loop(0, tm // CONV_ROWS, row_block, 0)

        c = c_ref[...]
        mu = jnp.mean(c, axis=-1, keepdims=True)
        xc = c - mu
        rstd = lax.rsqrt(jnp.mean(xc * xc, axis=-1, keepdims=True) + EPS)
        ln = xc * rstd * lg_ref[...] + lb_ref[...]
        s = ln * _sigmoid(ln)
        z = z_ref[...]
        ycin = (s * (z * _sigmoid(z))).astype(BF16)
        ycin_ref[...] = ycin
        yconv_ref[...] = jnp.dot(ycin, w_ref[...], preferred_element_type=F32)

    row = lambda p: pl.BlockSpec((tm, d), lambda i, p=p: (i, p))
    halo = lambda p: pl.BlockSpec((HALO, d), lambda i, p=p: (jnp.maximum(i * hb - 1, 0), p))
    vec = pl.BlockSpec((1, d), lambda i: (0, 0))
    return pl.pallas_call(
        body, grid=(lp // tm,),
        in_specs=[row(0), row(1), row(2), halo(0), halo(1),
                  pl.BlockSpec((HALO, d), lambda i: (0, 0)), vec, vec, vec,
                  pl.BlockSpec((d, d), lambda i: (0, 0))],
        out_specs=[pl.BlockSpec((tm, d), lambda i: (i, 0))] * 3,
        out_shape=[jax.ShapeDtypeStruct((lp, d), F32), jax.ShapeDtypeStruct((lp, d), BF16),
                   jax.ShapeDtypeStruct((lp, d), F32)],
        scratch_shapes=[pltpu.VMEM((HALO + tm, d), F32)],
        name="conv_fwd", compiler_params=_params())(
            proj, proj, proj, proj, proj, conv_w, conv_b, ln_g, ln_b, w_conv)


def _lower_bound(lbl_ref):
    l0 = lbl_ref[0:1, :]
    l1 = lbl_ref[1:2, :]
    m = jnp.maximum(l0, l1)
    e0 = jnp.exp(l0 - m)
    e1 = jnp.exp(l1 - m)
    p0 = e0 / (e0 + e1)
    return p0, p0 * (e1 / (e0 + e1))


def _level_masks():
    r2 = lax.broadcasted_iota(jnp.int32, (CHUNK, CHUNK), 0)
    c2 = lax.broadcasted_iota(jnp.int32, (CHUNK, CHUNK), 1)
    out = []
    for lvl in range(1, N_LEVELS + 1):
        blk = CHUNK >> (lvl - 1)
        sh = blk.bit_length() - 1
        same = (r2 >> sh) == (c2 >> sh)
        t_upper = (r2 & (blk - 1)) >= (blk // 2)
        s_lower = (c2 & (blk - 1)) < (blk // 2)
        out.append(jnp.logical_and(same, jnp.logical_and(t_upper, s_lower)))
    return out


def _gates(qr, fr, lb, valid):
    sq = _sigmoid(qr)
    q = qr * sq
    sf = _sigmoid(fr)
    f = lb + (1.0 - lb) * sf
    g = jnp.where(valid, jnp.log(f), 0.0)
    k = jnp.where(valid, 1.0 - f, 0.0)
    return q, sq, f, sf, g, k


def _level_reference(lvl, b, t_ref, hs):
    if lvl >= FIRST_TABLE_LEVEL:
        base = CHUNK * (lvl - FIRST_TABLE_LEVEL + 1)
        return t_ref[base:base + CHUNK, hs]
    blk = CHUNK >> (lvl - 1)
    rows = [jnp.broadcast_to(b[m + blk // 2 - 1:m + blk // 2, :], (blk, HEAD)) for m in range(0, CHUNK, blk)]
    return rows[0] if len(rows) == 1 else jnp.concatenate(rows, axis=0)


def _level_factor(b, r):
    d = b - r
    return jnp.exp(jnp.minimum(d, -d))


def _hgrn_fwd(proj, lb_logits, n_pad):
    lp = proj.shape[0]
    d = lb_logits.shape[1]
    n_heads = d // HEAD
    nc = lp // CHUNK
    tab, _ = _hgrn_tables()
    n_tab = tab.shape[0]

    def body(qr_ref, fr_ref, ir_ref, lbl_ref, tab_ref, o_ref, sall_ref, s_ref, t_ref):
        n = pl.program_id(0)

        @pl.when(n == 0)
        def _():
            s_ref[...] = jnp.zeros_like(s_ref)

        sall_ref[0] = s_ref[...]
        lb_all, _ = _lower_bound(lbl_ref)
        rid = lax.broadcasted_iota(jnp.int32, (CHUNK, 1), 0)
        valid = jnp.logical_or(n > 0, rid >= n_pad)
        f_all = lb_all + (1.0 - lb_all) * _sigmoid(fr_ref[...])
        t_ref[...] = _dot2(tab_ref[...], jnp.where(valid, jnp.log(f_all), 0.0))
        masks = _level_masks()

        def head(h):
            off = h * HEAD if isinstance(h, int) else pl.multiple_of(h * HEAD, HEAD)
            hs = pl.ds(off, HEAD)
            lb = _lower_bound_slice(lbl_ref, hs)
            q, _, _, _, _, k = _gates(qr_ref[:, hs], fr_ref[:, hs], lb, valid)
            v = ir_ref[:, hs]
            b = t_ref[0:CHUNK, hs]
            s0 = s_ref[hs, :]
            o = jnp.dot((q * jnp.exp(b)).astype(BF16), s0.astype(BF16), preferred_element_type=F32)
            o = o + jnp.sum(q * k, axis=-1, keepdims=True) * v
            a = jnp.zeros((CHUNK, CHUNK), F32)
            for lvl in range(1, N_LEVELS + 1):
                e = _level_factor(b, _level_reference(lvl, b, t_ref, hs))
                p = lax.dot_general((q * e).astype(BF16), (k * e).astype(BF16), NT, preferred_element_type=F32)
                a = a + jnp.where(masks[lvl - 1], p, 0.0)
            vb = v.astype(BF16)
            o_ref[:, hs] = o + jnp.dot(a.astype(BF16), vb, preferred_element_type=F32)
            b_last = t_ref[CHUNK - 1:CHUNK, hs]
            khat = (k * jnp.exp(b_last - b)).astype(BF16)
            s_ref[hs, :] = _row_to_col(jnp.exp(b_last)) * s0 + lax.dot_general(khat, vb, TN, preferred_element_type=F32)
        per_trip = min(HEADS_PER_TRIP, n_heads)

        def head_group(p, carry):
            for u in range(per_trip):
                head(p * per_trip + u)
            return carry

        if n_heads == per_trip:
            head_group(0, 0)
        else:
            lax.fori_loop(0, n_heads // per_trip, head_group, 0)

    piece = lambda p: pl.BlockSpec((CHUNK, d), lambda n, p=p: (n, p))
    return pl.pallas_call(
        body, grid=(nc,),
        in_specs=[piece(3), piece(4), piece(5), pl.BlockSpec((2, d), lambda n: (0, 0)),
                  pl.BlockSpec((n_tab, CHUNK), lambda n: (0, 0))],
        out_specs=[pl.BlockSpec((CHUNK, d), lambda n: (n, 0)), pl.BlockSpec((1, d, HEAD), lambda n: (n, 0, 0))],
        out_shape=[jax.ShapeDtypeStruct((lp, d), F32), jax.ShapeDtypeStruct((nc, d, HEAD), F32)],
        scratch_shapes=[pltpu.VMEM((d, HEAD), F32), pltpu.VMEM((n_tab, d), F32)],
        name="hgrn_fwd", compiler_params=_params())(proj, proj, proj, lb_logits, tab)


def _lower_bound_slice(lbl_ref, hs):
    l0 = lbl_ref[0:1, hs]
    l1 = lbl_ref[1:2, hs]
    m = jnp.maximum(l0, l1)
    e0 = jnp.exp(l0 - m)
    e1 = jnp.exp(l1 - m)
    return e0 / (e0 + e1)


def _tail_fwd(o, proj, y_conv, hres, target, gnorm_g, final_g, w_rec, w_out):
    lp, d = o.shape
    n_heads = d // HEAD
    tm = _row_tile(lp)

    n_slabs = tm // CHUNK

    def body(o_ref, gr_ref, mc_ref, mr_ref, yc_ref, x_ref, gn_ref, fg_ref, wr_ref, wo_ref, *rest):
        t_refs = rest[:n_slabs]
        yrin_ref, mg_ref, yrec_ref, dout_ref, loss_ref, dfg_ref = rest[n_slabs:]
        i = pl.program_id(0)

        @pl.when(i == 0)
        def _():
            loss_ref[...] = jnp.zeros_like(loss_ref)
            dfg_ref[...] = jnp.zeros_like(dfg_ref)

        for h in range(n_heads):
            hs = slice(h * HEAD, (h + 1) * HEAD)
            oh = o_ref[:, hs]
            on = oh * lax.rsqrt(jnp.mean(oh * oh, axis=-1, keepdims=True) + EPS) * gn_ref[:, hs]
            gr = gr_ref[:, hs]
            yrin_ref[:, hs] = (on * (gr * _sigmoid(gr))).astype(BF16)
        yrec = jnp.dot(yrin_ref[...], wr_ref[...], preferred_element_type=F32)
        yrec_ref[...] = yrec
        merged = (_sigmoid(mc_ref[...]) * yc_ref[...] + _sigmoid(mr_ref[...]) * yrec).astype(BF16)
        mg_ref[...] = merged
        out = x_ref[...] + jnp.dot(merged, wo_ref[...], preferred_element_type=F32)
        r = lax.rsqrt(jnp.mean(out * out, axis=-1, keepdims=True) + EPS)
        yhat = out * r
        fg = fg_ref[...]
        rid = lax.broadcasted_iota(jnp.int32, (tm, 1), 0) + i * tm
        tgt = jnp.concatenate([t[...] for t in t_refs], axis=0)
        err = jnp.where(rid >= CHUNK, yhat * fg - tgt, 0.0)
        loss_ref[...] += 0.5 * jnp.sum(err * err) / d
        dy = err / d
        dfg_ref[...] += jnp.sum(dy * yhat, axis=0, keepdims=True)
        dyh = dy * fg
        dout_ref[...] = r * (dyh - yhat * jnp.mean(dyh * yhat, axis=-1, keepdims=True))

    row = lambda p: pl.BlockSpec((tm, d), lambda i, p=p: (i, p))
    vec = pl.BlockSpec((1, d), lambda i: (0, 0))
    mat = pl.BlockSpec((d, d), lambda i: (0, 0))
    return pl.pallas_call(
        body, grid=(lp // tm,),
        in_specs=[row(0), row(6), row(7), row(8), row(0), row(0), vec, vec, mat, mat]
        + [pl.BlockSpec((CHUNK, d), lambda i, u=u: (jnp.maximum(i * n_slabs + u - 1, 0), 0)) for u in range(n_slabs)],
        out_specs=[row(0), row(0), row(0), row(0), pl.BlockSpec((8, 128), lambda i: (0, 0)), vec],
        out_shape=[jax.ShapeDtypeStruct((lp, d), BF16), jax.ShapeDtypeStruct((lp, d), BF16),
                   jax.ShapeDtypeStruct((lp, d), F32), jax.ShapeDtypeStruct((lp, d), F32),
                   jax.ShapeDtypeStruct((8, 128), F32), jax.ShapeDtypeStruct((1, d), F32)],
        name="tail_fwd", compiler_params=_params())(
            o, proj, proj, proj, y_conv, hres, gnorm_g, final_g, w_rec, w_out, *([target] * n_slabs))


def _tail_bwd(dout, proj, y_conv, y_rec, o, c, w_out, w_rec, w_conv, ln_g, ln_b, gnorm_g):
    lp, d = dout.shape
    n_heads = d // HEAD
    tm = _row_tile(lp)

    def body(dout_ref, mc_ref, mr_ref, z_ref, gr_ref, yc_ref, yrec_ref, o_ref, c_ref,
             wo_ref, wr_ref, wc_ref, lg_ref, lb_ref, gn_ref,
             dyc_ref, dyr_ref, doutb_ref, dz_ref, dp_ref, do_ref, dc_ref,
             dgn_ref, dlg_ref, dlb_ref, dyrin_ref):
        i = pl.program_id(0)

        @pl.when(i == 0)
        def _():
            dgn_ref[...] = jnp.zeros_like(dgn_ref)
            dlg_ref[...] = jnp.zeros_like(dlg_ref)
            dlb_ref[...] = jnp.zeros_like(dlb_ref)

        doutb = dout_ref[...].astype(BF16)
        doutb_ref[...] = doutb
        dmerged = lax.dot_general(doutb, wo_ref[...], NT, preferred_element_type=F32)
        smc = _sigmoid(mc_ref[...])
        smr = _sigmoid(mr_ref[...])
        dyc = (dmerged * smc).astype(BF16)
        dyr = (dmerged * smr).astype(BF16)
        dyc_ref[...] = dyc
        dyr_ref[...] = dyr
        dp_ref[:, d:2 * d] = (dmerged * yc_ref[...] * smc * (1.0 - smc)).astype(BF16)
        dp_ref[:, 2 * d:3 * d] = (dmerged * yrec_ref[...] * smr * (1.0 - smr)).astype(BF16)

        dyrin_ref[...] = lax.dot_general(dyr, wr_ref[...], NT, preferred_element_type=F32)
        for h in range(n_heads):
            hs = slice(h * HEAD, (h + 1) * HEAD)
            oh = o_ref[:, hs]
            rstd = lax.rsqrt(jnp.mean(oh * oh, axis=-1, keepdims=True) + EPS)
            ohat = oh * rstd
            gn = gn_ref[:, hs]
            gr = gr_ref[:, hs]
            sg = _sigmoid(gr)
            dyrin = dyrin_ref[:, hs]
            don = dyrin * (gr * sg)
            dp_ref[:, hs] = (dyrin * (ohat * gn) * _dsilu(gr, sg)).astype(BF16)
            dgn_ref[:, hs] += jnp.sum(don * ohat, axis=0, keepdims=True)
            doh = don * gn
            do_ref[:, hs] = rstd * (doh - ohat * jnp.mean(doh * ohat, axis=-1, keepdims=True))

        dycin = lax.dot_general(dyc, wc_ref[...], NT, preferred_element_type=F32)
        c = c_ref[...]
        mu = jnp.mean(c, axis=-1, keepdims=True)
        xc = c - mu
        rstd = lax.rsqrt(jnp.mean(xc * xc, axis=-1, keepdims=True) + EPS)
        nrm = xc * rstd
        lg = lg_ref[...]
        ln = nrm * lg + lb_ref[...]
        sl = _sigmoid(ln)
        z = z_ref[...]
        sz = _sigmoid(z)
        dz_ref[...] = (dycin * (ln * sl) * _dsilu(z, sz)).astype(BF16)
        dln = dycin * (z * sz) * _dsilu(ln, sl)
        dlg_ref[...] += jnp.sum(dln * nrm, axis=0, keepdims=True)
        dlb_ref[...] += jnp.sum(dln, axis=0, keepdims=True)
        dn = dln * lg
        dc_ref[...] = rstd * (dn - jnp.mean(dn, axis=-1, keepdims=True)
                              - nrm * jnp.mean(dn * nrm, axis=-1, keepdims=True))

    row = lambda p: pl.BlockSpec((tm, d), lambda i, p=p: (i, p))
    vec = pl.BlockSpec((1, d), lambda i: (0, 0))
    mat = pl.BlockSpec((d, d), lambda i: (0, 0))
    act_bf = jax.ShapeDtypeStruct((lp, d), BF16)
    act_f32 = jax.ShapeDtypeStruct((lp, d), F32)
    vec_f32 = jax.ShapeDtypeStruct((1, d), F32)
    return pl.pallas_call(
        body, grid=(lp // tm,),
        in_specs=[row(0), row(7), row(8), row(2), row(6), row(0), row(0), row(0), row(0),
                  mat, mat, mat, vec, vec, vec],
        out_specs=[row(0)] * 4 + [pl.BlockSpec((tm, 3 * d), lambda i: (i, 2))] + [row(0)] * 2 + [vec] * 3,
        out_shape=[act_bf] * 4 + [jax.ShapeDtypeStruct((lp, 9 * d), BF16)] + [act_f32] * 2 + [vec_f32] * 3,
        scratch_shapes=[pltpu.VMEM((tm, d), F32)],
        name="tail_bwd", compiler_params=_params())(
            dout, proj, proj, proj, proj, y_conv, y_rec, o, c, w_out, w_rec, w_conv, ln_g, ln_b, gnorm_g)


def _hgrn_bwd(proj, do, s_all, lb_logits, n_pad, dproj):
    lp, d = do.shape
    n_heads = d // HEAD
    nc = lp // CHUNK
    tab, utri = _hgrn_tables()
    n_tab = tab.shape[0]

    def body(qr_ref, fr_ref, ir_ref, do_ref, s0_ref, lbl_ref, tab_ref, ut_ref, _,
             dp_ref, dlbl_ref, ds_ref, t_ref, dlb_ref):
        n = pl.program_id(0)
        chunk = nc - 1 - n

        @pl.when(n == 0)
        def _():
            ds_ref[...] = jnp.zeros_like(ds_ref)
            dlb_ref[...] = jnp.zeros_like(dlb_ref)

        lb_all, pp = _lower_bound(lbl_ref)
        rid = lax.broadcasted_iota(jnp.int32, (CHUNK, 1), 0)
        valid = jnp.logical_or(chunk > 0, rid >= n_pad)
        f_all = lb_all + (1.0 - lb_all) * _sigmoid(fr_ref[...])
        t_ref[...] = _dot2(tab_ref[...], jnp.where(valid, jnp.log(f_all), 0.0))
        masks = _level_masks()
        ut = ut_ref[...]

        def head(h):
            off = h * HEAD if isinstance(h, int) else pl.multiple_of(h * HEAD, HEAD)
            hs = pl.ds(off, HEAD)
            lb = _lower_bound_slice(lbl_ref, hs)
            qr = qr_ref[:, hs]
            q, sq, f, sf, _, k = _gates(qr, fr_ref[:, hs], lb, valid)
            v = ir_ref[:, hs]
            do_h = do_ref[:, hs]
            b = t_ref[0:CHUNK, hs]
            b_last = t_ref[CHUNK - 1:CHUNK, hs]
            s0 = s0_ref[0, hs, :]
            ds1 = ds_ref[hs, :]
            eb = jnp.exp(b)
            ekl = jnp.exp(b_last - b)
            do_bf = do_h.astype(BF16)
            v_bf = v.astype(BF16)
            ds1_bf = ds1.astype(BF16)

            da = lax.dot_general(do_bf, v_bf, NT, preferred_element_type=F32)
            da_diag = jnp.sum(do_h * v, axis=-1, keepdims=True)
            a = jnp.zeros((CHUNK, CHUNK), F32)
            dq_x = eb * lax.dot_general(do_bf, s0.astype(BF16), NT, preferred_element_type=F32)
            dk_x = ekl * lax.dot_general(v_bf, ds1_bf, NT, preferred_element_type=F32)
            x_after = q * dq_x
            x_before = k * dk_x
            for lvl in range(1, N_LEVELS + 1):
                e = _level_factor(b, _level_reference(lvl, b, t_ref, hs))
                qt = (q * e).astype(BF16)
                kt = (k * e).astype(BF16)
                p = lax.dot_general(qt, kt, NT, preferred_element_type=F32)
                a = a + jnp.where(masks[lvl - 1], p, 0.0)
                dam = jnp.where(masks[lvl - 1], da, 0.0).astype(BF16)
                dqt = jnp.dot(dam, kt, preferred_element_type=F32)
                dkt = lax.dot_general(dam, qt, TN, preferred_element_type=F32)
                dq_x = dq_x + e * dqt
                dk_x = dk_x + e * dkt
                x_after = x_after + (qt.astype(F32) * dqt - kt.astype(F32) * dkt)

            dv = (lax.dot_general(a.astype(BF16), do_bf, TN, preferred_element_type=F32)
                  + jnp.sum(q * k, axis=-1, keepdims=True) * do_h
                  + jnp.dot((k * ekl).astype(BF16), ds1_bf, preferred_element_type=F32))
            dp_ref[:, pl.ds(2 * d + off, HEAD)] = dv.astype(BF16)

            carried = jnp.exp(b_last) * _col_to_row(jnp.sum(s0 * ds1, axis=-1, keepdims=True))
            dg = _dot3(ut, jnp.concatenate([x_after, x_before], axis=0)) + carried
            dq = dq_x + da_diag * k
            dk = dk_x + da_diag * q
            dp_ref[:, hs] = (dq * _dsilu(qr, sq)).astype(BF16)
            df = jnp.where(valid, dg / f - dk, 0.0)
            dp_ref[:, pl.ds(d + off, HEAD)] = (df * (1.0 - lb) * sf * (1.0 - sf)).astype(BF16)
            dlb_ref[:, hs] += jnp.sum(df * (1.0 - sf), axis=0, keepdims=True)

            ds_ref[hs, :] = (_row_to_col(jnp.exp(b_last)) * ds1
                             + lax.dot_general((q * eb).astype(BF16), do_bf, TN, preferred_element_type=F32))
        per_trip = min(HEADS_PER_TRIP, n_heads)

        def head_group(p, carry):
            for u in range(per_trip):
                head(p * per_trip + u)
            return carry

        if n_heads == per_trip:
            head_group(0, 0)
        else:
            lax.fori_loop(0, n_heads // per_trip, head_group, 0)

        @pl.when(n == nc - 1)
        def _():
            dl0 = dlb_ref[...] * pp
            dlbl_ref[0:1, :] = dl0
            dlbl_ref[1:2, :] = -dl0

    piece = lambda p: pl.BlockSpec((CHUNK, d), lambda n, p=p: (nc - 1 - n, p))
    return pl.pallas_call(
        body, grid=(nc,),
        in_specs=[piece(3), piece(4), piece(5), piece(0),
                  pl.BlockSpec((1, d, HEAD), lambda n: (nc - 1 - n, 0, 0)),
                  pl.BlockSpec((2, d), lambda n: (0, 0)),
                  pl.BlockSpec((n_tab, CHUNK), lambda n: (0, 0)),
                  pl.BlockSpec((CHUNK, 2 * CHUNK), lambda n: (0, 0)), ANY],
        out_specs=[pl.BlockSpec((CHUNK, 3 * d), lambda n: (nc - 1 - n, 1)), pl.BlockSpec((2, d), lambda n: (0, 0))],
        out_shape=[jax.ShapeDtypeStruct(dproj.shape, BF16), jax.ShapeDtypeStruct((2, d), F32)],
        input_output_aliases={8: 0},
        scratch_shapes=[pltpu.VMEM((d, HEAD), F32), pltpu.VMEM((n_tab, d), F32), pltpu.VMEM((1, d), F32)],
        name="hgrn_bwd", compiler_params=_params())(proj, proj, proj, do, s_all, lb_logits, tab, utri, dproj)


def _conv_bwd(dc, proj, conv_w, dz, dproj):
    lp, d = dc.shape
    tm = _row_tile(lp)
    hb = tm // HALO
    n_tiles = lp // tm
    last_halo = lp // HALO - 1

    def body(dc_ref, dcn_ref, ua_ref, ub_ref, uap_ref, ubp_ref, cw_ref, dz_ref, _,
             dp_ref, dcw_ref, dcb_ref, aext_ref, dcext_ref, da_ref, dcw_acc):
        i = pl.program_id(0)

        @pl.when(i == 0)
        def _():
            dcw_acc[...] = jnp.zeros_like(dcw_acc)
            dcb_ref[...] = jnp.zeros_like(dcb_ref)

        ua = ua_ref[...]
        sb = _sigmoid(ub_ref[...])
        a_prev = uap_ref[...] * _sigmoid(ubp_ref[...])
        aext_ref[0:HALO, :] = jnp.where(i > 0, a_prev, 0.0)
        aext_ref[HALO:HALO + tm, :] = ua * sb
        dcext_ref[0:tm, :] = dc_ref[...]
        dcext_ref[tm:tm + HALO, :] = jnp.where(i < n_tiles - 1, dcn_ref[...], 0.0)
        dcb_ref[...] += jnp.sum(dc_ref[...], axis=0, keepdims=True)

        def row_block(r, carry):
            r0 = pl.multiple_of(r * CONV_ROWS, CONV_ROWS)
            n_rows = CONV_ROWS + HALO
            for cs in range(d // CONV_LANES):
                cl = slice(cs * CONV_LANES, (cs + 1) * CONV_LANES)
                dblk = dcext_ref[pl.ds(r0, n_rows), cl]
                ablk = aext_ref[pl.ds(r0, n_rows), cl]
                dcur = dblk[0:CONV_ROWS, :]
                acc = jnp.zeros((CONV_ROWS, CONV_LANES), F32)
                for b in range(SUBLANES):
                    dsh = dblk if b == 0 else pltpu.roll(dblk, n_rows - b, axis=0)
                    ash = ablk if b == 0 else pltpu.roll(ablk, n_rows - b, axis=0)
                    for a in range(5):
                        j_da = CONV_WIDTH - 1 - (SUBLANES * a + b)
                        if 0 <= j_da < CONV_WIDTH:
                            acc = acc + cw_ref[j_da:j_da + 1, cl] * dsh[SUBLANES * a:SUBLANES * a + CONV_ROWS, :]
                        j_w = SUBLANES * a + b - 2
                        if 0 <= j_w < CONV_WIDTH:
                            prod = dcur * ash[SUBLANES * a:SUBLANES * a + CONV_ROWS, :]
                            dcw_acc[j_w, :, cl] += prod.reshape(CONV_ROWS // SUBLANES, SUBLANES, CONV_LANES).sum(axis=0)
                da_ref[pl.ds(r0, CONV_ROWS), cl] = acc
            return carry

        lax.fori_loop(0, tm // CONV_ROWS, row_block, 0)

        da = da_ref[...]
        dp_ref[:, 0:d] = (da * sb).astype(BF16)
        dp_ref[:, d:2 * d] = (da * ua * sb * (1.0 - sb)).astype(BF16)
        dp_ref[:, 2 * d:3 * d] = dz_ref[...]

        @pl.when(i == n_tiles - 1)
        def _():
            dcw_ref[...] = jnp.sum(dcw_acc[...], axis=1)

    row = lambda p: pl.BlockSpec((tm, d), lambda i, p=p: (i, p))
    prev = lambda p: pl.BlockSpec((HALO, d), lambda i, p=p: (jnp.maximum(i * hb - 1, 0), p))
    nxt = pl.BlockSpec((HALO, d), lambda i: (jnp.minimum((i + 1) * hb, last_halo), 0))
    return pl.pallas_call(
        body, grid=(n_tiles,),
        in_specs=[row(0), nxt, row(0), row(1), prev(0), prev(1), pl.BlockSpec((HALO, d), lambda i: (0, 0)),
                  row(0), ANY],
        out_specs=[pl.BlockSpec((tm, 3 * d), lambda i: (i, 0)), pl.BlockSpec((HALO, d), lambda i: (0, 0)),
                   pl.BlockSpec((1, d), lambda i: (0, 0))],
        out_shape=[jax.ShapeDtypeStruct(dproj.shape, BF16),
                   jax.ShapeDtypeStruct((HALO, d), F32), jax.ShapeDtypeStruct((1, d), F32)],
        input_output_aliases={8: 0},
        scratch_shapes=[pltpu.VMEM((HALO + tm, d), F32), pltpu.VMEM((tm + HALO, d), F32), pltpu.VMEM((tm, d), F32),
                        pltpu.VMEM((HALO, SUBLANES, d), F32)],
        name="conv_bwd", compiler_params=_params())(dc, dc, proj, proj, proj, proj, conv_w, dz, dproj)


def _weight_grad(xs, dy, name, blocked):
    lp, dx = xs.shape
    n = dy.shape[1]
    tk = _mm_row_tile(lp)
    if blocked:
        ncol = n // N_CHIPS
        nt = W_IN_COL_TILES
        tn = ncol // nt
        grid = (N_CHIPS * nt, lp // tk)
        out_spec = pl.BlockSpec((1, dx, tn), lambda c, k: (c // nt, 0, c % nt))
        out_shape = jax.ShapeDtypeStruct((N_CHIPS, dx, ncol), F32)
    else:
        tn = n
        grid = (1, lp // tk)
        out_spec = pl.BlockSpec((dx, tn), lambda c, k: (0, c))
        out_shape = jax.ShapeDtypeStruct((dx, n), F32)

    def body(xs_ref, dy_ref, o_ref, *copy_ref):
        @pl.when(pl.program_id(1) == 0)
        def _():
            o_ref[...] = jnp.zeros_like(o_ref)

        p = lax.dot_general(xs_ref[...], dy_ref[...], TN, preferred_element_type=F32)
        if blocked:
            o_ref[0] += p

            @pl.when(pl.program_id(1) == lp // tk - 1)
            def _():
                copy_ref[0][0] = o_ref[0].astype(BF16)
        else:
            o_ref[...] += p

    if blocked:
        out_spec = [out_spec, out_spec]
        out_shape = [out_shape, jax.ShapeDtypeStruct(out_shape.shape, BF16)]
    return pl.pallas_call(
        body, grid=grid,
        in_specs=[pl.BlockSpec((tk, dx), lambda c, k: (k, 0)), pl.BlockSpec((tk, tn), lambda c, k: (k, c))],
        out_specs=out_spec, out_shape=out_shape,
        name=name, compiler_params=_params())(xs, dy)


def _in_proj_bwd(dproj, wtg, hres, norm_g, dout):
    lp, d = hres.shape
    _, ncol, _ = wtg.shape
    tm = _mm_row_tile(lp)
    nt = W_IN_COL_TILES
    tn = ncol // nt
    nk = N_CHIPS * nt

    def body(dp_ref, w_ref, x_ref, g_ref, dout_ref, dx_ref, dg_ref, acc_ref):
        i = pl.program_id(0)
        kk = pl.program_id(1)

        @pl.when(jnp.logical_and(i == 0, kk == 0))
        def _():
            dg_ref[...] = jnp.zeros_like(dg_ref)

        @pl.when(kk == 0)
        def _():
            acc_ref[...] = jnp.zeros_like(acc_ref)

        acc_ref[...] += jnp.dot(dp_ref[...], w_ref[0], preferred_element_type=F32)

        @pl.when(kk == nk - 1)
        def _():
            x = x_ref[...]
            r = lax.rsqrt(jnp.mean(x * x, axis=-1, keepdims=True) + EPS)
            xhat = x * r
            dh = acc_ref[...]
            dg_ref[...] += jnp.sum(dh * xhat, axis=0, keepdims=True)
            dxh = dh * g_ref[...]
            dx_ref[...] = dout_ref[...] + r * (dxh - xhat * jnp.mean(dxh * xhat, axis=-1, keepdims=True))

    return pl.pallas_call(
        body, grid=(lp // tm, nk),
        in_specs=[pl.BlockSpec((tm, tn), lambda i, k: (i, k)),
                  pl.BlockSpec((1, tn, d), lambda i, k: (k // nt, k % nt, 0)),
                  pl.BlockSpec((tm, d), lambda i, k: (i, 0)),
                  pl.BlockSpec((1, d), lambda i, k: (0, 0)),
                  pl.BlockSpec((tm, d), lambda i, k: (i, 0))],
        out_specs=[pl.BlockSpec((tm, d), lambda i, k: (i, 0)), pl.BlockSpec((1, d), lambda i, k: (0, 0))],
        out_shape=[jax.ShapeDtypeStruct((lp, d), F32), jax.ShapeDtypeStruct((1, d), F32)],
        scratch_shapes=[pltpu.VMEM((tm, d), F32)],
        name="in_proj_bwd", compiler_params=_params())(dproj, wtg, hres, norm_g, dout)


def _adamw_math(w, g, m, v):
    m = ADAM_B1 * m + (1.0 - ADAM_B1) * g
    v = ADAM_B2 * v + (1.0 - ADAM_B2) * (g * g)
    m_hat = m / (1.0 - ADAM_B1 ** ADAM_STEP)
    v_hat = v / (1.0 - ADAM_B2 ** ADAM_STEP)
    delta = -ADAM_LR * (m_hat / (jnp.sqrt(v_hat) + ADAM_EPS) + ADAM_WD * w)
    return delta, m, v


def _elementwise_rows(shape):
    r, c = shape
    for t in (256, 128, 64, 32, 16, 8):
        if r % t == 0 and r > t and t * c * 4 <= ELEMENTWISE_BLOCK_BYTES:
            return t
    return r


def _adamw(name, w, m, v, *g_parts):
    shape = w.shape
    tr = _elementwise_rows(shape)
    n_g = len(g_parts)

    def body(*refs):
        w_ref, m_ref, v_ref = refs[:3]
        g_refs = refs[3:3 + n_g]
        g_out, d_out, m_out, v_out = refs[3 + n_g:]
        g = g_refs[0][...]
        for gr in g_refs[1:]:
            g = g + gr[...]
        delta, m_new, v_new = _adamw_math(w_ref[...], g, m_ref[...], v_ref[...])
        g_out[...] = g
        d_out[...] = delta
        m_out[...] = m_new
        v_out[...] = v_new

    spec = pl.BlockSpec((tr, shape[1]), lambda i: (i, 0))
    return pl.pallas_call(
        body, grid=(shape[0] // tr,),
        in_specs=[spec] * (3 + n_g), out_specs=[spec] * 4,
        out_shape=[jax.ShapeDtypeStruct(shape, F32)] * 4,
        name=name, compiler_params=_params())(w, m, v, *g_parts)


def _chip_half_sum(name, g, recv, core):
    _, _, hr, cols = g.shape
    tr = _elementwise_rows((hr, cols))

    def body(core_ref, g_ref, r_ref, o_ref, ob_ref):
        s = g_ref[0, 0] + r_ref[0].astype(F32)
        o_ref[0] = s
        ob_ref[0] = s.astype(BF16)

    blk = pl.BlockSpec((1, tr, cols), lambda j, i, core_ref: (j, i, 0))
    grid_spec = pltpu.PrefetchScalarGridSpec(
        num_scalar_prefetch=1, grid=(N_CHIPS, hr // tr),
        in_specs=[pl.BlockSpec((1, 1, tr, cols), lambda j, i, core_ref: (j, core_ref[0], i, 0)), blk],
        out_specs=[blk, blk])
    return pl.pallas_call(
        body, grid_spec=grid_spec,
        out_shape=[jax.ShapeDtypeStruct((N_CHIPS, hr, cols), F32), jax.ShapeDtypeStruct((N_CHIPS, hr, cols), BF16)],
        name=name, compiler_params=_params())(core, g, recv)


def _block_half_total(name, chip_sums, recv, chip_core, after):
    _, hr, cols = chip_sums.shape
    tr = _elementwise_rows((hr, cols))

    def body(cc_ref, p_ref, r_ref, after_ref, o_ref):
        s = p_ref[0]
        for k in range(3):
            s = s + r_ref[k].astype(F32)
        o_ref[0] = s

    grid_spec = pltpu.PrefetchScalarGridSpec(
        num_scalar_prefetch=1, grid=(hr // tr,),
        in_specs=[pl.BlockSpec((1, tr, cols), lambda i, cc_ref: (cc_ref[0], i, 0)),
                  pl.BlockSpec((3, tr, cols), lambda i, cc_ref: (0, i, 0)), ANY],
        out_specs=pl.BlockSpec((1, tr, cols), lambda i, cc_ref: (cc_ref[1], i, 0)))
    return pl.pallas_call(
        body, grid_spec=grid_spec, out_shape=jax.ShapeDtypeStruct((2, hr, cols), F32),
        name=name, compiler_params=_params())(chip_core, chip_sums, recv, after)


def _place_shard(name, w, chip, dtype):
    r, c = w.shape
    tr = _elementwise_rows((r, c))

    def body(chip_ref, w_ref, o_ref):
        o_ref[0] = w_ref[...].astype(dtype)

    grid_spec = pltpu.PrefetchScalarGridSpec(
        num_scalar_prefetch=1, grid=(r // tr,),
        in_specs=[pl.BlockSpec((tr, c), lambda i, chip_ref: (i, 0))],
        out_specs=pl.BlockSpec((1, tr, c), lambda i, chip_ref: (chip_ref[0], i, 0)))
    return pl.pallas_call(
        body, grid_spec=grid_spec, out_shape=jax.ShapeDtypeStruct((N_CHIPS, r, c), dtype),
        name=name, compiler_params=_params())(chip, w)


def _sum_slots(name, slots, own, my_idx):
    k, r, c = slots.shape

    def body(idx_ref, s_ref, own_ref, o_ref):
        s = None
        for j in range(k):
            term = jnp.where(idx_ref[0] == j, own_ref[...], s_ref[j])
            s = term if s is None else s + term
        o_ref[...] = s

    grid_spec = pltpu.PrefetchScalarGridSpec(
        num_scalar_prefetch=1, grid=(1,),
        in_specs=[pl.BlockSpec((k, r, c), lambda i, idx_ref: (0, 0, 0)),
                  pl.BlockSpec((r, c), lambda i, idx_ref: (0, 0))],
        out_specs=pl.BlockSpec((r, c), lambda i, idx_ref: (0, 0)))
    return pl.pallas_call(body, grid_spec=grid_spec, out_shape=jax.ShapeDtypeStruct((r, c), F32), name=name,
                          compiler_params=_params())(my_idx, slots, own)


def _mesh_pos():
    return lax.axis_index("x"), lax.axis_index("y"), lax.axis_index("c")


def _other_chips(x, y):
    return [(1 - x, y), (x, 1 - y), (1 - x, 1 - y)]


def _gather_weights(bufs):
    n = len(bufs)
    half = [b.shape[1] // 2 for b in bufs]

    def body(*refs):
        gathered = refs[n:2 * n]
        ici_send, ici_recv, d2d_send, d2d_recv = refs[2 * n:]
        x, y, c = _mesh_pos()
        me = 2 * x + y
        chips = _other_chips(x, y)

        def part(a, block, core):
            return gathered[a].at[block, pl.ds(core * half[a], half[a])]

        def over_ici(a, k, block):
            px, py = chips[k]
            return pltpu.make_async_remote_copy(
                src_ref=part(a, block, c), dst_ref=part(a, block, c),
                send_sem=ici_send.at[a, k], recv_sem=ici_recv.at[a, k],
                device_id=(px, py, c), device_id_type=MESH)

        def over_d2d(a, k, core):
            px, py = chips[k]
            return pltpu.make_async_remote_copy(
                src_ref=part(a, 2 * px + py, core), dst_ref=part(a, 2 * px + py, core),
                send_sem=d2d_send.at[a, k], recv_sem=d2d_recv.at[a, k],
                device_id=(x, y, 1 - c), device_id_type=MESH)

        for a in range(n):
            for k in range(3):
                over_ici(a, k, me).start()
        for a in range(n):
            for k, (px, py) in enumerate(chips):
                over_ici(a, k, 2 * px + py).wait_recv()
                over_d2d(a, k, c).start()
        for a in range(n):
            for k in range(3):
                over_d2d(a, k, 1 - c).wait_recv()
        for a in range(n):
            for k in range(3):
                over_ici(a, k, me).wait_send()
                over_d2d(a, k, c).wait_send()

    return pl.pallas_call(
        body, in_specs=[ANY] * n, out_specs=[ANY] * n,
        out_shape=[jax.ShapeDtypeStruct(b.shape, b.dtype) for b in bufs],
        input_output_aliases={a: a for a in range(n)},
        scratch_shapes=[pltpu.SemaphoreType.DMA((n, 3))] * 4,
        name="gather_weights")(*bufs)


def _gather_in_proj(h, bufs, order):
    n = len(bufs)
    half = [b.shape[1] // 2 for b in bufs]
    lp, d = h.shape
    ncol = bufs[0].shape[2]
    tm = _mm_row_tile(lp)
    n_row = lp // tm

    def body(order_ref, h_ref, *refs):
        gathered = refs[n:2 * n]
        o_ref, wt_ref = refs[2 * n], refs[2 * n + 1]
        w_buf, ici_send, ici_recv, d2d_send, d2d_recv, w_sem = refs[2 * n + 2:]
        j = pl.program_id(0)
        i = pl.program_id(1)
        x, y, c = _mesh_pos()
        me = 2 * x + y
        chips = _other_chips(x, y)

        def part(a, block, core):
            return gathered[a].at[block, pl.ds(core * half[a], half[a])]

        def over_ici(a, k, block):
            px, py = chips[k]
            return pltpu.make_async_remote_copy(
                src_ref=part(a, block, c), dst_ref=part(a, block, c),
                send_sem=ici_send.at[a, k], recv_sem=ici_recv.at[a, k],
                device_id=(px, py, c), device_id_type=MESH)

        def over_d2d(a, k, core):
            px, py = chips[k]
            return pltpu.make_async_remote_copy(
                src_ref=part(a, 2 * px + py, core), dst_ref=part(a, 2 * px + py, core),
                send_sem=d2d_send.at[a, k], recv_sem=d2d_recv.at[a, k],
                device_id=(x, y, 1 - c), device_id_type=MESH)

        @pl.when(jnp.logical_and(j == 0, i == 0))
        def _():
            for a in range(n):
                for k in range(2):
                    over_ici(a, k, me).start()

        for k, (px, py) in enumerate(chips):
            @pl.when(jnp.logical_and(j == k + 1, i == 0))
            def _(k=k, px=px, py=py):
                for a in range(n):
                    over_ici(a, k, 2 * px + py).wait_recv()
                    over_d2d(a, k, c).start()
                if k == 0:
                    for a in range(n):
                        over_ici(a, 2, me).start()
                for a in range(n):
                    over_d2d(a, k, 1 - c).wait_recv()

        @pl.when(i == 0)
        def _():
            load = pltpu.make_async_copy(gathered[0].at[order_ref[j]], w_buf, w_sem)
            load.start()
            load.wait()
            wt_ref[0] = w_buf[...].T

        o_ref[...] = jnp.dot(h_ref[...], w_buf[...], preferred_element_type=F32)

        @pl.when(jnp.logical_and(j == N_CHIPS - 1, i == n_row - 1))
        def _():
            for a in range(n):
                for k in range(3):
                    over_ici(a, k, me).wait_send()
                    over_d2d(a, k, c).wait_send()

    grid_spec = pltpu.PrefetchScalarGridSpec(
        num_scalar_prefetch=1, grid=(N_CHIPS, n_row),
        in_specs=[pl.BlockSpec((tm, d), lambda j, i, order_ref: (i, 0))] + [ANY] * n,
        out_specs=[ANY] * n + [pl.BlockSpec((tm, ncol), lambda j, i, order_ref: (i, order_ref[j])),
                               pl.BlockSpec((1, ncol, d), lambda j, i, order_ref: (order_ref[j], 0, 0))],
        scratch_shapes=[pltpu.VMEM((d, ncol), BF16)] + [pltpu.SemaphoreType.DMA((n, 3))] * 4
        + [pltpu.SemaphoreType.DMA])
    out = pl.pallas_call(
        body, grid_spec=grid_spec,
        out_shape=[jax.ShapeDtypeStruct(b.shape, b.dtype) for b in bufs]
        + [jax.ShapeDtypeStruct((lp, N_CHIPS * ncol), F32), jax.ShapeDtypeStruct((N_CHIPS, ncol, d), BF16)],
        input_output_aliases={2 + a: a for a in range(n)},
        name="gather_in_proj", compiler_params=_params())(order, h, *bufs)
    return out[n], out[n + 1], out[:n]


def _send_other_halves(grads, tag):
    n = len(grads)

    def body(*refs):
        srcs = refs[:n]
        dsts = refs[n:2 * n]
        send_sems, recv_sems = refs[2 * n:]
        x, y, c = _mesh_pos()
        copies = [pltpu.make_async_remote_copy(
            src_ref=srcs[a].at[j, 1 - c], dst_ref=dsts[a].at[j], send_sem=send_sems.at[a, j],
            recv_sem=recv_sems.at[a, j], device_id=(x, y, 1 - c), device_id_type=MESH)
            for a in range(n) for j in range(N_CHIPS)]
        for cp in copies:
            cp.start()
        for cp in copies:
            cp.wait()

    return pl.pallas_call(
        body, in_specs=[ANY] * n, out_specs=[ANY] * n,
        out_shape=[jax.ShapeDtypeStruct((N_CHIPS,) + g.shape[2:], g.dtype) for g in grads],
        scratch_shapes=[pltpu.SemaphoreType.DMA((n, N_CHIPS))] * 2,
        name="send_other_halves_" + tag)(*grads)


HBM = pl.BlockSpec(memory_space=pltpu.HBM)
SEM = pl.BlockSpec(memory_space=pltpu.SEMAPHORE)


def _block_copies(n, srcs, dsts, send_sems, recv_sems):
    x, y, c = _mesh_pos()
    return [pltpu.make_async_remote_copy(
        src_ref=srcs[a].at[2 * px + py], dst_ref=dsts[a].at[k], send_sem=send_sems.at[3 * a + k],
        recv_sem=recv_sems.at[3 * a + k], device_id=(px, py, c), device_id_type=MESH)
        for a in range(n) for k, (px, py) in enumerate(_other_chips(x, y))]


def _exchange_start(blocked, tag):
    n = len(blocked)
    lands = [lax.empty((3,) + b.shape[1:], b.dtype) for b in blocked]
    bufs = [pltpu.with_memory_space_constraint(b, pltpu.HBM) for b in list(blocked) + lands]
    nb = 2 * n

    def body(*refs):
        for cp in _block_copies(n, refs[:n], refs[n:nb], refs[nb], refs[nb + 1]):
            cp.start()
        refs[-1][...] = jnp.zeros_like(refs[-1])

    out = pl.pallas_call(
        body, name="exchange_start_" + tag,
        in_specs=[HBM] * nb,
        out_shape=[pltpu.SemaphoreType.DMA((3 * n,)), pltpu.SemaphoreType.DMA((3 * n,))]
        + [pltpu.HBM(b.shape, b.dtype) for b in bufs] + [jax.ShapeDtypeStruct((8, 128), F32)],
        out_specs=[SEM] * 2 + [HBM] * nb + [pl.BlockSpec(memory_space=pltpu.VMEM)],
        input_output_aliases={i: 2 + i for i in range(nb)},
        compiler_params=pltpu.CompilerParams(has_side_effects=pltpu.SideEffectType.DATAFLOW_SIDE_EFFECTING),
    )(*bufs)
    return (out[:2], out[2:2 + nb]), out[-1]


def _exchange_wait(state, after, tag):
    sems, bufs = state
    nb = len(bufs)
    n = nb // 2

    def body(*refs):
        for cp in _block_copies(n, refs[:n], refs[n:nb], refs[nb], refs[nb + 1]):
            cp.wait_send()
            cp.wait_recv()

    out = pl.pallas_call(
        body, name="exchange_wait_" + tag,
        in_specs=[HBM] * nb + [SEM] * 2 + [ANY],
        out_shape=[pltpu.HBM(b.shape, b.dtype) for b in bufs],
        out_specs=[HBM] * nb,
        input_output_aliases={i: i for i in range(nb)},
        compiler_params=pltpu.CompilerParams(has_side_effects=pltpu.SideEffectType.DATAFLOW_SIDE_EFFECTING),
    )(*bufs, *sems, after)
    return out[n:nb]


def _whole_block_copies(buf, send_sems, recv_sems, incoming):
    x, y, c = _mesh_pos()
    me = 2 * x + y
    out = []
    for k, (px, py) in enumerate(_other_chips(x, y)):
        block = 2 * px + py if incoming else me
        out.append(pltpu.make_async_remote_copy(
            src_ref=buf.at[block], dst_ref=buf.at[block], send_sem=send_sems.at[k], recv_sem=recv_sems.at[k],
            device_id=(px, py, c), device_id_type=MESH))
    return out


def _gather_start(buf, after, tag):
    buf = pltpu.with_memory_space_constraint(buf, pltpu.HBM)

    def body(buf_ref, after_ref, send_sems, recv_sems, thru_ref, token):
        for cp in _whole_block_copies(buf_ref, send_sems, recv_sems, incoming=False):
            cp.start()
        token[...] = jnp.zeros_like(token)

    out = pl.pallas_call(
        body, name="gather_start_" + tag,
        in_specs=[HBM, ANY],
        out_shape=[pltpu.SemaphoreType.DMA((3,)), pltpu.SemaphoreType.DMA((3,)), pltpu.HBM(buf.shape, buf.dtype),
                   jax.ShapeDtypeStruct((8, 128), F32)],
        out_specs=[SEM, SEM, HBM, pl.BlockSpec(memory_space=pltpu.VMEM)],
        input_output_aliases={0: 2},
        compiler_params=pltpu.CompilerParams(has_side_effects=pltpu.SideEffectType.DATAFLOW_SIDE_EFFECTING),
    )(buf, after)
    return out[:3], out[3]


def _gather_wait(state, after, tag):
    send_sems, recv_sems, buf = state

    def body(buf_ref, send_ref, recv_ref, after_ref, out_ref):
        for cp in _whole_block_copies(buf_ref, send_ref, recv_ref, incoming=True):
            cp.wait_send()
            cp.wait_recv()

    return pl.pallas_call(
        body, name="gather_wait_" + tag,
        in_specs=[HBM, SEM, SEM, ANY],
        out_shape=pltpu.HBM(buf.shape, buf.dtype), out_specs=HBM,
        input_output_aliases={0: 0},
        compiler_params=pltpu.CompilerParams(has_side_effects=pltpu.SideEffectType.DATAFLOW_SIDE_EFFECTING),
    )(buf, send_sems, recv_sems, after)


def _small_copies(small_ref, slots_ref, send_sems, recv_sems, incoming):
    x, y, c = _mesh_pos()
    out = []
    for r in range(1, 8):
        px = 1 - x if r & 4 else x
        py = 1 - y if r & 2 else y
        pc = 1 - c if r & 1 else c
        slot = 4 * px + 2 * py + pc if incoming else 4 * x + 2 * y + c
        out.append(pltpu.make_async_remote_copy(
            src_ref=small_ref, dst_ref=slots_ref.at[slot], send_sem=send_sems.at[r - 1],
            recv_sem=recv_sems.at[r - 1], device_id=(px, py, pc), device_id_type=MESH))
    return out


def _small_start(small):
    bufs = [pltpu.with_memory_space_constraint(b, pltpu.HBM)
            for b in (small, lax.empty((8,) + small.shape, small.dtype))]

    def body(small_ref, slots_ref, send_sems, recv_sems, small_thru, slots_thru, token):
        for cp in _small_copies(small_ref, slots_ref, send_sems, recv_sems, incoming=False):
            cp.start()
        token[...] = jnp.zeros_like(token)

    out = pl.pallas_call(
        body, name="small_start",
        in_specs=[HBM, HBM],
        out_shape=[pltpu.SemaphoreType.DMA((7,)), pltpu.SemaphoreType.DMA((7,))]
        + [pltpu.HBM(b.shape, b.dtype) for b in bufs] + [jax.ShapeDtypeStruct((8, 128), F32)],
        out_specs=[SEM, SEM, HBM, HBM, pl.BlockSpec(memory_space=pltpu.VMEM)],
        input_output_aliases={0: 2, 1: 3},
        compiler_params=pltpu.CompilerParams(has_side_effects=pltpu.SideEffectType.DATAFLOW_SIDE_EFFECTING),
    )(*bufs)
    return out[:4], out[4]


def _small_wait(state, after):
    send_sems, recv_sems, small, slots = state

    def body(small_ref, slots_ref, send_ref, recv_ref, after_ref, small_out, slots_out):
        for cp in _small_copies(small_ref, slots_ref, send_ref, recv_ref, incoming=True):
            cp.wait_send()
            cp.wait_recv()

    return pl.pallas_call(
        body, name="small_wait",
        in_specs=[HBM, HBM, SEM, SEM, ANY],
        out_shape=[pltpu.HBM(small.shape, small.dtype), pltpu.HBM(slots.shape, slots.dtype)],
        out_specs=[HBM, HBM], input_output_aliases={0: 0, 1: 1},
        compiler_params=pltpu.CompilerParams(has_side_effects=pltpu.SideEffectType.DATAFLOW_SIDE_EFFECTING),
    )(small, slots, send_sems, recv_sems, after)[1]


def _join_halves(bufs):
    n = len(bufs)

    def body(*refs):
        joined = refs[n:2 * n]
        send_sems, recv_sems = refs[2 * n:]
        x, y, c = _mesh_pos()
        for a in range(n):
            pltpu.make_async_remote_copy(
                src_ref=joined[a].at[c], dst_ref=joined[a].at[c], send_sem=send_sems.at[a],
                recv_sem=recv_sems.at[a], device_id=(x, y, 1 - c), device_id_type=MESH).start()
        for a in range(n):
            pltpu.make_async_remote_copy(
                src_ref=joined[a].at[c], dst_ref=joined[a].at[1 - c], send_sem=send_sems.at[a],
                recv_sem=recv_sems.at[a], device_id=(x, y, 1 - c), device_id_type=MESH).wait()

    return pl.pallas_call(
        body, in_specs=[ANY] * n, out_specs=[ANY] * n,
        out_shape=[jax.ShapeDtypeStruct(b.shape, b.dtype) for b in bufs],
        input_output_aliases={a: a for a in range(n)},
        scratch_shapes=[pltpu.SemaphoreType.DMA((n,))] * 2,
        name="join_halves")(*bufs)


def kernel(x, meta_tokens, norm_g, w_in, conv_w, conv_b, ln_g, ln_b, w_conv_out, lb_logits, gnorm_g, w_rec_out, w_out, final_g, loss_target, m_meta_tokens, m_norm_g, m_w_in, m_conv_w, m_conv_b, m_ln_g, m_ln_b, m_w_conv_out, m_lb_logits, m_gnorm_g, m_w_rec_out, m_w_out, m_final_g, v_meta_tokens, v_norm_g, v_w_in, v_conv_w, v_conv_b, v_ln_g, v_ln_b, v_w_conv_out, v_lb_logits, v_gnorm_g, v_w_rec_out, v_w_out, v_final_g):
    d = x.shape[2]
    n_meta = meta_tokens.shape[0]
    n_pad = CHUNK - n_meta
    ds = d // N_CHIPS
    chip = 2 * lax.axis_index("x") + lax.axis_index("y")

    conv_w_pad = jnp.pad(conv_w[0], ((0, HALO - CONV_WIDTH), (0, 0)))
    chip_idx = chip.astype(jnp.int32).reshape(1)
    (small_g,) = _gather_weights([
        _place_shard("place_small", jnp.concatenate([conv_w_pad, meta_tokens], axis=0), chip_idx, F32)])
    cw_full = jnp.transpose(small_g[:, 0:HALO], (1, 0, 2)).reshape(HALO, d)
    meta_full = jnp.transpose(small_g[:, HALO:HALO + n_meta], (1, 0, 2)).reshape(n_meta, d)

    hres = jnp.concatenate([jnp.zeros((n_pad, d), F32), meta_full, x[0]], axis=0)
    target = loss_target[0]
    final_g2 = final_g.reshape(1, d)
    h = _rmsnorm_fwd(hres, norm_g)
    fx, fy = 1 - lax.axis_index("x"), 1 - lax.axis_index("y")
    order = jnp.stack([chip, 2 * fx + (1 - fy), 2 * (1 - fx) + fy, 2 * fx + fy]).astype(jnp.int32)
    proj, win_t, _ = _gather_in_proj(h, [_place_shard("place_w_in", w_in[0], chip_idx, BF16)], order)
    sq_own = _place_shard("place_square", jnp.concatenate([w_conv_out[0], w_rec_out[0], w_out[0]], axis=0),
                          chip_idx, BF16)
    sq_flight, sq_token = _gather_start(sq_own, proj, "square")
    o, s_all = _hgrn_fwd(proj, lb_logits + sq_token[0:1, 0:1], n_pad)
    sq_g = _gather_wait(sq_flight, s_all, "square")
    wc_full = sq_g[:, 0:ds].reshape(d, d)
    wr_full = sq_g[:, ds:2 * ds].reshape(d, d)
    wo_full = sq_g[:, 2 * ds:3 * ds].reshape(d, d)
    c, yc_in, y_conv = _conv_fwd(proj, cw_full, conv_b, ln_g, ln_b, wc_full)
    yr_in, merged, y_rec, dout, loss_acc, dfinal_g = _tail_fwd(
        o, proj, y_conv, hres, target, gnorm_g, final_g2, wr_full, wo_full)

    (dyc, dyr, dout_bf, dz, dproj, do, dc, dgnorm_g, dln_g, dln_b) = _tail_bwd(
        dout, proj, y_conv, y_rec, o, c, wo_full, wr_full, wc_full, ln_g, ln_b, gnorm_g)
    g_wc = _weight_grad(yc_in, dyc, "grad_w_conv_out", False)
    g_wr = _weight_grad(yr_in, dyr, "grad_w_rec_out", False)
    g_wo = _weight_grad(merged, dout_bf, "grad_w_out", False)

    core = lax.axis_index("c").astype(jnp.int32).reshape(1)

    def chip_sum_and_start(g, tag, g_to_sibling=None):
        halves = lambda a: a.reshape(N_CHIPS, 2, a.shape[1] // 2, a.shape[2])
        g = halves(g)
        (from_sibling,) = _send_other_halves([g if g_to_sibling is None else halves(g_to_sibling)], tag)
        sums = _chip_half_sum("chip_half_sum_" + tag, g, from_sibling, core)
        in_flight, token = _exchange_start([sums[1]], tag)
        return sums[0], in_flight, token[0:1, 0:1]

    g_sq = jnp.concatenate([g.reshape(N_CHIPS, ds, d) for g in (g_wc, g_wr, g_wo)], axis=1)
    sum_sq, flight_sq, token_sq = chip_sum_and_start(g_sq, "square")
    dproj, dlb_logits = _hgrn_bwd(proj, do, s_all, lb_logits + token_sq, n_pad, dproj)
    dproj, dconv_w, dconv_b = _conv_bwd(dc, proj, cw_full, dz, dproj)
    (recv_sq,) = _exchange_wait(flight_sq, dconv_b, "square")
    g_win, g_win_bf = _weight_grad(h, dproj, "grad_w_in", True)
    sum_win, flight_win, token_win = chip_sum_and_start(g_win, "w_in", g_win_bf)
    dhres, dnorm_g = _in_proj_bwd(dproj, win_t, hres, norm_g + token_win, dout)
    grad_x = dhres[CHUNK:][None]
    (recv_win,) = _exchange_wait(flight_win, dnorm_g, "w_in")
    small = jnp.concatenate([dnorm_g, dconv_b, dln_g, dln_b, dlb_logits, dgnorm_g, dfinal_g,
                             dhres[n_pad:CHUNK], dconv_w[:CONV_WIDTH],
                             jnp.broadcast_to(loss_acc[0:1, 0:1], (1, d))], axis=0)
    small_flight, small_token = _small_start(small)
    chip_core = jnp.concatenate([chip_idx, core])
    totals = [_block_half_total("block_half_total_" + nm, s, r, chip_core, small_token)
              for nm, s, r in zip(("w_in", "square"), (sum_win, sum_sq), (recv_win, recv_sq))]
    joined = _join_halves(totals)
    gt_win, gt_sq = [t.reshape(2 * t.shape[1], t.shape[2]) for t in joined]
    small_slots = _small_wait(small_flight, joined[1])
    device_idx = (2 * chip_idx + core).astype(jnp.int32)
    small_sum = _sum_slots("sum_small", small_slots, small, device_idx)

    res = {}
    res["w_in"] = _adamw("adamw_w_in", w_in[0], m_w_in[0], v_w_in[0], gt_win)
    res["w_conv_out"] = _adamw("adamw_w_conv_out", w_conv_out[0], m_w_conv_out[0], v_w_conv_out[0], gt_sq[0:ds])
    res["w_rec_out"] = _adamw("adamw_w_rec_out", w_rec_out[0], m_w_rec_out[0], v_w_rec_out[0], gt_sq[ds:2 * ds])
    res["w_out"] = _adamw("adamw_w_out", w_out[0], m_w_out[0], v_w_out[0], gt_sq[2 * ds:3 * ds])
    big = {k: tuple(a[None] for a in v) for k, v in res.items()}

    rep_names = ("norm_g", "conv_b", "ln_g", "ln_b", "lb_logits", "gnorm_g", "final_g")
    rep_w = (norm_g, conv_b, ln_g, ln_b, lb_logits, gnorm_g, final_g2)
    rep_m = (m_norm_g, m_conv_b, m_ln_g, m_ln_b, m_lb_logits, m_gnorm_g, m_final_g.reshape(1, d))
    rep_v = (v_norm_g, v_conv_b, v_ln_g, v_ln_b, v_lb_logits, v_gnorm_g, v_final_g.reshape(1, d))
    rep = _adamw("adamw_replicated", jnp.concatenate(rep_w, 0), jnp.concatenate(rep_m, 0),
                 jnp.concatenate(rep_v, 0), small_sum[0:8])
    rep_rows = {"norm_g": (0, 1), "conv_b": (1, 2), "ln_g": (2, 3), "ln_b": (3, 4), "lb_logits": (4, 6),
                "gnorm_g": (6, 7), "final_g": (7, 8)}
    small_out = {}
    for nm in rep_names:
        lo, hi = rep_rows[nm]
        vals = tuple(a[lo:hi] for a in rep)
        if nm == "final_g":
            vals = tuple(a.reshape(d) for a in vals)
        small_out[nm] = vals
    cw_row = 8 + n_meta
    g_meta = lax.dynamic_slice_in_dim(small_sum[8:cw_row], chip * ds, ds, axis=1)
    small_out["meta_tokens"] = _adamw("adamw_meta", meta_tokens, m_meta_tokens, v_meta_tokens, g_meta)
    g_cw = lax.dynamic_slice_in_dim(small_sum[cw_row:cw_row + HALO], chip * ds, ds, axis=1)
    pad_rows = ((0, HALO - CONV_WIDTH), (0, 0))
    cw_res = _adamw("adamw_conv_w", conv_w_pad, jnp.pad(m_conv_w[0], pad_rows),
                    jnp.pad(v_conv_w[0], pad_rows, constant_values=1.0), g_cw)
    small_out["conv_w"] = tuple(a[:CONV_WIDTH][None] for a in cw_res)

    loss = small_sum[cw_row + HALO - 1, 0]

    order = ("meta_tokens", "norm_g", "w_in", "conv_w", "conv_b", "ln_g", "ln_b", "w_conv_out", "lb_logits",
             "gnorm_g", "w_rec_out", "w_out", "final_g")
    allres = {**big, **small_out}
    outs = [loss, grad_x]
    for field in range(4):
        outs.extend(allres[nm][field] for nm in order)
    return tuple(outs)
```

```python
import numpy as np

import jax
import jax.numpy as jnp
from jax import lax
from jax.experimental import pallas as pl
from jax.experimental.pallas import tpu as pltpu

F32 = jnp.float32
BF16 = jnp.bfloat16

EPS = 1e-6
CHUNK = 64
N_LEVELS = 6
FIRST_TABLE_LEVEL = 6
CONV_WIDTH = 31
HALO = 32
CONV_ROWS = 64
CONV_LANES = 128
SUBLANES = 8
HEAD = 128
W_IN_COL_TILES = 1
HEADS_PER_TRIP = 8
N_CHIPS = 4
VMEM_LIMIT_BYTES = 56 * 1024 * 1024
ELEMENTWISE_BLOCK_BYTES = 3 * 1024 * 1024

ADAM_LR = 0.001
ADAM_B1 = 0.9
ADAM_B2 = 0.999
ADAM_EPS = 1e-08
ADAM_WD = 0.01
ADAM_STEP = 10

MESH = pl.DeviceIdType.MESH
ANY = pl.BlockSpec(memory_space=pl.ANY)

NT = (((1,), (1,)), ((), ()))
TN = (((0,), (0,)), ((), ()))


def _params(**kw):
    return pltpu.CompilerParams(vmem_limit_bytes=VMEM_LIMIT_BYTES, **kw)


def _sigmoid(x):
    return jax.nn.sigmoid(x)


def _dsilu(x, s):
    return s * (1.0 + x * (1.0 - s))


def _row_tile(lp):
    for t in (320, 256, 192, 128, 64):
        if lp % t == 0:
            return t
    raise ValueError(f"unsupported padded length {lp}")


def _mm_row_tile(lp):
    for t in (832, 640, 320, 256, 192, 128, 64):
        if lp % t == 0:
            return t
    raise ValueError(f"unsupported padded length {lp}")


def _dot3(m_bf16, x):
    hi = x.astype(BF16)
    r1 = x - hi.astype(F32)
    mid = r1.astype(BF16)
    lo = (r1 - mid.astype(F32)).astype(BF16)
    return (jnp.dot(m_bf16, hi, preferred_element_type=F32)
            + jnp.dot(m_bf16, mid, preferred_element_type=F32)
            + jnp.dot(m_bf16, lo, preferred_element_type=F32))


def _dot2(m_bf16, x):
    hi = x.astype(BF16)
    lo = (x - hi.astype(F32)).astype(BF16)
    return (jnp.dot(m_bf16, hi, preferred_element_type=F32)
            + jnp.dot(m_bf16, lo, preferred_element_type=F32))


def _col_to_row(col):
    return jnp.broadcast_to(col, (HEAD, SUBLANES)).T[0:1, :]


def _row_to_col(row):
    return jnp.broadcast_to(row, (SUBLANES, HEAD)).T[:, 0:1]


def _hgrn_tables():
    t = np.arange(CHUNK)
    ltri = (t[None, :] <= t[:, None]).astype(np.float32)
    mats = [ltri]
    for lvl in range(FIRST_TABLE_LEVEL, N_LEVELS + 1):
        blk = CHUNK >> (lvl - 1)
        mid = (t // blk) * blk + blk // 2
        mats.append(ltri[mid - 1])
    after = (t[None, :] >= t[:, None]).astype(np.float32)
    before = (t[None, :] < t[:, None]).astype(np.float32)
    return jnp.asarray(np.concatenate(mats, 0), BF16), jnp.asarray(np.concatenate([after, before], 1), BF16)


def _rmsnorm_fwd(hres, g):
    lp, d = hres.shape
    tm = _mm_row_tile(lp)

    def body(x_ref, g_ref, h_ref):
        x = x_ref[...]
        r = lax.rsqrt(jnp.mean(x * x, axis=-1, keepdims=True) + EPS)
        h_ref[...] = (x * r * g_ref[...]).astype(BF16)

    return pl.pallas_call(
        body, grid=(lp // tm,),
        in_specs=[pl.BlockSpec((tm, d), lambda i: (i, 0)), pl.BlockSpec((1, d), lambda i: (0, 0))],
        out_specs=pl.BlockSpec((tm, d), lambda i: (i, 0)),
        out_shape=jax.ShapeDtypeStruct((lp, d), BF16),
        name="rmsnorm_fwd", compiler_params=_params())(hres, g)


def _conv_fwd(proj, conv_w, conv_b, ln_g, ln_b, w_conv):
    lp = proj.shape[0]
    d = conv_b.shape[1]
    tm = _row_tile(lp)
    hb = tm // HALO

    def body(ua_ref, ub_ref, z_ref, uap_ref, ubp_ref, cw_ref, cb_ref, lg_ref, lb_ref, w_ref,
             c_ref, ycin_ref, yconv_ref, aext_ref):
        i = pl.program_id(0)
        a_prev = uap_ref[...] * _sigmoid(ubp_ref[...])
        aext_ref[0:HALO, :] = jnp.where(i > 0, a_prev, 0.0)
        aext_ref[HALO:HALO + tm, :] = ua_ref[...] * _sigmoid(ub_ref[...])

        def row_block(r, carry):
            r0 = pl.multiple_of(r * CONV_ROWS, CONV_ROWS)
            for cs in range(d // CONV_LANES):
                cl = slice(cs * CONV_LANES, (cs + 1) * CONV_LANES)
                blk = aext_ref[pl.ds(r0, CONV_ROWS + HALO), cl]
                acc = jnp.zeros((CONV_ROWS, CONV_LANES), F32) + cb_ref[:, cl]
                for b in range(SUBLANES):
                    sh = blk if b == 0 else pltpu.roll(blk, CONV_ROWS + HALO - b, axis=0)
                    for a in range(5):
                        j = SUBLANES * a + b - 2
                        if 0 <= j < CONV_WIDTH:
                            acc = acc + cw_ref[j:j + 1, cl] * sh[SUBLANES * a:SUBLANES * a + CONV_ROWS, :]
                c_ref[pl.ds(r0, CONV_ROWS), cl] = acc
            return carry

        lax.fori_loop(0, tm // CONV_ROWS, row_block, 0)

        c = c_ref[...]
        mu = jnp.mean(c, axis=-1, keepdims=True)
        xc = c - mu
        rstd = lax.rsqrt(jnp.mean(xc * xc, axis=-1, keepdims=True) + EPS)
        ln = xc * rstd * lg_ref[...] + lb_ref[...]
        s = ln * _sigmoid(ln)
        z = z_ref[...]
        ycin = (s * (z * _sigmoid(z))).astype(BF16)
        ycin_ref[...] = ycin
        yconv_ref[...] = jnp.dot(ycin, w_ref[...], preferred_element_type=F32)

    row = lambda p: pl.BlockSpec((tm, d), lambda i, p=p: (i, p))
    halo = lambda p: pl.BlockSpec((HALO, d), lambda i, p=p: (jnp.maximum(i * hb - 1, 0), p))
    vec = pl.BlockSpec((1, d), lambda i: (0, 0))
    return pl.pallas_call(
        body, grid=(lp // tm,),
        in_specs=[row(0), row(1), row(2), halo(0), halo(1),
                  pl.BlockSpec((HALO, d), lambda i: (0, 0)), vec, vec, vec,
                  pl.BlockSpec((d, d), lambda i: (0, 0))],
        out_specs=[pl.BlockSpec((tm, d), lambda i: (i, 0))] * 3,
        out_shape=[jax.ShapeDtypeStruct((lp, d), F32), jax.ShapeDtypeStruct((lp, d), BF16),
                   jax.ShapeDtypeStruct((lp, d), F32)],
        scratch_shapes=[pltpu.VMEM((HALO + tm, d), F32)],
        name="conv_fwd", compiler_params=_params())(
            proj, proj, proj, proj, proj, conv_w, conv_b, ln_g, ln_b, w_conv)


def _lower_bound(lbl_ref):
    l0 = lbl_ref[0:1, :]
    l1 = lbl_ref[1:2, :]
    m = jnp.maximum(l0, l1)
    e0 = jnp.exp(l0 - m)
    e1 = jnp.exp(l1 - m)
    p0 = e0 / (e0 + e1)
    return p0, p0 * (e1 / (e0 + e1))


def _level_masks():
    r2 = lax.broadcasted_iota(jnp.int32, (CHUNK, CHUNK), 0)
    c2 = lax.broadcasted_iota(jnp.int32, (CHUNK, CHUNK), 1)
    out = []
    for lvl in range(1, N_LEVELS + 1):
        blk = CHUNK >> (lvl - 1)
        sh = blk.bit_length() - 1
        same = (r2 >> sh) == (c2 >> sh)
        t_upper = (r2 & (blk - 1)) >= (blk // 2)
        s_lower = (c2 & (blk - 1)) < (blk // 2)
        out.append(jnp.logical_and(same, jnp.logical_and(t_upper, s_lower)))
    return out


def _gates(qr, fr, lb, valid):
    sq = _sigmoid(qr)
    q = qr * sq
    sf = _sigmoid(fr)
    f = lb + (1.0 - lb) * sf
    g = jnp.where(valid, jnp.log(f), 0.0)
    k = jnp.where(valid, 1.0 - f, 0.0)
    return q, sq, f, sf, g, k


def _level_reference(lvl, b, t_ref, hs):
    if lvl >= FIRST_TABLE_LEVEL:
        base = CHUNK * (lvl - FIRST_TABLE_LEVEL + 1)
        return t_ref[base:base + CHUNK, hs]
    blk = CHUNK >> (lvl - 1)
    rows = [jnp.broadcast_to(b[m + blk // 2 - 1:m + blk // 2, :], (blk, HEAD)) for m in range(0, CHUNK, blk)]
    return rows[0] if len(rows) == 1 else jnp.concatenate(rows, axis=0)


def _level_factor(b, r):
    d = b - r
    return jnp.exp(jnp.minimum(d, -d))


def _hgrn_fwd(proj, lb_logits, n_pad):
    lp = proj.shape[0]
    d = lb_logits.shape[1]
    n_heads = d // HEAD
    nc = lp // CHUNK
    tab, _ = _hgrn_tables()
    n_tab = tab.shape[0]

    def body(qr_ref, fr_ref, ir_ref, lbl_ref, tab_ref, o_ref, sall_ref, s_ref, t_ref):
        n = pl.program_id(0)

        @pl.when(n == 0)
        def _():
            s_ref[...] = jnp.zeros_like(s_ref)

        sall_ref[0] = s_ref[...]
        lb_all, _ = _lower_bound(lbl_ref)
        rid = lax.broadcasted_iota(jnp.int32, (CHUNK, 1), 0)
        valid = jnp.logical_or(n > 0, rid >= n_pad)
        f_all = lb_all + (1.0 - lb_all) * _sigmoid(fr_ref[...])
        t_ref[...] = _dot2(tab_ref[...], jnp.where(valid, jnp.log(f_all), 0.0))
        masks = _level_masks()

        def head(h):
            off = h * HEAD if isinstance(h, int) else pl.multiple_of(h * HEAD, HEAD)
            hs = pl.ds(off, HEAD)
            lb = _lower_bound_slice(lbl_ref, hs)
            q, _, _, _, _, k = _gates(qr_ref[:, hs], fr_ref[:, hs], lb, valid)
            v = ir_ref[:, hs]
            b = t_ref[0:CHUNK, hs]
            s0 = s_ref[hs, :]
            o = jnp.dot((q * jnp.exp(b)).astype(BF16), s0.astype(BF16), preferred_element_type=F32)
            o = o + jnp.sum(q * k, axis=-1, keepdims=True) * v
            a = jnp.zeros((CHUNK, CHUNK), F32)
            for lvl in range(1, N_LEVELS + 1):
                e = _level_factor(b, _level_reference(lvl, b, t_ref, hs))
                p = lax.dot_general((q * e).astype(BF16), (k * e).astype(BF16), NT, preferred_element_type=F32)
                a = a + jnp.where(masks[lvl - 1], p, 0.0)
            vb = v.astype(BF16)
            o_ref[:, hs] = o + jnp.dot(a.astype(BF16), vb, preferred_element_type=F32)
            b_last = t_ref[CHUNK - 1:CHUNK, hs]
            khat = (k * jnp.exp(b_last - b)).astype(BF16)
            s_ref[hs, :] = _row_to_col(jnp.exp(b_last)) * s0 + lax.dot_general(khat, vb, TN, preferred_element_type=F32)
        per_trip = min(HEADS_PER_TRIP, n_heads)

        def head_group(p, carry):
            for u in range(per_trip):
                head(p * per_trip + u)
            return carry

        if n_heads == per_trip:
            head_group(0, 0)
        else:
            lax.fori_loop(0, n_heads // per_trip, head_group, 0)

    piece = lambda p: pl.BlockSpec((CHUNK, d), lambda n, p=p: (n, p))
    return pl.pallas_call(
        body, grid=(nc,),
        in_specs=[piece(3), piece(4), piece(5), pl.BlockSpec((2, d), lambda n: (0, 0)),
                  pl.BlockSpec((n_tab, CHUNK), lambda n: (0, 0))],
        out_specs=[pl.BlockSpec((CHUNK, d), lambda n: (n, 0)), pl.BlockSpec((1, d, HEAD), lambda n: (n, 0, 0))],
        out_shape=[jax.ShapeDtypeStruct((lp, d), F32), jax.ShapeDtypeStruct((nc, d, HEAD), F32)],
        scratch_shapes=[pltpu.VMEM((d, HEAD), F32), pltpu.VMEM((n_tab, d), F32)],
        name="hgrn_fwd", compiler_params=_params())(proj, proj, proj, lb_logits, tab)


def _lower_bound_slice(lbl_ref, hs):
    l0 = lbl_ref[0:1, hs]
    l1 = lbl_ref[1:2, hs]
    m = jnp.maximum(l0, l1)
    e0 = jnp.exp(l0 - m)
    e1 = jnp.exp(l1 - m)
    return e0 / (e0 + e1)


def _tail_fwd(o, proj, y_conv, hres, target, gnorm_g, final_g, w_rec, w_out):
    lp, d = o.shape
    n_heads = d // HEAD
    tm = _row_tile(lp)

    n_slabs = tm // CHUNK

    def body(o_ref, gr_ref, mc_ref, mr_ref, yc_ref, x_ref, gn_ref, fg_ref, wr_ref, wo_ref, *rest):
        t_refs = rest[:n_slabs]
        yrin_ref, mg_ref, yrec_ref, dout_ref, loss_ref, dfg_ref = rest[n_slabs:]
        i = pl.program_id(0)

        @pl.when(i == 0)
        def _():
            loss_ref[...] = jnp.zeros_like(loss_ref)
            dfg_ref[...] = jnp.zeros_like(dfg_ref)

        for h in range(n_heads):
            hs = slice(h * HEAD, (h + 1) * HEAD)
            oh = o_ref[:, hs]
            on = oh * lax.rsqrt(jnp.mean(oh * oh, axis=-1, keepdims=True) + EPS) * gn_ref[:, hs]
            gr = gr_ref[:, hs]
            yrin_ref[:, hs] = (on * (gr * _sigmoid(gr))).astype(BF16)
        yrec = jnp.dot(yrin_ref[...], wr_ref[...], preferred_element_type=F32)
        yrec_ref[...] = yrec
        merged = (_sigmoid(mc_ref[...]) * yc_ref[...] + _sigmoid(mr_ref[...]) * yrec).astype(BF16)
        mg_ref[...] = merged
        out = x_ref[...] + jnp.dot(merged, wo_ref[...], preferred_element_type=F32)
        r = lax.rsqrt(jnp.mean(out * out, axis=-1, keepdims=True) + EPS)
        yhat = out * r
        fg = fg_ref[...]
        rid = lax.broadcasted_iota(jnp.int32, (tm, 1), 0) + i * tm
        tgt = jnp.concatenate([t[...] for t in t_refs], axis=0)
        err = jnp.where(rid >= CHUNK, yhat * fg - tgt, 0.0)
        loss_ref[...] += 0.5 * jnp.sum(err * err) / d
        dy = err / d
        dfg_ref[...] += jnp.sum(dy * yhat, axis=0, keepdims=True)
        dyh = dy * fg
        dout_ref[...] = r * (dyh - yhat * jnp.mean(dyh * yhat, axis=-1, keepdims=True))

    row = lambda p: pl.BlockSpec((tm, d), lambda i, p=p: (i, p))
    vec = pl.BlockSpec((1, d), lambda i: (0, 0))
    mat = pl.BlockSpec((d, d), lambda i: (0, 0))
    return pl.pallas_call(
        body, grid=(lp // tm,),
        in_specs=[row(0), row(6), row(7), row(8), row(0), row(0), vec, vec, mat, mat]
        + [pl.BlockSpec((CHUNK, d), lambda i, u=u: (jnp.maximum(i * n_slabs + u - 1, 0), 0)) for u in range(n_slabs)],
        out_specs=[row(0), row(0), row(0), row(0), pl.BlockSpec((8, 128), lambda i: (0, 0)), vec],
        out_shape=[jax.ShapeDtypeStruct((lp, d), BF16), jax.ShapeDtypeStruct((lp, d), BF16),
                   jax.ShapeDtypeStruct((lp, d), F32), jax.ShapeDtypeStruct((lp, d), F32),
                   jax.ShapeDtypeStruct((8, 128), F32), jax.ShapeDtypeStruct((1, d), F32)],
        name="tail_fwd", compiler_params=_params())(
            o, proj, proj, proj, y_conv, hres, gnorm_g, final_g, w_rec, w_out, *([target] * n_slabs))


def _tail_bwd(dout, proj, y_conv, y_rec, o, c, w_out, w_rec, w_conv, ln_g, ln_b, gnorm_g):
    lp, d = dout.shape
    n_heads = d // HEAD
    tm = _row_tile(lp)

    def body(dout_ref, mc_ref, mr_ref, z_ref, gr_ref, yc_ref, yrec_ref, o_ref, c_ref,
             wo_ref, wr_ref, wc_ref, lg_ref, lb_ref, gn_ref,
             dyc_ref, dyr_ref, doutb_ref, dz_ref, dp_ref, do_ref, dc_ref,
             dgn_ref, dlg_ref, dlb_ref, dyrin_ref):
        i = pl.program_id(0)

        @pl.when(i == 0)
        def _():
            dgn_ref[...] = jnp.zeros_like(dgn_ref)
            dlg_ref[...] = jnp.zeros_like(dlg_ref)
            dlb_ref[...] = jnp.zeros_like(dlb_ref)

        doutb = dout_ref[...].astype(BF16)
        doutb_ref[...] = doutb
        dmerged = lax.dot_general(doutb, wo_ref[...], NT, preferred_element_type=F32)
        smc = _sigmoid(mc_ref[...])
        smr = _sigmoid(mr_ref[...])
        dyc = (dmerged * smc).astype(BF16)
        dyr = (dmerged * smr).astype(BF16)
        dyc_ref[...] = dyc
        dyr_ref[...] = dyr
        dp_ref[:, d:2 * d] = (dmerged * yc_ref[...] * smc * (1.0 - smc)).astype(BF16)
        dp_ref[:, 2 * d:3 * d] = (dmerged * yrec_ref[...] * smr * (1.0 - smr)).astype(BF16)

        dyrin_ref[...] = lax.dot_general(dyr, wr_ref[...], NT, preferred_element_type=F32)
        for h in range(n_heads):
            hs = slice(h * HEAD, (h + 1) * HEAD)
            oh = o_ref[:, hs]
            rstd = lax.rsqrt(jnp.mean(oh * oh, axis=-1, keepdims=True) + EPS)
            ohat = oh * rstd
            gn = gn_ref[:, hs]
            gr = gr_ref[:, hs]
            sg = _sigmoid(gr)
            dyrin = dyrin_ref[:, hs]
            don = dyrin * (gr * sg)
            dp_ref[:, hs] = (dyrin * (ohat * gn) * _dsilu(gr, sg)).astype(BF16)
            dgn_ref[:, hs] += jnp.sum(don * ohat, axis=0, keepdims=True)
            doh = don * gn
            do_ref[:, hs] = rstd * (doh - ohat * jnp.mean(doh * ohat, axis=-1, keepdims=True))

        dycin = lax.dot_general(dyc, wc_ref[...], NT, preferred_element_type=F32)
        c = c_ref[...]
        mu = jnp.mean(c, axis=-1, keepdims=True)
        xc = c - mu
        rstd = lax.rsqrt(jnp.mean(xc * xc, axis=-1, keepdims=True) + EPS)
        nrm = xc * rstd
        lg = lg_ref[...]
        ln = nrm * lg + lb_ref[...]
        sl = _sigmoid(ln)
        z = z_ref[...]
        sz = _sigmoid(z)
        dz_ref[...] = (dycin * (ln * sl) * _dsilu(z, sz)).astype(BF16)
        dln = dycin * (z * sz) * _dsilu(ln, sl)
        dlg_ref[...] += jnp.sum(dln * nrm, axis=0, keepdims=True)
        dlb_ref[...] += jnp.sum(dln, axis=0, keepdims=True)
        dn = dln * lg
        dc_ref[...] = rstd * (dn - jnp.mean(dn, axis=-1, keepdims=True)
                              - nrm * jnp.mean(dn * nrm, axis=-1, keepdims=True))

    row = lambda p: pl.BlockSpec((tm, d), lambda i, p=p: (i, p))
    vec = pl.BlockSpec((1, d), lambda i: (0, 0))
    mat = pl.BlockSpec((d, d), lambda i: (0, 0))
    act_bf = jax.ShapeDtypeStruct((lp, d), BF16)
    act_f32 = jax.ShapeDtypeStruct((lp, d), F32)
    vec_f32 = jax.ShapeDtypeStruct((1, d), F32)
    return pl.pallas_call(
        body, grid=(lp // tm,),
        in_specs=[row(0), row(7), row(8), row(2), row(6), row(0), row(0), row(0), row(0),
                  mat, mat, mat, vec, vec, vec],
        out_specs=[row(0)] * 4 + [pl.BlockSpec((tm, 3 * d), lambda i: (i, 2))] + [row(0)] * 2 + [vec] * 3,
        out_shape=[act_bf] * 4 + [jax.ShapeDtypeStruct((lp, 9 * d), BF16)] + [act_f32] * 2 + [vec_f32] * 3,
        scratch_shapes=[pltpu.VMEM((tm, d), F32)],
        name="tail_bwd", compiler_params=_params())(
            dout, proj, proj, proj, proj, y_conv, y_rec, o, c, w_out, w_rec, w_conv, ln_g, ln_b, gnorm_g)


def _hgrn_bwd(proj, do, s_all, lb_logits, n_pad, dproj):
    lp, d = do.shape
    n_heads = d // HEAD
    nc = lp // CHUNK
    tab, utri = _hgrn_tables()
    n_tab = tab.shape[0]

    def body(qr_ref, fr_ref, ir_ref, do_ref, s0_ref, lbl_ref, tab_ref, ut_ref, _,
             dp_ref, dlbl_ref, ds_ref, t_ref, dlb_ref):
        n = pl.program_id(0)
        chunk = nc - 1 - n

        @pl.when(n == 0)
        def _():
            ds_ref[...] = jnp.zeros_like(ds_ref)
            dlb_ref[...] = jnp.zeros_like(dlb_ref)

        lb_all, pp = _lower_bound(lbl_ref)
        rid = lax.broadcasted_iota(jnp.int32, (CHUNK, 1), 0)
        valid = jnp.logical_or(chunk > 0, rid >= n_pad)
        f_all = lb_all + (1.0 - lb_all) * _sigmoid(fr_ref[...])
        t_ref[...] = _dot2(tab_ref[...], jnp.where(valid, jnp.log(f_all), 0.0))
        masks = _level_masks()
        ut = ut_ref[...]

        def head(h):
            off = h * HEAD if isinstance(h, int) else pl.multiple_of(h * HEAD, HEAD)
            hs = pl.ds(off, HEAD)
            lb = _lower_bound_slice(lbl_ref, hs)
            qr = qr_ref[:, hs]
            q, sq, f, sf, _, k = _gates(qr, fr_ref[:, hs], lb, valid)
            v = ir_ref[:, hs]
            do_h = do_ref[:, hs]
            b = t_ref[0:CHUNK, hs]
            b_last = t_ref[CHUNK - 1:CHUNK, hs]
            s0 = s0_ref[0, hs, :]
            ds1 = ds_ref[hs, :]
            eb = jnp.exp(b)
            ekl = jnp.exp(b_last - b)
            do_bf = do_h.astype(BF16)
            v_bf = v.astype(BF16)
            ds1_bf = ds1.astype(BF16)

            da = lax.dot_general(do_bf, v_bf, NT, preferred_element_type=F32)
            da_diag = jnp.sum(do_h * v, axis=-1, keepdims=True)
            a = jnp.zeros((CHUNK, CHUNK), F32)
            dq_x = eb * lax.dot_general(do_bf, s0.astype(BF16), NT, preferred_element_type=F32)
            dk_x = ekl * lax.dot_general(v_bf, ds1_bf, NT, preferred_element_type=F32)
            x_after = q * dq_x
            x_before = k * dk_x
            for lvl in range(1, N_LEVELS + 1):
                e = _level_factor(b, _level_reference(lvl, b, t_ref, hs))
                qt = (q * e).astype(BF16)
                kt = (k * e).astype(BF16)
                p = lax.dot_general(qt, kt, NT, preferred_element_type=F32)
                a = a + jnp.where(masks[lvl - 1], p, 0.0)
                dam = jnp.where(masks[lvl - 1], da, 0.0).astype(BF16)
                dqt = jnp.dot(dam, kt, preferred_element_type=F32)
                dkt = lax.dot_general(dam, qt, TN, preferred_element_type=F32)
                dq_x = dq_x + e * dqt
                dk_x = dk_x + e * dkt
                x_after = x_after + (qt.astype(F32) * dqt - kt.astype(F32) * dkt)

            dv = (lax.dot_general(a.astype(BF16), do_bf, TN, preferred_element_type=F32)
                  + jnp.sum(q * k, axis=-1, keepdims=True) * do_h
                  + jnp.dot((k * ekl).astype(BF16), ds1_bf, preferred_element_type=F32))
            dp_ref[:, pl.ds(2 * d + off, HEAD)] = dv.astype(BF16)

            carried = jnp.exp(b_last) * _col_to_row(jnp.sum(s0 * ds1, axis=-1, keepdims=True))
            dg = _dot3(ut, jnp.concatenate([x_after, x_before], axis=0)) + carried
            dq = dq_x + da_diag * k
            dk = dk_x + da_diag * q
            dp_ref[:, hs] = (dq * _dsilu(qr, sq)).astype(BF16)
            df = jnp.where(valid, dg / f - dk, 0.0)
            dp_ref[:, pl.ds(d + off, HEAD)] = (df * (1.0 - lb) * sf * (1.0 - sf)).astype(BF16)
            dlb_ref[:, hs] += jnp.sum(df * (1.0 - sf), axis=0, keepdims=True)

            ds_ref[hs, :] = (_row_to_col(jnp.exp(b_last)) * ds1
                             + lax.dot_general((q * eb).astype(BF16), do_bf, TN, preferred_element_type=F32))
        per_trip = min(HEADS_PER_TRIP, n_heads)

        def head_group(p, carry):
            for u in range(per_trip):
                head(p * per_trip + u)
            return carry

        if n_heads == per_trip:
            head_group(0, 0)
        else:
            lax.fori_loop(0, n_heads // per_trip, head_group, 0)

        @pl.when(n == nc - 1)
        def _():
            dl0 = dlb_ref[...] * pp
            dlbl_ref[0:1, :] = dl0
            dlbl_ref[1:2, :] = -dl0

    piece = lambda p: pl.BlockSpec((CHUNK, d), lambda n, p=p: (nc - 1 - n, p))
    return pl.pallas_call(
        body, grid=(nc,),
        in_specs=[piece(3), piece(4), piece(5), piece(0),
                  pl.BlockSpec((1, d, HEAD), lambda n: (nc - 1 - n, 0, 0)),
                  pl.BlockSpec((2, d), lambda n: (0, 0)),
                  pl.BlockSpec((n_tab, CHUNK), lambda n: (0, 0)),
                  pl.BlockSpec((CHUNK, 2 * CHUNK), lambda n: (0, 0)), ANY],
        out_specs=[pl.BlockSpec((CHUNK, 3 * d), lambda n: (nc - 1 - n, 1)), pl.BlockSpec((2, d), lambda n: (0, 0))],
        out_shape=[jax.ShapeDtypeStruct(dproj.shape, BF16), jax.ShapeDtypeStruct((2, d), F32)],
        input_output_aliases={8: 0},
        scratch_shapes=[pltpu.VMEM((d, HEAD), F32), pltpu.VMEM((n_tab, d), F32), pltpu.VMEM((1, d), F32)],
        name="hgrn_bwd", compiler_params=_params())(proj, proj, proj, do, s_all, lb_logits, tab, utri, dproj)


def _conv_bwd(dc, proj, conv_w, dz, dproj):
    lp, d = dc.shape
    tm = _row_tile(lp)
    hb = tm // HALO
    n_tiles = lp // tm
    last_halo = lp // HALO - 1

    def body(dc_ref, dcn_ref, ua_ref, ub_ref, uap_ref, ubp_ref, cw_ref, dz_ref, _,
             dp_ref, dcw_ref, dcb_ref, aext_ref, dcext_ref, da_ref, dcw_acc):
        i = pl.program_id(0)

        @pl.when(i == 0)
        def _():
            dcw_acc[...] = jnp.zeros_like(dcw_acc)
            dcb_ref[...] = jnp.zeros_like(dcb_ref)

        ua = ua_ref[...]
        sb = _sigmoid(ub_ref[...])
        a_prev = uap_ref[...] * _sigmoid(ubp_ref[...])
        aext_ref[0:HALO, :] = jnp.where(i > 0, a_prev, 0.0)
        aext_ref[HALO:HALO + tm, :] = ua * sb
        dcext_ref[0:tm, :] = dc_ref[...]
        dcext_ref[tm:tm + HALO, :] = jnp.where(i < n_tiles - 1, dcn_ref[...], 0.0)
        dcb_ref[...] += jnp.sum(dc_ref[...], axis=0, keepdims=True)

        def row_block(r, carry):
            r0 = pl.multiple_of(r * CONV_ROWS, CONV_ROWS)
            n_rows = CONV_ROWS + HALO
            for cs in range(d // CONV_LANES):
                cl = slice(cs * CONV_LANES, (cs + 1) * CONV_LANES)
                dblk = dcext_ref[pl.ds(r0, n_rows), cl]
                ablk = aext_ref[pl.ds(r0, n_rows), cl]
                dcur = dblk[0:CONV_ROWS, :]
                acc = jnp.zeros((CONV_ROWS, CONV_LANES), F32)
                for b in range(SUBLANES):
                    dsh = dblk if b == 0 else pltpu.roll(dblk, n_rows - b, axis=0)
                    ash = ablk if b == 0 else pltpu.roll(ablk, n_rows - b, axis=0)
                    for a in range(5):
                        j_da = CONV_WIDTH - 1 - (SUBLANES * a + b)
                        if 0 <= j_da < CONV_WIDTH:
                            acc = acc + cw_ref[j_da:j_da + 1, cl] * dsh[SUBLANES * a:SUBLANES * a + CONV_ROWS, :]
                        j_w = SUBLANES * a + b - 2
                        if 0 <= j_w < CONV_WIDTH:
                            prod = dcur * ash[SUBLANES * a:SUBLANES * a + CONV_ROWS, :]
                            dcw_acc[j_w, :, cl] += prod.reshape(CONV_ROWS // SUBLANES, SUBLANES, CONV_LANES).sum(axis=0)
                da_ref[pl.ds(r0, CONV_ROWS), cl] = acc
            return carry

        lax.fori_loop(0, tm // CONV_ROWS, row_block, 0)

        da = da_ref[...]
        dp_ref[:, 0:d] = (da * sb).astype(BF16)
        dp_ref[:, d:2 * d] = (da * ua * sb * (1.0 - sb)).astype(BF16)
        dp_ref[:, 2 * d:3 * d] = dz_ref[...]

        @pl.when(i == n_tiles - 1)
        def _():
            dcw_ref[...] = jnp.sum(dcw_acc[...], axis=1)

    row = lambda p: pl.BlockSpec((tm, d), lambda i, p=p: (i, p))
    prev = lambda p: pl.BlockSpec((HALO, d), lambda i, p=p: (jnp.maximum(i * hb - 1, 0), p))
    nxt = pl.BlockSpec((HALO, d), lambda i: (jnp.minimum((i + 1) * hb, last_halo), 0))
    return pl.pallas_call(
        body, grid=(n_tiles,),
        in_specs=[row(0), nxt, row(0), row(1), prev(0), prev(1), pl.BlockSpec((HALO, d), lambda i: (0, 0)),
                  row(0), ANY],
        out_specs=[pl.BlockSpec((tm, 3 * d), lambda i: (i, 0)), pl.BlockSpec((HALO, d), lambda i: (0, 0)),
                   pl.BlockSpec((1, d), lambda i: (0, 0))],
        out_shape=[jax.ShapeDtypeStruct(dproj.shape, BF16),
                   jax.ShapeDtypeStruct((HALO, d), F32), jax.ShapeDtypeStruct((1, d), F32)],
        input_output_aliases={8: 0},
        scratch_shapes=[pltpu.VMEM((HALO + tm, d), F32), pltpu.VMEM((tm + HALO, d), F32), pltpu.VMEM((tm, d), F32),
                        pltpu.VMEM((HALO, SUBLANES, d), F32)],
        name="conv_bwd", compiler_params=_params())(dc, dc, proj, proj, proj, proj, conv_w, dz, dproj)


def _weight_grad(xs, dy, name, blocked):
    lp, dx = xs.shape
    n = dy.shape[1]
    tk = _mm_row_tile(lp)
    if blocked:
        ncol = n // N_CHIPS
        nt = W_IN_COL_TILES
        tn = ncol // nt
        grid = (N_CHIPS * nt, lp // tk)
        out_spec = pl.BlockSpec((1, dx, tn), lambda c, k: (c // nt, 0, c % nt))
        out_shape = jax.ShapeDtypeStruct((N_CHIPS, dx, ncol), F32)
    else:
        tn = n
        grid = (1, lp // tk)
        out_spec = pl.BlockSpec((dx, tn), lambda c, k: (0, c))
        out_shape = jax.ShapeDtypeStruct((dx, n), F32)

    def body(xs_ref, dy_ref, o_ref, *copy_ref):
        @pl.when(pl.program_id(1) == 0)
        def _():
            o_ref[...] = jnp.zeros_like(o_ref)

        p = lax.dot_general(xs_ref[...], dy_ref[...], TN, preferred_element_type=F32)
        if blocked:
            o_ref[0] += p

            @pl.when(pl.program_id(1) == lp // tk - 1)
            def _():
                copy_ref[0][0] = o_ref[0].astype(BF16)
        else:
            o_ref[...] += p

    if blocked:
        out_spec = [out_spec, out_spec]
        out_shape = [out_shape, jax.ShapeDtypeStruct(out_shape.shape, BF16)]
    return pl.pallas_call(
        body, grid=grid,
        in_specs=[pl.BlockSpec((tk, dx), lambda c, k: (k, 0)), pl.BlockSpec((tk, tn), lambda c, k: (k, c))],
        out_specs=out_spec, out_shape=out_shape,
        name=name, compiler_params=_params())(xs, dy)


def _in_proj_bwd(dproj, wtg, hres, norm_g, dout):
    lp, d = hres.shape
    _, ncol, _ = wtg.shape
    tm = _mm_row_tile(lp)
    nt = W_IN_COL_TILES
    tn = ncol // nt
    nk = N_CHIPS * nt

    def body(dp_ref, w_ref, x_ref, g_ref, dout_ref, dx_ref, dg_ref, acc_ref):
        i = pl.program_id(0)
        kk = pl.program_id(1)

        @pl.when(jnp.logical_and(i == 0, kk == 0))
        def _():
            dg_ref[...] = jnp.zeros_like(dg_ref)

        @pl.when(kk == 0)
        def _():
            acc_ref[...] = jnp.zeros_like(acc_ref)

        acc_ref[...] += jnp.dot(dp_ref[...], w_ref[0], preferred_element_type=F32)

        @pl.when(kk == nk - 1)
        def _():
            x = x_ref[...]
            r = lax.rsqrt(jnp.mean(x * x, axis=-1, keepdims=True) + EPS)
            xhat = x * r
            dh = acc_ref[...]
            dg_ref[...] += jnp.sum(dh * xhat, axis=0, keepdims=True)
            dxh = dh * g_ref[...]
            dx_ref[...] = dout_ref[...] + r * (dxh - xhat * jnp.mean(dxh * xhat, axis=-1, keepdims=True))

    return pl.pallas_call(
        body, grid=(lp // tm, nk),
        in_specs=[pl.BlockSpec((tm, tn), lambda i, k: (i, k)),
                  pl.BlockSpec((1, tn, d), lambda i, k: (k // nt, k % nt, 0)),
                  pl.BlockSpec((tm, d), lambda i, k: (i, 0)),
                  pl.BlockSpec((1, d), lambda i, k: (0, 0)),
                  pl.BlockSpec((tm, d), lambda i, k: (i, 0))],
        out_specs=[pl.BlockSpec((tm, d), lambda i, k: (i, 0)), pl.BlockSpec((1, d), lambda i, k: (0, 0))],
        out_shape=[jax.ShapeDtypeStruct((lp, d), F32), jax.ShapeDtypeStruct((1, d), F32)],
        scratch_shapes=[pltpu.VMEM((tm, d), F32)],
        name="in_proj_bwd", compiler_params=_params())(dproj, wtg, hres, norm_g, dout)


def _adamw_math(w, g, m, v):
    m = ADAM_B1 * m + (1.0 - ADAM_B1) * g
    v = ADAM_B2 * v + (1.0 - ADAM_B2) * (g * g)
    m_hat = m / (1.0 - ADAM_B1 ** ADAM_STEP)
    v_hat = v / (1.0 - ADAM_B2 ** ADAM_STEP)
    delta = -ADAM_LR * (m_hat / (jnp.sqrt(v_hat) + ADAM_EPS) + ADAM_WD * w)
    return delta, m, v


def _elementwise_rows(shape):
    r, c = shape
    for t in (256, 128, 64, 32, 16, 8):
        if r % t == 0 and r > t and t * c * 4 <= ELEMENTWISE_BLOCK_BYTES:
            return t
    return r


def _adamw(name, w, m, v, *g_parts):
    shape = w.shape
    tr = _elementwise_rows(shape)
    n_g = len(g_parts)

    def body(*refs):
        w_ref, m_ref, v_ref = refs[:3]
        g_refs = refs[3:3 + n_g]
        g_out, d_out, m_out, v_out = refs[3 + n_g:]
        g = g_refs[0][...]
        for gr in g_refs[1:]:
            g = g + gr[...]
        delta, m_new, v_new = _adamw_math(w_ref[...], g, m_ref[...], v_ref[...])
        g_out[...] = g
        d_out[...] = delta
        m_out[...] = m_new
        v_out[...] = v_new

    spec = pl.BlockSpec((tr, shape[1]), lambda i: (i, 0))
    return pl.pallas_call(
        body, grid=(shape[0] // tr,),
        in_specs=[spec] * (3 + n_g), out_specs=[spec] * 4,
        out_shape=[jax.ShapeDtypeStruct(shape, F32)] * 4,
        name=name, compiler_params=_params())(w, m, v, *g_parts)


def _chip_half_sum(name, g, recv, core):
    _, _, hr, cols = g.shape
    tr = _elementwise_rows((hr, cols))

    def body(core_ref, g_ref, r_ref, o_ref, ob_ref):
        s = g_ref[0, 0] + r_ref[0].astype(F32)
        o_ref[0] = s
        ob_ref[0] = s.astype(BF16)

    blk = pl.BlockSpec((1, tr, cols), lambda j, i, core_ref: (j, i, 0))
    grid_spec = pltpu.PrefetchScalarGridSpec(
        num_scalar_prefetch=1, grid=(N_CHIPS, hr // tr),
        in_specs=[pl.BlockSpec((1, 1, tr, cols), lambda j, i, core_ref: (j, core_ref[0], i, 0)), blk],
        out_specs=[blk, blk])
    return pl.pallas_call(
        body, grid_spec=grid_spec,
        out_shape=[jax.ShapeDtypeStruct((N_CHIPS, hr, cols), F32), jax.ShapeDtypeStruct((N_CHIPS, hr, cols), BF16)],
        name=name, compiler_params=_params())(core, g, recv)


def _block_half_total(name, chip_sums, recv, chip_core, after):
    _, hr, cols = chip_sums.shape
    tr = _elementwise_rows((hr, cols))

    def body(cc_ref, p_ref, r_ref, after_ref, o_ref):
        s = p_ref[0]
        for k in range(3):
            s = s + r_ref[k].astype(F32)
        o_ref[0] = s

    grid_spec = pltpu.PrefetchScalarGridSpec(
        num_scalar_prefetch=1, grid=(hr // tr,),
        in_specs=[pl.BlockSpec((1, tr, cols), lambda i, cc_ref: (cc_ref[0], i, 0)),
                  pl.BlockSpec((3, tr, cols), lambda i, cc_ref: (0, i, 0)), ANY],
        out_specs=pl.BlockSpec((1, tr, cols), lambda i, cc_ref: (cc_ref[1], i, 0)))
    return pl.pallas_call(
        body, grid_spec=grid_spec, out_shape=jax.ShapeDtypeStruct((2, hr, cols), F32),
        name=name, compiler_params=_params())(chip_core, chip_sums, recv, after)


def _place_shard(name, w, chip, dtype):
    r, c = w.shape
    tr = _elementwise_rows((r, c))

    def body(chip_ref, w_ref, o_ref):
        o_ref[0] = w_ref[...].astype(dtype)

    grid_spec = pltpu.PrefetchScalarGridSpec(
        num_scalar_prefetch=1, grid=(r // tr,),
        in_specs=[pl.BlockSpec((tr, c), lambda i, chip_ref: (i, 0))],
        out_specs=pl.BlockSpec((1, tr, c), lambda i, chip_ref: (chip_ref[0], i, 0)))
    return pl.pallas_call(
        body, grid_spec=grid_spec, out_shape=jax.ShapeDtypeStruct((N_CHIPS, r, c), dtype),
        name=name, compiler_params=_params())(chip, w)


def _sum_slots(name, slots, own, my_idx):
    k, r, c = slots.shape

    def body(idx_ref, s_ref, own_ref, o_ref):
        s = None
        for j in range(k):
            term = jnp.where(idx_ref[0] == j, own_ref[...], s_ref[j])
            s = term if s is None else s + term
        o_ref[...] = s

    grid_spec = pltpu.PrefetchScalarGridSpec(
        num_scalar_prefetch=1, grid=(1,),
        in_specs=[pl.BlockSpec((k, r, c), lambda i, idx_ref: (0, 0, 0)),
                  pl.BlockSpec((r, c), lambda i, idx_ref: (0, 0))],
        out_specs=pl.BlockSpec((r, c), lambda i, idx_ref: (0, 0)))
    return pl.pallas_call(body, grid_spec=grid_spec, out_shape=jax.ShapeDtypeStruct((r, c), F32), name=name,
                          compiler_params=_params())(my_idx, slots, own)


def _mesh_pos():
    return lax.axis_index("x"), lax.axis_index("y"), lax.axis_index("c")


def _other_chips(x, y):
    return [(1 - x, y), (x, 1 - y), (1 - x, 1 - y)]


def _gather_weights(bufs):
    n = len(bufs)
    half = [b.shape[1] // 2 for b in bufs]

    def body(*refs):
        gathered = refs[n:2 * n]
        ici_send, ici_recv, d2d_send, d2d_recv = refs[2 * n:]
        x, y, c = _mesh_pos()
        me = 2 * x + y
        chips = _other_chips(x, y)

        def part(a, block, core):
            return gathered[a].at[block, pl.ds(core * half[a], half[a])]

        def over_ici(a, k, block):
            px, py = chips[k]
            return pltpu.make_async_remote_copy(
                src_ref=part(a, block, c), dst_ref=part(a, block, c),
                send_sem=ici_send.at[a, k], recv_sem=ici_recv.at[a, k],
                device_id=(px, py, c), device_id_type=MESH)

        def over_d2d(a, k, core):
            px, py = chips[k]
            return pltpu.make_async_remote_copy(
                src_ref=part(a, 2 * px + py, core), dst_ref=part(a, 2 * px + py, core),
                send_sem=d2d_send.at[a, k], recv_sem=d2d_recv.at[a, k],
                device_id=(x, y, 1 - c), device_id_type=MESH)

        for a in range(n):
            for k in range(3):
                over_ici(a, k, me).start()
        for a in range(n):
            for k, (px, py) in enumerate(chips):
                over_ici(a, k, 2 * px + py).wait_recv()
                over_d2d(a, k, c).start()
        for a in range(n):
            for k in range(3):
                over_d2d(a, k, 1 - c).wait_recv()
        for a in range(n):
            for k in range(3):
                over_ici(a, k, me).wait_send()
                over_d2d(a, k, c).wait_send()

    return pl.pallas_call(
        body, in_specs=[ANY] * n, out_specs=[ANY] * n,
        out_shape=[jax.ShapeDtypeStruct(b.shape, b.dtype) for b in bufs],
        input_output_aliases={a: a for a in range(n)},
        scratch_shapes=[pltpu.SemaphoreType.DMA((n, 3))] * 4,
        name="gather_weights")(*bufs)


def _gather_in_proj(h, bufs, order):
    n = len(bufs)
    half = [b.shape[1] // 2 for b in bufs]
    lp, d = h.shape
    ncol = bufs[0].shape[2]
    tm = _mm_row_tile(lp)
    n_row = lp // tm

    def body(order_ref, h_ref, *refs):
        gathered = refs[n:2 * n]
        o_ref, wt_ref = refs[2 * n], refs[2 * n + 1]
        w_buf, ici_send, ici_recv, d2d_send, d2d_recv, w_sem = refs[2 * n + 2:]
        j = pl.program_id(0)
        i = pl.program_id(1)
        x, y, c = _mesh_pos()
        me = 2 * x + y
        chips = _other_chips(x, y)

        def part(a, block, core):
            return gathered[a].at[block, pl.ds(core * half[a], half[a])]

        def over_ici(a, k, block):
            px, py = chips[k]
            return pltpu.make_async_remote_copy(
                src_ref=part(a, block, c), dst_ref=part(a, block, c),
                send_sem=ici_send.at[a, k], recv_sem=ici_recv.at[a, k],
                device_id=(px, py, c), device_id_type=MESH)

        def over_d2d(a, k, core):
            px, py = chips[k]
            return pltpu.make_async_remote_copy(
                src_ref=part(a, 2 * px + py, core), dst_ref=part(a, 2 * px + py, core),
                send_sem=d2d_send.at[a, k], recv_sem=d2d_recv.at[a, k],
                device_id=(x, y, 1 - c), device_id_type=MESH)

        @pl.when(jnp.logical_and(j == 0, i == 0))
        def _():
            for a in range(n):
                for k in range(2):
                    over_ici(a, k, me).start()

        for k, (px, py) in enumerate(chips):
            @pl.when(jnp.logical_and(j == k + 1, i == 0))
            def _(k=k, px=px, py=py):
                for a in range(n):
                    over_ici(a, k, 2 * px + py).wait_recv()
                    over_d2d(a, k, c).start()
                if k == 0:
                    for a in range(n):
                        over_ici(a, 2, me).start()
                for a in range(n):
                    over_d2d(a, k, 1 - c).wait_recv()

        @pl.when(i == 0)
        def _():
            load = pltpu.make_async_copy(gathered[0].at[order_ref[j]], w_buf, w_sem)
            load.start()
            load.wait()
            wt_ref[0] = w_buf[...].T

        o_ref[...] = jnp.dot(h_ref[...], w_buf[...], preferred_element_type=F32)

        @pl.when(jnp.logical_and(j == N_CHIPS - 1, i == n_row - 1))
        def _():
            for a in range(n):
                for k in range(3):
                    over_ici(a, k, me).wait_send()
                    over_d2d(a, k, c).wait_send()

    grid_spec = pltpu.PrefetchScalarGridSpec(
        num_scalar_prefetch=1, grid=(N_CHIPS, n_row),
        in_specs=[pl.BlockSpec((tm, d), lambda j, i, order_ref: (i, 0))] + [ANY] * n,
        out_specs=[ANY] * n + [pl.BlockSpec((tm, ncol), lambda j, i, order_ref: (i, order_ref[j])),
                               pl.BlockSpec((1, ncol, d), lambda j, i, order_ref: (order_ref[j], 0, 0))],
        scratch_shapes=[pltpu.VMEM((d, ncol), BF16)] + [pltpu.SemaphoreType.DMA((n, 3))] * 4
        + [pltpu.SemaphoreType.DMA])
    out = pl.pallas_call(
        body, grid_spec=grid_spec,
        out_shape=[jax.ShapeDtypeStruct(b.shape, b.dtype) for b in bufs]
        + [jax.ShapeDtypeStruct((lp, N_CHIPS * ncol), F32), jax.ShapeDtypeStruct((N_CHIPS, ncol, d), BF16)],
        input_output_aliases={2 + a: a for a in range(n)},
        name="gather_in_proj", compiler_params=_params())(order, h, *bufs)
    return out[n], out[n + 1], out[:n]


def _send_other_halves(grads, tag):
    n = len(grads)

    def body(*refs):
        srcs = refs[:n]
        dsts = refs[n:2 * n]
        send_sems, recv_sems = refs[2 * n:]
        x, y, c = _mesh_pos()
        copies = [pltpu.make_async_remote_copy(
            src_ref=srcs[a].at[j, 1 - c], dst_ref=dsts[a].at[j], send_sem=send_sems.at[a, j],
            recv_sem=recv_sems.at[a, j], device_id=(x, y, 1 - c), device_id_type=MESH)
            for a in range(n) for j in range(N_CHIPS)]
        for cp in copies:
            cp.start()
        for cp in copies:
            cp.wait()

    return pl.pallas_call(
        body, in_specs=[ANY] * n, out_specs=[ANY] * n,
        out_shape=[jax.ShapeDtypeStruct((N_CHIPS,) + g.shape[2:], g.dtype) for g in grads],
        scratch_shapes=[pltpu.SemaphoreType.DMA((n, N_CHIPS))] * 2,
        name="send_other_halves_" + tag)(*grads)


HBM = pl.BlockSpec(memory_space=pltpu.HBM)
SEM = pl.BlockSpec(memory_space=pltpu.SEMAPHORE)


def _block_copies(n, srcs, dsts, send_sems, recv_sems):
    x, y, c = _mesh_pos()
    return [pltpu.make_async_remote_copy(
        src_ref=srcs[a].at[2 * px + py], dst_ref=dsts[a].at[k], send_sem=send_sems.at[3 * a + k],
        recv_sem=recv_sems.at[3 * a + k], device_id=(px, py, c), device_id_type=MESH)
        for a in range(n) for k, (px, py) in enumerate(_other_chips(x, y))]


def _exchange_start(blocked, tag):
    n = len(blocked)
    lands = [lax.empty((3,) + b.shape[1:], b.dtype) for b in blocked]
    bufs = [pltpu.with_memory_space_constraint(b, pltpu.HBM) for b in list(blocked) + lands]
    nb = 2 * n

    def body(*refs):
        for cp in _block_copies(n, refs[:n], refs[n:nb], refs[nb], refs[nb + 1]):
            cp.start()
        refs[-1][...] = jnp.zeros_like(refs[-1])

    out = pl.pallas_call(
        body, name="exchange_start_" + tag,
        in_specs=[HBM] * nb,
        out_shape=[pltpu.SemaphoreType.DMA((3 * n,)), pltpu.SemaphoreType.DMA((3 * n,))]
        + [pltpu.HBM(b.shape, b.dtype) for b in bufs] + [jax.ShapeDtypeStruct((8, 128), F32)],
        out_specs=[SEM] * 2 + [HBM] * nb + [pl.BlockSpec(memory_space=pltpu.VMEM)],
        input_output_aliases={i: 2 + i for i in range(nb)},
        compiler_params=pltpu.CompilerParams(has_side_effects=pltpu.SideEffectType.DATAFLOW_SIDE_EFFECTING),
    )(*bufs)
    return (out[:2], out[2:2 + nb]), out[-1]


def _exchange_wait(state, after, tag):
    sems, bufs = state
    nb = len(bufs)
    n = nb // 2

    def body(*refs):
        for cp in _block_copies(n, refs[:n], refs[n:nb], refs[nb], refs[nb + 1]):
            cp.wait_send()
            cp.wait_recv()

    out = pl.pallas_call(
        body, name="exchange_wait_" + tag,
        in_specs=[HBM] * nb + [SEM] * 2 + [ANY],
        out_shape=[pltpu.HBM(b.shape, b.dtype) for b in bufs],
        out_specs=[HBM] * nb,
        input_output_aliases={i: i for i in range(nb)},
        compiler_params=pltpu.CompilerParams(has_side_effects=pltpu.SideEffectType.DATAFLOW_SIDE_EFFECTING),
    )(*bufs, *sems, after)
    return out[n:nb]


def _whole_block_copies(buf, send_sems, recv_sems, incoming):
    x, y, c = _mesh_pos()
    me = 2 * x + y
    out = []
    for k, (px, py) in enumerate(_other_chips(x, y)):
        block = 2 * px + py if incoming else me
        out.append(pltpu.make_async_remote_copy(
            src_ref=buf.at[block], dst_ref=buf.at[block], send_sem=send_sems.at[k], recv_sem=recv_sems.at[k],
            device_id=(px, py, c), device_id_type=MESH))
    return out


def _gather_start(buf, after, tag):
    buf = pltpu.with_memory_space_constraint(buf, pltpu.HBM)

    def body(buf_ref, after_ref, send_sems, recv_sems, thru_ref, token):
        for cp in _whole_block_copies(buf_ref, send_sems, recv_sems, incoming=False):
            cp.start()
        token[...] = jnp.zeros_like(token)

    out = pl.pallas_call(
        body, name="gather_start_" + tag,
        in_specs=[HBM, ANY],
        out_shape=[pltpu.SemaphoreType.DMA((3,)), pltpu.SemaphoreType.DMA((3,)), pltpu.HBM(buf.shape, buf.dtype),
                   jax.ShapeDtypeStruct((8, 128), F32)],
        out_specs=[SEM, SEM, HBM, pl.BlockSpec(memory_space=pltpu.VMEM)],
        input_output_aliases={0: 2},
        compiler_params=pltpu.CompilerParams(has_side_effects=pltpu.SideEffectType.DATAFLOW_SIDE_EFFECTING),
    )(buf, after)
    return out[:3], out[3]


def _gather_wait(state, after, tag):
    send_sems, recv_sems, buf = state

    def body(buf_ref, send_ref, recv_ref, after_ref, out_ref):
        for cp in _whole_block_copies(buf_ref, send_ref, recv_ref, incoming=True):
            cp.wait_send()
            cp.wait_recv()

    return pl.pallas_call(
        body, name="gather_wait_" + tag,
        in_specs=[HBM, SEM, SEM, ANY],
        out_shape=pltpu.HBM(buf.shape, buf.dtype), out_specs=HBM,
        input_output_aliases={0: 0},
        compiler_params=pltpu.CompilerParams(has_side_effects=pltpu.SideEffectType.DATAFLOW_SIDE_EFFECTING),
    )(buf, send_sems, recv_sems, after)


def _small_copies(small_ref, slots_ref, send_sems, recv_sems, incoming):
    x, y, c = _mesh_pos()
    out = []
    for r in range(1, 8):
        px = 1 - x if r & 4 else x
        py = 1 - y if r & 2 else y
        pc = 1 - c if r & 1 else c
        slot = 4 * px + 2 * py + pc if incoming else 4 * x + 2 * y + c
        out.append(pltpu.make_async_remote_copy(
            src_ref=small_ref, dst_ref=slots_ref.at[slot], send_sem=send_sems.at[r - 1],
            recv_sem=recv_sems.at[r - 1], device_id=(px, py, pc), device_id_type=MESH))
    return out


def _small_start(small):
    bufs = [pltpu.with_memory_space_constraint(b, pltpu.HBM)
            for b in (small, lax.empty((8,) + small.shape, small.dtype))]

    def body(small_ref, slots_ref, send_sems, recv_sems, small_thru, slots_thru, token):
        for cp in _small_copies(small_ref, slots_ref, send_sems, recv_sems, incoming=False):
            cp.start()
        token[...] = jnp.zeros_like(token)

    out = pl.pallas_call(
        body, name="small_start",
        in_specs=[HBM, HBM],
        out_shape=[pltpu.SemaphoreType.DMA((7,)), pltpu.SemaphoreType.DMA((7,))]
        + [pltpu.HBM(b.shape, b.dtype) for b in bufs] + [jax.ShapeDtypeStruct((8, 128), F32)],
        out_specs=[SEM, SEM, HBM, HBM, pl.BlockSpec(memory_space=pltpu.VMEM)],
        input_output_aliases={0: 2, 1: 3},
        compiler_params=pltpu.CompilerParams(has_side_effects=pltpu.SideEffectType.DATAFLOW_SIDE_EFFECTING),
    )(*bufs)
    return out[:4], out[4]


def _small_wait(state, after):
    send_sems, recv_sems, small, slots = state

    def body(small_ref, slots_ref, send_ref, recv_ref, after_ref, small_out, slots_out):
        for cp in _small_copies(small_ref, slots_ref, send_ref, recv_ref, incoming=True):
            cp.wait_send()
            cp.wait_recv()

    return pl.pallas_call(
        body, name="small_wait",
        in_specs=[HBM, HBM, SEM, SEM, ANY],
        out_shape=[pltpu.HBM(small.shape, small.dtype), pltpu.HBM(slots.shape, slots.dtype)],
        out_specs=[HBM, HBM], input_output_aliases={0: 0, 1: 1},
        compiler_params=pltpu.CompilerParams(has_side_effects=pltpu.SideEffectType.DATAFLOW_SIDE_EFFECTING),
    )(small, slots, send_sems, recv_sems, after)[1]


def _join_halves(bufs):
    n = len(bufs)

    def body(*refs):
        joined = refs[n:2 * n]
        send_sems, recv_sems = refs[2 * n:]
        x, y, c = _mesh_pos()
        for a in range(n):
            pltpu.make_async_remote_copy(
                src_ref=joined[a].at[c], dst_ref=joined[a].at[c], send_sem=send_sems.at[a],
                recv_sem=recv_sems.at[a], device_id=(x, y, 1 - c), device_id_type=MESH).start()
        for a in range(n):
            pltpu.make_async_remote_copy(
                src_ref=joined[a].at[c], dst_ref=joined[a].at[1 - c], send_sem=send_sems.at[a],
                recv_sem=recv_sems.at[a], device_id=(x, y, 1 - c), device_id_type=MESH).wait()

    return pl.pallas_call(
        body, in_specs=[ANY] * n, out_specs=[ANY] * n,
        out_shape=[jax.ShapeDtypeStruct(b.shape, b.dtype) for b in bufs],
        input_output_aliases={a: a for a in range(n)},
        scratch_shapes=[pltpu.SemaphoreType.DMA((n,))] * 2,
        name="join_halves")(*bufs)


def kernel(x, meta_tokens, norm_g, w_in, conv_w, conv_b, ln_g, ln_b, w_conv_out, lb_logits, gnorm_g, w_rec_out, w_out, final_g, loss_target, m_meta_tokens, m_norm_g, m_w_in, m_conv_w, m_conv_b, m_ln_g, m_ln_b, m_w_conv_out, m_lb_logits, m_gnorm_g, m_w_rec_out, m_w_out, m_final_g, v_meta_tokens, v_norm_g, v_w_in, v_conv_w, v_conv_b, v_ln_g, v_ln_b, v_w_conv_out, v_lb_logits, v_gnorm_g, v_w_rec_out, v_w_out, v_final_g):
    d = x.shape[2]
    n_meta = meta_tokens.shape[0]
    n_pad = CHUNK - n_meta
    ds = d // N_CHIPS
    chip = 2 * lax.axis_index("x") + lax.axis_index("y")

    conv_w_pad = jnp.pad(conv_w[0], ((0, HALO - CONV_WIDTH), (0, 0)))
    chip_idx = chip.astype(jnp.int32).reshape(1)
    (small_g,) = _gather_weights([
        _place_shard("place_small", jnp.concatenate([conv_w_pad, meta_tokens], axis=0), chip_idx, F32)])
    cw_full = jnp.transpose(small_g[:, 0:HALO], (1, 0, 2)).reshape(HALO, d)
    meta_full = jnp.transpose(small_g[:, HALO:HALO + n_meta], (1, 0, 2)).reshape(n_meta, d)

    hres = jnp.concatenate([jnp.zeros((n_pad, d), F32), meta_full, x[0]], axis=0)
    target = loss_target[0]
    final_g2 = final_g.reshape(1, d)
    h = _rmsnorm_fwd(hres, norm_g)
    fx, fy = 1 - lax.axis_index("x"), 1 - lax.axis_index("y")
    order = jnp.stack([chip, 2 * fx + (1 - fy), 2 * (1 - fx) + fy, 2 * fx + fy]).astype(jnp.int32)
    proj, win_t, _ = _gather_in_proj(h, [_place_shard("place_w_in", w_in[0], chip_idx, BF16)], order)
    sq_own = _place_shard("place_square", jnp.concatenate([w_conv_out[0], w_rec_out[0], w_out[0]], axis=0),
                          chip_idx, BF16)
    sq_flight, sq_token = _gather_start(sq_own, proj, "square")
    o, s_all = _hgrn_fwd(proj, lb_logits + sq_token[0:1, 0:1], n_pad)
    sq_g = _gather_wait(sq_flight, s_all, "square")
    wc_full = sq_g[:, 0:ds].reshape(d, d)
    wr_full = sq_g[:, ds:2 * ds].reshape(d, d)
    wo_full = sq_g[:, 2 * ds:3 * ds].reshape(d, d)
    c, yc_in, y_conv = _conv_fwd(proj, cw_full, conv_b, ln_g, ln_b, wc_full)
    yr_in, merged, y_rec, dout, loss_acc, dfinal_g = _tail_fwd(
        o, proj, y_conv, hres, target, gnorm_g, final_g2, wr_full, wo_full)

    (dyc, dyr, dout_bf, dz, dproj, do, dc, dgnorm_g, dln_g, dln_b) = _tail_bwd(
        dout, proj, y_conv, y_rec, o, c, wo_full, wr_full, wc_full, ln_g, ln_b, gnorm_g)
    g_wc = _weight_grad(yc_in, dyc, "grad_w_conv_out", False)
    g_wr = _weight_grad(yr_in, dyr, "grad_w_rec_out", False)
    g_wo = _weight_grad(merged, dout_bf, "grad_w_out", False)

    core = lax.axis_index("c").astype(jnp.int32).reshape(1)

    def chip_sum_and_start(g, tag, g_to_sibling=None):
        halves = lambda a: a.reshape(N_CHIPS, 2, a.shape[1] // 2, a.shape[2])
        g = halves(g)
        (from_sibling,) = _send_other_halves([g if g_to_sibling is None else halves(g_to_sibling)], tag)
        sums = _chip_half_sum("chip_half_sum_" + tag, g, from_sibling, core)
        in_flight, token = _exchange_start([sums[1]], tag)
        return sums[0], in_flight, token[0:1, 0:1]

    g_sq = jnp.concatenate([g.reshape(N_CHIPS, ds, d) for g in (g_wc, g_wr, g_wo)], axis=1)
    sum_sq, flight_sq, token_sq = chip_sum_and_start(g_sq, "square")
    dproj, dlb_logits = _hgrn_bwd(proj, do, s_all, lb_logits + token_sq, n_pad, dproj)
    dproj, dconv_w, dconv_b = _conv_bwd(dc, proj, cw_full, dz, dproj)
    (recv_sq,) = _exchange_wait(flight_sq, dconv_b, "square")
    g_win, g_win_bf = _weight_grad(h, dproj, "grad_w_in", True)
    sum_win, flight_win, token_win = chip_sum_and_start(g_win, "w_in", g_win_bf)
    dhres, dnorm_g = _in_proj_bwd(dproj, win_t, hres, norm_g + token_win, dout)
    grad_x = dhres[CHUNK:][None]
    (recv_win,) = _exchange_wait(flight_win, dnorm_g, "w_in")
    small = jnp.concatenate([dnorm_g, dconv_b, dln_g, dln_b, dlb_logits, dgnorm_g, dfinal_g,
                             dhres[n_pad:CHUNK], dconv_w[:CONV_WIDTH],
                             jnp.broadcast_to(loss_acc[0:1, 0:1], (1, d))], axis=0)
    small_flight, small_token = _small_start(small)
    chip_core = jnp.concatenate([chip_idx, core])
    totals = [_block_half_total("block_half_total_" + nm, s, r, chip_core, small_token)
              for nm, s, r in zip(("w_in", "square"), (sum_win, sum_sq), (recv_win, recv_sq))]
    joined = _join_halves(totals)
    gt_win, gt_sq = [t.reshape(2 * t.shape[1], t.shape[2]) for t in joined]
    small_slots = _small_wait(small_flight, joined[1])
    device_idx = (2 * chip_idx + core).astype(jnp.int32)
    small_sum = _sum_slots("sum_small", small_slots, small, device_idx)

    res = {}
    res["w_in"] = _adamw("adamw_w_in", w_in[0], m_w_in[0], v_w_in[0], gt_win)
    res["w_conv_out"] = _adamw("adamw_w_conv_out", w_conv_out[0], m_w_conv_out[0], v_w_conv_out[0], gt_sq[0:ds])
    res["w_rec_out"] = _adamw("adamw_w_rec_out", w_rec_out[0], m_w_rec_out[0], v_w_rec_out[0], gt_sq[ds:2 * ds])
    res["w_out"] = _adamw("adamw_w_out", w_out[0], m_w_out[0], v_w_out[0], gt_sq[2 * ds:3 * ds])
    big = {k: tuple(a[None] for a in v) for k, v in res.items()}

    rep_names = ("norm_g", "conv_b", "ln_g", "ln_b", "lb_logits", "gnorm_g", "final_g")
    rep_w = (norm_g, conv_b, ln_g, ln_b, lb_logits, gnorm_g, final_g2)
    rep_m = (m_norm_g, m_conv_b, m_ln_g, m_ln_b, m_lb_logits, m_gnorm_g, m_final_g.reshape(1, d))
    rep_v = (v_norm_g, v_conv_b, v_ln_g, v_ln_b, v_lb_logits, v_gnorm_g, v_final_g.reshape(1, d))
    rep = _adamw("adamw_replicated", jnp.concatenate(rep_w, 0), jnp.concatenate(rep_m, 0),
                 jnp.concatenate(rep_v, 0), small_sum[0:8])
    rep_rows = {"norm_g": (0, 1), "conv_b": (1, 2), "ln_g": (2, 3), "ln_b": (3, 4), "lb_logits": (4, 6),
                "gnorm_g": (6, 7), "final_g": (7, 8)}
    small_out = {}
    for nm in rep_names:
        lo, hi = rep_rows[nm]
        vals = tuple(a[lo:hi] for a in rep)
        if nm == "final_g":
            vals = tuple(a.reshape(d) for a in vals)
        small_out[nm] = vals
    cw_row = 8 + n_meta
    g_meta = lax.dynamic_slice_in_dim(small_sum[8:cw_row], chip * ds, ds, axis=1)
    small_out["meta_tokens"] = _adamw("adamw_meta", meta_tokens, m_meta_tokens, v_meta_tokens, g_meta)
    g_cw = lax.dynamic_slice_in_dim(small_sum[cw_row:cw_row + HALO], chip * ds, ds, axis=1)
    pad_rows = ((0, HALO - CONV_WIDTH), (0, 0))
    cw_res = _adamw("adamw_conv_w", conv_w_pad, jnp.pad(m_conv_w[0], pad_rows),
                    jnp.pad(v_conv_w[0], pad_rows, constant_values=1.0), g_cw)
    small_out["conv_w"] = tuple(a[:CONV_WIDTH][None] for a in cw_res)

    loss = small_sum[cw_row + HALO - 1, 0]

    order = ("meta_tokens", "norm_g", "w_in", "conv_w", "conv_b", "ln_g", "ln_b", "w_conv_out", "lb_logits",
             "gnorm_g", "w_rec_out", "w_out", "final_g")
    allres = {**big, **small_out}
    outs = [loss, grad_x]
    for field in range(4):
        outs.extend(allres[nm][field] for nm in order)
    return tuple(outs)
```

```python
import numpy as np

import jax
import jax.numpy as jnp
from jax import lax
from jax.experimental import pallas as pl
from jax.experimental.pallas import tpu as pltpu

F32 = jnp.float32
BF16 = jnp.bfloat16

EPS = 1e-6
CHUNK = 64
N_LEVELS = 6
FIRST_TABLE_LEVEL = 6
CONV_WIDTH = 31
HALO = 32
CONV_ROWS = 64
CONV_LANES = 128
SUBLANES = 8
HEAD = 128
W_IN_COL_TILES = 1
HEADS_PER_TRIP = 8
N_CHIPS = 4
VMEM_LIMIT_BYTES = 56 * 1024 * 1024
ELEMENTWISE_BLOCK_BYTES = 3 * 1024 * 1024

ADAM_LR = 0.001
ADAM_B1 = 0.9
ADAM_B2 = 0.999
ADAM_EPS = 1e-08
ADAM_WD = 0.01
ADAM_STEP = 10

MESH = pl.DeviceIdType.MESH
ANY = pl.BlockSpec(memory_space=pl.ANY)

NT = (((1,), (1,)), ((), ()))
TN = (((0,), (0,)), ((), ()))


def _params(**kw):
    return pltpu.CompilerParams(vmem_limit_bytes=VMEM_LIMIT_BYTES, **kw)


def _sigmoid(x):
    return jax.nn.sigmoid(x)


def _dsilu(x, s):
    return s * (1.0 + x * (1.0 - s))


def _row_tile(lp):
    for t in (320, 256, 192, 128, 64):
        if lp % t == 0:
            return t
    raise ValueError(f"unsupported padded length {lp}")


def _mm_row_tile(lp):
    for t in (832, 640, 320, 256, 192, 128, 64):
        if lp % t == 0:
            return t
    raise ValueError(f"unsupported padded length {lp}")


def _dot3(m_bf16, x):
    hi = x.astype(BF16)
    r1 = x - hi.astype(F32)
    mid = r1.astype(BF16)
    lo = (r1 - mid.astype(F32)).astype(BF16)
    return (jnp.dot(m_bf16, hi, preferred_element_type=F32)
            + jnp.dot(m_bf16, mid, preferred_element_type=F32)
            + jnp.dot(m_bf16, lo, preferred_element_type=F32))


def _dot2(m_bf16, x):
    hi = x.astype(BF16)
    lo = (x - hi.astype(F32)).astype(BF16)
    return (jnp.dot(m_bf16, hi, preferred_element_type=F32)
            + jnp.dot(m_bf16, lo, preferred_element_type=F32))


def _col_to_row(col):
    return jnp.broadcast_to(col, (HEAD, SUBLANES)).T[0:1, :]


def _row_to_col(row):
    return jnp.broadcast_to(row, (SUBLANES, HEAD)).T[:, 0:1]


def _hgrn_tables():
    t = np.arange(CHUNK)
    ltri = (t[None, :] <= t[:, None]).astype(np.float32)
    mats = [ltri]
    for lvl in range(FIRST_TABLE_LEVEL, N_LEVELS + 1):
        blk = CHUNK >> (lvl - 1)
        mid = (t // blk) * blk + blk // 2
        mats.append(ltri[mid - 1])
    after = (t[None, :] >= t[:, None]).astype(np.float32)
    before = (t[None, :] < t[:, None]).astype(np.float32)
    return jnp.asarray(np.concatenate(mats, 0), BF16), jnp.asarray(np.concatenate([after, before], 1), BF16)


def _rmsnorm_fwd(hres, g):
    lp, d = hres.shape
    tm = _mm_row_tile(lp)

    def body(x_ref, g_ref, h_ref):
        x = x_ref[...]
        r = lax.rsqrt(jnp.mean(x * x, axis=-1, keepdims=True) + EPS)
        h_ref[...] = (x * r * g_ref[...]).astype(BF16)

    return pl.pallas_call(
        body, grid=(lp // tm,),
        in_specs=[pl.BlockSpec((tm, d), lambda i: (i, 0)), pl.BlockSpec((1, d), lambda i: (0, 0))],
        out_specs=pl.BlockSpec((tm, d), lambda i: (i, 0)),
        out_shape=jax.ShapeDtypeStruct((lp, d), BF16),
        name="rmsnorm_fwd", compiler_params=_params())(hres, g)


def _conv_fwd(proj, conv_w, conv_b, ln_g, ln_b, w_conv):
    lp = proj.shape[0]
    d = conv_b.shape[1]
    tm = _row_tile(lp)
    hb = tm // HALO

    def body(ua_ref, ub_ref, z_ref, uap_ref, ubp_ref, cw_ref, cb_ref, lg_ref, lb_ref, w_ref,
             c_ref, ycin_ref, yconv_ref, aext_ref):
        i = pl.program_id(0)
        a_prev = uap_ref[...] * _sigmoid(ubp_ref[...])
        aext_ref[0:HALO, :] = jnp.where(i > 0, a_prev, 0.0)
        aext_ref[HALO:HALO + tm, :] = ua_ref[...] * _sigmoid(ub_ref[...])

        def row_block(r, carry):
            r0 = pl.multiple_of(r * CONV_ROWS, CONV_ROWS)
            for cs in range(d // CONV_LANES):
                cl = slice(cs * CONV_LANES, (cs + 1) * CONV_LANES)
                blk = aext_ref[pl.ds(r0, CONV_ROWS + HALO), cl]
                acc = jnp.zeros((CONV_ROWS, CONV_LANES), F32) + cb_ref[:, cl]
                for b in range(SUBLANES):
                    sh = blk if b == 0 else pltpu.roll(blk, CONV_ROWS + HALO - b, axis=0)
                    for a in range(5):
                        j = SUBLANES * a + b - 2
                        if 0 <= j < CONV_WIDTH:
                            acc = acc + cw_ref[j:j + 1, cl] * sh[SUBLANES * a:SUBLANES * a + CONV_ROWS, :]
                c_ref[pl.ds(r0, CONV_ROWS), cl] = acc
            return carry

        lax.fori_loop(0, tm // CONV_ROWS, row_block, 0)

        c = c_ref[...]
        mu = jnp.mean(c, axis=-1, keepdims=True)
        xc = c - mu
        rstd = lax.rsqrt(jnp.mean(xc * xc, axis=-1, keepdims=True) + EPS)
        ln = xc * rstd * lg_ref[...] + lb_ref[...]
        s = ln * _sigmoid(ln)
        z = z_ref[...]
        ycin = (s * (z * _sigmoid(z))).astype(BF16)
        ycin_ref[...] = ycin
        yconv_ref[...] = jnp.dot(ycin, w_ref[...], preferred_element_type=F32)

    row = lambda p: pl.BlockSpec((tm, d), lambda i, p=p: (i, p))
    halo = lambda p: pl.BlockSpec((HALO, d), lambda i, p=p: (jnp.maximum(i * hb - 1, 0), p))
    vec = pl.BlockSpec((1, d), lambda i: (0, 0))
    return pl.pallas_call(
        body, grid=(lp // tm,),
        in_specs=[row(0), row(1), row(2), halo(0), halo(1),
                  pl.BlockSpec((HALO, d), lambda i: (0, 0)), vec, vec, vec,
                  pl.BlockSpec((d, d), lambda i: (0, 0))],
        out_specs=[pl.BlockSpec((tm, d), lambda i: (i, 0))] * 3,
        out_shape=[jax.ShapeDtypeStruct((lp, d), F32), jax.ShapeDtypeStruct((lp, d), BF16),
                   jax.ShapeDtypeStruct((lp, d), F32)],
        scratch_shapes=[pltpu.VMEM((HALO + tm, d), F32)],
        name="conv_fwd", compiler_params=_params())(
            proj, proj, proj, proj, proj, conv_w, conv_b, ln_g, ln_b, w_conv)


def _lower_bound(lbl_ref):
    l0 = lbl_ref[0:1, :]
    l1 = lbl_ref[1:2, :]
    m = jnp.maximum(l0, l1)
    e0 = jnp.exp(l0 - m)
    e1 = jnp.exp(l1 - m)
    p0 = e0 / (e0 + e1)
    return p0, p0 * (e1 / (e0 + e1))


def _level_masks():
    r2 = lax.broadcasted_iota(jnp.int32, (CHUNK, CHUNK), 0)
    c2 = lax.broadcasted_iota(jnp.int32, (CHUNK, CHUNK), 1)
    out = []
    for lvl in range(1, N_LEVELS + 1):
        blk = CHUNK >> (lvl - 1)
        sh = blk.bit_length() - 1
        same = (r2 >> sh) == (c2 >> sh)
        t_upper = (r2 & (blk - 1)) >= (blk // 2)
        s_lower = (c2 & (blk - 1)) < (blk // 2)
        out.append(jnp.logical_and(same, jnp.logical_and(t_upper, s_lower)))
    return out


def _gates(qr, fr, lb, valid):
    sq = _sigmoid(qr)
    q = qr * sq
    sf = _sigmoid(fr)
    f = lb + (1.0 - lb) * sf
    g = jnp.where(valid, jnp.log(f), 0.0)
    k = jnp.where(valid, 1.0 - f, 0.0)
    return q, sq, f, sf, g, k


def _level_reference(lvl, b, t_ref, hs):
    if lvl >= FIRST_TABLE_LEVEL:
        base = CHUNK * (lvl - FIRST_TABLE_LEVEL + 1)
        return t_ref[base:base + CHUNK, hs]
    blk = CHUNK >> (lvl - 1)
    rows = [jnp.broadcast_to(b[m + blk // 2 - 1:m + blk // 2, :], (blk, HEAD)) for m in range(0, CHUNK, blk)]
    return rows[0] if len(rows) == 1 else jnp.concatenate(rows, axis=0)


def _level_factor(b, r):
    d = b - r
    return jnp.exp(jnp.minimum(d, -d))


def _hgrn_fwd(proj, lb_logits, n_pad):
    lp = proj.shape[0]
    d = lb_logits.shape[1]
    n_heads = d // HEAD
    nc = lp // CHUNK
    tab, _ = _hgrn_tables()
    n_tab = tab.shape[0]

    def body(qr_ref, fr_ref, ir_ref, lbl_ref, tab_ref, o_ref, sall_ref, s_ref, t_ref):
        n = pl.program_id(0)

        @pl.when(n == 0)
        def _():
            s_ref[...] = jnp.zeros_like(s_ref)

        sall_ref[0] = s_ref[...]
        lb_all, _ = _lower_bound(lbl_ref)
        rid = lax.broadcasted_iota(jnp.int32, (CHUNK, 1), 0)
        valid = jnp.logical_or(n > 0, rid >= n_pad)
        f_all = lb_all + (1.0 - lb_all) * _sigmoid(fr_ref[...])
        t_ref[...] = _dot2(tab_ref[...], jnp.where(valid, jnp.log(f_all), 0.0))
        masks = _level_masks()

        def head(h):
            off = h * HEAD if isinstance(h, int) else pl.multiple_of(h * HEAD, HEAD)
            hs = pl.ds(off, HEAD)
            lb = _lower_bound_slice(lbl_ref, hs)
            q, _, _, _, _, k = _gates(qr_ref[:, hs], fr_ref[:, hs], lb, valid)
            v = ir_ref[:, hs]
            b = t_ref[0:CHUNK, hs]
            s0 = s_ref[hs, :]
            o = jnp.dot((q * jnp.exp(b)).astype(BF16), s0.astype(BF16), preferred_element_type=F32)
            o = o + jnp.sum(q * k, axis=-1, keepdims=True) * v
            a = jnp.zeros((CHUNK, CHUNK), F32)
            for lvl in range(1, N_LEVELS + 1):
                e = _level_factor(b, _level_reference(lvl, b, t_ref, hs))
                p = lax.dot_general((q * e).astype(BF16), (k * e).astype(BF16), NT, preferred_element_type=F32)
                a = a + jnp.where(masks[lvl - 1], p, 0.0)
            vb = v.astype(BF16)
            o_ref[:, hs] = o + jnp.dot(a.astype(BF16), vb, preferred_element_type=F32)
            b_last = t_ref[CHUNK - 1:CHUNK, hs]
            khat = (k * jnp.exp(b_last - b)).astype(BF16)
            s_ref[hs, :] = _row_to_col(jnp.exp(b_last)) * s0 + lax.dot_general(khat, vb, TN, preferred_element_type=F32)
        per_trip = min(HEADS_PER_TRIP, n_heads)

        def head_group(p, carry):
            for u in range(per_trip):
                head(p * per_trip + u)
            return carry

        if n_heads == per_trip:
            head_group(0, 0)
        else:
            lax.fori_loop(0, n_heads // per_trip, head_group, 0)

    piece = lambda p: pl.BlockSpec((CHUNK, d), lambda n, p=p: (n, p))
    return pl.pallas_call(
        body, grid=(nc,),
        in_specs=[piece(3), piece(4), piece(5), pl.BlockSpec((2, d), lambda n: (0, 0)),
                  pl.BlockSpec((n_tab, CHUNK), lambda n: (0, 0))],
        out_specs=[pl.BlockSpec((CHUNK, d), lambda n: (n, 0)), pl.BlockSpec((1, d, HEAD), lambda n: (n, 0, 0))],
        out_shape=[jax.ShapeDtypeStruct((lp, d), F32), jax.ShapeDtypeStruct((nc, d, HEAD), F32)],
        scratch_shapes=[pltpu.VMEM((d, HEAD), F32), pltpu.VMEM((n_tab, d), F32)],
        name="hgrn_fwd", compiler_params=_params())(proj, proj, proj, lb_logits, tab)


def _lower_bound_slice(lbl_ref, hs):
    l0 = lbl_ref[0:1, hs]
    l1 = lbl_ref[1:2, hs]
    m = jnp.maximum(l0, l1)
    e0 = jnp.exp(l0 - m)
    e1 = jnp.exp(l1 - m)
    return e0 / (e0 + e1)


def _tail_fwd(o, proj, y_conv, hres, target, gnorm_g, final_g, w_rec, w_out):
    lp, d = o.shape
    n_heads = d // HEAD
    tm = _row_tile(lp)

    n_slabs = tm // CHUNK

    def body(o_ref, gr_ref, mc_ref, mr_ref, yc_ref, x_ref, gn_ref, fg_ref, wr_ref, wo_ref, *rest):
        t_refs = rest[:n_slabs]
        yrin_ref, gwo_ref, yrec_ref, dout_ref, loss_ref, dfg_ref = rest[n_slabs:]
        i = pl.program_id(0)

        @pl.when(i == 0)
        def _():
            loss_ref[...] = jnp.zeros_like(loss_ref)
            dfg_ref[...] = jnp.zeros_like(dfg_ref)
            gwo_ref[...] = jnp.zeros_like(gwo_ref)

        for h in range(n_heads):
            hs = slice(h * HEAD, (h + 1) * HEAD)
            oh = o_ref[:, hs]
            on = oh * lax.rsqrt(jnp.mean(oh * oh, axis=-1, keepdims=True) + EPS) * gn_ref[:, hs]
            gr = gr_ref[:, hs]
            yrin_ref[:, hs] = (on * (gr * _sigmoid(gr))).astype(BF16)
        yrec = jnp.dot(yrin_ref[...], wr_ref[...], preferred_element_type=F32)
        yrec_ref[...] = yrec
        merged = (_sigmoid(mc_ref[...]) * yc_ref[...] + _sigmoid(mr_ref[...]) * yrec).astype(BF16)
        out = x_ref[...] + jnp.dot(merged, wo_ref[...], preferred_element_type=F32)
        r = lax.rsqrt(jnp.mean(out * out, axis=-1, keepdims=True) + EPS)
        yhat = out * r
        fg = fg_ref[...]
        rid = lax.broadcasted_iota(jnp.int32, (tm, 1), 0) + i * tm
        tgt = jnp.concatenate([t[...] for t in t_refs], axis=0)
        err = jnp.where(rid >= CHUNK, yhat * fg - tgt, 0.0)
        loss_ref[...] += 0.5 * jnp.sum(err * err) / d
        dy = err / d
        dfg_ref[...] += jnp.sum(dy * yhat, axis=0, keepdims=True)
        dyh = dy * fg
        dout = r * (dyh - yhat * jnp.mean(dyh * yhat, axis=-1, keepdims=True))
        dout_ref[...] = dout
        gwo_ref[...] += lax.dot_general(merged, dout.astype(BF16), TN, preferred_element_type=F32)

    row = lambda p: pl.BlockSpec((tm, d), lambda i, p=p: (i, p))
    vec = pl.BlockSpec((1, d), lambda i: (0, 0))
    mat = pl.BlockSpec((d, d), lambda i: (0, 0))
    return pl.pallas_call(
        body, grid=(lp // tm,),
        in_specs=[row(0), row(6), row(7), row(8), row(0), row(0), vec, vec, mat, mat]
        + [pl.BlockSpec((CHUNK, d), lambda i, u=u: (jnp.maximum(i * n_slabs + u - 1, 0), 0)) for u in range(n_slabs)],
        out_specs=[row(0), mat, row(0), row(0), pl.BlockSpec((8, 128), lambda i: (0, 0)), vec],
        out_shape=[jax.ShapeDtypeStruct((lp, d), BF16), jax.ShapeDtypeStruct((d, d), F32),
                   jax.ShapeDtypeStruct((lp, d), F32), jax.ShapeDtypeStruct((lp, d), F32),
                   jax.ShapeDtypeStruct((8, 128), F32), jax.ShapeDtypeStruct((1, d), F32)],
        name="tail_fwd", compiler_params=_params())(
            o, proj, proj, proj, y_conv, hres, gnorm_g, final_g, w_rec, w_out, *([target] * n_slabs))


def _tail_bwd(dout, proj, y_conv, y_rec, o, c, w_out, w_rec, w_conv, ln_g, ln_b, gnorm_g):
    lp, d = dout.shape
    n_heads = d // HEAD
    tm = _row_tile(lp)

    def body(dout_ref, mc_ref, mr_ref, z_ref, gr_ref, yc_ref, yrec_ref, o_ref, c_ref,
             wo_ref, wr_ref, wc_ref, lg_ref, lb_ref, gn_ref,
             dyc_ref, dyr_ref, doutb_ref, dz_ref, dp_ref, do_ref, dc_ref,
             dgn_ref, dlg_ref, dlb_ref, dyrin_ref):
        i = pl.program_id(0)

        @pl.when(i == 0)
        def _():
            dgn_ref[...] = jnp.zeros_like(dgn_ref)
            dlg_ref[...] = jnp.zeros_like(dlg_ref)
            dlb_ref[...] = jnp.zeros_like(dlb_ref)

        doutb = dout_ref[...].astype(BF16)
        doutb_ref[...] = doutb
        dmerged = lax.dot_general(doutb, wo_ref[...], NT, preferred_element_type=F32)
        smc = _sigmoid(mc_ref[...])
        smr = _sigmoid(mr_ref[...])
        dyc = (dmerged * smc).astype(BF16)
        dyr = (dmerged * smr).astype(BF16)
        dyc_ref[...] = dyc
        dyr_ref[...] = dyr
        dp_ref[:, d:2 * d] = (dmerged * yc_ref[...] * smc * (1.0 - smc)).astype(BF16)
        dp_ref[:, 2 * d:3 * d] = (dmerged * yrec_ref[...] * smr * (1.0 - smr)).astype(BF16)

        dyrin_ref[...] = lax.dot_general(dyr, wr_ref[...], NT, preferred_element_type=F32)
        for h in range(n_heads):
            hs = slice(h * HEAD, (h + 1) * HEAD)
            oh = o_ref[:, hs]
            rstd = lax.rsqrt(jnp.mean(oh * oh, axis=-1, keepdims=True) + EPS)
            ohat = oh * rstd
            gn = gn_ref[:, hs]
            gr = gr_ref[:, hs]
            sg = _sigmoid(gr)
            dyrin = dyrin_ref[:, hs]
            don = dyrin * (gr * sg)
            dp_ref[:, hs] = (dyrin * (ohat * gn) * _dsilu(gr, sg)).astype(BF16)
            dgn_ref[:, hs] += jnp.sum(don * ohat, axis=0, keepdims=True)
            doh = don * gn
            do_ref[:, hs] = rstd * (doh - ohat * jnp.mean(doh * ohat, axis=-1, keepdims=True))

        dycin = lax.dot_general(dyc, wc_ref[...], NT, preferred_element_type=F32)
        c = c_ref[...]
        mu = jnp.mean(c, axis=-1, keepdims=True)
        xc = c - mu
        rstd = lax.rsqrt(jnp.mean(xc * xc, axis=-1, keepdims=True) + EPS)
        nrm = xc * rstd
        lg = lg_ref[...]
        ln = nrm * lg + lb_ref[...]
        sl = _sigmoid(ln)
        z = z_ref[...]
        sz = _sigmoid(z)
        dz_ref[...] = (dycin * (ln * sl) * _dsilu(z, sz)).astype(BF16)
        dln = dycin * (z * sz) * _dsilu(ln, sl)
        dlg_ref[...] += jnp.sum(dln * nrm, axis=0, keepdims=True)
        dlb_ref[...] += jnp.sum(dln, axis=0, keepdims=True)
        dn = dln * lg
        dc_ref[...] = rstd * (dn - jnp.mean(dn, axis=-1, keepdims=True)
                              - nrm * jnp.mean(dn * nrm, axis=-1, keepdims=True))

    row = lambda p: pl.BlockSpec((tm, d), lambda i, p=p: (i, p))
    vec = pl.BlockSpec((1, d), lambda i: (0, 0))
    mat = pl.BlockSpec((d, d), lambda i: (0, 0))
    act_bf = jax.ShapeDtypeStruct((lp, d), BF16)
    act_f32 = jax.ShapeDtypeStruct((lp, d), F32)
    vec_f32 = jax.ShapeDtypeStruct((1, d), F32)
    return pl.pallas_call(
        body, grid=(lp // tm,),
        in_specs=[row(0), row(7), row(8), row(2), row(6), row(0), row(0), row(0), row(0),
                  mat, mat, mat, vec, vec, vec],
        out_specs=[row(0)] * 4 + [pl.BlockSpec((tm, 3 * d), lambda i: (i, 2))] + [row(0)] * 2 + [vec] * 3,
        out_shape=[act_bf] * 4 + [jax.ShapeDtypeStruct((lp, 9 * d), BF16)] + [act_f32] * 2 + [vec_f32] * 3,
        scratch_shapes=[pltpu.VMEM((tm, d), F32)],
        name="tail_bwd", compiler_params=_params())(
            dout, proj, proj, proj, proj, y_conv, y_rec, o, c, w_out, w_rec, w_conv, ln_g, ln_b, gnorm_g)


def _hgrn_bwd(proj, do, s_all, lb_logits, n_pad, dproj):
    lp, d = do.shape
    n_heads = d // HEAD
    nc = lp // CHUNK
    tab, utri = _hgrn_tables()
    n_tab = tab.shape[0]

    def body(qr_ref, fr_ref, ir_ref, do_ref, s0_ref, lbl_ref, tab_ref, ut_ref, _,
             dp_ref, dlbl_ref, ds_ref, t_ref, dlb_ref):
        n = pl.program_id(0)
        chunk = nc - 1 - n

        @pl.when(n == 0)
        def _():
            ds_ref[...] = jnp.zeros_like(ds_ref)
            dlb_ref[...] = jnp.zeros_like(dlb_ref)

        lb_all, pp = _lower_bound(lbl_ref)
        rid = lax.broadcasted_iota(jnp.int32, (CHUNK, 1), 0)
        valid = jnp.logical_or(chunk > 0, rid >= n_pad)
        f_all = lb_all + (1.0 - lb_all) * _sigmoid(fr_ref[...])
        t_ref[...] = _dot2(tab_ref[...], jnp.where(valid, jnp.log(f_all), 0.0))
        masks = _level_masks()
        ut = ut_ref[...]

        def head(h):
            off = h * HEAD if isinstance(h, int) else pl.multiple_of(h * HEAD, HEAD)
            hs = pl.ds(off, HEAD)
            lb = _lower_bound_slice(lbl_ref, hs)
            qr = qr_ref[:, hs]
            q, sq, f, sf, _, k = _gates(qr, fr_ref[:, hs], lb, valid)
            v = ir_ref[:, hs]
            do_h = do_ref[:, hs]
            b = t_ref[0:CHUNK, hs]
            b_last = t_ref[CHUNK - 1:CHUNK, hs]
            s0 = s0_ref[0, hs, :]
            ds1 = ds_ref[hs, :]
            eb = jnp.exp(b)
            ekl = jnp.exp(b_last - b)
            do_bf = do_h.astype(BF16)
            v_bf = v.astype(BF16)
            ds1_bf = ds1.astype(BF16)

            da = lax.dot_general(do_bf, v_bf, NT, preferred_element_type=F32)
            da_diag = jnp.sum(do_h * v, axis=-1, keepdims=True)
            a = jnp.zeros((CHUNK, CHUNK), F32)
            dq_x = eb * lax.dot_general(do_bf, s0.astype(BF16), NT, preferred_element_type=F32)
            dk_x = ekl * lax.dot_general(v_bf, ds1_bf, NT, preferred_element_type=F32)
            x_after = q * dq_x
            x_before = k * dk_x
            for lvl in range(1, N_LEVELS + 1):
                e = _level_factor(b, _level_reference(lvl, b, t_ref, hs))
                qt = (q * e).astype(BF16)
                kt = (k * e).astype(BF16)
                p = lax.dot_general(qt, kt, NT, preferred_element_type=F32)
                a = a + jnp.where(masks[lvl - 1], p, 0.0)
                dam = jnp.where(masks[lvl - 1], da, 0.0).astype(BF16)
                dqt = jnp.dot(dam, kt, preferred_element_type=F32)
                dkt = lax.dot_general(dam, qt, TN, preferred_element_type=F32)
                dq_x = dq_x + e * dqt
                dk_x = dk_x + e * dkt
                x_after = x_after + (qt.astype(F32) * dqt - kt.astype(F32) * dkt)

            dv = (lax.dot_general(a.astype(BF16), do_bf, TN, preferred_element_type=F32)
                  + jnp.sum(q * k, axis=-1, keepdims=True) * do_h
                  + jnp.dot((k * ekl).astype(BF16), ds1_bf, preferred_element_type=F32))
            dp_ref[:, pl.ds(2 * d + off, HEAD)] = dv.astype(BF16)

            carried = jnp.exp(b_last) * _col_to_row(jnp.sum(s0 * ds1, axis=-1, keepdims=True))
            dg = _dot3(ut, jnp.concatenate([x_after, x_before], axis=0)) + carried
            dq = dq_x + da_diag * k
            dk = dk_x + da_diag * q
            dp_ref[:, hs] = (dq * _dsilu(qr, sq)).astype(BF16)
            df = jnp.where(valid, dg / f - dk, 0.0)
            dp_ref[:, pl.ds(d + off, HEAD)] = (df * (1.0 - lb) * sf * (1.0 - sf)).astype(BF16)
            dlb_ref[:, hs] += jnp.sum(df * (1.0 - sf), axis=0, keepdims=True)

            ds_ref[hs, :] = (_row_to_col(jnp.exp(b_last)) * ds1
                             + lax.dot_general((q * eb).astype(BF16), do_bf, TN, preferred_element_type=F32))
        per_trip = min(HEADS_PER_TRIP, n_heads)

        def head_group(p, carry):
            for u in range(per_trip):
                head(p * per_trip + u)
            return carry

        if n_heads == per_trip:
            head_group(0, 0)
        else:
            lax.fori_loop(0, n_heads // per_trip, head_group, 0)

        @pl.when(n == nc - 1)
        def _():
            dl0 = dlb_ref[...] * pp
            dlbl_ref[0:1, :] = dl0
            dlbl_ref[1:2, :] = -dl0

    piece = lambda p: pl.BlockSpec((CHUNK, d), lambda n, p=p: (nc - 1 - n, p))
    return pl.pallas_call(
        body, grid=(nc,),
        in_specs=[piece(3), piece(4), piece(5), piece(0),
                  pl.BlockSpec((1, d, HEAD), lambda n: (nc - 1 - n, 0, 0)),
                  pl.BlockSpec((2, d), lambda n: (0, 0)),
                  pl.BlockSpec((n_tab, CHUNK), lambda n: (0, 0)),
                  pl.BlockSpec((CHUNK, 2 * CHUNK), lambda n: (0, 0)), ANY],
        out_specs=[pl.BlockSpec((CHUNK, 3 * d), lambda n: (nc - 1 - n, 1)), pl.BlockSpec((2, d), lambda n: (0, 0))],
        out_shape=[jax.ShapeDtypeStruct(dproj.shape, BF16), jax.ShapeDtypeStruct((2, d), F32)],
        input_output_aliases={8: 0},
        scratch_shapes=[pltpu.VMEM((d, HEAD), F32), pltpu.VMEM((n_tab, d), F32), pltpu.VMEM((1, d), F32)],
        name="hgrn_bwd", compiler_params=_params())(proj, proj, proj, do, s_all, lb_logits, tab, utri, dproj)


def _conv_bwd(dc, proj, conv_w, dz, dproj):
    lp, d = dc.shape
    tm = _row_tile(lp)
    hb = tm // HALO
    n_tiles = lp // tm
    last_halo = lp // HALO - 1

    def body(dc_ref, dcn_ref, ua_ref, ub_ref, uap_ref, ubp_ref, cw_ref, dz_ref, _,
             dp_ref, dcw_ref, dcb_ref, aext_ref, dcext_ref, da_ref, dcw_acc):
        i = pl.program_id(0)

        @pl.when(i == 0)
        def _():
            dcw_acc[...] = jnp.zeros_like(dcw_acc)
            dcb_ref[...] = jnp.zeros_like(dcb_ref)

        ua = ua_ref[...]
        sb = _sigmoid(ub_ref[...])
        a_prev = uap_ref[...] * _sigmoid(ubp_ref[...])
        aext_ref[0:HALO, :] = jnp.where(i > 0, a_prev, 0.0)
        aext_ref[HALO:HALO + tm, :] = ua * sb
        dcext_ref[0:tm, :] = dc_ref[...]
        dcext_ref[tm:tm + HALO, :] = jnp.where(i < n_tiles - 1, dcn_ref[...], 0.0)
        dcb_ref[...] += jnp.sum(dc_ref[...], axis=0, keepdims=True)

        def row_block(r, carry):
            r0 = pl.multiple_of(r * CONV_ROWS, CONV_ROWS)
            n_rows = CONV_ROWS + HALO
            for cs in range(d // CONV_LANES):
                cl = slice(cs * CONV_LANES, (cs + 1) * CONV_LANES)
                dblk = dcext_ref[pl.ds(r0, n_rows), cl]
                ablk = aext_ref[pl.ds(r0, n_rows), cl]
                dcur = dblk[0:CONV_ROWS, :]
                acc = jnp.zeros((CONV_ROWS, CONV_LANES), F32)
                for b in range(SUBLANES):
                    dsh = dblk if b == 0 else pltpu.roll(dblk, n_rows - b, axis=0)
                    ash = ablk if b == 0 else pltpu.roll(ablk, n_rows - b, axis=0)
                    for a in range(5):
                        j_da = CONV_WIDTH - 1 - (SUBLANES * a + b)
                        if 0 <= j_da < CONV_WIDTH:
                            acc = acc + cw_ref[j_da:j_da + 1, cl] * dsh[SUBLANES * a:SUBLANES * a + CONV_ROWS, :]
                        j_w = SUBLANES * a + b - 2
                        if 0 <= j_w < CONV_WIDTH:
                            prod = dcur * ash[SUBLANES * a:SUBLANES * a + CONV_ROWS, :]
                            dcw_acc[j_w, :, cl] += prod.reshape(CONV_ROWS // SUBLANES, SUBLANES, CONV_LANES).sum(axis=0)
                da_ref[pl.ds(r0, CONV_ROWS), cl] = acc
            return carry

        lax.fori_loop(0, tm // CONV_ROWS, row_block, 0)

        da = da_ref[...]
        dp_ref[:, 0:d] = (da * sb).astype(BF16)
        dp_ref[:, d:2 * d] = (da * ua * sb * (1.0 - sb)).astype(BF16)
        dp_ref[:, 2 * d:3 * d] = dz_ref[...]

        @pl.when(i == n_tiles - 1)
        def _():
            dcw_ref[...] = jnp.sum(dcw_acc[...], axis=1)

    row = lambda p: pl.BlockSpec((tm, d), lambda i, p=p: (i, p))
    prev = lambda p: pl.BlockSpec((HALO, d), lambda i, p=p: (jnp.maximum(i * hb - 1, 0), p))
    nxt = pl.BlockSpec((HALO, d), lambda i: (jnp.minimum((i + 1) * hb, last_halo), 0))
    return pl.pallas_call(
        body, grid=(n_tiles,),
        in_specs=[row(0), nxt, row(0), row(1), prev(0), prev(1), pl.BlockSpec((HALO, d), lambda i: (0, 0)),
                  row(0), ANY],
        out_specs=[pl.BlockSpec((tm, 3 * d), lambda i: (i, 0)), pl.BlockSpec((HALO, d), lambda i: (0, 0)),
                   pl.BlockSpec((1, d), lambda i: (0, 0))],
        out_shape=[jax.ShapeDtypeStruct(dproj.shape, BF16),
                   jax.ShapeDtypeStruct((HALO, d), F32), jax.ShapeDtypeStruct((1, d), F32)],
        input_output_aliases={8: 0},
        scratch_shapes=[pltpu.VMEM((HALO + tm, d), F32), pltpu.VMEM((tm + HALO, d), F32), pltpu.VMEM((tm, d), F32),
                        pltpu.VMEM((HALO, SUBLANES, d), F32)],
        name="conv_bwd", compiler_params=_params())(dc, dc, proj, proj, proj, proj, conv_w, dz, dproj)


def _weight_grad(xs, dy, name, blocked):
    lp, dx = xs.shape
    n = dy.shape[1]
    tk = _mm_row_tile(lp)
    if blocked:
        ncol = n // N_CHIPS
        nt = W_IN_COL_TILES
        tn = ncol // nt
        grid = (N_CHIPS * nt, lp // tk)
        out_spec = pl.BlockSpec((1, dx, tn), lambda c, k: (c // nt, 0, c % nt))
        out_shape = jax.ShapeDtypeStruct((N_CHIPS, dx, ncol), F32)
    else:
        tn = n
        grid = (1, lp // tk)
        out_spec = pl.BlockSpec((dx, tn), lambda c, k: (0, c))
        out_shape = jax.ShapeDtypeStruct((dx, n), F32)

    def body(xs_ref, dy_ref, o_ref, *copy_ref):
        @pl.when(pl.program_id(1) == 0)
        def _():
            o_ref[...] = jnp.zeros_like(o_ref)

        p = lax.dot_general(xs_ref[...], dy_ref[...], TN, preferred_element_type=F32)
        if blocked:
            o_ref[0] += p

            @pl.when(pl.program_id(1) == lp // tk - 1)
            def _():
                copy_ref[0][0] = o_ref[0].astype(BF16)
        else:
            o_ref[...] += p

    if blocked:
        out_spec = [out_spec, out_spec]
        out_shape = [out_shape, jax.ShapeDtypeStruct(out_shape.shape, BF16)]
    return pl.pallas_call(
        body, grid=grid,
        in_specs=[pl.BlockSpec((tk, dx), lambda c, k: (k, 0)), pl.BlockSpec((tk, tn), lambda c, k: (k, c))],
        out_specs=out_spec, out_shape=out_shape,
        name=name, compiler_params=_params())(xs, dy)


def _in_proj_bwd(dproj, wtg, hres, norm_g, dout):
    lp, d = hres.shape
    _, ncol, _ = wtg.shape
    tm = _mm_row_tile(lp)
    nt = W_IN_COL_TILES
    tn = ncol // nt
    nk = N_CHIPS * nt

    def body(dp_ref, w_ref, x_ref, g_ref, dout_ref, dx_ref, dg_ref, acc_ref):
        i = pl.program_id(0)
        kk = pl.program_id(1)

        @pl.when(jnp.logical_and(i == 0, kk == 0))
        def _():
            dg_ref[...] = jnp.zeros_like(dg_ref)

        @pl.when(kk == 0)
        def _():
            acc_ref[...] = jnp.zeros_like(acc_ref)

        acc_ref[...] += jnp.dot(dp_ref[...], w_ref[0], preferred_element_type=F32)

        @pl.when(kk == nk - 1)
        def _():
            x = x_ref[...]
            r = lax.rsqrt(jnp.mean(x * x, axis=-1, keepdims=True) + EPS)
            xhat = x * r
            dh = acc_ref[...]
            dg_ref[...] += jnp.sum(dh * xhat, axis=0, keepdims=True)
            dxh = dh * g_ref[...]
            dx_ref[...] = dout_ref[...] + r * (dxh - xhat * jnp.mean(dxh * xhat, axis=-1, keepdims=True))

    return pl.pallas_call(
        body, grid=(lp // tm, nk),
        in_specs=[pl.BlockSpec((tm, tn), lambda i, k: (i, k)),
                  pl.BlockSpec((1, tn, d), lambda i, k: (k // nt, k % nt, 0)),
                  pl.BlockSpec((tm, d), lambda i, k: (i, 0)),
                  pl.BlockSpec((1, d), lambda i, k: (0, 0)),
                  pl.BlockSpec((tm, d), lambda i, k: (i, 0))],
        out_specs=[pl.BlockSpec((tm, d), lambda i, k: (i, 0)), pl.BlockSpec((1, d), lambda i, k: (0, 0))],
        out_shape=[jax.ShapeDtypeStruct((lp, d), F32), jax.ShapeDtypeStruct((1, d), F32)],
        scratch_shapes=[pltpu.VMEM((tm, d), F32)],
        name="in_proj_bwd", compiler_params=_params())(dproj, wtg, hres, norm_g, dout)


def _adamw_math(w, g, m, v):
    m = ADAM_B1 * m + (1.0 - ADAM_B1) * g
    v = ADAM_B2 * v + (1.0 - ADAM_B2) * (g * g)
    m_hat = m / (1.0 - ADAM_B1 ** ADAM_STEP)
    v_hat = v / (1.0 - ADAM_B2 ** ADAM_STEP)
    delta = -ADAM_LR * (m_hat / (jnp.sqrt(v_hat) + ADAM_EPS) + ADAM_WD * w)
    return delta, m, v


def _elementwise_rows(shape):
    r, c = shape
    for t in (256, 128, 64, 32, 16, 8):
        if r % t == 0 and r > t and t * c * 4 <= ELEMENTWISE_BLOCK_BYTES:
            return t
    return r


def _adamw(name, w, m, v, *g_parts):
    shape = w.shape
    tr = _elementwise_rows(shape)
    n_g = len(g_parts)

    def body(*refs):
        w_ref, m_ref, v_ref = refs[:3]
        g_refs = refs[3:3 + n_g]
        g_out, d_out, m_out, v_out = refs[3 + n_g:]
        g = g_refs[0][...]
        for gr in g_refs[1:]:
            g = g + gr[...]
        delta, m_new, v_new = _adamw_math(w_ref[...], g, m_ref[...], v_ref[...])
        g_out[...] = g
        d_out[...] = delta
        m_out[...] = m_new
        v_out[...] = v_new

    spec = pl.BlockSpec((tr, shape[1]), lambda i: (i, 0))
    return pl.pallas_call(
        body, grid=(shape[0] // tr,),
        in_specs=[spec] * (3 + n_g), out_specs=[spec] * 4,
        out_shape=[jax.ShapeDtypeStruct(shape, F32)] * 4,
        name=name, compiler_params=_params())(w, m, v, *g_parts)


def _chip_half_sum(name, g, recv, core):
    _, _, hr, cols = g.shape
    tr = _elementwise_rows((hr, cols))

    def body(core_ref, g_ref, r_ref, o_ref, ob_ref):
        s = g_ref[0, 0] + r_ref[0].astype(F32)
        o_ref[0] = s
        ob_ref[0] = s.astype(BF16)

    blk = pl.BlockSpec((1, tr, cols), lambda j, i, core_ref: (j, i, 0))
    grid_spec = pltpu.PrefetchScalarGridSpec(
        num_scalar_prefetch=1, grid=(N_CHIPS, hr // tr),
        in_specs=[pl.BlockSpec((1, 1, tr, cols), lambda j, i, core_ref: (j, core_ref[0], i, 0)), blk],
        out_specs=[blk, blk])
    return pl.pallas_call(
        body, grid_spec=grid_spec,
        out_shape=[jax.ShapeDtypeStruct((N_CHIPS, hr, cols), F32), jax.ShapeDtypeStruct((N_CHIPS, hr, cols), BF16)],
        name=name, compiler_params=_params())(core, g, recv)


def _block_half_total(name, chip_sums, recv, chip_core, after):
    _, hr, cols = chip_sums.shape
    tr = _elementwise_rows((hr, cols))

    def body(cc_ref, p_ref, r_ref, after_ref, o_ref):
        s = p_ref[0]
        for k in range(3):
            s = s + r_ref[k].astype(F32)
        o_ref[0] = s

    grid_spec = pltpu.PrefetchScalarGridSpec(
        num_scalar_prefetch=1, grid=(hr // tr,),
        in_specs=[pl.BlockSpec((1, tr, cols), lambda i, cc_ref: (cc_ref[0], i, 0)),
                  pl.BlockSpec((3, tr, cols), lambda i, cc_ref: (0, i, 0)), ANY],
        out_specs=pl.BlockSpec((1, tr, cols), lambda i, cc_ref: (cc_ref[1], i, 0)))
    return pl.pallas_call(
        body, grid_spec=grid_spec, out_shape=jax.ShapeDtypeStruct((2, hr, cols), F32),
        name=name, compiler_params=_params())(chip_core, chip_sums, recv, after)


def _place_shard(name, w, chip, dtype):
    r, c = w.shape
    tr = _elementwise_rows((r, c))

    def body(chip_ref, w_ref, o_ref):
        o_ref[0] = w_ref[...].astype(dtype)

    grid_spec = pltpu.PrefetchScalarGridSpec(
        num_scalar_prefetch=1, grid=(r // tr,),
        in_specs=[pl.BlockSpec((tr, c), lambda i, chip_ref: (i, 0))],
        out_specs=pl.BlockSpec((1, tr, c), lambda i, chip_ref: (chip_ref[0], i, 0)))
    return pl.pallas_call(
        body, grid_spec=grid_spec, out_shape=jax.ShapeDtypeStruct((N_CHIPS, r, c), dtype),
        name=name, compiler_params=_params())(chip, w)


def _sum_slots(name, slots, own, my_idx):
    k, r, c = slots.shape

    def body(idx_ref, s_ref, own_ref, o_ref):
        s = None
        for j in range(k):
            term = jnp.where(idx_ref[0] == j, own_ref[...], s_ref[j])
            s = term if s is None else s + term
        o_ref[...] = s

    grid_spec = pltpu.PrefetchScalarGridSpec(
        num_scalar_prefetch=1, grid=(1,),
        in_specs=[pl.BlockSpec((k, r, c), lambda i, idx_ref: (0, 0, 0)),
                  pl.BlockSpec((r, c), lambda i, idx_ref: (0, 0))],
        out_specs=pl.BlockSpec((r, c), lambda i, idx_ref: (0, 0)))
    return pl.pallas_call(body, grid_spec=grid_spec, out_shape=jax.ShapeDtypeStruct((r, c), F32), name=name,
                          compiler_params=_params())(my_idx, slots, own)


def _mesh_pos():
    return lax.axis_index("x"), lax.axis_index("y"), lax.axis_index("c")


def _other_chips(x, y):
    return [(1 - x, y), (x, 1 - y), (1 - x, 1 - y)]


def _gather_weights(bufs):
    n = len(bufs)
    half = [b.shape[1] // 2 for b in bufs]

    def body(*refs):
        gathered = refs[n:2 * n]
        ici_send, ici_recv, d2d_send, d2d_recv = refs[2 * n:]
        x, y, c = _mesh_pos()
        me = 2 * x + y
        chips = _other_chips(x, y)

        def part(a, block, core):
            return gathered[a].at[block, pl.ds(core * half[a], half[a])]

        def over_ici(a, k, block):
            px, py = chips[k]
            return pltpu.make_async_remote_copy(
                src_ref=part(a, block, c), dst_ref=part(a, block, c),
                send_sem=ici_send.at[a, k], recv_sem=ici_recv.at[a, k],
                device_id=(px, py, c), device_id_type=MESH)

        def over_d2d(a, k, core):
            px, py = chips[k]
            return pltpu.make_async_remote_copy(
                src_ref=part(a, 2 * px + py, core), dst_ref=part(a, 2 * px + py, core),
                send_sem=d2d_send.at[a, k], recv_sem=d2d_recv.at[a, k],
                device_id=(x, y, 1 - c), device_id_type=MESH)

        for a in range(n):
            for k in range(3):
                over_ici(a, k, me).start()
        for a in range(n):
            for k, (px, py) in enumerate(chips):
                over_ici(a, k, 2 * px + py).wait_recv()
                over_d2d(a, k, c).start()
        for a in range(n):
            for k in range(3):
                over_d2d(a, k, 1 - c).wait_recv()
        for a in range(n):
            for k in range(3):
                over_ici(a, k, me).wait_send()
                over_d2d(a, k, c).wait_send()

    return pl.pallas_call(
        body, in_specs=[ANY] * n, out_specs=[ANY] * n,
        out_shape=[jax.ShapeDtypeStruct(b.shape, b.dtype) for b in bufs],
        input_output_aliases={a: a for a in range(n)},
        scratch_shapes=[pltpu.SemaphoreType.DMA((n, 3))] * 4,
        name="gather_weights")(*bufs)


def _gather_in_proj(h, bufs, order):
    n = len(bufs)
    half = [b.shape[1] // 2 for b in bufs]
    lp, d = h.shape
    ncol = bufs[0].shape[2]
    tm = _mm_row_tile(lp)
    n_row = lp // tm

    def body(order_ref, h_ref, *refs):
        gathered = refs[n:2 * n]
        o_ref, wt_ref = refs[2 * n], refs[2 * n + 1]
        w_buf, ici_send, ici_recv, d2d_send, d2d_recv, w_sem = refs[2 * n + 2:]
        j = pl.program_id(0)
        i = pl.program_id(1)
        x, y, c = _mesh_pos()
        me = 2 * x + y
        chips = _other_chips(x, y)

        def part(a, block, core):
            return gathered[a].at[block, pl.ds(core * half[a], half[a])]

        def over_ici(a, k, block):
            px, py = chips[k]
            return pltpu.make_async_remote_copy(
                src_ref=part(a, block, c), dst_ref=part(a, block, c),
                send_sem=ici_send.at[a, k], recv_sem=ici_recv.at[a, k],
                device_id=(px, py, c), device_id_type=MESH)

        def over_d2d(a, k, core):
            px, py = chips[k]
            return pltpu.make_async_remote_copy(
                src_ref=part(a, 2 * px + py, core), dst_ref=part(a, 2 * px + py, core),
                send_sem=d2d_send.at[a, k], recv_sem=d2d_recv.at[a, k],
                device_id=(x, y, 1 - c), device_id_type=MESH)

        @pl.when(jnp.logical_and(j == 0, i == 0))
        def _():
            for a in range(n):
                for k in range(2):
                    over_ici(a, k, me).start()

        for k, (px, py) in enumerate(chips):
            @pl.when(jnp.logical_and(j == k + 1, i == 0))
            def _(k=k, px=px, py=py):
                for a in range(n):
                    over_ici(a, k, 2 * px + py).wait_recv()
                    over_d2d(a, k, c).start()
                if k == 0:
                    for a in range(n):
                        over_ici(a, 2, me).start()
                for a in range(n):
                    over_d2d(a, k, 1 - c).wait_recv()

        @pl.when(i == 0)
        def _():
            load = pltpu.make_async_copy(gathered[0].at[order_ref[j]], w_buf, w_sem)
            load.start()
            load.wait()
            wt_ref[0] = w_buf[...].T

        o_ref[...] = jnp.dot(h_ref[...], w_buf[...], preferred_element_type=F32)

        @pl.when(jnp.logical_and(j == N_CHIPS - 1, i == n_row - 1))
        def _():
            for a in range(n):
                for k in range(3):
                    over_ici(a, k, me).wait_send()
                    over_d2d(a, k, c).wait_send()

    grid_spec = pltpu.PrefetchScalarGridSpec(
        num_scalar_prefetch=1, grid=(N_CHIPS, n_row),
        in_specs=[pl.BlockSpec((tm, d), lambda j, i, order_ref: (i, 0))] + [ANY] * n,
        out_specs=[ANY] * n + [pl.BlockSpec((tm, ncol), lambda j, i, order_ref: (i, order_ref[j])),
                               pl.BlockSpec((1, ncol, d), lambda j, i, order_ref: (order_ref[j], 0, 0))],
        scratch_shapes=[pltpu.VMEM((d, ncol), BF16)] + [pltpu.SemaphoreType.DMA((n, 3))] * 4
        + [pltpu.SemaphoreType.DMA])
    out = pl.pallas_call(
        body, grid_spec=grid_spec,
        out_shape=[jax.ShapeDtypeStruct(b.shape, b.dtype) for b in bufs]
        + [jax.ShapeDtypeStruct((lp, N_CHIPS * ncol), F32), jax.ShapeDtypeStruct((N_CHIPS, ncol, d), BF16)],
        input_output_aliases={2 + a: a for a in range(n)},
        name="gather_in_proj", compiler_params=_params())(order, h, *bufs)
    return out[n], out[n + 1], out[:n]


def _send_other_halves(grads, tag):
    n = len(grads)

    def body(*refs):
        srcs = refs[:n]
        dsts = refs[n:2 * n]
        send_sems, recv_sems = refs[2 * n:]
        x, y, c = _mesh_pos()
        copies = [pltpu.make_async_remote_copy(
            src_ref=srcs[a].at[j, 1 - c], dst_ref=dsts[a].at[j], send_sem=send_sems.at[a, j],
            recv_sem=recv_sems.at[a, j], device_id=(x, y, 1 - c), device_id_type=MESH)
            for a in range(n) for j in range(N_CHIPS)]
        for cp in copies:
            cp.start()
        for cp in copies:
            cp.wait()

    return pl.pallas_call(
        body, in_specs=[ANY] * n, out_specs=[ANY] * n,
        out_shape=[jax.ShapeDtypeStruct((N_CHIPS,) + g.shape[2:], g.dtype) for g in grads],
        scratch_shapes=[pltpu.SemaphoreType.DMA((n, N_CHIPS))] * 2,
        name="send_other_halves_" + tag)(*grads)


HBM = pl.BlockSpec(memory_space=pltpu.HBM)
SEM = pl.BlockSpec(memory_space=pltpu.SEMAPHORE)


def _block_copies(n, srcs, dsts, send_sems, recv_sems):
    x, y, c = _mesh_pos()
    return [pltpu.make_async_remote_copy(
        src_ref=srcs[a].at[2 * px + py], dst_ref=dsts[a].at[k], send_sem=send_sems.at[3 * a + k],
        recv_sem=recv_sems.at[3 * a + k], device_id=(px, py, c), device_id_type=MESH)
        for a in range(n) for k, (px, py) in enumerate(_other_chips(x, y))]


def _exchange_start(blocked, tag):
    n = len(blocked)
    lands = [lax.empty((3,) + b.shape[1:], b.dtype) for b in blocked]
    bufs = [pltpu.with_memory_space_constraint(b, pltpu.HBM) for b in list(blocked) + lands]
    nb = 2 * n

    def body(*refs):
        for cp in _block_copies(n, refs[:n], refs[n:nb], refs[nb], refs[nb + 1]):
            cp.start()
        refs[-1][...] = jnp.zeros_like(refs[-1])

    out = pl.pallas_call(
        body, name="exchange_start_" + tag,
        in_specs=[HBM] * nb,
        out_shape=[pltpu.SemaphoreType.DMA((3 * n,)), pltpu.SemaphoreType.DMA((3 * n,))]
        + [pltpu.HBM(b.shape, b.dtype) for b in bufs] + [jax.ShapeDtypeStruct((8, 128), F32)],
        out_specs=[SEM] * 2 + [HBM] * nb + [pl.BlockSpec(memory_space=pltpu.VMEM)],
        input_output_aliases={i: 2 + i for i in range(nb)},
        compiler_params=pltpu.CompilerParams(has_side_effects=pltpu.SideEffectType.DATAFLOW_SIDE_EFFECTING),
    )(*bufs)
    return (out[:2], out[2:2 + nb]), out[-1]


def _exchange_wait(state, after, tag):
    sems, bufs = state
    nb = len(bufs)
    n = nb // 2

    def body(*refs):
        for cp in _block_copies(n, refs[:n], refs[n:nb], refs[nb], refs[nb + 1]):
            cp.wait_send()
            cp.wait_recv()

    out = pl.pallas_call(
        body, name="exchange_wait_" + tag,
        in_specs=[HBM] * nb + [SEM] * 2 + [ANY],
        out_shape=[pltpu.HBM(b.shape, b.dtype) for b in bufs],
        out_specs=[HBM] * nb,
        input_output_aliases={i: i for i in range(nb)},
        compiler_params=pltpu.CompilerParams(has_side_effects=pltpu.SideEffectType.DATAFLOW_SIDE_EFFECTING),
    )(*bufs, *sems, after)
    return out[n:nb]


def _whole_block_copies(buf, send_sems, recv_sems, incoming):
    x, y, c = _mesh_pos()
    me = 2 * x + y
    out = []
    for k, (px, py) in enumerate(_other_chips(x, y)):
        block = 2 * px + py if incoming else me
        out.append(pltpu.make_async_remote_copy(
            src_ref=buf.at[block], dst_ref=buf.at[block], send_sem=send_sems.at[k], recv_sem=recv_sems.at[k],
            device_id=(px, py, c), device_id_type=MESH))
    return out


def _gather_start(buf, after, tag):
    buf = pltpu.with_memory_space_constraint(buf, pltpu.HBM)

    def body(buf_ref, after_ref, send_sems, recv_sems, thru_ref, token):
        for cp in _whole_block_copies(buf_ref, send_sems, recv_sems, incoming=False):
            cp.start()
        token[...] = jnp.zeros_like(token)

    out = pl.pallas_call(
        body, name="gather_start_" + tag,
        in_specs=[HBM, ANY],
        out_shape=[pltpu.SemaphoreType.DMA((3,)), pltpu.SemaphoreType.DMA((3,)), pltpu.HBM(buf.shape, buf.dtype),
                   jax.ShapeDtypeStruct((8, 128), F32)],
        out_specs=[SEM, SEM, HBM, pl.BlockSpec(memory_space=pltpu.VMEM)],
        input_output_aliases={0: 2},
        compiler_params=pltpu.CompilerParams(has_side_effects=pltpu.SideEffectType.DATAFLOW_SIDE_EFFECTING),
    )(buf, after)
    return out[:3], out[3]


def _gather_wait(state, after, tag):
    send_sems, recv_sems, buf = state

    def body(buf_ref, send_ref, recv_ref, after_ref, out_ref):
        for cp in _whole_block_copies(buf_ref, send_ref, recv_ref, incoming=True):
            cp.wait_send()
            cp.wait_recv()

    return pl.pallas_call(
        body, name="gather_wait_" + tag,
        in_specs=[HBM, SEM, SEM, ANY],
        out_shape=pltpu.HBM(buf.shape, buf.dtype), out_specs=HBM,
        input_output_aliases={0: 0},
        compiler_params=pltpu.CompilerParams(has_side_effects=pltpu.SideEffectType.DATAFLOW_SIDE_EFFECTING),
    )(buf, send_sems, recv_sems, after)


def _small_copies(small_ref, slots_ref, send_sems, recv_sems, incoming):
    x, y, c = _mesh_pos()
    out = []
    for r in range(1, 8):
        px = 1 - x if r & 4 else x
        py = 1 - y if r & 2 else y
        pc = 1 - c if r & 1 else c
        slot = 4 * px + 2 * py + pc if incoming else 4 * x + 2 * y + c
        out.append(pltpu.make_async_remote_copy(
            src_ref=small_ref, dst_ref=slots_ref.at[slot], send_sem=send_sems.at[r - 1],
            recv_sem=recv_sems.at[r - 1], device_id=(px, py, pc), device_id_type=MESH))
    return out


def _small_start(small):
    bufs = [pltpu.with_memory_space_constraint(b, pltpu.HBM)
            for b in (small, lax.empty((8,) + small.shape, small.dtype))]

    def body(small_ref, slots_ref, send_sems, recv_sems, small_thru, slots_thru, token):
        for cp in _small_copies(small_ref, slots_ref, send_sems, recv_sems, incoming=False):
            cp.start()
        token[...] = jnp.zeros_like(token)

    out = pl.pallas_call(
        body, name="small_start",
        in_specs=[HBM, HBM],
        out_shape=[pltpu.SemaphoreType.DMA((7,)), pltpu.SemaphoreType.DMA((7,))]
        + [pltpu.HBM(b.shape, b.dtype) for b in bufs] + [jax.ShapeDtypeStruct((8, 128), F32)],
        out_specs=[SEM, SEM, HBM, HBM, pl.BlockSpec(memory_space=pltpu.VMEM)],
        input_output_aliases={0: 2, 1: 3},
        compiler_params=pltpu.CompilerParams(has_side_effects=pltpu.SideEffectType.DATAFLOW_SIDE_EFFECTING),
    )(*bufs)
    return out[:4], out[4]


def _small_wait(state, after):
    send_sems, recv_sems, small, slots = state

    def body(small_ref, slots_ref, send_ref, recv_ref, after_ref, small_out, slots_out):
        for cp in _small_copies(small_ref, slots_ref, send_ref, recv_ref, incoming=True):
            cp.wait_send()
            cp.wait_recv()

    return pl.pallas_call(
        body, name="small_wait",
        in_specs=[HBM, HBM, SEM, SEM, ANY],
        out_shape=[pltpu.HBM(small.shape, small.dtype), pltpu.HBM(slots.shape, slots.dtype)],
        out_specs=[HBM, HBM], input_output_aliases={0: 0, 1: 1},
        compiler_params=pltpu.CompilerParams(has_side_effects=pltpu.SideEffectType.DATAFLOW_SIDE_EFFECTING),
    )(small, slots, send_sems, recv_sems, after)[1]


def _join_halves(bufs):
    n = len(bufs)

    def body(*refs):
        joined = refs[n:2 * n]
        send_sems, recv_sems = refs[2 * n:]
        x, y, c = _mesh_pos()
        for a in range(n):
            pltpu.make_async_remote_copy(
                src_ref=joined[a].at[c], dst_ref=joined[a].at[c], send_sem=send_sems.at[a],
                recv_sem=recv_sems.at[a], device_id=(x, y, 1 - c), device_id_type=MESH).start()
        for a in range(n):
            pltpu.make_async_remote_copy(
                src_ref=joined[a].at[c], dst_ref=joined[a].at[1 - c], send_sem=send_sems.at[a],
                recv_sem=recv_sems.at[a], device_id=(x, y, 1 - c), device_id_type=MESH).wait()

    return pl.pallas_call(
        body, in_specs=[ANY] * n, out_specs=[ANY] * n,
        out_shape=[jax.ShapeDtypeStruct(b.shape, b.dtype) for b in bufs],
        input_output_aliases={a: a for a in range(n)},
        scratch_shapes=[pltpu.SemaphoreType.DMA((n,))] * 2,
        name="join_halves")(*bufs)


def kernel(x, meta_tokens, norm_g, w_in, conv_w, conv_b, ln_g, ln_b, w_conv_out, lb_logits, gnorm_g, w_rec_out, w_out, final_g, loss_target, m_meta_tokens, m_norm_g, m_w_in, m_conv_w, m_conv_b, m_ln_g, m_ln_b, m_w_conv_out, m_lb_logits, m_gnorm_g, m_w_rec_out, m_w_out, m_final_g, v_meta_tokens, v_norm_g, v_w_in, v_conv_w, v_conv_b, v_ln_g, v_ln_b, v_w_conv_out, v_lb_logits, v_gnorm_g, v_w_rec_out, v_w_out, v_final_g):
    d = x.shape[2]
    n_meta = meta_tokens.shape[0]
    n_pad = CHUNK - n_meta
    ds = d // N_CHIPS
    chip = 2 * lax.axis_index("x") + lax.axis_index("y")

    conv_w_pad = jnp.pad(conv_w[0], ((0, HALO - CONV_WIDTH), (0, 0)))
    chip_idx = chip.astype(jnp.int32).reshape(1)
    (small_g,) = _gather_weights([
        _place_shard("place_small", jnp.concatenate([conv_w_pad, meta_tokens], axis=0), chip_idx, F32)])
    cw_full = jnp.transpose(small_g[:, 0:HALO], (1, 0, 2)).reshape(HALO, d)
    meta_full = jnp.transpose(small_g[:, HALO:HALO + n_meta], (1, 0, 2)).reshape(n_meta, d)

    hres = jnp.concatenate([jnp.zeros((n_pad, d), F32), meta_full, x[0]], axis=0)
    target = loss_target[0]
    final_g2 = final_g.reshape(1, d)
    h = _rmsnorm_fwd(hres, norm_g)
    fx, fy = 1 - lax.axis_index("x"), 1 - lax.axis_index("y")
    order = jnp.stack([chip, 2 * fx + (1 - fy), 2 * (1 - fx) + fy, 2 * fx + fy]).astype(jnp.int32)
    proj, win_t, _ = _gather_in_proj(h, [_place_shard("place_w_in", w_in[0], chip_idx, BF16)], order)
    sq_own = _place_shard("place_square", jnp.concatenate([w_conv_out[0], w_rec_out[0], w_out[0]], axis=0),
                          chip_idx, BF16)
    sq_flight, sq_token = _gather_start(sq_own, proj, "square")
    o, s_all = _hgrn_fwd(proj, lb_logits + sq_token[0:1, 0:1], n_pad)
    sq_g = _gather_wait(sq_flight, s_all, "square")
    wc_full = sq_g[:, 0:ds].reshape(d, d)
    wr_full = sq_g[:, ds:2 * ds].reshape(d, d)
    wo_full = sq_g[:, 2 * ds:3 * ds].reshape(d, d)
    c, yc_in, y_conv = _conv_fwd(proj, cw_full, conv_b, ln_g, ln_b, wc_full)
    yr_in, g_wo, y_rec, dout, loss_acc, dfinal_g = _tail_fwd(
        o, proj, y_conv, hres, target, gnorm_g, final_g2, wr_full, wo_full)

    (dyc, dyr, dout_bf, dz, dproj, do, dc, dgnorm_g, dln_g, dln_b) = _tail_bwd(
        dout, proj, y_conv, y_rec, o, c, wo_full, wr_full, wc_full, ln_g, ln_b, gnorm_g)
    g_wc = _weight_grad(yc_in, dyc, "grad_w_conv_out", False)
    g_wr = _weight_grad(yr_in, dyr, "grad_w_rec_out", False)

    core = lax.axis_index("c").astype(jnp.int32).reshape(1)

    def chip_sum_and_start(g, tag, g_to_sibling=None):
        halves = lambda a: a.reshape(N_CHIPS, 2, a.shape[1] // 2, a.shape[2])
        g = halves(g)
        (from_sibling,) = _send_other_halves([g if g_to_sibling is None else halves(g_to_sibling)], tag)
        sums = _chip_half_sum("chip_half_sum_" + tag, g, from_sibling, core)
        in_flight, token = _exchange_start([sums[1]], tag)
        return sums[0], in_flight, token[0:1, 0:1]

    g_sq = jnp.concatenate([g.reshape(N_CHIPS, ds, d) for g in (g_wc, g_wr, g_wo)], axis=1)
    sum_sq, flight_sq, token_sq = chip_sum_and_start(g_sq, "square")
    dproj, dlb_logits = _hgrn_bwd(proj, do, s_all, lb_logits + token_sq, n_pad, dproj)
    dproj, dconv_w, dconv_b = _conv_bwd(dc, proj, cw_full, dz, dproj)
    (recv_sq,) = _exchange_wait(flight_sq, dconv_b, "square")
    g_win, g_win_bf = _weight_grad(h, dproj, "grad_w_in", True)
    sum_win, flight_win, token_win = chip_sum_and_start(g_win, "w_in", g_win_bf)
    dhres, dnorm_g = _in_proj_bwd(dproj, win_t, hres, norm_g + token_win, dout)
    grad_x = dhres[CHUNK:][None]
    (recv_win,) = _exchange_wait(flight_win, dnorm_g, "w_in")
    small = jnp.concatenate([dnorm_g, dconv_b, dln_g, dln_b, dlb_logits, dgnorm_g, dfinal_g,
                             dhres[n_pad:CHUNK], dconv_w[:CONV_WIDTH],
                             jnp.broadcast_to(loss_acc[0:1, 0:1], (1, d))], axis=0)
    small_flight, small_token = _small_start(small)
    chip_core = jnp.concatenate([chip_idx, core])
    totals = [_block_half_total("block_half_total_" + nm, s, r, chip_core, small_token)
              for nm, s, r in zip(("w_in", "square"), (sum_win, sum_sq), (recv_win, recv_sq))]
    joined = _join_halves(totals)
    gt_win, gt_sq = [t.reshape(2 * t.shape[1], t.shape[2]) for t in joined]
    small_slots = _small_wait(small_flight, joined[1])
    device_idx = (2 * chip_idx + core).astype(jnp.int32)
    small_sum = _sum_slots("sum_small", small_slots, small, device_idx)

    res = {}
    res["w_in"] = _adamw("adamw_w_in", w_in[0], m_w_in[0], v_w_in[0], gt_win)
    res["w_conv_out"] = _adamw("adamw_w_conv_out", w_conv_out[0], m_w_conv_out[0], v_w_conv_out[0], gt_sq[0:ds])
    res["w_rec_out"] = _adamw("adamw_w_rec_out", w_rec_out[0], m_w_rec_out[0], v_w_rec_out[0], gt_sq[ds:2 * ds])
    res["w_out"] = _adamw("adamw_w_out", w_out[0], m_w_out[0], v_w_out[0], gt_sq[2 * ds:3 * ds])
    big = {k: tuple(a[None] for a in v) for k, v in res.items()}

    rep_names = ("norm_g", "conv_b", "ln_g", "ln_b", "lb_logits", "gnorm_g", "final_g")
    rep_w = (norm_g, conv_b, ln_g, ln_b, lb_logits, gnorm_g, final_g2)
    rep_m = (m_norm_g, m_conv_b, m_ln_g, m_ln_b, m_lb_logits, m_gnorm_g, m_final_g.reshape(1, d))
    rep_v = (v_norm_g, v_conv_b, v_ln_g, v_ln_b, v_lb_logits, v_gnorm_g, v_final_g.reshape(1, d))
    rep = _adamw("adamw_replicated", jnp.concatenate(rep_w, 0), jnp.concatenate(rep_m, 0),
                 jnp.concatenate(rep_v, 0), small_sum[0:8])
    rep_rows = {"norm_g": (0, 1), "conv_b": (1, 2), "ln_g": (2, 3), "ln_b": (3, 4), "lb_logits": (4, 6),
                "gnorm_g": (6, 7), "final_g": (7, 8)}
    small_out = {}
    for nm in rep_names:
        lo, hi = rep_rows[nm]
        vals = tuple(a[lo:hi] for a in rep)
        if nm == "final_g":
            vals = tuple(a.reshape(d) for a in vals)
        small_out[nm] = vals
    cw_row = 8 + n_meta
    g_meta = lax.dynamic_slice_in_dim(small_sum[8:cw_row], chip * ds, ds, axis=1)
    small_out["meta_tokens"] = _adamw("adamw_meta", meta_tokens, m_meta_tokens, v_meta_tokens, g_meta)
    g_cw = lax.dynamic_slice_in_dim(small_sum[cw_row:cw_row + HALO], chip * ds, ds, axis=1)
    pad_rows = ((0, HALO - CONV_WIDTH), (0, 0))
    cw_res = _adamw("adamw_conv_w", conv_w_pad, jnp.pad(m_conv_w[0], pad_rows),
                    jnp.pad(v_conv_w[0], pad_rows, constant_values=1.0), g_cw)
    small_out["conv_w"] = tuple(a[:CONV_WIDTH][None] for a in cw_res)

    loss = small_sum[cw_row + HALO - 1, 0]

    order = ("meta_tokens", "norm_g", "w_in", "conv_w", "conv_b", "ln_g", "ln_b", "w_conv_out", "lb_logits",
             "gnorm_g", "w_rec_out", "w_out", "final_g")
    allres = {**big, **small_out}
    outs = [loss, grad_x]
    for field in range(4):
        outs.extend(allres[nm][field] for nm in order)
    return tuple(outs)
```

```python
import numpy as np

import jax
import jax.numpy as jnp
from jax import lax
from jax.experimental import pallas as pl
from jax.experimental.pallas import tpu as pltpu

F32 = jnp.float32
BF16 = jnp.bfloat16

EPS = 1e-6
CHUNK = 64
N_LEVELS = 6
FIRST_TABLE_LEVEL = 6
CONV_WIDTH = 31
HALO = 32
CONV_ROWS = 64
CONV_LANES = 128
SUBLANES = 8
HEAD = 128
W_IN_COL_TILES = 1
HEADS_PER_TRIP = 8
N_CHIPS = 4
VMEM_LIMIT_BYTES = 56 * 1024 * 1024
ELEMENTWISE_BLOCK_BYTES = 3 * 1024 * 1024

ADAM_LR = 0.001
ADAM_B1 = 0.9
ADAM_B2 = 0.999
ADAM_EPS = 1e-08
ADAM_WD = 0.01
ADAM_STEP = 10

MESH = pl.DeviceIdType.MESH
ANY = pl.BlockSpec(memory_space=pl.ANY)

NT = (((1,), (1,)), ((), ()))
TN = (((0,), (0,)), ((), ()))


def _params(**kw):
    return pltpu.CompilerParams(vmem_limit_bytes=VMEM_LIMIT_BYTES, **kw)


def _sigmoid(x):
    return jax.nn.sigmoid(x)


def _dsilu(x, s):
    return s * (1.0 + x * (1.0 - s))


def _row_tile(lp):
    for t in (320, 256, 192, 128, 64):
        if lp % t == 0:
            return t
    raise ValueError(f"unsupported padded length {lp}")


def _mm_row_tile(lp):
    for t in (832, 640, 320, 256, 192, 128, 64):
        if lp % t == 0:
            return t
    raise ValueError(f"unsupported padded length {lp}")


def _dot3(m_bf16, x):
    hi = x.astype(BF16)
    r1 = x - hi.astype(F32)
    mid = r1.astype(BF16)
    lo = (r1 - mid.astype(F32)).astype(BF16)
    return (jnp.dot(m_bf16, hi, preferred_element_type=F32)
            + jnp.dot(m_bf16, mid, preferred_element_type=F32)
            + jnp.dot(m_bf16, lo, preferred_element_type=F32))


def _dot2(m_bf16, x):
    hi = x.astype(BF16)
    lo = (x - hi.astype(F32)).astype(BF16)
    return (jnp.dot(m_bf16, hi, preferred_element_type=F32)
            + jnp.dot(m_bf16, lo, preferred_element_type=F32))


def _col_to_row(col):
    return jnp.broadcast_to(col, (HEAD, SUBLANES)).T[0:1, :]


def _row_to_col(row):
    return jnp.broadcast_to(row, (SUBLANES, HEAD)).T[:, 0:1]


def _hgrn_tables():
    t = np.arange(CHUNK)
    ltri = (t[None, :] <= t[:, None]).astype(np.float32)
    mats = [ltri]
    for lvl in range(FIRST_TABLE_LEVEL, N_LEVELS + 1):
        blk = CHUNK >> (lvl - 1)
        mid = (t // blk) * blk + blk // 2
        mats.append(ltri[mid - 1])
    after = (t[None, :] >= t[:, None]).astype(np.float32)
    before = (t[None, :] < t[:, None]).astype(np.float32)
    return jnp.asarray(np.concatenate(mats, 0), BF16), jnp.asarray(np.concatenate([after, before], 1), BF16)


def _rmsnorm_fwd(hres, g):
    lp, d = hres.shape
    tm = _mm_row_tile(lp)

    def body(x_ref, g_ref, h_ref):
        x = x_ref[...]
        r = lax.rsqrt(jnp.mean(x * x, axis=-1, keepdims=True) + EPS)
        h_ref[...] = (x * r * g_ref[...]).astype(BF16)

    return pl.pallas_call(
        body, grid=(lp // tm,),
        in_specs=[pl.BlockSpec((tm, d), lambda i: (i, 0)), pl.BlockSpec((1, d), lambda i: (0, 0))],
        out_specs=pl.BlockSpec((tm, d), lambda i: (i, 0)),
        out_shape=jax.ShapeDtypeStruct((lp, d), BF16),
        name="rmsnorm_fwd", compiler_params=_params())(hres, g)


def _conv_fwd(proj, conv_w, conv_b, ln_g, ln_b, w_conv):
    lp = proj.shape[0]
    d = conv_b.shape[1]
    tm = _row_tile(lp)
    hb = tm // HALO

    def body(ua_ref, ub_ref, z_ref, uap_ref, ubp_ref, cw_ref, cb_ref, lg_ref, lb_ref, w_ref,
             c_ref, ycin_ref, yconv_ref, aext_ref):
        i = pl.program_id(0)
        a_prev = uap_ref[...] * _sigmoid(ubp_ref[...])
        aext_ref[0:HALO, :] = jnp.where(i > 0, a_prev, 0.0)
        aext_ref[HALO:HALO + tm, :] = ua_ref[...] * _sigmoid(ub_ref[...])

        def row_block(r, carry):
            r0 = pl.multiple_of(r * CONV_ROWS, CONV_ROWS)
            for cs in range(d // CONV_LANES):
                cl = slice(cs * CONV_LANES, (cs + 1) * CONV_LANES)
                blk = aext_ref[pl.ds(r0, CONV_ROWS + HALO), cl]
                acc = jnp.zeros((CONV_ROWS, CONV_LANES), F32) + cb_ref[:, cl]
                for b in range(SUBLANES):
                    sh = blk if b == 0 else pltpu.roll(blk, CONV_ROWS + HALO - b, axis=0)
                    for a in range(5):
                        j = SUBLANES * a + b - 2
                        if 0 <= j < CONV_WIDTH:
                            acc = acc + cw_ref[j:j + 1, cl] * sh[SUBLANES * a:SUBLANES * a + CONV_ROWS, :]
                c_ref[pl.ds(r0, CONV_ROWS), cl] = acc
            return carry

        lax.fori_loop(0, tm // CONV_ROWS, row_block, 0)

        c = c_ref[...]
        mu = jnp.mean(c, axis=-1, keepdims=True)
        xc = c - mu
        rstd = lax.rsqrt(jnp.mean(xc * xc, axis=-1, keepdims=True) + EPS)
        ln = xc * rstd * lg_ref[...] + lb_ref[...]
        s = ln * _sigmoid(ln)
        z = z_ref[...]
        ycin = (s * (z * _sigmoid(z))).astype(BF16)
        ycin_ref[...] = ycin
        yconv_ref[...] = jnp.dot(ycin, w_ref[...], preferred_element_type=F32)

    row = lambda p: pl.BlockSpec((tm, d), lambda i, p=p: (i, p))
    halo = lambda p: pl.BlockSpec((HALO, d), lambda i, p=p: (jnp.maximum(i * hb - 1, 0), p))
    vec = pl.BlockSpec((1, d), lambda i: (0, 0))
    return pl.pallas_call(
        body, grid=(lp // tm,),
        in_specs=[row(0), row(1), row(2), halo(0), halo(1),
                  pl.BlockSpec((HALO, d), lambda i: (0, 0)), vec, vec, vec,
                  pl.BlockSpec((d, d), lambda i: (0, 0))],
        out_specs=[pl.BlockSpec((tm, d), lambda i: (i, 0))] * 3,
        out_shape=[jax.ShapeDtypeStruct((lp, d), F32), jax.ShapeDtypeStruct((lp, d), BF16),
                   jax.ShapeDtypeStruct((lp, d), F32)],
        scratch_shapes=[pltpu.VMEM((HALO + tm, d), F32)],
        name="conv_fwd", compiler_params=_params())(
            proj, proj, proj, proj, proj, conv_w, conv_b, ln_g, ln_b, w_conv)


def _lower_bound(lbl_ref):
    l0 = lbl_ref[0:1, :]
    l1 = lbl_ref[1:2, :]
    m = jnp.maximum(l0, l1)
    e0 = jnp.exp(l0 - m)
    e1 = jnp.exp(l1 - m)
    p0 = e0 / (e0 + e1)
    return p0, p0 * (e1 / (e0 + e1))


def _level_masks():
    r2 = lax.broadcasted_iota(jnp.int32, (CHUNK, CHUNK), 0)
    c2 = lax.broadcasted_iota(jnp.int32, (CHUNK, CHUNK), 1)
    out = []
    for lvl in range(1, N_LEVELS + 1):
        blk = CHUNK >> (lvl - 1)
        sh = blk.bit_length() - 1
        same = (r2 >> sh) == (c2 >> sh)
        t_upper = (r2 & (blk - 1)) >= (blk // 2)
        s_lower = (c2 & (blk - 1)) < (blk // 2)
        out.append(jnp.logical_and(same, jnp.logical_and(t_upper, s_lower)))
    return out


def _gates(qr, fr, lb, valid):
    sq = _sigmoid(qr)
    q = qr * sq
    sf = _sigmoid(fr)
    f = lb + (1.0 - lb) * sf
    g = jnp.where(valid, jnp.log(f), 0.0)
    k = jnp.where(valid, 1.0 - f, 0.0)
    return q, sq, f, sf, g, k


def _level_reference(lvl, b, t_ref, hs):
    if lvl >= FIRST_TABLE_LEVEL:
        base = CHUNK * (lvl - FIRST_TABLE_LEVEL + 1)
        return t_ref[base:base + CHUNK, hs]
    blk = CHUNK >> (lvl - 1)
    rows = [jnp.broadcast_to(b[m + blk // 2 - 1:m + blk // 2, :], (blk, HEAD)) for m in range(0, CHUNK, blk)]
    return rows[0] if len(rows) == 1 else jnp.concatenate(rows, axis=0)


def _level_factor(b, r):
    d = b - r
    return jnp.exp(jnp.minimum(d, -d))


def _hgrn_fwd(proj, lb_logits, n_pad):
    lp = proj.shape[0]
    d = lb_logits.shape[1]
    n_heads = d // HEAD
    nc = lp // CHUNK
    tab, _ = _hgrn_tables()
    n_tab = tab.shape[0]

    def body(qr_ref, fr_ref, ir_ref, lbl_ref, tab_ref, o_ref, sall_ref, s_ref, t_ref):
        n = pl.program_id(0)

        @pl.when(n == 0)
        def _():
            s_ref[...] = jnp.zeros_like(s_ref)

        sall_ref[0] = s_ref[...]
        lb_all, _ = _lower_bound(lbl_ref)
        rid = lax.broadcasted_iota(jnp.int32, (CHUNK, 1), 0)
        valid = jnp.logical_or(n > 0, rid >= n_pad)
        f_all = lb_all + (1.0 - lb_all) * _sigmoid(fr_ref[...])
        t_ref[...] = _dot2(tab_ref[...], jnp.where(valid, jnp.log(f_all), 0.0))
        masks = _level_masks()

        def head(h):
            off = h * HEAD if isinstance(h, int) else pl.multiple_of(h * HEAD, HEAD)
            hs = pl.ds(off, HEAD)
            lb = _lower_bound_slice(lbl_ref, hs)
            q, _, _, _, _, k = _gates(qr_ref[:, hs], fr_ref[:, hs], lb, valid)
            v = ir_ref[:, hs]
            b = t_ref[0:CHUNK, hs]
            s0 = s_ref[hs, :]
            o = jnp.dot((q * jnp.exp(b)).astype(BF16), s0.astype(BF16), preferred_element_type=F32)
            o = o + jnp.sum(q * k, axis=-1, keepdims=True) * v
            a = jnp.zeros((CHUNK, CHUNK), F32)
            for lvl in range(1, N_LEVELS + 1):
                e = _level_factor(b, _level_reference(lvl, b, t_ref, hs))
                p = lax.dot_general((q * e).astype(BF16), (k * e).astype(BF16), NT, preferred_element_type=F32)
                a = a + jnp.where(masks[lvl - 1], p, 0.0)
            vb = v.astype(BF16)
            o_ref[:, hs] = o + jnp.dot(a.astype(BF16), vb, preferred_element_type=F32)
            b_last = t_ref[CHUNK - 1:CHUNK, hs]
            khat = (k * jnp.exp(b_last - b)).astype(BF16)
            s_ref[hs, :] = _row_to_col(jnp.exp(b_last)) * s0 + lax.dot_general(khat, vb, TN, preferred_element_type=F32)
        per_trip = min(HEADS_PER_TRIP, n_heads)

        def head_group(p, carry):
            for u in range(per_trip):
                head(p * per_trip + u)
            return carry

        if n_heads == per_trip:
            head_group(0, 0)
        else:
            lax.fori_loop(0, n_heads // per_trip, head_group, 0)

    piece = lambda p: pl.BlockSpec((CHUNK, d), lambda n, p=p: (n, p))
    return pl.pallas_call(
        body, grid=(nc,),
        in_specs=[piece(3), piece(4), piece(5), pl.BlockSpec((2, d), lambda n: (0, 0)),
                  pl.BlockSpec((n_tab, CHUNK), lambda n: (0, 0))],
        out_specs=[pl.BlockSpec((CHUNK, d), lambda n: (n, 0)), pl.BlockSpec((1, d, HEAD), lambda n: (n, 0, 0))],
        out_shape=[jax.ShapeDtypeStruct((lp, d), F32), jax.ShapeDtypeStruct((nc, d, HEAD), F32)],
        scratch_shapes=[pltpu.VMEM((d, HEAD), F32), pltpu.VMEM((n_tab, d), F32)],
        name="hgrn_fwd", compiler_params=_params())(proj, proj, proj, lb_logits, tab)


def _lower_bound_slice(lbl_ref, hs):
    l0 = lbl_ref[0:1, hs]
    l1 = lbl_ref[1:2, hs]
    m = jnp.maximum(l0, l1)
    e0 = jnp.exp(l0 - m)
    e1 = jnp.exp(l1 - m)
    return e0 / (e0 + e1)


def _tail_fwd(o, proj, y_conv, hres, target, gnorm_g, final_g, w_rec, w_out):
    lp, d = o.shape
    n_heads = d // HEAD
    tm = _row_tile(lp)

    n_slabs = tm // CHUNK

    def body(o_ref, gr_ref, mc_ref, mr_ref, yc_ref, x_ref, gn_ref, fg_ref, wr_ref, wo_ref, *rest):
        t_refs = rest[:n_slabs]
        yrin_ref, gwo_ref, yrec_ref, dout_ref, loss_ref, dfg_ref = rest[n_slabs:]
        i = pl.program_id(0)

        @pl.when(i == 0)
        def _():
            loss_ref[...] = jnp.zeros_like(loss_ref)
            dfg_ref[...] = jnp.zeros_like(dfg_ref)
            gwo_ref[...] = jnp.zeros_like(gwo_ref)

        for h in range(n_heads):
            hs = slice(h * HEAD, (h + 1) * HEAD)
            oh = o_ref[:, hs]
            on = oh * lax.rsqrt(jnp.mean(oh * oh, axis=-1, keepdims=True) + EPS) * gn_ref[:, hs]
            gr = gr_ref[:, hs]
            yrin_ref[:, hs] = (on * (gr * _sigmoid(gr))).astype(BF16)
        yrec = jnp.dot(yrin_ref[...], wr_ref[...], preferred_element_type=F32)
        yrec_ref[...] = yrec
        merged = (_sigmoid(mc_ref[...]) * yc_ref[...] + _sigmoid(mr_ref[...]) * yrec).astype(BF16)
        out = x_ref[...] + jnp.dot(merged, wo_ref[...], preferred_element_type=F32)
        r = lax.rsqrt(jnp.mean(out * out, axis=-1, keepdims=True) + EPS)
        yhat = out * r
        fg = fg_ref[...]
        rid = lax.broadcasted_iota(jnp.int32, (tm, 1), 0) + i * tm
        tgt = jnp.concatenate([t[...] for t in t_refs], axis=0)
        err = jnp.where(rid >= CHUNK, yhat * fg - tgt, 0.0)
        loss_ref[...] += 0.5 * jnp.sum(err * err) / d
        dy = err / d
        dfg_ref[...] += jnp.sum(dy * yhat, axis=0, keepdims=True)
        dyh = dy * fg
        dout = r * (dyh - yhat * jnp.mean(dyh * yhat, axis=-1, keepdims=True))
        dout_ref[...] = dout
        gwo_ref[...] += lax.dot_general(merged, dout.astype(BF16), TN, preferred_element_type=F32)

    row = lambda p: pl.BlockSpec((tm, d), lambda i, p=p: (i, p))
    vec = pl.BlockSpec((1, d), lambda i: (0, 0))
    mat = pl.BlockSpec((d, d), lambda i: (0, 0))
    return pl.pallas_call(
        body, grid=(lp // tm,),
        in_specs=[row(0), row(6), row(7), row(8), row(0), row(0), vec, vec, mat, mat]
        + [pl.BlockSpec((CHUNK, d), lambda i, u=u: (jnp.maximum(i * n_slabs + u - 1, 0), 0)) for u in range(n_slabs)],
        out_specs=[row(0), mat, row(0), row(0), pl.BlockSpec((8, 128), lambda i: (0, 0)), vec],
        out_shape=[jax.ShapeDtypeStruct((lp, d), BF16), jax.ShapeDtypeStruct((d, d), F32),
                   jax.ShapeDtypeStruct((lp, d), F32), jax.ShapeDtypeStruct((lp, d), F32),
                   jax.ShapeDtypeStruct((8, 128), F32), jax.ShapeDtypeStruct((1, d), F32)],
        name="tail_fwd", compiler_params=_params())(
            o, proj, proj, proj, y_conv, hres, gnorm_g, final_g, w_rec, w_out, *([target] * n_slabs))


def _tail_bwd(dout, proj, y_conv, y_rec, o, c, w_out, w_rec, w_conv, ln_g, ln_b, gnorm_g):
    lp, d = dout.shape
    n_heads = d // HEAD
    tm = _row_tile(lp)

    def body(dout_ref, mc_ref, mr_ref, z_ref, gr_ref, yc_ref, yrec_ref, o_ref, c_ref,
             wo_ref, wr_ref, wc_ref, lg_ref, lb_ref, gn_ref,
             dyc_ref, dyr_ref, doutb_ref, dz_ref, dp_ref, do_ref, dc_ref,
             dgn_ref, dlg_ref, dlb_ref, dyrin_ref):
        i = pl.program_id(0)

        @pl.when(i == 0)
        def _():
            dgn_ref[...] = jnp.zeros_like(dgn_ref)
            dlg_ref[...] = jnp.zeros_like(dlg_ref)
            dlb_ref[...] = jnp.zeros_like(dlb_ref)

        doutb = dout_ref[...].astype(BF16)
        doutb_ref[...] = doutb
        dmerged = lax.dot_general(doutb, wo_ref[...], NT, preferred_element_type=F32)
        smc = _sigmoid(mc_ref[...])
        smr = _sigmoid(mr_ref[...])
        dyc = (dmerged * smc).astype(BF16)
        dyr = (dmerged * smr).astype(BF16)
        dyc_ref[...] = dyc
        dyr_ref[...] = dyr
        dp_ref[:, d:2 * d] = (dmerged * yc_ref[...] * smc * (1.0 - smc)).astype(BF16)
        dp_ref[:, 2 * d:3 * d] = (dmerged * yrec_ref[...] * smr * (1.0 - smr)).astype(BF16)

        dyrin_ref[...] = lax.dot_general(dyr, wr_ref[...], NT, preferred_element_type=F32)
        for h in range(n_heads):
            hs = slice(h * HEAD, (h + 1) * HEAD)
            oh = o_ref[:, hs]
            rstd = lax.rsqrt(jnp.mean(oh * oh, axis=-1, keepdims=True) + EPS)
            ohat = oh * rstd
            gn = gn_ref[:, hs]
            gr = gr_ref[:, hs]
            sg = _sigmoid(gr)
            dyrin = dyrin_ref[:, hs]
            don = dyrin * (gr * sg)
            dp_ref[:, hs] = (dyrin * (ohat * gn) * _dsilu(gr, sg)).astype(BF16)
            dgn_ref[:, hs] += jnp.sum(don * ohat, axis=0, keepdims=True)
            doh = don * gn
            do_ref[:, hs] = rstd * (doh - ohat * jnp.mean(doh * ohat, axis=-1, keepdims=True))

        dycin = lax.dot_general(dyc, wc_ref[...], NT, preferred_element_type=F32)
        c = c_ref[...]
        mu = jnp.mean(c, axis=-1, keepdims=True)
        xc = c - mu
        rstd = lax.rsqrt(jnp.mean(xc * xc, axis=-1, keepdims=True) + EPS)
        nrm = xc * rstd
        lg = lg_ref[...]
        ln = nrm * lg + lb_ref[...]
        sl = _sigmoid(ln)
        z = z_ref[...]
        sz = _sigmoid(z)
        dz_ref[...] = (dycin * (ln * sl) * _dsilu(z, sz)).astype(BF16)
        dln = dycin * (z * sz) * _dsilu(ln, sl)
        dlg_ref[...] += jnp.sum(dln * nrm, axis=0, keepdims=True)
        dlb_ref[...] += jnp.sum(dln, axis=0, keepdims=True)
        dn = dln * lg
        dc_ref[...] = rstd * (dn - jnp.mean(dn, axis=-1, keepdims=True)
                              - nrm * jnp.mean(dn * nrm, axis=-1, keepdims=True))

    row = lambda p: pl.BlockSpec((tm, d), lambda i, p=p: (i, p))
    vec = pl.BlockSpec((1, d), lambda i: (0, 0))
    mat = pl.BlockSpec((d, d), lambda i: (0, 0))
    act_bf = jax.ShapeDtypeStruct((lp, d), BF16)
    act_f32 = jax.ShapeDtypeStruct((lp, d), F32)
    vec_f32 = jax.ShapeDtypeStruct((1, d), F32)
    return pl.pallas_call(
        body, grid=(lp // tm,),
        in_specs=[row(0), row(7), row(8), row(2), row(6), row(0), row(0), row(0), row(0),
                  mat, mat, mat, vec, vec, vec],
        out_specs=[row(0)] * 4 + [pl.BlockSpec((tm, 3 * d), lambda i: (i, 2))] + [row(0)] * 2 + [vec] * 3,
        out_shape=[act_bf] * 4 + [jax.ShapeDtypeStruct((lp, 9 * d), BF16)] + [act_f32] * 2 + [vec_f32] * 3,
        scratch_shapes=[pltpu.VMEM((tm, d), F32)],
        name="tail_bwd", compiler_params=_params())(
            dout, proj, proj, proj, proj, y_conv, y_rec, o, c, w_out, w_rec, w_conv, ln_g, ln_b, gnorm_g)


def _hgrn_bwd(proj, do, s_all, lb_logits, n_pad, dproj):
    lp, d = do.shape
    n_heads = d // HEAD
    nc = lp // CHUNK
    tab, utri = _hgrn_tables()
    n_tab = tab.shape[0]

    def body(qr_ref, fr_ref, ir_ref, do_ref, s0_ref, lbl_ref, tab_ref, ut_ref, _,
             dp_ref, dlbl_ref, ds_ref, t_ref, dlb_ref):
        n = pl.program_id(0)
        chunk = nc - 1 - n

        @pl.when(n == 0)
        def _():
            ds_ref[...] = jnp.zeros_like(ds_ref)
            dlb_ref[...] = jnp.zeros_like(dlb_ref)

        lb_all, pp = _lower_bound(lbl_ref)
        rid = lax.broadcasted_iota(jnp.int32, (CHUNK, 1), 0)
        valid = jnp.logical_or(chunk > 0, rid >= n_pad)
        f_all = lb_all + (1.0 - lb_all) * _sigmoid(fr_ref[...])
        t_ref[...] = _dot2(tab_ref[...], jnp.where(valid, jnp.log(f_all), 0.0))
        masks = _level_masks()
        ut = ut_ref[...]

        def head(h):
            off = h * HEAD if isinstance(h, int) else pl.multiple_of(h * HEAD, HEAD)
            hs = pl.ds(off, HEAD)
            lb = _lower_bound_slice(lbl_ref, hs)
            qr = qr_ref[:, hs]
            q, sq, f, sf, _, k = _gates(qr, fr_ref[:, hs], lb, valid)
            v = ir_ref[:, hs]
            do_h = do_ref[:, hs]
            b = t_ref[0:CHUNK, hs]
            b_last = t_ref[CHUNK - 1:CHUNK, hs]
            s0 = s0_ref[0, hs, :]
            ds1 = ds_ref[hs, :]
            eb = jnp.exp(b)
            ekl = jnp.exp(b_last - b)
            do_bf = do_h.astype(BF16)
            v_bf = v.astype(BF16)
            ds1_bf = ds1.astype(BF16)

            da = lax.dot_general(do_bf, v_bf, NT, preferred_element_type=F32)
            da_diag = jnp.sum(do_h * v, axis=-1, keepdims=True)
            a = jnp.zeros((CHUNK, CHUNK), F32)
            dq_x = eb * lax.dot_general(do_bf, s0.astype(BF16), NT, preferred_element_type=F32)
            dk_x = ekl * lax.dot_general(v_bf, ds1_bf, NT, preferred_element_type=F32)
            x_after = q * dq_x
            x_before = k * dk_x
            for lvl in range(1, N_LEVELS + 1):
                e = _level_factor(b, _level_reference(lvl, b, t_ref, hs))
                qt = (q * e).astype(BF16)
                kt = (k * e).astype(BF16)
                p = lax.dot_general(qt, kt, NT, preferred_element_type=F32)
                a = a + jnp.where(masks[lvl - 1], p, 0.0)
                dam = jnp.where(masks[lvl - 1], da, 0.0).astype(BF16)
                dqt = jnp.dot(dam, kt, preferred_element_type=F32)
                dkt = lax.dot_general(dam, qt, TN, preferred_element_type=F32)
                dq_x = dq_x + e * dqt
                dk_x = dk_x + e * dkt
                x_after = x_after + (qt.astype(F32) * dqt - kt.astype(F32) * dkt)

            dv = (lax.dot_general(a.astype(BF16), do_bf, TN, preferred_element_type=F32)
                  + jnp.sum(q * k, axis=-1, keepdims=True) * do_h
                  + jnp.dot((k * ekl).astype(BF16), ds1_bf, preferred_element_type=F32))
            dp_ref[:, pl.ds(2 * d + off, HEAD)] = dv.astype(BF16)

            carried = jnp.exp(b_last) * _col_to_row(jnp.sum(s0 * ds1, axis=-1, keepdims=True))
            dg = _dot3(ut, jnp.concatenate([x_after, x_before], axis=0)) + carried
            dq = dq_x + da_diag * k
            dk = dk_x + da_diag * q
            dp_ref[:, hs] = (dq * _dsilu(qr, sq)).astype(BF16)
            df = jnp.where(valid, dg / f - dk, 0.0)
            dp_ref[:, pl.ds(d + off, HEAD)] = (df * (1.0 - lb) * sf * (1.0 - sf)).astype(BF16)
            dlb_ref[:, hs] += jnp.sum(df * (1.0 - sf), axis=0, keepdims=True)

            ds_ref[hs, :] = (_row_to_col(jnp.exp(b_last)) * ds1
                             + lax.dot_general((q * eb).astype(BF16), do_bf, TN, preferred_element_type=F32))
        per_trip = min(HEADS_PER_TRIP, n_heads)

        def head_group(p, carry):
            for u in range(per_trip):
                head(p * per_trip + u)
            return carry

        if n_heads == per_trip:
            head_group(0, 0)
        else:
            lax.fori_loop(0, n_heads // per_trip, head_group, 0)

        @pl.when(n == nc - 1)
        def _():
            dl0 = dlb_ref[...] * pp
            dlbl_ref[0:1, :] = dl0
            dlbl_ref[1:2, :] = -dl0

    piece = lambda p: pl.BlockSpec((CHUNK, d), lambda n, p=p: (nc - 1 - n, p))
    return pl.pallas_call(
        body, grid=(nc,),
        in_specs=[piece(3), piece(4), piece(5), piece(0),
                  pl.BlockSpec((1, d, HEAD), lambda n: (nc - 1 - n, 0, 0)),
                  pl.BlockSpec((2, d), lambda n: (0, 0)),
                  pl.BlockSpec((n_tab, CHUNK), lambda n: (0, 0)),
                  pl.BlockSpec((CHUNK, 2 * CHUNK), lambda n: (0, 0)), ANY],
        out_specs=[pl.BlockSpec((CHUNK, 3 * d), lambda n: (nc - 1 - n, 1)), pl.BlockSpec((2, d), lambda n: (0, 0))],
        out_shape=[jax.ShapeDtypeStruct(dproj.shape, BF16), jax.ShapeDtypeStruct((2, d), F32)],
        input_output_aliases={8: 0},
        scratch_shapes=[pltpu.VMEM((d, HEAD), F32), pltpu.VMEM((n_tab, d), F32), pltpu.VMEM((1, d), F32)],
        name="hgrn_bwd", compiler_params=_params())(proj, proj, proj, do, s_all, lb_logits, tab, utri, dproj)


def _conv_bwd(dc, proj, conv_w, dz, dproj, yc_in, dyc):
    lp, d = dc.shape
    tm = _row_tile(lp)
    hb = tm // HALO
    n_tiles = lp // tm
    last_halo = lp // HALO - 1

    def body(dc_ref, dcn_ref, ua_ref, ub_ref, uap_ref, ubp_ref, cw_ref, dz_ref, _, ycin_ref, dyc_ref,
             dp_ref, dcw_ref, dcb_ref, gwc_ref, aext_ref, dcext_ref, da_ref, dcw_acc):
        i = pl.program_id(0)

        @pl.when(i == 0)
        def _():
            dcw_acc[...] = jnp.zeros_like(dcw_acc)
            dcb_ref[...] = jnp.zeros_like(dcb_ref)
            gwc_ref[...] = jnp.zeros_like(gwc_ref)

        gwc_ref[...] += lax.dot_general(ycin_ref[...], dyc_ref[...], TN, preferred_element_type=F32)

        ua = ua_ref[...]
        sb = _sigmoid(ub_ref[...])
        a_prev = uap_ref[...] * _sigmoid(ubp_ref[...])
        aext_ref[0:HALO, :] = jnp.where(i > 0, a_prev, 0.0)
        aext_ref[HALO:HALO + tm, :] = ua * sb
        dcext_ref[0:tm, :] = dc_ref[...]
        dcext_ref[tm:tm + HALO, :] = jnp.where(i < n_tiles - 1, dcn_ref[...], 0.0)
        dcb_ref[...] += jnp.sum(dc_ref[...], axis=0, keepdims=True)

        def row_block(r, carry):
            r0 = pl.multiple_of(r * CONV_ROWS, CONV_ROWS)
            n_rows = CONV_ROWS + HALO
            for cs in range(d // CONV_LANES):
                cl = slice(cs * CONV_LANES, (cs + 1) * CONV_LANES)
                dblk = dcext_ref[pl.ds(r0, n_rows), cl]
                ablk = aext_ref[pl.ds(r0, n_rows), cl]
                dcur = dblk[0:CONV_ROWS, :]
                acc = jnp.zeros((CONV_ROWS, CONV_LANES), F32)
                for b in range(SUBLANES):
                    dsh = dblk if b == 0 else pltpu.roll(dblk, n_rows - b, axis=0)
                    ash = ablk if b == 0 else pltpu.roll(ablk, n_rows - b, axis=0)
                    for a in range(5):
                        j_da = CONV_WIDTH - 1 - (SUBLANES * a + b)
                        if 0 <= j_da < CONV_WIDTH:
                            acc = acc + cw_ref[j_da:j_da + 1, cl] * dsh[SUBLANES * a:SUBLANES * a + CONV_ROWS, :]
                        j_w = SUBLANES * a + b - 2
                        if 0 <= j_w < CONV_WIDTH:
                            prod = dcur * ash[SUBLANES * a:SUBLANES * a + CONV_ROWS, :]
                            dcw_acc[j_w, :, cl] += prod.reshape(CONV_ROWS // SUBLANES, SUBLANES, CONV_LANES).sum(axis=0)
                da_ref[pl.ds(r0, CONV_ROWS), cl] = acc
            return carry

        lax.fori_loop(0, tm // CONV_ROWS, row_block, 0)

        da = da_ref[...]
        dp_ref[:, 0:d] = (da * sb).astype(BF16)
        dp_ref[:, d:2 * d] = (da * ua * sb * (1.0 - sb)).astype(BF16)
        dp_ref[:, 2 * d:3 * d] = dz_ref[...]

        @pl.when(i == n_tiles - 1)
        def _():
            dcw_ref[...] = jnp.sum(dcw_acc[...], axis=1)

    row = lambda p: pl.BlockSpec((tm, d), lambda i, p=p: (i, p))
    prev = lambda p: pl.BlockSpec((HALO, d), lambda i, p=p: (jnp.maximum(i * hb - 1, 0), p))
    nxt = pl.BlockSpec((HALO, d), lambda i: (jnp.minimum((i + 1) * hb, last_halo), 0))
    return pl.pallas_call(
        body, grid=(n_tiles,),
        in_specs=[row(0), nxt, row(0), row(1), prev(0), prev(1), pl.BlockSpec((HALO, d), lambda i: (0, 0)),
                  row(0), ANY, row(0), row(0)],
        out_specs=[pl.BlockSpec((tm, 3 * d), lambda i: (i, 0)), pl.BlockSpec((HALO, d), lambda i: (0, 0)),
                   pl.BlockSpec((1, d), lambda i: (0, 0)), pl.BlockSpec((d, d), lambda i: (0, 0))],
        out_shape=[jax.ShapeDtypeStruct(dproj.shape, BF16),
                   jax.ShapeDtypeStruct((HALO, d), F32), jax.ShapeDtypeStruct((1, d), F32),
                   jax.ShapeDtypeStruct((d, d), F32)],
        input_output_aliases={8: 0},
        scratch_shapes=[pltpu.VMEM((HALO + tm, d), F32), pltpu.VMEM((tm + HALO, d), F32), pltpu.VMEM((tm, d), F32),
                        pltpu.VMEM((HALO, SUBLANES, d), F32)],
        name="conv_bwd", compiler_params=_params())(dc, dc, proj, proj, proj, proj, conv_w, dz, dproj, yc_in, dyc)


def _weight_grad(xs, dy, name, blocked):
    lp, dx = xs.shape
    n = dy.shape[1]
    tk = _mm_row_tile(lp)
    if blocked:
        ncol = n // N_CHIPS
        nt = W_IN_COL_TILES
        tn = ncol // nt
        grid = (N_CHIPS * nt, lp // tk)
        out_spec = pl.BlockSpec((1, dx, tn), lambda c, k: (c // nt, 0, c % nt))
        out_shape = jax.ShapeDtypeStruct((N_CHIPS, dx, ncol), F32)
    else:
        tn = n
        grid = (1, lp // tk)
        out_spec = pl.BlockSpec((dx, tn), lambda c, k: (0, c))
        out_shape = jax.ShapeDtypeStruct((dx, n), F32)

    def body(xs_ref, dy_ref, o_ref, *copy_ref):
        @pl.when(pl.program_id(1) == 0)
        def _():
            o_ref[...] = jnp.zeros_like(o_ref)

        p = lax.dot_general(xs_ref[...], dy_ref[...], TN, preferred_element_type=F32)
        if blocked:
            o_ref[0] += p

            @pl.when(pl.program_id(1) == lp // tk - 1)
            def _():
                copy_ref[0][0] = o_ref[0].astype(BF16)
        else:
            o_ref[...] += p

    if blocked:
        out_spec = [out_spec, out_spec]
        out_shape = [out_shape, jax.ShapeDtypeStruct(out_shape.shape, BF16)]
    return pl.pallas_call(
        body, grid=grid,
        in_specs=[pl.BlockSpec((tk, dx), lambda c, k: (k, 0)), pl.BlockSpec((tk, tn), lambda c, k: (k, c))],
        out_specs=out_spec, out_shape=out_shape,
        name=name, compiler_params=_params())(xs, dy)


def _in_proj_bwd(dproj, wtg, hres, norm_g, dout):
    lp, d = hres.shape
    _, ncol, _ = wtg.shape
    tm = _mm_row_tile(lp)
    nt = W_IN_COL_TILES
    tn = ncol // nt
    nk = N_CHIPS * nt

    def body(dp_ref, w_ref, x_ref, g_ref, dout_ref, dx_ref, dg_ref, acc_ref):
        i = pl.program_id(0)
        kk = pl.program_id(1)

        @pl.when(jnp.logical_and(i == 0, kk == 0))
        def _():
            dg_ref[...] = jnp.zeros_like(dg_ref)

        @pl.when(kk == 0)
        def _():
            acc_ref[...] = jnp.zeros_like(acc_ref)

        acc_ref[...] += jnp.dot(dp_ref[...], w_ref[0], preferred_element_type=F32)

        @pl.when(kk == nk - 1)
        def _():
            x = x_ref[...]
            r = lax.rsqrt(jnp.mean(x * x, axis=-1, keepdims=True) + EPS)
            xhat = x * r
            dh = acc_ref[...]
            dg_ref[...] += jnp.sum(dh * xhat, axis=0, keepdims=True)
            dxh = dh * g_ref[...]
            dx_ref[...] = dout_ref[...] + r * (dxh - xhat * jnp.mean(dxh * xhat, axis=-1, keepdims=True))

    return pl.pallas_call(
        body, grid=(lp // tm, nk),
        in_specs=[pl.BlockSpec((tm, tn), lambda i, k: (i, k)),
                  pl.BlockSpec((1, tn, d), lambda i, k: (k // nt, k % nt, 0)),
                  pl.BlockSpec((tm, d), lambda i, k: (i, 0)),
                  pl.BlockSpec((1, d), lambda i, k: (0, 0)),
                  pl.BlockSpec((tm, d), lambda i, k: (i, 0))],
        out_specs=[pl.BlockSpec((tm, d), lambda i, k: (i, 0)), pl.BlockSpec((1, d), lambda i, k: (0, 0))],
        out_shape=[jax.ShapeDtypeStruct((lp, d), F32), jax.ShapeDtypeStruct((1, d), F32)],
        scratch_shapes=[pltpu.VMEM((tm, d), F32)],
        name="in_proj_bwd", compiler_params=_params())(dproj, wtg, hres, norm_g, dout)


def _adamw_math(w, g, m, v):
    m = ADAM_B1 * m + (1.0 - ADAM_B1) * g
    v = ADAM_B2 * v + (1.0 - ADAM_B2) * (g * g)
    m_hat = m / (1.0 - ADAM_B1 ** ADAM_STEP)
    v_hat = v / (1.0 - ADAM_B2 ** ADAM_STEP)
    delta = -ADAM_LR * (m_hat / (jnp.sqrt(v_hat) + ADAM_EPS) + ADAM_WD * w)
    return delta, m, v


def _elementwise_rows(shape):
    r, c = shape
    for t in (256, 128, 64, 32, 16, 8):
        if r % t == 0 and r > t and t * c * 4 <= ELEMENTWISE_BLOCK_BYTES:
            return t
    return r


def _adamw(name, w, m, v, *g_parts):
    shape = w.shape
    tr = _elementwise_rows(shape)
    n_g = len(g_parts)

    def body(*refs):
        w_ref, m_ref, v_ref = refs[:3]
        g_refs = refs[3:3 + n_g]
        g_out, d_out, m_out, v_out = refs[3 + n_g:]
        g = g_refs[0][...]
        for gr in g_refs[1:]:
            g = g + gr[...]
        delta, m_new, v_new = _adamw_math(w_ref[...], g, m_ref[...], v_ref[...])
        g_out[...] = g
        d_out[...] = delta
        m_out[...] = m_new
        v_out[...] = v_new

    spec = pl.BlockSpec((tr, shape[1]), lambda i: (i, 0))
    return pl.pallas_call(
        body, grid=(shape[0] // tr,),
        in_specs=[spec] * (3 + n_g), out_specs=[spec] * 4,
        out_shape=[jax.ShapeDtypeStruct(shape, F32)] * 4,
        name=name, compiler_params=_params())(w, m, v, *g_parts)


def _chip_half_sum(name, g, recv, core):
    _, _, hr, cols = g.shape
    tr = _elementwise_rows((hr, cols))

    def body(core_ref, g_ref, r_ref, o_ref, ob_ref):
        s = g_ref[0, 0] + r_ref[0].astype(F32)
        o_ref[0] = s
        ob_ref[0] = s.astype(BF16)

    blk = pl.BlockSpec((1, tr, cols), lambda j, i, core_ref: (j, i, 0))
    grid_spec = pltpu.PrefetchScalarGridSpec(
        num_scalar_prefetch=1, grid=(N_CHIPS, hr // tr),
        in_specs=[pl.BlockSpec((1, 1, tr, cols), lambda j, i, core_ref: (j, core_ref[0], i, 0)), blk],
        out_specs=[blk, blk])
    return pl.pallas_call(
        body, grid_spec=grid_spec,
        out_shape=[jax.ShapeDtypeStruct((N_CHIPS, hr, cols), F32), jax.ShapeDtypeStruct((N_CHIPS, hr, cols), BF16)],
        name=name, compiler_params=_params())(core, g, recv)


def _block_half_total(name, chip_sums, recv, chip_core, after):
    _, hr, cols = chip_sums.shape
    tr = _elementwise_rows((hr, cols))

    def body(cc_ref, p_ref, r_ref, after_ref, o_ref):
        s = p_ref[0]
        for k in range(3):
            s = s + r_ref[k].astype(F32)
        o_ref[0] = s

    grid_spec = pltpu.PrefetchScalarGridSpec(
        num_scalar_prefetch=1, grid=(hr // tr,),
        in_specs=[pl.BlockSpec((1, tr, cols), lambda i, cc_ref: (cc_ref[0], i, 0)),
                  pl.BlockSpec((3, tr, cols), lambda i, cc_ref: (0, i, 0)), ANY],
        out_specs=pl.BlockSpec((1, tr, cols), lambda i, cc_ref: (cc_ref[1], i, 0)))
    return pl.pallas_call(
        body, grid_spec=grid_spec, out_shape=jax.ShapeDtypeStruct((2, hr, cols), F32),
        name=name, compiler_params=_params())(chip_core, chip_sums, recv, after)


def _place_shard(name, w, chip, dtype):
    r, c = w.shape
    tr = _elementwise_rows((r, c))

    def body(chip_ref, w_ref, o_ref):
        o_ref[0] = w_ref[...].astype(dtype)

    grid_spec = pltpu.PrefetchScalarGridSpec(
        num_scalar_prefetch=1, grid=(r // tr,),
        in_specs=[pl.BlockSpec((tr, c), lambda i, chip_ref: (i, 0))],
        out_specs=pl.BlockSpec((1, tr, c), lambda i, chip_ref: (chip_ref[0], i, 0)))
    return pl.pallas_call(
        body, grid_spec=grid_spec, out_shape=jax.ShapeDtypeStruct((N_CHIPS, r, c), dtype),
        name=name, compiler_params=_params())(chip, w)


def _sum_slots(name, slots, own, my_idx):
    k, r, c = slots.shape

    def body(idx_ref, s_ref, own_ref, o_ref):
        s = None
        for j in range(k):
            term = jnp.where(idx_ref[0] == j, own_ref[...], s_ref[j])
            s = term if s is None else s + term
        o_ref[...] = s

    grid_spec = pltpu.PrefetchScalarGridSpec(
        num_scalar_prefetch=1, grid=(1,),
        in_specs=[pl.BlockSpec((k, r, c), lambda i, idx_ref: (0, 0, 0)),
                  pl.BlockSpec((r, c), lambda i, idx_ref: (0, 0))],
        out_specs=pl.BlockSpec((r, c), lambda i, idx_ref: (0, 0)))
    return pl.pallas_call(body, grid_spec=grid_spec, out_shape=jax.ShapeDtypeStruct((r, c), F32), name=name,
                          compiler_params=_params())(my_idx, slots, own)


def _mesh_pos():
    return lax.axis_index("x"), lax.axis_index("y"), lax.axis_index("c")


def _other_chips(x, y):
    return [(1 - x, y), (x, 1 - y), (1 - x, 1 - y)]


def _gather_weights(bufs):
    n = len(bufs)
    half = [b.shape[1] // 2 for b in bufs]

    def body(*refs):
        gathered = refs[n:2 * n]
        ici_send, ici_recv, d2d_send, d2d_recv = refs[2 * n:]
        x, y, c = _mesh_pos()
        me = 2 * x + y
        chips = _other_chips(x, y)

        def part(a, block, core):
            return gathered[a].at[block, pl.ds(core * half[a], half[a])]

        def over_ici(a, k, block):
            px, py = chips[k]
            return pltpu.make_async_remote_copy(
                src_ref=part(a, block, c), dst_ref=part(a, block, c),
                send_sem=ici_send.at[a, k], recv_sem=ici_recv.at[a, k],
                device_id=(px, py, c), device_id_type=MESH)

        def over_d2d(a, k, core):
            px, py = chips[k]
            return pltpu.make_async_remote_copy(
                src_ref=part(a, 2 * px + py, core), dst_ref=part(a, 2 * px + py, core),
                send_sem=d2d_send.at[a, k], recv_sem=d2d_recv.at[a, k],
                device_id=(x, y, 1 - c), device_id_type=MESH)

        for a in range(n):
            for k in range(3):
                over_ici(a, k, me).start()
        for a in range(n):
            for k, (px, py) in enumerate(chips):
                over_ici(a, k, 2 * px + py).wait_recv()
                over_d2d(a, k, c).start()
        for a in range(n):
            for k in range(3):
                over_d2d(a, k, 1 - c).wait_recv()
        for a in range(n):
            for k in range(3):
                over_ici(a, k, me).wait_send()
                over_d2d(a, k, c).wait_send()

    return pl.pallas_call(
        body, in_specs=[ANY] * n, out_specs=[ANY] * n,
        out_shape=[jax.ShapeDtypeStruct(b.shape, b.dtype) for b in bufs],
        input_output_aliases={a: a for a in range(n)},
        scratch_shapes=[pltpu.SemaphoreType.DMA((n, 3))] * 4,
        name="gather_weights")(*bufs)


def _gather_in_proj(h, bufs, order):
    n = len(bufs)
    half = [b.shape[1] // 2 for b in bufs]
    lp, d = h.shape
    ncol = bufs[0].shape[2]
    tm = _mm_row_tile(lp)
    n_row = lp // tm

    def body(order_ref, h_ref, *refs):
        gathered = refs[n:2 * n]
        o_ref, wt_ref = refs[2 * n], refs[2 * n + 1]
        w_buf, ici_send, ici_recv, d2d_send, d2d_recv, w_sem = refs[2 * n + 2:]
        j = pl.program_id(0)
        i = pl.program_id(1)
        x, y, c = _mesh_pos()
        me = 2 * x + y
        chips = _other_chips(x, y)

        def part(a, block, core):
            return gathered[a].at[block, pl.ds(core * half[a], half[a])]

        def over_ici(a, k, block):
            px, py = chips[k]
            return pltpu.make_async_remote_copy(
                src_ref=part(a, block, c), dst_ref=part(a, block, c),
                send_sem=ici_send.at[a, k], recv_sem=ici_recv.at[a, k],
                device_id=(px, py, c), device_id_type=MESH)

        def over_d2d(a, k, core):
            px, py = chips[k]
            return pltpu.make_async_remote_copy(
                src_ref=part(a, 2 * px + py, core), dst_ref=part(a, 2 * px + py, core),
                send_sem=d2d_send.at[a, k], recv_sem=d2d_recv.at[a, k],
                device_id=(x, y, 1 - c), device_id_type=MESH)

        @pl.when(jnp.logical_and(j == 0, i == 0))
        def _():
            for a in range(n):
                for k in range(2):
                    over_ici(a, k, me).start()

        for k, (px, py) in enumerate(chips):
            @pl.when(jnp.logical_and(j == k + 1, i == 0))
            def _(k=k, px=px, py=py):
                for a in range(n):
                    over_ici(a, k, 2 * px + py).wait_recv()
                    over_d2d(a, k, c).start()
                if k == 0:
                    for a in range(n):
                        over_ici(a, 2, me).start()
                for a in range(n):
                    over_d2d(a, k, 1 - c).wait_recv()

        @pl.when(i == 0)
        def _():
            load = pltpu.make_async_copy(gathered[0].at[order_ref[j]], w_buf, w_sem)
            load.start()
            load.wait()
            wt_ref[0] = w_buf[...].T

        o_ref[...] = jnp.dot(h_ref[...], w_buf[...], preferred_element_type=F32)

        @pl.when(jnp.logical_and(j == N_CHIPS - 1, i == n_row - 1))
        def _():
            for a in range(n):
                for k in range(3):
                    over_ici(a, k, me).wait_send()
                    over_d2d(a, k, c).wait_send()

    grid_spec = pltpu.PrefetchScalarGridSpec(
        num_scalar_prefetch=1, grid=(N_CHIPS, n_row),
        in_specs=[pl.BlockSpec((tm, d), lambda j, i, order_ref: (i, 0))] + [ANY] * n,
        out_specs=[ANY] * n + [pl.BlockSpec((tm, ncol), lambda j, i, order_ref: (i, order_ref[j])),
                               pl.BlockSpec((1, ncol, d), lambda j, i, order_ref: (order_ref[j], 0, 0))],
        scratch_shapes=[pltpu.VMEM((d, ncol), BF16)] + [pltpu.SemaphoreType.DMA((n, 3))] * 4
        + [pltpu.SemaphoreType.DMA])
    out = pl.pallas_call(
        body, grid_spec=grid_spec,
        out_shape=[jax.ShapeDtypeStruct(b.shape, b.dtype) for b in bufs]
        + [jax.ShapeDtypeStruct((lp, N_CHIPS * ncol), F32), jax.ShapeDtypeStruct((N_CHIPS, ncol, d), BF16)],
        input_output_aliases={2 + a: a for a in range(n)},
        name="gather_in_proj", compiler_params=_params())(order, h, *bufs)
    return out[n], out[n + 1], out[:n]


def _send_other_halves(grads, tag):
    n = len(grads)

    def body(*refs):
        srcs = refs[:n]
        dsts = refs[n:2 * n]
        send_sems, recv_sems = refs[2 * n:]
        x, y, c = _mesh_pos()
        copies = [pltpu.make_async_remote_copy(
            src_ref=srcs[a].at[j, 1 - c], dst_ref=dsts[a].at[j], send_sem=send_sems.at[a, j],
            recv_sem=recv_sems.at[a, j], device_id=(x, y, 1 - c), device_id_type=MESH)
            for a in range(n) for j in range(N_CHIPS)]
        for cp in copies:
            cp.start()
        for cp in copies:
            cp.wait()

    return pl.pallas_call(
        body, in_specs=[ANY] * n, out_specs=[ANY] * n,
        out_shape=[jax.ShapeDtypeStruct((N_CHIPS,) + g.shape[2:], g.dtype) for g in grads],
        scratch_shapes=[pltpu.SemaphoreType.DMA((n, N_CHIPS))] * 2,
        name="send_other_halves_" + tag)(*grads)


HBM = pl.BlockSpec(memory_space=pltpu.HBM)
SEM = pl.BlockSpec(memory_space=pltpu.SEMAPHORE)


def _block_copies(n, srcs, dsts, send_sems, recv_sems):
    x, y, c = _mesh_pos()
    return [pltpu.make_async_remote_copy(
        src_ref=srcs[a].at[2 * px + py], dst_ref=dsts[a].at[k], send_sem=send_sems.at[3 * a + k],
        recv_sem=recv_sems.at[3 * a + k], device_id=(px, py, c), device_id_type=MESH)
        for a in range(n) for k, (px, py) in enumerate(_other_chips(x, y))]


def _exchange_start(blocked, tag):
    n = len(blocked)
    lands = [lax.empty((3,) + b.shape[1:], b.dtype) for b in blocked]
    bufs = [pltpu.with_memory_space_constraint(b, pltpu.HBM) for b in list(blocked) + lands]
    nb = 2 * n

    def body(*refs):
        for cp in _block_copies(n, refs[:n], refs[n:nb], refs[nb], refs[nb + 1]):
            cp.start()
        refs[-1][...] = jnp.zeros_like(refs[-1])

    out = pl.pallas_call(
        body, name="exchange_start_" + tag,
        in_specs=[HBM] * nb,
        out_shape=[pltpu.SemaphoreType.DMA((3 * n,)), pltpu.SemaphoreType.DMA((3 * n,))]
        + [pltpu.HBM(b.shape, b.dtype) for b in bufs] + [jax.ShapeDtypeStruct((8, 128), F32)],
        out_specs=[SEM] * 2 + [HBM] * nb + [pl.BlockSpec(memory_space=pltpu.VMEM)],
        input_output_aliases={i: 2 + i for i in range(nb)},
        compiler_params=pltpu.CompilerParams(has_side_effects=pltpu.SideEffectType.DATAFLOW_SIDE_EFFECTING),
    )(*bufs)
    return (out[:2], out[2:2 + nb]), out[-1]


def _exchange_wait(state, after, tag):
    sems, bufs = state
    nb = len(bufs)
    n = nb // 2

    def body(*refs):
        for cp in _block_copies(n, refs[:n], refs[n:nb], refs[nb], refs[nb + 1]):
            cp.wait_send()
            cp.wait_recv()

    out = pl.pallas_call(
        body, name="exchange_wait_" + tag,
        in_specs=[HBM] * nb + [SEM] * 2 + [ANY],
        out_shape=[pltpu.HBM(b.shape, b.dtype) for b in bufs],
        out_specs=[HBM] * nb,
        input_output_aliases={i: i for i in range(nb)},
        compiler_params=pltpu.CompilerParams(has_side_effects=pltpu.SideEffectType.DATAFLOW_SIDE_EFFECTING),
    )(*bufs, *sems, after)
    return out[n:nb]


def _whole_block_copies(buf, send_sems, recv_sems, incoming):
    x, y, c = _mesh_pos()
    me = 2 * x + y
    out = []
    for k, (px, py) in enumerate(_other_chips(x, y)):
        block = 2 * px + py if incoming else me
        out.append(pltpu.make_async_remote_copy(
            src_ref=buf.at[block], dst_ref=buf.at[block], send_sem=send_sems.at[k], recv_sem=recv_sems.at[k],
            device_id=(px, py, c), device_id_type=MESH))
    return out


def _gather_start(buf, after, tag):
    buf = pltpu.with_memory_space_constraint(buf, pltpu.HBM)

    def body(buf_ref, after_ref, send_sems, recv_sems, thru_ref, token):
        for cp in _whole_block_copies(buf_ref, send_sems, recv_sems, incoming=False):
            cp.start()
        token[...] = jnp.zeros_like(token)

    out = pl.pallas_call(
        body, name="gather_start_" + tag,
        in_specs=[HBM, ANY],
        out_shape=[pltpu.SemaphoreType.DMA((3,)), pltpu.SemaphoreType.DMA((3,)), pltpu.HBM(buf.shape, buf.dtype),
                   jax.ShapeDtypeStruct((8, 128), F32)],
        out_specs=[SEM, SEM, HBM, pl.BlockSpec(memory_space=pltpu.VMEM)],
        input_output_aliases={0: 2},
        compiler_params=pltpu.CompilerParams(has_side_effects=pltpu.SideEffectType.DATAFLOW_SIDE_EFFECTING),
    )(buf, after)
    return out[:3], out[3]


def _gather_wait(state, after, tag):
    send_sems, recv_sems, buf = state

    def body(buf_ref, send_ref, recv_ref, after_ref, out_ref):
        for cp in _whole_block_copies(buf_ref, send_ref, recv_ref, incoming=True):
            cp.wait_send()
            cp.wait_recv()

    return pl.pallas_call(
        body, name="gather_wait_" + tag,
        in_specs=[HBM, SEM, SEM, ANY],
        out_shape=pltpu.HBM(buf.shape, buf.dtype), out_specs=HBM,
        input_output_aliases={0: 0},
        compiler_params=pltpu.CompilerParams(has_side_effects=pltpu.SideEffectType.DATAFLOW_SIDE_EFFECTING),
    )(buf, send_sems, recv_sems, after)


def _small_copies(small_ref, slots_ref, send_sems, recv_sems, incoming):
    x, y, c = _mesh_pos()
    out = []
    for r in range(1, 8):
        px = 1 - x if r & 4 else x
        py = 1 - y if r & 2 else y
        pc = 1 - c if r & 1 else c
        slot = 4 * px + 2 * py + pc if incoming else 4 * x + 2 * y + c
        out.append(pltpu.make_async_remote_copy(
            src_ref=small_ref, dst_ref=slots_ref.at[slot], send_sem=send_sems.at[r - 1],
            recv_sem=recv_sems.at[r - 1], device_id=(px, py, pc), device_id_type=MESH))
    return out


def _small_start(small):
    bufs = [pltpu.with_memory_space_constraint(b, pltpu.HBM)
            for b in (small, lax.empty((8,) + small.shape, small.dtype))]

    def body(small_ref, slots_ref, send_sems, recv_sems, small_thru, slots_thru, token):
        for cp in _small_copies(small_ref, slots_ref, send_sems, recv_sems, incoming=False):
            cp.start()
        token[...] = jnp.zeros_like(token)

    out = pl.pallas_call(
        body, name="small_start",
        in_specs=[HBM, HBM],
        out_shape=[pltpu.SemaphoreType.DMA((7,)), pltpu.SemaphoreType.DMA((7,))]
        + [pltpu.HBM(b.shape, b.dtype) for b in bufs] + [jax.ShapeDtypeStruct((8, 128), F32)],
        out_specs=[SEM, SEM, HBM, HBM, pl.BlockSpec(memory_space=pltpu.VMEM)],
        input_output_aliases={0: 2, 1: 3},
        compiler_params=pltpu.CompilerParams(has_side_effects=pltpu.SideEffectType.DATAFLOW_SIDE_EFFECTING),
    )(*bufs)
    return out[:4], out[4]


def _small_wait(state, after):
    send_sems, recv_sems, small, slots = state

    def body(small_ref, slots_ref, send_ref, recv_ref, after_ref, small_out, slots_out):
        for cp in _small_copies(small_ref, slots_ref, send_ref, recv_ref, incoming=True):
            cp.wait_send()
            cp.wait_recv()

    return pl.pallas_call(
        body, name="small_wait",
        in_specs=[HBM, HBM, SEM, SEM, ANY],
        out_shape=[pltpu.HBM(small.shape, small.dtype), pltpu.HBM(slots.shape, slots.dtype)],
        out_specs=[HBM, HBM], input_output_aliases={0: 0, 1: 1},
        compiler_params=pltpu.CompilerParams(has_side_effects=pltpu.SideEffectType.DATAFLOW_SIDE_EFFECTING),
    )(small, slots, send_sems, recv_sems, after)[1]


def _join_halves(bufs):
    n = len(bufs)

    def body(*refs):
        joined = refs[n:2 * n]
        send_sems, recv_sems = refs[2 * n:]
        x, y, c = _mesh_pos()
        for a in range(n):
            pltpu.make_async_remote_copy(
                src_ref=joined[a].at[c], dst_ref=joined[a].at[c], send_sem=send_sems.at[a],
                recv_sem=recv_sems.at[a], device_id=(x, y, 1 - c), device_id_type=MESH).start()
        for a in range(n):
            pltpu.make_async_remote_copy(
                src_ref=joined[a].at[c], dst_ref=joined[a].at[1 - c], send_sem=send_sems.at[a],
                recv_sem=recv_sems.at[a], device_id=(x, y, 1 - c), device_id_type=MESH).wait()

    return pl.pallas_call(
        body, in_specs=[ANY] * n, out_specs=[ANY] * n,
        out_shape=[jax.ShapeDtypeStruct(b.shape, b.dtype) for b in bufs],
        input_output_aliases={a: a for a in range(n)},
        scratch_shapes=[pltpu.SemaphoreType.DMA((n,))] * 2,
        name="join_halves")(*bufs)


def kernel(x, meta_tokens, norm_g, w_in, conv_w, conv_b, ln_g, ln_b, w_conv_out, lb_logits, gnorm_g, w_rec_out, w_out, final_g, loss_target, m_meta_tokens, m_norm_g, m_w_in, m_conv_w, m_conv_b, m_ln_g, m_ln_b, m_w_conv_out, m_lb_logits, m_gnorm_g, m_w_rec_out, m_w_out, m_final_g, v_meta_tokens, v_norm_g, v_w_in, v_conv_w, v_conv_b, v_ln_g, v_ln_b, v_w_conv_out, v_lb_logits, v_gnorm_g, v_w_rec_out, v_w_out, v_final_g):
    d = x.shape[2]
    n_meta = meta_tokens.shape[0]
    n_pad = CHUNK - n_meta
    ds = d // N_CHIPS
    chip = 2 * lax.axis_index("x") + lax.axis_index("y")

    conv_w_pad = jnp.pad(conv_w[0], ((0, HALO - CONV_WIDTH), (0, 0)))
    chip_idx = chip.astype(jnp.int32).reshape(1)
    (small_g,) = _gather_weights([
        _place_shard("place_small", jnp.concatenate([conv_w_pad, meta_tokens], axis=0), chip_idx, F32)])
    cw_full = jnp.transpose(small_g[:, 0:HALO], (1, 0, 2)).reshape(HALO, d)
    meta_full = jnp.transpose(small_g[:, HALO:HALO + n_meta], (1, 0, 2)).reshape(n_meta, d)

    hres = jnp.concatenate([jnp.zeros((n_pad, d), F32), meta_full, x[0]], axis=0)
    target = loss_target[0]
    final_g2 = final_g.reshape(1, d)
    h = _rmsnorm_fwd(hres, norm_g)
    fx, fy = 1 - lax.axis_index("x"), 1 - lax.axis_index("y")
    order = jnp.stack([chip, 2 * fx + (1 - fy), 2 * (1 - fx) + fy, 2 * fx + fy]).astype(jnp.int32)
    proj, win_t, _ = _gather_in_proj(h, [_place_shard("place_w_in", w_in[0], chip_idx, BF16)], order)
    sq_own = _place_shard("place_square", jnp.concatenate([w_conv_out[0], w_rec_out[0], w_out[0]], axis=0),
                          chip_idx, BF16)
    sq_flight, sq_token = _gather_start(sq_own, proj, "square")
    o, s_all = _hgrn_fwd(proj, lb_logits + sq_token[0:1, 0:1], n_pad)
    sq_g = _gather_wait(sq_flight, s_all, "square")
    wc_full = sq_g[:, 0:ds].reshape(d, d)
    wr_full = sq_g[:, ds:2 * ds].reshape(d, d)
    wo_full = sq_g[:, 2 * ds:3 * ds].reshape(d, d)
    c, yc_in, y_conv = _conv_fwd(proj, cw_full, conv_b, ln_g, ln_b, wc_full)
    yr_in, g_wo, y_rec, dout, loss_acc, dfinal_g = _tail_fwd(
        o, proj, y_conv, hres, target, gnorm_g, final_g2, wr_full, wo_full)

    (dyc, dyr, dout_bf, dz, dproj, do, dc, dgnorm_g, dln_g, dln_b) = _tail_bwd(
        dout, proj, y_conv, y_rec, o, c, wo_full, wr_full, wc_full, ln_g, ln_b, gnorm_g)
    g_wr = _weight_grad(yr_in, dyr, "grad_w_rec_out", False)
    dproj, dlb_logits = _hgrn_bwd(proj, do, s_all, lb_logits, n_pad, dproj)
    dproj, dconv_w, dconv_b, g_wc = _conv_bwd(dc, proj, cw_full, dz, dproj, yc_in, dyc)

    core = lax.axis_index("c").astype(jnp.int32).reshape(1)

    def chip_sum_and_start(g, tag, g_to_sibling=None):
        halves = lambda a: a.reshape(N_CHIPS, 2, a.shape[1] // 2, a.shape[2])
        g = halves(g)
        (from_sibling,) = _send_other_halves([g if g_to_sibling is None else halves(g_to_sibling)], tag)
        sums = _chip_half_sum("chip_half_sum_" + tag, g, from_sibling, core)
        in_flight, token = _exchange_start([sums[1]], tag)
        return sums[0], in_flight, token[0:1, 0:1]

    g_sq = jnp.concatenate([g.reshape(N_CHIPS, ds, d) for g in (g_wc, g_wr, g_wo)], axis=1)
    sum_sq, flight_sq, _ = chip_sum_and_start(g_sq, "square")
    g_win, g_win_bf = _weight_grad(h, dproj, "grad_w_in", True)
    (recv_sq,) = _exchange_wait(flight_sq, g_win_bf, "square")
    sum_win, flight_win, token_win = chip_sum_and_start(g_win, "w_in", g_win_bf)
    dhres, dnorm_g = _in_proj_bwd(dproj, win_t, hres, norm_g + token_win, dout)
    grad_x = dhres[CHUNK:][None]
    (recv_win,) = _exchange_wait(flight_win, dnorm_g, "w_in")
    small = jnp.concatenate([dnorm_g, dconv_b, dln_g, dln_b, dlb_logits, dgnorm_g, dfinal_g,
                             dhres[n_pad:CHUNK], dconv_w[:CONV_WIDTH],
                             jnp.broadcast_to(loss_acc[0:1, 0:1], (1, d))], axis=0)
    small_flight, small_token = _small_start(small)
    chip_core = jnp.concatenate([chip_idx, core])
    totals = [_block_half_total("block_half_total_" + nm, s, r, chip_core, small_token)
              for nm, s, r in zip(("w_in", "square"), (sum_win, sum_sq), (recv_win, recv_sq))]
    joined = _join_halves(totals)
    gt_win, gt_sq = [t.reshape(2 * t.shape[1], t.shape[2]) for t in joined]
    small_slots = _small_wait(small_flight, joined[1])
    device_idx = (2 * chip_idx + core).astype(jnp.int32)
    small_sum = _sum_slots("sum_small", small_slots, small, device_idx)

    res = {}
    res["w_in"] = _adamw("adamw_w_in", w_in[0], m_w_in[0], v_w_in[0], gt_win)
    res["w_conv_out"] = _adamw("adamw_w_conv_out", w_conv_out[0], m_w_conv_out[0], v_w_conv_out[0], gt_sq[0:ds])
    res["w_rec_out"] = _adamw("adamw_w_rec_out", w_rec_out[0], m_w_rec_out[0], v_w_rec_out[0], gt_sq[ds:2 * ds])
    res["w_out"] = _adamw("adamw_w_out", w_out[0], m_w_out[0], v_w_out[0], gt_sq[2 * ds:3 * ds])
    big = {k: tuple(a[None] for a in v) for k, v in res.items()}

    rep_names = ("norm_g", "conv_b", "ln_g", "ln_b", "lb_logits", "gnorm_g", "final_g")
    rep_w = (norm_g, conv_b, ln_g, ln_b, lb_logits, gnorm_g, final_g2)
    rep_m = (m_norm_g, m_conv_b, m_ln_g, m_ln_b, m_lb_logits, m_gnorm_g, m_final_g.reshape(1, d))
    rep_v = (v_norm_g, v_conv_b, v_ln_g, v_ln_b, v_lb_logits, v_gnorm_g, v_final_g.reshape(1, d))
    rep = _adamw("adamw_replicated", jnp.concatenate(rep_w, 0), jnp.concatenate(rep_m, 0),
                 jnp.concatenate(rep_v, 0), small_sum[0:8])
    rep_rows = {"norm_g": (0, 1), "conv_b": (1, 2), "ln_g": (2, 3), "ln_b": (3, 4), "lb_logits": (4, 6),
                "gnorm_g": (6, 7), "final_g": (7, 8)}
    small_out = {}
    for nm in rep_names:
        lo, hi = rep_rows[nm]
        vals = tuple(a[lo:hi] for a in rep)
        if nm == "final_g":
            vals = tuple(a.reshape(d) for a in vals)
        small_out[nm] = vals
    cw_row = 8 + n_meta
    g_meta = lax.dynamic_slice_in_dim(small_sum[8:cw_row], chip * ds, ds, axis=1)
    small_out["meta_tokens"] = _adamw("adamw_meta", meta_tokens, m_meta_tokens, v_meta_tokens, g_meta)
    g_cw = lax.dynamic_slice_in_dim(small_sum[cw_row:cw_row + HALO], chip * ds, ds, axis=1)
    pad_rows = ((0, HALO - CONV_WIDTH), (0, 0))
    cw_res = _adamw("adamw_conv_w", conv_w_pad, jnp.pad(m_conv_w[0], pad_rows),
                    jnp.pad(v_conv_w[0], pad_rows, constant_values=1.0), g_cw)
    small_out["conv_w"] = tuple(a[:CONV_WIDTH][None] for a in cw_res)

    loss = small_sum[cw_row + HALO - 1, 0]

    order = ("meta_tokens", "norm_g", "w_in", "conv_w", "conv_b", "ln_g", "ln_b", "w_conv_out", "lb_logits",
             "gnorm_g", "w_rec_out", "w_out", "final_g")
    allres = {**big, **small_out}
    outs = [loss, grad_x]
    for field in range(4):
        outs.extend(allres[nm][field] for nm in order)
    return tuple(outs)
```
